```python
import jax, jax.numpy as jnp
from jax import lax
import numpy as np

D_MODEL = 2048
BATCH = 8
SEQ = 4096
DEPTH = 4

CHUNK = 64
N_MIXERS = 2
N_A = (DEPTH + N_MIXERS - 1) // N_MIXERS
N_B = DEPTH // N_MIXERS
NORM_EPS = 1e-6

SGU_BLOCK = 128
SGU_WIDTH = 2 * D_MODEL
SGU_GROUPS = 16
SGU_GROUP_DIM = SGU_WIDTH // SGU_GROUPS

MLA_HEADS = 16
Q_LORA_RANK = 448
KV_LORA_RANK = 512
QK_NOPE_DIM = 128
QK_ROPE_DIM = 64
V_HEAD_DIM = 128
MLA_WIDTH = MLA_HEADS * V_HEAD_DIM
ROPE_THETA = 10000.0
Q_BLOCK = 128

kernel_name = "hybrid_sgu_mla_adaln_sandwich"


def rms_norm(x, g):
    xf = x.astype(jnp.float32)
    y = xf * lax.rsqrt(jnp.mean(xf * xf, axis=-1, keepdims=True) + NORM_EPS)
    return (y * g.astype(jnp.float32)).astype(x.dtype)


def layer_norm(x, g):
    xf = x.astype(jnp.float32)
    mu = jnp.mean(xf, axis=-1, keepdims=True)
    var = jnp.mean(jnp.square(xf - mu), axis=-1, keepdims=True)
    return ((xf - mu) * lax.rsqrt(var + NORM_EPS) * g.astype(jnp.float32)).astype(x.dtype)


def apply_rope(x, cos, sin):
    xf = x.astype(jnp.float32)
    x1, x2 = jnp.split(xf, 2, axis=-1)
    return jnp.concatenate([x1 * cos - x2 * sin, x1 * sin + x2 * cos], axis=-1).astype(x.dtype)


def sgu_mixer(h, w_in, norm_g, w_s, b_s, w_out):
    B, S, _ = h.shape
    u, v, z = jnp.split(h @ w_in, 3, axis=-1)
    u = jax.nn.gelu(u, approximate=False)
    v = layer_norm(jax.nn.gelu(v, approximate=False), norm_g)
    t_chunk = jnp.arange(SGU_BLOCK) // CHUNK
    mask = (t_chunk[None, :] <= t_chunk[:, None]).astype(w_s.dtype)
    w = w_s * mask[None]
    vb = v.reshape(B, S // SGU_BLOCK, SGU_BLOCK, SGU_GROUPS, SGU_GROUP_DIM)
    vm = jnp.einsum('gts,bnsgc->bntgc', w, vb) + b_s.T[:, :, None]
    y = u * vm.reshape(B, S, SGU_WIDTH) * jax.nn.silu(z)
    return y @ w_out


def chunk_causal_attention(q_nope, q_rope, k_nope, k_rope, v):
    B, S, H, _ = q_nope.shape
    nqb = S // Q_BLOCK
    scale = (QK_NOPE_DIM + QK_ROPE_DIM) ** -0.5
    k_chunk = jnp.arange(S) // CHUNK

    def to_blocks(t):
        return jnp.moveaxis(t.reshape(B, nqb, Q_BLOCK, *t.shape[2:]), 1, 0)

    def one_block(args):
        idx, qn, qr = args
        s = (jnp.einsum('bqhd,bkhd->bhqk', qn, k_nope)
             + jnp.einsum('bqhd,bkd->bhqk', qr, k_rope)).astype(jnp.float32) * scale
        q_chunk = (idx * Q_BLOCK + jnp.arange(Q_BLOCK)) // CHUNK
        mask = k_chunk[None, :] <= q_chunk[:, None]
        p = jax.nn.softmax(jnp.where(mask, s, -1e30), axis=-1).astype(v.dtype)
        return jnp.einsum('bhqk,bkhd->bqhd', p, v)

    out = lax.map(one_block, (jnp.arange(nqb), to_blocks(q_nope), to_blocks(q_rope)))
    return jnp.moveaxis(out, 0, 1).reshape(B, S, H, V_HEAD_DIM)


def mla_mixer(h, w_in, q_norm_g, kv_norm_g, w_uq, w_ukv, w_out):
    B, S, _ = h.shape
    o1 = Q_LORA_RANK
    o2 = o1 + KV_LORA_RANK
    o3 = o2 + QK_ROPE_DIM
    cq, ckv, k_rope, z = jnp.split(h @ w_in, [o1, o2, o3], axis=-1)
    q = (rms_norm(cq, q_norm_g) @ w_uq).reshape(B, S, MLA_HEADS, QK_NOPE_DIM + QK_ROPE_DIM)
    kv = (rms_norm(ckv, kv_norm_g) @ w_ukv).reshape(B, S, MLA_HEADS, QK_NOPE_DIM + V_HEAD_DIM)
    q_nope, q_rope = q[..., :QK_NOPE_DIM], q[..., QK_NOPE_DIM:]
    k_nope, v = kv[..., :QK_NOPE_DIM], kv[..., QK_NOPE_DIM:]
    pos = jnp.arange(S, dtype=jnp.float32)
    inv_freq = ROPE_THETA ** (-jnp.arange(0, QK_ROPE_DIM, 2, dtype=jnp.float32) / QK_ROPE_DIM)
    ang = pos[:, None] * inv_freq[None, :]
    cos, sin = jnp.cos(ang), jnp.sin(ang)
    q_rope = apply_rope(q_rope, cos[:, None, :], sin[:, None, :])
    k_rope = apply_rope(k_rope, cos, sin)
    o = chunk_causal_attention(q_nope, q_rope, k_nope, k_rope, v)
    y = o.reshape(B, S, MLA_WIDTH) * jax.nn.silu(z)
    return y @ w_out


def _fwd_setup_inputs(seed: int = 0) -> dict:
    key = jax.random.key(seed)
    ks = jax.random.split(key, 17)
    f32 = jnp.float32
    D = D_MODEL

    def nrm(k, shape, s):
        return jax.random.normal(k, shape, f32) * s

    def gain(k, shape):
        return 1.0 + 0.05 * jax.random.normal(k, shape, f32)

    mla_in_cols = Q_LORA_RANK + KV_LORA_RANK + QK_ROPE_DIM + MLA_WIDTH
    return {
        'x': nrm(ks[0], (BATCH, SEQ, D), 1.0),
        'c': nrm(ks[1], (BATCH, D), 1.0),
        'ada_w': nrm(ks[2], (DEPTH, D, 3 * D), 0.5 * D ** -0.5),
        'ada_b': nrm(ks[3], (DEPTH, 3 * D), 0.01),
        'pre_g': gain(ks[4], (DEPTH, D)),
        'post_g': gain(ks[5], (DEPTH, D)),
        'sgu_w_in': nrm(ks[6], (N_A, D, 3 * SGU_WIDTH), D ** -0.5),
        'sgu_norm_g': gain(ks[7], (N_A, SGU_WIDTH)),
        'sgu_w_s': nrm(ks[8], (N_A, SGU_GROUPS, SGU_BLOCK, SGU_BLOCK), SGU_BLOCK ** -0.5),
        'sgu_b_s': gain(ks[9], (N_A, SGU_GROUPS, SGU_BLOCK)),
        'sgu_w_out': nrm(ks[10], (N_A, SGU_WIDTH, D), SGU_WIDTH ** -0.5),
        'mla_w_in': nrm(ks[11], (N_B, D, mla_in_cols), D ** -0.5),
        'mla_q_norm_g': gain(ks[12], (N_B, Q_LORA_RANK)),
        'mla_kv_norm_g': gain(ks[13], (N_B, KV_LORA_RANK)),
        'mla_w_uq': nrm(ks[14], (N_B, Q_LORA_RANK, MLA_HEADS * (QK_NOPE_DIM + QK_ROPE_DIM)), Q_LORA_RANK ** -0.5),
        'mla_w_ukv': nrm(ks[15], (N_B, KV_LORA_RANK, MLA_HEADS * (QK_NOPE_DIM + V_HEAD_DIM)), KV_LORA_RANK ** -0.5),
        'mla_w_out': nrm(ks[16], (N_B, MLA_WIDTH, D), MLA_WIDTH ** -0.5),
    }


def _fwd_reference(x, c, ada_w, ada_b, pre_g, post_g, sgu_w_in, sgu_norm_g, sgu_w_s, sgu_b_s, sgu_w_out,
              mla_w_in, mla_q_norm_g, mla_kv_norm_g, mla_w_uq, mla_w_ukv, mla_w_out):
    cond = jax.nn.silu(c)
    for i in range(DEPTH):
        mod = cond @ ada_w[i] + ada_b[i]
        shift, scale, gate = jnp.split(mod, 3, axis=-1)
        h = rms_norm(x, pre_g[i]) * (1 + scale[:, None, :]) + shift[:, None, :]
        j = i // N_MIXERS
        if i % N_MIXERS == 0:
            y = sgu_mixer(h, sgu_w_in[j], sgu_norm_g[j], sgu_w_s[j], sgu_b_s[j], sgu_w_out[j])
        else:
            y = mla_mixer(h, mla_w_in[j], mla_q_norm_g[j], mla_kv_norm_g[j], mla_w_uq[j], mla_w_ukv[j], mla_w_out[j])
        x = x + gate[:, None, :] * rms_norm(y, post_g[i])
    return x


import jax as _jax
import jax.numpy as _jnp

TWIN_FORMAT = 'train_step'
FWD_PARAMS = ['x', 'c', 'ada_w', 'ada_b', 'pre_g', 'post_g', 'sgu_w_in', 'sgu_norm_g', 'sgu_w_s', 'sgu_b_s', 'sgu_w_out', 'mla_w_in', 'mla_q_norm_g', 'mla_kv_norm_g', 'mla_w_uq', 'mla_w_ukv', 'mla_w_out']
TWIN_WEIGHTS = ['ada_w', 'ada_b', 'pre_g', 'post_g', 'sgu_w_in', 'sgu_norm_g', 'sgu_w_s', 'sgu_b_s', 'sgu_w_out', 'mla_w_in', 'mla_q_norm_g', 'mla_kv_norm_g', 'mla_w_uq', 'mla_w_ukv', 'mla_w_out']
TWIN_DIFF_INPUT = 'x'
TWIN_INPUTS = ['x', 'c', 'ada_w', 'ada_b', 'pre_g', 'post_g', 'sgu_w_in', 'sgu_norm_g', 'sgu_w_s', 'sgu_b_s', 'sgu_w_out', 'mla_w_in', 'mla_q_norm_g', 'mla_kv_norm_g', 'mla_w_uq', 'mla_w_ukv', 'mla_w_out', 'loss_target', 'm_ada_w', 'm_ada_b', 'm_pre_g', 'm_post_g', 'm_sgu_w_in', 'm_sgu_norm_g', 'm_sgu_w_s', 'm_sgu_b_s', 'm_sgu_w_out', 'm_mla_w_in', 'm_mla_q_norm_g', 'm_mla_kv_norm_g', 'm_mla_w_uq', 'm_mla_w_ukv', 'm_mla_w_out', 'v_ada_w', 'v_ada_b', 'v_pre_g', 'v_post_g', 'v_sgu_w_in', 'v_sgu_norm_g', 'v_sgu_w_s', 'v_sgu_b_s', 'v_sgu_w_out', 'v_mla_w_in', 'v_mla_q_norm_g', 'v_mla_kv_norm_g', 'v_mla_w_uq', 'v_mla_w_ukv', 'v_mla_w_out']
TWIN_OUTPUTS = ['loss', 'grad_x', 'grad_ada_w', 'grad_ada_b', 'grad_pre_g', 'grad_post_g', 'grad_sgu_w_in', 'grad_sgu_norm_g', 'grad_sgu_w_s', 'grad_sgu_b_s', 'grad_sgu_w_out', 'grad_mla_w_in', 'grad_mla_q_norm_g', 'grad_mla_kv_norm_g', 'grad_mla_w_uq', 'grad_mla_w_ukv', 'grad_mla_w_out', 'delta_ada_w', 'delta_ada_b', 'delta_pre_g', 'delta_post_g', 'delta_sgu_w_in', 'delta_sgu_norm_g', 'delta_sgu_w_s', 'delta_sgu_b_s', 'delta_sgu_w_out', 'delta_mla_w_in', 'delta_mla_q_norm_g', 'delta_mla_kv_norm_g', 'delta_mla_w_uq', 'delta_mla_w_ukv', 'delta_mla_w_out', 'new_m_ada_w', 'new_m_ada_b', 'new_m_pre_g', 'new_m_post_g', 'new_m_sgu_w_in', 'new_m_sgu_norm_g', 'new_m_sgu_w_s', 'new_m_sgu_b_s', 'new_m_sgu_w_out', 'new_m_mla_w_in', 'new_m_mla_q_norm_g', 'new_m_mla_kv_norm_g', 'new_m_mla_w_uq', 'new_m_mla_w_ukv', 'new_m_mla_w_out', 'new_v_ada_w', 'new_v_ada_b', 'new_v_pre_g', 'new_v_post_g', 'new_v_sgu_w_in', 'new_v_sgu_norm_g', 'new_v_sgu_w_s', 'new_v_sgu_b_s', 'new_v_sgu_w_out', 'new_v_mla_w_in', 'new_v_mla_q_norm_g', 'new_v_mla_kv_norm_g', 'new_v_mla_w_uq', 'new_v_mla_w_ukv', 'new_v_mla_w_out']
TWIN_LEAF_KINDS = {'loss': 'loss', 'grad_x': 'grad_x', 'grad_ada_w': 'grad_w', 'grad_ada_b': 'grad_w', 'grad_pre_g': 'grad_w', 'grad_post_g': 'grad_w', 'grad_sgu_w_in': 'grad_w', 'grad_sgu_norm_g': 'grad_w', 'grad_sgu_w_s': 'grad_w', 'grad_sgu_b_s': 'grad_w', 'grad_sgu_w_out': 'grad_w', 'grad_mla_w_in': 'grad_w', 'grad_mla_q_norm_g': 'grad_w', 'grad_mla_kv_norm_g': 'grad_w', 'grad_mla_w_uq': 'grad_w', 'grad_mla_w_ukv': 'grad_w', 'grad_mla_w_out': 'grad_w', 'delta_ada_w': 'delta_w', 'delta_ada_b': 'delta_w', 'delta_pre_g': 'delta_w', 'delta_post_g': 'delta_w', 'delta_sgu_w_in': 'delta_w', 'delta_sgu_norm_g': 'delta_w', 'delta_sgu_w_s': 'delta_w', 'delta_sgu_b_s': 'delta_w', 'delta_sgu_w_out': 'delta_w', 'delta_mla_w_in': 'delta_w', 'delta_mla_q_norm_g': 'delta_w', 'delta_mla_kv_norm_g': 'delta_w', 'delta_mla_w_uq': 'delta_w', 'delta_mla_w_ukv': 'delta_w', 'delta_mla_w_out': 'delta_w', 'new_m_ada_w': 'new_m', 'new_m_ada_b': 'new_m', 'new_m_pre_g': 'new_m', 'new_m_post_g': 'new_m', 'new_m_sgu_w_in': 'new_m', 'new_m_sgu_norm_g': 'new_m', 'new_m_sgu_w_s': 'new_m', 'new_m_sgu_b_s': 'new_m', 'new_m_sgu_w_out': 'new_m', 'new_m_mla_w_in': 'new_m', 'new_m_mla_q_norm_g': 'new_m', 'new_m_mla_kv_norm_g': 'new_m', 'new_m_mla_w_uq': 'new_m', 'new_m_mla_w_ukv': 'new_m', 'new_m_mla_w_out': 'new_m', 'new_v_ada_w': 'new_v', 'new_v_ada_b': 'new_v', 'new_v_pre_g': 'new_v', 'new_v_post_g': 'new_v', 'new_v_sgu_w_in': 'new_v', 'new_v_sgu_norm_g': 'new_v', 'new_v_sgu_w_s': 'new_v', 'new_v_sgu_b_s': 'new_v', 'new_v_sgu_w_out': 'new_v', 'new_v_mla_w_in': 'new_v', 'new_v_mla_q_norm_g': 'new_v', 'new_v_mla_kv_norm_g': 'new_v', 'new_v_mla_w_uq': 'new_v', 'new_v_mla_w_ukv': 'new_v', 'new_v_mla_w_out': 'new_v'}


def _forward(args):
    return _fwd_reference(*[args[k] for k in FWD_PARAMS])


def _output_shape():
    out = _jax.eval_shape(lambda: _forward(_fwd_setup_inputs(0)))
    return out.shape, out.dtype

N_MICROBATCH = 1
ADAM_LR = 0.001
ADAM_B1 = 0.9
ADAM_B2 = 0.999
ADAM_EPS = 1e-08
ADAM_WD = 0.01
ADAM_STEP = 10
PER_EXAMPLE_BATCH_AXIS = {'x': 0, 'c': 0, 'loss_target': 0}
SHARED_INPUTS = []
_WEIGHT_DTYPES = {'ada_w': _jnp.float32, 'ada_b': _jnp.float32, 'pre_g': _jnp.float32, 'post_g': _jnp.float32, 'sgu_w_in': _jnp.float32, 'sgu_norm_g': _jnp.float32, 'sgu_w_s': _jnp.float32, 'sgu_b_s': _jnp.float32, 'sgu_w_out': _jnp.float32, 'mla_w_in': _jnp.float32, 'mla_q_norm_g': _jnp.float32, 'mla_kv_norm_g': _jnp.float32, 'mla_w_uq': _jnp.float32, 'mla_w_ukv': _jnp.float32, 'mla_w_out': _jnp.float32}
MOMENT_SCALE = {'ada_w': 8.643923e-01, 'ada_b': 1.629668e+00, 'pre_g': 7.504856e-02, 'post_g': 1.781647e+00, 'sgu_w_in': 3.521779e-02, 'sgu_norm_g': 2.166185e-02, 'sgu_w_s': 3.053168e-02, 'sgu_b_s': 3.555488e-02, 'sgu_w_out': 7.579116e-02, 'mla_w_in': 1.794114e-01, 'mla_q_norm_g': 4.256058e-02, 'mla_kv_norm_g': 4.460877e-01, 'mla_w_uq': 1.583679e-02, 'mla_w_ukv': 1.458047e-01, 'mla_w_out': 2.118251e-01}


def _to_microbatches(a, axis):
    t = _jnp.moveaxis(a, axis, 0)
    t = t.reshape((N_MICROBATCH, t.shape[0] // N_MICROBATCH) + t.shape[1:])
    return _jnp.moveaxis(t, 1, axis + 1)


def setup_inputs(seed: int = 0) -> dict:
    inp = _fwd_setup_inputs(seed)
    key = _jax.random.fold_in(_jax.random.key(seed), 7919)
    shape, _ = _output_shape()
    out = dict(inp)
    out["loss_target"] = _jax.random.normal(_jax.random.fold_in(key, 0), shape, _jnp.float32)
    for i, name in enumerate(TWIN_WEIGHTS):
        w = inp[name].astype(_jnp.float32)
        if MOMENT_SCALE is None:
            s = _jnp.sqrt(_jnp.mean(_jnp.square(w)) + 1e-30)
        else:
            s = MOMENT_SCALE[name]
        km, kv = _jax.random.split(_jax.random.fold_in(key, i + 1))
        out[name] = w
        out["m_" + name] = s * _jax.random.normal(km, w.shape, _jnp.float32)
        out["v_" + name] = (s * s) * _jax.random.uniform(kv, w.shape, _jnp.float32, 0.5, 1.5)
    if N_MICROBATCH > 1:
        for name, axis in PER_EXAMPLE_BATCH_AXIS.items():
            out[name] = _to_microbatches(out[name], axis)
    return {'x': out['x'], 'c': out['c'], 'ada_w': out['ada_w'], 'ada_b': out['ada_b'], 'pre_g': out['pre_g'], 'post_g': out['post_g'], 'sgu_w_in': out['sgu_w_in'], 'sgu_norm_g': out['sgu_norm_g'], 'sgu_w_s': out['sgu_w_s'], 'sgu_b_s': out['sgu_b_s'], 'sgu_w_out': out['sgu_w_out'], 'mla_w_in': out['mla_w_in'], 'mla_q_norm_g': out['mla_q_norm_g'], 'mla_kv_norm_g': out['mla_kv_norm_g'], 'mla_w_uq': out['mla_w_uq'], 'mla_w_ukv': out['mla_w_ukv'], 'mla_w_out': out['mla_w_out'], 'loss_target': out['loss_target'], 'm_ada_w': out['m_ada_w'], 'm_ada_b': out['m_ada_b'], 'm_pre_g': out['m_pre_g'], 'm_post_g': out['m_post_g'], 'm_sgu_w_in': out['m_sgu_w_in'], 'm_sgu_norm_g': out['m_sgu_norm_g'], 'm_sgu_w_s': out['m_sgu_w_s'], 'm_sgu_b_s': out['m_sgu_b_s'], 'm_sgu_w_out': out['m_sgu_w_out'], 'm_mla_w_in': out['m_mla_w_in'], 'm_mla_q_norm_g': out['m_mla_q_norm_g'], 'm_mla_kv_norm_g': out['m_mla_kv_norm_g'], 'm_mla_w_uq': out['m_mla_w_uq'], 'm_mla_w_ukv': out['m_mla_w_ukv'], 'm_mla_w_out': out['m_mla_w_out'], 'v_ada_w': out['v_ada_w'], 'v_ada_b': out['v_ada_b'], 'v_pre_g': out['v_pre_g'], 'v_post_g': out['v_post_g'], 'v_sgu_w_in': out['v_sgu_w_in'], 'v_sgu_norm_g': out['v_sgu_norm_g'], 'v_sgu_w_s': out['v_sgu_w_s'], 'v_sgu_b_s': out['v_sgu_b_s'], 'v_sgu_w_out': out['v_sgu_w_out'], 'v_mla_w_in': out['v_mla_w_in'], 'v_mla_q_norm_g': out['v_mla_q_norm_g'], 'v_mla_kv_norm_g': out['v_mla_kv_norm_g'], 'v_mla_w_uq': out['v_mla_w_uq'], 'v_mla_w_ukv': out['v_mla_w_ukv'], 'v_mla_w_out': out['v_mla_w_out']}


def _loss(weights, diff, rest, loss_target):
    with _jax.named_scope("forward"):
        args = {**rest, TWIN_DIFF_INPUT: diff, **{k: w.astype(_WEIGHT_DTYPES[k]) for k, w in weights.items()}}
        y = _forward(args)
    with _jax.named_scope("loss_head"):
        err = _jnp.square(y.astype(_jnp.float32) - loss_target)
        return 0.5 * _jnp.sum(_jnp.mean(err, axis=-1)) if err.ndim else 0.5 * err


def _adamw(w, g, m, v):
    m = ADAM_B1 * m + (1.0 - ADAM_B1) * g
    v = ADAM_B2 * v + (1.0 - ADAM_B2) * _jnp.square(g)
    m_hat = m / (1.0 - ADAM_B1 ** ADAM_STEP)
    v_hat = v / (1.0 - ADAM_B2 ** ADAM_STEP)
    delta = -ADAM_LR * (m_hat / (_jnp.sqrt(v_hat) + ADAM_EPS) + ADAM_WD * w)
    return delta, m, v


def reference(x, c, ada_w, ada_b, pre_g, post_g, sgu_w_in, sgu_norm_g, sgu_w_s, sgu_b_s, sgu_w_out, mla_w_in, mla_q_norm_g, mla_kv_norm_g, mla_w_uq, mla_w_ukv, mla_w_out, loss_target, m_ada_w, m_ada_b, m_pre_g, m_post_g, m_sgu_w_in, m_sgu_norm_g, m_sgu_w_s, m_sgu_b_s, m_sgu_w_out, m_mla_w_in, m_mla_q_norm_g, m_mla_kv_norm_g, m_mla_w_uq, m_mla_w_ukv, m_mla_w_out, v_ada_w, v_ada_b, v_pre_g, v_post_g, v_sgu_w_in, v_sgu_norm_g, v_sgu_w_s, v_sgu_b_s, v_sgu_w_out, v_mla_w_in, v_mla_q_norm_g, v_mla_kv_norm_g, v_mla_w_uq, v_mla_w_ukv, v_mla_w_out):
    given = dict(x=x, c=c, ada_w=ada_w, ada_b=ada_b, pre_g=pre_g, post_g=post_g, sgu_w_in=sgu_w_in, sgu_norm_g=sgu_norm_g, sgu_w_s=sgu_w_s, sgu_b_s=sgu_b_s, sgu_w_out=sgu_w_out, mla_w_in=mla_w_in, mla_q_norm_g=mla_q_norm_g, mla_kv_norm_g=mla_kv_norm_g, mla_w_uq=mla_w_uq, mla_w_ukv=mla_w_ukv, mla_w_out=mla_w_out, loss_target=loss_target, m_ada_w=m_ada_w, m_ada_b=m_ada_b, m_pre_g=m_pre_g, m_post_g=m_post_g, m_sgu_w_in=m_sgu_w_in, m_sgu_norm_g=m_sgu_norm_g, m_sgu_w_s=m_sgu_w_s, m_sgu_b_s=m_sgu_b_s, m_sgu_w_out=m_sgu_w_out, m_mla_w_in=m_mla_w_in, m_mla_q_norm_g=m_mla_q_norm_g, m_mla_kv_norm_g=m_mla_kv_norm_g, m_mla_w_uq=m_mla_w_uq, m_mla_w_ukv=m_mla_w_ukv, m_mla_w_out=m_mla_w_out, v_ada_w=v_ada_w, v_ada_b=v_ada_b, v_pre_g=v_pre_g, v_post_g=v_post_g, v_sgu_w_in=v_sgu_w_in, v_sgu_norm_g=v_sgu_norm_g, v_sgu_w_s=v_sgu_w_s, v_sgu_b_s=v_sgu_b_s, v_sgu_w_out=v_sgu_w_out, v_mla_w_in=v_mla_w_in, v_mla_q_norm_g=v_mla_q_norm_g, v_mla_kv_norm_g=v_mla_kv_norm_g, v_mla_w_uq=v_mla_w_uq, v_mla_w_ukv=v_mla_w_ukv, v_mla_w_out=v_mla_w_out)
    weights = {n: given[n] for n in TWIN_WEIGHTS}
    shared = {n: given[n] for n in SHARED_INPUTS}
    per_example = {n: given[n] for n in ['x', 'c']}
    grad_fn = _jax.value_and_grad(_loss, argnums=(0, 1))

    def one_microbatch(ex, loss_target):
        ex = dict(ex)
        diff = ex.pop(TWIN_DIFF_INPUT)
        return grad_fn(weights, diff, {**shared, **ex}, loss_target)

    if N_MICROBATCH == 1:
        loss, (grad_w, grad_x) = one_microbatch(per_example, given["loss_target"])
    else:
        def body(carry, xs):
            loss_sum, grad_sum = carry
            l_k, (gw_k, gx_k) = one_microbatch(xs[0], xs[1])
            with _jax.named_scope("update"):
                return (loss_sum + l_k, _jax.tree.map(_jnp.add, grad_sum, gw_k)), gx_k

        init = (_jnp.zeros((), _jnp.float32), _jax.tree.map(_jnp.zeros_like, weights))
        (loss, grad_w), grad_x = _jax.lax.scan(body, init, (per_example, given["loss_target"]))
    with _jax.named_scope("update"):
        delta_w, new_m, new_v = {}, {}, {}
        for n in TWIN_WEIGHTS:
            delta_w[n], new_m[n], new_v[n] = _adamw(weights[n], grad_w[n], given["m_" + n], given["v_" + n])
    return (loss, grad_x, *[grad_w[n] for n in TWIN_WEIGHTS], *[delta_w[n] for n in TWIN_WEIGHTS],
            *[new_m[n] for n in TWIN_WEIGHTS], *[new_v[n] for n in TWIN_WEIGHTS])
```

```python
import functools
import math

import jax
import jax.numpy as jnp
import numpy as np
from jax import lax
from jax.experimental import pallas as pl
from jax.experimental.pallas import tpu as pltpu

F32 = jnp.float32
BF16 = jnp.bfloat16
MESH = pl.DeviceIdType.MESH

NORM_EPS = 1e-6
CHUNK = 64
SGU_BLOCK = 128
SGU_GROUPS = 16
HEADS = 16
NOPE = 128
ROPE = 64
VDIM = 128
QK_PAD = 256
Q_RANK = 448
Q_RANK_PAD = 512
KV_RANK = 512
ROPE_THETA = 10000.0
ATTN_SCALE = (NOPE + ROPE) ** -0.5

ADAM_LR = 0.001
ADAM_B1 = 0.9
ADAM_B2 = 0.999
ADAM_EPS = 1e-08
ADAM_WD = 0.01
ADAM_STEP = 10

LANE = 128
VMEM_LIMIT = 48 * 1024 * 1024

NN = (((1,), (0,)), ((), ()))
NT = (((1,), (1,)), ((), ()))
TN = (((0,), (0,)), ((), ()))


def _params(*sem):
    return pltpu.CompilerParams(dimension_semantics=sem, vmem_limit_bytes=VMEM_LIMIT)


def _row_tile(rows, row_bytes, target_bytes=1 << 20):
    if rows * row_bytes <= target_bytes or rows % 16:
        return rows
    best = 16
    t = 16
    while t <= rows:
        if rows % t == 0 and t * row_bytes <= target_bytes:
            best = t
        t += 16
    return best


def _fit(dim, target):
    if dim <= target:
        return dim
    t = (target // LANE) * LANE
    while t > LANE and dim % t:
        t -= LANE
    return t


def _gelu(x):
    return 0.5 * x * (1.0 + lax.erf(x * 0.7071067811865476))


def _gelu_grad(x):
    return 0.5 * (1.0 + lax.erf(x * 0.7071067811865476)) + x * jnp.exp(-0.5 * x * x) * 0.3989422804014327


def _mm(a, b, *, ta=False, tb=False, out_dtype=F32, tm=1024, tn=1024, tk=2048, name):
    if ta:
        K, M = a.shape
    else:
        M, K = a.shape
    if tb:
        N, K2 = b.shape
    else:
        K2, N = b.shape
    assert K == K2, (a.shape, b.shape, ta, tb)
    tm, tn, tk = _fit(M, tm), _fit(N, tn), _fit(K, tk)
    assert M % tm == 0 and N % tn == 0 and K % tk == 0, (M, N, K, tm, tn, tk)
    nk = K // tk
    dims = (((0 if ta else 1,), (1 if tb else 0,)), ((), ()))

    def body(a_ref, b_ref, o_ref, *scratch):
        prod = lax.dot_general(a_ref[...].astype(BF16), b_ref[...].astype(BF16), dims,
                               preferred_element_type=F32)
        if nk == 1:
            o_ref[...] = prod.astype(out_dtype)
        else:
            acc_ref, = scratch
            k = pl.program_id(2)

            @pl.when(k == 0)
            def _():
                acc_ref[...] = prod

            @pl.when(k > 0)
            def _():
                acc_ref[...] += prod

            @pl.when(k == nk - 1)
            def _():
                o_ref[...] = acc_ref[...].astype(out_dtype)

    a_spec = (pl.BlockSpec((tk, tm), lambda i, j, k: (k, i)) if ta
              else pl.BlockSpec((tm, tk), lambda i, j, k: (i, k)))
    b_spec = (pl.BlockSpec((tn, tk), lambda i, j, k: (j, k)) if tb
              else pl.BlockSpec((tk, tn), lambda i, j, k: (k, j)))
    return pl.pallas_call(
        body, name=name,
        out_shape=jax.ShapeDtypeStruct((M, N), out_dtype),
        grid=(M // tm, N // tn, nk),
        in_specs=[a_spec, b_spec],
        out_specs=pl.BlockSpec((tm, tn), lambda i, j, k: (i, j)),
        scratch_shapes=[] if nk == 1 else [pltpu.VMEM((tm, tn), F32)],
        compiler_params=_params("parallel", "parallel", "arbitrary"),
    )(a, b)


def _split_bf16(v):
    hi = v.astype(BF16)
    lo = (v - hi.astype(F32)).astype(BF16)
    return hi, lo


def _dot3(a, b, dims):
    a_hi, a_lo = _split_bf16(a)
    b_hi, b_lo = _split_bf16(b)
    out = lax.dot_general(a_hi, b_hi, dims, preferred_element_type=F32)
    out += lax.dot_general(a_lo, b_hi, dims, preferred_element_type=F32)
    out += lax.dot_general(a_hi, b_lo, dims, preferred_element_type=F32)
    return out


def _ada_mod(c_all, ada_w, ada_b_cols, name):
    L, D, cols = ada_w.shape
    B = c_all.shape[0]
    tn = 512 if cols % 512 == 0 else cols

    def body(c_ref, w_ref, b_ref, o_ref):
        cv = c_ref[...]
        cond = cv * jax.nn.sigmoid(cv)
        o_ref[...] = _dot3(cond, w_ref[...], NN) + b_ref[...]

    return pl.pallas_call(
        body, name=name,
        out_shape=jax.ShapeDtypeStruct((L, B, cols), F32),
        grid=(L, cols // tn),
        in_specs=[pl.BlockSpec((B, D), lambda l, j: (0, 0)),
                  pl.BlockSpec((None, D, tn), lambda l, j: (l, 0, j)),
                  pl.BlockSpec((None, 1, tn), lambda l, j: (l, 0, j))],
        out_specs=pl.BlockSpec((None, B, tn), lambda l, j: (l, 0, j)),
        compiler_params=_params("parallel", "parallel"),
    )(c_all, ada_w, ada_b_cols.reshape(L, 1, cols))


def _ada_grad(c_t, dmod_cols, name):
    L, B, cols = dmod_cols.shape
    D = c_t.shape[0]
    tn = 512 if cols % 512 == 0 else cols

    def body(c_ref, d_ref, o_ref):
        cv = c_ref[...]
        cond = cv * jax.nn.sigmoid(cv)
        o_ref[...] = _dot3(cond, d_ref[...], NN)

    return pl.pallas_call(
        body, name=name,
        out_shape=jax.ShapeDtypeStruct((L, D, cols), F32),
        grid=(L, cols // tn),
        in_specs=[pl.BlockSpec((D, B), lambda l, j: (0, 0)),
                  pl.BlockSpec((None, B, tn), lambda l, j: (l, 0, j))],
        out_specs=pl.BlockSpec((None, D, tn), lambda l, j: (l, 0, j)),
        compiler_params=_params("parallel", "parallel"),
    )(c_t, dmod_cols)


def _row_spec(ts, width):
    return pl.BlockSpec((ts, width), lambda i: (i, 0))


def _vec_spec(width):
    return pl.BlockSpec((1, width), lambda i: (0, 0))


def _pre_fwd(x, pre_g, scale, shift, name):
    S, D = x.shape
    ts = min(256, S)

    def body(x_ref, g_ref, sc_ref, sh_ref, h_ref):
        xv = x_ref[...]
        r = lax.rsqrt(jnp.mean(xv * xv, axis=-1, keepdims=True) + NORM_EPS)
        h_ref[...] = ((xv * r * g_ref[...]) * (1.0 + sc_ref[...]) + sh_ref[...]).astype(BF16)

    return pl.pallas_call(
        body, name=name, out_shape=jax.ShapeDtypeStruct((S, D), BF16), grid=(S // ts,),
        in_specs=[_row_spec(ts, D), _vec_spec(D), _vec_spec(D), _vec_spec(D)],
        out_specs=_row_spec(ts, D), compiler_params=_params("parallel"),
    )(x, pre_g, scale, shift)


def _pre_bwd(dh, x, dx_res, pre_g, scale, name):
    S, D = x.shape
    ts = min(256, S)

    def body(dh_ref, x_ref, dr_ref, g_ref, sc_ref, dx_ref, dsh_ref, dsc_ref, dg_ref):
        @pl.when(pl.program_id(0) == 0)
        def _():
            dsh_ref[...] = jnp.zeros_like(dsh_ref)
            dsc_ref[...] = jnp.zeros_like(dsc_ref)
            dg_ref[...] = jnp.zeros_like(dg_ref)

        dh = dh_ref[...]
        xv = x_ref[...]
        g = g_ref[...]
        one_sc = 1.0 + sc_ref[...]
        r = lax.rsqrt(jnp.mean(xv * xv, axis=-1, keepdims=True) + NORM_EPS)
        xn = xv * r
        dsh_ref[...] += jnp.sum(dh, axis=0, keepdims=True)
        dsc_ref[...] += jnp.sum(dh * (xn * g), axis=0, keepdims=True)
        dg_ref[...] += jnp.sum(dh * one_sc * xn, axis=0, keepdims=True)
        dxn = dh * one_sc * g
        dx_ref[...] = dr_ref[...] + r * (dxn - xn * jnp.mean(dxn * xn, axis=-1, keepdims=True))

    vec = jax.ShapeDtypeStruct((1, D), F32)
    return pl.pallas_call(
        body, name=name, out_shape=(jax.ShapeDtypeStruct((S, D), F32), vec, vec, vec), grid=(S // ts,),
        in_specs=[_row_spec(ts, D), _row_spec(ts, D), _row_spec(ts, D), _vec_spec(D), _vec_spec(D)],
        out_specs=(_row_spec(ts, D), _vec_spec(D), _vec_spec(D), _vec_spec(D)),
        compiler_params=_params("arbitrary"),
    )(dh, x, dx_res, pre_g, scale)


def _post_fwd(x, y, gate, post_g, name):
    S, D = x.shape
    ts = min(256, S)

    def body(x_ref, y_ref, gt_ref, g_ref, o_ref):
        yv = y_ref[...]
        r = lax.rsqrt(jnp.mean(yv * yv, axis=-1, keepdims=True) + NORM_EPS)
        o_ref[...] = x_ref[...] + gt_ref[...] * (yv * r * g_ref[...])

    return pl.pallas_call(
        body, name=name, out_shape=jax.ShapeDtypeStruct((S, D), F32), grid=(S // ts,),
        in_specs=[_row_spec(ts, D), _row_spec(ts, D), _vec_spec(D), _vec_spec(D)],
        out_specs=_row_spec(ts, D), compiler_params=_params("parallel"),
    )(x, y, gate, post_g)


def _post_bwd(dx, y, gate, post_g, name):
    S, D = y.shape
    ts = min(256, S)

    def body(dx_ref, y_ref, gt_ref, g_ref, dy_ref, dgt_ref, dg_ref):
        @pl.when(pl.program_id(0) == 0)
        def _():
            dgt_ref[...] = jnp.zeros_like(dgt_ref)
            dg_ref[...] = jnp.zeros_like(dg_ref)

        dxv = dx_ref[...]
        yv = y_ref[...]
        g = g_ref[...]
        gt = gt_ref[...]
        r = lax.rsqrt(jnp.mean(yv * yv, axis=-1, keepdims=True) + NORM_EPS)
        yn = yv * r
        dgt_ref[...] += jnp.sum(dxv * (yn * g), axis=0, keepdims=True)
        dg_ref[...] += jnp.sum(dxv * gt * yn, axis=0, keepdims=True)
        dyn = dxv * gt * g
        dy_ref[...] = (r * (dyn - yn * jnp.mean(dyn * yn, axis=-1, keepdims=True))).astype(BF16)

    vec = jax.ShapeDtypeStruct((1, D), F32)
    return pl.pallas_call(
        body, name=name, out_shape=(jax.ShapeDtypeStruct((S, D), BF16), vec, vec), grid=(S // ts,),
        in_specs=[_row_spec(ts, D), _row_spec(ts, D), _vec_spec(D), _vec_spec(D)],
        out_specs=(_row_spec(ts, D), _vec_spec(D), _vec_spec(D)),
        compiler_params=_params("arbitrary"),
    )(dx, y, gate, post_g)


def _loss_grad(xf, target, name):
    S, D = xf.shape
    ts = min(256, S)

    def body(x_ref, t_ref, dx_ref, l_ref):
        @pl.when(pl.program_id(0) == 0)
        def _():
            l_ref[...] = jnp.zeros_like(l_ref)

        e = x_ref[...] - t_ref[...]
        dx_ref[...] = e * (1.0 / D)
        row = jnp.sum(e * e, axis=1, keepdims=True) * (1.0 / D)
        l_ref[...] += 0.5 * jnp.sum(row, axis=0, keepdims=True)

    return pl.pallas_call(
        body, name=name,
        out_shape=(jax.ShapeDtypeStruct((S, D), F32), jax.ShapeDtypeStruct((1, 1), F32)), grid=(S // ts,),
        in_specs=[_row_spec(ts, D), _row_spec(ts, D)],
        out_specs=(_row_spec(ts, D), pl.BlockSpec((1, 1), lambda i: (0, 0))),
        compiler_params=_params("arbitrary"),
    )(xf, target)


def _chunk_mask(transposed=False):
    row = lax.broadcasted_iota(jnp.int32, (SGU_BLOCK, SGU_BLOCK), 0) // CHUNK
    col = lax.broadcasted_iota(jnp.int32, (SGU_BLOCK, SGU_BLOCK), 1) // CHUNK
    return (row <= col) if transposed else (col <= row)


def _sgu_gate_fwd(uvz, norm_g, w_s, b_bc, name):
    S, E3 = uvz.shape
    E = E3 // 3
    T = SGU_BLOCK
    gd = E // SGU_GROUPS

    def body(uvz_ref, ng_ref, ws_ref, bb_ref, y_ref, v_scr):
        gv = _gelu(uvz_ref[:, E:2 * E])
        mu = jnp.mean(gv, axis=-1, keepdims=True)
        xc = gv - mu
        rstd = lax.rsqrt(jnp.mean(xc * xc, axis=-1, keepdims=True) + NORM_EPS)
        v_scr[...] = (xc * rstd * ng_ref[...]).astype(BF16)
        mask = _chunk_mask()
        for g in range(SGU_GROUPS):
            sl = slice(g * gd, (g + 1) * gd)
            wg = jnp.where(mask, ws_ref[g], 0.0).astype(BF16)
            vm = lax.dot_general(wg, v_scr[:, sl], NN, preferred_element_type=F32)
            vm = vm + jnp.tile(bb_ref[g], (1, gd // LANE))
            z = uvz_ref[:, 2 * E + g * gd:2 * E + (g + 1) * gd]
            y_ref[:, sl] = (_gelu(uvz_ref[:, sl]) * vm * (z * jax.nn.sigmoid(z))).astype(BF16)

    return pl.pallas_call(
        body, name=name, out_shape=jax.ShapeDtypeStruct((S, E), BF16), grid=(S // T,),
        in_specs=[_row_spec(T, E3), _vec_spec(E),
                  pl.BlockSpec((SGU_GROUPS, T, T), lambda i: (0, 0, 0)),
                  pl.BlockSpec((SGU_GROUPS, T, LANE), lambda i: (0, 0, 0))],
        out_specs=_row_spec(T, E),
        scratch_shapes=[pltpu.VMEM((T, E), BF16)],
        compiler_params=_params("parallel"),
    )(uvz, norm_g, w_s, b_bc)


def _sgu_gate_bwd(uvz, dyv, norm_g, w_s, w_sT, b_bc, name):
    S, E3 = uvz.shape
    E = E3 // 3
    T = SGU_BLOCK
    gd = E // SGU_GROUPS

    def body(uvz_ref, dyv_ref, ng_ref, ws_ref, wst_ref, bb_ref,
             d_ref, dng_ref, dws_ref, dbs_ref, vhat_scr, dv_scr):
        @pl.when(pl.program_id(0) == 0)
        def _():
            dng_ref[...] = jnp.zeros_like(dng_ref)
            dws_ref[...] = jnp.zeros_like(dws_ref)
            dbs_ref[...] = jnp.zeros_like(dbs_ref)

        gv = _gelu(uvz_ref[:, E:2 * E])
        mu = jnp.mean(gv, axis=-1, keepdims=True)
        xc = gv - mu
        rstd = lax.rsqrt(jnp.mean(xc * xc, axis=-1, keepdims=True) + NORM_EPS)
        vhat_scr[...] = xc * rstd
        mask = _chunk_mask()
        mask_t = _chunk_mask(transposed=True)
        for g in range(SGU_GROUPS):
            sl = slice(g * gd, (g + 1) * gd)
            u_pre = uvz_ref[:, sl]
            z = uvz_ref[:, 2 * E + g * gd:2 * E + (g + 1) * gd]
            dy = dyv_ref[:, sl]
            u = _gelu(u_pre)
            sig = jax.nn.sigmoid(z)
            sz = z * sig
            vg = (vhat_scr[:, sl] * ng_ref[:, sl]).astype(BF16)
            wg = jnp.where(mask, ws_ref[g], 0.0).astype(BF16)
            vm = lax.dot_general(wg, vg, NN, preferred_element_type=F32)
            vm = vm + jnp.tile(bb_ref[g], (1, gd // LANE))
            dy_u = dy * u
            d_ref[:, sl] = (dy * vm * sz * _gelu_grad(u_pre)).astype(BF16)
            d_ref[:, 2 * E + g * gd:2 * E + (g + 1) * gd] = (
                dy_u * vm * (sig * (1.0 + z * (1.0 - sig)))).astype(BF16)
            dvm = dy_u * sz
            dvm_b = dvm.astype(BF16)
            dws_ref[g] += jnp.where(mask, lax.dot_general(dvm_b, vg, NT, preferred_element_type=F32), 0.0)
            dbs_ref[g] += jnp.broadcast_to(jnp.sum(dvm, axis=1, keepdims=True), (T, LANE))
            wgt = jnp.where(mask_t, wst_ref[g], 0.0).astype(BF16)
            dv_scr[:, sl] = lax.dot_general(wgt, dvm_b, NN, preferred_element_type=F32)
        dv = dv_scr[...]
        vhat = vhat_scr[...]
        dng_ref[...] += jnp.sum(dv * vhat, axis=0, keepdims=True)
        dvh = dv * ng_ref[...]
        dgv = rstd * (dvh - jnp.mean(dvh, axis=-1, keepdims=True)
                      - vhat * jnp.mean(dvh * vhat, axis=-1, keepdims=True))
        d_ref[:, E:2 * E] = (dgv * _gelu_grad(uvz_ref[:, E:2 * E])).astype(BF16)

    wspec = pl.BlockSpec((SGU_GROUPS, T, T), lambda i: (0, 0, 0))
    bspec = pl.BlockSpec((SGU_GROUPS, T, LANE), lambda i: (0, 0, 0))
    return pl.pallas_call(
        body, name=name,
        out_shape=(jax.ShapeDtypeStruct((S, E3), BF16), jax.ShapeDtypeStruct((1, E), F32),
                   jax.ShapeDtypeStruct((SGU_GROUPS, T, T), F32),
                   jax.ShapeDtypeStruct((SGU_GROUPS, T, LANE), F32)),
        grid=(S // T,),
        in_specs=[_row_spec(T, E3), _row_spec(T, E), _vec_spec(E), wspec, wspec, bspec],
        out_specs=(_row_spec(T, E3), _vec_spec(E), wspec, bspec),
        scratch_shapes=[pltpu.VMEM((T, E), F32), pltpu.VMEM((T, E), F32)],
        compiler_params=_params("arbitrary"),
    )(uvz, dyv, norm_g, w_s, w_sT, b_bc)


MLA_WIDTH = HEADS * VDIM
P_CQ = MLA_WIDTH
P_CKV = P_CQ + Q_RANK_PAD
P_KR = P_CKV + KV_RANK
P_WIDTH = P_KR + LANE


def _swap_halves(v):
    lane = lax.broadcasted_iota(jnp.int32, v.shape, 1)
    return jnp.where(lane % ROPE < ROPE // 2, pltpu.roll(v, LANE - ROPE // 2, 1), pltpu.roll(v, ROPE // 2, 1))


def _mla_mid_fwd(p, qg, kvg, name):
    S, PW = p.shape
    ts = min(256, S)

    def body(cq_ref, ckv_ref, qg_ref, kvg_ref, cqn_ref, ckvn_ref):
        cq = cq_ref[...]
        r = lax.rsqrt(jnp.sum(cq * cq, axis=-1, keepdims=True) * (1.0 / Q_RANK) + NORM_EPS)
        cqn_ref[...] = (cq * r * qg_ref[...]).astype(BF16)
        ckv = ckv_ref[...]
        r2 = lax.rsqrt(jnp.mean(ckv * ckv, axis=-1, keepdims=True) + NORM_EPS)
        ckvn_ref[...] = (ckv * r2 * kvg_ref[...]).astype(BF16)

    return pl.pallas_call(
        body, name=name,
        out_shape=(jax.ShapeDtypeStruct((S, Q_RANK_PAD), BF16), jax.ShapeDtypeStruct((S, KV_RANK), BF16)),
        grid=(S // ts,),
        in_specs=[pl.BlockSpec((ts, Q_RANK_PAD), lambda i: (i, P_CQ // Q_RANK_PAD)),
                  pl.BlockSpec((ts, KV_RANK), lambda i: (i, P_CKV // KV_RANK)),
                  _vec_spec(Q_RANK_PAD), _vec_spec(KV_RANK)],
        out_specs=(_row_spec(ts, Q_RANK_PAD), _row_spec(ts, KV_RANK)),
        compiler_params=_params("parallel"),
    )(p, p, qg, kvg)


def _mla_pack(q, kv, p, cos_t, sin_t, name):
    S = q.shape[0]
    ts = min(256, S)
    nope_w = HEADS * NOPE

    def body(q_ref, kv_ref, kr_ref, cos_ref, sin_ref, qo_ref, ko_ref, vo_ref):
        cosv = cos_ref[...]
        sinv = sin_ref[...]
        lane = lax.broadcasted_iota(jnp.int32, (ts, LANE), 1)
        low = lane < ROPE
        kr = kr_ref[...]
        kr = (kr * cosv + _swap_halves(kr) * sinv).astype(BF16)
        for pair in range(HEADS // 2):
            blk = q_ref[:, nope_w + pair * LANE:nope_w + (pair + 1) * LANE]
            roped = blk * cosv + _swap_halves(blk) * sinv
            for sub in range(2):
                h = 2 * pair + sub
                piece = roped if sub == 0 else pltpu.roll(roped, ROPE, 1)
                qo_ref[h, :, 0:NOPE] = q_ref[:, h * NOPE:(h + 1) * NOPE].astype(BF16)
                qo_ref[h, :, NOPE:QK_PAD] = jnp.where(low, piece, 0.0).astype(BF16)
        for h in range(HEADS):
            ko_ref[h, :, 0:NOPE] = kv_ref[:, h * NOPE:(h + 1) * NOPE].astype(BF16)
            ko_ref[h, :, NOPE:QK_PAD] = kr
            vo_ref[h] = kv_ref[:, nope_w + h * VDIM:nope_w + (h + 1) * VDIM].astype(BF16)

    return pl.pallas_call(
        body, name=name,
        out_shape=(jax.ShapeDtypeStruct((HEADS, S, QK_PAD), BF16), jax.ShapeDtypeStruct((HEADS, S, QK_PAD), BF16),
                   jax.ShapeDtypeStruct((HEADS, S, VDIM), BF16)),
        grid=(S // ts,),
        in_specs=[_row_spec(ts, q.shape[1]), _row_spec(ts, kv.shape[1]),
                  pl.BlockSpec((ts, LANE), lambda i: (i, P_KR // LANE)),
                  _row_spec(ts, LANE), _row_spec(ts, LANE)],
        out_specs=(pl.BlockSpec((HEADS, ts, QK_PAD), lambda i: (0, i, 0)),
                   pl.BlockSpec((HEADS, ts, QK_PAD), lambda i: (0, i, 0)),
                   pl.BlockSpec((HEADS, ts, VDIM), lambda i: (0, i, 0))),
        compiler_params=_params("parallel"),
    )(q, kv, p, cos_t, sin_t)


def _mla_unpack(dQ, dK, dV, cos_t, sin_t, name):
    S = dQ.shape[1]
    ts = min(256, S)
    nope_w = HEADS * NOPE

    def body(dq_ref, dk_ref, dv_ref, cos_ref, sin_ref, q_ref, kv_ref, kr_ref):
        cosv = cos_ref[...]
        sinv = sin_ref[...]
        for pair in range(HEADS // 2):
            blk = dq_ref[2 * pair, :, NOPE:QK_PAD] + pltpu.roll(dq_ref[2 * pair + 1, :, NOPE:QK_PAD], ROPE, 1)
            q_ref[:, nope_w + pair * LANE:nope_w + (pair + 1) * LANE] = (
                blk * cosv - _swap_halves(blk) * sinv).astype(BF16)
        dkr = dk_ref[0, :, NOPE:QK_PAD]
        for h in range(1, HEADS):
            dkr = dkr + dk_ref[h, :, NOPE:QK_PAD]
        kr_ref[...] = dkr * cosv - _swap_halves(dkr) * sinv
        for h in range(HEADS):
            q_ref[:, h * NOPE:(h + 1) * NOPE] = dq_ref[h, :, 0:NOPE].astype(BF16)
            kv_ref[:, h * NOPE:(h + 1) * NOPE] = dk_ref[h, :, 0:NOPE].astype(BF16)
            kv_ref[:, nope_w + h * VDIM:nope_w + (h + 1) * VDIM] = dv_ref[h].astype(BF16)

    return pl.pallas_call(
        body, name=name,
        out_shape=(jax.ShapeDtypeStruct((S, HEADS * (NOPE + ROPE)), BF16),
                   jax.ShapeDtypeStruct((S, HEADS * (NOPE + VDIM)), BF16),
                   jax.ShapeDtypeStruct((S, LANE), F32)),
        grid=(S // ts,),
        in_specs=[pl.BlockSpec((HEADS, ts, QK_PAD), lambda i: (0, i, 0)),
                  pl.BlockSpec((HEADS, ts, QK_PAD), lambda i: (0, i, 0)),
                  pl.BlockSpec((HEADS, ts, VDIM), lambda i: (0, i, 0)),
                  _row_spec(ts, LANE), _row_spec(ts, LANE)],
        out_specs=(_row_spec(ts, HEADS * (NOPE + ROPE)), _row_spec(ts, HEADS * (NOPE + VDIM)),
                   _row_spec(ts, LANE)),
        compiler_params=_params("parallel"),
    )(dQ, dK, dV, cos_t, sin_t)


def _mla_gate_fwd(o, p, name):
    S, W = o.shape
    ts = min(256, S)

    def body(o_ref, z_ref, y_ref):
        z = z_ref[...]
        y_ref[...] = (o_ref[...] * (z * jax.nn.sigmoid(z))).astype(BF16)

    return pl.pallas_call(
        body, name=name, out_shape=jax.ShapeDtypeStruct((S, W), BF16), grid=(S // ts,),
        in_specs=[_row_spec(ts, W), _row_spec(ts, W)],
        out_specs=_row_spec(ts, W), compiler_params=_params("parallel"),
    )(o, p)


def _mla_gate_bwd(dyv, p, name):
    S, W = dyv.shape
    ts = min(256, S)

    def body(d_ref, z_ref, do_ref):
        z = z_ref[...]
        do_ref[...] = d_ref[...] * (z * jax.nn.sigmoid(z))

    return pl.pallas_call(
        body, name=name, out_shape=jax.ShapeDtypeStruct((S, W), F32), grid=(S // ts,),
        in_specs=[_row_spec(ts, W), _row_spec(ts, W)],
        out_specs=_row_spec(ts, W), compiler_params=_params("parallel"),
    )(dyv, p)


def _mla_mid_bwd(p, dcqn, dckvn, dkr, dyv, o, qg, kvg, name):
    S, PW = p.shape
    W = o.shape[1]
    ts = min(256, S)

    def rms_bwd(xv, dy, g, count):
        r = lax.rsqrt(jnp.sum(xv * xv, axis=-1, keepdims=True) * (1.0 / count) + NORM_EPS)
        xn = xv * r
        dg = jnp.sum(dy * xn, axis=0, keepdims=True)
        dxn = dy * g
        dx = r * (dxn - xn * (jnp.sum(dxn * xn, axis=-1, keepdims=True) * (1.0 / count)))
        return dx, dg

    def body(p_ref, dcq_ref, dckv_ref, dkr_ref, dyv_ref, o_ref, qg_ref, kvg_ref, dp_ref, dqg_ref, dkvg_ref):
        @pl.when(pl.program_id(0) == 0)
        def _():
            dqg_ref[...] = jnp.zeros_like(dqg_ref)
            dkvg_ref[...] = jnp.zeros_like(dkvg_ref)

        dx, dg = rms_bwd(p_ref[:, P_CQ:P_CKV], dcq_ref[...], qg_ref[...], Q_RANK)
        dp_ref[:, P_CQ:P_CKV] = dx.astype(BF16)
        dqg_ref[...] += dg
        dx, dg = rms_bwd(p_ref[:, P_CKV:P_KR], dckv_ref[...], kvg_ref[...], KV_RANK)
        dp_ref[:, P_CKV:P_KR] = dx.astype(BF16)
        dkvg_ref[...] += dg
        dp_ref[:, P_KR:P_WIDTH] = dkr_ref[...].astype(BF16)
        z = p_ref[:, 0:MLA_WIDTH]
        sig = jax.nn.sigmoid(z)
        dp_ref[:, 0:MLA_WIDTH] = (dyv_ref[...] * o_ref[...] * (sig * (1.0 + z * (1.0 - sig)))).astype(BF16)

    return pl.pallas_call(
        body, name=name,
        out_shape=(jax.ShapeDtypeStruct((S, PW), BF16), jax.ShapeDtypeStruct((1, Q_RANK_PAD), F32),
                   jax.ShapeDtypeStruct((1, KV_RANK), F32)),
        grid=(S // ts,),
        in_specs=[_row_spec(ts, PW), _row_spec(ts, Q_RANK_PAD), _row_spec(ts, KV_RANK), _row_spec(ts, LANE),
                  _row_spec(ts, W), _row_spec(ts, W), _vec_spec(Q_RANK_PAD), _vec_spec(KV_RANK)],
        out_specs=(_row_spec(ts, PW), _vec_spec(Q_RANK_PAD), _vec_spec(KV_RANK)),
        compiler_params=_params("arbitrary"),
    )(p, dcqn, dckvn, dkr, dyv, o, qg, kvg)


def _tile_mask(T):
    row = lax.broadcasted_iota(jnp.int32, (T, T), 0) // CHUNK
    col = lax.broadcasted_iota(jnp.int32, (T, T), 1) // CHUNK
    return col <= row


def _attn_fwd(Q, K, V, name):
    H, S, _ = Q.shape
    T = min(512, S)

    def body(q_ref, k_ref, v_ref, o_ref, lse_ref, m_scr, l_scr, acc_scr):
        qi = pl.program_id(1)
        q = q_ref[...]
        m_scr[...] = jnp.full_like(m_scr, -jnp.inf)
        l_scr[...] = jnp.zeros_like(l_scr)
        acc_scr[...] = jnp.zeros_like(acc_scr)

        def tile(j, masked):
            rows = pl.ds(pl.multiple_of(j * T, T), T)
            s = lax.dot_general(q, k_ref[rows, :], NT, preferred_element_type=F32) * ATTN_SCALE
            if masked:
                s = jnp.where(_tile_mask(T), s, -1e30)
            m_prev = m_scr[...]
            m_new = jnp.maximum(m_prev, jnp.max(s, axis=1, keepdims=True))
            pr = jnp.exp(s - m_new)
            alpha = jnp.exp(m_prev - m_new)
            l_scr[...] = alpha * l_scr[...] + jnp.sum(pr, axis=1, keepdims=True)
            acc_scr[...] = alpha * acc_scr[...] + lax.dot_general(
                pr.astype(BF16), v_ref[rows, :], NN, preferred_element_type=F32)
            m_scr[...] = m_new

        def full_tile(j, carry):
            tile(j, False)
            return carry

        lax.fori_loop(0, qi, full_tile, 0)
        tile(qi, True)
        l = l_scr[...]
        o_ref[...] = acc_scr[...] / l
        lse_ref[...] = jnp.broadcast_to(m_scr[...] + jnp.log(l), (T, LANE))

    return pl.pallas_call(
        body, name=name,
        out_shape=(jax.ShapeDtypeStruct((S, H * VDIM), F32), jax.ShapeDtypeStruct((H, S, LANE), F32)),
        grid=(H, S // T),
        in_specs=[pl.BlockSpec((None, T, QK_PAD), lambda h, i: (h, i, 0)),
                  pl.BlockSpec((None, S, QK_PAD), lambda h, i: (h, 0, 0)),
                  pl.BlockSpec((None, S, VDIM), lambda h, i: (h, 0, 0))],
        out_specs=(pl.BlockSpec((T, VDIM), lambda h, i: (i, h)),
                   pl.BlockSpec((None, T, LANE), lambda h, i: (h, i, 0))),
        scratch_shapes=[pltpu.VMEM((T, 1), F32), pltpu.VMEM((T, 1), F32), pltpu.VMEM((T, VDIM), F32)],
        compiler_params=_params("parallel", "arbitrary"),
    )(Q, K, V)


def _attn_bwd(Q, K, V, o, do, lse, name):
    H, S, _ = Q.shape
    T = min(512, S)
    nq = S // T

    def body(q_ref, k_ref, v_ref, o_ref, do_ref, lse_ref, dq_ref, dk_ref, dv_ref, dk_scr, dv_scr):
        ki = pl.program_id(1)

        @pl.when(ki == 0)
        def _():
            dq_ref[...] = jnp.zeros_like(dq_ref)

        dk_scr[...] = jnp.zeros_like(dk_scr)
        dv_scr[...] = jnp.zeros_like(dv_scr)
        k = k_ref[...]
        v = v_ref[...]

        def tile(i, masked):
            rows = pl.ds(pl.multiple_of(i * T, T), T)
            q = q_ref[rows, :]
            do_f = do_ref[rows, :]
            do_b = do_f.astype(BF16)
            delta = jnp.sum(do_f * o_ref[rows, :], axis=1, keepdims=True)
            s = lax.dot_general(q, k, NT, preferred_element_type=F32) * ATTN_SCALE
            pr = jnp.exp(s - lse_ref[rows, 0:1])
            if masked:
                pr = jnp.where(_tile_mask(T), pr, 0.0)
            dv_scr[...] += lax.dot_general(pr.astype(BF16), do_b, TN, preferred_element_type=F32)
            dp = lax.dot_general(do_b, v, NT, preferred_element_type=F32)
            ds = (pr * (dp - delta) * ATTN_SCALE).astype(BF16)
            dk_scr[...] += lax.dot_general(ds, q, TN, preferred_element_type=F32)
            dq_ref[rows, :] += lax.dot_general(ds, k, NN, preferred_element_type=F32)

        def full_tile(i, carry):
            tile(i, False)
            return carry

        tile(ki, True)
        lax.fori_loop(ki + 1, nq, full_tile, 0)
        dk_ref[...] = dk_scr[...]
        dv_ref[...] = dv_scr[...]

    return pl.pallas_call(
        body, name=name,
        out_shape=(jax.ShapeDtypeStruct((H, S, QK_PAD), F32), jax.ShapeDtypeStruct((H, S, QK_PAD), F32),
                   jax.ShapeDtypeStruct((H, S, VDIM), F32)),
        grid=(H, nq),
        in_specs=[pl.BlockSpec((None, S, QK_PAD), lambda h, j: (h, 0, 0)),
                  pl.BlockSpec((None, T, QK_PAD), lambda h, j: (h, j, 0)),
                  pl.BlockSpec((None, T, VDIM), lambda h, j: (h, j, 0)),
                  pl.BlockSpec((S, VDIM), lambda h, j: (0, h)),
                  pl.BlockSpec((S, VDIM), lambda h, j: (0, h)),
                  pl.BlockSpec((None, S, LANE), lambda h, j: (h, 0, 0))],
        out_specs=(pl.BlockSpec((None, S, QK_PAD), lambda h, j: (h, 0, 0)),
                   pl.BlockSpec((None, T, QK_PAD), lambda h, j: (h, j, 0)),
                   pl.BlockSpec((None, T, VDIM), lambda h, j: (h, j, 0))),
        scratch_shapes=[pltpu.VMEM((T, QK_PAD), F32), pltpu.VMEM((T, VDIM), F32)],
        compiler_params=_params("parallel", "arbitrary"),
    )(Q, K, V, o, do, lse)


def _adamw(w, g, m, v, name):
    shape = w.shape
    C = shape[-1]
    R = math.prod(shape[:-1])
    flat = [t.reshape(R, C) for t in (w, g, m, v)]
    tr = _row_tile(R, C * 4)

    def body(w_ref, g_ref, m_ref, v_ref, d_ref, nm_ref, nv_ref):
        gv = g_ref[...]
        m_new = ADAM_B1 * m_ref[...] + (1.0 - ADAM_B1) * gv
        v_new = ADAM_B2 * v_ref[...] + (1.0 - ADAM_B2) * jnp.square(gv)
        m_hat = m_new / (1.0 - ADAM_B1 ** ADAM_STEP)
        v_hat = v_new / (1.0 - ADAM_B2 ** ADAM_STEP)
        d_ref[...] = -ADAM_LR * (m_hat / (jnp.sqrt(v_hat) + ADAM_EPS) + ADAM_WD * w_ref[...])
        nm_ref[...] = m_new
        nv_ref[...] = v_new

    spec = pl.BlockSpec((tr, C), lambda i: (i, 0))
    out = jax.ShapeDtypeStruct((R, C), F32)
    d, nm, nv = pl.pallas_call(
        body, name=name, out_shape=(out, out, out), grid=(R // tr,),
        in_specs=[spec] * 4, out_specs=(spec, spec, spec), compiler_params=_params("parallel"),
    )(*flat)
    return d.reshape(shape), nm.reshape(shape), nv.reshape(shape)


def _sum_slots(r, name):
    n, M, N = r.shape
    tr = _row_tile(M, N * 4 * n, 4 << 20)

    def body(r_ref, o_ref):
        acc = r_ref[0].astype(F32)
        for s in range(1, n):
            acc = acc + r_ref[s].astype(F32)
        o_ref[...] = acc

    return pl.pallas_call(
        body, name=name, out_shape=jax.ShapeDtypeStruct((M, N), F32), grid=(M // tr,),
        in_specs=[pl.BlockSpec((n, tr, N), lambda i: (0, i, 0))],
        out_specs=pl.BlockSpec((tr, N), lambda i: (i, 0)), compiler_params=_params("parallel"),
    )(r)


ANY = pl.BlockSpec(memory_space=pl.ANY)


def _position():
    return lax.axis_index("x"), lax.axis_index("y"), lax.axis_index("c")


def _all_gather8(xs, name):
    n = len(xs)

    def body(*refs):
        x_refs, o_refs = refs[:n], refs[n:2 * n]
        send_sems, recv_sems, local_sems = refs[2 * n:]
        x, y, c = _position()
        me, sibling = (x, y, c), (x, y, 1 - c)
        chips = [(1 - x, y), (x, 1 - y), (1 - x, 1 - y)]

        def slot(a, dev):
            return o_refs[a].at[4 * dev[0] + 2 * dev[1] + dev[2]]

        def copy(a, k, block, to, src=None):
            return pltpu.make_async_remote_copy(
                src_ref=slot(a, block) if src is None else src, dst_ref=slot(a, block),
                send_sem=send_sems.at[a, k], recv_sem=recv_sems.at[a, k],
                device_id=to, device_id_type=MESH)

        started = []
        for a in range(n):
            mine = pltpu.make_async_copy(x_refs[a], slot(a, me), local_sems.at[a])
            mine.start()
            started.append(mine)
        sends = []
        for a in range(n):
            first = [copy(a, 0, me, sibling, src=x_refs[a])]
            first += [copy(a, 1 + j, me, (*chip, c), src=x_refs[a]) for j, chip in enumerate(chips)]
            for cp in first:
                cp.start()
            sends += first
        for a in range(n):
            for j, chip in enumerate(chips):
                copy(a, 1 + j, (*chip, c), me).wait_recv()
                passed = copy(a, 4 + j, (*chip, c), sibling)
                passed.start()
                sends.append(passed)
        for a in range(n):
            copy(a, 0, sibling, me).wait_recv()
            for j, chip in enumerate(chips):
                copy(a, 4 + j, (*chip, 1 - c), me).wait_recv()
        for cp in sends:
            cp.wait_send()
        for mine in started:
            mine.wait()

    return pl.pallas_call(
        body, name=name,
        out_shape=[jax.ShapeDtypeStruct((8,) + t.shape, t.dtype) for t in xs],
        in_specs=[ANY] * n, out_specs=[ANY] * n,
        scratch_shapes=[pltpu.SemaphoreType.DMA((n, 7)), pltpu.SemaphoreType.DMA((n, 7)),
                        pltpu.SemaphoreType.DMA((n,))],
    )(*xs)


def _exchange8(gs, name):
    n = len(gs)

    def body(*refs):
        g_refs, r_refs = refs[:n], refs[n:2 * n]
        send_sems, recv_sems, local_sems = refs[2 * n:]
        x, y, c = _position()
        my = 4 * x + 2 * y + c
        copies = []
        for a in range(n):
            mine = pltpu.make_async_copy(g_refs[a].at[my], r_refs[a].at[my], local_sems.at[a])
            mine.start()
            copies.append(mine)
        remote = []
        for a in range(n):
            for m in range(1, 8):
                px = (1 - x) if m & 4 else x
                py = (1 - y) if m & 2 else y
                pc = (1 - c) if m & 1 else c
                cp = pltpu.make_async_remote_copy(
                    src_ref=g_refs[a].at[4 * px + 2 * py + pc], dst_ref=r_refs[a].at[my],
                    send_sem=send_sems.at[a, m - 1], recv_sem=recv_sems.at[a, m - 1],
                    device_id=(px, py, pc), device_id_type=MESH)
                cp.start()
                remote.append(cp)
        for cp in remote:
            cp.wait_recv()
        for cp in remote:
            cp.wait_send()
        for mine in copies:
            mine.wait()

    return pl.pallas_call(
        body, name=name,
        out_shape=[jax.ShapeDtypeStruct(t.shape, t.dtype) for t in gs],
        in_specs=[ANY] * n, out_specs=[ANY] * n,
        scratch_shapes=[pltpu.SemaphoreType.DMA((n, 7)), pltpu.SemaphoreType.DMA((n, 7)),
                        pltpu.SemaphoreType.DMA((n,))],
    )(*gs)


def _pair_swap(xs, name):
    n = len(xs)

    def body(*refs):
        x_refs, o_refs = refs[:n], refs[n:2 * n]
        send_sems, recv_sems, local_sems = refs[2 * n:]
        x, y, c = _position()
        copies, remote = [], []
        for a in range(n):
            mine = pltpu.make_async_copy(x_refs[a], o_refs[a].at[c], local_sems.at[a])
            mine.start()
            copies.append(mine)
            cp = pltpu.make_async_remote_copy(
                src_ref=x_refs[a], dst_ref=o_refs[a].at[c],
                send_sem=send_sems.at[a], recv_sem=recv_sems.at[a],
                device_id=(x, y, 1 - c), device_id_type=MESH)
            cp.start()
            remote.append(cp)
        for cp in remote:
            cp.wait_recv()
        for cp in remote:
            cp.wait_send()
        for mine in copies:
            mine.wait()

    return pl.pallas_call(
        body, name=name,
        out_shape=[jax.ShapeDtypeStruct((2,) + t.shape, t.dtype) for t in xs],
        in_specs=[ANY] * n, out_specs=[ANY] * n,
        scratch_shapes=[pltpu.SemaphoreType.DMA((n,)), pltpu.SemaphoreType.DMA((n,)),
                        pltpu.SemaphoreType.DMA((n,))],
    )(*xs)


def _pack_rows(parts):
    flat = jnp.concatenate([t.reshape(-1).astype(F32) for t in parts])
    pad = (-flat.shape[0]) % (256 * LANE)
    return jnp.pad(flat, (0, pad)).reshape(-1, LANE)


def _from_col_shards(g):
    _, R, Cs = g.shape
    return jnp.transpose(g.reshape(4, 2, R, Cs), (1, 2, 0, 3)).reshape(2, R, 4 * Cs)


def _from_row_shards(g):
    _, Rs, C = g.shape
    return jnp.transpose(g.reshape(4, 2, Rs, C), (1, 0, 2, 3)).reshape(2, 4 * Rs, C)


def _to_col_shards(dw):
    _, R, C = dw.shape
    return jnp.transpose(dw.reshape(2, R, 4, C // 4), (2, 0, 1, 3)).reshape(8, R, C // 4)


def _to_row_shards(dw):
    _, R, C = dw.shape
    return jnp.transpose(dw.reshape(2, 4, R // 4, C), (1, 0, 2, 3)).reshape(8, R // 4, C)


def _uq_cols(w):
    R = w.shape[0]
    w3 = w.reshape(R, HEADS, NOPE + ROPE)
    out = jnp.concatenate([w3[:, :, :NOPE].reshape(R, HEADS * NOPE), w3[:, :, NOPE:].reshape(R, HEADS * ROPE)], axis=1)
    return jnp.pad(out, ((0, Q_RANK_PAD - R), (0, 0)))


def _uq_cols_inv(dw):
    dw = dw[:Q_RANK]
    nope = dw[:, :HEADS * NOPE].reshape(Q_RANK, HEADS, NOPE)
    rope = dw[:, HEADS * NOPE:].reshape(Q_RANK, HEADS, ROPE)
    return jnp.concatenate([nope, rope], axis=2).reshape(Q_RANK, HEADS * (NOPE + ROPE))


def _ukv_cols(w):
    R = w.shape[0]
    w3 = w.reshape(R, HEADS, NOPE + VDIM)
    return jnp.concatenate([w3[:, :, :NOPE].reshape(R, HEADS * NOPE), w3[:, :, NOPE:].reshape(R, HEADS * VDIM)], axis=1)


def _ukv_cols_inv(dw):
    R = dw.shape[0]
    nope = dw[:, :HEADS * NOPE].reshape(R, HEADS, NOPE)
    val = dw[:, HEADS * NOPE:].reshape(R, HEADS, VDIM)
    return jnp.concatenate([nope, val], axis=2).reshape(R, HEADS * (NOPE + VDIM))


def _win_cols(w):
    R = w.shape[0]
    zq = jnp.zeros((R, Q_RANK_PAD - Q_RANK), w.dtype)
    zr = jnp.zeros((R, LANE - ROPE), w.dtype)
    o1, o2, o3 = Q_RANK, Q_RANK + KV_RANK, Q_RANK + KV_RANK + ROPE
    return jnp.concatenate([w[:, o3:], w[:, :o1], zq, w[:, o1:o2], w[:, o2:o3], zr], axis=1)


def _win_cols_inv(dw):
    return jnp.concatenate([dw[:, P_CQ:P_CQ + Q_RANK], dw[:, P_CKV:P_KR], dw[:, P_KR:P_KR + ROPE],
                            dw[:, :MLA_WIDTH]], axis=1)


def _rope_tables(S):
    pos = jnp.arange(S, dtype=F32)
    inv_freq = ROPE_THETA ** (-jnp.arange(0, ROPE, 2, dtype=F32) / ROPE)
    ang = pos[:, None] * inv_freq[None, :]
    cos, sin = jnp.cos(ang), jnp.sin(ang)
    cos_t = jnp.concatenate([cos, cos, cos, cos], axis=1)
    sin_t = jnp.concatenate([-sin, sin, -sin, sin], axis=1)
    return cos_t, sin_t


def kernel(x, c, ada_w, ada_b, pre_g, post_g, sgu_w_in, sgu_norm_g, sgu_w_s, sgu_b_s, sgu_w_out, mla_w_in, mla_q_norm_g, mla_kv_norm_g, mla_w_uq, mla_w_ukv, mla_w_out, loss_target, m_ada_w, m_ada_b, m_pre_g, m_post_g, m_sgu_w_in, m_sgu_norm_g, m_sgu_w_s, m_sgu_b_s, m_sgu_w_out, m_mla_w_in, m_mla_q_norm_g, m_mla_kv_norm_g, m_mla_w_uq, m_mla_w_ukv, m_mla_w_out, v_ada_w, v_ada_b, v_pre_g, v_post_g, v_sgu_w_in, v_sgu_norm_g, v_sgu_w_s, v_sgu_b_s, v_sgu_w_out, v_mla_w_in, v_mla_q_norm_g, v_mla_kv_norm_g, v_mla_w_uq, v_mla_w_ukv, v_mla_w_out):
    S, D = x.shape[1], x.shape[2]
    depth = ada_w.shape[0]
    E = sgu_w_out.shape[1] * 4
    xi, yi, ci = _position()
    chip = 2 * xi + yi
    dev = 4 * xi + 2 * yi + ci
    x0 = x.reshape(S, D)
    target = loss_target.reshape(S, D)

    small = _pack_rows([c, mla_q_norm_g, mla_kv_norm_g])
    mixer_w = [sgu_w_in, sgu_w_out, mla_w_in, mla_w_uq, mla_w_ukv, mla_w_out]
    mine = [lax.dynamic_index_in_dim(w, ci, 0, keepdims=False).astype(BF16) for w in mixer_w]
    small_all, g_sin, g_sout, g_min, g_uq, g_ukv, g_mout = _all_gather8([small] + mine, "gather_weights")
    small_all = small_all.reshape(8, -1)
    qn_w, kvn_w = mla_q_norm_g.shape[1], mla_kv_norm_g.shape[1]
    c_all = small_all[:, :D]
    qn_all = small_all[0::2, D:D + 2 * qn_w].reshape(4, 2, qn_w)
    kvn_all = small_all[0::2, D + 2 * qn_w:D + 2 * qn_w + 2 * kvn_w].reshape(4, 2, kvn_w)
    q_gain = jnp.pad(jnp.transpose(qn_all, (1, 0, 2)).reshape(2, 1, Q_RANK), ((0, 0), (0, 0), (0, Q_RANK_PAD - Q_RANK)))
    kv_gain = jnp.transpose(kvn_all, (1, 0, 2)).reshape(2, 1, KV_RANK)

    w_sin = _from_col_shards(g_sin)
    w_sout = _from_row_shards(g_sout)
    w_min = jax.vmap(_win_cols)(_from_col_shards(g_min))
    w_uq = jax.vmap(_uq_cols)(_from_col_shards(g_uq))
    w_ukv = jax.vmap(_ukv_cols)(_from_col_shards(g_ukv))
    w_mout = _from_row_shards(g_mout)

    cols = ada_w.shape[2]
    ada_b_cols = lax.dynamic_slice_in_dim(ada_b, chip * cols, cols, axis=1)
    c_pad = jnp.pad(c_all, ((0, 8), (0, 0)))
    mod_cols = _ada_mod(c_pad, ada_w, ada_b_cols, "ada_mod")[:, :8]
    mod_all, = _all_gather8([mod_cols.reshape(depth * 8, cols)], "gather_mod")
    mod_all = jnp.transpose(mod_all[0::2].reshape(4, depth, 8, cols), (1, 2, 0, 3)).reshape(depth, 8, 4 * cols)
    mod = lax.dynamic_index_in_dim(mod_all, dev, 1, keepdims=False)
    shift = [mod[i:i + 1, :D] for i in range(depth)]
    scale = [mod[i:i + 1, D:2 * D] for i in range(depth)]
    gate = [mod[i:i + 1, 2 * D:] for i in range(depth)]

    cos_t, sin_t = _rope_tables(S)
    b_bc = jnp.broadcast_to(sgu_b_s[:, :, :, None], sgu_b_s.shape + (LANE,))
    w_sT = jnp.swapaxes(sgu_w_s, 2, 3)

    saved = []
    xs = x0
    for i in range(depth):
        j = i // 2
        tag = f"l{i}"
        h = _pre_fwd(xs, pre_g[i:i + 1], scale[i], shift[i], f"pre_fwd_{tag}")
        if i % 2 == 0:
            uvz = _mm(h, w_sin[j], name=f"sgu_in_{tag}")
            y = _sgu_gate_fwd(uvz, sgu_norm_g[j:j + 1], sgu_w_s[j], b_bc[j], f"sgu_gate_fwd_{tag}")
            out = _mm(y, w_sout[j], name=f"sgu_out_{tag}")
            saved.append(dict(x=xs, h=h, uvz=uvz, y=y, out=out))
        else:
            p = _mm(h, w_min[j], tn=640, name=f"mla_in_{tag}")
            cqn, ckvn = _mla_mid_fwd(p, q_gain[j], kv_gain[j], f"mla_mid_fwd_{tag}")
            q = _mm(cqn, w_uq[j], name=f"mla_uq_{tag}")
            kv = _mm(ckvn, w_ukv[j], name=f"mla_ukv_{tag}")
            Q, K, V = _mla_pack(q, kv, p, cos_t, sin_t, f"mla_pack_{tag}")
            o, lse = _attn_fwd(Q, K, V, f"attn_fwd_{tag}")
            y = _mla_gate_fwd(o, p, f"mla_gate_fwd_{tag}")
            out = _mm(y, w_mout[j], name=f"mla_out_{tag}")
            saved.append(dict(x=xs, h=h, p=p, cqn=cqn, ckvn=ckvn, Q=Q, K=K, V=V, o=o, lse=lse, y=y, out=out))
        xs = _post_fwd(xs, out, gate[i], post_g[i:i + 1], f"post_fwd_{tag}")

    dx, loss_part = _loss_grad(xs, target, "loss")
    loss = lax.psum(loss_part[0, 0], ("x", "y", "c"))

    dmod = [None] * depth
    d_pre_g = [None] * depth
    d_post_g = [None] * depth
    d_sgu = [None] * 2
    d_mla = [None] * 2
    for i in reversed(range(depth)):
        j = i // 2
        tag = f"l{i}"
        sv = saved[i]
        dy, dgate, d_post_g[i] = _post_bwd(dx, sv["out"], gate[i], post_g[i:i + 1], f"post_bwd_{tag}")
        if i % 2 == 0:
            dw_out = _mm(sv["y"], dy, ta=True, out_dtype=BF16, name=f"sgu_out_dw_{tag}")
            dyv = _mm(dy, w_sout[j], tb=True, name=f"sgu_out_dx_{tag}")
            duvz, dng, dws, dbs = _sgu_gate_bwd(sv["uvz"], dyv, sgu_norm_g[j:j + 1], sgu_w_s[j], w_sT[j], b_bc[j],
                                                f"sgu_gate_bwd_{tag}")
            dw_in = _mm(sv["h"], duvz, ta=True, out_dtype=BF16, name=f"sgu_in_dw_{tag}")
            dh = _mm(duvz, w_sin[j], tb=True, name=f"sgu_in_dx_{tag}")
            d_sgu[j] = dict(w_in=dw_in, w_out=dw_out, norm_g=dng, w_s=dws, b_s=dbs[:, :, 0])
        else:
            dw_out = _mm(sv["y"], dy, ta=True, out_dtype=BF16, name=f"mla_out_dw_{tag}")
            dyv = _mm(dy, w_mout[j], tb=True, name=f"mla_out_dx_{tag}")
            p = sv["p"]
            do = _mla_gate_bwd(dyv, p, f"mla_gate_bwd_{tag}")
            dQ, dK, dV = _attn_bwd(sv["Q"], sv["K"], sv["V"], sv["o"], do, sv["lse"], f"attn_bwd_{tag}")
            dq, dkv, dkr = _mla_unpack(dQ, dK, dV, cos_t, sin_t, f"mla_unpack_{tag}")
            dw_uq = _mm(sv["cqn"], dq, ta=True, out_dtype=BF16, name=f"mla_uq_dw_{tag}")
            dcqn = _mm(dq, w_uq[j], tb=True, name=f"mla_uq_dx_{tag}")
            dw_ukv = _mm(sv["ckvn"], dkv, ta=True, out_dtype=BF16, name=f"mla_ukv_dw_{tag}")
            dckvn = _mm(dkv, w_ukv[j], tb=True, name=f"mla_ukv_dx_{tag}")
            dp, dqg, dkvg = _mla_mid_bwd(p, dcqn, dckvn, dkr, dyv, sv["o"], q_gain[j], kv_gain[j], f"mla_mid_bwd_{tag}")
            dw_in = _mm(sv["h"], dp, ta=True, out_dtype=BF16, tn=640, name=f"mla_in_dw_{tag}")
            dh = _mm(dp, w_min[j], tb=True, tk=640, name=f"mla_in_dx_{tag}")
            d_mla[j] = dict(w_in=_win_cols_inv(dw_in), w_uq=_uq_cols_inv(dw_uq), w_ukv=_ukv_cols_inv(dw_ukv),
                            w_out=dw_out, qg=dqg[0, :Q_RANK], kvg=dkvg[0])
        dx, dshift, dscale, d_pre_g[i] = _pre_bwd(dh, sv["x"], dx, pre_g[i:i + 1], scale[i], f"pre_bwd_{tag}")
        dmod[i] = jnp.concatenate([dshift, dscale, dgate], axis=1)
    grad_x = dx.reshape(x.shape)

    parts = [jnp.concatenate(dmod, axis=0), jnp.concatenate(d_pre_g, axis=0), jnp.concatenate(d_post_g, axis=0),
             jnp.stack([d["norm_g"][0] for d in d_sgu]), jnp.stack([d["w_s"] for d in d_sgu]),
             jnp.stack([d["b_s"] for d in d_sgu]), jnp.stack([d["qg"] for d in d_mla]),
             jnp.stack([d["kvg"] for d in d_mla])]
    sizes = [int(np.prod(t.shape)) for t in parts]
    packed = _pack_rows(parts)
    packed_all, = _all_gather8([packed], "gather_small_grads")
    total = _sum_slots(packed_all, "sum_small_grads").reshape(-1)
    offs = np.concatenate([[0], np.cumsum(sizes)])
    pieces = [total[int(offs[t]):int(offs[t + 1])].reshape(parts[t].shape) for t in range(len(parts))]
    g_ada_b, g_pre_g, g_post_g, g_norm_g, g_w_s, g_b_s, g_qg_full, g_kvg_full = pieces
    g_qg = lax.dynamic_slice_in_dim(g_qg_full, chip * qn_w, qn_w, axis=1)
    g_kvg = lax.dynamic_slice_in_dim(g_kvg_full, chip * kvn_w, kvn_w, axis=1)
    dmod_all = packed_all.reshape(8, -1)[:, :sizes[0]].reshape(8, depth, 3 * D)
    dmod_cols = lax.dynamic_slice_in_dim(jnp.transpose(dmod_all, (1, 0, 2)), chip * cols, cols, axis=2)
    dmod_cols = jnp.pad(dmod_cols, ((0, 0), (0, LANE - 8), (0, 0)))
    g_ada_w = _ada_grad(jnp.pad(c_all.T, ((0, 0), (0, LANE - 8))), dmod_cols, "ada_grad")

    big = [_to_col_shards(jnp.stack([d["w_in"] for d in d_sgu])),
           _to_row_shards(jnp.stack([d["w_out"] for d in d_sgu])),
           _to_col_shards(jnp.stack([d["w_in"] for d in d_mla])),
           _to_col_shards(jnp.stack([d["w_uq"] for d in d_mla])),
           _to_col_shards(jnp.stack([d["w_ukv"] for d in d_mla])),
           _to_row_shards(jnp.stack([d["w_out"] for d in d_mla]))]
    received = _exchange8(big, "exchange_grads")
    names = ["sgu_w_in", "sgu_w_out", "mla_w_in", "mla_w_uq", "mla_w_ukv", "mla_w_out"]
    reduced = [_sum_slots(r, f"sum_{nm}") for r, nm in zip(received, names)]
    g_sgu_w_in, g_sgu_w_out, g_mla_w_in, g_mla_w_uq, g_mla_w_ukv, g_mla_w_out = _pair_swap(reduced, "swap_grads")

    grads = [g_ada_w, g_ada_b, g_pre_g, g_post_g, g_sgu_w_in, g_norm_g, g_w_s, g_b_s, g_sgu_w_out,
             g_mla_w_in, g_qg, g_kvg, g_mla_w_uq, g_mla_w_ukv, g_mla_w_out]
    weights = [ada_w, ada_b, pre_g, post_g, sgu_w_in, sgu_norm_g, sgu_w_s, sgu_b_s, sgu_w_out,
               mla_w_in, mla_q_norm_g, mla_kv_norm_g, mla_w_uq, mla_w_ukv, mla_w_out]
    ms = [m_ada_w, m_ada_b, m_pre_g, m_post_g, m_sgu_w_in, m_sgu_norm_g, m_sgu_w_s, m_sgu_b_s, m_sgu_w_out,
          m_mla_w_in, m_mla_q_norm_g, m_mla_kv_norm_g, m_mla_w_uq, m_mla_w_ukv, m_mla_w_out]
    vs = [v_ada_w, v_ada_b, v_pre_g, v_post_g, v_sgu_w_in, v_sgu_norm_g, v_sgu_w_s, v_sgu_b_s, v_sgu_w_out,
          v_mla_w_in, v_mla_q_norm_g, v_mla_kv_norm_g, v_mla_w_uq, v_mla_w_ukv, v_mla_w_out]
    wnames = ["ada_w", "ada_b", "pre_g", "post_g", "sgu_w_in", "sgu_norm_g", "sgu_w_s", "sgu_b_s", "sgu_w_out",
              "mla_w_in", "mla_q_norm_g", "mla_kv_norm_g", "mla_w_uq", "mla_w_ukv", "mla_w_out"]
    grads = [g.reshape(w.shape) for g, w in zip(grads, weights)]
    deltas, new_m, new_v = [], [], []
    for w, g, m, v, nm in zip(weights, grads, ms, vs, wnames):
        d, a, b = _adamw(w, g, m, v, f"adamw_{nm}")
        deltas.append(d)
        new_m.append(a)
        new_v.append(b)
    return (loss, grad_x, *grads, *deltas, *new_m, *new_v)
```

```python
import functools
import math

import jax
import jax.numpy as jnp
import numpy as np
from jax import lax
from jax.experimental import pallas as pl
from jax.experimental.pallas import tpu as pltpu

F32 = jnp.float32
BF16 = jnp.bfloat16
MESH = pl.DeviceIdType.MESH

NORM_EPS = 1e-6
CHUNK = 64
SGU_BLOCK = 128
SGU_GROUPS = 16
HEADS = 16
NOPE = 128
ROPE = 64
VDIM = 128
QK_PAD = 256
Q_RANK = 448
Q_RANK_PAD = 512
KV_RANK = 512
ROPE_THETA = 10000.0
ATTN_SCALE = (NOPE + ROPE) ** -0.5

ADAM_LR = 0.001
ADAM_B1 = 0.9
ADAM_B2 = 0.999
ADAM_EPS = 1e-08
ADAM_WD = 0.01
ADAM_STEP = 10

LANE = 128
VMEM_LIMIT = 48 * 1024 * 1024

NN = (((1,), (0,)), ((), ()))
NT = (((1,), (1,)), ((), ()))
TN = (((0,), (0,)), ((), ()))


def _params(*sem):
    return pltpu.CompilerParams(dimension_semantics=sem, vmem_limit_bytes=VMEM_LIMIT)


def _row_tile(rows, row_bytes, target_bytes=1 << 20):
    if rows * row_bytes <= target_bytes or rows % 16:
        return rows
    best = 16
    t = 16
    while t <= rows:
        if rows % t == 0 and t * row_bytes <= target_bytes:
            best = t
        t += 16
    return best


def _fit(dim, target):
    if dim <= target:
        return dim
    t = (target // LANE) * LANE
    while t > LANE and dim % t:
        t -= LANE
    return t


def _gelu(x):
    return 0.5 * x * (1.0 + lax.erf(x * 0.7071067811865476))


def _gelu_grad(x):
    return 0.5 * (1.0 + lax.erf(x * 0.7071067811865476)) + x * jnp.exp(-0.5 * x * x) * 0.3989422804014327


def _mm(a, b, *, ta=False, tb=False, b_sharded=False, out_sharded=False, out_dtype=F32,
        tm=1024, tn=1024, tk=2048, name):
    if ta:
        K, M = a.shape
    else:
        M, K = a.shape
    if b_sharded:
        shards, rows, Cs = b.shape
        b_shape = (rows, shards * Cs)
    else:
        b_shape = b.shape
    if tb:
        N, K2 = b_shape
    else:
        K2, N = b_shape
    assert K == K2, (a.shape, b.shape, ta, tb)
    n_lim = Cs if (b_sharded and not tb) else (N // 4 if out_sharded else N)
    k_lim = Cs if (b_sharded and tb) else K
    tm, tn, tk = _fit(M, tm), _fit(n_lim, tn), _fit(k_lim, tk)
    assert M % tm == 0 and n_lim % tn == 0 and k_lim % tk == 0, (M, N, K, tm, tn, tk)
    nk = K // tk
    nb_n = n_lim // tn
    nb_k = k_lim // tk
    dims = (((0 if ta else 1,), (1 if tb else 0,)), ((), ()))

    def body(a_ref, b_ref, o_ref, *scratch):
        prod = lax.dot_general(a_ref[...].astype(BF16), b_ref[...].astype(BF16), dims,
                               preferred_element_type=F32)
        if nk == 1:
            o_ref[...] = prod.astype(out_dtype)
        else:
            acc_ref, = scratch
            k = pl.program_id(2)

            @pl.when(k == 0)
            def _():
                acc_ref[...] = prod

            @pl.when(k > 0)
            def _():
                acc_ref[...] += prod

            @pl.when(k == nk - 1)
            def _():
                o_ref[...] = acc_ref[...].astype(out_dtype)

    a_spec = (pl.BlockSpec((tk, tm), lambda i, j, k: (k, i)) if ta
              else pl.BlockSpec((tm, tk), lambda i, j, k: (i, k)))
    if b_sharded and tb:
        b_spec = pl.BlockSpec((None, tn, tk), lambda i, j, k: (k // nb_k, j, k % nb_k))
    elif b_sharded:
        b_spec = pl.BlockSpec((None, tk, tn), lambda i, j, k: (j // nb_n, k, j % nb_n))
    elif tb:
        b_spec = pl.BlockSpec((tn, tk), lambda i, j, k: (j, k))
    else:
        b_spec = pl.BlockSpec((tk, tn), lambda i, j, k: (k, j))
    if out_sharded:
        out_shape = jax.ShapeDtypeStruct((4, M, N // 4), out_dtype)
        out_spec = pl.BlockSpec((None, tm, tn), lambda i, j, k: (j // nb_n, i, j % nb_n))
    else:
        out_shape = jax.ShapeDtypeStruct((M, N), out_dtype)
        out_spec = pl.BlockSpec((tm, tn), lambda i, j, k: (i, j))
    return pl.pallas_call(
        body, name=name,
        out_shape=out_shape,
        grid=(M // tm, N // tn, nk),
        in_specs=[a_spec, b_spec],
        out_specs=out_spec,
        scratch_shapes=[] if nk == 1 else [pltpu.VMEM((tm, tn), F32)],
        compiler_params=_params("parallel", "parallel", "arbitrary"),
    )(a, b)


def _split_bf16(v):
    hi = v.astype(BF16)
    lo = (v - hi.astype(F32)).astype(BF16)
    return hi, lo


def _dot3(a, b, dims):
    a_hi, a_lo = _split_bf16(a)
    b_hi, b_lo = _split_bf16(b)
    out = lax.dot_general(a_hi, b_hi, dims, preferred_element_type=F32)
    out += lax.dot_general(a_lo, b_hi, dims, preferred_element_type=F32)
    out += lax.dot_general(a_hi, b_lo, dims, preferred_element_type=F32)
    return out


def _ada_mod(c_all, ada_w, ada_b_cols, name):
    L, D, cols = ada_w.shape
    B = c_all.shape[0]
    tn = 512 if cols % 512 == 0 else cols

    def body(c_ref, w_ref, b_ref, o_ref):
        cv = c_ref[...]
        cond = cv * jax.nn.sigmoid(cv)
        o_ref[...] = _dot3(cond, w_ref[...], NN) + b_ref[...]

    return pl.pallas_call(
        body, name=name,
        out_shape=jax.ShapeDtypeStruct((L, B, cols), F32),
        grid=(L, cols // tn),
        in_specs=[pl.BlockSpec((B, D), lambda l, j: (0, 0)),
                  pl.BlockSpec((None, D, tn), lambda l, j: (l, 0, j)),
                  pl.BlockSpec((None, 1, tn), lambda l, j: (l, 0, j))],
        out_specs=pl.BlockSpec((None, B, tn), lambda l, j: (l, 0, j)),
        compiler_params=_params("parallel", "parallel"),
    )(c_all, ada_w, ada_b_cols.reshape(L, 1, cols))


def _ada_grad(c_t, dmod_cols, name):
    L, B, cols = dmod_cols.shape
    D = c_t.shape[0]
    tn = 512 if cols % 512 == 0 else cols

    def body(c_ref, d_ref, o_ref):
        cv = c_ref[...]
        cond = cv * jax.nn.sigmoid(cv)
        o_ref[...] = _dot3(cond, d_ref[...], NN)

    return pl.pallas_call(
        body, name=name,
        out_shape=jax.ShapeDtypeStruct((L, D, cols), F32),
        grid=(L, cols // tn),
        in_specs=[pl.BlockSpec((D, B), lambda l, j: (0, 0)),
                  pl.BlockSpec((None, B, tn), lambda l, j: (l, 0, j))],
        out_specs=pl.BlockSpec((None, D, tn), lambda l, j: (l, 0, j)),
        compiler_params=_params("parallel", "parallel"),
    )(c_t, dmod_cols)


def _row_spec(ts, width):
    return pl.BlockSpec((ts, width), lambda i: (i, 0))


def _vec_spec(width):
    return pl.BlockSpec((1, width), lambda i: (0, 0))


def _pre_fwd(x, pre_g, scale, shift, name):
    S, D = x.shape
    ts = min(256, S)

    def body(x_ref, g_ref, sc_ref, sh_ref, h_ref):
        xv = x_ref[...]
        r = lax.rsqrt(jnp.mean(xv * xv, axis=-1, keepdims=True) + NORM_EPS)
        h_ref[...] = ((xv * r * g_ref[...]) * (1.0 + sc_ref[...]) + sh_ref[...]).astype(BF16)

    return pl.pallas_call(
        body, name=name, out_shape=jax.ShapeDtypeStruct((S, D), BF16), grid=(S // ts,),
        in_specs=[_row_spec(ts, D), _vec_spec(D), _vec_spec(D), _vec_spec(D)],
        out_specs=_row_spec(ts, D), compiler_params=_params("parallel"),
    )(x, pre_g, scale, shift)


def _pre_bwd(dh, x, dx_res, pre_g, scale, name):
    S, D = x.shape
    ts = min(256, S)

    def body(dh_ref, x_ref, dr_ref, g_ref, sc_ref, dx_ref, dsh_ref, dsc_ref, dg_ref):
        @pl.when(pl.program_id(0) == 0)
        def _():
            dsh_ref[...] = jnp.zeros_like(dsh_ref)
            dsc_ref[...] = jnp.zeros_like(dsc_ref)
            dg_ref[...] = jnp.zeros_like(dg_ref)

        dh = dh_ref[...]
        xv = x_ref[...]
        g = g_ref[...]
        one_sc = 1.0 + sc_ref[...]
        r = lax.rsqrt(jnp.mean(xv * xv, axis=-1, keepdims=True) + NORM_EPS)
        xn = xv * r
        dsh_ref[...] += jnp.sum(dh, axis=0, keepdims=True)
        dsc_ref[...] += jnp.sum(dh * (xn * g), axis=0, keepdims=True)
        dg_ref[...] += jnp.sum(dh * one_sc * xn, axis=0, keepdims=True)
        dxn = dh * one_sc * g
        dx_ref[...] = dr_ref[...] + r * (dxn - xn * jnp.mean(dxn * xn, axis=-1, keepdims=True))

    vec = jax.ShapeDtypeStruct((1, D), F32)
    return pl.pallas_call(
        body, name=name, out_shape=(jax.ShapeDtypeStruct((S, D), F32), vec, vec, vec), grid=(S // ts,),
        in_specs=[_row_spec(ts, D), _row_spec(ts, D), _row_spec(ts, D), _vec_spec(D), _vec_spec(D)],
        out_specs=(_row_spec(ts, D), _vec_spec(D), _vec_spec(D), _vec_spec(D)),
        compiler_params=_params("arbitrary"),
    )(dh, x, dx_res, pre_g, scale)


def _post_fwd(x, y, gate, post_g, name):
    S, D = x.shape
    ts = min(256, S)

    def body(x_ref, y_ref, gt_ref, g_ref, o_ref):
        yv = y_ref[...]
        r = lax.rsqrt(jnp.mean(yv * yv, axis=-1, keepdims=True) + NORM_EPS)
        o_ref[...] = x_ref[...] + gt_ref[...] * (yv * r * g_ref[...])

    return pl.pallas_call(
        body, name=name, out_shape=jax.ShapeDtypeStruct((S, D), F32), grid=(S // ts,),
        in_specs=[_row_spec(ts, D), _row_spec(ts, D), _vec_spec(D), _vec_spec(D)],
        out_specs=_row_spec(ts, D), compiler_params=_params("parallel"),
    )(x, y, gate, post_g)


def _post_bwd(dx, y, gate, post_g, name):
    S, D = y.shape
    ts = min(256, S)

    def body(dx_ref, y_ref, gt_ref, g_ref, dy_ref, dgt_ref, dg_ref):
        @pl.when(pl.program_id(0) == 0)
        def _():
            dgt_ref[...] = jnp.zeros_like(dgt_ref)
            dg_ref[...] = jnp.zeros_like(dg_ref)

        dxv = dx_ref[...]
        yv = y_ref[...]
        g = g_ref[...]
        gt = gt_ref[...]
        r = lax.rsqrt(jnp.mean(yv * yv, axis=-1, keepdims=True) + NORM_EPS)
        yn = yv * r
        dgt_ref[...] += jnp.sum(dxv * (yn * g), axis=0, keepdims=True)
        dg_ref[...] += jnp.sum(dxv * gt * yn, axis=0, keepdims=True)
        dyn = dxv * gt * g
        dy_ref[...] = (r * (dyn - yn * jnp.mean(dyn * yn, axis=-1, keepdims=True))).astype(BF16)

    vec = jax.ShapeDtypeStruct((1, D), F32)
    return pl.pallas_call(
        body, name=name, out_shape=(jax.ShapeDtypeStruct((S, D), BF16), vec, vec), grid=(S // ts,),
        in_specs=[_row_spec(ts, D), _row_spec(ts, D), _vec_spec(D), _vec_spec(D)],
        out_specs=(_row_spec(ts, D), _vec_spec(D), _vec_spec(D)),
        compiler_params=_params("arbitrary"),
    )(dx, y, gate, post_g)


def _loss_grad(xf, target, name):
    S, D = xf.shape
    ts = min(256, S)

    def body(x_ref, t_ref, dx_ref, l_ref):
        @pl.when(pl.program_id(0) == 0)
        def _():
            l_ref[...] = jnp.zeros_like(l_ref)

        e = x_ref[...] - t_ref[...]
        dx_ref[...] = e * (1.0 / D)
        row = jnp.sum(e * e, axis=1, keepdims=True) * (1.0 / D)
        l_ref[...] += 0.5 * jnp.sum(row, axis=0, keepdims=True)

    return pl.pallas_call(
        body, name=name,
        out_shape=(jax.ShapeDtypeStruct((S, D), F32), jax.ShapeDtypeStruct((1, 1), F32)), grid=(S // ts,),
        in_specs=[_row_spec(ts, D), _row_spec(ts, D)],
        out_specs=(_row_spec(ts, D), pl.BlockSpec((1, 1), lambda i: (0, 0))),
        compiler_params=_params("arbitrary"),
    )(xf, target)


def _chunk_mask(transposed=False):
    row = lax.broadcasted_iota(jnp.int32, (SGU_BLOCK, SGU_BLOCK), 0) // CHUNK
    col = lax.broadcasted_iota(jnp.int32, (SGU_BLOCK, SGU_BLOCK), 1) // CHUNK
    return (row <= col) if transposed else (col <= row)


def _sgu_gate_fwd(uvz, norm_g, w_s, b_bc, name):
    S, E3 = uvz.shape
    E = E3 // 3
    T = SGU_BLOCK
    gd = E // SGU_GROUPS

    def body(uvz_ref, ng_ref, ws_ref, bb_ref, y_ref, v_scr):
        gv = _gelu(uvz_ref[:, E:2 * E])
        mu = jnp.mean(gv, axis=-1, keepdims=True)
        xc = gv - mu
        rstd = lax.rsqrt(jnp.mean(xc * xc, axis=-1, keepdims=True) + NORM_EPS)
        v_scr[...] = (xc * rstd * ng_ref[...]).astype(BF16)
        mask = _chunk_mask()
        for g in range(SGU_GROUPS):
            sl = slice(g * gd, (g + 1) * gd)
            wg = jnp.where(mask, ws_ref[g], 0.0).astype(BF16)
            vm = lax.dot_general(wg, v_scr[:, sl], NN, preferred_element_type=F32)
            vm = vm + jnp.tile(bb_ref[g], (1, gd // LANE))
            z = uvz_ref[:, 2 * E + g * gd:2 * E + (g + 1) * gd]
            y_ref[:, sl] = (_gelu(uvz_ref[:, sl]) * vm * (z * jax.nn.sigmoid(z))).astype(BF16)

    return pl.pallas_call(
        body, name=name, out_shape=jax.ShapeDtypeStruct((S, E), BF16), grid=(S // T,),
        in_specs=[_row_spec(T, E3), _vec_spec(E),
                  pl.BlockSpec((SGU_GROUPS, T, T), lambda i: (0, 0, 0)),
                  pl.BlockSpec((SGU_GROUPS, T, LANE), lambda i: (0, 0, 0))],
        out_specs=_row_spec(T, E),
        scratch_shapes=[pltpu.VMEM((T, E), BF16)],
        compiler_params=_params("parallel"),
    )(uvz, norm_g, w_s, b_bc)


def _sgu_gate_bwd(uvz, dyv, norm_g, w_s, w_sT, b_bc, name):
    S, E3 = uvz.shape
    E = E3 // 3
    T = SGU_BLOCK
    gd = E // SGU_GROUPS

    def body(uvz_ref, dyv_ref, ng_ref, ws_ref, wst_ref, bb_ref,
             d_ref, dng_ref, dws_ref, dbs_ref, vhat_scr, dv_scr):
        @pl.when(pl.program_id(0) == 0)
        def _():
            dng_ref[...] = jnp.zeros_like(dng_ref)
            dws_ref[...] = jnp.zeros_like(dws_ref)
            dbs_ref[...] = jnp.zeros_like(dbs_ref)

        gv = _gelu(uvz_ref[:, E:2 * E])
        mu = jnp.mean(gv, axis=-1, keepdims=True)
        xc = gv - mu
        rstd = lax.rsqrt(jnp.mean(xc * xc, axis=-1, keepdims=True) + NORM_EPS)
        vhat_scr[...] = xc * rstd
        mask = _chunk_mask()
        mask_t = _chunk_mask(transposed=True)
        for g in range(SGU_GROUPS):
            sl = slice(g * gd, (g + 1) * gd)
            u_pre = uvz_ref[:, sl]
            z = uvz_ref[:, 2 * E + g * gd:2 * E + (g + 1) * gd]
            dy = dyv_ref[:, sl]
            u = _gelu(u_pre)
            sig = jax.nn.sigmoid(z)
            sz = z * sig
            vg = (vhat_scr[:, sl] * ng_ref[:, sl]).astype(BF16)
            wg = jnp.where(mask, ws_ref[g], 0.0).astype(BF16)
            vm = lax.dot_general(wg, vg, NN, preferred_element_type=F32)
            vm = vm + jnp.tile(bb_ref[g], (1, gd // LANE))
            dy_u = dy * u
            d_ref[:, sl] = (dy * vm * sz * _gelu_grad(u_pre)).astype(BF16)
            d_ref[:, 2 * E + g * gd:2 * E + (g + 1) * gd] = (
                dy_u * vm * (sig * (1.0 + z * (1.0 - sig)))).astype(BF16)
            dvm = dy_u * sz
            dvm_b = dvm.astype(BF16)
            dws_ref[g] += jnp.where(mask, lax.dot_general(dvm_b, vg, NT, preferred_element_type=F32), 0.0)
            dbs_ref[g] += jnp.broadcast_to(jnp.sum(dvm, axis=1, keepdims=True), (T, LANE))
            wgt = jnp.where(mask_t, wst_ref[g], 0.0).astype(BF16)
            dv_scr[:, sl] = lax.dot_general(wgt, dvm_b, NN, preferred_element_type=F32)
        dv = dv_scr[...]
        vhat = vhat_scr[...]
        dng_ref[...] += jnp.sum(dv * vhat, axis=0, keepdims=True)
        dvh = dv * ng_ref[...]
        dgv = rstd * (dvh - jnp.mean(dvh, axis=-1, keepdims=True)
                      - vhat * jnp.mean(dvh * vhat, axis=-1, keepdims=True))
        d_ref[:, E:2 * E] = (dgv * _gelu_grad(uvz_ref[:, E:2 * E])).astype(BF16)

    wspec = pl.BlockSpec((SGU_GROUPS, T, T), lambda i: (0, 0, 0))
    bspec = pl.BlockSpec((SGU_GROUPS, T, LANE), lambda i: (0, 0, 0))
    return pl.pallas_call(
        body, name=name,
        out_shape=(jax.ShapeDtypeStruct((S, E3), BF16), jax.ShapeDtypeStruct((1, E), F32),
                   jax.ShapeDtypeStruct((SGU_GROUPS, T, T), F32),
                   jax.ShapeDtypeStruct((SGU_GROUPS, T, LANE), F32)),
        grid=(S // T,),
        in_specs=[_row_spec(T, E3), _row_spec(T, E), _vec_spec(E), wspec, wspec, bspec],
        out_specs=(_row_spec(T, E3), _vec_spec(E), wspec, bspec),
        scratch_shapes=[pltpu.VMEM((T, E), F32), pltpu.VMEM((T, E), F32)],
        compiler_params=_params("arbitrary"),
    )(uvz, dyv, norm_g, w_s, w_sT, b_bc)


MLA_WIDTH = HEADS * VDIM
P_LATENT = Q_RANK + KV_RANK + ROPE
P_WIDTH = P_LATENT + MLA_WIDTH


def _swap_halves(v):
    lane = lax.broadcasted_iota(jnp.int32, v.shape, 1)
    return jnp.where(lane % ROPE < ROPE // 2, pltpu.roll(v, LANE - ROPE // 2, 1), pltpu.roll(v, ROPE // 2, 1))


def _low_lanes(rows):
    return lax.broadcasted_iota(jnp.int32, (rows, LANE), 1) < ROPE


def _latent_tiles(ref):
    return [ref[:, t * LANE:(t + 1) * LANE] for t in range(P_LATENT // LANE)]


def _split_latents(tiles, low):
    cq = jnp.concatenate(tiles[0:3] + [jnp.where(low, tiles[3], 0.0)], axis=1)
    rolled = [pltpu.roll(t, ROPE, 1) for t in tiles[3:8]]
    ckv = jnp.concatenate([jnp.where(low, rolled[t], rolled[t + 1]) for t in range(4)], axis=1)
    kr = jnp.where(low, rolled[4], 0.0)
    return cq, ckv, kr


def _mla_mid_fwd(p, qg, kvg, name):
    S, PW = p.shape
    ts = min(256, S)

    def body(p_ref, qg_ref, kvg_ref, cqn_ref, ckvn_ref):
        cq, ckv, _ = _split_latents(_latent_tiles(p_ref), _low_lanes(ts))
        r = lax.rsqrt(jnp.sum(cq * cq, axis=-1, keepdims=True) * (1.0 / Q_RANK) + NORM_EPS)
        cqn_ref[...] = (cq * r * qg_ref[...]).astype(BF16)
        r2 = lax.rsqrt(jnp.mean(ckv * ckv, axis=-1, keepdims=True) + NORM_EPS)
        ckvn_ref[...] = (ckv * r2 * kvg_ref[...]).astype(BF16)

    return pl.pallas_call(
        body, name=name,
        out_shape=(jax.ShapeDtypeStruct((S, Q_RANK_PAD), BF16), jax.ShapeDtypeStruct((S, KV_RANK), BF16)),
        grid=(S // ts,),
        in_specs=[_row_spec(ts, P_LATENT), _vec_spec(Q_RANK_PAD), _vec_spec(KV_RANK)],
        out_specs=(_row_spec(ts, Q_RANK_PAD), _row_spec(ts, KV_RANK)),
        compiler_params=_params("parallel"),
    )(p, qg, kvg)


def _mla_pack(q, kv, p, cos_t, sin_t, name):
    S = q.shape[0]
    ts = min(256, S)
    pair_w = 2 * (NOPE + ROPE)
    head_w = NOPE + VDIM

    def body(q_ref, kv_ref, kr_ref, cos_ref, sin_ref, qo_ref, ko_ref, vo_ref):
        cosv = cos_ref[...]
        sinv = sin_ref[...]
        low = _low_lanes(ts)
        kr = jnp.where(low, pltpu.roll(kr_ref[...], ROPE, 1), 0.0)
        kr = (kr * cosv + _swap_halves(kr) * sinv).astype(BF16)
        for pair in range(HEADS // 2):
            t0, t1, t2 = (q_ref[:, pair * pair_w + t * LANE:pair * pair_w + (t + 1) * LANE] for t in range(3))
            nope_b = jnp.where(low, pltpu.roll(t1, ROPE, 1), pltpu.roll(t2, ROPE, 1))
            ropes = jnp.where(low, t1, t2)
            roped = ropes * cosv + _swap_halves(ropes) * sinv
            qo_ref[2 * pair, :, 0:NOPE] = t0.astype(BF16)
            qo_ref[2 * pair, :, NOPE:QK_PAD] = jnp.where(low, roped, 0.0).astype(BF16)
            qo_ref[2 * pair + 1, :, 0:NOPE] = nope_b.astype(BF16)
            qo_ref[2 * pair + 1, :, NOPE:QK_PAD] = jnp.where(low, pltpu.roll(roped, ROPE, 1), 0.0).astype(BF16)
        for h in range(HEADS):
            ko_ref[h, :, 0:NOPE] = kv_ref[:, h * head_w:h * head_w + NOPE].astype(BF16)
            ko_ref[h, :, NOPE:QK_PAD] = kr
            vo_ref[h] = kv_ref[:, h * head_w + NOPE:(h + 1) * head_w].astype(BF16)

    return pl.pallas_call(
        body, name=name,
        out_shape=(jax.ShapeDtypeStruct((HEADS, S, QK_PAD), BF16), jax.ShapeDtypeStruct((HEADS, S, QK_PAD), BF16),
                   jax.ShapeDtypeStruct((HEADS, S, VDIM), BF16)),
        grid=(S // ts,),
        in_specs=[_row_spec(ts, q.shape[1]), _row_spec(ts, kv.shape[1]),
                  pl.BlockSpec((ts, LANE), lambda i: (i, P_LATENT // LANE - 1)),
                  _row_spec(ts, LANE), _row_spec(ts, LANE)],
        out_specs=(pl.BlockSpec((HEADS, ts, QK_PAD), lambda i: (0, i, 0)),
                   pl.BlockSpec((HEADS, ts, QK_PAD), lambda i: (0, i, 0)),
                   pl.BlockSpec((HEADS, ts, VDIM), lambda i: (0, i, 0))),
        compiler_params=_params("parallel"),
    )(q, kv, p, cos_t, sin_t)


def _mla_unpack(dQ, dK, dV, cos_t, sin_t, name):
    S = dQ.shape[1]
    ts = min(256, S)
    pair_w = 2 * (NOPE + ROPE)
    head_w = NOPE + VDIM

    def body(dq_ref, dk_ref, dv_ref, cos_ref, sin_ref, q_ref, kv_ref, kr_ref):
        cosv = cos_ref[...]
        sinv = sin_ref[...]
        low = _low_lanes(ts)
        for pair in range(HEADS // 2):
            blk = dq_ref[2 * pair, :, NOPE:QK_PAD] + pltpu.roll(dq_ref[2 * pair + 1, :, NOPE:QK_PAD], ROPE, 1)
            ropes = blk * cosv - _swap_halves(blk) * sinv
            nope_b = pltpu.roll(dq_ref[2 * pair + 1, :, 0:NOPE], ROPE, 1)
            q_ref[:, pair * pair_w:pair * pair_w + LANE] = dq_ref[2 * pair, :, 0:NOPE].astype(BF16)
            q_ref[:, pair * pair_w + LANE:pair * pair_w + 2 * LANE] = jnp.where(low, ropes, nope_b).astype(BF16)
            q_ref[:, pair * pair_w + 2 * LANE:(pair + 1) * pair_w] = jnp.where(low, nope_b, ropes).astype(BF16)
        dkr = dk_ref[0, :, NOPE:QK_PAD]
        for h in range(1, HEADS):
            dkr = dkr + dk_ref[h, :, NOPE:QK_PAD]
        kr_ref[...] = dkr * cosv - _swap_halves(dkr) * sinv
        for h in range(HEADS):
            kv_ref[:, h * head_w:h * head_w + NOPE] = dk_ref[h, :, 0:NOPE].astype(BF16)
            kv_ref[:, h * head_w + NOPE:(h + 1) * head_w] = dv_ref[h].astype(BF16)

    return pl.pallas_call(
        body, name=name,
        out_shape=(jax.ShapeDtypeStruct((S, HEADS * (NOPE + ROPE)), BF16),
                   jax.ShapeDtypeStruct((S, HEADS * (NOPE + VDIM)), BF16),
                   jax.ShapeDtypeStruct((S, LANE), F32)),
        grid=(S // ts,),
        in_specs=[pl.BlockSpec((HEADS, ts, QK_PAD), lambda i: (0, i, 0)),
                  pl.BlockSpec((HEADS, ts, QK_PAD), lambda i: (0, i, 0)),
                  pl.BlockSpec((HEADS, ts, VDIM), lambda i: (0, i, 0)),
                  _row_spec(ts, LANE), _row_spec(ts, LANE)],
        out_specs=(_row_spec(ts, HEADS * (NOPE + ROPE)), _row_spec(ts, HEADS * (NOPE + VDIM)),
                   _row_spec(ts, LANE)),
        compiler_params=_params("parallel"),
    )(dQ, dK, dV, cos_t, sin_t)


def _mla_gate_fwd(o, p, name):
    S, W = o.shape
    ts = min(256, S)
    wb = P_LATENT

    def body(o_ref, z_ref, y_ref):
        z = z_ref[...]
        y_ref[...] = (o_ref[...] * (z * jax.nn.sigmoid(z))).astype(BF16)

    return pl.pallas_call(
        body, name=name, out_shape=jax.ShapeDtypeStruct((S, W), BF16), grid=(S // ts, W // wb),
        in_specs=[pl.BlockSpec((ts, wb), lambda i, j: (i, j)), pl.BlockSpec((ts, wb), lambda i, j: (i, j + 1))],
        out_specs=pl.BlockSpec((ts, wb), lambda i, j: (i, j)), compiler_params=_params("parallel", "parallel"),
    )(o, p)


def _mla_gate_bwd(dyv, p, name):
    S, W = dyv.shape
    ts = min(256, S)
    wb = P_LATENT

    def body(d_ref, z_ref, do_ref):
        z = z_ref[...]
        do_ref[...] = d_ref[...] * (z * jax.nn.sigmoid(z))

    return pl.pallas_call(
        body, name=name, out_shape=jax.ShapeDtypeStruct((S, W), F32), grid=(S // ts, W // wb),
        in_specs=[pl.BlockSpec((ts, wb), lambda i, j: (i, j)), pl.BlockSpec((ts, wb), lambda i, j: (i, j + 1))],
        out_specs=pl.BlockSpec((ts, wb), lambda i, j: (i, j)), compiler_params=_params("parallel", "parallel"),
    )(dyv, p)


def _mla_mid_bwd(p, dcqn, dckvn, dkr, dyv, o, qg, kvg, name):
    S, PW = p.shape
    W = o.shape[1]
    ts = min(256, S)
    nt = Q_RANK_PAD // LANE

    def rms_bwd(xv, dy, g, count):
        r = lax.rsqrt(jnp.sum(xv * xv, axis=-1, keepdims=True) * (1.0 / count) + NORM_EPS)
        xn = xv * r
        dg = jnp.sum(dy * xn, axis=0, keepdims=True)
        dxn = dy * g
        dx = r * (dxn - xn * (jnp.sum(dxn * xn, axis=-1, keepdims=True) * (1.0 / count)))
        return dx, dg

    def body(p_ref, dcq_ref, dckv_ref, dkr_ref, dyv_ref, o_ref, qg_ref, kvg_ref, dp_ref, dqg_ref, dkvg_ref):
        @pl.when(pl.program_id(0) == 0)
        def _():
            dqg_ref[...] = jnp.zeros_like(dqg_ref)
            dkvg_ref[...] = jnp.zeros_like(dkvg_ref)

        low = _low_lanes(ts)
        cq, ckv, _ = _split_latents(_latent_tiles(p_ref), low)
        dcq, dg = rms_bwd(cq, dcq_ref[...], qg_ref[...], Q_RANK)
        dqg_ref[...] += dg
        dckv, dg = rms_bwd(ckv, dckv_ref[...], kvg_ref[...], KV_RANK)
        dkvg_ref[...] += dg
        moved = [pltpu.roll(dckv[:, t * LANE:(t + 1) * LANE], ROPE, 1) for t in range(nt)]
        moved.append(pltpu.roll(dkr_ref[...], ROPE, 1))
        for t in range(nt - 1):
            dp_ref[:, t * LANE:(t + 1) * LANE] = dcq[:, t * LANE:(t + 1) * LANE].astype(BF16)
        dp_ref[:, (nt - 1) * LANE:nt * LANE] = jnp.where(low, dcq[:, (nt - 1) * LANE:nt * LANE], moved[0]).astype(BF16)
        for t in range(nt):
            dp_ref[:, (nt + t) * LANE:(nt + t + 1) * LANE] = jnp.where(low, moved[t], moved[t + 1]).astype(BF16)
        z = p_ref[:, P_LATENT:PW]
        sig = jax.nn.sigmoid(z)
        dp_ref[:, P_LATENT:PW] = (dyv_ref[...] * o_ref[...] * (sig * (1.0 + z * (1.0 - sig)))).astype(BF16)

    return pl.pallas_call(
        body, name=name,
        out_shape=(jax.ShapeDtypeStruct((S, PW), BF16), jax.ShapeDtypeStruct((1, Q_RANK_PAD), F32),
                   jax.ShapeDtypeStruct((1, KV_RANK), F32)),
        grid=(S // ts,),
        in_specs=[_row_spec(ts, PW), _row_spec(ts, Q_RANK_PAD), _row_spec(ts, KV_RANK), _row_spec(ts, LANE),
                  _row_spec(ts, W), _row_spec(ts, W), _vec_spec(Q_RANK_PAD), _vec_spec(KV_RANK)],
        out_specs=(_row_spec(ts, PW), _vec_spec(Q_RANK_PAD), _vec_spec(KV_RANK)),
        compiler_params=_params("arbitrary"),
    )(p, dcqn, dckvn, dkr, dyv, o, qg, kvg)


def _tile_mask(T):
    row = lax.broadcasted_iota(jnp.int32, (T, T), 0) // CHUNK
    col = lax.broadcasted_iota(jnp.int32, (T, T), 1) // CHUNK
    return col <= row


def _attn_fwd(Q, K, V, name):
    H, S, _ = Q.shape
    T = min(512, S)
    n_part = 2 if T % 256 == 0 else 1
    Tq = T // n_part

    def body(q_ref, k_ref, v_ref, o_ref, lse_ref, m_scr, l_scr, acc_scr):
        qi = pl.program_id(1)
        q = q_ref[...]
        m_scr[...] = jnp.full_like(m_scr, -jnp.inf)
        l_scr[...] = jnp.zeros_like(l_scr)
        acc_scr[...] = jnp.zeros_like(acc_scr)

        def tile(j, masked):
            rows = pl.ds(pl.multiple_of(j * T, T), T)
            kt = k_ref[rows, :]
            vt = v_ref[rows, :]
            for part in range(n_part):
                sub = slice(part * Tq, (part + 1) * Tq)
                s = lax.dot_general(q[sub], kt, NT, preferred_element_type=F32) * ATTN_SCALE
                if masked:
                    row = (lax.broadcasted_iota(jnp.int32, (Tq, T), 0) + part * Tq) // CHUNK
                    col = lax.broadcasted_iota(jnp.int32, (Tq, T), 1) // CHUNK
                    s = jnp.where(col <= row, s, -1e30)
                m_prev = m_scr[sub]
                m_new = jnp.maximum(m_prev, jnp.max(s, axis=1, keepdims=True))
                pr = jnp.exp(s - m_new)
                alpha = jnp.exp(m_prev - m_new)
                l_scr[sub] = alpha * l_scr[sub] + jnp.sum(pr, axis=1, keepdims=True)
                acc_scr[sub] = alpha * acc_scr[sub] + lax.dot_general(
                    pr.astype(BF16), vt, NN, preferred_element_type=F32)
                m_scr[sub] = m_new

        def full_tile(j, carry):
            tile(j, False)
            return carry

        lax.fori_loop(0, qi, full_tile, 0)
        tile(qi, True)
        l = l_scr[...]
        o_ref[...] = acc_scr[...] / l
        lse_ref[...] = jnp.broadcast_to(m_scr[...] + jnp.log(l), (T, LANE))

    return pl.pallas_call(
        body, name=name,
        out_shape=(jax.ShapeDtypeStruct((S, H * VDIM), F32), jax.ShapeDtypeStruct((H, S, LANE), F32)),
        grid=(H, S // T),
        in_specs=[pl.BlockSpec((None, T, QK_PAD), lambda h, i: (h, i, 0)),
                  pl.BlockSpec((None, S, QK_PAD), lambda h, i: (h, 0, 0)),
                  pl.BlockSpec((None, S, VDIM), lambda h, i: (h, 0, 0))],
        out_specs=(pl.BlockSpec((T, VDIM), lambda h, i: (i, h)),
                   pl.BlockSpec((None, T, LANE), lambda h, i: (h, i, 0))),
        scratch_shapes=[pltpu.VMEM((T, 1), F32), pltpu.VMEM((T, 1), F32), pltpu.VMEM((T, VDIM), F32)],
        compiler_params=_params("parallel", "arbitrary"),
    )(Q, K, V)


def _attn_bwd(Q, K, V, o, do, lse, name):
    H, S, _ = Q.shape
    T = min(512, S)
    nq = S // T

    def body(q_ref, k_ref, v_ref, o_ref, do_ref, lse_ref, dq_ref, dk_ref, dv_ref, dk_scr, dv_scr):
        ki = pl.program_id(1)

        @pl.when(ki == 0)
        def _():
            dq_ref[...] = jnp.zeros_like(dq_ref)

        dk_scr[...] = jnp.zeros_like(dk_scr)
        dv_scr[...] = jnp.zeros_like(dv_scr)
        k = k_ref[...]
        v = v_ref[...]

        def tile(i, masked):
            rows = pl.ds(pl.multiple_of(i * T, T), T)
            q = q_ref[rows, :]
            do_f = do_ref[rows, :]
            do_b = do_f.astype(BF16)
            delta = jnp.sum(do_f * o_ref[rows, :], axis=1, keepdims=True)
            s = lax.dot_general(q, k, NT, preferred_element_type=F32) * ATTN_SCALE
            pr = jnp.exp(s - lse_ref[rows, 0:1])
            if masked:
                pr = jnp.where(_tile_mask(T), pr, 0.0)
            dv_scr[...] += lax.dot_general(pr.astype(BF16), do_b, TN, preferred_element_type=F32)
            dp = lax.dot_general(do_b, v, NT, preferred_element_type=F32)
            ds = (pr * (dp - delta) * ATTN_SCALE).astype(BF16)
            dk_scr[...] += lax.dot_general(ds, q, TN, preferred_element_type=F32)
            dq_ref[rows, :] += lax.dot_general(ds, k, NN, preferred_element_type=F32)

        def full_tile(i, carry):
            tile(i, False)
            return carry

        tile(ki, True)
        lax.fori_loop(ki + 1, nq, full_tile, 0)
        dk_ref[...] = dk_scr[...]
        dv_ref[...] = dv_scr[...]

    return pl.pallas_call(
        body, name=name,
        out_shape=(jax.ShapeDtypeStruct((H, S, QK_PAD), F32), jax.ShapeDtypeStruct((H, S, QK_PAD), F32),
                   jax.ShapeDtypeStruct((H, S, VDIM), F32)),
        grid=(H, nq),
        in_specs=[pl.BlockSpec((None, S, QK_PAD), lambda h, j: (h, 0, 0)),
                  pl.BlockSpec((None, T, QK_PAD), lambda h, j: (h, j, 0)),
                  pl.BlockSpec((None, T, VDIM), lambda h, j: (h, j, 0)),
                  pl.BlockSpec((S, VDIM), lambda h, j: (0, h)),
                  pl.BlockSpec((S, VDIM), lambda h, j: (0, h)),
                  pl.BlockSpec((None, S, LANE), lambda h, j: (h, 0, 0))],
        out_specs=(pl.BlockSpec((None, S, QK_PAD), lambda h, j: (h, 0, 0)),
                   pl.BlockSpec((None, T, QK_PAD), lambda h, j: (h, j, 0)),
                   pl.BlockSpec((None, T, VDIM), lambda h, j: (h, j, 0))),
        scratch_shapes=[pltpu.VMEM((T, QK_PAD), F32), pltpu.VMEM((T, VDIM), F32)],
        compiler_params=_params("parallel", "arbitrary"),
    )(Q, K, V, o, do, lse)


def _adamw(w, g, m, v, name):
    shape = w.shape
    C = shape[-1]
    R = math.prod(shape[:-1])
    flat = [t.reshape(R, C) for t in (w, g, m, v)]
    tr = _row_tile(R, C * 4)

    def body(w_ref, g_ref, m_ref, v_ref, d_ref, nm_ref, nv_ref):
        gv = g_ref[...]
        m_new = ADAM_B1 * m_ref[...] + (1.0 - ADAM_B1) * gv
        v_new = ADAM_B2 * v_ref[...] + (1.0 - ADAM_B2) * jnp.square(gv)
        m_hat = m_new / (1.0 - ADAM_B1 ** ADAM_STEP)
        v_hat = v_new / (1.0 - ADAM_B2 ** ADAM_STEP)
        d_ref[...] = -ADAM_LR * (m_hat / (jnp.sqrt(v_hat) + ADAM_EPS) + ADAM_WD * w_ref[...])
        nm_ref[...] = m_new
        nv_ref[...] = v_new

    spec = pl.BlockSpec((tr, C), lambda i: (i, 0))
    out = jax.ShapeDtypeStruct((R, C), F32)
    d, nm, nv = pl.pallas_call(
        body, name=name, out_shape=(out, out, out), grid=(R // tr,),
        in_specs=[spec] * 4, out_specs=(spec, spec, spec), compiler_params=_params("parallel"),
    )(*flat)
    return d.reshape(shape), nm.reshape(shape), nv.reshape(shape)


def _sum_slots(r, name):
    n, M, N = r.shape
    tr = _row_tile(M, N * 4 * n, 4 << 20)

    def body(r_ref, o_ref):
        acc = r_ref[0].astype(F32)
        for s in range(1, n):
            acc = acc + r_ref[s].astype(F32)
        o_ref[...] = acc

    return pl.pallas_call(
        body, name=name, out_shape=jax.ShapeDtypeStruct((M, N), F32), grid=(M // tr,),
        in_specs=[pl.BlockSpec((n, tr, N), lambda i: (0, i, 0))],
        out_specs=pl.BlockSpec((tr, N), lambda i: (i, 0)), compiler_params=_params("parallel"),
    )(r)


ANY = pl.BlockSpec(memory_space=pl.ANY)
DMA_CHUNK_BYTES = 1 << 20
DMA_MAX_CHUNKS = 16
PEER_ORDER = (1, 4, 5, 2, 3, 6, 7)


def _position():
    return lax.axis_index("x"), lax.axis_index("y"), lax.axis_index("c")


def _row_chunks(shape, dtype):
    rows, cols = shape
    n = max(1, min(DMA_MAX_CHUNKS, rows * cols * jnp.dtype(dtype).itemsize // DMA_CHUNK_BYTES))
    while n > 1 and (rows % n or (rows // n) % 16):
        n -= 1
    step = rows // n
    return [pl.ds(q * step, step) for q in range(n)]


def _all_gather8(xs, name):
    n = len(xs)

    def body(*refs):
        x_refs, o_refs = refs[:n], refs[n:2 * n]
        send_sems, recv_sems, local_sems = refs[2 * n:]
        x, y, c = _position()
        me, sibling = (x, y, c), (x, y, 1 - c)
        chips = [(1 - x, y), (x, 1 - y), (1 - x, 1 - y)]

        def slot(a, dev, rows):
            return o_refs[a].at[4 * dev[0] + 2 * dev[1] + dev[2], rows]

        def copy(a, k, block, to, rows, from_input=False):
            return pltpu.make_async_remote_copy(
                src_ref=x_refs[a].at[rows] if from_input else slot(a, block, rows), dst_ref=slot(a, block, rows),
                send_sem=send_sems.at[a, k], recv_sem=recv_sems.at[a, k],
                device_id=to, device_id_type=MESH)

        def mine(a, rows):
            return pltpu.make_async_copy(x_refs[a].at[rows], slot(a, me, rows), local_sems.at[a])

        chunks = [_row_chunks(t.shape, t.dtype) for t in xs]
        whole = [pl.ds(0, t.shape[0]) for t in xs]
        for a in range(n):
            for rows in chunks[a]:
                mine(a, rows).start()
        sent = []
        for a in range(n):
            for k, to in enumerate([sibling] + [(*chip, c) for chip in chips]):
                for rows in chunks[a]:
                    copy(a, k, me, to, rows, from_input=True).start()
                sent.append(copy(a, k, me, to, whole[a], from_input=True))
        for a in range(n):
            for j, chip in enumerate(chips):
                copy(a, 1 + j, (*chip, c), me, whole[a]).wait_recv()
                for rows in chunks[a]:
                    copy(a, 4 + j, (*chip, c), sibling, rows).start()
                sent.append(copy(a, 4 + j, (*chip, c), sibling, whole[a]))
        for a in range(n):
            copy(a, 0, sibling, me, whole[a]).wait_recv()
            for j, chip in enumerate(chips):
                copy(a, 4 + j, (*chip, 1 - c), me, whole[a]).wait_recv()
        for cp in sent:
            cp.wait_send()
        for a in range(n):
            mine(a, whole[a]).wait()

    return pl.pallas_call(
        body, name=name,
        out_shape=[jax.ShapeDtypeStruct((8,) + t.shape, t.dtype) for t in xs],
        in_specs=[ANY] * n, out_specs=[ANY] * n,
        scratch_shapes=[pltpu.SemaphoreType.DMA((n, 7)), pltpu.SemaphoreType.DMA((n, 7)),
                        pltpu.SemaphoreType.DMA((n,))],
    )(*xs)


def _exchange8(gs, name):
    n = len(gs)

    def body(*refs):
        g_refs, r_refs = refs[:n], refs[n:2 * n]
        send_sems, recv_sems, local_sems = refs[2 * n:]
        x, y, c = _position()
        my = 4 * x + 2 * y + c

        def mine(a, rows):
            return pltpu.make_async_copy(g_refs[a].at[my, rows], r_refs[a].at[my, rows], local_sems.at[a])

        def copy(a, m, rows):
            px = (1 - x) if m & 4 else x
            py = (1 - y) if m & 2 else y
            pc = (1 - c) if m & 1 else c
            return pltpu.make_async_remote_copy(
                src_ref=g_refs[a].at[4 * px + 2 * py + pc, rows], dst_ref=r_refs[a].at[my, rows],
                send_sem=send_sems.at[a, m - 1], recv_sem=recv_sems.at[a, m - 1],
                device_id=(px, py, pc), device_id_type=MESH)

        chunks = [_row_chunks(t.shape[1:], t.dtype) for t in gs]
        whole = [pl.ds(0, t.shape[1]) for t in gs]
        for a in range(n):
            for rows in chunks[a]:
                mine(a, rows).start()
        for a in range(n):
            for m in PEER_ORDER:
                for rows in chunks[a]:
                    copy(a, m, rows).start()
        for a in range(n):
            for m in PEER_ORDER:
                copy(a, m, whole[a]).wait_recv()
        for a in range(n):
            for m in PEER_ORDER:
                copy(a, m, whole[a]).wait_send()
            mine(a, whole[a]).wait()

    return pl.pallas_call(
        body, name=name,
        out_shape=[jax.ShapeDtypeStruct(t.shape, t.dtype) for t in gs],
        in_specs=[ANY] * n, out_specs=[ANY] * n,
        scratch_shapes=[pltpu.SemaphoreType.DMA((n, 7)), pltpu.SemaphoreType.DMA((n, 7)),
                        pltpu.SemaphoreType.DMA((n,))],
    )(*gs)


def _pair_swap(groups, name):
    xs = [t for grp in groups for t in grp]
    owner = [(gi, li) for gi, grp in enumerate(groups) for li in range(len(grp))]
    n, ng = len(xs), len(groups)

    def body(*refs):
        x_refs, o_refs = refs[:n], refs[n:n + ng]
        send_sems, recv_sems, local_sems = refs[n + ng:]
        x, y, c = _position()

        def mine(a, rows):
            gi, li = owner[a]
            return pltpu.make_async_copy(x_refs[a].at[rows], o_refs[gi].at[li, c, rows], local_sems.at[a])

        def copy(a, rows):
            gi, li = owner[a]
            return pltpu.make_async_remote_copy(
                src_ref=x_refs[a].at[rows], dst_ref=o_refs[gi].at[li, c, rows],
                send_sem=send_sems.at[a], recv_sem=recv_sems.at[a],
                device_id=(x, y, 1 - c), device_id_type=MESH)

        chunks = [_row_chunks(t.shape, t.dtype) for t in xs]
        whole = [pl.ds(0, t.shape[0]) for t in xs]
        for a in range(n):
            for rows in chunks[a]:
                copy(a, rows).start()
            for rows in chunks[a]:
                mine(a, rows).start()
        for a in range(n):
            copy(a, whole[a]).wait_recv()
        for a in range(n):
            copy(a, whole[a]).wait_send()
            mine(a, whole[a]).wait()

    return pl.pallas_call(
        body, name=name,
        out_shape=[jax.ShapeDtypeStruct((len(grp), 2) + grp[0].shape, grp[0].dtype) for grp in groups],
        in_specs=[ANY] * n, out_specs=[ANY] * ng,
        scratch_shapes=[pltpu.SemaphoreType.DMA((n,)), pltpu.SemaphoreType.DMA((n,)),
                        pltpu.SemaphoreType.DMA((n,))],
    )(*xs)


def _pack_rows(parts):
    flat = jnp.concatenate([t.reshape(-1).astype(F32) for t in parts])
    pad = (-flat.shape[0]) % (256 * LANE)
    return jnp.pad(flat, (0, pad)).reshape(-1, LANE)


def _my_half(w2d, ci):
    half = w2d.shape[0] // 2
    return lax.dynamic_slice_in_dim(w2d, ci * half, half, axis=0).astype(BF16)


def _col_view(g):
    _, half, Cs = g.shape
    return g.reshape(4, 2 * half, Cs)


def _row_view(g):
    _, half, C = g.shape
    return g.reshape(8 * half, C)


def _rope_tables(S):
    pos = jnp.arange(S, dtype=F32)
    inv_freq = ROPE_THETA ** (-jnp.arange(0, ROPE, 2, dtype=F32) / ROPE)
    ang = pos[:, None] * inv_freq[None, :]
    cos, sin = jnp.cos(ang), jnp.sin(ang)
    cos_t = jnp.concatenate([cos, cos, cos, cos], axis=1)
    sin_t = jnp.concatenate([-sin, sin, -sin, sin], axis=1)
    return cos_t, sin_t


def kernel(x, c, ada_w, ada_b, pre_g, post_g, sgu_w_in, sgu_norm_g, sgu_w_s, sgu_b_s, sgu_w_out, mla_w_in, mla_q_norm_g, mla_kv_norm_g, mla_w_uq, mla_w_ukv, mla_w_out, loss_target, m_ada_w, m_ada_b, m_pre_g, m_post_g, m_sgu_w_in, m_sgu_norm_g, m_sgu_w_s, m_sgu_b_s, m_sgu_w_out, m_mla_w_in, m_mla_q_norm_g, m_mla_kv_norm_g, m_mla_w_uq, m_mla_w_ukv, m_mla_w_out, v_ada_w, v_ada_b, v_pre_g, v_post_g, v_sgu_w_in, v_sgu_norm_g, v_sgu_w_s, v_sgu_b_s, v_sgu_w_out, v_mla_w_in, v_mla_q_norm_g, v_mla_kv_norm_g, v_mla_w_uq, v_mla_w_ukv, v_mla_w_out):
    S, D = x.shape[1], x.shape[2]
    depth = ada_w.shape[0]
    E = sgu_w_out.shape[1] * 4
    xi, yi, ci = _position()
    chip = 2 * xi + yi
    dev = 4 * xi + 2 * yi + ci
    x0 = x.reshape(S, D)
    target = loss_target.reshape(S, D)

    small = _pack_rows([c, mla_q_norm_g, mla_kv_norm_g])
    mixer_w = [sgu_w_in, sgu_w_out, mla_w_in, mla_w_uq, mla_w_ukv, mla_w_out]
    mine = [_my_half(w[j], ci) for j in range(2) for w in mixer_w]
    gathered = _all_gather8([small] + mine, "gather_weights")
    small_all = gathered[0].reshape(8, -1)
    g_sin, g_sout, g_min, g_uq, g_ukv, g_mout = ([gathered[1 + 6 * j + t] for j in range(2)] for t in range(6))
    qn_w, kvn_w = mla_q_norm_g.shape[1], mla_kv_norm_g.shape[1]
    c_all = small_all[:, :D]
    qn_all = small_all[0::2, D:D + 2 * qn_w].reshape(4, 2, qn_w)
    kvn_all = small_all[0::2, D + 2 * qn_w:D + 2 * qn_w + 2 * kvn_w].reshape(4, 2, kvn_w)
    q_gain = jnp.pad(jnp.transpose(qn_all, (1, 0, 2)).reshape(2, 1, Q_RANK), ((0, 0), (0, 0), (0, Q_RANK_PAD - Q_RANK)))
    kv_gain = jnp.transpose(kvn_all, (1, 0, 2)).reshape(2, 1, KV_RANK)

    w_sin = [_col_view(g) for g in g_sin]
    w_sout = [_row_view(g) for g in g_sout]
    w_min = [_col_view(g) for g in g_min]
    w_uq = [jnp.pad(_col_view(g), ((0, 0), (0, Q_RANK_PAD - Q_RANK), (0, 0))) for g in g_uq]
    w_ukv = [_col_view(g) for g in g_ukv]
    w_mout = [_row_view(g) for g in g_mout]

    cols = ada_w.shape[2]
    ada_b_cols = lax.dynamic_slice_in_dim(ada_b, chip * cols, cols, axis=1)
    c_pad = jnp.pad(c_all, ((0, 8), (0, 0)))
    mod_cols = _ada_mod(c_pad, ada_w, ada_b_cols, "ada_mod")[:, :8]
    mod_all, = _all_gather8([mod_cols.reshape(depth * 8, cols)], "gather_mod")
    mod_all = jnp.transpose(mod_all[0::2].reshape(4, depth, 8, cols), (1, 2, 0, 3)).reshape(depth, 8, 4 * cols)
    mod = lax.dynamic_index_in_dim(mod_all, dev, 1, keepdims=False)
    shift = [mod[i:i + 1, :D] for i in range(depth)]
    scale = [mod[i:i + 1, D:2 * D] for i in range(depth)]
    gate = [mod[i:i + 1, 2 * D:] for i in range(depth)]

    cos_t, sin_t = _rope_tables(S)
    b_bc = jnp.broadcast_to(sgu_b_s[:, :, :, None], sgu_b_s.shape + (LANE,))
    w_sT = jnp.swapaxes(sgu_w_s, 2, 3)

    saved = []
    xs = x0
    for i in range(depth):
        j = i // 2
        tag = f"l{i}"
        h = _pre_fwd(xs, pre_g[i:i + 1], scale[i], shift[i], f"pre_fwd_{tag}")
        if i % 2 == 0:
            uvz = _mm(h, w_sin[j], b_sharded=True, name=f"sgu_in_{tag}")
            y = _sgu_gate_fwd(uvz, sgu_norm_g[j:j + 1], sgu_w_s[j], b_bc[j], f"sgu_gate_fwd_{tag}")
            out = _mm(y, w_sout[j], name=f"sgu_out_{tag}")
            saved.append(dict(x=xs, h=h, uvz=uvz, y=y, out=out))
        else:
            p = _mm(h, w_min[j], b_sharded=True, name=f"mla_in_{tag}")
            cqn, ckvn = _mla_mid_fwd(p, q_gain[j], kv_gain[j], f"mla_mid_fwd_{tag}")
            q = _mm(cqn, w_uq[j], b_sharded=True, name=f"mla_uq_{tag}")
            kv = _mm(ckvn, w_ukv[j], b_sharded=True, name=f"mla_ukv_{tag}")
            Q, K, V = _mla_pack(q, kv, p, cos_t, sin_t, f"mla_pack_{tag}")
            o, lse = _attn_fwd(Q, K, V, f"attn_fwd_{tag}")
            y = _mla_gate_fwd(o, p, f"mla_gate_fwd_{tag}")
            out = _mm(y, w_mout[j], name=f"mla_out_{tag}")
            saved.append(dict(x=xs, h=h, p=p, cqn=cqn, ckvn=ckvn, Q=Q, K=K, V=V, o=o, lse=lse, y=y, out=out))
        xs = _post_fwd(xs, out, gate[i], post_g[i:i + 1], f"post_fwd_{tag}")

    dx, loss_part = _loss_grad(xs, target, "loss")
    loss = lax.psum(loss_part[0, 0], ("x", "y", "c"))

    dmod = [None] * depth
    d_pre_g = [None] * depth
    d_post_g = [None] * depth
    d_sgu = [None] * 2
    d_mla = [None] * 2
    for i in reversed(range(depth)):
        j = i // 2
        tag = f"l{i}"
        sv = saved[i]
        dy, dgate, d_post_g[i] = _post_bwd(dx, sv["out"], gate[i], post_g[i:i + 1], f"post_bwd_{tag}")
        if i % 2 == 0:
            dw_out = _mm(sv["y"], dy, ta=True, out_dtype=BF16, name=f"sgu_out_dw_{tag}")
            dyv = _mm(dy, w_sout[j], tb=True, name=f"sgu_out_dx_{tag}")
            duvz, dng, dws, dbs = _sgu_gate_bwd(sv["uvz"], dyv, sgu_norm_g[j:j + 1], sgu_w_s[j], w_sT[j], b_bc[j],
                                                f"sgu_gate_bwd_{tag}")
            dw_in = _mm(sv["h"], duvz, ta=True, out_sharded=True, out_dtype=BF16, name=f"sgu_in_dw_{tag}")
            dh = _mm(duvz, w_sin[j], tb=True, b_sharded=True, name=f"sgu_in_dx_{tag}")
            d_sgu[j] = dict(w_in=dw_in, w_out=dw_out, norm_g=dng, w_s=dws, b_s=dbs[:, :, 0])
        else:
            dw_out = _mm(sv["y"], dy, ta=True, out_dtype=BF16, name=f"mla_out_dw_{tag}")
            dyv = _mm(dy, w_mout[j], tb=True, name=f"mla_out_dx_{tag}")
            p = sv["p"]
            do = _mla_gate_bwd(dyv, p, f"mla_gate_bwd_{tag}")
            dQ, dK, dV = _attn_bwd(sv["Q"], sv["K"], sv["V"], sv["o"], do, sv["lse"], f"attn_bwd_{tag}")
            dq, dkv, dkr = _mla_unpack(dQ, dK, dV, cos_t, sin_t, f"mla_unpack_{tag}")
            dw_uq = _mm(sv["cqn"], dq, ta=True, out_sharded=True, out_dtype=BF16, name=f"mla_uq_dw_{tag}")
            dcqn = _mm(dq, w_uq[j], tb=True, b_sharded=True, name=f"mla_uq_dx_{tag}")
            dw_ukv = _mm(sv["ckvn"], dkv, ta=True, out_sharded=True, out_dtype=BF16, name=f"mla_ukv_dw_{tag}")
            dckvn = _mm(dkv, w_ukv[j], tb=True, b_sharded=True, name=f"mla_ukv_dx_{tag}")
            dp, dqg, dkvg = _mla_mid_bwd(p, dcqn, dckvn, dkr, dyv, sv["o"], q_gain[j], kv_gain[j], f"mla_mid_bwd_{tag}")
            dw_in = _mm(sv["h"], dp, ta=True, out_sharded=True, out_dtype=BF16, name=f"mla_in_dw_{tag}")
            dh = _mm(dp, w_min[j], tb=True, b_sharded=True, name=f"mla_in_dx_{tag}")
            d_mla[j] = dict(w_in=dw_in, w_uq=dw_uq[:, :Q_RANK], w_ukv=dw_ukv, w_out=dw_out,
                            qg=dqg[0, :Q_RANK], kvg=dkvg[0])
        dx, dshift, dscale, d_pre_g[i] = _pre_bwd(dh, sv["x"], dx, pre_g[i:i + 1], scale[i], f"pre_bwd_{tag}")
        dmod[i] = jnp.concatenate([dshift, dscale, dgate], axis=1)
    grad_x = dx.reshape(x.shape)

    parts = [jnp.concatenate(dmod, axis=0), jnp.concatenate(d_pre_g, axis=0), jnp.concatenate(d_post_g, axis=0),
             jnp.stack([d["norm_g"][0] for d in d_sgu]), jnp.stack([d["w_s"] for d in d_sgu]),
             jnp.stack([d["b_s"] for d in d_sgu]), jnp.stack([d["qg"] for d in d_mla]),
             jnp.stack([d["kvg"] for d in d_mla])]
    sizes = [int(np.prod(t.shape)) for t in parts]
    packed = _pack_rows(parts)
    packed_all, = _all_gather8([packed], "gather_small_grads")
    total = _sum_slots(packed_all, "sum_small_grads").reshape(-1)
    offs = np.concatenate([[0], np.cumsum(sizes)])
    pieces = [total[int(offs[t]):int(offs[t + 1])].reshape(parts[t].shape) for t in range(len(parts))]
    g_ada_b, g_pre_g, g_post_g, g_norm_g, g_w_s, g_b_s, g_qg_full, g_kvg_full = pieces
    g_qg = lax.dynamic_slice_in_dim(g_qg_full, chip * qn_w, qn_w, axis=1)
    g_kvg = lax.dynamic_slice_in_dim(g_kvg_full, chip * kvn_w, kvn_w, axis=1)
    dmod_all = packed_all.reshape(8, -1)[:, :sizes[0]].reshape(8, depth, 3 * D)
    dmod_cols = lax.dynamic_slice_in_dim(jnp.transpose(dmod_all, (1, 0, 2)), chip * cols, cols, axis=2)
    dmod_cols = jnp.pad(dmod_cols, ((0, 0), (0, LANE - 8), (0, 0)))
    g_ada_w = _ada_grad(jnp.pad(c_all.T, ((0, 0), (0, LANE - 8))), dmod_cols, "ada_grad")

    def slots(dw):
        return dw.reshape(8, -1, dw.shape[-1])

    names = ["sgu_w_in", "sgu_w_out", "mla_w_in", "mla_w_uq", "mla_w_ukv", "mla_w_out"]
    big = [slots(d[key]) for d_mix, keys in ((d_sgu, ("w_in", "w_out")), (d_mla, ("w_in", "w_uq", "w_ukv", "w_out")))
           for key in keys for d in d_mix]
    received = _exchange8(big, "exchange_grads")
    reduced = [_sum_slots(r, f"sum_{names[t // 2]}_{t % 2}") for t, r in enumerate(received)]
    swapped = _pair_swap([reduced[2 * t:2 * t + 2] for t in range(6)], "swap_grads")
    g_sgu_w_in, g_sgu_w_out, g_mla_w_in, g_mla_w_uq, g_mla_w_ukv, g_mla_w_out = swapped

    grads = [g_ada_w, g_ada_b, g_pre_g, g_post_g, g_sgu_w_in, g_norm_g, g_w_s, g_b_s, g_sgu_w_out,
             g_mla_w_in, g_qg, g_kvg, g_mla_w_uq, g_mla_w_ukv, g_mla_w_out]
    weights = [ada_w, ada_b, pre_g, post_g, sgu_w_in, sgu_norm_g, sgu_w_s, sgu_b_s, sgu_w_out,
               mla_w_in, mla_q_norm_g, mla_kv_norm_g, mla_w_uq, mla_w_ukv, mla_w_out]
    ms = [m_ada_w, m_ada_b, m_pre_g, m_post_g, m_sgu_w_in, m_sgu_norm_g, m_sgu_w_s, m_sgu_b_s, m_sgu_w_out,
          m_mla_w_in, m_mla_q_norm_g, m_mla_kv_norm_g, m_mla_w_uq, m_mla_w_ukv, m_mla_w_out]
    vs = [v_ada_w, v_ada_b, v_pre_g, v_post_g, v_sgu_w_in, v_sgu_norm_g, v_sgu_w_s, v_sgu_b_s, v_sgu_w_out,
          v_mla_w_in, v_mla_q_norm_g, v_mla_kv_norm_g, v_mla_w_uq, v_mla_w_ukv, v_mla_w_out]
    wnames = ["ada_w", "ada_b", "pre_g", "post_g", "sgu_w_in", "sgu_norm_g", "sgu_w_s", "sgu_b_s", "sgu_w_out",
              "mla_w_in", "mla_q_norm_g", "mla_kv_norm_g", "mla_w_uq", "mla_w_ukv", "mla_w_out"]
    grads = [g.reshape(w.shape) for g, w in zip(grads, weights)]
    deltas, new_m, new_v = [], [], []
    for w, g, m, v, nm in zip(weights, grads, ms, vs, wnames):
        d, a, b = _adamw(w, g, m, v, f"adamw_{nm}")
        deltas.append(d)
        new_m.append(a)
        new_v.append(b)
    return (loss, grad_x, *grads, *deltas, *new_m, *new_v)
```

```python
import functools
import math

import jax
import jax.numpy as jnp
import numpy as np
from jax import lax
from jax.experimental import pallas as pl
from jax.experimental.pallas import tpu as pltpu

F32 = jnp.float32
BF16 = jnp.bfloat16
MESH = pl.DeviceIdType.MESH

NORM_EPS = 1e-6
CHUNK = 64
SGU_BLOCK = 128
SGU_GROUPS = 16
HEADS = 16
NOPE = 128
ROPE = 64
VDIM = 128
QK_PAD = 256
Q_RANK = 448
Q_RANK_PAD = 512
KV_RANK = 512
ROPE_THETA = 10000.0
ATTN_SCALE = (NOPE + ROPE) ** -0.5

ADAM_LR = 0.001
ADAM_B1 = 0.9
ADAM_B2 = 0.999
ADAM_EPS = 1e-08
ADAM_WD = 0.01
ADAM_STEP = 10

LANE = 128
VMEM_LIMIT = 48 * 1024 * 1024

NN = (((1,), (0,)), ((), ()))
NT = (((1,), (1,)), ((), ()))
TN = (((0,), (0,)), ((), ()))


def _params(*sem):
    return pltpu.CompilerParams(dimension_semantics=sem, vmem_limit_bytes=VMEM_LIMIT)


def _row_tile(rows, row_bytes, target_bytes=1 << 20):
    if rows * row_bytes <= target_bytes or rows % 16:
        return rows
    best = 16
    t = 16
    while t <= rows:
        if rows % t == 0 and t * row_bytes <= target_bytes:
            best = t
        t += 16
    return best


def _fit(dim, target):
    if dim <= target:
        return dim
    t = (target // LANE) * LANE
    while t > LANE and dim % t:
        t -= LANE
    return t


def _gelu(x):
    return 0.5 * x * (1.0 + lax.erf(x * 0.7071067811865476))


def _gelu_grad(x):
    return 0.5 * (1.0 + lax.erf(x * 0.7071067811865476)) + x * jnp.exp(-0.5 * x * x) * 0.3989422804014327


def _mm(a, b, *, ta=False, tb=False, b_sharded=False, out_sharded=False, out_dtype=F32,
        tm=1024, tn=1024, tk=2048, name):
    if ta:
        K, M = a.shape
    else:
        M, K = a.shape
    if b_sharded:
        shards, rows, Cs = b.shape
        b_shape = (rows, shards * Cs)
    else:
        b_shape = b.shape
    if tb:
        N, K2 = b_shape
    else:
        K2, N = b_shape
    assert K == K2, (a.shape, b.shape, ta, tb)
    n_lim = Cs if (b_sharded and not tb) else (N // 4 if out_sharded else N)
    k_lim = Cs if (b_sharded and tb) else K
    tm, tn, tk = _fit(M, tm), _fit(n_lim, tn), _fit(k_lim, tk)
    assert M % tm == 0 and n_lim % tn == 0 and k_lim % tk == 0, (M, N, K, tm, tn, tk)
    nk = K // tk
    nb_n = n_lim // tn
    nb_k = k_lim // tk
    dims = (((0 if ta else 1,), (1 if tb else 0,)), ((), ()))

    def body(a_ref, b_ref, o_ref, *scratch):
        prod = lax.dot_general(a_ref[...].astype(BF16), b_ref[...].astype(BF16), dims,
                               preferred_element_type=F32)
        if nk == 1:
            o_ref[...] = prod.astype(out_dtype)
        else:
            acc_ref, = scratch
            k = pl.program_id(2)

            @pl.when(k == 0)
            def _():
                acc_ref[...] = prod

            @pl.when(k > 0)
            def _():
                acc_ref[...] += prod

            @pl.when(k == nk - 1)
            def _():
                o_ref[...] = acc_ref[...].astype(out_dtype)

    a_spec = (pl.BlockSpec((tk, tm), lambda i, j, k: (k, i)) if ta
              else pl.BlockSpec((tm, tk), lambda i, j, k: (i, k)))
    if b_sharded and tb:
        b_spec = pl.BlockSpec((None, tn, tk), lambda i, j, k: (k // nb_k, j, k % nb_k))
    elif b_sharded:
        b_spec = pl.BlockSpec((None, tk, tn), lambda i, j, k: (j // nb_n, k, j % nb_n))
    elif tb:
        b_spec = pl.BlockSpec((tn, tk), lambda i, j, k: (j, k))
    else:
        b_spec = pl.BlockSpec((tk, tn), lambda i, j, k: (k, j))
    if out_sharded:
        out_shape = jax.ShapeDtypeStruct((4, M, N // 4), out_dtype)
        out_spec = pl.BlockSpec((None, tm, tn), lambda i, j, k: (j // nb_n, i, j % nb_n))
    else:
        out_shape = jax.ShapeDtypeStruct((M, N), out_dtype)
        out_spec = pl.BlockSpec((tm, tn), lambda i, j, k: (i, j))
    return pl.pallas_call(
        body, name=name,
        out_shape=out_shape,
        grid=(M // tm, N // tn, nk),
        in_specs=[a_spec, b_spec],
        out_specs=out_spec,
        scratch_shapes=[] if nk == 1 else [pltpu.VMEM((tm, tn), F32)],
        compiler_params=_params("parallel", "parallel", "arbitrary"),
    )(a, b)


def _split_bf16(v):
    hi = v.astype(BF16)
    lo = (v - hi.astype(F32)).astype(BF16)
    return hi, lo


def _dot3(a, b, dims):
    a_hi, a_lo = _split_bf16(a)
    b_hi, b_lo = _split_bf16(b)
    out = lax.dot_general(a_hi, b_hi, dims, preferred_element_type=F32)
    out += lax.dot_general(a_lo, b_hi, dims, preferred_element_type=F32)
    out += lax.dot_general(a_hi, b_lo, dims, preferred_element_type=F32)
    return out


def _ada_mod(c_all, ada_w, ada_b_cols, name):
    L, D, cols = ada_w.shape
    B = c_all.shape[0]
    tn = 512 if cols % 512 == 0 else cols

    def body(c_ref, w_ref, b_ref, o_ref):
        cv = c_ref[...]
        cond = cv * jax.nn.sigmoid(cv)
        o_ref[...] = _dot3(cond, w_ref[...], NN) + b_ref[...]

    return pl.pallas_call(
        body, name=name,
        out_shape=jax.ShapeDtypeStruct((L, B, cols), F32),
        grid=(L, cols // tn),
        in_specs=[pl.BlockSpec((B, D), lambda l, j: (0, 0)),
                  pl.BlockSpec((None, D, tn), lambda l, j: (l, 0, j)),
                  pl.BlockSpec((None, 1, tn), lambda l, j: (l, 0, j))],
        out_specs=pl.BlockSpec((None, B, tn), lambda l, j: (l, 0, j)),
        compiler_params=_params("parallel", "parallel"),
    )(c_all, ada_w, ada_b_cols.reshape(L, 1, cols))


def _ada_grad(c_t, dmod_cols, name):
    L, B, cols = dmod_cols.shape
    D = c_t.shape[0]
    tn = 512 if cols % 512 == 0 else cols

    def body(c_ref, d_ref, o_ref):
        cv = c_ref[...]
        cond = cv * jax.nn.sigmoid(cv)
        o_ref[...] = _dot3(cond, d_ref[...], NN)

    return pl.pallas_call(
        body, name=name,
        out_shape=jax.ShapeDtypeStruct((L, D, cols), F32),
        grid=(L, cols // tn),
        in_specs=[pl.BlockSpec((D, B), lambda l, j: (0, 0)),
                  pl.BlockSpec((None, B, tn), lambda l, j: (l, 0, j))],
        out_specs=pl.BlockSpec((None, D, tn), lambda l, j: (l, 0, j)),
        compiler_params=_params("parallel", "parallel"),
    )(c_t, dmod_cols)


def _row_spec(ts, width):
    return pl.BlockSpec((ts, width), lambda i: (i, 0))


def _vec_spec(width):
    return pl.BlockSpec((1, width), lambda i: (0, 0))


def _pre_fwd(x, pre_g, scale, shift, name):
    S, D = x.shape
    ts = min(256, S)

    def body(x_ref, g_ref, sc_ref, sh_ref, h_ref):
        xv = x_ref[...]
        r = lax.rsqrt(jnp.mean(xv * xv, axis=-1, keepdims=True) + NORM_EPS)
        h_ref[...] = ((xv * r * g_ref[...]) * (1.0 + sc_ref[...]) + sh_ref[...]).astype(BF16)

    return pl.pallas_call(
        body, name=name, out_shape=jax.ShapeDtypeStruct((S, D), BF16), grid=(S // ts,),
        in_specs=[_row_spec(ts, D), _vec_spec(D), _vec_spec(D), _vec_spec(D)],
        out_specs=_row_spec(ts, D), compiler_params=_params("parallel"),
    )(x, pre_g, scale, shift)


def _pre_bwd(dh, x, dx_res, pre_g, scale, name):
    S, D = x.shape
    ts = min(256, S)

    def body(dh_ref, x_ref, dr_ref, g_ref, sc_ref, dx_ref, dsh_ref, dsc_ref, dg_ref):
        @pl.when(pl.program_id(0) == 0)
        def _():
            dsh_ref[...] = jnp.zeros_like(dsh_ref)
            dsc_ref[...] = jnp.zeros_like(dsc_ref)
            dg_ref[...] = jnp.zeros_like(dg_ref)

        dh = dh_ref[...]
        xv = x_ref[...]
        g = g_ref[...]
        one_sc = 1.0 + sc_ref[...]
        r = lax.rsqrt(jnp.mean(xv * xv, axis=-1, keepdims=True) + NORM_EPS)
        xn = xv * r
        dsh_ref[...] += jnp.sum(dh, axis=0, keepdims=True)
        dsc_ref[...] += jnp.sum(dh * (xn * g), axis=0, keepdims=True)
        dg_ref[...] += jnp.sum(dh * one_sc * xn, axis=0, keepdims=True)
        dxn = dh * one_sc * g
        dx_ref[...] = dr_ref[...] + r * (dxn - xn * jnp.mean(dxn * xn, axis=-1, keepdims=True))

    vec = jax.ShapeDtypeStruct((1, D), F32)
    return pl.pallas_call(
        body, name=name, out_shape=(jax.ShapeDtypeStruct((S, D), F32), vec, vec, vec), grid=(S // ts,),
        in_specs=[_row_spec(ts, D), _row_spec(ts, D), _row_spec(ts, D), _vec_spec(D), _vec_spec(D)],
        out_specs=(_row_spec(ts, D), _vec_spec(D), _vec_spec(D), _vec_spec(D)),
        compiler_params=_params("arbitrary"),
    )(dh, x, dx_res, pre_g, scale)


def _post_fwd(x, y, gate, post_g, name):
    S, D = x.shape
    ts = min(256, S)

    def body(x_ref, y_ref, gt_ref, g_ref, o_ref):
        yv = y_ref[...]
        r = lax.rsqrt(jnp.mean(yv * yv, axis=-1, keepdims=True) + NORM_EPS)
        o_ref[...] = x_ref[...] + gt_ref[...] * (yv * r * g_ref[...])

    return pl.pallas_call(
        body, name=name, out_shape=jax.ShapeDtypeStruct((S, D), F32), grid=(S // ts,),
        in_specs=[_row_spec(ts, D), _row_spec(ts, D), _vec_spec(D), _vec_spec(D)],
        out_specs=_row_spec(ts, D), compiler_params=_params("parallel"),
    )(x, y, gate, post_g)


def _post_bwd(dx, y, gate, post_g, name):
    S, D = y.shape
    ts = min(256, S)

    def body(dx_ref, y_ref, gt_ref, g_ref, dy_ref, dgt_ref, dg_ref):
        @pl.when(pl.program_id(0) == 0)
        def _():
            dgt_ref[...] = jnp.zeros_like(dgt_ref)
            dg_ref[...] = jnp.zeros_like(dg_ref)

        dxv = dx_ref[...]
        yv = y_ref[...]
        g = g_ref[...]
        gt = gt_ref[...]
        r = lax.rsqrt(jnp.mean(yv * yv, axis=-1, keepdims=True) + NORM_EPS)
        yn = yv * r
        dgt_ref[...] += jnp.sum(dxv * (yn * g), axis=0, keepdims=True)
        dg_ref[...] += jnp.sum(dxv * gt * yn, axis=0, keepdims=True)
        dyn = dxv * gt * g
        dy_ref[...] = (r * (dyn - yn * jnp.mean(dyn * yn, axis=-1, keepdims=True))).astype(BF16)

    vec = jax.ShapeDtypeStruct((1, D), F32)
    return pl.pallas_call(
        body, name=name, out_shape=(jax.ShapeDtypeStruct((S, D), BF16), vec, vec), grid=(S // ts,),
        in_specs=[_row_spec(ts, D), _row_spec(ts, D), _vec_spec(D), _vec_spec(D)],
        out_specs=(_row_spec(ts, D), _vec_spec(D), _vec_spec(D)),
        compiler_params=_params("arbitrary"),
    )(dx, y, gate, post_g)


def _loss_grad(xf, target, name):
    S, D = xf.shape
    ts = min(256, S)

    def body(x_ref, t_ref, dx_ref, l_ref):
        @pl.when(pl.program_id(0) == 0)
        def _():
            l_ref[...] = jnp.zeros_like(l_ref)

        e = x_ref[...] - t_ref[...]
        dx_ref[...] = e * (1.0 / D)
        row = jnp.sum(e * e, axis=1, keepdims=True) * (1.0 / D)
        l_ref[...] += 0.5 * jnp.sum(row, axis=0, keepdims=True)

    return pl.pallas_call(
        body, name=name,
        out_shape=(jax.ShapeDtypeStruct((S, D), F32), jax.ShapeDtypeStruct((1, 1), F32)), grid=(S // ts,),
        in_specs=[_row_spec(ts, D), _row_spec(ts, D)],
        out_specs=(_row_spec(ts, D), pl.BlockSpec((1, 1), lambda i: (0, 0))),
        compiler_params=_params("arbitrary"),
    )(xf, target)


def _chunk_mask(transposed=False):
    row = lax.broadcasted_iota(jnp.int32, (SGU_BLOCK, SGU_BLOCK), 0) // CHUNK
    col = lax.broadcasted_iota(jnp.int32, (SGU_BLOCK, SGU_BLOCK), 1) // CHUNK
    return (row <= col) if transposed else (col <= row)


def _sgu_gate_fwd(uvz, norm_g, w_s, b_bc, name):
    S, E3 = uvz.shape
    E = E3 // 3
    T = SGU_BLOCK
    gd = E // SGU_GROUPS

    def body(uvz_ref, ng_ref, ws_ref, bb_ref, y_ref, v_scr):
        gv = _gelu(uvz_ref[:, E:2 * E])
        mu = jnp.mean(gv, axis=-1, keepdims=True)
        xc = gv - mu
        rstd = lax.rsqrt(jnp.mean(xc * xc, axis=-1, keepdims=True) + NORM_EPS)
        v_scr[...] = (xc * rstd * ng_ref[...]).astype(BF16)
        mask = _chunk_mask()
        for g in range(SGU_GROUPS):
            sl = slice(g * gd, (g + 1) * gd)
            wg = jnp.where(mask, ws_ref[g], 0.0).astype(BF16)
            vm = lax.dot_general(wg, v_scr[:, sl], NN, preferred_element_type=F32)
            vm = vm + jnp.tile(bb_ref[g], (1, gd // LANE))
            z = uvz_ref[:, 2 * E + g * gd:2 * E + (g + 1) * gd]
            y_ref[:, sl] = (_gelu(uvz_ref[:, sl]) * vm * (z * jax.nn.sigmoid(z))).astype(BF16)

    return pl.pallas_call(
        body, name=name, out_shape=jax.ShapeDtypeStruct((S, E), BF16), grid=(S // T,),
        in_specs=[_row_spec(T, E3), _vec_spec(E),
                  pl.BlockSpec((SGU_GROUPS, T, T), lambda i: (0, 0, 0)),
                  pl.BlockSpec((SGU_GROUPS, T, LANE), lambda i: (0, 0, 0))],
        out_specs=_row_spec(T, E),
        scratch_shapes=[pltpu.VMEM((T, E), BF16)],
        compiler_params=_params("parallel"),
    )(uvz, norm_g, w_s, b_bc)


def _sgu_gate_bwd(uvz, dyv, norm_g, w_s, w_sT, b_bc, name):
    S, E3 = uvz.shape
    E = E3 // 3
    T = SGU_BLOCK
    gd = E // SGU_GROUPS

    def body(uvz_ref, dyv_ref, ng_ref, ws_ref, wst_ref, bb_ref,
             d_ref, dng_ref, dws_ref, dbs_ref, vhat_scr, dv_scr):
        @pl.when(pl.program_id(0) == 0)
        def _():
            dng_ref[...] = jnp.zeros_like(dng_ref)
            dws_ref[...] = jnp.zeros_like(dws_ref)
            dbs_ref[...] = jnp.zeros_like(dbs_ref)

        gv = _gelu(uvz_ref[:, E:2 * E])
        mu = jnp.mean(gv, axis=-1, keepdims=True)
        xc = gv - mu
        rstd = lax.rsqrt(jnp.mean(xc * xc, axis=-1, keepdims=True) + NORM_EPS)
        vhat_scr[...] = xc * rstd
        mask = _chunk_mask()
        mask_t = _chunk_mask(transposed=True)
        for g in range(SGU_GROUPS):
            sl = slice(g * gd, (g + 1) * gd)
            u_pre = uvz_ref[:, sl]
            z = uvz_ref[:, 2 * E + g * gd:2 * E + (g + 1) * gd]
            dy = dyv_ref[:, sl]
            u = _gelu(u_pre)
            sig = jax.nn.sigmoid(z)
            sz = z * sig
            vg = (vhat_scr[:, sl] * ng_ref[:, sl]).astype(BF16)
            wg = jnp.where(mask, ws_ref[g], 0.0).astype(BF16)
            vm = lax.dot_general(wg, vg, NN, preferred_element_type=F32)
            vm = vm + jnp.tile(bb_ref[g], (1, gd // LANE))
            dy_u = dy * u
            d_ref[:, sl] = (dy * vm * sz * _gelu_grad(u_pre)).astype(BF16)
            d_ref[:, 2 * E + g * gd:2 * E + (g + 1) * gd] = (
                dy_u * vm * (sig * (1.0 + z * (1.0 - sig)))).astype(BF16)
            dvm = dy_u * sz
            dvm_b = dvm.astype(BF16)
            dws_ref[g] += jnp.where(mask, lax.dot_general(dvm_b, vg, NT, preferred_element_type=F32), 0.0)
            dbs_ref[g] += jnp.broadcast_to(jnp.sum(dvm, axis=1, keepdims=True), (T, LANE))
            wgt = jnp.where(mask_t, wst_ref[g], 0.0).astype(BF16)
            dv_scr[:, sl] = lax.dot_general(wgt, dvm_b, NN, preferred_element_type=F32)
        dv = dv_scr[...]
        vhat = vhat_scr[...]
        dng_ref[...] += jnp.sum(dv * vhat, axis=0, keepdims=True)
        dvh = dv * ng_ref[...]
        dgv = rstd * (dvh - jnp.mean(dvh, axis=-1, keepdims=True)
                      - vhat * jnp.mean(dvh * vhat, axis=-1, keepdims=True))
        d_ref[:, E:2 * E] = (dgv * _gelu_grad(uvz_ref[:, E:2 * E])).astype(BF16)

    wspec = pl.BlockSpec((SGU_GROUPS, T, T), lambda i: (0, 0, 0))
    bspec = pl.BlockSpec((SGU_GROUPS, T, LANE), lambda i: (0, 0, 0))
    return pl.pallas_call(
        body, name=name,
        out_shape=(jax.ShapeDtypeStruct((S, E3), BF16), jax.ShapeDtypeStruct((1, E), F32),
                   jax.ShapeDtypeStruct((SGU_GROUPS, T, T), F32),
                   jax.ShapeDtypeStruct((SGU_GROUPS, T, LANE), F32)),
        grid=(S // T,),
        in_specs=[_row_spec(T, E3), _row_spec(T, E), _vec_spec(E), wspec, wspec, bspec],
        out_specs=(_row_spec(T, E3), _vec_spec(E), wspec, bspec),
        scratch_shapes=[pltpu.VMEM((T, E), F32), pltpu.VMEM((T, E), F32)],
        compiler_params=_params("arbitrary"),
    )(uvz, dyv, norm_g, w_s, w_sT, b_bc)


MLA_WIDTH = HEADS * VDIM
P_LATENT = Q_RANK + KV_RANK + ROPE
P_WIDTH = P_LATENT + MLA_WIDTH


def _swap_halves(v):
    lane = lax.broadcasted_iota(jnp.int32, v.shape, 1)
    return jnp.where(lane % ROPE < ROPE // 2, pltpu.roll(v, LANE - ROPE // 2, 1), pltpu.roll(v, ROPE // 2, 1))


def _low_lanes(rows):
    return lax.broadcasted_iota(jnp.int32, (rows, LANE), 1) < ROPE


def _latent_tiles(ref):
    return [ref[:, t * LANE:(t + 1) * LANE] for t in range(P_LATENT // LANE)]


def _split_latents(tiles, low):
    cq = jnp.concatenate(tiles[0:3] + [jnp.where(low, tiles[3], 0.0)], axis=1)
    rolled = [pltpu.roll(t, ROPE, 1) for t in tiles[3:8]]
    ckv = jnp.concatenate([jnp.where(low, rolled[t], rolled[t + 1]) for t in range(4)], axis=1)
    kr = jnp.where(low, rolled[4], 0.0)
    return cq, ckv, kr


def _mla_mid_fwd(p, qg, kvg, name):
    S, PW = p.shape
    ts = min(256, S)

    def body(p_ref, qg_ref, kvg_ref, cqn_ref, ckvn_ref):
        cq, ckv, _ = _split_latents(_latent_tiles(p_ref), _low_lanes(ts))
        r = lax.rsqrt(jnp.sum(cq * cq, axis=-1, keepdims=True) * (1.0 / Q_RANK) + NORM_EPS)
        cqn_ref[...] = (cq * r * qg_ref[...]).astype(BF16)
        r2 = lax.rsqrt(jnp.mean(ckv * ckv, axis=-1, keepdims=True) + NORM_EPS)
        ckvn_ref[...] = (ckv * r2 * kvg_ref[...]).astype(BF16)

    return pl.pallas_call(
        body, name=name,
        out_shape=(jax.ShapeDtypeStruct((S, Q_RANK_PAD), BF16), jax.ShapeDtypeStruct((S, KV_RANK), BF16)),
        grid=(S // ts,),
        in_specs=[_row_spec(ts, P_LATENT), _vec_spec(Q_RANK_PAD), _vec_spec(KV_RANK)],
        out_specs=(_row_spec(ts, Q_RANK_PAD), _row_spec(ts, KV_RANK)),
        compiler_params=_params("parallel"),
    )(p, qg, kvg)


def _mla_pack(q, kv, p, cos_t, sin_t, name):
    S = q.shape[0]
    ts = min(256, S)
    pair_w = 2 * (NOPE + ROPE)
    head_w = NOPE + VDIM

    def body(q_ref, kv_ref, kr_ref, cos_ref, sin_ref, qo_ref, ko_ref, vo_ref):
        cosv = cos_ref[...]
        sinv = sin_ref[...]
        low = _low_lanes(ts)
        kr = jnp.where(low, pltpu.roll(kr_ref[...], ROPE, 1), 0.0)
        kr = (kr * cosv + _swap_halves(kr) * sinv).astype(BF16)
        for pair in range(HEADS // 2):
            t0, t1, t2 = (q_ref[:, pair * pair_w + t * LANE:pair * pair_w + (t + 1) * LANE] for t in range(3))
            nope_b = jnp.where(low, pltpu.roll(t1, ROPE, 1), pltpu.roll(t2, ROPE, 1))
            ropes = jnp.where(low, t1, t2)
            roped = ropes * cosv + _swap_halves(ropes) * sinv
            qo_ref[2 * pair, :, 0:NOPE] = t0.astype(BF16)
            qo_ref[2 * pair, :, NOPE:QK_PAD] = jnp.where(low, roped, 0.0).astype(BF16)
            qo_ref[2 * pair + 1, :, 0:NOPE] = nope_b.astype(BF16)
            qo_ref[2 * pair + 1, :, NOPE:QK_PAD] = jnp.where(low, pltpu.roll(roped, ROPE, 1), 0.0).astype(BF16)
        for h in range(HEADS):
            ko_ref[h, :, 0:NOPE] = kv_ref[:, h * head_w:h * head_w + NOPE].astype(BF16)
            ko_ref[h, :, NOPE:QK_PAD] = kr
            vo_ref[h] = kv_ref[:, h * head_w + NOPE:(h + 1) * head_w].astype(BF16)

    return pl.pallas_call(
        body, name=name,
        out_shape=(jax.ShapeDtypeStruct((HEADS, S, QK_PAD), BF16), jax.ShapeDtypeStruct((HEADS, S, QK_PAD), BF16),
                   jax.ShapeDtypeStruct((HEADS, S, VDIM), BF16)),
        grid=(S // ts,),
        in_specs=[_row_spec(ts, q.shape[1]), _row_spec(ts, kv.shape[1]),
                  pl.BlockSpec((ts, LANE), lambda i: (i, P_LATENT // LANE - 1)),
                  _row_spec(ts, LANE), _row_spec(ts, LANE)],
        out_specs=(pl.BlockSpec((HEADS, ts, QK_PAD), lambda i: (0, i, 0)),
                   pl.BlockSpec((HEADS, ts, QK_PAD), lambda i: (0, i, 0)),
                   pl.BlockSpec((HEADS, ts, VDIM), lambda i: (0, i, 0))),
        compiler_params=_params("parallel"),
    )(q, kv, p, cos_t, sin_t)


def _mla_unpack(dQ, dK, dV, cos_t, sin_t, name):
    S = dQ.shape[1]
    ts = min(256, S)
    pair_w = 2 * (NOPE + ROPE)
    head_w = NOPE + VDIM

    def body(dq_ref, dk_ref, dv_ref, cos_ref, sin_ref, q_ref, kv_ref, kr_ref):
        cosv = cos_ref[...]
        sinv = sin_ref[...]
        low = _low_lanes(ts)
        for pair in range(HEADS // 2):
            blk = dq_ref[2 * pair, :, NOPE:QK_PAD] + pltpu.roll(dq_ref[2 * pair + 1, :, NOPE:QK_PAD], ROPE, 1)
            ropes = blk * cosv - _swap_halves(blk) * sinv
            nope_b = pltpu.roll(dq_ref[2 * pair + 1, :, 0:NOPE], ROPE, 1)
            q_ref[:, pair * pair_w:pair * pair_w + LANE] = dq_ref[2 * pair, :, 0:NOPE].astype(BF16)
            q_ref[:, pair * pair_w + LANE:pair * pair_w + 2 * LANE] = jnp.where(low, ropes, nope_b).astype(BF16)
            q_ref[:, pair * pair_w + 2 * LANE:(pair + 1) * pair_w] = jnp.where(low, nope_b, ropes).astype(BF16)
        dkr = dk_ref[0, :, NOPE:QK_PAD]
        for h in range(1, HEADS):
            dkr = dkr + dk_ref[h, :, NOPE:QK_PAD]
        kr_ref[...] = dkr * cosv - _swap_halves(dkr) * sinv
        for h in range(HEADS):
            kv_ref[:, h * head_w:h * head_w + NOPE] = dk_ref[h, :, 0:NOPE].astype(BF16)
            kv_ref[:, h * head_w + NOPE:(h + 1) * head_w] = dv_ref[h].astype(BF16)

    return pl.pallas_call(
        body, name=name,
        out_shape=(jax.ShapeDtypeStruct((S, HEADS * (NOPE + ROPE)), BF16),
                   jax.ShapeDtypeStruct((S, HEADS * (NOPE + VDIM)), BF16),
                   jax.ShapeDtypeStruct((S, LANE), F32)),
        grid=(S // ts,),
        in_specs=[pl.BlockSpec((HEADS, ts, QK_PAD), lambda i: (0, i, 0)),
                  pl.BlockSpec((HEADS, ts, QK_PAD), lambda i: (0, i, 0)),
                  pl.BlockSpec((HEADS, ts, VDIM), lambda i: (0, i, 0)),
                  _row_spec(ts, LANE), _row_spec(ts, LANE)],
        out_specs=(_row_spec(ts, HEADS * (NOPE + ROPE)), _row_spec(ts, HEADS * (NOPE + VDIM)),
                   _row_spec(ts, LANE)),
        compiler_params=_params("parallel"),
    )(dQ, dK, dV, cos_t, sin_t)


def _mla_gate_fwd(o, p, name):
    S, W = o.shape
    ts = min(256, S)
    wb = P_LATENT

    def body(o_ref, z_ref, y_ref):
        z = z_ref[...]
        y_ref[...] = (o_ref[...] * (z * jax.nn.sigmoid(z))).astype(BF16)

    return pl.pallas_call(
        body, name=name, out_shape=jax.ShapeDtypeStruct((S, W), BF16), grid=(S // ts, W // wb),
        in_specs=[pl.BlockSpec((ts, wb), lambda i, j: (i, j)), pl.BlockSpec((ts, wb), lambda i, j: (i, j + 1))],
        out_specs=pl.BlockSpec((ts, wb), lambda i, j: (i, j)), compiler_params=_params("parallel", "parallel"),
    )(o, p)


def _mla_gate_bwd(dyv, p, name):
    S, W = dyv.shape
    ts = min(256, S)
    wb = P_LATENT

    def body(d_ref, z_ref, do_ref):
        z = z_ref[...]
        do_ref[...] = d_ref[...] * (z * jax.nn.sigmoid(z))

    return pl.pallas_call(
        body, name=name, out_shape=jax.ShapeDtypeStruct((S, W), F32), grid=(S // ts, W // wb),
        in_specs=[pl.BlockSpec((ts, wb), lambda i, j: (i, j)), pl.BlockSpec((ts, wb), lambda i, j: (i, j + 1))],
        out_specs=pl.BlockSpec((ts, wb), lambda i, j: (i, j)), compiler_params=_params("parallel", "parallel"),
    )(dyv, p)


def _mla_mid_bwd(p, dcqn, dckvn, dkr, dyv, o, qg, kvg, name):
    S, PW = p.shape
    W = o.shape[1]
    ts = min(256, S)
    nt = Q_RANK_PAD // LANE

    def rms_bwd(xv, dy, g, count):
        r = lax.rsqrt(jnp.sum(xv * xv, axis=-1, keepdims=True) * (1.0 / count) + NORM_EPS)
        xn = xv * r
        dg = jnp.sum(dy * xn, axis=0, keepdims=True)
        dxn = dy * g
        dx = r * (dxn - xn * (jnp.sum(dxn * xn, axis=-1, keepdims=True) * (1.0 / count)))
        return dx, dg

    def body(p_ref, dcq_ref, dckv_ref, dkr_ref, dyv_ref, o_ref, qg_ref, kvg_ref, dp_ref, dqg_ref, dkvg_ref):
        @pl.when(pl.program_id(0) == 0)
        def _():
            dqg_ref[...] = jnp.zeros_like(dqg_ref)
            dkvg_ref[...] = jnp.zeros_like(dkvg_ref)

        low = _low_lanes(ts)
        cq, ckv, _ = _split_latents(_latent_tiles(p_ref), low)
        dcq, dg = rms_bwd(cq, dcq_ref[...], qg_ref[...], Q_RANK)
        dqg_ref[...] += dg
        dckv, dg = rms_bwd(ckv, dckv_ref[...], kvg_ref[...], KV_RANK)
        dkvg_ref[...] += dg
        moved = [pltpu.roll(dckv[:, t * LANE:(t + 1) * LANE], ROPE, 1) for t in range(nt)]
        moved.append(pltpu.roll(dkr_ref[...], ROPE, 1))
        for t in range(nt - 1):
            dp_ref[:, t * LANE:(t + 1) * LANE] = dcq[:, t * LANE:(t + 1) * LANE].astype(BF16)
        dp_ref[:, (nt - 1) * LANE:nt * LANE] = jnp.where(low, dcq[:, (nt - 1) * LANE:nt * LANE], moved[0]).astype(BF16)
        for t in range(nt):
            dp_ref[:, (nt + t) * LANE:(nt + t + 1) * LANE] = jnp.where(low, moved[t], moved[t + 1]).astype(BF16)
        z = p_ref[:, P_LATENT:PW]
        sig = jax.nn.sigmoid(z)
        dp_ref[:, P_LATENT:PW] = (dyv_ref[...] * o_ref[...] * (sig * (1.0 + z * (1.0 - sig)))).astype(BF16)

    return pl.pallas_call(
        body, name=name,
        out_shape=(jax.ShapeDtypeStruct((S, PW), BF16), jax.ShapeDtypeStruct((1, Q_RANK_PAD), F32),
                   jax.ShapeDtypeStruct((1, KV_RANK), F32)),
        grid=(S // ts,),
        in_specs=[_row_spec(ts, PW), _row_spec(ts, Q_RANK_PAD), _row_spec(ts, KV_RANK), _row_spec(ts, LANE),
                  _row_spec(ts, W), _row_spec(ts, W), _vec_spec(Q_RANK_PAD), _vec_spec(KV_RANK)],
        out_specs=(_row_spec(ts, PW), _vec_spec(Q_RANK_PAD), _vec_spec(KV_RANK)),
        compiler_params=_params("arbitrary"),
    )(p, dcqn, dckvn, dkr, dyv, o, qg, kvg)


def _tile_mask(T):
    row = lax.broadcasted_iota(jnp.int32, (T, T), 0) // CHUNK
    col = lax.broadcasted_iota(jnp.int32, (T, T), 1) // CHUNK
    return col <= row


def _attn_fwd(Q, K, V, name):
    H, S, _ = Q.shape
    T = min(512, S)
    n_part = 2 if T % 256 == 0 else 1
    Tq = T // n_part

    def body(q_ref, k_ref, v_ref, o_ref, lse_ref, m_scr, l_scr, acc_scr):
        qi = pl.program_id(1)
        q = q_ref[...]
        m_scr[...] = jnp.full_like(m_scr, -jnp.inf)
        l_scr[...] = jnp.zeros_like(l_scr)
        acc_scr[...] = jnp.zeros_like(acc_scr)

        def tile(j, masked):
            rows = pl.ds(pl.multiple_of(j * T, T), T)
            kt = k_ref[rows, :]
            vt = v_ref[rows, :]
            for part in range(n_part):
                sub = slice(part * Tq, (part + 1) * Tq)
                s = lax.dot_general(q[sub], kt, NT, preferred_element_type=F32) * ATTN_SCALE
                if masked:
                    row = (lax.broadcasted_iota(jnp.int32, (Tq, T), 0) + part * Tq) // CHUNK
                    col = lax.broadcasted_iota(jnp.int32, (Tq, T), 1) // CHUNK
                    s = jnp.where(col <= row, s, -1e30)
                m_prev = m_scr[sub]
                m_new = jnp.maximum(m_prev, jnp.max(s, axis=1, keepdims=True))
                pr = jnp.exp(s - m_new)
                alpha = jnp.exp(m_prev - m_new)
                l_scr[sub] = alpha * l_scr[sub] + jnp.sum(pr, axis=1, keepdims=True)
                acc_scr[sub] = alpha * acc_scr[sub] + lax.dot_general(
                    pr.astype(BF16), vt, NN, preferred_element_type=F32)
                m_scr[sub] = m_new

        def full_tile(j, carry):
            tile(j, False)
            return carry

        lax.fori_loop(0, qi, full_tile, 0)
        tile(qi, True)
        l = l_scr[...]
        o_ref[...] = acc_scr[...] / l
        lse_ref[...] = jnp.broadcast_to(m_scr[...] + jnp.log(l), (T, LANE))

    return pl.pallas_call(
        body, name=name,
        out_shape=(jax.ShapeDtypeStruct((S, H * VDIM), F32), jax.ShapeDtypeStruct((H, S, LANE), F32)),
        grid=(H, S // T),
        in_specs=[pl.BlockSpec((None, T, QK_PAD), lambda h, i: (h, i, 0)),
                  pl.BlockSpec((None, S, QK_PAD), lambda h, i: (h, 0, 0)),
                  pl.BlockSpec((None, S, VDIM), lambda h, i: (h, 0, 0))],
        out_specs=(pl.BlockSpec((T, VDIM), lambda h, i: (i, h)),
                   pl.BlockSpec((None, T, LANE), lambda h, i: (h, i, 0))),
        scratch_shapes=[pltpu.VMEM((T, 1), F32), pltpu.VMEM((T, 1), F32), pltpu.VMEM((T, VDIM), F32)],
        compiler_params=_params("parallel", "arbitrary"),
    )(Q, K, V)


def _attn_bwd(Q, K, V, o, do, lse, name):
    H, S, _ = Q.shape
    T = min(512, S)
    nq = S // T

    def body(q_ref, k_ref, v_ref, o_ref, do_ref, lse_ref, dq_ref, dk_ref, dv_ref, dk_scr, dv_scr):
        ki = pl.program_id(1)

        @pl.when(ki == 0)
        def _():
            dq_ref[...] = jnp.zeros_like(dq_ref)

        dk_scr[...] = jnp.zeros_like(dk_scr)
        dv_scr[...] = jnp.zeros_like(dv_scr)
        k = k_ref[...]
        v = v_ref[...]

        def tile(i, masked):
            rows = pl.ds(pl.multiple_of(i * T, T), T)
            q = q_ref[rows, :]
            do_f = do_ref[rows, :]
            do_b = do_f.astype(BF16)
            delta = jnp.sum(do_f * o_ref[rows, :], axis=1, keepdims=True)
            s = lax.dot_general(q, k, NT, preferred_element_type=F32) * ATTN_SCALE
            pr = jnp.exp(s - lse_ref[rows, 0:1])
            if masked:
                pr = jnp.where(_tile_mask(T), pr, 0.0)
            dv_scr[...] += lax.dot_general(pr.astype(BF16), do_b, TN, preferred_element_type=F32)
            dp = lax.dot_general(do_b, v, NT, preferred_element_type=F32)
            ds = (pr * (dp - delta) * ATTN_SCALE).astype(BF16)
            dk_scr[...] += lax.dot_general(ds, q, TN, preferred_element_type=F32)
            dq_ref[rows, :] += lax.dot_general(ds, k, NN, preferred_element_type=F32)

        def full_tile(i, carry):
            tile(i, False)
            return carry

        tile(ki, True)
        lax.fori_loop(ki + 1, nq, full_tile, 0)
        dk_ref[...] = dk_scr[...]
        dv_ref[...] = dv_scr[...]

    return pl.pallas_call(
        body, name=name,
        out_shape=(jax.ShapeDtypeStruct((H, S, QK_PAD), F32), jax.ShapeDtypeStruct((H, S, QK_PAD), F32),
                   jax.ShapeDtypeStruct((H, S, VDIM), F32)),
        grid=(H, nq),
        in_specs=[pl.BlockSpec((None, S, QK_PAD), lambda h, j: (h, 0, 0)),
                  pl.BlockSpec((None, T, QK_PAD), lambda h, j: (h, j, 0)),
                  pl.BlockSpec((None, T, VDIM), lambda h, j: (h, j, 0)),
                  pl.BlockSpec((S, VDIM), lambda h, j: (0, h)),
                  pl.BlockSpec((S, VDIM), lambda h, j: (0, h)),
                  pl.BlockSpec((None, S, LANE), lambda h, j: (h, 0, 0))],
        out_specs=(pl.BlockSpec((None, S, QK_PAD), lambda h, j: (h, 0, 0)),
                   pl.BlockSpec((None, T, QK_PAD), lambda h, j: (h, j, 0)),
                   pl.BlockSpec((None, T, VDIM), lambda h, j: (h, j, 0))),
        scratch_shapes=[pltpu.VMEM((T, QK_PAD), F32), pltpu.VMEM((T, VDIM), F32)],
        compiler_params=_params("parallel", "arbitrary"),
    )(Q, K, V, o, do, lse)


def _adamw(w, g, m, v, name):
    shape = w.shape
    C = shape[-1]
    R = math.prod(shape[:-1])
    flat = [t.reshape(R, C) for t in (w, g, m, v)]
    tr = _row_tile(R, C * 4)

    def body(w_ref, g_ref, m_ref, v_ref, d_ref, nm_ref, nv_ref):
        gv = g_ref[...]
        m_new = ADAM_B1 * m_ref[...] + (1.0 - ADAM_B1) * gv
        v_new = ADAM_B2 * v_ref[...] + (1.0 - ADAM_B2) * jnp.square(gv)
        m_hat = m_new / (1.0 - ADAM_B1 ** ADAM_STEP)
        v_hat = v_new / (1.0 - ADAM_B2 ** ADAM_STEP)
        d_ref[...] = -ADAM_LR * (m_hat / (jnp.sqrt(v_hat) + ADAM_EPS) + ADAM_WD * w_ref[...])
        nm_ref[...] = m_new
        nv_ref[...] = v_new

    spec = pl.BlockSpec((tr, C), lambda i: (i, 0))
    out = jax.ShapeDtypeStruct((R, C), F32)
    d, nm, nv = pl.pallas_call(
        body, name=name, out_shape=(out, out, out), grid=(R // tr,),
        in_specs=[spec] * 4, out_specs=(spec, spec, spec), compiler_params=_params("parallel"),
    )(*flat)
    return d.reshape(shape), nm.reshape(shape), nv.reshape(shape)


def _sum_into_half(r, buf, layer, ci, n_layers, name):
    n, M, N = r.shape
    tr = _row_tile(M, N * 4 * n, 4 << 20)

    def body(c_ref, r_ref, *rest):
        o_ref = rest[-1]
        acc = r_ref[0].astype(F32)
        for s in range(1, n):
            acc = acc + r_ref[s].astype(F32)
        o_ref[...] = acc

    in_specs = [pl.BlockSpec((n, tr, N), lambda i, c: (0, i, 0))]
    operands = [ci.reshape(1), r]
    aliases = {}
    if buf is not None:
        in_specs.append(ANY)
        operands.append(buf)
        aliases = {2: 0}
    return pl.pallas_call(
        body, name=name, out_shape=jax.ShapeDtypeStruct((n_layers, 2, M, N), F32),
        grid_spec=pltpu.PrefetchScalarGridSpec(
            num_scalar_prefetch=1, grid=(M // tr,), in_specs=in_specs,
            out_specs=pl.BlockSpec((None, None, tr, N), lambda i, c: (layer, c[0], i, 0))),
        input_output_aliases=aliases, compiler_params=_params("parallel"),
    )(*operands)


def _sum_slots(r, name):
    n, M, N = r.shape
    tr = _row_tile(M, N * 4 * n, 4 << 20)

    def body(r_ref, o_ref):
        acc = r_ref[0].astype(F32)
        for s in range(1, n):
            acc = acc + r_ref[s].astype(F32)
        o_ref[...] = acc

    return pl.pallas_call(
        body, name=name, out_shape=jax.ShapeDtypeStruct((M, N), F32), grid=(M // tr,),
        in_specs=[pl.BlockSpec((n, tr, N), lambda i: (0, i, 0))],
        out_specs=pl.BlockSpec((tr, N), lambda i: (i, 0)), compiler_params=_params("parallel"),
    )(r)


ANY = pl.BlockSpec(memory_space=pl.ANY)
DMA_CHUNK_BYTES = 1 << 20
DMA_MAX_CHUNKS = 16
PEER_ORDER = (1, 4, 5, 2, 3, 6, 7)


def _position():
    return lax.axis_index("x"), lax.axis_index("y"), lax.axis_index("c")


def _row_chunks(shape, dtype):
    rows, cols = shape
    n = max(1, min(DMA_MAX_CHUNKS, rows * cols * jnp.dtype(dtype).itemsize // DMA_CHUNK_BYTES))
    while n > 1 and (rows % n or (rows // n) % 16):
        n -= 1
    step = rows // n
    return [pl.ds(q * step, step) for q in range(n)]


def _all_gather8(xs, name):
    n = len(xs)

    def body(*refs):
        x_refs, o_refs = refs[:n], refs[n:2 * n]
        send_sems, recv_sems, local_sems = refs[2 * n:]
        x, y, c = _position()
        me, sibling = (x, y, c), (x, y, 1 - c)
        chips = [(1 - x, y), (x, 1 - y), (1 - x, 1 - y)]

        def slot(a, dev, rows):
            return o_refs[a].at[4 * dev[0] + 2 * dev[1] + dev[2], rows]

        def copy(a, k, block, to, rows, from_input=False):
            return pltpu.make_async_remote_copy(
                src_ref=x_refs[a].at[rows] if from_input else slot(a, block, rows), dst_ref=slot(a, block, rows),
                send_sem=send_sems.at[a, k], recv_sem=recv_sems.at[a, k],
                device_id=to, device_id_type=MESH)

        def mine(a, rows):
            return pltpu.make_async_copy(x_refs[a].at[rows], slot(a, me, rows), local_sems.at[a])

        chunks = [_row_chunks(t.shape, t.dtype) for t in xs]
        whole = [pl.ds(0, t.shape[0]) for t in xs]
        for a in range(n):
            for rows in chunks[a]:
                mine(a, rows).start()
        sent = []
        for a in range(n):
            for k, to in enumerate([sibling] + [(*chip, c) for chip in chips]):
                for rows in chunks[a]:
                    copy(a, k, me, to, rows, from_input=True).start()
                sent.append(copy(a, k, me, to, whole[a], from_input=True))
        for a in range(n):
            for j, chip in enumerate(chips):
                copy(a, 1 + j, (*chip, c), me, whole[a]).wait_recv()
                for rows in chunks[a]:
                    copy(a, 4 + j, (*chip, c), sibling, rows).start()
                sent.append(copy(a, 4 + j, (*chip, c), sibling, whole[a]))
        for a in range(n):
            copy(a, 0, sibling, me, whole[a]).wait_recv()
            for j, chip in enumerate(chips):
                copy(a, 4 + j, (*chip, 1 - c), me, whole[a]).wait_recv()
        for cp in sent:
            cp.wait_send()
        for a in range(n):
            mine(a, whole[a]).wait()

    return pl.pallas_call(
        body, name=name,
        out_shape=[jax.ShapeDtypeStruct((8,) + t.shape, t.dtype) for t in xs],
        in_specs=[ANY] * n, out_specs=[ANY] * n,
        scratch_shapes=[pltpu.SemaphoreType.DMA((n, 7)), pltpu.SemaphoreType.DMA((n, 7)),
                        pltpu.SemaphoreType.DMA((n,))],
    )(*xs)


def _exchange8(gs, name):
    n = len(gs)

    def body(*refs):
        g_refs, r_refs = refs[:n], refs[n:2 * n]
        send_sems, recv_sems, local_sems = refs[2 * n:]
        x, y, c = _position()
        my = 4 * x + 2 * y + c

        def mine(a, rows):
            return pltpu.make_async_copy(g_refs[a].at[my, rows], r_refs[a].at[my, rows], local_sems.at[a])

        def copy(a, m, rows):
            px = (1 - x) if m & 4 else x
            py = (1 - y) if m & 2 else y
            pc = (1 - c) if m & 1 else c
            return pltpu.make_async_remote_copy(
                src_ref=g_refs[a].at[4 * px + 2 * py + pc, rows], dst_ref=r_refs[a].at[my, rows],
                send_sem=send_sems.at[a, m - 1], recv_sem=recv_sems.at[a, m - 1],
                device_id=(px, py, pc), device_id_type=MESH)

        chunks = [_row_chunks(t.shape[1:], t.dtype) for t in gs]
        whole = [pl.ds(0, t.shape[1]) for t in gs]
        for a in range(n):
            for rows in chunks[a]:
                mine(a, rows).start()
        for a in range(n):
            for m in PEER_ORDER:
                for rows in chunks[a]:
                    copy(a, m, rows).start()
        for a in range(n):
            for m in PEER_ORDER:
                copy(a, m, whole[a]).wait_recv()
        for a in range(n):
            for m in PEER_ORDER:
                copy(a, m, whole[a]).wait_send()
            mine(a, whole[a]).wait()

    return pl.pallas_call(
        body, name=name,
        out_shape=[jax.ShapeDtypeStruct(t.shape, t.dtype) for t in gs],
        in_specs=[ANY] * n, out_specs=[ANY] * n,
        scratch_shapes=[pltpu.SemaphoreType.DMA((n, 7)), pltpu.SemaphoreType.DMA((n, 7)),
                        pltpu.SemaphoreType.DMA((n,))],
    )(*gs)


def _pair_swap(bufs, name):
    n = len(bufs)
    pieces = [(a, l) for a, t in enumerate(bufs) for l in range(t.shape[0])]

    def body(*refs):
        b_refs = refs[n:2 * n]
        send_sems, recv_sems = refs[2 * n:]
        x, y, c = _position()

        def copy(k, rows):
            a, l = pieces[k]
            half = b_refs[a].at[l, c, rows]
            return pltpu.make_async_remote_copy(
                src_ref=half, dst_ref=half, send_sem=send_sems.at[k], recv_sem=recv_sems.at[k],
                device_id=(x, y, 1 - c), device_id_type=MESH)

        chunks = [_row_chunks(bufs[a].shape[2:], bufs[a].dtype) for a, _ in pieces]
        whole = [pl.ds(0, bufs[a].shape[2]) for a, _ in pieces]
        for k in range(len(pieces)):
            for rows in chunks[k]:
                copy(k, rows).start()
        for k in range(len(pieces)):
            copy(k, whole[k]).wait_recv()
        for k in range(len(pieces)):
            copy(k, whole[k]).wait_send()

    return pl.pallas_call(
        body, name=name,
        out_shape=[jax.ShapeDtypeStruct(t.shape, t.dtype) for t in bufs],
        in_specs=[ANY] * n, out_specs=[ANY] * n,
        input_output_aliases={a: a for a in range(n)},
        scratch_shapes=[pltpu.SemaphoreType.DMA((len(pieces),)), pltpu.SemaphoreType.DMA((len(pieces),))],
    )(*bufs)


def _pack_rows(parts):
    flat = jnp.concatenate([t.reshape(-1).astype(F32) for t in parts])
    pad = (-flat.shape[0]) % (256 * LANE)
    return jnp.pad(flat, (0, pad)).reshape(-1, LANE)


def _my_half(w2d, ci):
    half = w2d.shape[0] // 2
    return lax.dynamic_slice_in_dim(w2d, ci * half, half, axis=0).astype(BF16)


def _col_view(g):
    _, half, Cs = g.shape
    return g.reshape(4, 2 * half, Cs)


def _row_view(g):
    _, half, C = g.shape
    return g.reshape(8 * half, C)


def _rope_tables(S):
    pos = jnp.arange(S, dtype=F32)
    inv_freq = ROPE_THETA ** (-jnp.arange(0, ROPE, 2, dtype=F32) / ROPE)
    ang = pos[:, None] * inv_freq[None, :]
    cos, sin = jnp.cos(ang), jnp.sin(ang)
    cos_t = jnp.concatenate([cos, cos, cos, cos], axis=1)
    sin_t = jnp.concatenate([-sin, sin, -sin, sin], axis=1)
    return cos_t, sin_t


def kernel(x, c, ada_w, ada_b, pre_g, post_g, sgu_w_in, sgu_norm_g, sgu_w_s, sgu_b_s, sgu_w_out, mla_w_in, mla_q_norm_g, mla_kv_norm_g, mla_w_uq, mla_w_ukv, mla_w_out, loss_target, m_ada_w, m_ada_b, m_pre_g, m_post_g, m_sgu_w_in, m_sgu_norm_g, m_sgu_w_s, m_sgu_b_s, m_sgu_w_out, m_mla_w_in, m_mla_q_norm_g, m_mla_kv_norm_g, m_mla_w_uq, m_mla_w_ukv, m_mla_w_out, v_ada_w, v_ada_b, v_pre_g, v_post_g, v_sgu_w_in, v_sgu_norm_g, v_sgu_w_s, v_sgu_b_s, v_sgu_w_out, v_mla_w_in, v_mla_q_norm_g, v_mla_kv_norm_g, v_mla_w_uq, v_mla_w_ukv, v_mla_w_out):
    S, D = x.shape[1], x.shape[2]
    depth = ada_w.shape[0]
    E = sgu_w_out.shape[1] * 4
    xi, yi, ci = _position()
    chip = 2 * xi + yi
    dev = 4 * xi + 2 * yi + ci
    x0 = x.reshape(S, D)
    target = loss_target.reshape(S, D)

    small = _pack_rows([c, mla_q_norm_g, mla_kv_norm_g])
    mixer_w = [sgu_w_in, sgu_w_out, mla_w_in, mla_w_uq, mla_w_ukv, mla_w_out]
    mine = [_my_half(w[j], ci) for j in range(2) for w in mixer_w]
    gathered = _all_gather8([small] + mine, "gather_weights")
    small_all = gathered[0].reshape(8, -1)
    g_sin, g_sout, g_min, g_uq, g_ukv, g_mout = ([gathered[1 + 6 * j + t] for j in range(2)] for t in range(6))
    qn_w, kvn_w = mla_q_norm_g.shape[1], mla_kv_norm_g.shape[1]
    c_all = small_all[:, :D]
    qn_all = small_all[0::2, D:D + 2 * qn_w].reshape(4, 2, qn_w)
    kvn_all = small_all[0::2, D + 2 * qn_w:D + 2 * qn_w + 2 * kvn_w].reshape(4, 2, kvn_w)
    q_gain = jnp.pad(jnp.transpose(qn_all, (1, 0, 2)).reshape(2, 1, Q_RANK), ((0, 0), (0, 0), (0, Q_RANK_PAD - Q_RANK)))
    kv_gain = jnp.transpose(kvn_all, (1, 0, 2)).reshape(2, 1, KV_RANK)

    w_sin = [_col_view(g) for g in g_sin]
    w_sout = [_row_view(g) for g in g_sout]
    w_min = [_col_view(g) for g in g_min]
    w_uq = [jnp.pad(_col_view(g), ((0, 0), (0, Q_RANK_PAD - Q_RANK), (0, 0))) for g in g_uq]
    w_ukv = [_col_view(g) for g in g_ukv]
    w_mout = [_row_view(g) for g in g_mout]

    cols = ada_w.shape[2]
    ada_b_cols = lax.dynamic_slice_in_dim(ada_b, chip * cols, cols, axis=1)
    c_pad = jnp.pad(c_all, ((0, 8), (0, 0)))
    mod_cols = _ada_mod(c_pad, ada_w, ada_b_cols, "ada_mod")[:, :8]
    mod_all, = _all_gather8([mod_cols.reshape(depth * 8, cols)], "gather_mod")
    mod_all = jnp.transpose(mod_all[0::2].reshape(4, depth, 8, cols), (1, 2, 0, 3)).reshape(depth, 8, 4 * cols)
    mod = lax.dynamic_index_in_dim(mod_all, dev, 1, keepdims=False)
    shift = [mod[i:i + 1, :D] for i in range(depth)]
    scale = [mod[i:i + 1, D:2 * D] for i in range(depth)]
    gate = [mod[i:i + 1, 2 * D:] for i in range(depth)]

    cos_t, sin_t = _rope_tables(S)
    b_bc = jnp.broadcast_to(sgu_b_s[:, :, :, None], sgu_b_s.shape + (LANE,))
    w_sT = jnp.swapaxes(sgu_w_s, 2, 3)

    saved = []
    xs = x0
    for i in range(depth):
        j = i // 2
        tag = f"l{i}"
        h = _pre_fwd(xs, pre_g[i:i + 1], scale[i], shift[i], f"pre_fwd_{tag}")
        if i % 2 == 0:
            uvz = _mm(h, w_sin[j], b_sharded=True, name=f"sgu_in_{tag}")
            y = _sgu_gate_fwd(uvz, sgu_norm_g[j:j + 1], sgu_w_s[j], b_bc[j], f"sgu_gate_fwd_{tag}")
            out = _mm(y, w_sout[j], name=f"sgu_out_{tag}")
            saved.append(dict(x=xs, h=h, uvz=uvz, y=y, out=out))
        else:
            p = _mm(h, w_min[j], b_sharded=True, name=f"mla_in_{tag}")
            cqn, ckvn = _mla_mid_fwd(p, q_gain[j], kv_gain[j], f"mla_mid_fwd_{tag}")
            q = _mm(cqn, w_uq[j], b_sharded=True, name=f"mla_uq_{tag}")
            kv = _mm(ckvn, w_ukv[j], b_sharded=True, name=f"mla_ukv_{tag}")
            Q, K, V = _mla_pack(q, kv, p, cos_t, sin_t, f"mla_pack_{tag}")
            o, lse = _attn_fwd(Q, K, V, f"attn_fwd_{tag}")
            y = _mla_gate_fwd(o, p, f"mla_gate_fwd_{tag}")
            out = _mm(y, w_mout[j], name=f"mla_out_{tag}")
            saved.append(dict(x=xs, h=h, p=p, cqn=cqn, ckvn=ckvn, Q=Q, K=K, V=V, o=o, lse=lse, y=y, out=out))
        xs = _post_fwd(xs, out, gate[i], post_g[i:i + 1], f"post_fwd_{tag}")

    dx, loss_part = _loss_grad(xs, target, "loss")
    loss = lax.psum(loss_part[0, 0], ("x", "y", "c"))

    dmod = [None] * depth
    d_pre_g = [None] * depth
    d_post_g = [None] * depth
    d_sgu = [None] * 2
    d_mla = [None] * 2
    for i in reversed(range(depth)):
        j = i // 2
        tag = f"l{i}"
        sv = saved[i]
        dy, dgate, d_post_g[i] = _post_bwd(dx, sv["out"], gate[i], post_g[i:i + 1], f"post_bwd_{tag}")
        if i % 2 == 0:
            dw_out = _mm(sv["y"], dy, ta=True, out_dtype=BF16, name=f"sgu_out_dw_{tag}")
            dyv = _mm(dy, w_sout[j], tb=True, name=f"sgu_out_dx_{tag}")
            duvz, dng, dws, dbs = _sgu_gate_bwd(sv["uvz"], dyv, sgu_norm_g[j:j + 1], sgu_w_s[j], w_sT[j], b_bc[j],
                                                f"sgu_gate_bwd_{tag}")
            dw_in = _mm(sv["h"], duvz, ta=True, out_sharded=True, out_dtype=BF16, name=f"sgu_in_dw_{tag}")
            dh = _mm(duvz, w_sin[j], tb=True, b_sharded=True, name=f"sgu_in_dx_{tag}")
            d_sgu[j] = dict(w_in=dw_in, w_out=dw_out, norm_g=dng, w_s=dws, b_s=dbs[:, :, 0])
        else:
            dw_out = _mm(sv["y"], dy, ta=True, out_dtype=BF16, name=f"mla_out_dw_{tag}")
            dyv = _mm(dy, w_mout[j], tb=True, name=f"mla_out_dx_{tag}")
            p = sv["p"]
            do = _mla_gate_bwd(dyv, p, f"mla_gate_bwd_{tag}")
            dQ, dK, dV = _attn_bwd(sv["Q"], sv["K"], sv["V"], sv["o"], do, sv["lse"], f"attn_bwd_{tag}")
            dq, dkv, dkr = _mla_unpack(dQ, dK, dV, cos_t, sin_t, f"mla_unpack_{tag}")
            dw_uq = _mm(sv["cqn"], dq, ta=True, out_sharded=True, out_dtype=BF16, name=f"mla_uq_dw_{tag}")
            dcqn = _mm(dq, w_uq[j], tb=True, b_sharded=True, name=f"mla_uq_dx_{tag}")
            dw_ukv = _mm(sv["ckvn"], dkv, ta=True, out_sharded=True, out_dtype=BF16, name=f"mla_ukv_dw_{tag}")
            dckvn = _mm(dkv, w_ukv[j], tb=True, b_sharded=True, name=f"mla_ukv_dx_{tag}")
            dp, dqg, dkvg = _mla_mid_bwd(p, dcqn, dckvn, dkr, dyv, sv["o"], q_gain[j], kv_gain[j], f"mla_mid_bwd_{tag}")
            dw_in = _mm(sv["h"], dp, ta=True, out_sharded=True, out_dtype=BF16, name=f"mla_in_dw_{tag}")
            dh = _mm(dp, w_min[j], tb=True, b_sharded=True, name=f"mla_in_dx_{tag}")
            d_mla[j] = dict(w_in=dw_in, w_uq=dw_uq[:, :Q_RANK], w_ukv=dw_ukv, w_out=dw_out,
                            qg=dqg[0, :Q_RANK], kvg=dkvg[0])
        dx, dshift, dscale, d_pre_g[i] = _pre_bwd(dh, sv["x"], dx, pre_g[i:i + 1], scale[i], f"pre_bwd_{tag}")
        dmod[i] = jnp.concatenate([dshift, dscale, dgate], axis=1)
    grad_x = dx.reshape(x.shape)

    parts = [jnp.concatenate(dmod, axis=0), jnp.concatenate(d_pre_g, axis=0), jnp.concatenate(d_post_g, axis=0),
             jnp.stack([d["norm_g"][0] for d in d_sgu]), jnp.stack([d["w_s"] for d in d_sgu]),
             jnp.stack([d["b_s"] for d in d_sgu]), jnp.stack([d["qg"] for d in d_mla]),
             jnp.stack([d["kvg"] for d in d_mla])]
    sizes = [int(np.prod(t.shape)) for t in parts]
    packed = _pack_rows(parts)
    packed_all, dmod_all = _all_gather8([packed, parts[0]], "gather_small_grads")
    total = _sum_slots(packed_all, "sum_small_grads").reshape(-1)
    offs = np.concatenate([[0], np.cumsum(sizes)])
    pieces = [total[int(offs[t]):int(offs[t + 1])].reshape(parts[t].shape) for t in range(len(parts))]
    g_ada_b, g_pre_g, g_post_g, g_norm_g, g_w_s, g_b_s, g_qg_full, g_kvg_full = pieces
    g_qg = lax.dynamic_slice_in_dim(g_qg_full, chip * qn_w, qn_w, axis=1)
    g_kvg = lax.dynamic_slice_in_dim(g_kvg_full, chip * kvn_w, kvn_w, axis=1)
    dmod_cols = jnp.stack([lax.dynamic_slice_in_dim(dmod_all[:, i], chip * cols, cols, axis=1) for i in range(depth)])
    dmod_cols = jnp.pad(dmod_cols, ((0, 0), (0, LANE - 8), (0, 0)))
    g_ada_w = _ada_grad(jnp.pad(c_all.T, ((0, 0), (0, LANE - 8))), dmod_cols, "ada_grad")

    def slots(dw):
        return dw.reshape(8, -1, dw.shape[-1])

    names = ["sgu_w_in", "sgu_w_out", "mla_w_in", "mla_w_uq", "mla_w_ukv", "mla_w_out"]
    big = [slots(d[key]) for d_mix, keys in ((d_sgu, ("w_in", "w_out")), (d_mla, ("w_in", "w_uq", "w_ukv", "w_out")))
           for key in keys for d in d_mix]
    received = _exchange8(big, "exchange_grads")
    halves = []
    for t in range(6):
        buf = None
        for layer in range(2):
            buf = _sum_into_half(received[2 * t + layer], buf, layer, ci, 2, f"sum_{names[t]}_{layer}")
        halves.append(buf)
    g_sgu_w_in, g_sgu_w_out, g_mla_w_in, g_mla_w_uq, g_mla_w_ukv, g_mla_w_out = _pair_swap(halves, "swap_grads")

    grads = [g_ada_w, g_ada_b, g_pre_g, g_post_g, g_sgu_w_in, g_norm_g, g_w_s, g_b_s, g_sgu_w_out,
             g_mla_w_in, g_qg, g_kvg, g_mla_w_uq, g_mla_w_ukv, g_mla_w_out]
    weights = [ada_w, ada_b, pre_g, post_g, sgu_w_in, sgu_norm_g, sgu_w_s, sgu_b_s, sgu_w_out,
               mla_w_in, mla_q_norm_g, mla_kv_norm_g, mla_w_uq, mla_w_ukv, mla_w_out]
    ms = [m_ada_w, m_ada_b, m_pre_g, m_post_g, m_sgu_w_in, m_sgu_norm_g, m_sgu_w_s, m_sgu_b_s, m_sgu_w_out,
          m_mla_w_in, m_mla_q_norm_g, m_mla_kv_norm_g, m_mla_w_uq, m_mla_w_ukv, m_mla_w_out]
    vs = [v_ada_w, v_ada_b, v_pre_g, v_post_g, v_sgu_w_in, v_sgu_norm_g, v_sgu_w_s, v_sgu_b_s, v_sgu_w_out,
          v_mla_w_in, v_mla_q_norm_g, v_mla_kv_norm_g, v_mla_w_uq, v_mla_w_ukv, v_mla_w_out]
    wnames = ["ada_w", "ada_b", "pre_g", "post_g", "sgu_w_in", "sgu_norm_g", "sgu_w_s", "sgu_b_s", "sgu_w_out",
              "mla_w_in", "mla_q_norm_g", "mla_kv_norm_g", "mla_w_uq", "mla_w_ukv", "mla_w_out"]
    grads = [g.reshape(w.shape) for g, w in zip(grads, weights)]
    deltas, new_m, new_v = [], [], []
    for w, g, m, v, nm in zip(weights, grads, ms, vs, wnames):
        d, a, b = _adamw(w, g, m, v, f"adamw_{nm}")
        deltas.append(d)
        new_m.append(a)
        new_v.append(b)
    return (loss, grad_x, *grads, *deltas, *new_m, *new_v)
```

```python
import functools
import math

import jax
import jax.numpy as jnp
import numpy as np
from jax import lax
from jax.experimental import pallas as pl
from jax.experimental.pallas import tpu as pltpu

F32 = jnp.float32
BF16 = jnp.bfloat16
MESH = pl.DeviceIdType.MESH

NORM_EPS = 1e-6
CHUNK = 64
SGU_BLOCK = 128
SGU_GROUPS = 16
HEADS = 16
NOPE = 128
ROPE = 64
VDIM = 128
QK_PAD = 256
Q_RANK = 448
Q_RANK_PAD = 512
KV_RANK = 512
ROPE_THETA = 10000.0
ATTN_SCALE = (NOPE + ROPE) ** -0.5

ADAM_LR = 0.001
ADAM_B1 = 0.9
ADAM_B2 = 0.999
ADAM_EPS = 1e-08
ADAM_WD = 0.01
ADAM_STEP = 10

LANE = 128
VMEM_LIMIT = 48 * 1024 * 1024

NN = (((1,), (0,)), ((), ()))
NT = (((1,), (1,)), ((), ()))
TN = (((0,), (0,)), ((), ()))


def _params(*sem):
    return pltpu.CompilerParams(dimension_semantics=sem, vmem_limit_bytes=VMEM_LIMIT)


def _row_tile(rows, row_bytes, target_bytes=1 << 20):
    if rows * row_bytes <= target_bytes or rows % 16:
        return rows
    best = 16
    t = 16
    while t <= rows:
        if rows % t == 0 and t * row_bytes <= target_bytes:
            best = t
        t += 16
    return best


def _fit(dim, target):
    if dim <= target:
        return dim
    t = (target // LANE) * LANE
    while t > LANE and dim % t:
        t -= LANE
    return t


def _gelu(x):
    return 0.5 * x * (1.0 + lax.erf(x * 0.7071067811865476))


def _gelu_grad(x):
    return 0.5 * (1.0 + lax.erf(x * 0.7071067811865476)) + x * jnp.exp(-0.5 * x * x) * 0.3989422804014327


def _mm(a, b, *, ta=False, tb=False, b_sharded=False, out_sharded=False, out_dtype=F32,
        tm=1024, tn=1024, tk=2048, name):
    if ta:
        K, M = a.shape
    else:
        M, K = a.shape
    if b_sharded:
        shards, rows, Cs = b.shape
        b_shape = (rows, shards * Cs)
    else:
        b_shape = b.shape
    if tb:
        N, K2 = b_shape
    else:
        K2, N = b_shape
    assert K == K2, (a.shape, b.shape, ta, tb)
    n_lim = Cs if (b_sharded and not tb) else (N // 4 if out_sharded else N)
    k_lim = Cs if (b_sharded and tb) else K
    tm, tn, tk = _fit(M, tm), _fit(n_lim, tn), _fit(k_lim, tk)
    assert M % tm == 0 and n_lim % tn == 0 and k_lim % tk == 0, (M, N, K, tm, tn, tk)
    nk = K // tk
    nb_n = n_lim // tn
    nb_k = k_lim // tk
    dims = (((0 if ta else 1,), (1 if tb else 0,)), ((), ()))

    def body(a_ref, b_ref, o_ref, *scratch):
        prod = lax.dot_general(a_ref[...].astype(BF16), b_ref[...].astype(BF16), dims,
                               preferred_element_type=F32)
        if nk == 1:
            o_ref[...] = prod.astype(out_dtype)
        else:
            acc_ref, = scratch
            k = pl.program_id(2)

            @pl.when(k == 0)
            def _():
                acc_ref[...] = prod

            @pl.when(k > 0)
            def _():
                acc_ref[...] += prod

            @pl.when(k == nk - 1)
            def _():
                o_ref[...] = acc_ref[...].astype(out_dtype)

    a_spec = (pl.BlockSpec((tk, tm), lambda i, j, k: (k, i)) if ta
              else pl.BlockSpec((tm, tk), lambda i, j, k: (i, k)))
    if b_sharded and tb:
        b_spec = pl.BlockSpec((None, tn, tk), lambda i, j, k: (k // nb_k, j, k % nb_k))
    elif b_sharded:
        b_spec = pl.BlockSpec((None, tk, tn), lambda i, j, k: (j // nb_n, k, j % nb_n))
    elif tb:
        b_spec = pl.BlockSpec((tn, tk), lambda i, j, k: (j, k))
    else:
        b_spec = pl.BlockSpec((tk, tn), lambda i, j, k: (k, j))
    if out_sharded:
        out_shape = jax.ShapeDtypeStruct((4, M, N // 4), out_dtype)
        out_spec = pl.BlockSpec((None, tm, tn), lambda i, j, k: (j // nb_n, i, j % nb_n))
    else:
        out_shape = jax.ShapeDtypeStruct((M, N), out_dtype)
        out_spec = pl.BlockSpec((tm, tn), lambda i, j, k: (i, j))
    return pl.pallas_call(
        body, name=name,
        out_shape=out_shape,
        grid=(M // tm, N // tn, nk),
        in_specs=[a_spec, b_spec],
        out_specs=out_spec,
        scratch_shapes=[] if nk == 1 else [pltpu.VMEM((tm, tn), F32)],
        compiler_params=_params("parallel", "parallel", "arbitrary"),
    )(a, b)


def _split_bf16(v):
    hi = v.astype(BF16)
    lo = (v - hi.astype(F32)).astype(BF16)
    return hi, lo


def _dot3(a, b, dims):
    a_hi, a_lo = _split_bf16(a)
    b_hi, b_lo = _split_bf16(b)
    out = lax.dot_general(a_hi, b_hi, dims, preferred_element_type=F32)
    out += lax.dot_general(a_lo, b_hi, dims, preferred_element_type=F32)
    out += lax.dot_general(a_hi, b_lo, dims, preferred_element_type=F32)
    return out


def _ada_mod(c_all, ada_w, ada_b_cols, name):
    L, D, cols = ada_w.shape
    B = c_all.shape[0]
    tn = 512 if cols % 512 == 0 else cols

    def body(c_ref, w_ref, b_ref, o_ref):
        cv = c_ref[...]
        cond = cv * jax.nn.sigmoid(cv)
        o_ref[...] = _dot3(cond, w_ref[...], NN) + b_ref[...]

    return pl.pallas_call(
        body, name=name,
        out_shape=jax.ShapeDtypeStruct((L, B, cols), F32),
        grid=(L, cols // tn),
        in_specs=[pl.BlockSpec((B, D), lambda l, j: (0, 0)),
                  pl.BlockSpec((None, D, tn), lambda l, j: (l, 0, j)),
                  pl.BlockSpec((None, 1, tn), lambda l, j: (l, 0, j))],
        out_specs=pl.BlockSpec((None, B, tn), lambda l, j: (l, 0, j)),
        compiler_params=_params("parallel", "parallel"),
    )(c_all, ada_w, ada_b_cols.reshape(L, 1, cols))


def _ada_grad(c_t, dmod_cols, name):
    L, B, cols = dmod_cols.shape
    D = c_t.shape[0]
    tn = 512 if cols % 512 == 0 else cols

    def body(c_ref, d_ref, o_ref):
        cv = c_ref[...]
        cond = cv * jax.nn.sigmoid(cv)
        o_ref[...] = _dot3(cond, d_ref[...], NN)

    return pl.pallas_call(
        body, name=name,
        out_shape=jax.ShapeDtypeStruct((L, D, cols), F32),
        grid=(L, cols // tn),
        in_specs=[pl.BlockSpec((D, B), lambda l, j: (0, 0)),
                  pl.BlockSpec((None, B, tn), lambda l, j: (l, 0, j))],
        out_specs=pl.BlockSpec((None, D, tn), lambda l, j: (l, 0, j)),
        compiler_params=_params("parallel", "parallel"),
    )(c_t, dmod_cols)


def _row_spec(ts, width):
    return pl.BlockSpec((ts, width), lambda i: (i, 0))


def _vec_spec(width):
    return pl.BlockSpec((1, width), lambda i: (0, 0))


def _pre_fwd(x, pre_g, scale, shift, name, after=()):
    S, D = x.shape
    ts = min(256, S)

    def body(x_ref, g_ref, sc_ref, sh_ref, *rest):
        h_ref = rest[-1]
        xv = x_ref[...]
        r = lax.rsqrt(jnp.mean(xv * xv, axis=-1, keepdims=True) + NORM_EPS)
        h_ref[...] = ((xv * r * g_ref[...]) * (1.0 + sc_ref[...]) + sh_ref[...]).astype(BF16)

    return pl.pallas_call(
        body, name=name, out_shape=jax.ShapeDtypeStruct((S, D), BF16), grid=(S // ts,),
        in_specs=[_row_spec(ts, D), _vec_spec(D), _vec_spec(D), _vec_spec(D)]
        + [pl.BlockSpec(memory_space=pl.ANY)] * len(after),
        out_specs=_row_spec(ts, D), compiler_params=_params("parallel"),
    )(x, pre_g, scale, shift, *after)


def _pre_bwd(dh, x, dx_res, pre_g, scale, name):
    S, D = x.shape
    ts = min(256, S)

    def body(dh_ref, x_ref, dr_ref, g_ref, sc_ref, dx_ref, dsh_ref, dsc_ref, dg_ref):
        @pl.when(pl.program_id(0) == 0)
        def _():
            dsh_ref[...] = jnp.zeros_like(dsh_ref)
            dsc_ref[...] = jnp.zeros_like(dsc_ref)
            dg_ref[...] = jnp.zeros_like(dg_ref)

        dh = dh_ref[...]
        xv = x_ref[...]
        g = g_ref[...]
        one_sc = 1.0 + sc_ref[...]
        r = lax.rsqrt(jnp.mean(xv * xv, axis=-1, keepdims=True) + NORM_EPS)
        xn = xv * r
        dsh_ref[...] += jnp.sum(dh, axis=0, keepdims=True)
        dsc_ref[...] += jnp.sum(dh * (xn * g), axis=0, keepdims=True)
        dg_ref[...] += jnp.sum(dh * one_sc * xn, axis=0, keepdims=True)
        dxn = dh * one_sc * g
        dx_ref[...] = dr_ref[...] + r * (dxn - xn * jnp.mean(dxn * xn, axis=-1, keepdims=True))

    vec = jax.ShapeDtypeStruct((1, D), F32)
    return pl.pallas_call(
        body, name=name, out_shape=(jax.ShapeDtypeStruct((S, D), F32), vec, vec, vec), grid=(S // ts,),
        in_specs=[_row_spec(ts, D), _row_spec(ts, D), _row_spec(ts, D), _vec_spec(D), _vec_spec(D)],
        out_specs=(_row_spec(ts, D), _vec_spec(D), _vec_spec(D), _vec_spec(D)),
        compiler_params=_params("arbitrary"),
    )(dh, x, dx_res, pre_g, scale)


def _post_fwd(x, y, gate, post_g, name):
    S, D = x.shape
    ts = min(256, S)

    def body(x_ref, y_ref, gt_ref, g_ref, o_ref):
        yv = y_ref[...]
        r = lax.rsqrt(jnp.mean(yv * yv, axis=-1, keepdims=True) + NORM_EPS)
        o_ref[...] = x_ref[...] + gt_ref[...] * (yv * r * g_ref[...])

    return pl.pallas_call(
        body, name=name, out_shape=jax.ShapeDtypeStruct((S, D), F32), grid=(S // ts,),
        in_specs=[_row_spec(ts, D), _row_spec(ts, D), _vec_spec(D), _vec_spec(D)],
        out_specs=_row_spec(ts, D), compiler_params=_params("parallel"),
    )(x, y, gate, post_g)


def _post_bwd(dx, y, gate, post_g, name):
    S, D = y.shape
    ts = min(256, S)

    def body(dx_ref, y_ref, gt_ref, g_ref, dy_ref, dgt_ref, dg_ref):
        @pl.when(pl.program_id(0) == 0)
        def _():
            dgt_ref[...] = jnp.zeros_like(dgt_ref)
            dg_ref[...] = jnp.zeros_like(dg_ref)

        dxv = dx_ref[...]
        yv = y_ref[...]
        g = g_ref[...]
        gt = gt_ref[...]
        r = lax.rsqrt(jnp.mean(yv * yv, axis=-1, keepdims=True) + NORM_EPS)
        yn = yv * r
        dgt_ref[...] += jnp.sum(dxv * (yn * g), axis=0, keepdims=True)
        dg_ref[...] += jnp.sum(dxv * gt * yn, axis=0, keepdims=True)
        dyn = dxv * gt * g
        dy_ref[...] = (r * (dyn - yn * jnp.mean(dyn * yn, axis=-1, keepdims=True))).astype(BF16)

    vec = jax.ShapeDtypeStruct((1, D), F32)
    return pl.pallas_call(
        body, name=name, out_shape=(jax.ShapeDtypeStruct((S, D), BF16), vec, vec), grid=(S // ts,),
        in_specs=[_row_spec(ts, D), _row_spec(ts, D), _vec_spec(D), _vec_spec(D)],
        out_specs=(_row_spec(ts, D), _vec_spec(D), _vec_spec(D)),
        compiler_params=_params("arbitrary"),
    )(dx, y, gate, post_g)


def _loss_grad(xf, target, name):
    S, D = xf.shape
    ts = min(256, S)

    def body(x_ref, t_ref, dx_ref, l_ref):
        @pl.when(pl.program_id(0) == 0)
        def _():
            l_ref[...] = jnp.zeros_like(l_ref)

        e = x_ref[...] - t_ref[...]
        dx_ref[...] = e * (1.0 / D)
        row = jnp.sum(e * e, axis=1, keepdims=True) * (1.0 / D)
        l_ref[...] += 0.5 * jnp.sum(row, axis=0, keepdims=True)

    return pl.pallas_call(
        body, name=name,
        out_shape=(jax.ShapeDtypeStruct((S, D), F32), jax.ShapeDtypeStruct((1, 1), F32)), grid=(S // ts,),
        in_specs=[_row_spec(ts, D), _row_spec(ts, D)],
        out_specs=(_row_spec(ts, D), pl.BlockSpec((1, 1), lambda i: (0, 0))),
        compiler_params=_params("arbitrary"),
    )(xf, target)


def _chunk_mask(transposed=False):
    row = lax.broadcasted_iota(jnp.int32, (SGU_BLOCK, SGU_BLOCK), 0) // CHUNK
    col = lax.broadcasted_iota(jnp.int32, (SGU_BLOCK, SGU_BLOCK), 1) // CHUNK
    return (row <= col) if transposed else (col <= row)


def _sgu_gate_fwd(uvz, norm_g, w_s, b_bc, name):
    S, E3 = uvz.shape
    E = E3 // 3
    T = SGU_BLOCK
    gd = E // SGU_GROUPS

    def body(uvz_ref, ng_ref, ws_ref, bb_ref, y_ref, v_scr):
        gv = _gelu(uvz_ref[:, E:2 * E])
        mu = jnp.mean(gv, axis=-1, keepdims=True)
        xc = gv - mu
        rstd = lax.rsqrt(jnp.mean(xc * xc, axis=-1, keepdims=True) + NORM_EPS)
        v_scr[...] = (xc * rstd * ng_ref[...]).astype(BF16)
        mask = _chunk_mask()
        for g in range(SGU_GROUPS):
            sl = slice(g * gd, (g + 1) * gd)
            wg = jnp.where(mask, ws_ref[g], 0.0).astype(BF16)
            vm = lax.dot_general(wg, v_scr[:, sl], NN, preferred_element_type=F32)
            vm = vm + jnp.tile(bb_ref[g], (1, gd // LANE))
            z = uvz_ref[:, 2 * E + g * gd:2 * E + (g + 1) * gd]
            y_ref[:, sl] = (_gelu(uvz_ref[:, sl]) * vm * (z * jax.nn.sigmoid(z))).astype(BF16)

    return pl.pallas_call(
        body, name=name, out_shape=jax.ShapeDtypeStruct((S, E), BF16), grid=(S // T,),
        in_specs=[_row_spec(T, E3), _vec_spec(E),
                  pl.BlockSpec((SGU_GROUPS, T, T), lambda i: (0, 0, 0)),
                  pl.BlockSpec((SGU_GROUPS, T, LANE), lambda i: (0, 0, 0))],
        out_specs=_row_spec(T, E),
        scratch_shapes=[pltpu.VMEM((T, E), BF16)],
        compiler_params=_params("parallel"),
    )(uvz, norm_g, w_s, b_bc)


def _sgu_gate_bwd(uvz, dyv, norm_g, w_s, w_sT, b_bc, name):
    S, E3 = uvz.shape
    E = E3 // 3
    T = SGU_BLOCK
    gd = E // SGU_GROUPS

    def body(uvz_ref, dyv_ref, ng_ref, ws_ref, wst_ref, bb_ref,
             d_ref, dng_ref, dws_ref, dbs_ref, vhat_scr, dv_scr):
        @pl.when(pl.program_id(0) == 0)
        def _():
            dng_ref[...] = jnp.zeros_like(dng_ref)
            dws_ref[...] = jnp.zeros_like(dws_ref)
            dbs_ref[...] = jnp.zeros_like(dbs_ref)

        gv = _gelu(uvz_ref[:, E:2 * E])
        mu = jnp.mean(gv, axis=-1, keepdims=True)
        xc = gv - mu
        rstd = lax.rsqrt(jnp.mean(xc * xc, axis=-1, keepdims=True) + NORM_EPS)
        vhat_scr[...] = xc * rstd
        mask = _chunk_mask()
        mask_t = _chunk_mask(transposed=True)
        for g in range(SGU_GROUPS):
            sl = slice(g * gd, (g + 1) * gd)
            u_pre = uvz_ref[:, sl]
            z = uvz_ref[:, 2 * E + g * gd:2 * E + (g + 1) * gd]
            dy = dyv_ref[:, sl]
            u = _gelu(u_pre)
            sig = jax.nn.sigmoid(z)
            sz = z * sig
            vg = (vhat_scr[:, sl] * ng_ref[:, sl]).astype(BF16)
            wg = jnp.where(mask, ws_ref[g], 0.0).astype(BF16)
            vm = lax.dot_general(wg, vg, NN, preferred_element_type=F32)
            vm = vm + jnp.tile(bb_ref[g], (1, gd // LANE))
            dy_u = dy * u
            d_ref[:, sl] = (dy * vm * sz * _gelu_grad(u_pre)).astype(BF16)
            d_ref[:, 2 * E + g * gd:2 * E + (g + 1) * gd] = (
                dy_u * vm * (sig * (1.0 + z * (1.0 - sig)))).astype(BF16)
            dvm = dy_u * sz
            dvm_b = dvm.astype(BF16)
            dws_ref[g] += jnp.where(mask, lax.dot_general(dvm_b, vg, NT, preferred_element_type=F32), 0.0)
            dbs_ref[g] += jnp.broadcast_to(jnp.sum(dvm, axis=1, keepdims=True), (T, LANE))
            wgt = jnp.where(mask_t, wst_ref[g], 0.0).astype(BF16)
            dv_scr[:, sl] = lax.dot_general(wgt, dvm_b, NN, preferred_element_type=F32)
        dv = dv_scr[...]
        vhat = vhat_scr[...]
        dng_ref[...] += jnp.sum(dv * vhat, axis=0, keepdims=True)
        dvh = dv * ng_ref[...]
        dgv = rstd * (dvh - jnp.mean(dvh, axis=-1, keepdims=True)
                      - vhat * jnp.mean(dvh * vhat, axis=-1, keepdims=True))
        d_ref[:, E:2 * E] = (dgv * _gelu_grad(uvz_ref[:, E:2 * E])).astype(BF16)

    wspec = pl.BlockSpec((SGU_GROUPS, T, T), lambda i: (0, 0, 0))
    bspec = pl.BlockSpec((SGU_GROUPS, T, LANE), lambda i: (0, 0, 0))
    return pl.pallas_call(
        body, name=name,
        out_shape=(jax.ShapeDtypeStruct((S, E3), BF16), jax.ShapeDtypeStruct((1, E), F32),
                   jax.ShapeDtypeStruct((SGU_GROUPS, T, T), F32),
                   jax.ShapeDtypeStruct((SGU_GROUPS, T, LANE), F32)),
        grid=(S // T,),
        in_specs=[_row_spec(T, E3), _row_spec(T, E), _vec_spec(E), wspec, wspec, bspec],
        out_specs=(_row_spec(T, E3), _vec_spec(E), wspec, bspec),
        scratch_shapes=[pltpu.VMEM((T, E), F32), pltpu.VMEM((T, E), F32)],
        compiler_params=_params("arbitrary"),
    )(uvz, dyv, norm_g, w_s, w_sT, b_bc)


MLA_WIDTH = HEADS * VDIM
P_LATENT = Q_RANK + KV_RANK + ROPE
P_WIDTH = P_LATENT + MLA_WIDTH


def _swap_halves(v):
    lane = lax.broadcasted_iota(jnp.int32, v.shape, 1)
    return jnp.where(lane % ROPE < ROPE // 2, pltpu.roll(v, LANE - ROPE // 2, 1), pltpu.roll(v, ROPE // 2, 1))


def _low_lanes(rows):
    return lax.broadcasted_iota(jnp.int32, (rows, LANE), 1) < ROPE


def _latent_tiles(ref):
    return [ref[:, t * LANE:(t + 1) * LANE] for t in range(P_LATENT // LANE)]


def _split_latents(tiles, low):
    cq = jnp.concatenate(tiles[0:3] + [jnp.where(low, tiles[3], 0.0)], axis=1)
    rolled = [pltpu.roll(t, ROPE, 1) for t in tiles[3:8]]
    ckv = jnp.concatenate([jnp.where(low, rolled[t], rolled[t + 1]) for t in range(4)], axis=1)
    kr = jnp.where(low, rolled[4], 0.0)
    return cq, ckv, kr


def _mla_mid_fwd(p, qg, kvg, name):
    S, PW = p.shape
    ts = min(256, S)

    def body(p_ref, qg_ref, kvg_ref, cqn_ref, ckvn_ref):
        cq, ckv, _ = _split_latents(_latent_tiles(p_ref), _low_lanes(ts))
        r = lax.rsqrt(jnp.sum(cq * cq, axis=-1, keepdims=True) * (1.0 / Q_RANK) + NORM_EPS)
        cqn_ref[...] = (cq * r * qg_ref[...]).astype(BF16)
        r2 = lax.rsqrt(jnp.mean(ckv * ckv, axis=-1, keepdims=True) + NORM_EPS)
        ckvn_ref[...] = (ckv * r2 * kvg_ref[...]).astype(BF16)

    return pl.pallas_call(
        body, name=name,
        out_shape=(jax.ShapeDtypeStruct((S, Q_RANK_PAD), BF16), jax.ShapeDtypeStruct((S, KV_RANK), BF16)),
        grid=(S // ts,),
        in_specs=[_row_spec(ts, P_LATENT), _vec_spec(Q_RANK_PAD), _vec_spec(KV_RANK)],
        out_specs=(_row_spec(ts, Q_RANK_PAD), _row_spec(ts, KV_RANK)),
        compiler_params=_params("parallel"),
    )(p, qg, kvg)


def _mla_pack(q, kv, p, cos_t, sin_t, name):
    S = q.shape[0]
    ts = min(256, S)
    pair_w = 2 * (NOPE + ROPE)
    head_w = NOPE + VDIM

    def body(q_ref, kv_ref, kr_ref, cos_ref, sin_ref, qo_ref, ko_ref, vo_ref):
        cosv = cos_ref[...]
        sinv = sin_ref[...]
        low = _low_lanes(ts)
        kr = jnp.where(low, pltpu.roll(kr_ref[...], ROPE, 1), 0.0)
        kr = (kr * cosv + _swap_halves(kr) * sinv).astype(BF16)
        for pair in range(HEADS // 2):
            t0, t1, t2 = (q_ref[:, pair * pair_w + t * LANE:pair * pair_w + (t + 1) * LANE] for t in range(3))
            nope_b = jnp.where(low, pltpu.roll(t1, ROPE, 1), pltpu.roll(t2, ROPE, 1))
            ropes = jnp.where(low, t1, t2)
            roped = ropes * cosv + _swap_halves(ropes) * sinv
            qo_ref[2 * pair, :, 0:NOPE] = t0.astype(BF16)
            qo_ref[2 * pair, :, NOPE:QK_PAD] = jnp.where(low, roped, 0.0).astype(BF16)
            qo_ref[2 * pair + 1, :, 0:NOPE] = nope_b.astype(BF16)
            qo_ref[2 * pair + 1, :, NOPE:QK_PAD] = jnp.where(low, pltpu.roll(roped, ROPE, 1), 0.0).astype(BF16)
        for h in range(HEADS):
            ko_ref[h, :, 0:NOPE] = kv_ref[:, h * head_w:h * head_w + NOPE].astype(BF16)
            ko_ref[h, :, NOPE:QK_PAD] = kr
            vo_ref[h] = kv_ref[:, h * head_w + NOPE:(h + 1) * head_w].astype(BF16)

    return pl.pallas_call(
        body, name=name,
        out_shape=(jax.ShapeDtypeStruct((HEADS, S, QK_PAD), BF16), jax.ShapeDtypeStruct((HEADS, S, QK_PAD), BF16),
                   jax.ShapeDtypeStruct((HEADS, S, VDIM), BF16)),
        grid=(S // ts,),
        in_specs=[_row_spec(ts, q.shape[1]), _row_spec(ts, kv.shape[1]),
                  pl.BlockSpec((ts, LANE), lambda i: (i, P_LATENT // LANE - 1)),
                  _row_spec(ts, LANE), _row_spec(ts, LANE)],
        out_specs=(pl.BlockSpec((HEADS, ts, QK_PAD), lambda i: (0, i, 0)),
                   pl.BlockSpec((HEADS, ts, QK_PAD), lambda i: (0, i, 0)),
                   pl.BlockSpec((HEADS, ts, VDIM), lambda i: (0, i, 0))),
        compiler_params=_params("parallel"),
    )(q, kv, p, cos_t, sin_t)


def _mla_unpack(dQ, dK, dV, cos_t, sin_t, name):
    S = dQ.shape[1]
    ts = min(256, S)
    pair_w = 2 * (NOPE + ROPE)
    head_w = NOPE + VDIM

    def body(dq_ref, dk_ref, dv_ref, cos_ref, sin_ref, q_ref, kv_ref, kr_ref):
        cosv = cos_ref[...]
        sinv = sin_ref[...]
        low = _low_lanes(ts)
        for pair in range(HEADS // 2):
            blk = dq_ref[2 * pair, :, NOPE:QK_PAD] + pltpu.roll(dq_ref[2 * pair + 1, :, NOPE:QK_PAD], ROPE, 1)
            ropes = blk * cosv - _swap_halves(blk) * sinv
            nope_b = pltpu.roll(dq_ref[2 * pair + 1, :, 0:NOPE], ROPE, 1)
            q_ref[:, pair * pair_w:pair * pair_w + LANE] = dq_ref[2 * pair, :, 0:NOPE].astype(BF16)
            q_ref[:, pair * pair_w + LANE:pair * pair_w + 2 * LANE] = jnp.where(low, ropes, nope_b).astype(BF16)
            q_ref[:, pair * pair_w + 2 * LANE:(pair + 1) * pair_w] = jnp.where(low, nope_b, ropes).astype(BF16)
        dkr = dk_ref[0, :, NOPE:QK_PAD]
        for h in range(1, HEADS):
            dkr = dkr + dk_ref[h, :, NOPE:QK_PAD]
        kr_ref[...] = dkr * cosv - _swap_halves(dkr) * sinv
        for h in range(HEADS):
            kv_ref[:, h * head_w:h * head_w + NOPE] = dk_ref[h, :, 0:NOPE].astype(BF16)
            kv_ref[:, h * head_w + NOPE:(h + 1) * head_w] = dv_ref[h].astype(BF16)

    return pl.pallas_call(
        body, name=name,
        out_shape=(jax.ShapeDtypeStruct((S, HEADS * (NOPE + ROPE)), BF16),
                   jax.ShapeDtypeStruct((S, HEADS * (NOPE + VDIM)), BF16),
                   jax.ShapeDtypeStruct((S, LANE), F32)),
        grid=(S // ts,),
        in_specs=[pl.BlockSpec((HEADS, ts, QK_PAD), lambda i: (0, i, 0)),
                  pl.BlockSpec((HEADS, ts, QK_PAD), lambda i: (0, i, 0)),
                  pl.BlockSpec((HEADS, ts, VDIM), lambda i: (0, i, 0)),
                  _row_spec(ts, LANE), _row_spec(ts, LANE)],
        out_specs=(_row_spec(ts, HEADS * (NOPE + ROPE)), _row_spec(ts, HEADS * (NOPE + VDIM)),
                   _row_spec(ts, LANE)),
        compiler_params=_params("parallel"),
    )(dQ, dK, dV, cos_t, sin_t)


def _mla_gate_fwd(o, p, name):
    S, W = o.shape
    ts = min(256, S)
    wb = P_LATENT

    def body(o_ref, z_ref, y_ref):
        z = z_ref[...]
        y_ref[...] = (o_ref[...] * (z * jax.nn.sigmoid(z))).astype(BF16)

    return pl.pallas_call(
        body, name=name, out_shape=jax.ShapeDtypeStruct((S, W), BF16), grid=(S // ts, W // wb),
        in_specs=[pl.BlockSpec((ts, wb), lambda i, j: (i, j)), pl.BlockSpec((ts, wb), lambda i, j: (i, j + 1))],
        out_specs=pl.BlockSpec((ts, wb), lambda i, j: (i, j)), compiler_params=_params("parallel", "parallel"),
    )(o, p)


def _mla_gate_bwd(dyv, p, name):
    S, W = dyv.shape
    ts = min(256, S)
    wb = P_LATENT

    def body(d_ref, z_ref, do_ref):
        z = z_ref[...]
        do_ref[...] = d_ref[...] * (z * jax.nn.sigmoid(z))

    return pl.pallas_call(
        body, name=name, out_shape=jax.ShapeDtypeStruct((S, W), F32), grid=(S // ts, W // wb),
        in_specs=[pl.BlockSpec((ts, wb), lambda i, j: (i, j)), pl.BlockSpec((ts, wb), lambda i, j: (i, j + 1))],
        out_specs=pl.BlockSpec((ts, wb), lambda i, j: (i, j)), compiler_params=_params("parallel", "parallel"),
    )(dyv, p)


def _mla_mid_bwd(p, dcqn, dckvn, dkr, dyv, o, qg, kvg, name):
    S, PW = p.shape
    W = o.shape[1]
    ts = min(256, S)
    nt = Q_RANK_PAD // LANE

    def rms_bwd(xv, dy, g, count):
        r = lax.rsqrt(jnp.sum(xv * xv, axis=-1, keepdims=True) * (1.0 / count) + NORM_EPS)
        xn = xv * r
        dg = jnp.sum(dy * xn, axis=0, keepdims=True)
        dxn = dy * g
        dx = r * (dxn - xn * (jnp.sum(dxn * xn, axis=-1, keepdims=True) * (1.0 / count)))
        return dx, dg

    def body(p_ref, dcq_ref, dckv_ref, dkr_ref, dyv_ref, o_ref, qg_ref, kvg_ref, dp_ref, dqg_ref, dkvg_ref):
        @pl.when(pl.program_id(0) == 0)
        def _():
            dqg_ref[...] = jnp.zeros_like(dqg_ref)
            dkvg_ref[...] = jnp.zeros_like(dkvg_ref)

        low = _low_lanes(ts)
        cq, ckv, _ = _split_latents(_latent_tiles(p_ref), low)
        dcq, dg = rms_bwd(cq, dcq_ref[...], qg_ref[...], Q_RANK)
        dqg_ref[...] += dg
        dckv, dg = rms_bwd(ckv, dckv_ref[...], kvg_ref[...], KV_RANK)
        dkvg_ref[...] += dg
        moved = [pltpu.roll(dckv[:, t * LANE:(t + 1) * LANE], ROPE, 1) for t in range(nt)]
        moved.append(pltpu.roll(dkr_ref[...], ROPE, 1))
        for t in range(nt - 1):
            dp_ref[:, t * LANE:(t + 1) * LANE] = dcq[:, t * LANE:(t + 1) * LANE].astype(BF16)
        dp_ref[:, (nt - 1) * LANE:nt * LANE] = jnp.where(low, dcq[:, (nt - 1) * LANE:nt * LANE], moved[0]).astype(BF16)
        for t in range(nt):
            dp_ref[:, (nt + t) * LANE:(nt + t + 1) * LANE] = jnp.where(low, moved[t], moved[t + 1]).astype(BF16)
        z = p_ref[:, P_LATENT:PW]
        sig = jax.nn.sigmoid(z)
        dp_ref[:, P_LATENT:PW] = (dyv_ref[...] * o_ref[...] * (sig * (1.0 + z * (1.0 - sig)))).astype(BF16)

    return pl.pallas_call(
        body, name=name,
        out_shape=(jax.ShapeDtypeStruct((S, PW), BF16), jax.ShapeDtypeStruct((1, Q_RANK_PAD), F32),
                   jax.ShapeDtypeStruct((1, KV_RANK), F32)),
        grid=(S // ts,),
        in_specs=[_row_spec(ts, PW), _row_spec(ts, Q_RANK_PAD), _row_spec(ts, KV_RANK), _row_spec(ts, LANE),
                  _row_spec(ts, W), _row_spec(ts, W), _vec_spec(Q_RANK_PAD), _vec_spec(KV_RANK)],
        out_specs=(_row_spec(ts, PW), _vec_spec(Q_RANK_PAD), _vec_spec(KV_RANK)),
        compiler_params=_params("arbitrary"),
    )(p, dcqn, dckvn, dkr, dyv, o, qg, kvg)


def _tile_mask(T):
    row = lax.broadcasted_iota(jnp.int32, (T, T), 0) // CHUNK
    col = lax.broadcasted_iota(jnp.int32, (T, T), 1) // CHUNK
    return col <= row


def _attn_fwd(Q, K, V, name):
    H, S, _ = Q.shape
    T = min(512, S)
    n_part = 2 if T % 256 == 0 else 1
    Tq = T // n_part

    def body(q_ref, k_ref, v_ref, o_ref, lse_ref, m_scr, l_scr, acc_scr):
        qi = pl.program_id(1)
        q = q_ref[...]
        m_scr[...] = jnp.full_like(m_scr, -jnp.inf)
        l_scr[...] = jnp.zeros_like(l_scr)
        acc_scr[...] = jnp.zeros_like(acc_scr)

        def tile(j, masked):
            rows = pl.ds(pl.multiple_of(j * T, T), T)
            kt = k_ref[rows, :]
            vt = v_ref[rows, :]
            for part in range(n_part):
                sub = slice(part * Tq, (part + 1) * Tq)
                s = lax.dot_general(q[sub], kt, NT, preferred_element_type=F32) * ATTN_SCALE
                if masked:
                    row = (lax.broadcasted_iota(jnp.int32, (Tq, T), 0) + part * Tq) // CHUNK
                    col = lax.broadcasted_iota(jnp.int32, (Tq, T), 1) // CHUNK
                    s = jnp.where(col <= row, s, -1e30)
                m_prev = m_scr[sub]
                m_new = jnp.maximum(m_prev, jnp.max(s, axis=1, keepdims=True))
                pr = jnp.exp(s - m_new)
                alpha = jnp.exp(m_prev - m_new)
                l_scr[sub] = alpha * l_scr[sub] + jnp.sum(pr, axis=1, keepdims=True)
                acc_scr[sub] = alpha * acc_scr[sub] + lax.dot_general(
                    pr.astype(BF16), vt, NN, preferred_element_type=F32)
                m_scr[sub] = m_new

        def full_tile(j, carry):
            tile(j, False)
            return carry

        lax.fori_loop(0, qi, full_tile, 0)
        tile(qi, True)
        l = l_scr[...]
        o_ref[...] = acc_scr[...] / l
        lse_ref[...] = jnp.broadcast_to(m_scr[...] + jnp.log(l), (T, LANE))

    return pl.pallas_call(
        body, name=name,
        out_shape=(jax.ShapeDtypeStruct((S, H * VDIM), F32), jax.ShapeDtypeStruct((H, S, LANE), F32)),
        grid=(H, S // T),
        in_specs=[pl.BlockSpec((None, T, QK_PAD), lambda h, i: (h, i, 0)),
                  pl.BlockSpec((None, S, QK_PAD), lambda h, i: (h, 0, 0)),
                  pl.BlockSpec((None, S, VDIM), lambda h, i: (h, 0, 0))],
        out_specs=(pl.BlockSpec((T, VDIM), lambda h, i: (i, h)),
                   pl.BlockSpec((None, T, LANE), lambda h, i: (h, i, 0))),
        scratch_shapes=[pltpu.VMEM((T, 1), F32), pltpu.VMEM((T, 1), F32), pltpu.VMEM((T, VDIM), F32)],
        compiler_params=_params("parallel", "arbitrary"),
    )(Q, K, V)


def _attn_bwd(Q, K, V, o, do, lse, name):
    H, S, _ = Q.shape
    T = min(512, S)
    nq = S // T

    def body(q_ref, k_ref, v_ref, o_ref, do_ref, lse_ref, dq_ref, dk_ref, dv_ref, dk_scr, dv_scr):
        ki = pl.program_id(1)

        @pl.when(ki == 0)
        def _():
            dq_ref[...] = jnp.zeros_like(dq_ref)

        dk_scr[...] = jnp.zeros_like(dk_scr)
        dv_scr[...] = jnp.zeros_like(dv_scr)
        k = k_ref[...]
        v = v_ref[...]

        def tile(i, masked):
            rows = pl.ds(pl.multiple_of(i * T, T), T)
            q = q_ref[rows, :]
            do_f = do_ref[rows, :]
            do_b = do_f.astype(BF16)
            delta = jnp.sum(do_f * o_ref[rows, :], axis=1, keepdims=True)
            s = lax.dot_general(q, k, NT, preferred_element_type=F32) * ATTN_SCALE
            pr = jnp.exp(s - lse_ref[rows, 0:1])
            if masked:
                pr = jnp.where(_tile_mask(T), pr, 0.0)
            dv_scr[...] += lax.dot_general(pr.astype(BF16), do_b, TN, preferred_element_type=F32)
            dp = lax.dot_general(do_b, v, NT, preferred_element_type=F32)
            ds = (pr * (dp - delta) * ATTN_SCALE).astype(BF16)
            dk_scr[...] += lax.dot_general(ds, q, TN, preferred_element_type=F32)
            dq_ref[rows, :] += lax.dot_general(ds, k, NN, preferred_element_type=F32)

        def full_tile(i, carry):
            tile(i, False)
            return carry

        tile(ki, True)
        lax.fori_loop(ki + 1, nq, full_tile, 0)
        dk_ref[...] = dk_scr[...]
        dv_ref[...] = dv_scr[...]

    return pl.pallas_call(
        body, name=name,
        out_shape=(jax.ShapeDtypeStruct((H, S, QK_PAD), F32), jax.ShapeDtypeStruct((H, S, QK_PAD), F32),
                   jax.ShapeDtypeStruct((H, S, VDIM), F32)),
        grid=(H, nq),
        in_specs=[pl.BlockSpec((None, S, QK_PAD), lambda h, j: (h, 0, 0)),
                  pl.BlockSpec((None, T, QK_PAD), lambda h, j: (h, j, 0)),
                  pl.BlockSpec((None, T, VDIM), lambda h, j: (h, j, 0)),
                  pl.BlockSpec((S, VDIM), lambda h, j: (0, h)),
                  pl.BlockSpec((S, VDIM), lambda h, j: (0, h)),
                  pl.BlockSpec((None, S, LANE), lambda h, j: (h, 0, 0))],
        out_specs=(pl.BlockSpec((None, S, QK_PAD), lambda h, j: (h, 0, 0)),
                   pl.BlockSpec((None, T, QK_PAD), lambda h, j: (h, j, 0)),
                   pl.BlockSpec((None, T, VDIM), lambda h, j: (h, j, 0))),
        scratch_shapes=[pltpu.VMEM((T, QK_PAD), F32), pltpu.VMEM((T, VDIM), F32)],
        compiler_params=_params("parallel", "arbitrary"),
    )(Q, K, V, o, do, lse)


def _adamw(w, g, m, v, name):
    shape = w.shape
    C = shape[-1]
    R = math.prod(shape[:-1])
    flat = [t.reshape(R, C) for t in (w, g, m, v)]
    tr = _row_tile(R, C * 4)

    def body(w_ref, g_ref, m_ref, v_ref, d_ref, nm_ref, nv_ref):
        gv = g_ref[...]
        m_new = ADAM_B1 * m_ref[...] + (1.0 - ADAM_B1) * gv
        v_new = ADAM_B2 * v_ref[...] + (1.0 - ADAM_B2) * jnp.square(gv)
        m_hat = m_new / (1.0 - ADAM_B1 ** ADAM_STEP)
        v_hat = v_new / (1.0 - ADAM_B2 ** ADAM_STEP)
        d_ref[...] = -ADAM_LR * (m_hat / (jnp.sqrt(v_hat) + ADAM_EPS) + ADAM_WD * w_ref[...])
        nm_ref[...] = m_new
        nv_ref[...] = v_new

    spec = pl.BlockSpec((tr, C), lambda i: (i, 0))
    out = jax.ShapeDtypeStruct((R, C), F32)
    d, nm, nv = pl.pallas_call(
        body, name=name, out_shape=(out, out, out), grid=(R // tr,),
        in_specs=[spec] * 4, out_specs=(spec, spec, spec), compiler_params=_params("parallel"),
    )(*flat)
    return d.reshape(shape), nm.reshape(shape), nv.reshape(shape)


def _sum_into_half(r, buf, layer, ci, n_layers, name):
    n, M, N = r.shape
    tr = _row_tile(M, N * 4 * n, 4 << 20)

    def body(c_ref, r_ref, *rest):
        o_ref = rest[-1]
        acc = r_ref[0].astype(F32)
        for s in range(1, n):
            acc = acc + r_ref[s].astype(F32)
        o_ref[...] = acc

    in_specs = [pl.BlockSpec((n, tr, N), lambda i, c: (0, i, 0))]
    operands = [ci.reshape(1), r]
    aliases = {}
    if buf is not None:
        in_specs.append(ANY)
        operands.append(buf)
        aliases = {2: 0}
    return pl.pallas_call(
        body, name=name, out_shape=jax.ShapeDtypeStruct((n_layers, 2, M, N), F32),
        grid_spec=pltpu.PrefetchScalarGridSpec(
            num_scalar_prefetch=1, grid=(M // tr,), in_specs=in_specs,
            out_specs=pl.BlockSpec((None, None, tr, N), lambda i, c: (layer, c[0], i, 0))),
        input_output_aliases=aliases, compiler_params=_params("parallel"),
    )(*operands)


def _sum_slots(r, name):
    n, M, N = r.shape
    tr = _row_tile(M, N * 4 * n, 4 << 20)

    def body(r_ref, o_ref):
        acc = r_ref[0].astype(F32)
        for s in range(1, n):
            acc = acc + r_ref[s].astype(F32)
        o_ref[...] = acc

    return pl.pallas_call(
        body, name=name, out_shape=jax.ShapeDtypeStruct((M, N), F32), grid=(M // tr,),
        in_specs=[pl.BlockSpec((n, tr, N), lambda i: (0, i, 0))],
        out_specs=pl.BlockSpec((tr, N), lambda i: (i, 0)), compiler_params=_params("parallel"),
    )(r)


ANY = pl.BlockSpec(memory_space=pl.ANY)
DMA_CHUNK_BYTES = 1 << 20
DMA_MAX_CHUNKS = 16
PEER_ORDER = (1, 4, 5, 2, 3, 6, 7)


def _position():
    return lax.axis_index("x"), lax.axis_index("y"), lax.axis_index("c")


def _row_chunks(shape, dtype):
    rows, cols = shape
    n = max(1, min(DMA_MAX_CHUNKS, rows * cols * jnp.dtype(dtype).itemsize // DMA_CHUNK_BYTES))
    while n > 1 and (rows % n or (rows // n) % 16):
        n -= 1
    step = rows // n
    return [pl.ds(q * step, step) for q in range(n)]


def _all_gather8(xs, name):
    n = len(xs)

    def body(*refs):
        x_refs, o_refs = refs[:n], refs[n:2 * n]
        send_sems, recv_sems, local_sems = refs[2 * n:]
        x, y, c = _position()
        me, sibling = (x, y, c), (x, y, 1 - c)
        chips = [(1 - x, y), (x, 1 - y), (1 - x, 1 - y)]

        def slot(a, dev, rows):
            return o_refs[a].at[4 * dev[0] + 2 * dev[1] + dev[2], rows]

        def copy(a, k, block, to, rows, from_input=False):
            return pltpu.make_async_remote_copy(
                src_ref=x_refs[a].at[rows] if from_input else slot(a, block, rows), dst_ref=slot(a, block, rows),
                send_sem=send_sems.at[a, k], recv_sem=recv_sems.at[a, k],
                device_id=to, device_id_type=MESH)

        def mine(a, rows):
            return pltpu.make_async_copy(x_refs[a].at[rows], slot(a, me, rows), local_sems.at[a])

        chunks = [_row_chunks(t.shape, t.dtype) for t in xs]
        whole = [pl.ds(0, t.shape[0]) for t in xs]
        for a in range(n):
            for rows in chunks[a]:
                mine(a, rows).start()
        sent = []
        for a in range(n):
            for k, to in enumerate([sibling] + [(*chip, c) for chip in chips]):
                for rows in chunks[a]:
                    copy(a, k, me, to, rows, from_input=True).start()
                sent.append(copy(a, k, me, to, whole[a], from_input=True))
        for a in range(n):
            for j, chip in enumerate(chips):
                copy(a, 1 + j, (*chip, c), me, whole[a]).wait_recv()
                for rows in chunks[a]:
                    copy(a, 4 + j, (*chip, c), sibling, rows).start()
                sent.append(copy(a, 4 + j, (*chip, c), sibling, whole[a]))
        for a in range(n):
            copy(a, 0, sibling, me, whole[a]).wait_recv()
            for j, chip in enumerate(chips):
                copy(a, 4 + j, (*chip, 1 - c), me, whole[a]).wait_recv()
        for cp in sent:
            cp.wait_send()
        for a in range(n):
            mine(a, whole[a]).wait()

    return pl.pallas_call(
        body, name=name,
        out_shape=[jax.ShapeDtypeStruct((8,) + t.shape, t.dtype) for t in xs],
        in_specs=[ANY] * n, out_specs=[ANY] * n,
        scratch_shapes=[pltpu.SemaphoreType.DMA((n, 7)), pltpu.SemaphoreType.DMA((n, 7)),
                        pltpu.SemaphoreType.DMA((n,))],
    )(*xs)


def _exchange8(gs, name):
    n = len(gs)

    def body(*refs):
        g_refs, r_refs = refs[:n], refs[n:2 * n]
        send_sems, recv_sems, local_sems = refs[2 * n:]
        x, y, c = _position()
        my = 4 * x + 2 * y + c

        def mine(a, rows):
            return pltpu.make_async_copy(g_refs[a].at[my, rows], r_refs[a].at[my, rows], local_sems.at[a])

        def copy(a, m, rows):
            px = (1 - x) if m & 4 else x
            py = (1 - y) if m & 2 else y
            pc = (1 - c) if m & 1 else c
            return pltpu.make_async_remote_copy(
                src_ref=g_refs[a].at[4 * px + 2 * py + pc, rows], dst_ref=r_refs[a].at[my, rows],
                send_sem=send_sems.at[a, m - 1], recv_sem=recv_sems.at[a, m - 1],
                device_id=(px, py, pc), device_id_type=MESH)

        chunks = [_row_chunks(t.shape[1:], t.dtype) for t in gs]
        whole = [pl.ds(0, t.shape[1]) for t in gs]
        for a in range(n):
            for rows in chunks[a]:
                mine(a, rows).start()
        for a in range(n):
            for m in PEER_ORDER:
                for rows in chunks[a]:
                    copy(a, m, rows).start()
        for a in range(n):
            for m in PEER_ORDER:
                copy(a, m, whole[a]).wait_recv()
        for a in range(n):
            for m in PEER_ORDER:
                copy(a, m, whole[a]).wait_send()
            mine(a, whole[a]).wait()

    return pl.pallas_call(
        body, name=name,
        out_shape=[jax.ShapeDtypeStruct(t.shape, t.dtype) for t in gs],
        in_specs=[ANY] * n, out_specs=[ANY] * n,
        scratch_shapes=[pltpu.SemaphoreType.DMA((n, 7)), pltpu.SemaphoreType.DMA((n, 7)),
                        pltpu.SemaphoreType.DMA((n,))],
    )(*gs)


HBM = pl.BlockSpec(memory_space=pltpu.HBM)
SEM = pl.BlockSpec(memory_space=pltpu.SEMAPHORE)
EFFECT = pltpu.SideEffectType.DATAFLOW_SIDE_EFFECTING


def _peer(m, x, y, c):
    return ((1 - x) if m & 4 else x, (1 - y) if m & 2 else y, (1 - c) if m & 1 else c)


def _send_copies(src_refs, land_refs, send_sems, recv_sems, broadcast):
    x, y, c = _position()
    my = 4 * x + 2 * y + c
    out = []
    for a in range(len(src_refs)):
        for m in PEER_ORDER:
            px, py, pc = _peer(m, x, y, c)
            src = src_refs[a] if broadcast else src_refs[a].at[4 * px + 2 * py + pc]
            out.append(pltpu.make_async_remote_copy(
                src_ref=src, dst_ref=land_refs[a].at[my], send_sem=send_sems[a], recv_sem=recv_sems[a],
                device_id=(px, py, pc), device_id_type=MESH))
    return out


def _send_drain(land_refs, send_sems, recv_sems):
    x, y, c = _position()
    for a in range(len(land_refs)):
        seven = land_refs[a].at[pl.ds(0, 7)]
        both = pltpu.make_async_remote_copy(
            src_ref=seven, dst_ref=seven, send_sem=send_sems[a], recv_sem=recv_sems[a],
            device_id=(x, y, c), device_id_type=MESH)
        both.wait_send()
        both.wait_recv()


def _send_start(srcs, lands, after, broadcast, name):
    n = len(srcs)
    extra = [] if after is None else [after]

    def body(*refs):
        src_refs, land_refs = refs[:n], refs[n:2 * n]
        outs = refs[2 * n + len(extra):]
        send_sems, recv_sems = outs[:n], outs[n:2 * n]
        token = refs[-1]
        for cp in _send_copies(src_refs, land_refs, send_sems, recv_sems, broadcast):
            cp.start()
        token[...] = jnp.zeros_like(token)

    hbm = [pltpu.with_memory_space_constraint(t, pltpu.HBM) for t in list(srcs) + list(lands)]
    res = pl.pallas_call(
        body, name=name,
        out_shape=(*[pltpu.SemaphoreType.DMA(())] * (2 * n),
                   *[pltpu.HBM(t.shape, t.dtype) for t in hbm], jax.ShapeDtypeStruct((8, LANE), F32)),
        in_specs=[HBM] * (2 * n) + [ANY] * len(extra),
        out_specs=(*[SEM] * (2 * n), *[HBM] * (2 * n), pl.BlockSpec(memory_space=pltpu.VMEM)),
        input_output_aliases={i: 2 * n + i for i in range(2 * n)},
        compiler_params=pltpu.CompilerParams(has_side_effects=EFFECT),
    )(*hbm, *extra)
    return dict(sems=res[:2 * n], srcs=res[2 * n:3 * n], lands=res[3 * n:4 * n], token=res[-1], broadcast=broadcast)


def _send_wait(started, after, name):
    n = len(started["srcs"])

    def body(*refs):
        land_refs = refs[n:2 * n]
        send_sems, recv_sems = refs[2 * n:3 * n], refs[3 * n:4 * n]
        _send_drain(land_refs, send_sems, recv_sems)

    operands = list(started["srcs"]) + list(started["lands"])
    res = pl.pallas_call(
        body, name=name,
        out_shape=[pltpu.HBM(t.shape, t.dtype) for t in operands],
        in_specs=[HBM] * (2 * n) + [SEM] * (2 * n) + [ANY],
        out_specs=[HBM] * (2 * n),
        input_output_aliases={i: i for i in range(2 * n)},
        compiler_params=pltpu.CompilerParams(has_side_effects=EFFECT),
    )(*operands, *started["sems"], after)
    return res[n:]


def _own_slot(block, dev):
    zone = lax.empty((8,) + block.shape, block.dtype)
    return lax.dynamic_update_slice(zone, block[None], (dev, 0, 0))


def _pair_swap(bufs, name):
    n = len(bufs)
    pieces = [(a, l) for a, t in enumerate(bufs) for l in range(t.shape[0])]

    def body(*refs):
        b_refs = refs[n:2 * n]
        send_sems, recv_sems = refs[2 * n:]
        x, y, c = _position()

        def copy(k, rows):
            a, l = pieces[k]
            half = b_refs[a].at[l, c, rows]
            return pltpu.make_async_remote_copy(
                src_ref=half, dst_ref=half, send_sem=send_sems.at[k], recv_sem=recv_sems.at[k],
                device_id=(x, y, 1 - c), device_id_type=MESH)

        chunks = [_row_chunks(bufs[a].shape[2:], bufs[a].dtype) for a, _ in pieces]
        whole = [pl.ds(0, bufs[a].shape[2]) for a, _ in pieces]
        for k in range(len(pieces)):
            for rows in chunks[k]:
                copy(k, rows).start()
        for k in range(len(pieces)):
            copy(k, whole[k]).wait_recv()
        for k in range(len(pieces)):
            copy(k, whole[k]).wait_send()

    return pl.pallas_call(
        body, name=name,
        out_shape=[jax.ShapeDtypeStruct(t.shape, t.dtype) for t in bufs],
        in_specs=[ANY] * n, out_specs=[ANY] * n,
        input_output_aliases={a: a for a in range(n)},
        scratch_shapes=[pltpu.SemaphoreType.DMA((len(pieces),)), pltpu.SemaphoreType.DMA((len(pieces),))],
    )(*bufs)


def _pack_rows(parts):
    flat = jnp.concatenate([t.reshape(-1).astype(F32) for t in parts])
    pad = (-flat.shape[0]) % (256 * LANE)
    return jnp.pad(flat, (0, pad)).reshape(-1, LANE)


def _my_half(w2d, ci):
    half = w2d.shape[0] // 2
    return lax.dynamic_slice_in_dim(w2d, ci * half, half, axis=0).astype(BF16)


def _col_view(g):
    _, half, Cs = g.shape
    return g.reshape(4, 2 * half, Cs)


def _row_view(g):
    _, half, C = g.shape
    return g.reshape(8 * half, C)


def _rope_tables(S):
    pos = jnp.arange(S, dtype=F32)
    inv_freq = ROPE_THETA ** (-jnp.arange(0, ROPE, 2, dtype=F32) / ROPE)
    ang = pos[:, None] * inv_freq[None, :]
    cos, sin = jnp.cos(ang), jnp.sin(ang)
    cos_t = jnp.concatenate([cos, cos, cos, cos], axis=1)
    sin_t = jnp.concatenate([-sin, sin, -sin, sin], axis=1)
    return cos_t, sin_t


def kernel(x, c, ada_w, ada_b, pre_g, post_g, sgu_w_in, sgu_norm_g, sgu_w_s, sgu_b_s, sgu_w_out, mla_w_in, mla_q_norm_g, mla_kv_norm_g, mla_w_uq, mla_w_ukv, mla_w_out, loss_target, m_ada_w, m_ada_b, m_pre_g, m_post_g, m_sgu_w_in, m_sgu_norm_g, m_sgu_w_s, m_sgu_b_s, m_sgu_w_out, m_mla_w_in, m_mla_q_norm_g, m_mla_kv_norm_g, m_mla_w_uq, m_mla_w_ukv, m_mla_w_out, v_ada_w, v_ada_b, v_pre_g, v_post_g, v_sgu_w_in, v_sgu_norm_g, v_sgu_w_s, v_sgu_b_s, v_sgu_w_out, v_mla_w_in, v_mla_q_norm_g, v_mla_kv_norm_g, v_mla_w_uq, v_mla_w_ukv, v_mla_w_out):
    S, D = x.shape[1], x.shape[2]
    depth = ada_w.shape[0]
    E = sgu_w_out.shape[1] * 4
    xi, yi, ci = _position()
    chip = 2 * xi + yi
    dev = 4 * xi + 2 * yi + ci
    x0 = x.reshape(S, D)
    target = loss_target.reshape(S, D)

    small = _pack_rows([c, mla_q_norm_g, mla_kv_norm_g])
    mixer_w = dict(sin=sgu_w_in, sout=sgu_w_out, min=mla_w_in, uq=mla_w_uq, ukv=mla_w_ukv, mout=mla_w_out)
    small_g, first_g = _all_gather8([small, _my_half(sgu_w_in[0], ci)], "gather_first")
    small_all = small_g.reshape(8, -1)
    groups = [("sout0", [("sout", 0)]), ("mla0", [(t, 0) for t in ("min", "uq", "ukv", "mout")]),
              ("sgu1", [("sin", 1), ("sout", 1)]), ("mla1", [(t, 1) for t in ("min", "uq", "ukv", "mout")])]
    sends = {}
    behind = small_g
    for gname, items in groups:
        blocks = [_my_half(mixer_w[t][j], ci) for t, j in items]
        sends[gname] = _send_start(blocks, [_own_slot(b, dev) for b in blocks], behind, True, f"send_{gname}")
        behind = sends[gname]["token"]
    gathered_w = {("sin", 0): first_g}

    def arrive(gname, after):
        lands = _send_wait(sends[gname], after, f"arrive_{gname}")
        gathered_w.update(zip(dict(groups)[gname], lands))
    qn_w, kvn_w = mla_q_norm_g.shape[1], mla_kv_norm_g.shape[1]
    c_all = small_all[:, :D]
    qn_all = small_all[0::2, D:D + 2 * qn_w].reshape(4, 2, qn_w)
    kvn_all = small_all[0::2, D + 2 * qn_w:D + 2 * qn_w + 2 * kvn_w].reshape(4, 2, kvn_w)
    q_gain = jnp.pad(jnp.transpose(qn_all, (1, 0, 2)).reshape(2, 1, Q_RANK), ((0, 0), (0, 0), (0, Q_RANK_PAD - Q_RANK)))
    kv_gain = jnp.transpose(kvn_all, (1, 0, 2)).reshape(2, 1, KV_RANK)

    views = {}

    def weight(t, j):
        if (t, j) not in views:
            g = gathered_w[(t, j)]
            v = _row_view(g) if t in ("sout", "mout") else _col_view(g)
            if t == "uq":
                v = jnp.pad(v, ((0, 0), (0, Q_RANK_PAD - Q_RANK), (0, 0)))
            views[(t, j)] = v
        return views[(t, j)]

    cols = ada_w.shape[2]
    ada_b_cols = lax.dynamic_slice_in_dim(ada_b, chip * cols, cols, axis=1)
    c_pad = jnp.pad(c_all, ((0, 8), (0, 0)))
    mod_cols = _ada_mod(c_pad, ada_w, ada_b_cols, "ada_mod")[:, :8]
    mod_all, = _all_gather8([mod_cols.reshape(depth * 8, cols)], "gather_mod")
    mod_all = jnp.transpose(mod_all[0::2].reshape(4, depth, 8, cols), (1, 2, 0, 3)).reshape(depth, 8, 4 * cols)
    mod = lax.dynamic_index_in_dim(mod_all, dev, 1, keepdims=False)
    shift = [mod[i:i + 1, :D] for i in range(depth)]
    scale = [mod[i:i + 1, D:2 * D] for i in range(depth)]
    gate = [mod[i:i + 1, 2 * D:] for i in range(depth)]

    cos_t, sin_t = _rope_tables(S)
    b_bc = jnp.broadcast_to(sgu_b_s[:, :, :, None], sgu_b_s.shape + (LANE,))
    w_sT = jnp.swapaxes(sgu_w_s, 2, 3)

    saved = []
    xs = x0
    for i in range(depth):
        j = i // 2
        tag = f"l{i}"
        h = _pre_fwd(xs, pre_g[i:i + 1], scale[i], shift[i], f"pre_fwd_{tag}", after=(behind,) if i == 0 else ())
        if i % 2 == 0:
            if j > 0:
                arrive(f"sgu{j}", h)
            uvz = _mm(h, weight("sin", j), b_sharded=True, name=f"sgu_in_{tag}")
            y = _sgu_gate_fwd(uvz, sgu_norm_g[j:j + 1], sgu_w_s[j], b_bc[j], f"sgu_gate_fwd_{tag}")
            if j == 0:
                arrive("sout0", y)
            out = _mm(y, weight("sout", j), name=f"sgu_out_{tag}")
            saved.append(dict(x=xs, h=h, uvz=uvz, y=y, out=out))
        else:
            arrive(f"mla{j}", h)
            p = _mm(h, weight("min", j), b_sharded=True, name=f"mla_in_{tag}")
            cqn, ckvn = _mla_mid_fwd(p, q_gain[j], kv_gain[j], f"mla_mid_fwd_{tag}")
            q = _mm(cqn, weight("uq", j), b_sharded=True, name=f"mla_uq_{tag}")
            kv = _mm(ckvn, weight("ukv", j), b_sharded=True, name=f"mla_ukv_{tag}")
            Q, K, V = _mla_pack(q, kv, p, cos_t, sin_t, f"mla_pack_{tag}")
            o, lse = _attn_fwd(Q, K, V, f"attn_fwd_{tag}")
            y = _mla_gate_fwd(o, p, f"mla_gate_fwd_{tag}")
            out = _mm(y, weight("mout", j), name=f"mla_out_{tag}")
            saved.append(dict(x=xs, h=h, p=p, cqn=cqn, ckvn=ckvn, Q=Q, K=K, V=V, o=o, lse=lse, y=y, out=out))
        xs = _post_fwd(xs, out, gate[i], post_g[i:i + 1], f"post_fwd_{tag}")

    dx, loss_part = _loss_grad(xs, target, "loss")
    loss = lax.psum(loss_part[0, 0], ("x", "y", "c"))

    dmod = [None] * depth
    d_pre_g = [None] * depth
    d_post_g = [None] * depth
    d_sgu = [None] * 2
    d_mla = [None] * 2
    kinds = ("sgu_w_in", "sgu_w_out", "mla_w_in", "mla_w_uq", "mla_w_ukv", "mla_w_out")
    halves = dict.fromkeys(kinds)
    in_flight = []

    def send_grads(items, label):
        slices = [dw.reshape(8, -1, dw.shape[-1]) for _, _, dw in items]
        lands = [_own_slot(lax.dynamic_index_in_dim(s, dev, 0, keepdims=False), dev) for s in slices]
        started = _send_start(slices, lands, None, False, f"send_{label}")
        in_flight.append((started, [(kind, layer) for kind, layer, _ in items], label))

    def collect(count, after):
        for _ in range(count):
            started, keys, label = in_flight.pop(0)
            lands = _send_wait(started, after, f"arrive_{label}")
            for (kind, layer), r in zip(keys, lands):
                halves[kind] = _sum_into_half(r, halves[kind], layer, ci, 2, f"sum_{kind}_{layer}")

    for i in reversed(range(depth)):
        j = i // 2
        tag = f"l{i}"
        sv = saved[i]
        older = len(in_flight)
        dy, dgate, d_post_g[i] = _post_bwd(dx, sv["out"], gate[i], post_g[i:i + 1], f"post_bwd_{tag}")
        if i % 2 == 0:
            dw_out = _mm(sv["y"], dy, ta=True, out_dtype=BF16, name=f"sgu_out_dw_{tag}")
            send_grads([("sgu_w_out", j, dw_out)], f"{tag}_out")
            dyv = _mm(dy, weight("sout", j), tb=True, name=f"sgu_out_dx_{tag}")
            duvz, dng, dws, dbs = _sgu_gate_bwd(sv["uvz"], dyv, sgu_norm_g[j:j + 1], sgu_w_s[j], w_sT[j], b_bc[j],
                                                f"sgu_gate_bwd_{tag}")
            dw_in = _mm(sv["h"], duvz, ta=True, out_sharded=True, out_dtype=BF16, name=f"sgu_in_dw_{tag}")
            send_grads([("sgu_w_in", j, dw_in)], f"{tag}_in")
            dh = _mm(duvz, weight("sin", j), tb=True, b_sharded=True, name=f"sgu_in_dx_{tag}")
            d_sgu[j] = dict(norm_g=dng, w_s=dws, b_s=dbs[:, :, 0])
        else:
            dw_out = _mm(sv["y"], dy, ta=True, out_dtype=BF16, name=f"mla_out_dw_{tag}")
            send_grads([("mla_w_out", j, dw_out)], f"{tag}_out")
            dyv = _mm(dy, weight("mout", j), tb=True, name=f"mla_out_dx_{tag}")
            p = sv["p"]
            do = _mla_gate_bwd(dyv, p, f"mla_gate_bwd_{tag}")
            dQ, dK, dV = _attn_bwd(sv["Q"], sv["K"], sv["V"], sv["o"], do, sv["lse"], f"attn_bwd_{tag}")
            dq, dkv, dkr = _mla_unpack(dQ, dK, dV, cos_t, sin_t, f"mla_unpack_{tag}")
            dw_uq = _mm(sv["cqn"], dq, ta=True, out_sharded=True, out_dtype=BF16, name=f"mla_uq_dw_{tag}")
            dcqn = _mm(dq, weight("uq", j), tb=True, b_sharded=True, name=f"mla_uq_dx_{tag}")
            dw_ukv = _mm(sv["ckvn"], dkv, ta=True, out_sharded=True, out_dtype=BF16, name=f"mla_ukv_dw_{tag}")
            dckvn = _mm(dkv, weight("ukv", j), tb=True, b_sharded=True, name=f"mla_ukv_dx_{tag}")
            dp, dqg, dkvg = _mla_mid_bwd(p, dcqn, dckvn, dkr, dyv, sv["o"], q_gain[j], kv_gain[j], f"mla_mid_bwd_{tag}")
            dw_in = _mm(sv["h"], dp, ta=True, out_sharded=True, out_dtype=BF16, name=f"mla_in_dw_{tag}")
            send_grads([("mla_w_in", j, dw_in), ("mla_w_uq", j, dw_uq[:, :Q_RANK]), ("mla_w_ukv", j, dw_ukv)],
                       f"{tag}_in")
            dh = _mm(dp, weight("min", j), tb=True, b_sharded=True, name=f"mla_in_dx_{tag}")
            d_mla[j] = dict(qg=dqg[0, :Q_RANK], kvg=dkvg[0])
        dx, dshift, dscale, d_pre_g[i] = _pre_bwd(dh, sv["x"], dx, pre_g[i:i + 1], scale[i], f"pre_bwd_{tag}")
        dmod[i] = jnp.concatenate([dshift, dscale, dgate], axis=1)
        collect(older, dx)
    grad_x = dx.reshape(x.shape)

    parts = [jnp.concatenate(dmod, axis=0), jnp.concatenate(d_pre_g, axis=0), jnp.concatenate(d_post_g, axis=0),
             jnp.stack([d["norm_g"][0] for d in d_sgu]), jnp.stack([d["w_s"] for d in d_sgu]),
             jnp.stack([d["b_s"] for d in d_sgu]), jnp.stack([d["qg"] for d in d_mla]),
             jnp.stack([d["kvg"] for d in d_mla])]
    sizes = [int(np.prod(t.shape)) for t in parts]
    packed = _pack_rows(parts)
    packed_all, dmod_all = _all_gather8([packed, parts[0]], "gather_small_grads")
    total = _sum_slots(packed_all, "sum_small_grads").reshape(-1)
    offs = np.concatenate([[0], np.cumsum(sizes)])
    pieces = [total[int(offs[t]):int(offs[t + 1])].reshape(parts[t].shape) for t in range(len(parts))]
    g_ada_b, g_pre_g, g_post_g, g_norm_g, g_w_s, g_b_s, g_qg_full, g_kvg_full = pieces
    g_qg = lax.dynamic_slice_in_dim(g_qg_full, chip * qn_w, qn_w, axis=1)
    g_kvg = lax.dynamic_slice_in_dim(g_kvg_full, chip * kvn_w, kvn_w, axis=1)
    dmod_cols = jnp.stack([lax.dynamic_slice_in_dim(dmod_all[:, i], chip * cols, cols, axis=1) for i in range(depth)])
    dmod_cols = jnp.pad(dmod_cols, ((0, 0), (0, LANE - 8), (0, 0)))
    g_ada_w = _ada_grad(jnp.pad(c_all.T, ((0, 0), (0, LANE - 8))), dmod_cols, "ada_grad")

    wnames = ["ada_w", "ada_b", "pre_g", "post_g", "sgu_w_in", "sgu_norm_g", "sgu_w_s", "sgu_b_s", "sgu_w_out",
              "mla_w_in", "mla_q_norm_g", "mla_kv_norm_g", "mla_w_uq", "mla_w_ukv", "mla_w_out"]
    weights = dict(zip(wnames, [ada_w, ada_b, pre_g, post_g, sgu_w_in, sgu_norm_g, sgu_w_s, sgu_b_s, sgu_w_out,
                                mla_w_in, mla_q_norm_g, mla_kv_norm_g, mla_w_uq, mla_w_ukv, mla_w_out]))
    ms = dict(zip(wnames, [m_ada_w, m_ada_b, m_pre_g, m_post_g, m_sgu_w_in, m_sgu_norm_g, m_sgu_w_s, m_sgu_b_s,
                           m_sgu_w_out, m_mla_w_in, m_mla_q_norm_g, m_mla_kv_norm_g, m_mla_w_uq, m_mla_w_ukv,
                           m_mla_w_out]))
    vs = dict(zip(wnames, [v_ada_w, v_ada_b, v_pre_g, v_post_g, v_sgu_w_in, v_sgu_norm_g, v_sgu_w_s, v_sgu_b_s,
                           v_sgu_w_out, v_mla_w_in, v_mla_q_norm_g, v_mla_kv_norm_g, v_mla_w_uq, v_mla_w_ukv,
                           v_mla_w_out]))
    grads = dict(ada_w=g_ada_w, ada_b=g_ada_b, pre_g=g_pre_g, post_g=g_post_g, sgu_norm_g=g_norm_g, sgu_w_s=g_w_s,
                 sgu_b_s=g_b_s, mla_q_norm_g=g_qg, mla_kv_norm_g=g_kvg)
    stepped = {}

    def step(nm):
        grads[nm] = grads[nm].reshape(weights[nm].shape)
        stepped[nm] = _adamw(weights[nm], grads[nm], ms[nm], vs[nm], f"adamw_{nm}")

    for nm in wnames:
        if nm in grads:
            step(nm)
    collect(len(in_flight), stepped["ada_w"][0])
    swapped = _pair_swap([halves[kind] for kind in kinds], "swap_grads")
    for kind, g in zip(kinds, swapped):
        grads[kind] = g
        step(kind)
    return (loss, grad_x, *[grads[nm] for nm in wnames], *[stepped[nm][0] for nm in wnames],
            *[stepped[nm][1] for nm in wnames], *[stepped[nm][2] for nm in wnames])
```

```python
import functools
import math

import jax
import jax.numpy as jnp
import numpy as np
from jax import lax
from jax.experimental import pallas as pl
from jax.experimental.pallas import tpu as pltpu

F32 = jnp.float32
BF16 = jnp.bfloat16
MESH = pl.DeviceIdType.MESH

NORM_EPS = 1e-6
CHUNK = 64
SGU_BLOCK = 128
SGU_GROUPS = 16
HEADS = 16
NOPE = 128
ROPE = 64
VDIM = 128
QK_PAD = 256
Q_RANK = 448
Q_RANK_PAD = 512
KV_RANK = 512
ROPE_THETA = 10000.0
ATTN_SCALE = (NOPE + ROPE) ** -0.5

ADAM_LR = 0.001
ADAM_B1 = 0.9
ADAM_B2 = 0.999
ADAM_EPS = 1e-08
ADAM_WD = 0.01
ADAM_STEP = 10

LANE = 128
VMEM_LIMIT = 48 * 1024 * 1024

NN = (((1,), (0,)), ((), ()))
NT = (((1,), (1,)), ((), ()))
TN = (((0,), (0,)), ((), ()))


def _params(*sem):
    return pltpu.CompilerParams(dimension_semantics=sem, vmem_limit_bytes=VMEM_LIMIT)


def _row_tile(rows, row_bytes, target_bytes=1 << 20):
    if rows * row_bytes <= target_bytes or rows % 16:
        return rows
    best = 16
    t = 16
    while t <= rows:
        if rows % t == 0 and t * row_bytes <= target_bytes:
            best = t
        t += 16
    return best


def _fit(dim, target):
    if dim <= target:
        return dim
    t = (target // LANE) * LANE
    while t > LANE and dim % t:
        t -= LANE
    return t


def _gelu(x):
    return 0.5 * x * (1.0 + lax.erf(x * 0.7071067811865476))


def _gelu_grad(x):
    return 0.5 * (1.0 + lax.erf(x * 0.7071067811865476)) + x * jnp.exp(-0.5 * x * x) * 0.3989422804014327


def _mm(a, b, *, ta=False, tb=False, b_sharded=False, out_sharded=False, out_dtype=F32,
        tm=1024, tn=1024, tk=2048, after=(), name):
    if ta:
        K, M = a.shape
    else:
        M, K = a.shape
    if b_sharded:
        shards, rows, Cs = b.shape
        b_shape = (rows, shards * Cs)
    else:
        b_shape = b.shape
    if tb:
        N, K2 = b_shape
    else:
        K2, N = b_shape
    assert K == K2, (a.shape, b.shape, ta, tb)
    n_lim = Cs if (b_sharded and not tb) else (N // 4 if out_sharded else N)
    k_lim = Cs if (b_sharded and tb) else K
    tm, tn, tk = _fit(M, tm), _fit(n_lim, tn), _fit(k_lim, tk)
    assert M % tm == 0 and n_lim % tn == 0 and k_lim % tk == 0, (M, N, K, tm, tn, tk)
    nk = K // tk
    nb_n = n_lim // tn
    nb_k = k_lim // tk
    dims = (((0 if ta else 1,), (1 if tb else 0,)), ((), ()))

    def body(a_ref, b_ref, *rest):
        o_ref, *scratch = rest[len(after):]
        prod = lax.dot_general(a_ref[...].astype(BF16), b_ref[...].astype(BF16), dims,
                               preferred_element_type=F32)
        if nk == 1:
            o_ref[...] = prod.astype(out_dtype)
        else:
            acc_ref, = scratch
            k = pl.program_id(2)

            @pl.when(k == 0)
            def _():
                acc_ref[...] = prod

            @pl.when(k > 0)
            def _():
                acc_ref[...] += prod

            @pl.when(k == nk - 1)
            def _():
                o_ref[...] = acc_ref[...].astype(out_dtype)

    a_spec = (pl.BlockSpec((tk, tm), lambda i, j, k: (k, i)) if ta
              else pl.BlockSpec((tm, tk), lambda i, j, k: (i, k)))
    if b_sharded and tb:
        b_spec = pl.BlockSpec((None, tn, tk), lambda i, j, k: (k // nb_k, j, k % nb_k))
    elif b_sharded:
        b_spec = pl.BlockSpec((None, tk, tn), lambda i, j, k: (j // nb_n, k, j % nb_n))
    elif tb:
        b_spec = pl.BlockSpec((tn, tk), lambda i, j, k: (j, k))
    else:
        b_spec = pl.BlockSpec((tk, tn), lambda i, j, k: (k, j))
    if out_sharded:
        out_shape = jax.ShapeDtypeStruct((4, M, N // 4), out_dtype)
        out_spec = pl.BlockSpec((None, tm, tn), lambda i, j, k: (j // nb_n, i, j % nb_n))
    else:
        out_shape = jax.ShapeDtypeStruct((M, N), out_dtype)
        out_spec = pl.BlockSpec((tm, tn), lambda i, j, k: (i, j))
    return pl.pallas_call(
        body, name=name,
        out_shape=out_shape,
        grid=(M // tm, N // tn, nk),
        in_specs=[a_spec, b_spec] + [pl.BlockSpec(memory_space=pl.ANY)] * len(after),
        out_specs=out_spec,
        scratch_shapes=[] if nk == 1 else [pltpu.VMEM((tm, tn), F32)],
        compiler_params=_params("parallel", "parallel", "arbitrary"),
    )(a, b, *after)


def _split_bf16(v):
    hi = v.astype(BF16)
    lo = (v - hi.astype(F32)).astype(BF16)
    return hi, lo


def _dot3(a, b, dims):
    a_hi, a_lo = _split_bf16(a)
    b_hi, b_lo = _split_bf16(b)
    out = lax.dot_general(a_hi, b_hi, dims, preferred_element_type=F32)
    out += lax.dot_general(a_lo, b_hi, dims, preferred_element_type=F32)
    out += lax.dot_general(a_hi, b_lo, dims, preferred_element_type=F32)
    return out


def _ada_mod(c_all, ada_w, ada_b_cols, name):
    L, D, cols = ada_w.shape
    B = c_all.shape[0]
    tn = 512 if cols % 512 == 0 else cols

    def body(c_ref, w_ref, b_ref, o_ref):
        cv = c_ref[...]
        cond = cv * jax.nn.sigmoid(cv)
        o_ref[...] = _dot3(cond, w_ref[...], NN) + b_ref[...]

    return pl.pallas_call(
        body, name=name,
        out_shape=jax.ShapeDtypeStruct((L, B, cols), F32),
        grid=(L, cols // tn),
        in_specs=[pl.BlockSpec((B, D), lambda l, j: (0, 0)),
                  pl.BlockSpec((None, D, tn), lambda l, j: (l, 0, j)),
                  pl.BlockSpec((None, 1, tn), lambda l, j: (l, 0, j))],
        out_specs=pl.BlockSpec((None, B, tn), lambda l, j: (l, 0, j)),
        compiler_params=_params("parallel", "parallel"),
    )(c_all, ada_w, ada_b_cols.reshape(L, 1, cols))


def _ada_grad(c_t, dmod_cols, name, after=()):
    L, B, cols = dmod_cols.shape
    D = c_t.shape[0]
    tn = 512 if cols % 512 == 0 else cols

    def body(c_ref, d_ref, *rest):
        o_ref = rest[-1]
        cv = c_ref[...]
        cond = cv * jax.nn.sigmoid(cv)
        o_ref[...] = _dot3(cond, d_ref[...], NN)

    return pl.pallas_call(
        body, name=name,
        out_shape=jax.ShapeDtypeStruct((L, D, cols), F32),
        grid=(L, cols // tn),
        in_specs=[pl.BlockSpec((D, B), lambda l, j: (0, 0)),
                  pl.BlockSpec((None, B, tn), lambda l, j: (l, 0, j))] + [pl.BlockSpec(memory_space=pl.ANY)] * len(after),
        out_specs=pl.BlockSpec((None, D, tn), lambda l, j: (l, 0, j)),
        compiler_params=_params("parallel", "parallel"),
    )(c_t, dmod_cols, *after)


def _row_spec(ts, width):
    return pl.BlockSpec((ts, width), lambda i: (i, 0))


def _vec_spec(width):
    return pl.BlockSpec((1, width), lambda i: (0, 0))


def _pre_fwd(x, pre_g, scale, shift, name, after=()):
    S, D = x.shape
    ts = min(256, S)

    def body(x_ref, g_ref, sc_ref, sh_ref, *rest):
        h_ref = rest[-1]
        xv = x_ref[...]
        r = lax.rsqrt(jnp.mean(xv * xv, axis=-1, keepdims=True) + NORM_EPS)
        h_ref[...] = ((xv * r * g_ref[...]) * (1.0 + sc_ref[...]) + sh_ref[...]).astype(BF16)

    return pl.pallas_call(
        body, name=name, out_shape=jax.ShapeDtypeStruct((S, D), BF16), grid=(S // ts,),
        in_specs=[_row_spec(ts, D), _vec_spec(D), _vec_spec(D), _vec_spec(D)]
        + [pl.BlockSpec(memory_space=pl.ANY)] * len(after),
        out_specs=_row_spec(ts, D), compiler_params=_params("parallel"),
    )(x, pre_g, scale, shift, *after)


def _pre_bwd(dh, x, dx_res, pre_g, scale, name):
    S, D = x.shape
    ts = min(256, S)

    def body(dh_ref, x_ref, dr_ref, g_ref, sc_ref, dx_ref, dsh_ref, dsc_ref, dg_ref):
        @pl.when(pl.program_id(0) == 0)
        def _():
            dsh_ref[...] = jnp.zeros_like(dsh_ref)
            dsc_ref[...] = jnp.zeros_like(dsc_ref)
            dg_ref[...] = jnp.zeros_like(dg_ref)

        dh = dh_ref[...]
        xv = x_ref[...]
        g = g_ref[...]
        one_sc = 1.0 + sc_ref[...]
        r = lax.rsqrt(jnp.mean(xv * xv, axis=-1, keepdims=True) + NORM_EPS)
        xn = xv * r
        dsh_ref[...] += jnp.sum(dh, axis=0, keepdims=True)
        dsc_ref[...] += jnp.sum(dh * (xn * g), axis=0, keepdims=True)
        dg_ref[...] += jnp.sum(dh * one_sc * xn, axis=0, keepdims=True)
        dxn = dh * one_sc * g
        dx_ref[...] = dr_ref[...] + r * (dxn - xn * jnp.mean(dxn * xn, axis=-1, keepdims=True))

    vec = jax.ShapeDtypeStruct((1, D), F32)
    return pl.pallas_call(
        body, name=name, out_shape=(jax.ShapeDtypeStruct((S, D), F32), vec, vec, vec), grid=(S // ts,),
        in_specs=[_row_spec(ts, D), _row_spec(ts, D), _row_spec(ts, D), _vec_spec(D), _vec_spec(D)],
        out_specs=(_row_spec(ts, D), _vec_spec(D), _vec_spec(D), _vec_spec(D)),
        compiler_params=_params("arbitrary"),
    )(dh, x, dx_res, pre_g, scale)


def _post_fwd(x, y, gate, post_g, name):
    S, D = x.shape
    ts = min(256, S)

    def body(x_ref, y_ref, gt_ref, g_ref, o_ref):
        yv = y_ref[...]
        r = lax.rsqrt(jnp.mean(yv * yv, axis=-1, keepdims=True) + NORM_EPS)
        o_ref[...] = x_ref[...] + gt_ref[...] * (yv * r * g_ref[...])

    return pl.pallas_call(
        body, name=name, out_shape=jax.ShapeDtypeStruct((S, D), F32), grid=(S // ts,),
        in_specs=[_row_spec(ts, D), _row_spec(ts, D), _vec_spec(D), _vec_spec(D)],
        out_specs=_row_spec(ts, D), compiler_params=_params("parallel"),
    )(x, y, gate, post_g)


def _post_bwd(dx, y, gate, post_g, name):
    S, D = y.shape
    ts = min(256, S)

    def body(dx_ref, y_ref, gt_ref, g_ref, dy_ref, dgt_ref, dg_ref):
        @pl.when(pl.program_id(0) == 0)
        def _():
            dgt_ref[...] = jnp.zeros_like(dgt_ref)
            dg_ref[...] = jnp.zeros_like(dg_ref)

        dxv = dx_ref[...]
        yv = y_ref[...]
        g = g_ref[...]
        gt = gt_ref[...]
        r = lax.rsqrt(jnp.mean(yv * yv, axis=-1, keepdims=True) + NORM_EPS)
        yn = yv * r
        dgt_ref[...] += jnp.sum(dxv * (yn * g), axis=0, keepdims=True)
        dg_ref[...] += jnp.sum(dxv * gt * yn, axis=0, keepdims=True)
        dyn = dxv * gt * g
        dy_ref[...] = (r * (dyn - yn * jnp.mean(dyn * yn, axis=-1, keepdims=True))).astype(BF16)

    vec = jax.ShapeDtypeStruct((1, D), F32)
    return pl.pallas_call(
        body, name=name, out_shape=(jax.ShapeDtypeStruct((S, D), BF16), vec, vec), grid=(S // ts,),
        in_specs=[_row_spec(ts, D), _row_spec(ts, D), _vec_spec(D), _vec_spec(D)],
        out_specs=(_row_spec(ts, D), _vec_spec(D), _vec_spec(D)),
        compiler_params=_params("arbitrary"),
    )(dx, y, gate, post_g)


def _loss_grad(xf, target, name):
    S, D = xf.shape
    ts = min(256, S)

    def body(x_ref, t_ref, dx_ref, l_ref):
        @pl.when(pl.program_id(0) == 0)
        def _():
            l_ref[...] = jnp.zeros_like(l_ref)

        e = x_ref[...] - t_ref[...]
        dx_ref[...] = e * (1.0 / D)
        row = jnp.sum(e * e, axis=1, keepdims=True) * (1.0 / D)
        l_ref[...] += 0.5 * jnp.sum(row, axis=0, keepdims=True)

    return pl.pallas_call(
        body, name=name,
        out_shape=(jax.ShapeDtypeStruct((S, D), F32), jax.ShapeDtypeStruct((1, 1), F32)), grid=(S // ts,),
        in_specs=[_row_spec(ts, D), _row_spec(ts, D)],
        out_specs=(_row_spec(ts, D), pl.BlockSpec((1, 1), lambda i: (0, 0))),
        compiler_params=_params("arbitrary"),
    )(xf, target)


def _chunk_mask(transposed=False):
    row = lax.broadcasted_iota(jnp.int32, (SGU_BLOCK, SGU_BLOCK), 0) // CHUNK
    col = lax.broadcasted_iota(jnp.int32, (SGU_BLOCK, SGU_BLOCK), 1) // CHUNK
    return (row <= col) if transposed else (col <= row)


def _sgu_gate_fwd(uvz, norm_g, w_s, b_bc, name):
    S, E3 = uvz.shape
    E = E3 // 3
    T = SGU_BLOCK
    gd = E // SGU_GROUPS

    def body(uvz_ref, ng_ref, ws_ref, bb_ref, y_ref, v_scr):
        gv = _gelu(uvz_ref[:, E:2 * E])
        mu = jnp.mean(gv, axis=-1, keepdims=True)
        xc = gv - mu
        rstd = lax.rsqrt(jnp.mean(xc * xc, axis=-1, keepdims=True) + NORM_EPS)
        v_scr[...] = (xc * rstd * ng_ref[...]).astype(BF16)
        mask = _chunk_mask()
        for g in range(SGU_GROUPS):
            sl = slice(g * gd, (g + 1) * gd)
            wg = jnp.where(mask, ws_ref[g], 0.0).astype(BF16)
            vm = lax.dot_general(wg, v_scr[:, sl], NN, preferred_element_type=F32)
            vm = vm + jnp.tile(bb_ref[g], (1, gd // LANE))
            z = uvz_ref[:, 2 * E + g * gd:2 * E + (g + 1) * gd]
            y_ref[:, sl] = (_gelu(uvz_ref[:, sl]) * vm * (z * jax.nn.sigmoid(z))).astype(BF16)

    return pl.pallas_call(
        body, name=name, out_shape=jax.ShapeDtypeStruct((S, E), BF16), grid=(S // T,),
        in_specs=[_row_spec(T, E3), _vec_spec(E),
                  pl.BlockSpec((SGU_GROUPS, T, T), lambda i: (0, 0, 0)),
                  pl.BlockSpec((SGU_GROUPS, T, LANE), lambda i: (0, 0, 0))],
        out_specs=_row_spec(T, E),
        scratch_shapes=[pltpu.VMEM((T, E), BF16)],
        compiler_params=_params("parallel"),
    )(uvz, norm_g, w_s, b_bc)


def _sgu_gate_bwd(uvz, dyv, norm_g, w_s, w_sT, b_bc, name):
    S, E3 = uvz.shape
    E = E3 // 3
    T = SGU_BLOCK
    gd = E // SGU_GROUPS

    def body(uvz_ref, dyv_ref, ng_ref, ws_ref, wst_ref, bb_ref,
             d_ref, dng_ref, dws_ref, dbs_ref, vhat_scr, dv_scr):
        @pl.when(pl.program_id(0) == 0)
        def _():
            dng_ref[...] = jnp.zeros_like(dng_ref)
            dws_ref[...] = jnp.zeros_like(dws_ref)
            dbs_ref[...] = jnp.zeros_like(dbs_ref)

        gv = _gelu(uvz_ref[:, E:2 * E])
        mu = jnp.mean(gv, axis=-1, keepdims=True)
        xc = gv - mu
        rstd = lax.rsqrt(jnp.mean(xc * xc, axis=-1, keepdims=True) + NORM_EPS)
        vhat_scr[...] = xc * rstd
        mask = _chunk_mask()
        mask_t = _chunk_mask(transposed=True)
        for g in range(SGU_GROUPS):
            sl = slice(g * gd, (g + 1) * gd)
            u_pre = uvz_ref[:, sl]
            z = uvz_ref[:, 2 * E + g * gd:2 * E + (g + 1) * gd]
            dy = dyv_ref[:, sl]
            u = _gelu(u_pre)
            sig = jax.nn.sigmoid(z)
            sz = z * sig
            vg = (vhat_scr[:, sl] * ng_ref[:, sl]).astype(BF16)
            wg = jnp.where(mask, ws_ref[g], 0.0).astype(BF16)
            vm = lax.dot_general(wg, vg, NN, preferred_element_type=F32)
            vm = vm + jnp.tile(bb_ref[g], (1, gd // LANE))
            dy_u = dy * u
            d_ref[:, sl] = (dy * vm * sz * _gelu_grad(u_pre)).astype(BF16)
            d_ref[:, 2 * E + g * gd:2 * E + (g + 1) * gd] = (
                dy_u * vm * (sig * (1.0 + z * (1.0 - sig)))).astype(BF16)
            dvm = dy_u * sz
            dvm_b = dvm.astype(BF16)
            dws_ref[g] += jnp.where(mask, lax.dot_general(dvm_b, vg, NT, preferred_element_type=F32), 0.0)
            dbs_ref[g] += jnp.broadcast_to(jnp.sum(dvm, axis=1, keepdims=True), (T, LANE))
            wgt = jnp.where(mask_t, wst_ref[g], 0.0).astype(BF16)
            dv_scr[:, sl] = lax.dot_general(wgt, dvm_b, NN, preferred_element_type=F32)
        dv = dv_scr[...]
        vhat = vhat_scr[...]
        dng_ref[...] += jnp.sum(dv * vhat, axis=0, keepdims=True)
        dvh = dv * ng_ref[...]
        dgv = rstd * (dvh - jnp.mean(dvh, axis=-1, keepdims=True)
                      - vhat * jnp.mean(dvh * vhat, axis=-1, keepdims=True))
        d_ref[:, E:2 * E] = (dgv * _gelu_grad(uvz_ref[:, E:2 * E])).astype(BF16)

    wspec = pl.BlockSpec((SGU_GROUPS, T, T), lambda i: (0, 0, 0))
    bspec = pl.BlockSpec((SGU_GROUPS, T, LANE), lambda i: (0, 0, 0))
    return pl.pallas_call(
        body, name=name,
        out_shape=(jax.ShapeDtypeStruct((S, E3), BF16), jax.ShapeDtypeStruct((1, E), F32),
                   jax.ShapeDtypeStruct((SGU_GROUPS, T, T), F32),
                   jax.ShapeDtypeStruct((SGU_GROUPS, T, LANE), F32)),
        grid=(S // T,),
        in_specs=[_row_spec(T, E3), _row_spec(T, E), _vec_spec(E), wspec, wspec, bspec],
        out_specs=(_row_spec(T, E3), _vec_spec(E), wspec, bspec),
        scratch_shapes=[pltpu.VMEM((T, E), F32), pltpu.VMEM((T, E), F32)],
        compiler_params=_params("arbitrary"),
    )(uvz, dyv, norm_g, w_s, w_sT, b_bc)


MLA_WIDTH = HEADS * VDIM
P_LATENT = Q_RANK + KV_RANK + ROPE
P_WIDTH = P_LATENT + MLA_WIDTH


def _swap_halves(v):
    lane = lax.broadcasted_iota(jnp.int32, v.shape, 1)
    return jnp.where(lane % ROPE < ROPE // 2, pltpu.roll(v, LANE - ROPE // 2, 1), pltpu.roll(v, ROPE // 2, 1))


def _low_lanes(rows):
    return lax.broadcasted_iota(jnp.int32, (rows, LANE), 1) < ROPE


def _latent_tiles(ref):
    return [ref[:, t * LANE:(t + 1) * LANE] for t in range(P_LATENT // LANE)]


def _split_latents(tiles, low):
    cq = jnp.concatenate(tiles[0:3] + [jnp.where(low, tiles[3], 0.0)], axis=1)
    rolled = [pltpu.roll(t, ROPE, 1) for t in tiles[3:8]]
    ckv = jnp.concatenate([jnp.where(low, rolled[t], rolled[t + 1]) for t in range(4)], axis=1)
    kr = jnp.where(low, rolled[4], 0.0)
    return cq, ckv, kr


def _mla_mid_fwd(p, qg, kvg, name):
    S, PW = p.shape
    ts = min(256, S)

    def body(p_ref, qg_ref, kvg_ref, cqn_ref, ckvn_ref):
        cq, ckv, _ = _split_latents(_latent_tiles(p_ref), _low_lanes(ts))
        r = lax.rsqrt(jnp.sum(cq * cq, axis=-1, keepdims=True) * (1.0 / Q_RANK) + NORM_EPS)
        cqn_ref[...] = (cq * r * qg_ref[...]).astype(BF16)
        r2 = lax.rsqrt(jnp.mean(ckv * ckv, axis=-1, keepdims=True) + NORM_EPS)
        ckvn_ref[...] = (ckv * r2 * kvg_ref[...]).astype(BF16)

    return pl.pallas_call(
        body, name=name,
        out_shape=(jax.ShapeDtypeStruct((S, Q_RANK_PAD), BF16), jax.ShapeDtypeStruct((S, KV_RANK), BF16)),
        grid=(S // ts,),
        in_specs=[_row_spec(ts, P_LATENT), _vec_spec(Q_RANK_PAD), _vec_spec(KV_RANK)],
        out_specs=(_row_spec(ts, Q_RANK_PAD), _row_spec(ts, KV_RANK)),
        compiler_params=_params("parallel"),
    )(p, qg, kvg)


def _mla_pack(q, kv, p, cos_t, sin_t, name):
    S = q.shape[0]
    ts = min(256, S)
    pair_w = 2 * (NOPE + ROPE)
    head_w = NOPE + VDIM

    def body(q_ref, kv_ref, kr_ref, cos_ref, sin_ref, qo_ref, ko_ref, vo_ref):
        cosv = cos_ref[...]
        sinv = sin_ref[...]
        low = _low_lanes(ts)
        kr = jnp.where(low, pltpu.roll(kr_ref[...], ROPE, 1), 0.0)
        kr = (kr * cosv + _swap_halves(kr) * sinv).astype(BF16)
        for pair in range(HEADS // 2):
            t0, t1, t2 = (q_ref[:, pair * pair_w + t * LANE:pair * pair_w + (t + 1) * LANE] for t in range(3))
            nope_b = jnp.where(low, pltpu.roll(t1, ROPE, 1), pltpu.roll(t2, ROPE, 1))
            ropes = jnp.where(low, t1, t2)
            roped = ropes * cosv + _swap_halves(ropes) * sinv
            qo_ref[2 * pair, :, 0:NOPE] = t0.astype(BF16)
            qo_ref[2 * pair, :, NOPE:QK_PAD] = jnp.where(low, roped, 0.0).astype(BF16)
            qo_ref[2 * pair + 1, :, 0:NOPE] = nope_b.astype(BF16)
            qo_ref[2 * pair + 1, :, NOPE:QK_PAD] = jnp.where(low, pltpu.roll(roped, ROPE, 1), 0.0).astype(BF16)
        for h in range(HEADS):
            ko_ref[h, :, 0:NOPE] = kv_ref[:, h * head_w:h * head_w + NOPE].astype(BF16)
            ko_ref[h, :, NOPE:QK_PAD] = kr
            vo_ref[h] = kv_ref[:, h * head_w + NOPE:(h + 1) * head_w].astype(BF16)

    return pl.pallas_call(
        body, name=name,
        out_shape=(jax.ShapeDtypeStruct((HEADS, S, QK_PAD), BF16), jax.ShapeDtypeStruct((HEADS, S, QK_PAD), BF16),
                   jax.ShapeDtypeStruct((HEADS, S, VDIM), BF16)),
        grid=(S // ts,),
        in_specs=[_row_spec(ts, q.shape[1]), _row_spec(ts, kv.shape[1]),
                  pl.BlockSpec((ts, LANE), lambda i: (i, P_LATENT // LANE - 1)),
                  _row_spec(ts, LANE), _row_spec(ts, LANE)],
        out_specs=(pl.BlockSpec((HEADS, ts, QK_PAD), lambda i: (0, i, 0)),
                   pl.BlockSpec((HEADS, ts, QK_PAD), lambda i: (0, i, 0)),
                   pl.BlockSpec((HEADS, ts, VDIM), lambda i: (0, i, 0))),
        compiler_params=_params("parallel"),
    )(q, kv, p, cos_t, sin_t)


def _mla_unpack(dQ, dK, dV, cos_t, sin_t, name):
    S = dQ.shape[1]
    ts = min(256, S)
    pair_w = 2 * (NOPE + ROPE)
    head_w = NOPE + VDIM

    def body(dq_ref, dk_ref, dv_ref, cos_ref, sin_ref, q_ref, kv_ref, kr_ref):
        cosv = cos_ref[...]
        sinv = sin_ref[...]
        low = _low_lanes(ts)
        for pair in range(HEADS // 2):
            blk = dq_ref[2 * pair, :, NOPE:QK_PAD] + pltpu.roll(dq_ref[2 * pair + 1, :, NOPE:QK_PAD], ROPE, 1)
            ropes = blk * cosv - _swap_halves(blk) * sinv
            nope_b = pltpu.roll(dq_ref[2 * pair + 1, :, 0:NOPE], ROPE, 1)
            q_ref[:, pair * pair_w:pair * pair_w + LANE] = dq_ref[2 * pair, :, 0:NOPE].astype(BF16)
            q_ref[:, pair * pair_w + LANE:pair * pair_w + 2 * LANE] = jnp.where(low, ropes, nope_b).astype(BF16)
            q_ref[:, pair * pair_w + 2 * LANE:(pair + 1) * pair_w] = jnp.where(low, nope_b, ropes).astype(BF16)
        dkr = dk_ref[0, :, NOPE:QK_PAD]
        for h in range(1, HEADS):
            dkr = dkr + dk_ref[h, :, NOPE:QK_PAD]
        kr_ref[...] = dkr * cosv - _swap_halves(dkr) * sinv
        for h in range(HEADS):
            kv_ref[:, h * head_w:h * head_w + NOPE] = dk_ref[h, :, 0:NOPE].astype(BF16)
            kv_ref[:, h * head_w + NOPE:(h + 1) * head_w] = dv_ref[h].astype(BF16)

    return pl.pallas_call(
        body, name=name,
        out_shape=(jax.ShapeDtypeStruct((S, HEADS * (NOPE + ROPE)), BF16),
                   jax.ShapeDtypeStruct((S, HEADS * (NOPE + VDIM)), BF16),
                   jax.ShapeDtypeStruct((S, LANE), F32)),
        grid=(S // ts,),
        in_specs=[pl.BlockSpec((HEADS, ts, QK_PAD), lambda i: (0, i, 0)),
                  pl.BlockSpec((HEADS, ts, QK_PAD), lambda i: (0, i, 0)),
                  pl.BlockSpec((HEADS, ts, VDIM), lambda i: (0, i, 0)),
                  _row_spec(ts, LANE), _row_spec(ts, LANE)],
        out_specs=(_row_spec(ts, HEADS * (NOPE + ROPE)), _row_spec(ts, HEADS * (NOPE + VDIM)),
                   _row_spec(ts, LANE)),
        compiler_params=_params("parallel"),
    )(dQ, dK, dV, cos_t, sin_t)


def _mla_gate_fwd(o, p, name):
    S, W = o.shape
    ts = min(256, S)
    wb = P_LATENT

    def body(o_ref, z_ref, y_ref):
        z = z_ref[...]
        y_ref[...] = (o_ref[...] * (z * jax.nn.sigmoid(z))).astype(BF16)

    return pl.pallas_call(
        body, name=name, out_shape=jax.ShapeDtypeStruct((S, W), BF16), grid=(S // ts, W // wb),
        in_specs=[pl.BlockSpec((ts, wb), lambda i, j: (i, j)), pl.BlockSpec((ts, wb), lambda i, j: (i, j + 1))],
        out_specs=pl.BlockSpec((ts, wb), lambda i, j: (i, j)), compiler_params=_params("parallel", "parallel"),
    )(o, p)


def _mla_gate_bwd(dyv, p, name):
    S, W = dyv.shape
    ts = min(256, S)
    wb = P_LATENT

    def body(d_ref, z_ref, do_ref):
        z = z_ref[...]
        do_ref[...] = d_ref[...] * (z * jax.nn.sigmoid(z))

    return pl.pallas_call(
        body, name=name, out_shape=jax.ShapeDtypeStruct((S, W), F32), grid=(S // ts, W // wb),
        in_specs=[pl.BlockSpec((ts, wb), lambda i, j: (i, j)), pl.BlockSpec((ts, wb), lambda i, j: (i, j + 1))],
        out_specs=pl.BlockSpec((ts, wb), lambda i, j: (i, j)), compiler_params=_params("parallel", "parallel"),
    )(dyv, p)


def _mla_mid_bwd(p, dcqn, dckvn, dkr, dyv, o, qg, kvg, name):
    S, PW = p.shape
    W = o.shape[1]
    ts = min(256, S)
    nt = Q_RANK_PAD // LANE

    def rms_bwd(xv, dy, g, count):
        r = lax.rsqrt(jnp.sum(xv * xv, axis=-1, keepdims=True) * (1.0 / count) + NORM_EPS)
        xn = xv * r
        dg = jnp.sum(dy * xn, axis=0, keepdims=True)
        dxn = dy * g
        dx = r * (dxn - xn * (jnp.sum(dxn * xn, axis=-1, keepdims=True) * (1.0 / count)))
        return dx, dg

    def body(p_ref, dcq_ref, dckv_ref, dkr_ref, dyv_ref, o_ref, qg_ref, kvg_ref, dp_ref, dqg_ref, dkvg_ref):
        @pl.when(pl.program_id(0) == 0)
        def _():
            dqg_ref[...] = jnp.zeros_like(dqg_ref)
            dkvg_ref[...] = jnp.zeros_like(dkvg_ref)

        low = _low_lanes(ts)
        cq, ckv, _ = _split_latents(_latent_tiles(p_ref), low)
        dcq, dg = rms_bwd(cq, dcq_ref[...], qg_ref[...], Q_RANK)
        dqg_ref[...] += dg
        dckv, dg = rms_bwd(ckv, dckv_ref[...], kvg_ref[...], KV_RANK)
        dkvg_ref[...] += dg
        moved = [pltpu.roll(dckv[:, t * LANE:(t + 1) * LANE], ROPE, 1) for t in range(nt)]
        moved.append(pltpu.roll(dkr_ref[...], ROPE, 1))
        for t in range(nt - 1):
            dp_ref[:, t * LANE:(t + 1) * LANE] = dcq[:, t * LANE:(t + 1) * LANE].astype(BF16)
        dp_ref[:, (nt - 1) * LANE:nt * LANE] = jnp.where(low, dcq[:, (nt - 1) * LANE:nt * LANE], moved[0]).astype(BF16)
        for t in range(nt):
            dp_ref[:, (nt + t) * LANE:(nt + t + 1) * LANE] = jnp.where(low, moved[t], moved[t + 1]).astype(BF16)
        z = p_ref[:, P_LATENT:PW]
        sig = jax.nn.sigmoid(z)
        dp_ref[:, P_LATENT:PW] = (dyv_ref[...] * o_ref[...] * (sig * (1.0 + z * (1.0 - sig)))).astype(BF16)

    return pl.pallas_call(
        body, name=name,
        out_shape=(jax.ShapeDtypeStruct((S, PW), BF16), jax.ShapeDtypeStruct((1, Q_RANK_PAD), F32),
                   jax.ShapeDtypeStruct((1, KV_RANK), F32)),
        grid=(S // ts,),
        in_specs=[_row_spec(ts, PW), _row_spec(ts, Q_RANK_PAD), _row_spec(ts, KV_RANK), _row_spec(ts, LANE),
                  _row_spec(ts, W), _row_spec(ts, W), _vec_spec(Q_RANK_PAD), _vec_spec(KV_RANK)],
        out_specs=(_row_spec(ts, PW), _vec_spec(Q_RANK_PAD), _vec_spec(KV_RANK)),
        compiler_params=_params("arbitrary"),
    )(p, dcqn, dckvn, dkr, dyv, o, qg, kvg)


def _tile_mask(T):
    row = lax.broadcasted_iota(jnp.int32, (T, T), 0) // CHUNK
    col = lax.broadcasted_iota(jnp.int32, (T, T), 1) // CHUNK
    return col <= row


def _attn_fwd(Q, K, V, name):
    H, S, _ = Q.shape
    T = min(512, S)
    n_part = 2 if T % 256 == 0 else 1
    Tq = T // n_part

    def body(q_ref, k_ref, v_ref, o_ref, lse_ref, m_scr, l_scr, acc_scr):
        qi = pl.program_id(1)
        q = q_ref[...]
        m_scr[...] = jnp.full_like(m_scr, -jnp.inf)
        l_scr[...] = jnp.zeros_like(l_scr)
        acc_scr[...] = jnp.zeros_like(acc_scr)

        def tile(j, masked):
            rows = pl.ds(pl.multiple_of(j * T, T), T)
            kt = k_ref[rows, :]
            vt = v_ref[rows, :]
            for part in range(n_part):
                sub = slice(part * Tq, (part + 1) * Tq)
                s = lax.dot_general(q[sub], kt, NT, preferred_element_type=F32) * ATTN_SCALE
                if masked:
                    row = (lax.broadcasted_iota(jnp.int32, (Tq, T), 0) + part * Tq) // CHUNK
                    col = lax.broadcasted_iota(jnp.int32, (Tq, T), 1) // CHUNK
                    s = jnp.where(col <= row, s, -1e30)
                m_prev = m_scr[sub]
                m_new = jnp.maximum(m_prev, jnp.max(s, axis=1, keepdims=True))
                pr = jnp.exp(s - m_new)
                alpha = jnp.exp(m_prev - m_new)
                l_scr[sub] = alpha * l_scr[sub] + jnp.sum(pr, axis=1, keepdims=True)
                acc_scr[sub] = alpha * acc_scr[sub] + lax.dot_general(
                    pr.astype(BF16), vt, NN, preferred_element_type=F32)
                m_scr[sub] = m_new

        def full_tile(j, carry):
            tile(j, False)
            return carry

        lax.fori_loop(0, qi, full_tile, 0)
        tile(qi, True)
        l = l_scr[...]
        o_ref[...] = acc_scr[...] / l
        lse_ref[...] = jnp.broadcast_to(m_scr[...] + jnp.log(l), (T, LANE))

    return pl.pallas_call(
        body, name=name,
        out_shape=(jax.ShapeDtypeStruct((S, H * VDIM), F32), jax.ShapeDtypeStruct((H, S, LANE), F32)),
        grid=(H, S // T),
        in_specs=[pl.BlockSpec((None, T, QK_PAD), lambda h, i: (h, i, 0)),
                  pl.BlockSpec((None, S, QK_PAD), lambda h, i: (h, 0, 0)),
                  pl.BlockSpec((None, S, VDIM), lambda h, i: (h, 0, 0))],
        out_specs=(pl.BlockSpec((T, VDIM), lambda h, i: (i, h)),
                   pl.BlockSpec((None, T, LANE), lambda h, i: (h, i, 0))),
        scratch_shapes=[pltpu.VMEM((T, 1), F32), pltpu.VMEM((T, 1), F32), pltpu.VMEM((T, VDIM), F32)],
        compiler_params=_params("parallel", "arbitrary"),
    )(Q, K, V)


def _attn_bwd(Q, K, V, o, do, lse, name):
    H, S, _ = Q.shape
    T = min(512, S)
    nq = S // T

    def body(q_ref, k_ref, v_ref, o_ref, do_ref, lse_ref, dq_ref, dk_ref, dv_ref, dk_scr, dv_scr):
        ki = pl.program_id(1)

        @pl.when(ki == 0)
        def _():
            dq_ref[...] = jnp.zeros_like(dq_ref)

        dk_scr[...] = jnp.zeros_like(dk_scr)
        dv_scr[...] = jnp.zeros_like(dv_scr)
        k = k_ref[...]
        v = v_ref[...]

        def tile(i, masked):
            rows = pl.ds(pl.multiple_of(i * T, T), T)
            q = q_ref[rows, :]
            do_f = do_ref[rows, :]
            do_b = do_f.astype(BF16)
            delta = jnp.sum(do_f * o_ref[rows, :], axis=1, keepdims=True)
            s = lax.dot_general(q, k, NT, preferred_element_type=F32) * ATTN_SCALE
            pr = jnp.exp(s - lse_ref[rows, 0:1])
            if masked:
                pr = jnp.where(_tile_mask(T), pr, 0.0)
            dv_scr[...] += lax.dot_general(pr.astype(BF16), do_b, TN, preferred_element_type=F32)
            dp = lax.dot_general(do_b, v, NT, preferred_element_type=F32)
            ds = (pr * (dp - delta) * ATTN_SCALE).astype(BF16)
            dk_scr[...] += lax.dot_general(ds, q, TN, preferred_element_type=F32)
            dq_ref[rows, :] += lax.dot_general(ds, k, NN, preferred_element_type=F32)

        def full_tile(i, carry):
            tile(i, False)
            return carry

        tile(ki, True)
        lax.fori_loop(ki + 1, nq, full_tile, 0)
        dk_ref[...] = dk_scr[...]
        dv_ref[...] = dv_scr[...]

    return pl.pallas_call(
        body, name=name,
        out_shape=(jax.ShapeDtypeStruct((H, S, QK_PAD), F32), jax.ShapeDtypeStruct((H, S, QK_PAD), F32),
                   jax.ShapeDtypeStruct((H, S, VDIM), F32)),
        grid=(H, nq),
        in_specs=[pl.BlockSpec((None, S, QK_PAD), lambda h, j: (h, 0, 0)),
                  pl.BlockSpec((None, T, QK_PAD), lambda h, j: (h, j, 0)),
                  pl.BlockSpec((None, T, VDIM), lambda h, j: (h, j, 0)),
                  pl.BlockSpec((S, VDIM), lambda h, j: (0, h)),
                  pl.BlockSpec((S, VDIM), lambda h, j: (0, h)),
                  pl.BlockSpec((None, S, LANE), lambda h, j: (h, 0, 0))],
        out_specs=(pl.BlockSpec((None, S, QK_PAD), lambda h, j: (h, 0, 0)),
                   pl.BlockSpec((None, T, QK_PAD), lambda h, j: (h, j, 0)),
                   pl.BlockSpec((None, T, VDIM), lambda h, j: (h, j, 0))),
        scratch_shapes=[pltpu.VMEM((T, QK_PAD), F32), pltpu.VMEM((T, VDIM), F32)],
        compiler_params=_params("parallel", "arbitrary"),
    )(Q, K, V, o, do, lse)


def _adamw(w, g, m, v, name):
    shape = w.shape
    C = shape[-1]
    R = math.prod(shape[:-1])
    flat = [t.reshape(R, C) for t in (w, g, m, v)]
    tr = _row_tile(R, C * 4)

    def body(w_ref, g_ref, m_ref, v_ref, d_ref, nm_ref, nv_ref):
        gv = g_ref[...]
        m_new = ADAM_B1 * m_ref[...] + (1.0 - ADAM_B1) * gv
        v_new = ADAM_B2 * v_ref[...] + (1.0 - ADAM_B2) * jnp.square(gv)
        m_hat = m_new / (1.0 - ADAM_B1 ** ADAM_STEP)
        v_hat = v_new / (1.0 - ADAM_B2 ** ADAM_STEP)
        d_ref[...] = -ADAM_LR * (m_hat / (jnp.sqrt(v_hat) + ADAM_EPS) + ADAM_WD * w_ref[...])
        nm_ref[...] = m_new
        nv_ref[...] = v_new

    spec = pl.BlockSpec((tr, C), lambda i: (i, 0))
    out = jax.ShapeDtypeStruct((R, C), F32)
    d, nm, nv = pl.pallas_call(
        body, name=name, out_shape=(out, out, out), grid=(R // tr,),
        in_specs=[spec] * 4, out_specs=(spec, spec, spec), compiler_params=_params("parallel"),
    )(*flat)
    return d.reshape(shape), nm.reshape(shape), nv.reshape(shape)


def _sum_into_half(r, buf, layer, ci, n_layers, name):
    n, M, N = r.shape
    tr = _row_tile(M, N * 4 * n, 4 << 20)

    def body(c_ref, r_ref, *rest):
        o_ref = rest[-1]
        acc = r_ref[0].astype(F32)
        for s in range(1, n):
            acc = acc + r_ref[s].astype(F32)
        o_ref[...] = acc

    in_specs = [pl.BlockSpec((n, tr, N), lambda i, c: (0, i, 0))]
    operands = [ci.reshape(1), r]
    aliases = {}
    if buf is not None:
        in_specs.append(ANY)
        operands.append(buf)
        aliases = {2: 0}
    return pl.pallas_call(
        body, name=name, out_shape=jax.ShapeDtypeStruct((n_layers, 2, M, N), F32),
        grid_spec=pltpu.PrefetchScalarGridSpec(
            num_scalar_prefetch=1, grid=(M // tr,), in_specs=in_specs,
            out_specs=pl.BlockSpec((None, None, tr, N), lambda i, c: (layer, c[0], i, 0))),
        input_output_aliases=aliases, compiler_params=_params("parallel"),
    )(*operands)


def _sum_slots(r, name):
    n, M, N = r.shape
    tr = _row_tile(M, N * 4 * n, 4 << 20)

    def body(r_ref, o_ref):
        acc = r_ref[0].astype(F32)
        for s in range(1, n):
            acc = acc + r_ref[s].astype(F32)
        o_ref[...] = acc

    return pl.pallas_call(
        body, name=name, out_shape=jax.ShapeDtypeStruct((M, N), F32), grid=(M // tr,),
        in_specs=[pl.BlockSpec((n, tr, N), lambda i: (0, i, 0))],
        out_specs=pl.BlockSpec((tr, N), lambda i: (i, 0)), compiler_params=_params("parallel"),
    )(r)


ANY = pl.BlockSpec(memory_space=pl.ANY)
DMA_CHUNK_BYTES = 1 << 20
DMA_MAX_CHUNKS = 16
PEER_ORDER = (1, 4, 5, 2, 3, 6, 7)


def _position():
    return lax.axis_index("x"), lax.axis_index("y"), lax.axis_index("c")


def _row_chunks(shape, dtype):
    rows, cols = shape
    n = max(1, min(DMA_MAX_CHUNKS, rows * cols * jnp.dtype(dtype).itemsize // DMA_CHUNK_BYTES))
    while n > 1 and (rows % n or (rows // n) % 16):
        n -= 1
    step = rows // n
    return [pl.ds(q * step, step) for q in range(n)]


def _all_gather8(xs, name):
    n = len(xs)

    def body(*refs):
        x_refs, o_refs = refs[:n], refs[n:2 * n]
        send_sems, recv_sems, local_sems = refs[2 * n:]
        x, y, c = _position()
        me, sibling = (x, y, c), (x, y, 1 - c)
        chips = [(1 - x, y), (x, 1 - y), (1 - x, 1 - y)]

        def slot(a, dev, rows):
            return o_refs[a].at[4 * dev[0] + 2 * dev[1] + dev[2], rows]

        def copy(a, k, block, to, rows, from_input=False):
            return pltpu.make_async_remote_copy(
                src_ref=x_refs[a].at[rows] if from_input else slot(a, block, rows), dst_ref=slot(a, block, rows),
                send_sem=send_sems.at[a, k], recv_sem=recv_sems.at[a, k],
                device_id=to, device_id_type=MESH)

        def mine(a, rows):
            return pltpu.make_async_copy(x_refs[a].at[rows], slot(a, me, rows), local_sems.at[a])

        chunks = [_row_chunks(t.shape, t.dtype) for t in xs]
        whole = [pl.ds(0, t.shape[0]) for t in xs]
        for a in range(n):
            for rows in chunks[a]:
                mine(a, rows).start()
        sent = []
        for a in range(n):
            for k, to in enumerate([sibling] + [(*chip, c) for chip in chips]):
                for rows in chunks[a]:
                    copy(a, k, me, to, rows, from_input=True).start()
                sent.append(copy(a, k, me, to, whole[a], from_input=True))
        for a in range(n):
            for j, chip in enumerate(chips):
                copy(a, 1 + j, (*chip, c), me, whole[a]).wait_recv()
                for rows in chunks[a]:
                    copy(a, 4 + j, (*chip, c), sibling, rows).start()
                sent.append(copy(a, 4 + j, (*chip, c), sibling, whole[a]))
        for a in range(n):
            copy(a, 0, sibling, me, whole[a]).wait_recv()
            for j, chip in enumerate(chips):
                copy(a, 4 + j, (*chip, 1 - c), me, whole[a]).wait_recv()
        for cp in sent:
            cp.wait_send()
        for a in range(n):
            mine(a, whole[a]).wait()

    return pl.pallas_call(
        body, name=name,
        out_shape=[jax.ShapeDtypeStruct((8,) + t.shape, t.dtype) for t in xs],
        in_specs=[ANY] * n, out_specs=[ANY] * n,
        scratch_shapes=[pltpu.SemaphoreType.DMA((n, 7)), pltpu.SemaphoreType.DMA((n, 7)),
                        pltpu.SemaphoreType.DMA((n,))],
    )(*xs)


def _exchange8(gs, name):
    n = len(gs)

    def body(*refs):
        g_refs, r_refs = refs[:n], refs[n:2 * n]
        send_sems, recv_sems, local_sems = refs[2 * n:]
        x, y, c = _position()
        my = 4 * x + 2 * y + c

        def mine(a, rows):
            return pltpu.make_async_copy(g_refs[a].at[my, rows], r_refs[a].at[my, rows], local_sems.at[a])

        def copy(a, m, rows):
            px = (1 - x) if m & 4 else x
            py = (1 - y) if m & 2 else y
            pc = (1 - c) if m & 1 else c
            return pltpu.make_async_remote_copy(
                src_ref=g_refs[a].at[4 * px + 2 * py + pc, rows], dst_ref=r_refs[a].at[my, rows],
                send_sem=send_sems.at[a, m - 1], recv_sem=recv_sems.at[a, m - 1],
                device_id=(px, py, pc), device_id_type=MESH)

        chunks = [_row_chunks(t.shape[1:], t.dtype) for t in gs]
        whole = [pl.ds(0, t.shape[1]) for t in gs]
        for a in range(n):
            for rows in chunks[a]:
                mine(a, rows).start()
        for a in range(n):
            for m in PEER_ORDER:
                for rows in chunks[a]:
                    copy(a, m, rows).start()
        for a in range(n):
            for m in PEER_ORDER:
                copy(a, m, whole[a]).wait_recv()
        for a in range(n):
            for m in PEER_ORDER:
                copy(a, m, whole[a]).wait_send()
            mine(a, whole[a]).wait()

    return pl.pallas_call(
        body, name=name,
        out_shape=[jax.ShapeDtypeStruct(t.shape, t.dtype) for t in gs],
        in_specs=[ANY] * n, out_specs=[ANY] * n,
        scratch_shapes=[pltpu.SemaphoreType.DMA((n, 7)), pltpu.SemaphoreType.DMA((n, 7)),
                        pltpu.SemaphoreType.DMA((n,))],
    )(*gs)


HBM = pl.BlockSpec(memory_space=pltpu.HBM)
SEM = pl.BlockSpec(memory_space=pltpu.SEMAPHORE)
EFFECT = pltpu.SideEffectType.DATAFLOW_SIDE_EFFECTING


def _peer(m, x, y, c):
    return ((1 - x) if m & 4 else x, (1 - y) if m & 2 else y, (1 - c) if m & 1 else c)


def _send_copies(src_refs, land_refs, send_sems, recv_sems, broadcast):
    x, y, c = _position()
    my = 4 * x + 2 * y + c
    out = []
    for a in range(len(src_refs)):
        for m in PEER_ORDER:
            px, py, pc = _peer(m, x, y, c)
            src = src_refs[a] if broadcast else src_refs[a].at[4 * px + 2 * py + pc]
            out.append(pltpu.make_async_remote_copy(
                src_ref=src, dst_ref=land_refs[a].at[my], send_sem=send_sems[a], recv_sem=recv_sems[a],
                device_id=(px, py, pc), device_id_type=MESH))
    return out


def _send_drain(land_refs, send_sems, recv_sems):
    x, y, c = _position()
    for a in range(len(land_refs)):
        seven = land_refs[a].at[pl.ds(0, 7)]
        both = pltpu.make_async_remote_copy(
            src_ref=seven, dst_ref=seven, send_sem=send_sems[a], recv_sem=recv_sems[a],
            device_id=(x, y, c), device_id_type=MESH)
        both.wait_send()
        both.wait_recv()


def _send_start(srcs, lands, after, broadcast, name):
    n = len(srcs)
    extra = [] if after is None else [after]

    def body(*refs):
        src_refs, land_refs = refs[:n], refs[n:2 * n]
        outs = refs[2 * n + len(extra):]
        send_sems, recv_sems = outs[:n], outs[n:2 * n]
        token = refs[-1]
        for cp in _send_copies(src_refs, land_refs, send_sems, recv_sems, broadcast):
            cp.start()
        token[...] = jnp.zeros_like(token)

    hbm = [pltpu.with_memory_space_constraint(t, pltpu.HBM) for t in list(srcs) + list(lands)]
    res = pl.pallas_call(
        body, name=name,
        out_shape=(*[pltpu.SemaphoreType.DMA(())] * (2 * n),
                   *[pltpu.HBM(t.shape, t.dtype) for t in hbm], jax.ShapeDtypeStruct((8, LANE), F32)),
        in_specs=[HBM] * (2 * n) + [ANY] * len(extra),
        out_specs=(*[SEM] * (2 * n), *[HBM] * (2 * n), pl.BlockSpec(memory_space=pltpu.VMEM)),
        input_output_aliases={i: 2 * n + i for i in range(2 * n)},
        compiler_params=pltpu.CompilerParams(has_side_effects=EFFECT),
    )(*hbm, *extra)
    return dict(sems=res[:2 * n], srcs=res[2 * n:3 * n], lands=res[3 * n:4 * n], token=res[-1], broadcast=broadcast)


def _send_wait(started, after, name):
    n = len(started["srcs"])

    def body(*refs):
        land_refs = refs[n:2 * n]
        send_sems, recv_sems = refs[2 * n:3 * n], refs[3 * n:4 * n]
        _send_drain(land_refs, send_sems, recv_sems)

    operands = list(started["srcs"]) + list(started["lands"])
    res = pl.pallas_call(
        body, name=name,
        out_shape=[pltpu.HBM(t.shape, t.dtype) for t in operands],
        in_specs=[HBM] * (2 * n) + [SEM] * (2 * n) + [ANY],
        out_specs=[HBM] * (2 * n),
        input_output_aliases={i: i for i in range(2 * n)},
        compiler_params=pltpu.CompilerParams(has_side_effects=EFFECT),
    )(*operands, *started["sems"], after)
    return res[n:]


def _own_slot(block, dev):
    zone = lax.empty((8,) + block.shape, block.dtype)
    return lax.dynamic_update_slice(zone, block[None], (dev, 0, 0))


def _pair_swap(bufs, name):
    n = len(bufs)
    pieces = [(a, l) for a, t in enumerate(bufs) for l in range(t.shape[0])]

    def body(*refs):
        b_refs = refs[n:2 * n]
        send_sems, recv_sems = refs[2 * n:]
        x, y, c = _position()

        def copy(k, rows):
            a, l = pieces[k]
            half = b_refs[a].at[l, c, rows]
            return pltpu.make_async_remote_copy(
                src_ref=half, dst_ref=half, send_sem=send_sems.at[k], recv_sem=recv_sems.at[k],
                device_id=(x, y, 1 - c), device_id_type=MESH)

        chunks = [_row_chunks(bufs[a].shape[2:], bufs[a].dtype) for a, _ in pieces]
        whole = [pl.ds(0, bufs[a].shape[2]) for a, _ in pieces]
        for k in range(len(pieces)):
            for rows in chunks[k]:
                copy(k, rows).start()
        for k in range(len(pieces)):
            copy(k, whole[k]).wait_recv()
        for k in range(len(pieces)):
            copy(k, whole[k]).wait_send()

    return pl.pallas_call(
        body, name=name,
        out_shape=[jax.ShapeDtypeStruct(t.shape, t.dtype) for t in bufs],
        in_specs=[ANY] * n, out_specs=[ANY] * n,
        input_output_aliases={a: a for a in range(n)},
        scratch_shapes=[pltpu.SemaphoreType.DMA((len(pieces),)), pltpu.SemaphoreType.DMA((len(pieces),))],
    )(*bufs)


def _pack_rows(parts):
    flat = jnp.concatenate([t.reshape(-1).astype(F32) for t in parts])
    pad = (-flat.shape[0]) % (256 * LANE)
    return jnp.pad(flat, (0, pad)).reshape(-1, LANE)


def _my_half(w2d, ci):
    half = w2d.shape[0] // 2
    return lax.dynamic_slice_in_dim(w2d, ci * half, half, axis=0).astype(BF16)


def _col_view(g):
    _, half, Cs = g.shape
    return g.reshape(4, 2 * half, Cs)


def _row_view(g):
    _, half, C = g.shape
    return g.reshape(8 * half, C)


def _rope_tables(S):
    pos = jnp.arange(S, dtype=F32)
    inv_freq = ROPE_THETA ** (-jnp.arange(0, ROPE, 2, dtype=F32) / ROPE)
    ang = pos[:, None] * inv_freq[None, :]
    cos, sin = jnp.cos(ang), jnp.sin(ang)
    cos_t = jnp.concatenate([cos, cos, cos, cos], axis=1)
    sin_t = jnp.concatenate([-sin, sin, -sin, sin], axis=1)
    return cos_t, sin_t


def kernel(x, c, ada_w, ada_b, pre_g, post_g, sgu_w_in, sgu_norm_g, sgu_w_s, sgu_b_s, sgu_w_out, mla_w_in, mla_q_norm_g, mla_kv_norm_g, mla_w_uq, mla_w_ukv, mla_w_out, loss_target, m_ada_w, m_ada_b, m_pre_g, m_post_g, m_sgu_w_in, m_sgu_norm_g, m_sgu_w_s, m_sgu_b_s, m_sgu_w_out, m_mla_w_in, m_mla_q_norm_g, m_mla_kv_norm_g, m_mla_w_uq, m_mla_w_ukv, m_mla_w_out, v_ada_w, v_ada_b, v_pre_g, v_post_g, v_sgu_w_in, v_sgu_norm_g, v_sgu_w_s, v_sgu_b_s, v_sgu_w_out, v_mla_w_in, v_mla_q_norm_g, v_mla_kv_norm_g, v_mla_w_uq, v_mla_w_ukv, v_mla_w_out):
    S, D = x.shape[1], x.shape[2]
    depth = ada_w.shape[0]
    E = sgu_w_out.shape[1] * 4
    xi, yi, ci = _position()
    chip = 2 * xi + yi
    dev = 4 * xi + 2 * yi + ci
    x0 = x.reshape(S, D)
    target = loss_target.reshape(S, D)

    small = _pack_rows([c, mla_q_norm_g, mla_kv_norm_g])
    mixer_w = dict(sin=sgu_w_in, sout=sgu_w_out, min=mla_w_in, uq=mla_w_uq, ukv=mla_w_ukv, mout=mla_w_out)
    small_g, first_g = _all_gather8([small, _my_half(sgu_w_in[0], ci)], "gather_first")
    small_all = small_g.reshape(8, -1)
    gathered_w = {("sin", 0): first_g}
    qn_w, kvn_w = mla_q_norm_g.shape[1], mla_kv_norm_g.shape[1]
    c_all = small_all[:, :D]
    qn_all = small_all[0::2, D:D + 2 * qn_w].reshape(4, 2, qn_w)
    kvn_all = small_all[0::2, D + 2 * qn_w:D + 2 * qn_w + 2 * kvn_w].reshape(4, 2, kvn_w)
    q_gain = jnp.pad(jnp.transpose(qn_all, (1, 0, 2)).reshape(2, 1, Q_RANK), ((0, 0), (0, 0), (0, Q_RANK_PAD - Q_RANK)))
    kv_gain = jnp.transpose(kvn_all, (1, 0, 2)).reshape(2, 1, KV_RANK)

    views = {}

    def weight(t, j):
        if (t, j) not in views:
            g = gathered_w[(t, j)]
            v = _row_view(g) if t in ("sout", "mout") else _col_view(g)
            if t == "uq":
                v = jnp.pad(v, ((0, 0), (0, Q_RANK_PAD - Q_RANK), (0, 0)))
            views[(t, j)] = v
        return views[(t, j)]

    cols = ada_w.shape[2]
    ada_b_cols = lax.dynamic_slice_in_dim(ada_b, chip * cols, cols, axis=1)
    c_pad = jnp.pad(c_all, ((0, 8), (0, 0)))
    mod_cols = _ada_mod(c_pad, ada_w, ada_b_cols, "ada_mod")[:, :8]
    mod_g, = _all_gather8([mod_cols.reshape(depth * 8, cols)], "gather_mod")
    mod_all = jnp.transpose(mod_g[0::2].reshape(4, depth, 8, cols), (1, 2, 0, 3)).reshape(depth, 8, 4 * cols)

    groups = [("sout0", [("sout", 0)]), ("mla0", [(t, 0) for t in ("min", "uq", "ukv", "mout")]),
              ("sgu1", [("sin", 1), ("sout", 1)]), ("mla1", [(t, 1) for t in ("min", "uq", "ukv", "mout")])]
    sends = {}
    behind = mod_g
    for gname, items in groups:
        blocks = [_my_half(mixer_w[t][j], ci) for t, j in items]
        sends[gname] = _send_start(blocks, [_own_slot(b, dev) for b in blocks], behind, True, f"send_{gname}")
        behind = sends[gname]["token"]

    def arrive(gname, after):
        lands = _send_wait(sends[gname], after, f"arrive_{gname}")
        gathered_w.update(zip(dict(groups)[gname], lands))
    mod = lax.dynamic_index_in_dim(mod_all, dev, 1, keepdims=False)
    shift = [mod[i:i + 1, :D] for i in range(depth)]
    scale = [mod[i:i + 1, D:2 * D] for i in range(depth)]
    gate = [mod[i:i + 1, 2 * D:] for i in range(depth)]

    cos_t, sin_t = _rope_tables(S)
    b_bc = jnp.broadcast_to(sgu_b_s[:, :, :, None], sgu_b_s.shape + (LANE,))
    w_sT = jnp.swapaxes(sgu_w_s, 2, 3)

    saved = []
    xs = x0
    for i in range(depth):
        j = i // 2
        tag = f"l{i}"
        h = _pre_fwd(xs, pre_g[i:i + 1], scale[i], shift[i], f"pre_fwd_{tag}", after=(behind,) if i == 0 else ())
        if i % 2 == 0:
            if j > 0:
                arrive(f"sgu{j}", h)
            uvz = _mm(h, weight("sin", j), b_sharded=True, name=f"sgu_in_{tag}")
            y = _sgu_gate_fwd(uvz, sgu_norm_g[j:j + 1], sgu_w_s[j], b_bc[j], f"sgu_gate_fwd_{tag}")
            if j == 0:
                arrive("sout0", y)
            out = _mm(y, weight("sout", j), name=f"sgu_out_{tag}")
            saved.append(dict(x=xs, h=h, uvz=uvz, y=y, out=out))
        else:
            arrive(f"mla{j}", h)
            p = _mm(h, weight("min", j), b_sharded=True, name=f"mla_in_{tag}")
            cqn, ckvn = _mla_mid_fwd(p, q_gain[j], kv_gain[j], f"mla_mid_fwd_{tag}")
            q = _mm(cqn, weight("uq", j), b_sharded=True, name=f"mla_uq_{tag}")
            kv = _mm(ckvn, weight("ukv", j), b_sharded=True, name=f"mla_ukv_{tag}")
            Q, K, V = _mla_pack(q, kv, p, cos_t, sin_t, f"mla_pack_{tag}")
            o, lse = _attn_fwd(Q, K, V, f"attn_fwd_{tag}")
            y = _mla_gate_fwd(o, p, f"mla_gate_fwd_{tag}")
            out = _mm(y, weight("mout", j), name=f"mla_out_{tag}")
            saved.append(dict(x=xs, h=h, p=p, cqn=cqn, ckvn=ckvn, Q=Q, K=K, V=V, o=o, lse=lse, y=y, out=out))
        xs = _post_fwd(xs, out, gate[i], post_g[i:i + 1], f"post_fwd_{tag}")

    dx, loss_part = _loss_grad(xs, target, "loss")
    loss = lax.psum(loss_part[0, 0], ("x", "y", "c"))

    dmod = [None] * depth
    d_pre_g = [None] * depth
    d_post_g = [None] * depth
    d_sgu = [None] * 2
    d_mla = [None] * 2
    kinds = ("sgu_w_in", "sgu_w_out", "mla_w_in", "mla_w_uq", "mla_w_ukv", "mla_w_out")
    halves = dict.fromkeys(kinds)
    in_flight = []

    def send_grads(items, label):
        slices = [dw.reshape(8, -1, dw.shape[-1]) for _, _, dw in items]
        lands = [_own_slot(lax.dynamic_index_in_dim(s, dev, 0, keepdims=False), dev) for s in slices]
        started = _send_start(slices, lands, None, False, f"send_{label}")
        in_flight.append((started, [(kind, layer) for kind, layer, _ in items], label))
        return (started["token"],)

    def collect(count, after):
        for _ in range(count):
            started, keys, label = in_flight.pop(0)
            lands = _send_wait(started, after, f"arrive_{label}")
            for (kind, layer), r in zip(keys, lands):
                halves[kind] = _sum_into_half(r, halves[kind], layer, ci, 2, f"sum_{kind}_{layer}")

    for i in reversed(range(depth)):
        j = i // 2
        tag = f"l{i}"
        sv = saved[i]
        older = len(in_flight)
        dy, dgate, d_post_g[i] = _post_bwd(dx, sv["out"], gate[i], post_g[i:i + 1], f"post_bwd_{tag}")
        if i % 2 == 0:
            dw_out = _mm(sv["y"], dy, ta=True, out_dtype=BF16, name=f"sgu_out_dw_{tag}")
            sent = send_grads([("sgu_w_out", j, dw_out)], f"{tag}_out")
            dyv = _mm(dy, weight("sout", j), tb=True, after=sent, name=f"sgu_out_dx_{tag}")
            duvz, dng, dws, dbs = _sgu_gate_bwd(sv["uvz"], dyv, sgu_norm_g[j:j + 1], sgu_w_s[j], w_sT[j], b_bc[j],
                                                f"sgu_gate_bwd_{tag}")
            sent = ()
            if i > 0:
                dw_in = _mm(sv["h"], duvz, ta=True, out_sharded=True, out_dtype=BF16, name=f"sgu_in_dw_{tag}")
                sent = send_grads([("sgu_w_in", j, dw_in)], f"{tag}_in")
            dh = _mm(duvz, weight("sin", j), tb=True, b_sharded=True, after=sent, name=f"sgu_in_dx_{tag}")
            d_sgu[j] = dict(norm_g=dng, w_s=dws, b_s=dbs[:, :, 0], duvz=duvz)
        else:
            dw_out = _mm(sv["y"], dy, ta=True, out_dtype=BF16, name=f"mla_out_dw_{tag}")
            sent = send_grads([("mla_w_out", j, dw_out)], f"{tag}_out")
            dyv = _mm(dy, weight("mout", j), tb=True, after=sent, name=f"mla_out_dx_{tag}")
            p = sv["p"]
            do = _mla_gate_bwd(dyv, p, f"mla_gate_bwd_{tag}")
            dQ, dK, dV = _attn_bwd(sv["Q"], sv["K"], sv["V"], sv["o"], do, sv["lse"], f"attn_bwd_{tag}")
            dq, dkv, dkr = _mla_unpack(dQ, dK, dV, cos_t, sin_t, f"mla_unpack_{tag}")
            dw_uq = _mm(sv["cqn"], dq, ta=True, out_sharded=True, out_dtype=BF16, name=f"mla_uq_dw_{tag}")
            dcqn = _mm(dq, weight("uq", j), tb=True, b_sharded=True, name=f"mla_uq_dx_{tag}")
            dw_ukv = _mm(sv["ckvn"], dkv, ta=True, out_sharded=True, out_dtype=BF16, name=f"mla_ukv_dw_{tag}")
            dckvn = _mm(dkv, weight("ukv", j), tb=True, b_sharded=True, name=f"mla_ukv_dx_{tag}")
            dp, dqg, dkvg = _mla_mid_bwd(p, dcqn, dckvn, dkr, dyv, sv["o"], q_gain[j], kv_gain[j], f"mla_mid_bwd_{tag}")
            dw_in = _mm(sv["h"], dp, ta=True, out_sharded=True, out_dtype=BF16, name=f"mla_in_dw_{tag}")
            sent = send_grads([("mla_w_in", j, dw_in), ("mla_w_uq", j, dw_uq[:, :Q_RANK]), ("mla_w_ukv", j, dw_ukv)],
                              f"{tag}_in")
            dh = _mm(dp, weight("min", j), tb=True, b_sharded=True, after=sent, name=f"mla_in_dx_{tag}")
            d_mla[j] = dict(qg=dqg[0, :Q_RANK], kvg=dkvg[0])
        dx, dshift, dscale, d_pre_g[i] = _pre_bwd(dh, sv["x"], dx, pre_g[i:i + 1], scale[i], f"pre_bwd_{tag}")
        dmod[i] = jnp.concatenate([dshift, dscale, dgate], axis=1)
        collect(older, dx)
    grad_x = dx.reshape(x.shape)

    parts = [jnp.concatenate(dmod, axis=0), jnp.concatenate(d_pre_g, axis=0), jnp.concatenate(d_post_g, axis=0),
             jnp.stack([d["norm_g"][0] for d in d_sgu]), jnp.stack([d["w_s"] for d in d_sgu]),
             jnp.stack([d["b_s"] for d in d_sgu]), jnp.stack([d["qg"] for d in d_mla]),
             jnp.stack([d["kvg"] for d in d_mla])]
    sizes = [int(np.prod(t.shape)) for t in parts]
    packed = _pack_rows(parts)
    packed_all, dmod_all = _all_gather8([packed, parts[0]], "gather_small_grads")
    total = _sum_slots(packed_all, "sum_small_grads").reshape(-1)
    offs = np.concatenate([[0], np.cumsum(sizes)])
    pieces = [total[int(offs[t]):int(offs[t + 1])].reshape(parts[t].shape) for t in range(len(parts))]
    g_ada_b, g_pre_g, g_post_g, g_norm_g, g_w_s, g_b_s, g_qg_full, g_kvg_full = pieces
    g_qg = lax.dynamic_slice_in_dim(g_qg_full, chip * qn_w, qn_w, axis=1)
    g_kvg = lax.dynamic_slice_in_dim(g_kvg_full, chip * kvn_w, kvn_w, axis=1)
    dmod_cols = jnp.stack([lax.dynamic_slice_in_dim(dmod_all[:, i], chip * cols, cols, axis=1) for i in range(depth)])
    dmod_cols = jnp.pad(dmod_cols, ((0, 0), (0, LANE - 8), (0, 0)))

    dw_in0 = _mm(saved[0]["h"], d_sgu[0]["duvz"], ta=True, out_sharded=True, out_dtype=BF16, after=(packed_all,),
                 name="sgu_in_dw_l0")
    sent = send_grads([("sgu_w_in", 0, dw_in0)], "l0_in")
    g_ada_w = _ada_grad(jnp.pad(c_all.T, ((0, 0), (0, LANE - 8))), dmod_cols, "ada_grad", after=sent)

    wnames = ["ada_w", "ada_b", "pre_g", "post_g", "sgu_w_in", "sgu_norm_g", "sgu_w_s", "sgu_b_s", "sgu_w_out",
              "mla_w_in", "mla_q_norm_g", "mla_kv_norm_g", "mla_w_uq", "mla_w_ukv", "mla_w_out"]
    weights = dict(zip(wnames, [ada_w, ada_b, pre_g, post_g, sgu_w_in, sgu_norm_g, sgu_w_s, sgu_b_s, sgu_w_out,
                                mla_w_in, mla_q_norm_g, mla_kv_norm_g, mla_w_uq, mla_w_ukv, mla_w_out]))
    ms = dict(zip(wnames, [m_ada_w, m_ada_b, m_pre_g, m_post_g, m_sgu_w_in, m_sgu_norm_g, m_sgu_w_s, m_sgu_b_s,
                           m_sgu_w_out, m_mla_w_in, m_mla_q_norm_g, m_mla_kv_norm_g, m_mla_w_uq, m_mla_w_ukv,
                           m_mla_w_out]))
    vs = dict(zip(wnames, [v_ada_w, v_ada_b, v_pre_g, v_post_g, v_sgu_w_in, v_sgu_norm_g, v_sgu_w_s, v_sgu_b_s,
                           v_sgu_w_out, v_mla_w_in, v_mla_q_norm_g, v_mla_kv_norm_g, v_mla_w_uq, v_mla_w_ukv,
                           v_mla_w_out]))
    grads = dict(ada_w=g_ada_w, ada_b=g_ada_b, pre_g=g_pre_g, post_g=g_post_g, sgu_norm_g=g_norm_g, sgu_w_s=g_w_s,
                 sgu_b_s=g_b_s, mla_q_norm_g=g_qg, mla_kv_norm_g=g_kvg)
    stepped = {}

    def step(nm):
        grads[nm] = grads[nm].reshape(weights[nm].shape)
        stepped[nm] = _adamw(weights[nm], grads[nm], ms[nm], vs[nm], f"adamw_{nm}")

    for nm in wnames:
        if nm in grads:
            step(nm)
    early = [kind for kind in kinds if kind != "sgu_w_in"]
    collect(len(in_flight) - 1, stepped["ada_w"][0])
    for kind, g in zip(early, _pair_swap([halves[kind] for kind in early], "swap_grads")):
        grads[kind] = g
        step(kind)
    collect(len(in_flight), stepped[early[-1]][0])
    grads["sgu_w_in"], = _pair_swap([halves["sgu_w_in"]], "swap_grads_last")
    step("sgu_w_in")
    return (loss, grad_x, *[grads[nm] for nm in wnames], *[stepped[nm][0] for nm in wnames],
            *[stepped[nm][1] for nm in wnames], *[stepped[nm][2] for nm in wnames])
```

```python
import functools
import math

import jax
import jax.numpy as jnp
import numpy as np
from jax import lax
from jax.experimental import pallas as pl
from jax.experimental.pallas import tpu as pltpu

F32 = jnp.float32
BF16 = jnp.bfloat16
MESH = pl.DeviceIdType.MESH

NORM_EPS = 1e-6
CHUNK = 64
SGU_BLOCK = 128
SGU_GROUPS = 16
HEADS = 16
NOPE = 128
ROPE = 64
VDIM = 128
QK_PAD = 256
Q_RANK = 448
Q_RANK_PAD = 512
KV_RANK = 512
ROPE_THETA = 10000.0
ATTN_SCALE = (NOPE + ROPE) ** -0.5
LOG2E = 1.4426950408889634
Q_FOLD = ATTN_SCALE * LOG2E

ADAM_LR = 0.001
ADAM_B1 = 0.9
ADAM_B2 = 0.999
ADAM_EPS = 1e-08
ADAM_WD = 0.01
ADAM_STEP = 10

LANE = 128
VMEM_LIMIT = 48 * 1024 * 1024

NN = (((1,), (0,)), ((), ()))
NT = (((1,), (1,)), ((), ()))
TN = (((0,), (0,)), ((), ()))


def _params(*sem):
    return pltpu.CompilerParams(dimension_semantics=sem, vmem_limit_bytes=VMEM_LIMIT)


def _row_tile(rows, row_bytes, target_bytes=1 << 20):
    if rows * row_bytes <= target_bytes or rows % 16:
        return rows
    best = 16
    t = 16
    while t <= rows:
        if rows % t == 0 and t * row_bytes <= target_bytes:
            best = t
        t += 16
    return best


def _fit(dim, target):
    if dim <= target:
        return dim
    t = (target // LANE) * LANE
    while t > LANE and dim % t:
        t -= LANE
    return t


def _gelu(x):
    return 0.5 * x * (1.0 + lax.erf(x * 0.7071067811865476))


def _gelu_grad(x):
    return 0.5 * (1.0 + lax.erf(x * 0.7071067811865476)) + x * jnp.exp(-0.5 * x * x) * 0.3989422804014327


def _mm(a, b, *, ta=False, tb=False, b_sharded=False, out_sharded=False, out_dtype=F32,
        tm=1024, tn=1024, tk=2048, after=(), name):
    if ta:
        K, M = a.shape
    else:
        M, K = a.shape
    if b_sharded:
        shards, rows, Cs = b.shape
        b_shape = (rows, shards * Cs)
    else:
        b_shape = b.shape
    if tb:
        N, K2 = b_shape
    else:
        K2, N = b_shape
    assert K == K2, (a.shape, b.shape, ta, tb)
    n_lim = Cs if (b_sharded and not tb) else (N // 4 if out_sharded else N)
    k_lim = Cs if (b_sharded and tb) else K
    tm, tn, tk = _fit(M, tm), _fit(n_lim, tn), _fit(k_lim, tk)
    assert M % tm == 0 and n_lim % tn == 0 and k_lim % tk == 0, (M, N, K, tm, tn, tk)
    nk = K // tk
    nb_n = n_lim // tn
    nb_k = k_lim // tk
    dims = (((0 if ta else 1,), (1 if tb else 0,)), ((), ()))

    def body(a_ref, b_ref, *rest):
        o_ref, *scratch = rest[len(after):]
        prod = lax.dot_general(a_ref[...].astype(BF16), b_ref[...].astype(BF16), dims,
                               preferred_element_type=F32)
        if nk == 1:
            o_ref[...] = prod.astype(out_dtype)
        else:
            acc_ref, = scratch
            k = pl.program_id(2)

            @pl.when(k == 0)
            def _():
                acc_ref[...] = prod

            @pl.when(k > 0)
            def _():
                acc_ref[...] += prod

            @pl.when(k == nk - 1)
            def _():
                o_ref[...] = acc_ref[...].astype(out_dtype)

    a_spec = (pl.BlockSpec((tk, tm), lambda i, j, k: (k, i)) if ta
              else pl.BlockSpec((tm, tk), lambda i, j, k: (i, k)))
    if b_sharded and tb:
        b_spec = pl.BlockSpec((None, tn, tk), lambda i, j, k: (k // nb_k, j, k % nb_k))
    elif b_sharded:
        b_spec = pl.BlockSpec((None, tk, tn), lambda i, j, k: (j // nb_n, k, j % nb_n))
    elif tb:
        b_spec = pl.BlockSpec((tn, tk), lambda i, j, k: (j, k))
    else:
        b_spec = pl.BlockSpec((tk, tn), lambda i, j, k: (k, j))
    if out_sharded:
        out_shape = jax.ShapeDtypeStruct((4, M, N // 4), out_dtype)
        out_spec = pl.BlockSpec((None, tm, tn), lambda i, j, k: (j // nb_n, i, j % nb_n))
    else:
        out_shape = jax.ShapeDtypeStruct((M, N), out_dtype)
        out_spec = pl.BlockSpec((tm, tn), lambda i, j, k: (i, j))
    return pl.pallas_call(
        body, name=name,
        out_shape=out_shape,
        grid=(M // tm, N // tn, nk),
        in_specs=[a_spec, b_spec] + [pl.BlockSpec(memory_space=pl.ANY)] * len(after),
        out_specs=out_spec,
        scratch_shapes=[] if nk == 1 else [pltpu.VMEM((tm, tn), F32)],
        compiler_params=_params("parallel", "parallel", "arbitrary"),
    )(a, b, *after)


def _split_bf16(v):
    hi = v.astype(BF16)
    lo = (v - hi.astype(F32)).astype(BF16)
    return hi, lo


def _dot3(a, b, dims):
    a_hi, a_lo = _split_bf16(a)
    b_hi, b_lo = _split_bf16(b)
    out = lax.dot_general(a_hi, b_hi, dims, preferred_element_type=F32)
    out += lax.dot_general(a_lo, b_hi, dims, preferred_element_type=F32)
    out += lax.dot_general(a_hi, b_lo, dims, preferred_element_type=F32)
    return out


def _ada_mod(c_all, ada_w, ada_b_cols, name):
    L, D, cols = ada_w.shape
    B = c_all.shape[0]
    tn = 512 if cols % 512 == 0 else cols

    def body(c_ref, w_ref, b_ref, o_ref):
        cv = c_ref[...]
        cond = cv * jax.nn.sigmoid(cv)
        o_ref[...] = _dot3(cond, w_ref[...], NN) + b_ref[...]

    return pl.pallas_call(
        body, name=name,
        out_shape=jax.ShapeDtypeStruct((L, B, cols), F32),
        grid=(L, cols // tn),
        in_specs=[pl.BlockSpec((B, D), lambda l, j: (0, 0)),
                  pl.BlockSpec((None, D, tn), lambda l, j: (l, 0, j)),
                  pl.BlockSpec((None, 1, tn), lambda l, j: (l, 0, j))],
        out_specs=pl.BlockSpec((None, B, tn), lambda l, j: (l, 0, j)),
        compiler_params=_params("parallel", "parallel"),
    )(c_all, ada_w, ada_b_cols.reshape(L, 1, cols))


def _ada_grad(c_t, dmod_cols, name, after=()):
    L, B, cols = dmod_cols.shape
    D = c_t.shape[0]
    tn = 512 if cols % 512 == 0 else cols

    def body(c_ref, d_ref, *rest):
        o_ref = rest[-1]
        cv = c_ref[...]
        cond = cv * jax.nn.sigmoid(cv)
        o_ref[...] = _dot3(cond, d_ref[...], NN)

    return pl.pallas_call(
        body, name=name,
        out_shape=jax.ShapeDtypeStruct((L, D, cols), F32),
        grid=(L, cols // tn),
        in_specs=[pl.BlockSpec((D, B), lambda l, j: (0, 0)),
                  pl.BlockSpec((None, B, tn), lambda l, j: (l, 0, j))] + [pl.BlockSpec(memory_space=pl.ANY)] * len(after),
        out_specs=pl.BlockSpec((None, D, tn), lambda l, j: (l, 0, j)),
        compiler_params=_params("parallel", "parallel"),
    )(c_t, dmod_cols, *after)


def _row_spec(ts, width):
    return pl.BlockSpec((ts, width), lambda i: (i, 0))


def _vec_spec(width):
    return pl.BlockSpec((1, width), lambda i: (0, 0))


def _pre_fwd(x, pre_g, scale, shift, name, after=()):
    S, D = x.shape
    ts = min(256, S)

    def body(x_ref, g_ref, sc_ref, sh_ref, *rest):
        h_ref = rest[-1]
        xv = x_ref[...]
        r = lax.rsqrt(jnp.mean(xv * xv, axis=-1, keepdims=True) + NORM_EPS)
        h_ref[...] = ((xv * r * g_ref[...]) * (1.0 + sc_ref[...]) + sh_ref[...]).astype(BF16)

    return pl.pallas_call(
        body, name=name, out_shape=jax.ShapeDtypeStruct((S, D), BF16), grid=(S // ts,),
        in_specs=[_row_spec(ts, D), _vec_spec(D), _vec_spec(D), _vec_spec(D)]
        + [pl.BlockSpec(memory_space=pl.ANY)] * len(after),
        out_specs=_row_spec(ts, D), compiler_params=_params("parallel"),
    )(x, pre_g, scale, shift, *after)


def _pre_bwd(dh, x, dx_res, pre_g, scale, name):
    S, D = x.shape
    ts = min(256, S)

    def body(dh_ref, x_ref, dr_ref, g_ref, sc_ref, dx_ref, dsh_ref, dsc_ref, dg_ref):
        @pl.when(pl.program_id(0) == 0)
        def _():
            dsh_ref[...] = jnp.zeros_like(dsh_ref)
            dsc_ref[...] = jnp.zeros_like(dsc_ref)
            dg_ref[...] = jnp.zeros_like(dg_ref)

        dh = dh_ref[...]
        xv = x_ref[...]
        g = g_ref[...]
        one_sc = 1.0 + sc_ref[...]
        r = lax.rsqrt(jnp.mean(xv * xv, axis=-1, keepdims=True) + NORM_EPS)
        xn = xv * r
        dsh_ref[...] += jnp.sum(dh, axis=0, keepdims=True)
        dsc_ref[...] += jnp.sum(dh * (xn * g), axis=0, keepdims=True)
        dg_ref[...] += jnp.sum(dh * one_sc * xn, axis=0, keepdims=True)
        dxn = dh * one_sc * g
        dx_ref[...] = dr_ref[...] + r * (dxn - xn * jnp.mean(dxn * xn, axis=-1, keepdims=True))

    vec = jax.ShapeDtypeStruct((1, D), F32)
    return pl.pallas_call(
        body, name=name, out_shape=(jax.ShapeDtypeStruct((S, D), F32), vec, vec, vec), grid=(S // ts,),
        in_specs=[_row_spec(ts, D), _row_spec(ts, D), _row_spec(ts, D), _vec_spec(D), _vec_spec(D)],
        out_specs=(_row_spec(ts, D), _vec_spec(D), _vec_spec(D), _vec_spec(D)),
        compiler_params=_params("arbitrary"),
    )(dh, x, dx_res, pre_g, scale)


def _post_fwd(x, y, gate, post_g, name):
    S, D = x.shape
    ts = min(256, S)

    def body(x_ref, y_ref, gt_ref, g_ref, o_ref):
        yv = y_ref[...]
        r = lax.rsqrt(jnp.mean(yv * yv, axis=-1, keepdims=True) + NORM_EPS)
        o_ref[...] = x_ref[...] + gt_ref[...] * (yv * r * g_ref[...])

    return pl.pallas_call(
        body, name=name, out_shape=jax.ShapeDtypeStruct((S, D), F32), grid=(S // ts,),
        in_specs=[_row_spec(ts, D), _row_spec(ts, D), _vec_spec(D), _vec_spec(D)],
        out_specs=_row_spec(ts, D), compiler_params=_params("parallel"),
    )(x, y, gate, post_g)


def _post_bwd(dx, y, gate, post_g, name):
    S, D = y.shape
    ts = min(256, S)

    def body(dx_ref, y_ref, gt_ref, g_ref, dy_ref, dgt_ref, dg_ref):
        @pl.when(pl.program_id(0) == 0)
        def _():
            dgt_ref[...] = jnp.zeros_like(dgt_ref)
            dg_ref[...] = jnp.zeros_like(dg_ref)

        dxv = dx_ref[...]
        yv = y_ref[...]
        g = g_ref[...]
        gt = gt_ref[...]
        r = lax.rsqrt(jnp.mean(yv * yv, axis=-1, keepdims=True) + NORM_EPS)
        yn = yv * r
        dgt_ref[...] += jnp.sum(dxv * (yn * g), axis=0, keepdims=True)
        dg_ref[...] += jnp.sum(dxv * gt * yn, axis=0, keepdims=True)
        dyn = dxv * gt * g
        dy_ref[...] = (r * (dyn - yn * jnp.mean(dyn * yn, axis=-1, keepdims=True))).astype(BF16)

    vec = jax.ShapeDtypeStruct((1, D), F32)
    return pl.pallas_call(
        body, name=name, out_shape=(jax.ShapeDtypeStruct((S, D), BF16), vec, vec), grid=(S // ts,),
        in_specs=[_row_spec(ts, D), _row_spec(ts, D), _vec_spec(D), _vec_spec(D)],
        out_specs=(_row_spec(ts, D), _vec_spec(D), _vec_spec(D)),
        compiler_params=_params("arbitrary"),
    )(dx, y, gate, post_g)


def _loss_grad(xf, target, name):
    S, D = xf.shape
    ts = min(256, S)

    def body(x_ref, t_ref, dx_ref, l_ref):
        @pl.when(pl.program_id(0) == 0)
        def _():
            l_ref[...] = jnp.zeros_like(l_ref)

        e = x_ref[...] - t_ref[...]
        dx_ref[...] = e * (1.0 / D)
        row = jnp.sum(e * e, axis=1, keepdims=True) * (1.0 / D)
        l_ref[...] += 0.5 * jnp.sum(row, axis=0, keepdims=True)

    return pl.pallas_call(
        body, name=name,
        out_shape=(jax.ShapeDtypeStruct((S, D), F32), jax.ShapeDtypeStruct((1, 1), F32)), grid=(S // ts,),
        in_specs=[_row_spec(ts, D), _row_spec(ts, D)],
        out_specs=(_row_spec(ts, D), pl.BlockSpec((1, 1), lambda i: (0, 0))),
        compiler_params=_params("arbitrary"),
    )(xf, target)


def _chunk_mask(transposed=False):
    row = lax.broadcasted_iota(jnp.int32, (SGU_BLOCK, SGU_BLOCK), 0) // CHUNK
    col = lax.broadcasted_iota(jnp.int32, (SGU_BLOCK, SGU_BLOCK), 1) // CHUNK
    return (row <= col) if transposed else (col <= row)


def _sgu_gate_fwd(uvz, norm_g, w_s, b_bc, name):
    S, E3 = uvz.shape
    E = E3 // 3
    T = SGU_BLOCK
    gd = E // SGU_GROUPS

    def body(uvz_ref, ng_ref, ws_ref, bb_ref, y_ref, v_scr):
        gv = _gelu(uvz_ref[:, E:2 * E])
        mu = jnp.mean(gv, axis=-1, keepdims=True)
        xc = gv - mu
        rstd = lax.rsqrt(jnp.mean(xc * xc, axis=-1, keepdims=True) + NORM_EPS)
        v_scr[...] = (xc * rstd * ng_ref[...]).astype(BF16)
        mask = _chunk_mask()
        for g in range(SGU_GROUPS):
            sl = slice(g * gd, (g + 1) * gd)
            wg = jnp.where(mask, ws_ref[g], 0.0).astype(BF16)
            vm = lax.dot_general(wg, v_scr[:, sl], NN, preferred_element_type=F32)
            vm = vm + jnp.tile(bb_ref[g], (1, gd // LANE))
            z = uvz_ref[:, 2 * E + g * gd:2 * E + (g + 1) * gd]
            y_ref[:, sl] = (_gelu(uvz_ref[:, sl]) * vm * (z * jax.nn.sigmoid(z))).astype(BF16)

    return pl.pallas_call(
        body, name=name, out_shape=jax.ShapeDtypeStruct((S, E), BF16), grid=(S // T,),
        in_specs=[_row_spec(T, E3), _vec_spec(E),
                  pl.BlockSpec((SGU_GROUPS, T, T), lambda i: (0, 0, 0)),
                  pl.BlockSpec((SGU_GROUPS, T, LANE), lambda i: (0, 0, 0))],
        out_specs=_row_spec(T, E),
        scratch_shapes=[pltpu.VMEM((T, E), BF16)],
        compiler_params=_params("parallel"),
    )(uvz, norm_g, w_s, b_bc)


def _sgu_gate_bwd(uvz, dyv, norm_g, w_s, w_sT, b_bc, name):
    S, E3 = uvz.shape
    E = E3 // 3
    T = SGU_BLOCK
    gd = E // SGU_GROUPS

    def body(uvz_ref, dyv_ref, ng_ref, ws_ref, wst_ref, bb_ref,
             d_ref, dng_ref, dws_ref, dbs_ref, vhat_scr, dv_scr):
        @pl.when(pl.program_id(0) == 0)
        def _():
            dng_ref[...] = jnp.zeros_like(dng_ref)
            dws_ref[...] = jnp.zeros_like(dws_ref)
            dbs_ref[...] = jnp.zeros_like(dbs_ref)

        gv = _gelu(uvz_ref[:, E:2 * E])
        mu = jnp.mean(gv, axis=-1, keepdims=True)
        xc = gv - mu
        rstd = lax.rsqrt(jnp.mean(xc * xc, axis=-1, keepdims=True) + NORM_EPS)
        vhat_scr[...] = xc * rstd
        mask = _chunk_mask()
        mask_t = _chunk_mask(transposed=True)
        for g in range(SGU_GROUPS):
            sl = slice(g * gd, (g + 1) * gd)
            u_pre = uvz_ref[:, sl]
            z = uvz_ref[:, 2 * E + g * gd:2 * E + (g + 1) * gd]
            dy = dyv_ref[:, sl]
            u = _gelu(u_pre)
            sig = jax.nn.sigmoid(z)
            sz = z * sig
            vg = (vhat_scr[:, sl] * ng_ref[:, sl]).astype(BF16)
            wg = jnp.where(mask, ws_ref[g], 0.0).astype(BF16)
            vm = lax.dot_general(wg, vg, NN, preferred_element_type=F32)
            vm = vm + jnp.tile(bb_ref[g], (1, gd // LANE))
            dy_u = dy * u
            d_ref[:, sl] = (dy * vm * sz * _gelu_grad(u_pre)).astype(BF16)
            d_ref[:, 2 * E + g * gd:2 * E + (g + 1) * gd] = (
                dy_u * vm * (sig * (1.0 + z * (1.0 - sig)))).astype(BF16)
            dvm = dy_u * sz
            dvm_b = dvm.astype(BF16)
            dws_ref[g] += jnp.where(mask, lax.dot_general(dvm_b, vg, NT, preferred_element_type=F32), 0.0)
            dbs_ref[g] += jnp.broadcast_to(jnp.sum(dvm, axis=1, keepdims=True), (T, LANE))
            wgt = jnp.where(mask_t, wst_ref[g], 0.0).astype(BF16)
            dv_scr[:, sl] = lax.dot_general(wgt, dvm_b, NN, preferred_element_type=F32)
        dv = dv_scr[...]
        vhat = vhat_scr[...]
        dng_ref[...] += jnp.sum(dv * vhat, axis=0, keepdims=True)
        dvh = dv * ng_ref[...]
        dgv = rstd * (dvh - jnp.mean(dvh, axis=-1, keepdims=True)
                      - vhat * jnp.mean(dvh * vhat, axis=-1, keepdims=True))
        d_ref[:, E:2 * E] = (dgv * _gelu_grad(uvz_ref[:, E:2 * E])).astype(BF16)

    wspec = pl.BlockSpec((SGU_GROUPS, T, T), lambda i: (0, 0, 0))
    bspec = pl.BlockSpec((SGU_GROUPS, T, LANE), lambda i: (0, 0, 0))
    return pl.pallas_call(
        body, name=name,
        out_shape=(jax.ShapeDtypeStruct((S, E3), BF16), jax.ShapeDtypeStruct((1, E), F32),
                   jax.ShapeDtypeStruct((SGU_GROUPS, T, T), F32),
                   jax.ShapeDtypeStruct((SGU_GROUPS, T, LANE), F32)),
        grid=(S // T,),
        in_specs=[_row_spec(T, E3), _row_spec(T, E), _vec_spec(E), wspec, wspec, bspec],
        out_specs=(_row_spec(T, E3), _vec_spec(E), wspec, bspec),
        scratch_shapes=[pltpu.VMEM((T, E), F32), pltpu.VMEM((T, E), F32)],
        compiler_params=_params("arbitrary"),
    )(uvz, dyv, norm_g, w_s, w_sT, b_bc)


MLA_WIDTH = HEADS * VDIM
P_LATENT = Q_RANK + KV_RANK + ROPE
P_WIDTH = P_LATENT + MLA_WIDTH


def _swap_halves(v):
    lane = lax.broadcasted_iota(jnp.int32, v.shape, 1)
    return jnp.where(lane % ROPE < ROPE // 2, pltpu.roll(v, LANE - ROPE // 2, 1), pltpu.roll(v, ROPE // 2, 1))


def _low_lanes(rows):
    return lax.broadcasted_iota(jnp.int32, (rows, LANE), 1) < ROPE


def _latent_tiles(ref):
    return [ref[:, t * LANE:(t + 1) * LANE] for t in range(P_LATENT // LANE)]


def _split_latents(tiles, low):
    cq = jnp.concatenate(tiles[0:3] + [jnp.where(low, tiles[3], 0.0)], axis=1)
    rolled = [pltpu.roll(t, ROPE, 1) for t in tiles[3:8]]
    ckv = jnp.concatenate([jnp.where(low, rolled[t], rolled[t + 1]) for t in range(4)], axis=1)
    kr = jnp.where(low, rolled[4], 0.0)
    return cq, ckv, kr


def _mla_mid_fwd(p, qg, kvg, name):
    S, PW = p.shape
    ts = min(256, S)

    def body(p_ref, qg_ref, kvg_ref, cqn_ref, ckvn_ref):
        cq, ckv, _ = _split_latents(_latent_tiles(p_ref), _low_lanes(ts))
        r = lax.rsqrt(jnp.sum(cq * cq, axis=-1, keepdims=True) * (1.0 / Q_RANK) + NORM_EPS)
        cqn_ref[...] = (cq * r * qg_ref[...]).astype(BF16)
        r2 = lax.rsqrt(jnp.mean(ckv * ckv, axis=-1, keepdims=True) + NORM_EPS)
        ckvn_ref[...] = (ckv * r2 * kvg_ref[...]).astype(BF16)

    return pl.pallas_call(
        body, name=name,
        out_shape=(jax.ShapeDtypeStruct((S, Q_RANK_PAD), BF16), jax.ShapeDtypeStruct((S, KV_RANK), BF16)),
        grid=(S // ts,),
        in_specs=[_row_spec(ts, P_LATENT), _vec_spec(Q_RANK_PAD), _vec_spec(KV_RANK)],
        out_specs=(_row_spec(ts, Q_RANK_PAD), _row_spec(ts, KV_RANK)),
        compiler_params=_params("parallel"),
    )(p, qg, kvg)


def _mla_pack(q, kv, p, cos_t, sin_t, name):
    S = q.shape[0]
    ts = min(256, S)
    pair_w = 2 * (NOPE + ROPE)
    head_w = NOPE + VDIM

    def body(q_ref, kv_ref, kr_ref, cos_ref, sin_ref, qo_ref, ko_ref, vo_ref):
        cosv = cos_ref[...]
        sinv = sin_ref[...]
        low = _low_lanes(ts)
        kr = jnp.where(low, pltpu.roll(kr_ref[...], ROPE, 1), 0.0)
        kr = (kr * cosv + _swap_halves(kr) * sinv).astype(BF16)
        for pair in range(HEADS // 2):
            t0, t1, t2 = (q_ref[:, pair * pair_w + t * LANE:pair * pair_w + (t + 1) * LANE] for t in range(3))
            nope_b = jnp.where(low, pltpu.roll(t1, ROPE, 1), pltpu.roll(t2, ROPE, 1))
            ropes = jnp.where(low, t1, t2)
            roped = (ropes * cosv + _swap_halves(ropes) * sinv) * Q_FOLD
            qo_ref[2 * pair, :, 0:NOPE] = (t0 * Q_FOLD).astype(BF16)
            qo_ref[2 * pair, :, NOPE:QK_PAD] = jnp.where(low, roped, 0.0).astype(BF16)
            qo_ref[2 * pair + 1, :, 0:NOPE] = (nope_b * Q_FOLD).astype(BF16)
            qo_ref[2 * pair + 1, :, NOPE:QK_PAD] = jnp.where(low, pltpu.roll(roped, ROPE, 1), 0.0).astype(BF16)
        for h in range(HEADS):
            ko_ref[h, :, 0:NOPE] = kv_ref[:, h * head_w:h * head_w + NOPE].astype(BF16)
            ko_ref[h, :, NOPE:QK_PAD] = kr
            vo_ref[h] = kv_ref[:, h * head_w + NOPE:(h + 1) * head_w].astype(BF16)

    return pl.pallas_call(
        body, name=name,
        out_shape=(jax.ShapeDtypeStruct((HEADS, S, QK_PAD), BF16), jax.ShapeDtypeStruct((HEADS, S, QK_PAD), BF16),
                   jax.ShapeDtypeStruct((HEADS, S, VDIM), BF16)),
        grid=(S // ts,),
        in_specs=[_row_spec(ts, q.shape[1]), _row_spec(ts, kv.shape[1]),
                  pl.BlockSpec((ts, LANE), lambda i: (i, P_LATENT // LANE - 1)),
                  _row_spec(ts, LANE), _row_spec(ts, LANE)],
        out_specs=(pl.BlockSpec((HEADS, ts, QK_PAD), lambda i: (0, i, 0)),
                   pl.BlockSpec((HEADS, ts, QK_PAD), lambda i: (0, i, 0)),
                   pl.BlockSpec((HEADS, ts, VDIM), lambda i: (0, i, 0))),
        compiler_params=_params("parallel"),
    )(q, kv, p, cos_t, sin_t)


def _mla_unpack(dQ, dK, dV, cos_t, sin_t, name):
    S = dQ.shape[1]
    ts = min(256, S)
    pair_w = 2 * (NOPE + ROPE)
    head_w = NOPE + VDIM

    def body(dq_ref, dk_ref, dv_ref, cos_ref, sin_ref, q_ref, kv_ref, kr_ref):
        cosv = cos_ref[...]
        sinv = sin_ref[...]
        low = _low_lanes(ts)
        for pair in range(HEADS // 2):
            blk = dq_ref[2 * pair, :, NOPE:QK_PAD] + pltpu.roll(dq_ref[2 * pair + 1, :, NOPE:QK_PAD], ROPE, 1)
            ropes = blk * cosv - _swap_halves(blk) * sinv
            nope_b = pltpu.roll(dq_ref[2 * pair + 1, :, 0:NOPE], ROPE, 1)
            q_ref[:, pair * pair_w:pair * pair_w + LANE] = dq_ref[2 * pair, :, 0:NOPE].astype(BF16)
            q_ref[:, pair * pair_w + LANE:pair * pair_w + 2 * LANE] = jnp.where(low, ropes, nope_b).astype(BF16)
            q_ref[:, pair * pair_w + 2 * LANE:(pair + 1) * pair_w] = jnp.where(low, nope_b, ropes).astype(BF16)
        dkr = dk_ref[0, :, NOPE:QK_PAD]
        for h in range(1, HEADS):
            dkr = dkr + dk_ref[h, :, NOPE:QK_PAD]
        kr_ref[...] = dkr * cosv - _swap_halves(dkr) * sinv
        for h in range(HEADS):
            kv_ref[:, h * head_w:h * head_w + NOPE] = dk_ref[h, :, 0:NOPE].astype(BF16)
            kv_ref[:, h * head_w + NOPE:(h + 1) * head_w] = dv_ref[h].astype(BF16)

    return pl.pallas_call(
        body, name=name,
        out_shape=(jax.ShapeDtypeStruct((S, HEADS * (NOPE + ROPE)), BF16),
                   jax.ShapeDtypeStruct((S, HEADS * (NOPE + VDIM)), BF16),
                   jax.ShapeDtypeStruct((S, LANE), F32)),
        grid=(S // ts,),
        in_specs=[pl.BlockSpec((HEADS, ts, QK_PAD), lambda i: (0, i, 0)),
                  pl.BlockSpec((HEADS, ts, QK_PAD), lambda i: (0, i, 0)),
                  pl.BlockSpec((HEADS, ts, VDIM), lambda i: (0, i, 0)),
                  _row_spec(ts, LANE), _row_spec(ts, LANE)],
        out_specs=(_row_spec(ts, HEADS * (NOPE + ROPE)), _row_spec(ts, HEADS * (NOPE + VDIM)),
                   _row_spec(ts, LANE)),
        compiler_params=_params("parallel"),
    )(dQ, dK, dV, cos_t, sin_t)


def _mla_gate_fwd(o, p, name):
    S, W = o.shape
    ts = min(256, S)
    wb = P_LATENT

    def body(o_ref, z_ref, y_ref):
        z = z_ref[...]
        y_ref[...] = (o_ref[...] * (z * jax.nn.sigmoid(z))).astype(BF16)

    return pl.pallas_call(
        body, name=name, out_shape=jax.ShapeDtypeStruct((S, W), BF16), grid=(S // ts, W // wb),
        in_specs=[pl.BlockSpec((ts, wb), lambda i, j: (i, j)), pl.BlockSpec((ts, wb), lambda i, j: (i, j + 1))],
        out_specs=pl.BlockSpec((ts, wb), lambda i, j: (i, j)), compiler_params=_params("parallel", "parallel"),
    )(o, p)


def _mla_gate_bwd(dyv, p, name):
    S, W = dyv.shape
    ts = min(256, S)
    wb = P_LATENT

    def body(d_ref, z_ref, do_ref):
        z = z_ref[...]
        do_ref[...] = d_ref[...] * (z * jax.nn.sigmoid(z))

    return pl.pallas_call(
        body, name=name, out_shape=jax.ShapeDtypeStruct((S, W), F32), grid=(S // ts, W // wb),
        in_specs=[pl.BlockSpec((ts, wb), lambda i, j: (i, j)), pl.BlockSpec((ts, wb), lambda i, j: (i, j + 1))],
        out_specs=pl.BlockSpec((ts, wb), lambda i, j: (i, j)), compiler_params=_params("parallel", "parallel"),
    )(dyv, p)


def _mla_mid_bwd(p, dcqn, dckvn, dkr, dyv, o, qg, kvg, name):
    S, PW = p.shape
    W = o.shape[1]
    ts = min(256, S)
    nt = Q_RANK_PAD // LANE

    def rms_bwd(xv, dy, g, count):
        r = lax.rsqrt(jnp.sum(xv * xv, axis=-1, keepdims=True) * (1.0 / count) + NORM_EPS)
        xn = xv * r
        dg = jnp.sum(dy * xn, axis=0, keepdims=True)
        dxn = dy * g
        dx = r * (dxn - xn * (jnp.sum(dxn * xn, axis=-1, keepdims=True) * (1.0 / count)))
        return dx, dg

    def body(p_ref, dcq_ref, dckv_ref, dkr_ref, dyv_ref, o_ref, qg_ref, kvg_ref, dp_ref, dqg_ref, dkvg_ref):
        @pl.when(pl.program_id(0) == 0)
        def _():
            dqg_ref[...] = jnp.zeros_like(dqg_ref)
            dkvg_ref[...] = jnp.zeros_like(dkvg_ref)

        low = _low_lanes(ts)
        cq, ckv, _ = _split_latents(_latent_tiles(p_ref), low)
        dcq, dg = rms_bwd(cq, dcq_ref[...], qg_ref[...], Q_RANK)
        dqg_ref[...] += dg
        dckv, dg = rms_bwd(ckv, dckv_ref[...], kvg_ref[...], KV_RANK)
        dkvg_ref[...] += dg
        moved = [pltpu.roll(dckv[:, t * LANE:(t + 1) * LANE], ROPE, 1) for t in range(nt)]
        moved.append(pltpu.roll(dkr_ref[...], ROPE, 1))
        for t in range(nt - 1):
            dp_ref[:, t * LANE:(t + 1) * LANE] = dcq[:, t * LANE:(t + 1) * LANE].astype(BF16)
        dp_ref[:, (nt - 1) * LANE:nt * LANE] = jnp.where(low, dcq[:, (nt - 1) * LANE:nt * LANE], moved[0]).astype(BF16)
        for t in range(nt):
            dp_ref[:, (nt + t) * LANE:(nt + t + 1) * LANE] = jnp.where(low, moved[t], moved[t + 1]).astype(BF16)
        z = p_ref[:, P_LATENT:PW]
        sig = jax.nn.sigmoid(z)
        dp_ref[:, P_LATENT:PW] = (dyv_ref[...] * o_ref[...] * (sig * (1.0 + z * (1.0 - sig)))).astype(BF16)

    return pl.pallas_call(
        body, name=name,
        out_shape=(jax.ShapeDtypeStruct((S, PW), BF16), jax.ShapeDtypeStruct((1, Q_RANK_PAD), F32),
                   jax.ShapeDtypeStruct((1, KV_RANK), F32)),
        grid=(S // ts,),
        in_specs=[_row_spec(ts, PW), _row_spec(ts, Q_RANK_PAD), _row_spec(ts, KV_RANK), _row_spec(ts, LANE),
                  _row_spec(ts, W), _row_spec(ts, W), _vec_spec(Q_RANK_PAD), _vec_spec(KV_RANK)],
        out_specs=(_row_spec(ts, PW), _vec_spec(Q_RANK_PAD), _vec_spec(KV_RANK)),
        compiler_params=_params("arbitrary"),
    )(p, dcqn, dckvn, dkr, dyv, o, qg, kvg)


def _tile_mask(T):
    row = lax.broadcasted_iota(jnp.int32, (T, T), 0) // CHUNK
    col = lax.broadcasted_iota(jnp.int32, (T, T), 1) // CHUNK
    return col <= row


def _attn_fwd(Q, K, V, name):
    H, S, _ = Q.shape
    T = min(512, S)
    n_part = 2 if T % 256 == 0 else 1
    Tq = T // n_part

    def body(q_ref, k_ref, v_ref, o_ref, lse_ref, m_scr, l_scr, acc_scr):
        qi = pl.program_id(1)
        q = q_ref[...]
        m_scr[...] = jnp.full_like(m_scr, -jnp.inf)
        l_scr[...] = jnp.zeros_like(l_scr)
        acc_scr[...] = jnp.zeros_like(acc_scr)

        def tile(j, masked):
            rows = pl.ds(pl.multiple_of(j * T, T), T)
            kt = k_ref[rows, :]
            vt = v_ref[rows, :]
            for part in range(n_part):
                sub = slice(part * Tq, (part + 1) * Tq)
                s = lax.dot_general(q[sub], kt, NT, preferred_element_type=F32)
                if masked:
                    row = (lax.broadcasted_iota(jnp.int32, (Tq, T), 0) + part * Tq) // CHUNK
                    col = lax.broadcasted_iota(jnp.int32, (Tq, T), 1) // CHUNK
                    s = jnp.where(col <= row, s, -1e30)
                m_prev = m_scr[sub]
                m_new = jnp.maximum(m_prev, jnp.max(s, axis=1, keepdims=True))
                pr = jnp.exp2(s - m_new)
                alpha = jnp.exp2(m_prev - m_new)
                l_scr[sub] = alpha * l_scr[sub] + jnp.sum(pr, axis=1, keepdims=True)
                acc_scr[sub] = alpha * acc_scr[sub] + lax.dot_general(
                    pr.astype(BF16), vt, NN, preferred_element_type=F32)
                m_scr[sub] = m_new

        def full_tile(j, carry):
            tile(j, False)
            return carry

        lax.fori_loop(0, qi, full_tile, 0)
        tile(qi, True)
        l = l_scr[...]
        o_ref[...] = acc_scr[...] / l
        lse_ref[...] = jnp.broadcast_to(m_scr[...] + jnp.log2(l), (T, LANE))

    return pl.pallas_call(
        body, name=name,
        out_shape=(jax.ShapeDtypeStruct((S, H * VDIM), F32), jax.ShapeDtypeStruct((H, S, LANE), F32)),
        grid=(H, S // T),
        in_specs=[pl.BlockSpec((None, T, QK_PAD), lambda h, i: (h, i, 0)),
                  pl.BlockSpec((None, S, QK_PAD), lambda h, i: (h, 0, 0)),
                  pl.BlockSpec((None, S, VDIM), lambda h, i: (h, 0, 0))],
        out_specs=(pl.BlockSpec((T, VDIM), lambda h, i: (i, h)),
                   pl.BlockSpec((None, T, LANE), lambda h, i: (h, i, 0))),
        scratch_shapes=[pltpu.VMEM((T, 1), F32), pltpu.VMEM((T, 1), F32), pltpu.VMEM((T, VDIM), F32)],
        compiler_params=_params("parallel", "arbitrary"),
    )(Q, K, V)


def _attn_bwd(Q, K, V, o, do, lse, name):
    H, S, _ = Q.shape
    T = min(512, S)
    nq = S // T

    def body(q_ref, k_ref, v_ref, o_ref, do_ref, lse_ref, dq_ref, dk_ref, dv_ref, dk_scr, dv_scr):
        ki = pl.program_id(1)

        @pl.when(ki == 0)
        def _():
            dq_ref[...] = jnp.zeros_like(dq_ref)

        dk_scr[...] = jnp.zeros_like(dk_scr)
        dv_scr[...] = jnp.zeros_like(dv_scr)
        k = k_ref[...]
        v = v_ref[...]

        def tile(i, masked):
            rows = pl.ds(pl.multiple_of(i * T, T), T)
            q = q_ref[rows, :]
            do_f = do_ref[rows, :]
            do_b = do_f.astype(BF16)
            delta = jnp.sum(do_f * o_ref[rows, :], axis=1, keepdims=True)
            s = lax.dot_general(q, k, NT, preferred_element_type=F32)
            pr = jnp.exp2(s - lse_ref[rows, 0:1])
            if masked:
                pr = jnp.where(_tile_mask(T), pr, 0.0)
            dv_scr[...] += lax.dot_general(pr.astype(BF16), do_b, TN, preferred_element_type=F32)
            dp = lax.dot_general(do_b, v, NT, preferred_element_type=F32)
            ds = (pr * (dp - delta)).astype(BF16)
            dk_scr[...] += lax.dot_general(ds, q, TN, preferred_element_type=F32)
            dq_ref[rows, :] += lax.dot_general(ds, k, NN, preferred_element_type=F32) * ATTN_SCALE

        def full_tile(i, carry):
            tile(i, False)
            return carry

        tile(ki, True)
        lax.fori_loop(ki + 1, nq, full_tile, 0)
        dk_ref[...] = dk_scr[...] * (1.0 / LOG2E)
        dv_ref[...] = dv_scr[...]

    return pl.pallas_call(
        body, name=name,
        out_shape=(jax.ShapeDtypeStruct((H, S, QK_PAD), F32), jax.ShapeDtypeStruct((H, S, QK_PAD), F32),
                   jax.ShapeDtypeStruct((H, S, VDIM), F32)),
        grid=(H, nq),
        in_specs=[pl.BlockSpec((None, S, QK_PAD), lambda h, j: (h, 0, 0)),
                  pl.BlockSpec((None, T, QK_PAD), lambda h, j: (h, j, 0)),
                  pl.BlockSpec((None, T, VDIM), lambda h, j: (h, j, 0)),
                  pl.BlockSpec((S, VDIM), lambda h, j: (0, h)),
                  pl.BlockSpec((S, VDIM), lambda h, j: (0, h)),
                  pl.BlockSpec((None, S, LANE), lambda h, j: (h, 0, 0))],
        out_specs=(pl.BlockSpec((None, S, QK_PAD), lambda h, j: (h, 0, 0)),
                   pl.BlockSpec((None, T, QK_PAD), lambda h, j: (h, j, 0)),
                   pl.BlockSpec((None, T, VDIM), lambda h, j: (h, j, 0))),
        scratch_shapes=[pltpu.VMEM((T, QK_PAD), F32), pltpu.VMEM((T, VDIM), F32)],
        compiler_params=_params("parallel", "arbitrary"),
    )(Q, K, V, o, do, lse)


def _tile_transpose(x, T, name):
    H, S, W = x.shape

    def body(x_ref, o_ref):
        o_ref[...] = jnp.transpose(x_ref[...].astype(F32)).astype(x.dtype)

    return pl.pallas_call(
        body, name=name, out_shape=jax.ShapeDtypeStruct((H, S // T, W, T), x.dtype), grid=(H, S // T),
        in_specs=[pl.BlockSpec((None, T, W), lambda h, i: (h, i, 0))],
        out_specs=pl.BlockSpec((None, None, W, T), lambda h, i: (h, i, 0, 0)),
        compiler_params=_params("parallel", "parallel"),
    )(x)


def _tile_untranspose(xt, name):
    H, nT, W, T = xt.shape

    def body(x_ref, o_ref):
        o_ref[...] = jnp.transpose(x_ref[...])

    return pl.pallas_call(
        body, name=name, out_shape=jax.ShapeDtypeStruct((H, nT * T, W), xt.dtype), grid=(H, nT),
        in_specs=[pl.BlockSpec((None, None, W, T), lambda h, i: (h, i, 0, 0))],
        out_specs=pl.BlockSpec((None, T, W), lambda h, i: (h, i, 0)),
        compiler_params=_params("parallel", "parallel"),
    )(xt)


def _key_le_query(rows, cols, col0):
    key = lax.broadcasted_iota(jnp.int32, (rows, cols), 0) // CHUNK
    query = (lax.broadcasted_iota(jnp.int32, (rows, cols), 1) + col0) // CHUNK
    return key <= query


def _attn_fwd_t(QT, K, VT, name):
    H, nT, _, T = QT.shape
    S = nT * T
    n_part = 2 if T % 256 == 0 else 1
    Tq = T // n_part

    def body(qt_ref, k_ref, vt_ref, o_ref, lse_ref, m_scr, l_scr, acc_scr, s_scr):
        qi = pl.program_id(1)
        m_scr[...] = jnp.full_like(m_scr, -jnp.inf)
        l_scr[...] = jnp.zeros_like(l_scr)
        acc_scr[...] = jnp.zeros_like(acc_scr)

        def scores(j):
            kt = k_ref[pl.ds(pl.multiple_of(j * T, T), T), :]
            return lax.dot_general(kt, qt_ref[...], NN, preferred_element_type=F32)

        def softmax_step(j, masked):
            vt = vt_ref[j]
            for part in range(n_part):
                sub = slice(part * Tq, (part + 1) * Tq)
                st = s_scr[:, sub]
                if masked:
                    st = jnp.where(_key_le_query(T, Tq, part * Tq), st, -1e30)
                m_prev = m_scr[:, sub]
                m_new = jnp.maximum(m_prev, jnp.max(st, axis=0, keepdims=True))
                pt = jnp.exp2(st - m_new)
                alpha = jnp.exp2(m_prev - m_new)
                l_scr[:, sub] = alpha * l_scr[:, sub] + jnp.sum(pt, axis=0, keepdims=True)
                acc_scr[:, sub] = alpha * acc_scr[:, sub] + lax.dot_general(
                    vt, pt.astype(BF16), NN, preferred_element_type=F32)
                m_scr[:, sub] = m_new

        s_scr[...] = scores(0)

        def step(j, carry):
            nxt = scores(j + 1)
            softmax_step(j, False)
            s_scr[...] = nxt
            return carry

        lax.fori_loop(0, qi, step, 0)
        softmax_step(qi, True)
        l = l_scr[...]
        o_ref[...] = jnp.transpose(acc_scr[...] / l)
        lse_ref[...] = m_scr[...] + jnp.log2(l)

    return pl.pallas_call(
        body, name=name,
        out_shape=(jax.ShapeDtypeStruct((S, H * VDIM), F32), jax.ShapeDtypeStruct((H, nT, 1, T), F32)),
        grid=(H, nT),
        in_specs=[pl.BlockSpec((None, None, QK_PAD, T), lambda h, i: (h, i, 0, 0)),
                  pl.BlockSpec((None, S, QK_PAD), lambda h, i: (h, 0, 0)),
                  pl.BlockSpec((None, nT, VDIM, T), lambda h, i: (h, 0, 0, 0))],
        out_specs=(pl.BlockSpec((T, VDIM), lambda h, i: (i, h)),
                   pl.BlockSpec((None, None, 1, T), lambda h, i: (h, i, 0, 0))),
        scratch_shapes=[pltpu.VMEM((1, T), F32), pltpu.VMEM((1, T), F32), pltpu.VMEM((VDIM, T), F32),
                        pltpu.VMEM((T, T), F32)],
        compiler_params=_params("parallel", "arbitrary"),
    )(QT, K, VT)


def _attn_bwd_t(QT, Q, K, KT, V, o, do, lse, name):
    H, nT, _, T = QT.shape
    S = nT * T
    n_part = 2 if T % 256 == 0 else 1
    Tq = T // n_part

    def body(qt_ref, q_ref, k_ref, kt_ref, v_ref, o_ref, do_ref, lse_ref, dqt_ref, dk_ref, dv_ref,
             dk_scr, dv_scr, s_scr, dp_scr):
        ki = pl.program_id(1)

        @pl.when(ki == 0)
        def _():
            dqt_ref[...] = jnp.zeros_like(dqt_ref)

        dk_scr[...] = jnp.zeros_like(dk_scr)
        dv_scr[...] = jnp.zeros_like(dv_scr)
        k = k_ref[...]
        kt = kt_ref[...]
        v = v_ref[...]
        ones = jnp.ones((8, VDIM), BF16)

        def scores(i):
            do_b = do_ref[pl.ds(pl.multiple_of(i * T, T), T), :].astype(BF16)
            st = lax.dot_general(k, qt_ref[i], NN, preferred_element_type=F32)
            dpt = lax.dot_general(v, do_b, NT, preferred_element_type=F32)
            return st, dpt

        def grads(i, st_of, dpt_of, masked):
            for part in range(n_part):
                sub = slice(part * Tq, (part + 1) * Tq)
                qrows = pl.ds(pl.multiple_of(i * T + part * Tq, Tq), Tq)
                do_f = do_ref[qrows, :]
                prod = do_f * o_ref[qrows, :]
                hi = prod.astype(BF16)
                lo = (prod - hi.astype(F32)).astype(BF16)
                delta = (lax.dot_general(ones, hi, NT, preferred_element_type=F32)
                         + lax.dot_general(ones, lo, NT, preferred_element_type=F32))[0:1]
                pt = jnp.exp2(st_of(sub) - lse_ref[i, :, sub])
                if masked:
                    pt = jnp.where(_key_le_query(T, Tq, part * Tq), pt, 0.0)
                dv_scr[...] += lax.dot_general(pt.astype(BF16), do_f.astype(BF16), NN, preferred_element_type=F32)
                dst = (pt * (dpt_of(sub) - delta)).astype(BF16)
                dk_scr[...] += lax.dot_general(dst, q_ref[qrows, :], NN, preferred_element_type=F32)
                dqt_ref[i, :, sub] += lax.dot_general(kt, dst, NN, preferred_element_type=F32) * ATTN_SCALE

        def from_scratch(ref):
            return lambda sub: ref[:, sub]

        st, dpt = scores(ki)
        grads(ki, lambda sub: st[:, sub], lambda sub: dpt[:, sub], True)

        @pl.when(ki + 1 < nT)
        def _():
            s_scr[...], dp_scr[...] = scores(ki + 1)

            def step(i, carry):
                nxt_s, nxt_dp = scores(i + 1)
                grads(i, from_scratch(s_scr), from_scratch(dp_scr), False)
                s_scr[...] = nxt_s
                dp_scr[...] = nxt_dp
                return carry

            lax.fori_loop(ki + 1, nT - 1, step, 0)
            grads(nT - 1, from_scratch(s_scr), from_scratch(dp_scr), False)

        dk_ref[...] = dk_scr[...] * (1.0 / LOG2E)
        dv_ref[...] = dv_scr[...]

    per_head = lambda h, j: (h, 0, 0, 0)
    return pl.pallas_call(
        body, name=name,
        out_shape=(jax.ShapeDtypeStruct((H, nT, QK_PAD, T), F32), jax.ShapeDtypeStruct((H, S, QK_PAD), F32),
                   jax.ShapeDtypeStruct((H, S, VDIM), F32)),
        grid=(H, nT),
        in_specs=[pl.BlockSpec((None, nT, QK_PAD, T), per_head),
                  pl.BlockSpec((None, S, QK_PAD), lambda h, j: (h, 0, 0)),
                  pl.BlockSpec((None, T, QK_PAD), lambda h, j: (h, j, 0)),
                  pl.BlockSpec((None, None, QK_PAD, T), lambda h, j: (h, j, 0, 0)),
                  pl.BlockSpec((None, T, VDIM), lambda h, j: (h, j, 0)),
                  pl.BlockSpec((S, VDIM), lambda h, j: (0, h)),
                  pl.BlockSpec((S, VDIM), lambda h, j: (0, h)),
                  pl.BlockSpec((None, nT, 1, T), per_head)],
        out_specs=(pl.BlockSpec((None, nT, QK_PAD, T), per_head),
                   pl.BlockSpec((None, T, QK_PAD), lambda h, j: (h, j, 0)),
                   pl.BlockSpec((None, T, VDIM), lambda h, j: (h, j, 0))),
        scratch_shapes=[pltpu.VMEM((T, QK_PAD), F32), pltpu.VMEM((T, VDIM), F32),
                        pltpu.VMEM((T, T), F32), pltpu.VMEM((T, T), F32)],
        compiler_params=_params("parallel", "arbitrary"),
    )(QT, Q, K, KT, V, o, do, lse)


def _adamw(w, g, m, v, name):
    shape = w.shape
    C = shape[-1]
    R = math.prod(shape[:-1])
    flat = [t.reshape(R, C) for t in (w, g, m, v)]
    tr = _row_tile(R, C * 4)

    def body(w_ref, g_ref, m_ref, v_ref, d_ref, nm_ref, nv_ref):
        gv = g_ref[...]
        m_new = ADAM_B1 * m_ref[...] + (1.0 - ADAM_B1) * gv
        v_new = ADAM_B2 * v_ref[...] + (1.0 - ADAM_B2) * jnp.square(gv)
        m_hat = m_new / (1.0 - ADAM_B1 ** ADAM_STEP)
        v_hat = v_new / (1.0 - ADAM_B2 ** ADAM_STEP)
        d_ref[...] = -ADAM_LR * (m_hat / (jnp.sqrt(v_hat) + ADAM_EPS) + ADAM_WD * w_ref[...])
        nm_ref[...] = m_new
        nv_ref[...] = v_new

    spec = pl.BlockSpec((tr, C), lambda i: (i, 0))
    out = jax.ShapeDtypeStruct((R, C), F32)
    d, nm, nv = pl.pallas_call(
        body, name=name, out_shape=(out, out, out), grid=(R // tr,),
        in_specs=[spec] * 4, out_specs=(spec, spec, spec), compiler_params=_params("parallel"),
    )(*flat)
    return d.reshape(shape), nm.reshape(shape), nv.reshape(shape)


def _sum_into_half(r, buf, layer, ci, n_layers, name):
    n, M, N = r.shape
    tr = _row_tile(M, N * 4 * n, 4 << 20)

    def body(c_ref, r_ref, *rest):
        o_ref = rest[-1]
        acc = r_ref[0].astype(F32)
        for s in range(1, n):
            acc = acc + r_ref[s].astype(F32)
        o_ref[...] = acc

    in_specs = [pl.BlockSpec((n, tr, N), lambda i, c: (0, i, 0))]
    operands = [ci.reshape(1), r]
    aliases = {}
    if buf is not None:
        in_specs.append(ANY)
        operands.append(buf)
        aliases = {2: 0}
    return pl.pallas_call(
        body, name=name, out_shape=jax.ShapeDtypeStruct((n_layers, 2, M, N), F32),
        grid_spec=pltpu.PrefetchScalarGridSpec(
            num_scalar_prefetch=1, grid=(M // tr,), in_specs=in_specs,
            out_specs=pl.BlockSpec((None, None, tr, N), lambda i, c: (layer, c[0], i, 0))),
        input_output_aliases=aliases, compiler_params=_params("parallel"),
    )(*operands)


def _sum_slots(r, name):
    n, M, N = r.shape
    tr = _row_tile(M, N * 4 * n, 4 << 20)

    def body(r_ref, o_ref):
        acc = r_ref[0].astype(F32)
        for s in range(1, n):
            acc = acc + r_ref[s].astype(F32)
        o_ref[...] = acc

    return pl.pallas_call(
        body, name=name, out_shape=jax.ShapeDtypeStruct((M, N), F32), grid=(M // tr,),
        in_specs=[pl.BlockSpec((n, tr, N), lambda i: (0, i, 0))],
        out_specs=pl.BlockSpec((tr, N), lambda i: (i, 0)), compiler_params=_params("parallel"),
    )(r)


ANY = pl.BlockSpec(memory_space=pl.ANY)
DMA_CHUNK_BYTES = 1 << 20
DMA_MAX_CHUNKS = 16
PEER_ORDER = (1, 4, 5, 2, 3, 6, 7)


def _position():
    return lax.axis_index("x"), lax.axis_index("y"), lax.axis_index("c")


def _row_chunks(shape, dtype):
    rows, cols = shape
    n = max(1, min(DMA_MAX_CHUNKS, rows * cols * jnp.dtype(dtype).itemsize // DMA_CHUNK_BYTES))
    while n > 1 and (rows % n or (rows // n) % 16):
        n -= 1
    step = rows // n
    return [pl.ds(q * step, step) for q in range(n)]


def _all_gather8(xs, name):
    n = len(xs)

    def body(*refs):
        x_refs, o_refs = refs[:n], refs[n:2 * n]
        send_sems, recv_sems, local_sems = refs[2 * n:]
        x, y, c = _position()
        me, sibling = (x, y, c), (x, y, 1 - c)
        chips = [(1 - x, y), (x, 1 - y), (1 - x, 1 - y)]

        def slot(a, dev, rows):
            return o_refs[a].at[4 * dev[0] + 2 * dev[1] + dev[2], rows]

        def copy(a, k, block, to, rows, from_input=False):
            return pltpu.make_async_remote_copy(
                src_ref=x_refs[a].at[rows] if from_input else slot(a, block, rows), dst_ref=slot(a, block, rows),
                send_sem=send_sems.at[a, k], recv_sem=recv_sems.at[a, k],
                device_id=to, device_id_type=MESH)

        def mine(a, rows):
            return pltpu.make_async_copy(x_refs[a].at[rows], slot(a, me, rows), local_sems.at[a])

        chunks = [_row_chunks(t.shape, t.dtype) for t in xs]
        whole = [pl.ds(0, t.shape[0]) for t in xs]
        for a in range(n):
            for rows in chunks[a]:
                mine(a, rows).start()
        sent = []
        for a in range(n):
            for k, to in enumerate([sibling] + [(*chip, c) for chip in chips]):
                for rows in chunks[a]:
                    copy(a, k, me, to, rows, from_input=True).start()
                sent.append(copy(a, k, me, to, whole[a], from_input=True))
        for a in range(n):
            for j, chip in enumerate(chips):
                copy(a, 1 + j, (*chip, c), me, whole[a]).wait_recv()
                for rows in chunks[a]:
                    copy(a, 4 + j, (*chip, c), sibling, rows).start()
                sent.append(copy(a, 4 + j, (*chip, c), sibling, whole[a]))
        for a in range(n):
            copy(a, 0, sibling, me, whole[a]).wait_recv()
            for j, chip in enumerate(chips):
                copy(a, 4 + j, (*chip, 1 - c), me, whole[a]).wait_recv()
        for cp in sent:
            cp.wait_send()
        for a in range(n):
            mine(a, whole[a]).wait()

    return pl.pallas_call(
        body, name=name,
        out_shape=[jax.ShapeDtypeStruct((8,) + t.shape, t.dtype) for t in xs],
        in_specs=[ANY] * n, out_specs=[ANY] * n,
        scratch_shapes=[pltpu.SemaphoreType.DMA((n, 7)), pltpu.SemaphoreType.DMA((n, 7)),
                        pltpu.SemaphoreType.DMA((n,))],
    )(*xs)


def _exchange8(gs, name):
    n = len(gs)

    def body(*refs):
        g_refs, r_refs = refs[:n], refs[n:2 * n]
        send_sems, recv_sems, local_sems = refs[2 * n:]
        x, y, c = _position()
        my = 4 * x + 2 * y + c

        def mine(a, rows):
            return pltpu.make_async_copy(g_refs[a].at[my, rows], r_refs[a].at[my, rows], local_sems.at[a])

        def copy(a, m, rows):
            px = (1 - x) if m & 4 else x
            py = (1 - y) if m & 2 else y
            pc = (1 - c) if m & 1 else c
            return pltpu.make_async_remote_copy(
                src_ref=g_refs[a].at[4 * px + 2 * py + pc, rows], dst_ref=r_refs[a].at[my, rows],
                send_sem=send_sems.at[a, m - 1], recv_sem=recv_sems.at[a, m - 1],
                device_id=(px, py, pc), device_id_type=MESH)

        chunks = [_row_chunks(t.shape[1:], t.dtype) for t in gs]
        whole = [pl.ds(0, t.shape[1]) for t in gs]
        for a in range(n):
            for rows in chunks[a]:
                mine(a, rows).start()
        for a in range(n):
            for m in PEER_ORDER:
                for rows in chunks[a]:
                    copy(a, m, rows).start()
        for a in range(n):
            for m in PEER_ORDER:
                copy(a, m, whole[a]).wait_recv()
        for a in range(n):
            for m in PEER_ORDER:
                copy(a, m, whole[a]).wait_send()
            mine(a, whole[a]).wait()

    return pl.pallas_call(
        body, name=name,
        out_shape=[jax.ShapeDtypeStruct(t.shape, t.dtype) for t in gs],
        in_specs=[ANY] * n, out_specs=[ANY] * n,
        scratch_shapes=[pltpu.SemaphoreType.DMA((n, 7)), pltpu.SemaphoreType.DMA((n, 7)),
                        pltpu.SemaphoreType.DMA((n,))],
    )(*gs)


HBM = pl.BlockSpec(memory_space=pltpu.HBM)
SEM = pl.BlockSpec(memory_space=pltpu.SEMAPHORE)
EFFECT = pltpu.SideEffectType.DATAFLOW_SIDE_EFFECTING


def _peer(m, x, y, c):
    return ((1 - x) if m & 4 else x, (1 - y) if m & 2 else y, (1 - c) if m & 1 else c)


def _send_copies(src_refs, land_refs, send_sems, recv_sems, broadcast):
    x, y, c = _position()
    my = 4 * x + 2 * y + c
    out = []
    for a in range(len(src_refs)):
        for m in PEER_ORDER:
            px, py, pc = _peer(m, x, y, c)
            src = src_refs[a] if broadcast else src_refs[a].at[4 * px + 2 * py + pc]
            out.append(pltpu.make_async_remote_copy(
                src_ref=src, dst_ref=land_refs[a].at[my], send_sem=send_sems[a], recv_sem=recv_sems[a],
                device_id=(px, py, pc), device_id_type=MESH))
    return out


def _send_drain(land_refs, send_sems, recv_sems):
    x, y, c = _position()
    for a in range(len(land_refs)):
        seven = land_refs[a].at[pl.ds(0, 7)]
        both = pltpu.make_async_remote_copy(
            src_ref=seven, dst_ref=seven, send_sem=send_sems[a], recv_sem=recv_sems[a],
            device_id=(x, y, c), device_id_type=MESH)
        both.wait_send()
        both.wait_recv()


def _send_start(srcs, lands, after, broadcast, name):
    n = len(srcs)
    extra = [] if after is None else [after]

    def body(*refs):
        src_refs, land_refs = refs[:n], refs[n:2 * n]
        outs = refs[2 * n + len(extra):]
        send_sems, recv_sems = outs[:n], outs[n:2 * n]
        token = refs[-1]
        for cp in _send_copies(src_refs, land_refs, send_sems, recv_sems, broadcast):
            cp.start()
        token[...] = jnp.zeros_like(token)

    hbm = [pltpu.with_memory_space_constraint(t, pltpu.HBM) for t in list(srcs) + list(lands)]
    res = pl.pallas_call(
        body, name=name,
        out_shape=(*[pltpu.SemaphoreType.DMA(())] * (2 * n),
                   *[pltpu.HBM(t.shape, t.dtype) for t in hbm], jax.ShapeDtypeStruct((8, LANE), F32)),
        in_specs=[HBM] * (2 * n) + [ANY] * len(extra),
        out_specs=(*[SEM] * (2 * n), *[HBM] * (2 * n), pl.BlockSpec(memory_space=pltpu.VMEM)),
        input_output_aliases={i: 2 * n + i for i in range(2 * n)},
        compiler_params=pltpu.CompilerParams(has_side_effects=EFFECT),
    )(*hbm, *extra)
    return dict(sems=res[:2 * n], srcs=res[2 * n:3 * n], lands=res[3 * n:4 * n], token=res[-1], broadcast=broadcast)


def _send_wait(started, after, name):
    n = len(started["srcs"])

    def body(*refs):
        land_refs = refs[n:2 * n]
        send_sems, recv_sems = refs[2 * n:3 * n], refs[3 * n:4 * n]
        _send_drain(land_refs, send_sems, recv_sems)

    operands = list(started["srcs"]) + list(started["lands"])
    res = pl.pallas_call(
        body, name=name,
        out_shape=[pltpu.HBM(t.shape, t.dtype) for t in operands],
        in_specs=[HBM] * (2 * n) + [SEM] * (2 * n) + [ANY],
        out_specs=[HBM] * (2 * n),
        input_output_aliases={i: i for i in range(2 * n)},
        compiler_params=pltpu.CompilerParams(has_side_effects=EFFECT),
    )(*operands, *started["sems"], after)
    return res[n:]


def _own_slot(block, dev):
    zone = lax.empty((8,) + block.shape, block.dtype)
    return lax.dynamic_update_slice(zone, block[None], (dev, 0, 0))


def _pair_swap(bufs, name):
    n = len(bufs)
    pieces = [(a, l) for a, t in enumerate(bufs) for l in range(t.shape[0])]

    def body(*refs):
        b_refs = refs[n:2 * n]
        send_sems, recv_sems = refs[2 * n:]
        x, y, c = _position()

        def copy(k, rows):
            a, l = pieces[k]
            half = b_refs[a].at[l, c, rows]
            return pltpu.make_async_remote_copy(
                src_ref=half, dst_ref=half, send_sem=send_sems.at[k], recv_sem=recv_sems.at[k],
                device_id=(x, y, 1 - c), device_id_type=MESH)

        chunks = [_row_chunks(bufs[a].shape[2:], bufs[a].dtype) for a, _ in pieces]
        whole = [pl.ds(0, bufs[a].shape[2]) for a, _ in pieces]
        for k in range(len(pieces)):
            for rows in chunks[k]:
                copy(k, rows).start()
        for k in range(len(pieces)):
            copy(k, whole[k]).wait_recv()
        for k in range(len(pieces)):
            copy(k, whole[k]).wait_send()

    return pl.pallas_call(
        body, name=name,
        out_shape=[jax.ShapeDtypeStruct(t.shape, t.dtype) for t in bufs],
        in_specs=[ANY] * n, out_specs=[ANY] * n,
        input_output_aliases={a: a for a in range(n)},
        scratch_shapes=[pltpu.SemaphoreType.DMA((len(pieces),)), pltpu.SemaphoreType.DMA((len(pieces),))],
    )(*bufs)


def _pack_rows(parts):
    flat = jnp.concatenate([t.reshape(-1).astype(F32) for t in parts])
    pad = (-flat.shape[0]) % (256 * LANE)
    return jnp.pad(flat, (0, pad)).reshape(-1, LANE)


def _my_half(w2d, ci):
    half = w2d.shape[0] // 2
    return lax.dynamic_slice_in_dim(w2d, ci * half, half, axis=0).astype(BF16)


def _col_view(g):
    _, half, Cs = g.shape
    return g.reshape(4, 2 * half, Cs)


def _row_view(g):
    _, half, C = g.shape
    return g.reshape(8 * half, C)


def _rope_tables(S):
    pos = jnp.arange(S, dtype=F32)
    inv_freq = ROPE_THETA ** (-jnp.arange(0, ROPE, 2, dtype=F32) / ROPE)
    ang = pos[:, None] * inv_freq[None, :]
    cos, sin = jnp.cos(ang), jnp.sin(ang)
    cos_t = jnp.concatenate([cos, cos, cos, cos], axis=1)
    sin_t = jnp.concatenate([-sin, sin, -sin, sin], axis=1)
    return cos_t, sin_t


def kernel(x, c, ada_w, ada_b, pre_g, post_g, sgu_w_in, sgu_norm_g, sgu_w_s, sgu_b_s, sgu_w_out, mla_w_in, mla_q_norm_g, mla_kv_norm_g, mla_w_uq, mla_w_ukv, mla_w_out, loss_target, m_ada_w, m_ada_b, m_pre_g, m_post_g, m_sgu_w_in, m_sgu_norm_g, m_sgu_w_s, m_sgu_b_s, m_sgu_w_out, m_mla_w_in, m_mla_q_norm_g, m_mla_kv_norm_g, m_mla_w_uq, m_mla_w_ukv, m_mla_w_out, v_ada_w, v_ada_b, v_pre_g, v_post_g, v_sgu_w_in, v_sgu_norm_g, v_sgu_w_s, v_sgu_b_s, v_sgu_w_out, v_mla_w_in, v_mla_q_norm_g, v_mla_kv_norm_g, v_mla_w_uq, v_mla_w_ukv, v_mla_w_out):
    S, D = x.shape[1], x.shape[2]
    depth = ada_w.shape[0]
    E = sgu_w_out.shape[1] * 4
    xi, yi, ci = _position()
    chip = 2 * xi + yi
    dev = 4 * xi + 2 * yi + ci
    x0 = x.reshape(S, D)
    target = loss_target.reshape(S, D)

    small = _pack_rows([c, mla_q_norm_g, mla_kv_norm_g])
    mixer_w = dict(sin=sgu_w_in, sout=sgu_w_out, min=mla_w_in, uq=mla_w_uq, ukv=mla_w_ukv, mout=mla_w_out)
    small_g, first_g = _all_gather8([small, _my_half(sgu_w_in[0], ci)], "gather_first")
    small_all = small_g.reshape(8, -1)
    gathered_w = {("sin", 0): first_g}
    qn_w, kvn_w = mla_q_norm_g.shape[1], mla_kv_norm_g.shape[1]
    c_all = small_all[:, :D]
    qn_all = small_all[0::2, D:D + 2 * qn_w].reshape(4, 2, qn_w)
    kvn_all = small_all[0::2, D + 2 * qn_w:D + 2 * qn_w + 2 * kvn_w].reshape(4, 2, kvn_w)
    q_gain = jnp.pad(jnp.transpose(qn_all, (1, 0, 2)).reshape(2, 1, Q_RANK), ((0, 0), (0, 0), (0, Q_RANK_PAD - Q_RANK)))
    kv_gain = jnp.transpose(kvn_all, (1, 0, 2)).reshape(2, 1, KV_RANK)

    views = {}

    def weight(t, j):
        if (t, j) not in views:
            g = gathered_w[(t, j)]
            v = _row_view(g) if t in ("sout", "mout") else _col_view(g)
            if t == "uq":
                v = jnp.pad(v, ((0, 0), (0, Q_RANK_PAD - Q_RANK), (0, 0)))
            views[(t, j)] = v
        return views[(t, j)]

    cols = ada_w.shape[2]
    ada_b_cols = lax.dynamic_slice_in_dim(ada_b, chip * cols, cols, axis=1)
    c_pad = jnp.pad(c_all, ((0, 8), (0, 0)))
    mod_cols = _ada_mod(c_pad, ada_w, ada_b_cols, "ada_mod")[:, :8]
    mod_g, = _all_gather8([mod_cols.reshape(depth * 8, cols)], "gather_mod")
    mod_all = jnp.transpose(mod_g[0::2].reshape(4, depth, 8, cols), (1, 2, 0, 3)).reshape(depth, 8, 4 * cols)

    groups = [("sout0", [("sout", 0)]), ("mla0", [(t, 0) for t in ("min", "uq", "ukv", "mout")]),
              ("sgu1", [("sin", 1), ("sout", 1)]), ("mla1", [(t, 1) for t in ("min", "uq", "ukv", "mout")])]
    sends = {}
    behind = mod_g
    for gname, items in groups:
        blocks = [_my_half(mixer_w[t][j], ci) for t, j in items]
        sends[gname] = _send_start(blocks, [_own_slot(b, dev) for b in blocks], behind, True, f"send_{gname}")
        behind = sends[gname]["token"]

    def arrive(gname, after):
        lands = _send_wait(sends[gname], after, f"arrive_{gname}")
        gathered_w.update(zip(dict(groups)[gname], lands))
    mod = lax.dynamic_index_in_dim(mod_all, dev, 1, keepdims=False)
    shift = [mod[i:i + 1, :D] for i in range(depth)]
    scale = [mod[i:i + 1, D:2 * D] for i in range(depth)]
    gate = [mod[i:i + 1, 2 * D:] for i in range(depth)]

    cos_t, sin_t = _rope_tables(S)
    b_bc = jnp.broadcast_to(sgu_b_s[:, :, :, None], sgu_b_s.shape + (LANE,))
    w_sT = jnp.swapaxes(sgu_w_s, 2, 3)

    saved = []
    xs = x0
    for i in range(depth):
        j = i // 2
        tag = f"l{i}"
        h = _pre_fwd(xs, pre_g[i:i + 1], scale[i], shift[i], f"pre_fwd_{tag}", after=(behind,) if i == 0 else ())
        if i % 2 == 0:
            if j > 0:
                arrive(f"sgu{j}", h)
            uvz = _mm(h, weight("sin", j), b_sharded=True, name=f"sgu_in_{tag}")
            y = _sgu_gate_fwd(uvz, sgu_norm_g[j:j + 1], sgu_w_s[j], b_bc[j], f"sgu_gate_fwd_{tag}")
            if j == 0:
                arrive("sout0", y)
            out = _mm(y, weight("sout", j), name=f"sgu_out_{tag}")
            saved.append(dict(x=xs, h=h, uvz=uvz, y=y, out=out))
        else:
            arrive(f"mla{j}", h)
            p = _mm(h, weight("min", j), b_sharded=True, name=f"mla_in_{tag}")
            cqn, ckvn = _mla_mid_fwd(p, q_gain[j], kv_gain[j], f"mla_mid_fwd_{tag}")
            q = _mm(cqn, weight("uq", j), b_sharded=True, name=f"mla_uq_{tag}")
            kv = _mm(ckvn, weight("ukv", j), b_sharded=True, name=f"mla_ukv_{tag}")
            Q, K, V = _mla_pack(q, kv, p, cos_t, sin_t, f"mla_pack_{tag}")
            T = min(512, S)
            QT = _tile_transpose(Q, T, f"q_tiles_{tag}")
            KT = _tile_transpose(K, T, f"k_tiles_{tag}")
            VT = _tile_transpose(V, T, f"v_tiles_{tag}")
            o, lse = _attn_fwd_t(QT, K, VT, f"attn_fwd_{tag}")
            y = _mla_gate_fwd(o, p, f"mla_gate_fwd_{tag}")
            out = _mm(y, weight("mout", j), name=f"mla_out_{tag}")
            saved.append(dict(x=xs, h=h, p=p, cqn=cqn, ckvn=ckvn, Q=Q, K=K, V=V, QT=QT, KT=KT, o=o, lse=lse, y=y,
                              out=out))
        xs = _post_fwd(xs, out, gate[i], post_g[i:i + 1], f"post_fwd_{tag}")

    dx, loss_part = _loss_grad(xs, target, "loss")
    loss = lax.psum(loss_part[0, 0], ("x", "y", "c"))

    dmod = [None] * depth
    d_pre_g = [None] * depth
    d_post_g = [None] * depth
    d_sgu = [None] * 2
    d_mla = [None] * 2
    kinds = ("sgu_w_in", "sgu_w_out", "mla_w_in", "mla_w_uq", "mla_w_ukv", "mla_w_out")
    halves = dict.fromkeys(kinds)
    in_flight = []

    def send_grads(items, label):
        slices = [dw.reshape(8, -1, dw.shape[-1]) for _, _, dw in items]
        lands = [_own_slot(lax.dynamic_index_in_dim(s, dev, 0, keepdims=False), dev) for s in slices]
        started = _send_start(slices, lands, None, False, f"send_{label}")
        in_flight.append((started, [(kind, layer) for kind, layer, _ in items], label))
        return (started["token"],)

    def collect(count, after):
        for _ in range(count):
            started, keys, label = in_flight.pop(0)
            lands = _send_wait(started, after, f"arrive_{label}")
            for (kind, layer), r in zip(keys, lands):
                halves[kind] = _sum_into_half(r, halves[kind], layer, ci, 2, f"sum_{kind}_{layer}")

    for i in reversed(range(depth)):
        j = i // 2
        tag = f"l{i}"
        sv = saved[i]
        older = len(in_flight)
        dy, dgate, d_post_g[i] = _post_bwd(dx, sv["out"], gate[i], post_g[i:i + 1], f"post_bwd_{tag}")
        if i % 2 == 0:
            dw_out = _mm(sv["y"], dy, ta=True, out_dtype=BF16, name=f"sgu_out_dw_{tag}")
            sent = send_grads([("sgu_w_out", j, dw_out)], f"{tag}_out")
            dyv = _mm(dy, weight("sout", j), tb=True, after=sent, name=f"sgu_out_dx_{tag}")
            duvz, dng, dws, dbs = _sgu_gate_bwd(sv["uvz"], dyv, sgu_norm_g[j:j + 1], sgu_w_s[j], w_sT[j], b_bc[j],
                                                f"sgu_gate_bwd_{tag}")
            sent = ()
            if i > 0:
                dw_in = _mm(sv["h"], duvz, ta=True, out_sharded=True, out_dtype=BF16, name=f"sgu_in_dw_{tag}")
                sent = send_grads([("sgu_w_in", j, dw_in)], f"{tag}_in")
            dh = _mm(duvz, weight("sin", j), tb=True, b_sharded=True, after=sent, name=f"sgu_in_dx_{tag}")
            d_sgu[j] = dict(norm_g=dng, w_s=dws, b_s=dbs[:, :, 0], duvz=duvz)
        else:
            dw_out = _mm(sv["y"], dy, ta=True, out_dtype=BF16, name=f"mla_out_dw_{tag}")
            sent = send_grads([("mla_w_out", j, dw_out)], f"{tag}_out")
            dyv = _mm(dy, weight("mout", j), tb=True, after=sent, name=f"mla_out_dx_{tag}")
            p = sv["p"]
            do = _mla_gate_bwd(dyv, p, f"mla_gate_bwd_{tag}")
            dQT, dK, dV = _attn_bwd_t(sv["QT"], sv["Q"], sv["K"], sv["KT"], sv["V"], sv["o"], do, sv["lse"],
                                      f"attn_bwd_{tag}")
            dQ = _tile_untranspose(dQT, f"dq_tiles_{tag}")
            dq, dkv, dkr = _mla_unpack(dQ, dK, dV, cos_t, sin_t, f"mla_unpack_{tag}")
            dw_uq = _mm(sv["cqn"], dq, ta=True, out_sharded=True, out_dtype=BF16, name=f"mla_uq_dw_{tag}")
            dcqn = _mm(dq, weight("uq", j), tb=True, b_sharded=True, name=f"mla_uq_dx_{tag}")
            dw_ukv = _mm(sv["ckvn"], dkv, ta=True, out_sharded=True, out_dtype=BF16, name=f"mla_ukv_dw_{tag}")
            dckvn = _mm(dkv, weight("ukv", j), tb=True, b_sharded=True, name=f"mla_ukv_dx_{tag}")
            dp, dqg, dkvg = _mla_mid_bwd(p, dcqn, dckvn, dkr, dyv, sv["o"], q_gain[j], kv_gain[j], f"mla_mid_bwd_{tag}")
            dw_in = _mm(sv["h"], dp, ta=True, out_sharded=True, out_dtype=BF16, name=f"mla_in_dw_{tag}")
            sent = send_grads([("mla_w_in", j, dw_in), ("mla_w_uq", j, dw_uq[:, :Q_RANK]), ("mla_w_ukv", j, dw_ukv)],
                              f"{tag}_in")
            dh = _mm(dp, weight("min", j), tb=True, b_sharded=True, after=sent, name=f"mla_in_dx_{tag}")
            d_mla[j] = dict(qg=dqg[0, :Q_RANK], kvg=dkvg[0])
        dx, dshift, dscale, d_pre_g[i] = _pre_bwd(dh, sv["x"], dx, pre_g[i:i + 1], scale[i], f"pre_bwd_{tag}")
        dmod[i] = jnp.concatenate([dshift, dscale, dgate], axis=1)
        collect(older, dx)
    grad_x = dx.reshape(x.shape)

    parts = [jnp.concatenate(dmod, axis=0), jnp.concatenate(d_pre_g, axis=0), jnp.concatenate(d_post_g, axis=0),
             jnp.stack([d["norm_g"][0] for d in d_sgu]), jnp.stack([d["w_s"] for d in d_sgu]),
             jnp.stack([d["b_s"] for d in d_sgu]), jnp.stack([d["qg"] for d in d_mla]),
             jnp.stack([d["kvg"] for d in d_mla])]
    sizes = [int(np.prod(t.shape)) for t in parts]
    packed = _pack_rows(parts)
    packed_all, dmod_all = _all_gather8([packed, parts[0]], "gather_small_grads")
    total = _sum_slots(packed_all, "sum_small_grads").reshape(-1)
    offs = np.concatenate([[0], np.cumsum(sizes)])
    pieces = [total[int(offs[t]):int(offs[t + 1])].reshape(parts[t].shape) for t in range(len(parts))]
    g_ada_b, g_pre_g, g_post_g, g_norm_g, g_w_s, g_b_s, g_qg_full, g_kvg_full = pieces
    g_qg = lax.dynamic_slice_in_dim(g_qg_full, chip * qn_w, qn_w, axis=1)
    g_kvg = lax.dynamic_slice_in_dim(g_kvg_full, chip * kvn_w, kvn_w, axis=1)
    dmod_cols = jnp.stack([lax.dynamic_slice_in_dim(dmod_all[:, i], chip * cols, cols, axis=1) for i in range(depth)])
    dmod_cols = jnp.pad(dmod_cols, ((0, 0), (0, LANE - 8), (0, 0)))

    dw_in0 = _mm(saved[0]["h"], d_sgu[0]["duvz"], ta=True, out_sharded=True, out_dtype=BF16, after=(packed_all,),
                 name="sgu_in_dw_l0")
    sent = send_grads([("sgu_w_in", 0, dw_in0)], "l0_in")
    g_ada_w = _ada_grad(jnp.pad(c_all.T, ((0, 0), (0, LANE - 8))), dmod_cols, "ada_grad", after=sent)

    wnames = ["ada_w", "ada_b", "pre_g", "post_g", "sgu_w_in", "sgu_norm_g", "sgu_w_s", "sgu_b_s", "sgu_w_out",
              "mla_w_in", "mla_q_norm_g", "mla_kv_norm_g", "mla_w_uq", "mla_w_ukv", "mla_w_out"]
    weights = dict(zip(wnames, [ada_w, ada_b, pre_g, post_g, sgu_w_in, sgu_norm_g, sgu_w_s, sgu_b_s, sgu_w_out,
                                mla_w_in, mla_q_norm_g, mla_kv_norm_g, mla_w_uq, mla_w_ukv, mla_w_out]))
    ms = dict(zip(wnames, [m_ada_w, m_ada_b, m_pre_g, m_post_g, m_sgu_w_in, m_sgu_norm_g, m_sgu_w_s, m_sgu_b_s,
                           m_sgu_w_out, m_mla_w_in, m_mla_q_norm_g, m_mla_kv_norm_g, m_mla_w_uq, m_mla_w_ukv,
                           m_mla_w_out]))
    vs = dict(zip(wnames, [v_ada_w, v_ada_b, v_pre_g, v_post_g, v_sgu_w_in, v_sgu_norm_g, v_sgu_w_s, v_sgu_b_s,
                           v_sgu_w_out, v_mla_w_in, v_mla_q_norm_g, v_mla_kv_norm_g, v_mla_w_uq, v_mla_w_ukv,
                           v_mla_w_out]))
    grads = dict(ada_w=g_ada_w, ada_b=g_ada_b, pre_g=g_pre_g, post_g=g_post_g, sgu_norm_g=g_norm_g, sgu_w_s=g_w_s,
                 sgu_b_s=g_b_s, mla_q_norm_g=g_qg, mla_kv_norm_g=g_kvg)
    stepped = {}

    def step(nm):
        grads[nm] = grads[nm].reshape(weights[nm].shape)
        stepped[nm] = _adamw(weights[nm], grads[nm], ms[nm], vs[nm], f"adamw_{nm}")

    for nm in wnames:
        if nm in grads:
            step(nm)
    early = [kind for kind in kinds if kind != "sgu_w_in"]
    collect(len(in_flight) - 1, stepped["ada_w"][0])
    for kind, g in zip(early, _pair_swap([halves[kind] for kind in early], "swap_grads")):
        grads[kind] = g
        step(kind)
    collect(len(in_flight), stepped[early[-1]][0])
    grads["sgu_w_in"], = _pair_swap([halves["sgu_w_in"]], "swap_grads_last")
    step("sgu_w_in")
    return (loss, grad_x, *[grads[nm] for nm in wnames], *[stepped[nm][0] for nm in wnames],
            *[stepped[nm][1] for nm in wnames], *[stepped[nm][2] for nm in wnames])
```

```python
import math

import jax
import jax.numpy as jnp
import numpy as np
from jax import lax
from jax.experimental import pallas as pl
from jax.experimental.pallas import tpu as pltpu

F32 = jnp.float32
BF16 = jnp.bfloat16
MESH = pl.DeviceIdType.MESH

NORM_EPS = 1e-6
CHUNK = 64
SGU_BLOCK = 128
SGU_GROUPS = 16
HEADS = 16
NOPE = 128
ROPE = 64
VDIM = 128
QK_PAD = 256
Q_RANK = 448
Q_RANK_PAD = 512
KV_RANK = 512
ROPE_THETA = 10000.0
ATTN_SCALE = (NOPE + ROPE) ** -0.5
LOG2E = 1.4426950408889634
Q_FOLD = ATTN_SCALE * LOG2E

ADAM_LR = 0.001
ADAM_B1 = 0.9
ADAM_B2 = 0.999
ADAM_EPS = 1e-08
ADAM_WD = 0.01
ADAM_STEP = 10

LANE = 128
VMEM_LIMIT = 48 * 1024 * 1024

NN = (((1,), (0,)), ((), ()))
NT = (((1,), (1,)), ((), ()))
TN = (((0,), (0,)), ((), ()))


def _params(*sem):
    return pltpu.CompilerParams(dimension_semantics=sem, vmem_limit_bytes=VMEM_LIMIT)


def _row_tile(rows, row_bytes, target_bytes=1 << 20):
    if rows * row_bytes <= target_bytes or rows % 16:
        return rows
    best = 16
    t = 16
    while t <= rows:
        if rows % t == 0 and t * row_bytes <= target_bytes:
            best = t
        t += 16
    return best


def _fit(dim, target):
    if dim <= target:
        return dim
    t = (target // LANE) * LANE
    while t > LANE and dim % t:
        t -= LANE
    return t


def _gelu(x):
    return 0.5 * x * (1.0 + lax.erf(x * 0.7071067811865476))


def _gelu_grad(x):
    return 0.5 * (1.0 + lax.erf(x * 0.7071067811865476)) + x * jnp.exp(-0.5 * x * x) * 0.3989422804014327


def _mm(a, b, *, ta=False, tb=False, b_sharded=False, out_sharded=False, out_dtype=F32,
        tm=1024, tn=1024, tk=2048, after=(), name):
    if ta:
        K, M = a.shape
    else:
        M, K = a.shape
    if b_sharded:
        shards, rows, Cs = b.shape
        b_shape = (rows, shards * Cs)
    else:
        b_shape = b.shape
    if tb:
        N, K2 = b_shape
    else:
        K2, N = b_shape
    assert K == K2, (a.shape, b.shape, ta, tb)
    n_lim = Cs if (b_sharded and not tb) else (N // 4 if out_sharded else N)
    k_lim = Cs if (b_sharded and tb) else K
    tm, tn, tk = _fit(M, tm), _fit(n_lim, tn), _fit(k_lim, tk)
    assert M % tm == 0 and n_lim % tn == 0 and k_lim % tk == 0, (M, N, K, tm, tn, tk)
    nk = K // tk
    nb_n = n_lim // tn
    nb_k = k_lim // tk
    dims = (((0 if ta else 1,), (1 if tb else 0,)), ((), ()))

    def body(a_ref, b_ref, *rest):
        o_ref, *scratch = rest[len(after):]
        prod = lax.dot_general(a_ref[...].astype(BF16), b_ref[...].astype(BF16), dims,
                               preferred_element_type=F32)
        if nk == 1:
            o_ref[...] = prod.astype(out_dtype)
        else:
            acc_ref, = scratch
            k = pl.program_id(2)

            @pl.when(k == 0)
            def _():
                acc_ref[...] = prod

            @pl.when(k > 0)
            def _():
                acc_ref[...] += prod

            @pl.when(k == nk - 1)
            def _():
                o_ref[...] = acc_ref[...].astype(out_dtype)

    a_spec = (pl.BlockSpec((tk, tm), lambda i, j, k: (k, i)) if ta
              else pl.BlockSpec((tm, tk), lambda i, j, k: (i, k)))
    if b_sharded and tb:
        b_spec = pl.BlockSpec((None, tn, tk), lambda i, j, k: (k // nb_k, j, k % nb_k))
    elif b_sharded:
        b_spec = pl.BlockSpec((None, tk, tn), lambda i, j, k: (j // nb_n, k, j % nb_n))
    elif tb:
        b_spec = pl.BlockSpec((tn, tk), lambda i, j, k: (j, k))
    else:
        b_spec = pl.BlockSpec((tk, tn), lambda i, j, k: (k, j))
    if out_sharded:
        out_shape = jax.ShapeDtypeStruct((4, M, N // 4), out_dtype)
        out_spec = pl.BlockSpec((None, tm, tn), lambda i, j, k: (j // nb_n, i, j % nb_n))
    else:
        out_shape = jax.ShapeDtypeStruct((M, N), out_dtype)
        out_spec = pl.BlockSpec((tm, tn), lambda i, j, k: (i, j))
    return pl.pallas_call(
        body, name=name,
        out_shape=out_shape,
        grid=(M // tm, N // tn, nk),
        in_specs=[a_spec, b_spec] + [pl.BlockSpec(memory_space=pl.ANY)] * len(after),
        out_specs=out_spec,
        scratch_shapes=[] if nk == 1 else [pltpu.VMEM((tm, tn), F32)],
        compiler_params=_params("parallel", "parallel", "arbitrary"),
    )(a, b, *after)


def _split_bf16(v):
    hi = v.astype(BF16)
    lo = (v - hi.astype(F32)).astype(BF16)
    return hi, lo


def _dot3(a, b, dims):
    a_hi, a_lo = _split_bf16(a)
    b_hi, b_lo = _split_bf16(b)
    out = lax.dot_general(a_hi, b_hi, dims, preferred_element_type=F32)
    out += lax.dot_general(a_lo, b_hi, dims, preferred_element_type=F32)
    out += lax.dot_general(a_hi, b_lo, dims, preferred_element_type=F32)
    return out


def _ada_mod(c_all, ada_w, ada_b_cols, name):
    L, D, cols = ada_w.shape
    B = c_all.shape[0]
    tn = 512 if cols % 512 == 0 else cols

    def body(c_ref, w_ref, b_ref, o_ref):
        cv = c_ref[...]
        cond = cv * jax.nn.sigmoid(cv)
        o_ref[...] = _dot3(cond, w_ref[...], NN) + b_ref[...]

    return pl.pallas_call(
        body, name=name,
        out_shape=jax.ShapeDtypeStruct((L, B, cols), F32),
        grid=(L, cols // tn),
        in_specs=[pl.BlockSpec((B, D), lambda l, j: (0, 0)),
                  pl.BlockSpec((None, D, tn), lambda l, j: (l, 0, j)),
                  pl.BlockSpec((None, 1, tn), lambda l, j: (l, 0, j))],
        out_specs=pl.BlockSpec((None, B, tn), lambda l, j: (l, 0, j)),
        compiler_params=_params("parallel", "parallel"),
    )(c_all, ada_w, ada_b_cols.reshape(L, 1, cols))


def _ada_grad(c_t, dmod_cols, name, after=()):
    L, B, cols = dmod_cols.shape
    D = c_t.shape[0]
    tn = 512 if cols % 512 == 0 else cols

    def body(c_ref, d_ref, *rest):
        o_ref = rest[-1]
        cv = c_ref[...]
        cond = cv * jax.nn.sigmoid(cv)
        o_ref[...] = _dot3(cond, d_ref[...], NN)

    return pl.pallas_call(
        body, name=name,
        out_shape=jax.ShapeDtypeStruct((L, D, cols), F32),
        grid=(L, cols // tn),
        in_specs=[pl.BlockSpec((D, B), lambda l, j: (0, 0)),
                  pl.BlockSpec((None, B, tn), lambda l, j: (l, 0, j))] + [pl.BlockSpec(memory_space=pl.ANY)] * len(after),
        out_specs=pl.BlockSpec((None, D, tn), lambda l, j: (l, 0, j)),
        compiler_params=_params("parallel", "parallel"),
    )(c_t, dmod_cols, *after)


def _row_spec(ts, width):
    return pl.BlockSpec((ts, width), lambda i: (i, 0))


def _vec_spec(width):
    return pl.BlockSpec((1, width), lambda i: (0, 0))


def _pre_fwd(x, pre_g, scale, shift, name, after=()):
    S, D = x.shape
    ts = min(256, S)

    def body(x_ref, g_ref, sc_ref, sh_ref, *rest):
        h_ref = rest[-1]
        xv = x_ref[...]
        r = lax.rsqrt(jnp.mean(xv * xv, axis=-1, keepdims=True) + NORM_EPS)
        h_ref[...] = ((xv * r * g_ref[...]) * (1.0 + sc_ref[...]) + sh_ref[...]).astype(BF16)

    return pl.pallas_call(
        body, name=name, out_shape=jax.ShapeDtypeStruct((S, D), BF16), grid=(S // ts,),
        in_specs=[_row_spec(ts, D), _vec_spec(D), _vec_spec(D), _vec_spec(D)]
        + [pl.BlockSpec(memory_space=pl.ANY)] * len(after),
        out_specs=_row_spec(ts, D), compiler_params=_params("parallel"),
    )(x, pre_g, scale, shift, *after)


def _pre_bwd(dh, x, dx_res, pre_g, scale, name):
    S, D = x.shape
    ts = min(256, S)

    def body(dh_ref, x_ref, dr_ref, g_ref, sc_ref, dx_ref, dsh_ref, dsc_ref, dg_ref):
        @pl.when(pl.program_id(0) == 0)
        def _():
            dsh_ref[...] = jnp.zeros_like(dsh_ref)
            dsc_ref[...] = jnp.zeros_like(dsc_ref)
            dg_ref[...] = jnp.zeros_like(dg_ref)

        dh = dh_ref[...]
        xv = x_ref[...]
        g = g_ref[...]
        one_sc = 1.0 + sc_ref[...]
        r = lax.rsqrt(jnp.mean(xv * xv, axis=-1, keepdims=True) + NORM_EPS)
        xn = xv * r
        dsh_ref[...] += jnp.sum(dh, axis=0, keepdims=True)
        dsc_ref[...] += jnp.sum(dh * (xn * g), axis=0, keepdims=True)
        dg_ref[...] += jnp.sum(dh * one_sc * xn, axis=0, keepdims=True)
        dxn = dh * one_sc * g
        dx_ref[...] = dr_ref[...] + r * (dxn - xn * jnp.mean(dxn * xn, axis=-1, keepdims=True))

    vec = jax.ShapeDtypeStruct((1, D), F32)
    return pl.pallas_call(
        body, name=name, out_shape=(jax.ShapeDtypeStruct((S, D), F32), vec, vec, vec), grid=(S // ts,),
        in_specs=[_row_spec(ts, D), _row_spec(ts, D), _row_spec(ts, D), _vec_spec(D), _vec_spec(D)],
        out_specs=(_row_spec(ts, D), _vec_spec(D), _vec_spec(D), _vec_spec(D)),
        compiler_params=_params("arbitrary"),
    )(dh, x, dx_res, pre_g, scale)


def _post_fwd(x, y, gate, post_g, name):
    S, D = x.shape
    ts = min(256, S)

    def body(x_ref, y_ref, gt_ref, g_ref, o_ref):
        yv = y_ref[...]
        r = lax.rsqrt(jnp.mean(yv * yv, axis=-1, keepdims=True) + NORM_EPS)
        o_ref[...] = x_ref[...] + gt_ref[...] * (yv * r * g_ref[...])

    return pl.pallas_call(
        body, name=name, out_shape=jax.ShapeDtypeStruct((S, D), F32), grid=(S // ts,),
        in_specs=[_row_spec(ts, D), _row_spec(ts, D), _vec_spec(D), _vec_spec(D)],
        out_specs=_row_spec(ts, D), compiler_params=_params("parallel"),
    )(x, y, gate, post_g)


def _post_bwd(dx, y, gate, post_g, name):
    S, D = y.shape
    ts = min(256, S)

    def body(dx_ref, y_ref, gt_ref, g_ref, dy_ref, dgt_ref, dg_ref):
        @pl.when(pl.program_id(0) == 0)
        def _():
            dgt_ref[...] = jnp.zeros_like(dgt_ref)
            dg_ref[...] = jnp.zeros_like(dg_ref)

        dxv = dx_ref[...]
        yv = y_ref[...]
        g = g_ref[...]
        gt = gt_ref[...]
        r = lax.rsqrt(jnp.mean(yv * yv, axis=-1, keepdims=True) + NORM_EPS)
        yn = yv * r
        dgt_ref[...] += jnp.sum(dxv * (yn * g), axis=0, keepdims=True)
        dg_ref[...] += jnp.sum(dxv * gt * yn, axis=0, keepdims=True)
        dyn = dxv * gt * g
        dy_ref[...] = (r * (dyn - yn * jnp.mean(dyn * yn, axis=-1, keepdims=True))).astype(BF16)

    vec = jax.ShapeDtypeStruct((1, D), F32)
    return pl.pallas_call(
        body, name=name, out_shape=(jax.ShapeDtypeStruct((S, D), BF16), vec, vec), grid=(S // ts,),
        in_specs=[_row_spec(ts, D), _row_spec(ts, D), _vec_spec(D), _vec_spec(D)],
        out_specs=(_row_spec(ts, D), _vec_spec(D), _vec_spec(D)),
        compiler_params=_params("arbitrary"),
    )(dx, y, gate, post_g)


def _loss_grad(xf, target, name):
    S, D = xf.shape
    ts = min(256, S)

    def body(x_ref, t_ref, dx_ref, l_ref):
        @pl.when(pl.program_id(0) == 0)
        def _():
            l_ref[...] = jnp.zeros_like(l_ref)

        e = x_ref[...] - t_ref[...]
        dx_ref[...] = e * (1.0 / D)
        row = jnp.sum(e * e, axis=1, keepdims=True) * (1.0 / D)
        l_ref[...] += 0.5 * jnp.sum(row, axis=0, keepdims=True)

    return pl.pallas_call(
        body, name=name,
        out_shape=(jax.ShapeDtypeStruct((S, D), F32), jax.ShapeDtypeStruct((1, 1), F32)), grid=(S // ts,),
        in_specs=[_row_spec(ts, D), _row_spec(ts, D)],
        out_specs=(_row_spec(ts, D), pl.BlockSpec((1, 1), lambda i: (0, 0))),
        compiler_params=_params("arbitrary"),
    )(xf, target)


def _chunk_mask(transposed=False):
    row = lax.broadcasted_iota(jnp.int32, (SGU_BLOCK, SGU_BLOCK), 0) // CHUNK
    col = lax.broadcasted_iota(jnp.int32, (SGU_BLOCK, SGU_BLOCK), 1) // CHUNK
    return (row <= col) if transposed else (col <= row)


def _sgu_gate_fwd(uvz, norm_g, w_s, b_bc, name):
    S, E3 = uvz.shape
    E = E3 // 3
    T = SGU_BLOCK
    gd = E // SGU_GROUPS

    def body(uvz_ref, ng_ref, ws_ref, bb_ref, y_ref, v_scr):
        gv = _gelu(uvz_ref[:, E:2 * E])
        mu = jnp.mean(gv, axis=-1, keepdims=True)
        xc = gv - mu
        rstd = lax.rsqrt(jnp.mean(xc * xc, axis=-1, keepdims=True) + NORM_EPS)
        v_scr[...] = (xc * rstd * ng_ref[...]).astype(BF16)
        mask = _chunk_mask()
        for g in range(SGU_GROUPS):
            sl = slice(g * gd, (g + 1) * gd)
            wg = jnp.where(mask, ws_ref[g], 0.0).astype(BF16)
            vm = lax.dot_general(wg, v_scr[:, sl], NN, preferred_element_type=F32)
            vm = vm + jnp.tile(bb_ref[g], (1, gd // LANE))
            z = uvz_ref[:, 2 * E + g * gd:2 * E + (g + 1) * gd]
            y_ref[:, sl] = (_gelu(uvz_ref[:, sl]) * vm * (z * jax.nn.sigmoid(z))).astype(BF16)

    return pl.pallas_call(
        body, name=name, out_shape=jax.ShapeDtypeStruct((S, E), BF16), grid=(S // T,),
        in_specs=[_row_spec(T, E3), _vec_spec(E),
                  pl.BlockSpec((SGU_GROUPS, T, T), lambda i: (0, 0, 0)),
                  pl.BlockSpec((SGU_GROUPS, T, LANE), lambda i: (0, 0, 0))],
        out_specs=_row_spec(T, E),
        scratch_shapes=[pltpu.VMEM((T, E), BF16)],
        compiler_params=_params("parallel"),
    )(uvz, norm_g, w_s, b_bc)


def _sgu_gate_bwd(uvz, dyv, norm_g, w_s, w_sT, b_bc, name):
    S, E3 = uvz.shape
    E = E3 // 3
    T = SGU_BLOCK
    gd = E // SGU_GROUPS

    def body(uvz_ref, dyv_ref, ng_ref, ws_ref, wst_ref, bb_ref,
             d_ref, dng_ref, dws_ref, dbs_ref, vhat_scr, dv_scr):
        @pl.when(pl.program_id(0) == 0)
        def _():
            dng_ref[...] = jnp.zeros_like(dng_ref)
            dws_ref[...] = jnp.zeros_like(dws_ref)
            dbs_ref[...] = jnp.zeros_like(dbs_ref)

        gv = _gelu(uvz_ref[:, E:2 * E])
        mu = jnp.mean(gv, axis=-1, keepdims=True)
        xc = gv - mu
        rstd = lax.rsqrt(jnp.mean(xc * xc, axis=-1, keepdims=True) + NORM_EPS)
        vhat_scr[...] = xc * rstd
        mask = _chunk_mask()
        mask_t = _chunk_mask(transposed=True)
        for g in range(SGU_GROUPS):
            sl = slice(g * gd, (g + 1) * gd)
            u_pre = uvz_ref[:, sl]
            z = uvz_ref[:, 2 * E + g * gd:2 * E + (g + 1) * gd]
            dy = dyv_ref[:, sl]
            u = _gelu(u_pre)
            sig = jax.nn.sigmoid(z)
            sz = z * sig
            vg = (vhat_scr[:, sl] * ng_ref[:, sl]).astype(BF16)
            wg = jnp.where(mask, ws_ref[g], 0.0).astype(BF16)
            vm = lax.dot_general(wg, vg, NN, preferred_element_type=F32)
            vm = vm + jnp.tile(bb_ref[g], (1, gd // LANE))
            dy_u = dy * u
            d_ref[:, sl] = (dy * vm * sz * _gelu_grad(u_pre)).astype(BF16)
            d_ref[:, 2 * E + g * gd:2 * E + (g + 1) * gd] = (
                dy_u * vm * (sig * (1.0 + z * (1.0 - sig)))).astype(BF16)
            dvm = dy_u * sz
            dvm_b = dvm.astype(BF16)
            dws_ref[g] += jnp.where(mask, lax.dot_general(dvm_b, vg, NT, preferred_element_type=F32), 0.0)
            dbs_ref[g] += jnp.broadcast_to(jnp.sum(dvm, axis=1, keepdims=True), (T, LANE))
            wgt = jnp.where(mask_t, wst_ref[g], 0.0).astype(BF16)
            dv_scr[:, sl] = lax.dot_general(wgt, dvm_b, NN, preferred_element_type=F32)
        dv = dv_scr[...]
        vhat = vhat_scr[...]
        dng_ref[...] += jnp.sum(dv * vhat, axis=0, keepdims=True)
        dvh = dv * ng_ref[...]
        dgv = rstd * (dvh - jnp.mean(dvh, axis=-1, keepdims=True)
                      - vhat * jnp.mean(dvh * vhat, axis=-1, keepdims=True))
        d_ref[:, E:2 * E] = (dgv * _gelu_grad(uvz_ref[:, E:2 * E])).astype(BF16)

    wspec = pl.BlockSpec((SGU_GROUPS, T, T), lambda i: (0, 0, 0))
    bspec = pl.BlockSpec((SGU_GROUPS, T, LANE), lambda i: (0, 0, 0))
    return pl.pallas_call(
        body, name=name,
        out_shape=(jax.ShapeDtypeStruct((S, E3), BF16), jax.ShapeDtypeStruct((1, E), F32),
                   jax.ShapeDtypeStruct((SGU_GROUPS, T, T), F32),
                   jax.ShapeDtypeStruct((SGU_GROUPS, T, LANE), F32)),
        grid=(S // T,),
        in_specs=[_row_spec(T, E3), _row_spec(T, E), _vec_spec(E), wspec, wspec, bspec],
        out_specs=(_row_spec(T, E3), _vec_spec(E), wspec, bspec),
        scratch_shapes=[pltpu.VMEM((T, E), F32), pltpu.VMEM((T, E), F32)],
        compiler_params=_params("arbitrary"),
    )(uvz, dyv, norm_g, w_s, w_sT, b_bc)


MLA_WIDTH = HEADS * VDIM
P_LATENT = Q_RANK + KV_RANK + ROPE
P_WIDTH = P_LATENT + MLA_WIDTH


def _swap_halves(v):
    lane = lax.broadcasted_iota(jnp.int32, v.shape, 1)
    return jnp.where(lane % ROPE < ROPE // 2, pltpu.roll(v, LANE - ROPE // 2, 1), pltpu.roll(v, ROPE // 2, 1))


def _low_lanes(rows):
    return lax.broadcasted_iota(jnp.int32, (rows, LANE), 1) < ROPE


def _latent_tiles(ref):
    return [ref[:, t * LANE:(t + 1) * LANE] for t in range(P_LATENT // LANE)]


def _split_latents(tiles, low):
    cq = jnp.concatenate(tiles[0:3] + [jnp.where(low, tiles[3], 0.0)], axis=1)
    rolled = [pltpu.roll(t, ROPE, 1) for t in tiles[3:8]]
    ckv = jnp.concatenate([jnp.where(low, rolled[t], rolled[t + 1]) for t in range(4)], axis=1)
    kr = jnp.where(low, rolled[4], 0.0)
    return cq, ckv, kr


def _mla_mid_fwd(p, qg, kvg, name):
    S, PW = p.shape
    ts = min(256, S)

    def body(p_ref, qg_ref, kvg_ref, cqn_ref, ckvn_ref):
        cq, ckv, _ = _split_latents(_latent_tiles(p_ref), _low_lanes(ts))
        r = lax.rsqrt(jnp.sum(cq * cq, axis=-1, keepdims=True) * (1.0 / Q_RANK) + NORM_EPS)
        cqn_ref[...] = (cq * r * qg_ref[...]).astype(BF16)
        r2 = lax.rsqrt(jnp.mean(ckv * ckv, axis=-1, keepdims=True) + NORM_EPS)
        ckvn_ref[...] = (ckv * r2 * kvg_ref[...]).astype(BF16)

    return pl.pallas_call(
        body, name=name,
        out_shape=(jax.ShapeDtypeStruct((S, Q_RANK_PAD), BF16), jax.ShapeDtypeStruct((S, KV_RANK), BF16)),
        grid=(S // ts,),
        in_specs=[_row_spec(ts, P_LATENT), _vec_spec(Q_RANK_PAD), _vec_spec(KV_RANK)],
        out_specs=(_row_spec(ts, Q_RANK_PAD), _row_spec(ts, KV_RANK)),
        compiler_params=_params("parallel"),
    )(p, qg, kvg)


def _mla_pack(q, kv, p, cos_t, sin_t, name):
    S = q.shape[0]
    ts = min(256, S)
    pair_w = 2 * (NOPE + ROPE)
    head_w = NOPE + VDIM

    def body(q_ref, kv_ref, kr_ref, cos_ref, sin_ref, qo_ref, ko_ref, vo_ref):
        cosv = cos_ref[...]
        sinv = sin_ref[...]
        low = _low_lanes(ts)
        kr = jnp.where(low, pltpu.roll(kr_ref[...], ROPE, 1), 0.0)
        kr = (kr * cosv + _swap_halves(kr) * sinv).astype(BF16)
        for pair in range(HEADS // 2):
            t0, t1, t2 = (q_ref[:, pair * pair_w + t * LANE:pair * pair_w + (t + 1) * LANE] for t in range(3))
            nope_b = jnp.where(low, pltpu.roll(t1, ROPE, 1), pltpu.roll(t2, ROPE, 1))
            ropes = jnp.where(low, t1, t2)
            roped = (ropes * cosv + _swap_halves(ropes) * sinv) * Q_FOLD
            qo_ref[2 * pair, :, 0:NOPE] = (t0 * Q_FOLD).astype(BF16)
            qo_ref[2 * pair, :, NOPE:QK_PAD] = jnp.where(low, roped, 0.0).astype(BF16)
            qo_ref[2 * pair + 1, :, 0:NOPE] = (nope_b * Q_FOLD).astype(BF16)
            qo_ref[2 * pair + 1, :, NOPE:QK_PAD] = jnp.where(low, pltpu.roll(roped, ROPE, 1), 0.0).astype(BF16)
        for h in range(HEADS):
            ko_ref[h, :, 0:NOPE] = kv_ref[:, h * head_w:h * head_w + NOPE].astype(BF16)
            ko_ref[h, :, NOPE:QK_PAD] = kr
            vo_ref[h] = kv_ref[:, h * head_w + NOPE:(h + 1) * head_w].astype(BF16)

    return pl.pallas_call(
        body, name=name,
        out_shape=(jax.ShapeDtypeStruct((HEADS, S, QK_PAD), BF16), jax.ShapeDtypeStruct((HEADS, S, QK_PAD), BF16),
                   jax.ShapeDtypeStruct((HEADS, S, VDIM), BF16)),
        grid=(S // ts,),
        in_specs=[_row_spec(ts, q.shape[1]), _row_spec(ts, kv.shape[1]),
                  pl.BlockSpec((ts, LANE), lambda i: (i, P_LATENT // LANE - 1)),
                  _row_spec(ts, LANE), _row_spec(ts, LANE)],
        out_specs=(pl.BlockSpec((HEADS, ts, QK_PAD), lambda i: (0, i, 0)),
                   pl.BlockSpec((HEADS, ts, QK_PAD), lambda i: (0, i, 0)),
                   pl.BlockSpec((HEADS, ts, VDIM), lambda i: (0, i, 0))),
        compiler_params=_params("parallel"),
    )(q, kv, p, cos_t, sin_t)


def _mla_unpack(dQ, dK, dV, cos_t, sin_t, name):
    S = dQ.shape[1]
    ts = min(256, S)
    pair_w = 2 * (NOPE + ROPE)
    head_w = NOPE + VDIM

    def body(dq_ref, dk_ref, dv_ref, cos_ref, sin_ref, q_ref, kv_ref, kr_ref):
        cosv = cos_ref[...]
        sinv = sin_ref[...]
        low = _low_lanes(ts)
        for pair in range(HEADS // 2):
            blk = dq_ref[2 * pair, :, NOPE:QK_PAD] + pltpu.roll(dq_ref[2 * pair + 1, :, NOPE:QK_PAD], ROPE, 1)
            ropes = blk * cosv - _swap_halves(blk) * sinv
            nope_b = pltpu.roll(dq_ref[2 * pair + 1, :, 0:NOPE], ROPE, 1)
            q_ref[:, pair * pair_w:pair * pair_w + LANE] = dq_ref[2 * pair, :, 0:NOPE].astype(BF16)
            q_ref[:, pair * pair_w + LANE:pair * pair_w + 2 * LANE] = jnp.where(low, ropes, nope_b).astype(BF16)
            q_ref[:, pair * pair_w + 2 * LANE:(pair + 1) * pair_w] = jnp.where(low, nope_b, ropes).astype(BF16)
        dkr = dk_ref[0, :, NOPE:QK_PAD]
        for h in range(1, HEADS):
            dkr = dkr + dk_ref[h, :, NOPE:QK_PAD]
        kr_ref[...] = dkr * cosv - _swap_halves(dkr) * sinv
        for h in range(HEADS):
            kv_ref[:, h * head_w:h * head_w + NOPE] = dk_ref[h, :, 0:NOPE].astype(BF16)
            kv_ref[:, h * head_w + NOPE:(h + 1) * head_w] = dv_ref[h].astype(BF16)

    return pl.pallas_call(
        body, name=name,
        out_shape=(jax.ShapeDtypeStruct((S, HEADS * (NOPE + ROPE)), BF16),
                   jax.ShapeDtypeStruct((S, HEADS * (NOPE + VDIM)), BF16),
                   jax.ShapeDtypeStruct((S, LANE), F32)),
        grid=(S // ts,),
        in_specs=[pl.BlockSpec((HEADS, ts, QK_PAD), lambda i: (0, i, 0)),
                  pl.BlockSpec((HEADS, ts, QK_PAD), lambda i: (0, i, 0)),
                  pl.BlockSpec((HEADS, ts, VDIM), lambda i: (0, i, 0)),
                  _row_spec(ts, LANE), _row_spec(ts, LANE)],
        out_specs=(_row_spec(ts, HEADS * (NOPE + ROPE)), _row_spec(ts, HEADS * (NOPE + VDIM)),
                   _row_spec(ts, LANE)),
        compiler_params=_params("parallel"),
    )(dQ, dK, dV, cos_t, sin_t)


def _mla_gate_fwd(o, p, name):
    S, W = o.shape
    ts = min(256, S)
    wb = P_LATENT

    def body(o_ref, z_ref, y_ref):
        z = z_ref[...]
        y_ref[...] = (o_ref[...] * (z * jax.nn.sigmoid(z))).astype(BF16)

    return pl.pallas_call(
        body, name=name, out_shape=jax.ShapeDtypeStruct((S, W), BF16), grid=(S // ts, W // wb),
        in_specs=[pl.BlockSpec((ts, wb), lambda i, j: (i, j)), pl.BlockSpec((ts, wb), lambda i, j: (i, j + 1))],
        out_specs=pl.BlockSpec((ts, wb), lambda i, j: (i, j)), compiler_params=_params("parallel", "parallel"),
    )(o, p)


def _mla_gate_bwd(dyv, p, name):
    S, W = dyv.shape
    ts = min(256, S)
    wb = P_LATENT

    def body(d_ref, z_ref, do_ref):
        z = z_ref[...]
        do_ref[...] = d_ref[...] * (z * jax.nn.sigmoid(z))

    return pl.pallas_call(
        body, name=name, out_shape=jax.ShapeDtypeStruct((S, W), F32), grid=(S // ts, W // wb),
        in_specs=[pl.BlockSpec((ts, wb), lambda i, j: (i, j)), pl.BlockSpec((ts, wb), lambda i, j: (i, j + 1))],
        out_specs=pl.BlockSpec((ts, wb), lambda i, j: (i, j)), compiler_params=_params("parallel", "parallel"),
    )(dyv, p)


def _mla_mid_bwd(p, dcqn, dckvn, dkr, dyv, o, qg, kvg, name):
    S, PW = p.shape
    W = o.shape[1]
    ts = min(256, S)
    nt = Q_RANK_PAD // LANE

    def rms_bwd(xv, dy, g, count):
        r = lax.rsqrt(jnp.sum(xv * xv, axis=-1, keepdims=True) * (1.0 / count) + NORM_EPS)
        xn = xv * r
        dg = jnp.sum(dy * xn, axis=0, keepdims=True)
        dxn = dy * g
        dx = r * (dxn - xn * (jnp.sum(dxn * xn, axis=-1, keepdims=True) * (1.0 / count)))
        return dx, dg

    def body(p_ref, dcq_ref, dckv_ref, dkr_ref, dyv_ref, o_ref, qg_ref, kvg_ref, dp_ref, dqg_ref, dkvg_ref):
        @pl.when(pl.program_id(0) == 0)
        def _():
            dqg_ref[...] = jnp.zeros_like(dqg_ref)
            dkvg_ref[...] = jnp.zeros_like(dkvg_ref)

        low = _low_lanes(ts)
        cq, ckv, _ = _split_latents(_latent_tiles(p_ref), low)
        dcq, dg = rms_bwd(cq, dcq_ref[...], qg_ref[...], Q_RANK)
        dqg_ref[...] += dg
        dckv, dg = rms_bwd(ckv, dckv_ref[...], kvg_ref[...], KV_RANK)
        dkvg_ref[...] += dg
        moved = [pltpu.roll(dckv[:, t * LANE:(t + 1) * LANE], ROPE, 1) for t in range(nt)]
        moved.append(pltpu.roll(dkr_ref[...], ROPE, 1))
        for t in range(nt - 1):
            dp_ref[:, t * LANE:(t + 1) * LANE] = dcq[:, t * LANE:(t + 1) * LANE].astype(BF16)
        dp_ref[:, (nt - 1) * LANE:nt * LANE] = jnp.where(low, dcq[:, (nt - 1) * LANE:nt * LANE], moved[0]).astype(BF16)
        for t in range(nt):
            dp_ref[:, (nt + t) * LANE:(nt + t + 1) * LANE] = jnp.where(low, moved[t], moved[t + 1]).astype(BF16)
        z = p_ref[:, P_LATENT:PW]
        sig = jax.nn.sigmoid(z)
        dp_ref[:, P_LATENT:PW] = (dyv_ref[...] * o_ref[...] * (sig * (1.0 + z * (1.0 - sig)))).astype(BF16)

    return pl.pallas_call(
        body, name=name,
        out_shape=(jax.ShapeDtypeStruct((S, PW), BF16), jax.ShapeDtypeStruct((1, Q_RANK_PAD), F32),
                   jax.ShapeDtypeStruct((1, KV_RANK), F32)),
        grid=(S // ts,),
        in_specs=[_row_spec(ts, PW), _row_spec(ts, Q_RANK_PAD), _row_spec(ts, KV_RANK), _row_spec(ts, LANE),
                  _row_spec(ts, W), _row_spec(ts, W), _vec_spec(Q_RANK_PAD), _vec_spec(KV_RANK)],
        out_specs=(_row_spec(ts, PW), _vec_spec(Q_RANK_PAD), _vec_spec(KV_RANK)),
        compiler_params=_params("arbitrary"),
    )(p, dcqn, dckvn, dkr, dyv, o, qg, kvg)


def _tile_mask(T):
    row = lax.broadcasted_iota(jnp.int32, (T, T), 0) // CHUNK
    col = lax.broadcasted_iota(jnp.int32, (T, T), 1) // CHUNK
    return col <= row


def _attn_bwd(Q, K, V, o, do, lse, name):
    H, S, _ = Q.shape
    T = min(512, S)
    nq = S // T

    def body(q_ref, k_ref, v_ref, o_ref, do_ref, lse_ref, dq_ref, dk_ref, dv_ref, dk_scr, dv_scr, s_scr, dp_scr):
        ki = pl.program_id(1)

        @pl.when(ki == 0)
        def _():
            dq_ref[...] = jnp.zeros_like(dq_ref)

        dk_scr[...] = jnp.zeros_like(dk_scr)
        dv_scr[...] = jnp.zeros_like(dv_scr)
        k = k_ref[...]
        v = v_ref[...]

        def scores(i):
            rows = pl.ds(pl.multiple_of(i * T, T), T)
            s = lax.dot_general(q_ref[rows, :], k, NT, preferred_element_type=F32)
            dp = lax.dot_general(do_ref[rows, :].astype(BF16), v, NT, preferred_element_type=F32)
            return s, dp

        def grads(i, s, dp, masked):
            rows = pl.ds(pl.multiple_of(i * T, T), T)
            do_f = do_ref[rows, :]
            delta = jnp.sum(do_f * o_ref[rows, :], axis=1, keepdims=True)
            pr = jnp.exp2(s - lse_ref[rows, 0:1])
            if masked:
                pr = jnp.where(_tile_mask(T), pr, 0.0)
            dv_scr[...] += lax.dot_general(pr.astype(BF16), do_f.astype(BF16), TN, preferred_element_type=F32)
            ds = (pr * (dp - delta)).astype(BF16)
            dk_scr[...] += lax.dot_general(ds, q_ref[rows, :], TN, preferred_element_type=F32)
            dq_ref[rows, :] += lax.dot_general(ds, k, NN, preferred_element_type=F32) * ATTN_SCALE

        grads(ki, *scores(ki), True)

        @pl.when(ki + 1 < nq)
        def _():
            s_scr[...], dp_scr[...] = scores(ki + 1)

            def step(i, carry):
                nxt_s, nxt_dp = scores(i + 1)
                grads(i, s_scr[...], dp_scr[...], False)
                s_scr[...] = nxt_s
                dp_scr[...] = nxt_dp
                return carry

            lax.fori_loop(ki + 1, nq - 1, step, 0)
            grads(nq - 1, s_scr[...], dp_scr[...], False)

        dk_ref[...] = dk_scr[...] * (1.0 / LOG2E)
        dv_ref[...] = dv_scr[...]

    return pl.pallas_call(
        body, name=name,
        out_shape=(jax.ShapeDtypeStruct((H, S, QK_PAD), F32), jax.ShapeDtypeStruct((H, S, QK_PAD), F32),
                   jax.ShapeDtypeStruct((H, S, VDIM), F32)),
        grid=(H, nq),
        in_specs=[pl.BlockSpec((None, S, QK_PAD), lambda h, j: (h, 0, 0)),
                  pl.BlockSpec((None, T, QK_PAD), lambda h, j: (h, j, 0)),
                  pl.BlockSpec((None, T, VDIM), lambda h, j: (h, j, 0)),
                  pl.BlockSpec((S, VDIM), lambda h, j: (0, h)),
                  pl.BlockSpec((S, VDIM), lambda h, j: (0, h)),
                  pl.BlockSpec((None, S, LANE), lambda h, j: (h, 0, 0))],
        out_specs=(pl.BlockSpec((None, S, QK_PAD), lambda h, j: (h, 0, 0)),
                   pl.BlockSpec((None, T, QK_PAD), lambda h, j: (h, j, 0)),
                   pl.BlockSpec((None, T, VDIM), lambda h, j: (h, j, 0))),
        scratch_shapes=[pltpu.VMEM((T, QK_PAD), F32), pltpu.VMEM((T, VDIM), F32),
                        pltpu.VMEM((T, T), F32), pltpu.VMEM((T, T), F32)],
        compiler_params=_params("parallel", "arbitrary"),
    )(Q, K, V, o, do, lse)


def _identity(n):
    return (lax.broadcasted_iota(jnp.int32, (n, n), 0) == lax.broadcasted_iota(jnp.int32, (n, n), 1)).astype(BF16)


def _tile_transpose(x, T, name):
    H, S, W = x.shape
    assert x.dtype == BF16

    def body(x_ref, o_ref):
        o_ref[...] = lax.dot_general(_identity(W), x_ref[...], NT, preferred_element_type=F32).astype(BF16)

    return pl.pallas_call(
        body, name=name, out_shape=jax.ShapeDtypeStruct((H, S // T, W, T), x.dtype), grid=(H, S // T),
        in_specs=[pl.BlockSpec((None, T, W), lambda h, i: (h, i, 0))],
        out_specs=pl.BlockSpec((None, None, W, T), lambda h, i: (h, i, 0, 0)),
        compiler_params=_params("parallel", "parallel"),
    )(x)


def _key_le_query(rows, cols, col0):
    key = lax.broadcasted_iota(jnp.int32, (rows, cols), 0) // CHUNK
    query = (lax.broadcasted_iota(jnp.int32, (rows, cols), 1) + col0) // CHUNK
    return key <= query


def _attn_fwd_t(Q, K, VT, name):
    H, nT, _, T = VT.shape
    S = nT * T
    n_part = 2 if T % 256 == 0 else 1
    Tq = T // n_part

    def body(q_ref, k_ref, vt_ref, o_ref, lse_ref, m_scr, l_scr, acc_scr, s_scr):
        qi = pl.program_id(1)
        m_scr[...] = jnp.full_like(m_scr, -jnp.inf)
        l_scr[...] = jnp.zeros_like(l_scr)
        acc_scr[...] = jnp.zeros_like(acc_scr)

        def scores(j):
            kt = k_ref[pl.ds(pl.multiple_of(j * T, T), T), :]
            return lax.dot_general(kt, q_ref[...], NT, preferred_element_type=F32)

        def softmax_step(j, masked):
            vt = vt_ref[j]
            for part in range(n_part):
                sub = slice(part * Tq, (part + 1) * Tq)
                st = s_scr[:, sub]
                if masked:
                    st = jnp.where(_key_le_query(T, Tq, part * Tq), st, -1e30)
                m_prev = m_scr[:, sub]
                m_new = jnp.maximum(m_prev, jnp.max(st, axis=0, keepdims=True))
                pt = jnp.exp2(st - m_new)
                alpha = jnp.exp2(m_prev - m_new)
                l_scr[:, sub] = alpha * l_scr[:, sub] + jnp.sum(pt, axis=0, keepdims=True)
                acc_scr[:, sub] = alpha * acc_scr[:, sub] + lax.dot_general(
                    vt, pt.astype(BF16), NN, preferred_element_type=F32)
                m_scr[:, sub] = m_new

        s_scr[...] = scores(0)

        def step(j, carry):
            nxt = scores(j + 1)
            softmax_step(j, False)
            s_scr[...] = nxt
            return carry

        lax.fori_loop(0, qi, step, 0)
        softmax_step(qi, True)
        l = l_scr[...]
        o_ref[...] = jnp.transpose(acc_scr[...] / l)
        lse_ref[...] = jnp.transpose(jnp.broadcast_to(m_scr[...] + jnp.log2(l), (LANE, T)))

    return pl.pallas_call(
        body, name=name,
        out_shape=(jax.ShapeDtypeStruct((S, H * VDIM), F32), jax.ShapeDtypeStruct((H, S, LANE), F32)),
        grid=(H, nT),
        in_specs=[pl.BlockSpec((None, T, QK_PAD), lambda h, i: (h, i, 0)),
                  pl.BlockSpec((None, S, QK_PAD), lambda h, i: (h, 0, 0)),
                  pl.BlockSpec((None, nT, VDIM, T), lambda h, i: (h, 0, 0, 0))],
        out_specs=(pl.BlockSpec((T, VDIM), lambda h, i: (i, h)),
                   pl.BlockSpec((None, T, LANE), lambda h, i: (h, i, 0))),
        scratch_shapes=[pltpu.VMEM((1, T), F32), pltpu.VMEM((1, T), F32), pltpu.VMEM((VDIM, T), F32),
                        pltpu.VMEM((T, T), F32)],
        compiler_params=_params("parallel", "arbitrary"),
    )(Q, K, VT)


def _adamw(w, g, m, v, name):
    shape = w.shape
    C = shape[-1]
    R = math.prod(shape[:-1])
    flat = [t.reshape(R, C) for t in (w, g, m, v)]
    tr = _row_tile(R, C * 4)

    def body(w_ref, g_ref, m_ref, v_ref, d_ref, nm_ref, nv_ref):
        gv = g_ref[...]
        m_new = ADAM_B1 * m_ref[...] + (1.0 - ADAM_B1) * gv
        v_new = ADAM_B2 * v_ref[...] + (1.0 - ADAM_B2) * jnp.square(gv)
        m_hat = m_new / (1.0 - ADAM_B1 ** ADAM_STEP)
        v_hat = v_new / (1.0 - ADAM_B2 ** ADAM_STEP)
        d_ref[...] = -ADAM_LR * (m_hat / (jnp.sqrt(v_hat) + ADAM_EPS) + ADAM_WD * w_ref[...])
        nm_ref[...] = m_new
        nv_ref[...] = v_new

    spec = pl.BlockSpec((tr, C), lambda i: (i, 0))
    out = jax.ShapeDtypeStruct((R, C), F32)
    d, nm, nv = pl.pallas_call(
        body, name=name, out_shape=(out, out, out), grid=(R // tr,),
        in_specs=[spec] * 4, out_specs=(spec, spec, spec), compiler_params=_params("parallel"),
    )(*flat)
    return d.reshape(shape), nm.reshape(shape), nv.reshape(shape)


def _sum_into_half(r, buf, layer, ci, n_layers, name):
    n, M, N = r.shape
    tr = _row_tile(M, N * 4 * n, 4 << 20)

    def body(c_ref, r_ref, *rest):
        o_ref = rest[-1]
        acc = r_ref[0].astype(F32)
        for s in range(1, n):
            acc = acc + r_ref[s].astype(F32)
        o_ref[...] = acc

    in_specs = [pl.BlockSpec((n, tr, N), lambda i, c: (0, i, 0))]
    operands = [ci.reshape(1), r]
    aliases = {}
    if buf is not None:
        in_specs.append(ANY)
        operands.append(buf)
        aliases = {2: 0}
    return pl.pallas_call(
        body, name=name, out_shape=jax.ShapeDtypeStruct((n_layers, 2, M, N), F32),
        grid_spec=pltpu.PrefetchScalarGridSpec(
            num_scalar_prefetch=1, grid=(M // tr,), in_specs=in_specs,
            out_specs=pl.BlockSpec((None, None, tr, N), lambda i, c: (layer, c[0], i, 0))),
        input_output_aliases=aliases, compiler_params=_params("parallel"),
    )(*operands)


def _sum_slots(r, name):
    n, M, N = r.shape
    tr = _row_tile(M, N * 4 * n, 4 << 20)

    def body(r_ref, o_ref):
        acc = r_ref[0].astype(F32)
        for s in range(1, n):
            acc = acc + r_ref[s].astype(F32)
        o_ref[...] = acc

    return pl.pallas_call(
        body, name=name, out_shape=jax.ShapeDtypeStruct((M, N), F32), grid=(M // tr,),
        in_specs=[pl.BlockSpec((n, tr, N), lambda i: (0, i, 0))],
        out_specs=pl.BlockSpec((tr, N), lambda i: (i, 0)), compiler_params=_params("parallel"),
    )(r)


ANY = pl.BlockSpec(memory_space=pl.ANY)
DMA_CHUNK_BYTES = 1 << 20
DMA_MAX_CHUNKS = 16
PEER_ORDER = (1, 4, 5, 2, 3, 6, 7)


def _position():
    return lax.axis_index("x"), lax.axis_index("y"), lax.axis_index("c")


def _row_chunks(shape, dtype):
    rows, cols = shape
    n = max(1, min(DMA_MAX_CHUNKS, rows * cols * jnp.dtype(dtype).itemsize // DMA_CHUNK_BYTES))
    while n > 1 and (rows % n or (rows // n) % 16):
        n -= 1
    step = rows // n
    return [pl.ds(q * step, step) for q in range(n)]


def _all_gather8(xs, name):
    n = len(xs)

    def body(*refs):
        x_refs, o_refs = refs[:n], refs[n:2 * n]
        send_sems, recv_sems, local_sems = refs[2 * n:]
        x, y, c = _position()
        me, sibling = (x, y, c), (x, y, 1 - c)
        chips = [(1 - x, y), (x, 1 - y), (1 - x, 1 - y)]

        def slot(a, dev, rows):
            return o_refs[a].at[4 * dev[0] + 2 * dev[1] + dev[2], rows]

        def copy(a, k, block, to, rows, from_input=False):
            return pltpu.make_async_remote_copy(
                src_ref=x_refs[a].at[rows] if from_input else slot(a, block, rows), dst_ref=slot(a, block, rows),
                send_sem=send_sems.at[a, k], recv_sem=recv_sems.at[a, k],
                device_id=to, device_id_type=MESH)

        def mine(a, rows):
            return pltpu.make_async_copy(x_refs[a].at[rows], slot(a, me, rows), local_sems.at[a])

        chunks = [_row_chunks(t.shape, t.dtype) for t in xs]
        whole = [pl.ds(0, t.shape[0]) for t in xs]
        for a in range(n):
            for rows in chunks[a]:
                mine(a, rows).start()
        sent = []
        for a in range(n):
            for k, to in enumerate([sibling] + [(*chip, c) for chip in chips]):
                for rows in chunks[a]:
                    copy(a, k, me, to, rows, from_input=True).start()
                sent.append(copy(a, k, me, to, whole[a], from_input=True))
        for a in range(n):
            for j, chip in enumerate(chips):
                copy(a, 1 + j, (*chip, c), me, whole[a]).wait_recv()
                for rows in chunks[a]:
                    copy(a, 4 + j, (*chip, c), sibling, rows).start()
                sent.append(copy(a, 4 + j, (*chip, c), sibling, whole[a]))
        for a in range(n):
            copy(a, 0, sibling, me, whole[a]).wait_recv()
            for j, chip in enumerate(chips):
                copy(a, 4 + j, (*chip, 1 - c), me, whole[a]).wait_recv()
        for cp in sent:
            cp.wait_send()
        for a in range(n):
            mine(a, whole[a]).wait()

    return pl.pallas_call(
        body, name=name,
        out_shape=[jax.ShapeDtypeStruct((8,) + t.shape, t.dtype) for t in xs],
        in_specs=[ANY] * n, out_specs=[ANY] * n,
        scratch_shapes=[pltpu.SemaphoreType.DMA((n, 7)), pltpu.SemaphoreType.DMA((n, 7)),
                        pltpu.SemaphoreType.DMA((n,))],
    )(*xs)


HBM = pl.BlockSpec(memory_space=pltpu.HBM)
SEM = pl.BlockSpec(memory_space=pltpu.SEMAPHORE)
EFFECT = pltpu.SideEffectType.DATAFLOW_SIDE_EFFECTING


def _peer(m, x, y, c):
    return ((1 - x) if m & 4 else x, (1 - y) if m & 2 else y, (1 - c) if m & 1 else c)


def _send_copies(src_refs, land_refs, send_sems, recv_sems, broadcast):
    x, y, c = _position()
    my = 4 * x + 2 * y + c
    out = []
    for a in range(len(src_refs)):
        for m in PEER_ORDER:
            px, py, pc = _peer(m, x, y, c)
            src = src_refs[a] if broadcast else src_refs[a].at[4 * px + 2 * py + pc]
            out.append(pltpu.make_async_remote_copy(
                src_ref=src, dst_ref=land_refs[a].at[my], send_sem=send_sems[a], recv_sem=recv_sems[a],
                device_id=(px, py, pc), device_id_type=MESH))
    return out


def _send_drain(land_refs, send_sems, recv_sems):
    x, y, c = _position()
    for a in range(len(land_refs)):
        seven = land_refs[a].at[pl.ds(0, 7)]
        both = pltpu.make_async_remote_copy(
            src_ref=seven, dst_ref=seven, send_sem=send_sems[a], recv_sem=recv_sems[a],
            device_id=(x, y, c), device_id_type=MESH)
        both.wait_send()
        both.wait_recv()


def _send_start(srcs, lands, after, broadcast, name):
    n = len(srcs)
    extra = [] if after is None else [after]

    def body(*refs):
        src_refs, land_refs = refs[:n], refs[n:2 * n]
        outs = refs[2 * n + len(extra):]
        send_sems, recv_sems = outs[:n], outs[n:2 * n]
        token = refs[-1]
        for cp in _send_copies(src_refs, land_refs, send_sems, recv_sems, broadcast):
            cp.start()
        token[...] = jnp.zeros_like(token)

    hbm = [pltpu.with_memory_space_constraint(t, pltpu.HBM) for t in list(srcs) + list(lands)]
    res = pl.pallas_call(
        body, name=name,
        out_shape=(*[pltpu.SemaphoreType.DMA(())] * (2 * n),
                   *[pltpu.HBM(t.shape, t.dtype) for t in hbm], jax.ShapeDtypeStruct((8, LANE), F32)),
        in_specs=[HBM] * (2 * n) + [ANY] * len(extra),
        out_specs=(*[SEM] * (2 * n), *[HBM] * (2 * n), pl.BlockSpec(memory_space=pltpu.VMEM)),
        input_output_aliases={i: 2 * n + i for i in range(2 * n)},
        compiler_params=pltpu.CompilerParams(has_side_effects=EFFECT),
    )(*hbm, *extra)
    return dict(sems=res[:2 * n], srcs=res[2 * n:3 * n], lands=res[3 * n:4 * n], token=res[-1], broadcast=broadcast)


def _send_wait(started, after, name):
    n = len(started["srcs"])

    def body(*refs):
        land_refs = refs[n:2 * n]
        send_sems, recv_sems = refs[2 * n:3 * n], refs[3 * n:4 * n]
        _send_drain(land_refs, send_sems, recv_sems)

    operands = list(started["srcs"]) + list(started["lands"])
    res = pl.pallas_call(
        body, name=name,
        out_shape=[pltpu.HBM(t.shape, t.dtype) for t in operands],
        in_specs=[HBM] * (2 * n) + [SEM] * (2 * n) + [ANY],
        out_specs=[HBM] * (2 * n),
        input_output_aliases={i: i for i in range(2 * n)},
        compiler_params=pltpu.CompilerParams(has_side_effects=EFFECT),
    )(*operands, *started["sems"], after)
    return res[n:]


def _own_slot(block, dev):
    zone = lax.empty((8,) + block.shape, block.dtype)
    return lax.dynamic_update_slice(zone, block[None], (dev, 0, 0))


def _pair_swap(bufs, name):
    n = len(bufs)
    pieces = [(a, l) for a, t in enumerate(bufs) for l in range(t.shape[0])]

    def body(*refs):
        b_refs = refs[n:2 * n]
        send_sems, recv_sems = refs[2 * n:]
        x, y, c = _position()

        def copy(k, rows):
            a, l = pieces[k]
            half = b_refs[a].at[l, c, rows]
            return pltpu.make_async_remote_copy(
                src_ref=half, dst_ref=half, send_sem=send_sems.at[k], recv_sem=recv_sems.at[k],
                device_id=(x, y, 1 - c), device_id_type=MESH)

        chunks = [_row_chunks(bufs[a].shape[2:], bufs[a].dtype) for a, _ in pieces]
        whole = [pl.ds(0, bufs[a].shape[2]) for a, _ in pieces]
        for k in range(len(pieces)):
            for rows in chunks[k]:
                copy(k, rows).start()
        for k in range(len(pieces)):
            copy(k, whole[k]).wait_recv()
        for k in range(len(pieces)):
            copy(k, whole[k]).wait_send()

    return pl.pallas_call(
        body, name=name,
        out_shape=[jax.ShapeDtypeStruct(t.shape, t.dtype) for t in bufs],
        in_specs=[ANY] * n, out_specs=[ANY] * n,
        input_output_aliases={a: a for a in range(n)},
        scratch_shapes=[pltpu.SemaphoreType.DMA((len(pieces),)), pltpu.SemaphoreType.DMA((len(pieces),))],
    )(*bufs)


def _pack_rows(parts):
    flat = jnp.concatenate([t.reshape(-1).astype(F32) for t in parts])
    pad = (-flat.shape[0]) % (256 * LANE)
    return jnp.pad(flat, (0, pad)).reshape(-1, LANE)


def _my_half(w2d, ci):
    half = w2d.shape[0] // 2
    return lax.dynamic_slice_in_dim(w2d, ci * half, half, axis=0).astype(BF16)


def _col_view(g):
    _, half, Cs = g.shape
    return g.reshape(4, 2 * half, Cs)


def _row_view(g):
    _, half, C = g.shape
    return g.reshape(8 * half, C)


def _rope_tables(S):
    pos = jnp.arange(S, dtype=F32)
    inv_freq = ROPE_THETA ** (-jnp.arange(0, ROPE, 2, dtype=F32) / ROPE)
    ang = pos[:, None] * inv_freq[None, :]
    cos, sin = jnp.cos(ang), jnp.sin(ang)
    cos_t = jnp.concatenate([cos, cos, cos, cos], axis=1)
    sin_t = jnp.concatenate([-sin, sin, -sin, sin], axis=1)
    return cos_t, sin_t


def kernel(x, c, ada_w, ada_b, pre_g, post_g, sgu_w_in, sgu_norm_g, sgu_w_s, sgu_b_s, sgu_w_out, mla_w_in, mla_q_norm_g, mla_kv_norm_g, mla_w_uq, mla_w_ukv, mla_w_out, loss_target, m_ada_w, m_ada_b, m_pre_g, m_post_g, m_sgu_w_in, m_sgu_norm_g, m_sgu_w_s, m_sgu_b_s, m_sgu_w_out, m_mla_w_in, m_mla_q_norm_g, m_mla_kv_norm_g, m_mla_w_uq, m_mla_w_ukv, m_mla_w_out, v_ada_w, v_ada_b, v_pre_g, v_post_g, v_sgu_w_in, v_sgu_norm_g, v_sgu_w_s, v_sgu_b_s, v_sgu_w_out, v_mla_w_in, v_mla_q_norm_g, v_mla_kv_norm_g, v_mla_w_uq, v_mla_w_ukv, v_mla_w_out):
    S, D = x.shape[1], x.shape[2]
    depth = ada_w.shape[0]
    E = sgu_w_out.shape[1] * 4
    xi, yi, ci = _position()
    chip = 2 * xi + yi
    dev = 4 * xi + 2 * yi + ci
    x0 = x.reshape(S, D)
    target = loss_target.reshape(S, D)

    small = _pack_rows([c, mla_q_norm_g, mla_kv_norm_g])
    mixer_w = dict(sin=sgu_w_in, sout=sgu_w_out, min=mla_w_in, uq=mla_w_uq, ukv=mla_w_ukv, mout=mla_w_out)
    small_g, first_g = _all_gather8([small, _my_half(sgu_w_in[0], ci)], "gather_first")
    small_all = small_g.reshape(8, -1)
    gathered_w = {("sin", 0): first_g}
    qn_w, kvn_w = mla_q_norm_g.shape[1], mla_kv_norm_g.shape[1]
    c_all = small_all[:, :D]
    qn_all = small_all[0::2, D:D + 2 * qn_w].reshape(4, 2, qn_w)
    kvn_all = small_all[0::2, D + 2 * qn_w:D + 2 * qn_w + 2 * kvn_w].reshape(4, 2, kvn_w)
    q_gain = jnp.pad(jnp.transpose(qn_all, (1, 0, 2)).reshape(2, 1, Q_RANK), ((0, 0), (0, 0), (0, Q_RANK_PAD - Q_RANK)))
    kv_gain = jnp.transpose(kvn_all, (1, 0, 2)).reshape(2, 1, KV_RANK)

    views = {}

    def weight(t, j):
        if (t, j) not in views:
            g = gathered_w[(t, j)]
            v = _row_view(g) if t in ("sout", "mout") else _col_view(g)
            if t == "uq":
                v = jnp.pad(v, ((0, 0), (0, Q_RANK_PAD - Q_RANK), (0, 0)))
            views[(t, j)] = v
        return views[(t, j)]

    cols = ada_w.shape[2]
    ada_b_cols = lax.dynamic_slice_in_dim(ada_b, chip * cols, cols, axis=1)
    c_pad = jnp.pad(c_all, ((0, 8), (0, 0)))
    mod_cols = _ada_mod(c_pad, ada_w, ada_b_cols, "ada_mod")[:, :8]
    mod_g, = _all_gather8([mod_cols.reshape(depth * 8, cols)], "gather_mod")
    mod_all = jnp.transpose(mod_g[0::2].reshape(4, depth, 8, cols), (1, 2, 0, 3)).reshape(depth, 8, 4 * cols)

    groups = [("sout0", [("sout", 0)]), ("mla0", [(t, 0) for t in ("min", "uq", "ukv", "mout")]),
              ("sgu1", [("sin", 1), ("sout", 1)]), ("mla1", [(t, 1) for t in ("min", "uq", "ukv", "mout")])]
    sends = {}
    behind = mod_g
    for gname, items in groups:
        blocks = [_my_half(mixer_w[t][j], ci) for t, j in items]
        sends[gname] = _send_start(blocks, [_own_slot(b, dev) for b in blocks], behind, True, f"send_{gname}")
        behind = sends[gname]["token"]

    def arrive(gname, after):
        lands = _send_wait(sends[gname], after, f"arrive_{gname}")
        gathered_w.update(zip(dict(groups)[gname], lands))
    mod = lax.dynamic_index_in_dim(mod_all, dev, 1, keepdims=False)
    shift = [mod[i:i + 1, :D] for i in range(depth)]
    scale = [mod[i:i + 1, D:2 * D] for i in range(depth)]
    gate = [mod[i:i + 1, 2 * D:] for i in range(depth)]

    cos_t, sin_t = _rope_tables(S)
    b_bc = jnp.broadcast_to(sgu_b_s[:, :, :, None], sgu_b_s.shape + (LANE,))
    w_sT = jnp.swapaxes(sgu_w_s, 2, 3)

    saved = []
    xs = x0
    for i in range(depth):
        j = i // 2
        tag = f"l{i}"
        h = _pre_fwd(xs, pre_g[i:i + 1], scale[i], shift[i], f"pre_fwd_{tag}", after=(behind,) if i == 0 else ())
        if i % 2 == 0:
            if j > 0:
                arrive(f"sgu{j}", h)
            uvz = _mm(h, weight("sin", j), b_sharded=True, name=f"sgu_in_{tag}")
            y = _sgu_gate_fwd(uvz, sgu_norm_g[j:j + 1], sgu_w_s[j], b_bc[j], f"sgu_gate_fwd_{tag}")
            if j == 0:
                arrive("sout0", y)
            out = _mm(y, weight("sout", j), name=f"sgu_out_{tag}")
            saved.append(dict(x=xs, h=h, uvz=uvz, y=y, out=out))
        else:
            arrive(f"mla{j}", h)
            p = _mm(h, weight("min", j), b_sharded=True, name=f"mla_in_{tag}")
            cqn, ckvn = _mla_mid_fwd(p, q_gain[j], kv_gain[j], f"mla_mid_fwd_{tag}")
            q = _mm(cqn, weight("uq", j), b_sharded=True, name=f"mla_uq_{tag}")
            kv = _mm(ckvn, weight("ukv", j), b_sharded=True, name=f"mla_ukv_{tag}")
            Q, K, V = _mla_pack(q, kv, p, cos_t, sin_t, f"mla_pack_{tag}")
            T = min(512, S)
            VT = _tile_transpose(V, T, f"v_tiles_{tag}")
            o, lse = _attn_fwd_t(Q, K, VT, f"attn_fwd_{tag}")
            y = _mla_gate_fwd(o, p, f"mla_gate_fwd_{tag}")
            out = _mm(y, weight("mout", j), name=f"mla_out_{tag}")
            saved.append(dict(x=xs, h=h, p=p, cqn=cqn, ckvn=ckvn, Q=Q, K=K, V=V, o=o, lse=lse, y=y, out=out))
        xs = _post_fwd(xs, out, gate[i], post_g[i:i + 1], f"post_fwd_{tag}")

    dx, loss_part = _loss_grad(xs, target, "loss")
    loss = lax.psum(loss_part[0, 0], ("x", "y", "c"))

    dmod = [None] * depth
    d_pre_g = [None] * depth
    d_post_g = [None] * depth
    d_sgu = [None] * 2
    d_mla = [None] * 2
    kinds = ("sgu_w_in", "sgu_w_out", "mla_w_in", "mla_w_uq", "mla_w_ukv", "mla_w_out")
    halves = dict.fromkeys(kinds)
    in_flight = []

    def send_grads(items, label):
        slices = [dw.reshape(8, -1, dw.shape[-1]) for _, _, dw in items]
        lands = [_own_slot(lax.dynamic_index_in_dim(s, dev, 0, keepdims=False), dev) for s in slices]
        started = _send_start(slices, lands, None, False, f"send_{label}")
        in_flight.append((started, [(kind, layer) for kind, layer, _ in items], label))
        return (started["token"],)

    def collect(count, after):
        for _ in range(count):
            started, keys, label = in_flight.pop(0)
            lands = _send_wait(started, after, f"arrive_{label}")
            for (kind, layer), r in zip(keys, lands):
                halves[kind] = _sum_into_half(r, halves[kind], layer, ci, 2, f"sum_{kind}_{layer}")

    for i in reversed(range(depth)):
        j = i // 2
        tag = f"l{i}"
        sv = saved[i]
        older = len(in_flight)
        dy, dgate, d_post_g[i] = _post_bwd(dx, sv["out"], gate[i], post_g[i:i + 1], f"post_bwd_{tag}")
        if i % 2 == 0:
            dw_out = _mm(sv["y"], dy, ta=True, out_dtype=BF16, name=f"sgu_out_dw_{tag}")
            sent = send_grads([("sgu_w_out", j, dw_out)], f"{tag}_out")
            dyv = _mm(dy, weight("sout", j), tb=True, after=sent, name=f"sgu_out_dx_{tag}")
            duvz, dng, dws, dbs = _sgu_gate_bwd(sv["uvz"], dyv, sgu_norm_g[j:j + 1], sgu_w_s[j], w_sT[j], b_bc[j],
                                                f"sgu_gate_bwd_{tag}")
            sent = ()
            if i > 0:
                dw_in = _mm(sv["h"], duvz, ta=True, out_sharded=True, out_dtype=BF16, name=f"sgu_in_dw_{tag}")
                sent = send_grads([("sgu_w_in", j, dw_in)], f"{tag}_in")
            dh = _mm(duvz, weight("sin", j), tb=True, b_sharded=True, after=sent, name=f"sgu_in_dx_{tag}")
            d_sgu[j] = dict(norm_g=dng, w_s=dws, b_s=dbs[:, :, 0], duvz=duvz)
        else:
            dw_out = _mm(sv["y"], dy, ta=True, out_dtype=BF16, name=f"mla_out_dw_{tag}")
            sent = send_grads([("mla_w_out", j, dw_out)], f"{tag}_out")
            dyv = _mm(dy, weight("mout", j), tb=True, after=sent, name=f"mla_out_dx_{tag}")
            p = sv["p"]
            do = _mla_gate_bwd(dyv, p, f"mla_gate_bwd_{tag}")
            dQ, dK, dV = _attn_bwd(sv["Q"], sv["K"], sv["V"], sv["o"], do, sv["lse"], f"attn_bwd_{tag}")
            dq, dkv, dkr = _mla_unpack(dQ, dK, dV, cos_t, sin_t, f"mla_unpack_{tag}")
            dw_uq = _mm(sv["cqn"], dq, ta=True, out_sharded=True, out_dtype=BF16, name=f"mla_uq_dw_{tag}")
            dcqn = _mm(dq, weight("uq", j), tb=True, b_sharded=True, name=f"mla_uq_dx_{tag}")
            dw_ukv = _mm(sv["ckvn"], dkv, ta=True, out_sharded=True, out_dtype=BF16, name=f"mla_ukv_dw_{tag}")
            dckvn = _mm(dkv, weight("ukv", j), tb=True, b_sharded=True, name=f"mla_ukv_dx_{tag}")
            dp, dqg, dkvg = _mla_mid_bwd(p, dcqn, dckvn, dkr, dyv, sv["o"], q_gain[j], kv_gain[j], f"mla_mid_bwd_{tag}")
            dw_in = _mm(sv["h"], dp, ta=True, out_sharded=True, out_dtype=BF16, name=f"mla_in_dw_{tag}")
            sent = send_grads([("mla_w_in", j, dw_in), ("mla_w_uq", j, dw_uq[:, :Q_RANK]), ("mla_w_ukv", j, dw_ukv)],
                              f"{tag}_in")
            dh = _mm(dp, weight("min", j), tb=True, b_sharded=True, after=sent, name=f"mla_in_dx_{tag}")
            d_mla[j] = dict(qg=dqg[0, :Q_RANK], kvg=dkvg[0])
        dx, dshift, dscale, d_pre_g[i] = _pre_bwd(dh, sv["x"], dx, pre_g[i:i + 1], scale[i], f"pre_bwd_{tag}")
        dmod[i] = jnp.concatenate([dshift, dscale, dgate], axis=1)
        collect(older, dx)
    grad_x = dx.reshape(x.shape)

    parts = [jnp.concatenate(dmod, axis=0), jnp.concatenate(d_pre_g, axis=0), jnp.concatenate(d_post_g, axis=0),
             jnp.stack([d["norm_g"][0] for d in d_sgu]), jnp.stack([d["w_s"] for d in d_sgu]),
             jnp.stack([d["b_s"] for d in d_sgu]), jnp.stack([d["qg"] for d in d_mla]),
             jnp.stack([d["kvg"] for d in d_mla])]
    sizes = [int(np.prod(t.shape)) for t in parts]
    packed = _pack_rows(parts)
    packed_all, dmod_all = _all_gather8([packed, parts[0]], "gather_small_grads")
    total = _sum_slots(packed_all, "sum_small_grads").reshape(-1)
    offs = np.concatenate([[0], np.cumsum(sizes)])
    pieces = [total[int(offs[t]):int(offs[t + 1])].reshape(parts[t].shape) for t in range(len(parts))]
    g_ada_b, g_pre_g, g_post_g, g_norm_g, g_w_s, g_b_s, g_qg_full, g_kvg_full = pieces
    g_qg = lax.dynamic_slice_in_dim(g_qg_full, chip * qn_w, qn_w, axis=1)
    g_kvg = lax.dynamic_slice_in_dim(g_kvg_full, chip * kvn_w, kvn_w, axis=1)
    dmod_cols = jnp.stack([lax.dynamic_slice_in_dim(dmod_all[:, i], chip * cols, cols, axis=1) for i in range(depth)])
    dmod_cols = jnp.pad(dmod_cols, ((0, 0), (0, LANE - 8), (0, 0)))

    dw_in0 = _mm(saved[0]["h"], d_sgu[0]["duvz"], ta=True, out_sharded=True, out_dtype=BF16, after=(packed_all,),
                 name="sgu_in_dw_l0")
    sent = send_grads([("sgu_w_in", 0, dw_in0)], "l0_in")
    g_ada_w = _ada_grad(jnp.pad(c_all.T, ((0, 0), (0, LANE - 8))), dmod_cols, "ada_grad", after=sent)

    wnames = ["ada_w", "ada_b", "pre_g", "post_g", "sgu_w_in", "sgu_norm_g", "sgu_w_s", "sgu_b_s", "sgu_w_out",
              "mla_w_in", "mla_q_norm_g", "mla_kv_norm_g", "mla_w_uq", "mla_w_ukv", "mla_w_out"]
    weights = dict(zip(wnames, [ada_w, ada_b, pre_g, post_g, sgu_w_in, sgu_norm_g, sgu_w_s, sgu_b_s, sgu_w_out,
                                mla_w_in, mla_q_norm_g, mla_kv_norm_g, mla_w_uq, mla_w_ukv, mla_w_out]))
    ms = dict(zip(wnames, [m_ada_w, m_ada_b, m_pre_g, m_post_g, m_sgu_w_in, m_sgu_norm_g, m_sgu_w_s, m_sgu_b_s,
                           m_sgu_w_out, m_mla_w_in, m_mla_q_norm_g, m_mla_kv_norm_g, m_mla_w_uq, m_mla_w_ukv,
                           m_mla_w_out]))
    vs = dict(zip(wnames, [v_ada_w, v_ada_b, v_pre_g, v_post_g, v_sgu_w_in, v_sgu_norm_g, v_sgu_w_s, v_sgu_b_s,
                           v_sgu_w_out, v_mla_w_in, v_mla_q_norm_g, v_mla_kv_norm_g, v_mla_w_uq, v_mla_w_ukv,
                           v_mla_w_out]))
    grads = dict(ada_w=g_ada_w, ada_b=g_ada_b, pre_g=g_pre_g, post_g=g_post_g, sgu_norm_g=g_norm_g, sgu_w_s=g_w_s,
                 sgu_b_s=g_b_s, mla_q_norm_g=g_qg, mla_kv_norm_g=g_kvg)
    stepped = {}

    def step(nm):
        grads[nm] = grads[nm].reshape(weights[nm].shape)
        stepped[nm] = _adamw(weights[nm], grads[nm], ms[nm], vs[nm], f"adamw_{nm}")

    for nm in wnames:
        if nm in grads:
            step(nm)
    early = [kind for kind in kinds if kind != "sgu_w_in"]
    collect(len(in_flight) - 1, stepped["ada_w"][0])
    for kind, g in zip(early, _pair_swap([halves[kind] for kind in early], "swap_grads")):
        grads[kind] = g
        step(kind)
    collect(len(in_flight), stepped[early[-1]][0])
    grads["sgu_w_in"], = _pair_swap([halves["sgu_w_in"]], "swap_grads_last")
    step("sgu_w_in")
    return (loss, grad_x, *[grads[nm] for nm in wnames], *[stepped[nm][0] for nm in wnames],
            *[stepped[nm][1] for nm in wnames], *[stepped[nm][2] for nm in wnames])
```

```python
import math

import jax
import jax.numpy as jnp
import numpy as np
from jax import lax
from jax.experimental import pallas as pl
from jax.experimental.pallas import tpu as pltpu

F32 = jnp.float32
BF16 = jnp.bfloat16
MESH = pl.DeviceIdType.MESH

NORM_EPS = 1e-6
CHUNK = 64
SGU_BLOCK = 128
SGU_GROUPS = 16
HEADS = 16
NOPE = 128
ROPE = 64
VDIM = 128
QK_PAD = 256
Q_RANK = 448
Q_RANK_PAD = 512
KV_RANK = 512
ROPE_THETA = 10000.0
ATTN_SCALE = (NOPE + ROPE) ** -0.5
LOG2E = 1.4426950408889634
Q_FOLD = ATTN_SCALE * LOG2E
ATTN_TILE = 512

ADAM_LR = 0.001
ADAM_B1 = 0.9
ADAM_B2 = 0.999
ADAM_EPS = 1e-08
ADAM_WD = 0.01
ADAM_STEP = 10

LANE = 128
VMEM_LIMIT = 48 * 1024 * 1024

NN = (((1,), (0,)), ((), ()))
NT = (((1,), (1,)), ((), ()))
TN = (((0,), (0,)), ((), ()))


def _params(*sem):
    return pltpu.CompilerParams(dimension_semantics=sem, vmem_limit_bytes=VMEM_LIMIT)


def _row_tile(rows, row_bytes, target_bytes=1 << 20):
    if rows * row_bytes <= target_bytes or rows % 16:
        return rows
    best = 16
    t = 16
    while t <= rows:
        if rows % t == 0 and t * row_bytes <= target_bytes:
            best = t
        t += 16
    return best


def _fit(dim, target):
    if dim <= target:
        return dim
    t = (target // LANE) * LANE
    while t > LANE and dim % t:
        t -= LANE
    return t


def _gelu(x):
    return 0.5 * x * (1.0 + lax.erf(x * 0.7071067811865476))


def _gelu_grad(x):
    return 0.5 * (1.0 + lax.erf(x * 0.7071067811865476)) + x * jnp.exp(-0.5 * x * x) * 0.3989422804014327


def _mm(a, b, *, ta=False, tb=False, b_sharded=False, out_sharded=False, out_dtype=F32,
        tm=1024, tn=1024, tk=2048, after=(), name):
    if ta:
        K, M = a.shape
    else:
        M, K = a.shape
    if b_sharded:
        shards, rows, Cs = b.shape
        b_shape = (rows, shards * Cs)
    else:
        b_shape = b.shape
    if tb:
        N, K2 = b_shape
    else:
        K2, N = b_shape
    assert K == K2, (a.shape, b.shape, ta, tb)
    n_lim = Cs if (b_sharded and not tb) else (N // 4 if out_sharded else N)
    k_lim = Cs if (b_sharded and tb) else K
    tm, tn, tk = _fit(M, tm), _fit(n_lim, tn), _fit(k_lim, tk)
    assert M % tm == 0 and n_lim % tn == 0 and k_lim % tk == 0, (M, N, K, tm, tn, tk)
    nk = K // tk
    nb_n = n_lim // tn
    nb_k = k_lim // tk
    dims = (((0 if ta else 1,), (1 if tb else 0,)), ((), ()))

    def body(a_ref, b_ref, *rest):
        o_ref, *scratch = rest[len(after):]
        prod = lax.dot_general(a_ref[...].astype(BF16), b_ref[...].astype(BF16), dims,
                               preferred_element_type=F32)
        if nk == 1:
            o_ref[...] = prod.astype(out_dtype)
        else:
            acc_ref, = scratch
            k = pl.program_id(2)

            @pl.when(k == 0)
            def _():
                acc_ref[...] = prod

            @pl.when(k > 0)
            def _():
                acc_ref[...] += prod

            @pl.when(k == nk - 1)
            def _():
                o_ref[...] = acc_ref[...].astype(out_dtype)

    a_spec = (pl.BlockSpec((tk, tm), lambda i, j, k: (k, i)) if ta
              else pl.BlockSpec((tm, tk), lambda i, j, k: (i, k)))
    if b_sharded and tb:
        b_spec = pl.BlockSpec((None, tn, tk), lambda i, j, k: (k // nb_k, j, k % nb_k))
    elif b_sharded:
        b_spec = pl.BlockSpec((None, tk, tn), lambda i, j, k: (j // nb_n, k, j % nb_n))
    elif tb:
        b_spec = pl.BlockSpec((tn, tk), lambda i, j, k: (j, k))
    else:
        b_spec = pl.BlockSpec((tk, tn), lambda i, j, k: (k, j))
    if out_sharded:
        out_shape = jax.ShapeDtypeStruct((4, M, N // 4), out_dtype)
        out_spec = pl.BlockSpec((None, tm, tn), lambda i, j, k: (j // nb_n, i, j % nb_n))
    else:
        out_shape = jax.ShapeDtypeStruct((M, N), out_dtype)
        out_spec = pl.BlockSpec((tm, tn), lambda i, j, k: (i, j))
    return pl.pallas_call(
        body, name=name,
        out_shape=out_shape,
        grid=(M // tm, N // tn, nk),
        in_specs=[a_spec, b_spec] + [pl.BlockSpec(memory_space=pl.ANY)] * len(after),
        out_specs=out_spec,
        scratch_shapes=[] if nk == 1 else [pltpu.VMEM((tm, tn), F32)],
        compiler_params=_params("parallel", "parallel", "arbitrary"),
    )(a, b, *after)


def _split_bf16(v):
    hi = v.astype(BF16)
    lo = (v - hi.astype(F32)).astype(BF16)
    return hi, lo


def _dot3(a, b, dims):
    a_hi, a_lo = _split_bf16(a)
    b_hi, b_lo = _split_bf16(b)
    out = lax.dot_general(a_hi, b_hi, dims, preferred_element_type=F32)
    out += lax.dot_general(a_lo, b_hi, dims, preferred_element_type=F32)
    out += lax.dot_general(a_hi, b_lo, dims, preferred_element_type=F32)
    return out


def _ada_mod(c_all, ada_w, ada_b_cols, name):
    L, D, cols = ada_w.shape
    B = c_all.shape[0]
    tn = 512 if cols % 512 == 0 else cols

    def body(c_ref, w_ref, b_ref, o_ref):
        cv = c_ref[...]
        cond = cv * jax.nn.sigmoid(cv)
        o_ref[...] = _dot3(cond, w_ref[...], NN) + b_ref[...]

    return pl.pallas_call(
        body, name=name,
        out_shape=jax.ShapeDtypeStruct((L, B, cols), F32),
        grid=(L, cols // tn),
        in_specs=[pl.BlockSpec((B, D), lambda l, j: (0, 0)),
                  pl.BlockSpec((None, D, tn), lambda l, j: (l, 0, j)),
                  pl.BlockSpec((None, 1, tn), lambda l, j: (l, 0, j))],
        out_specs=pl.BlockSpec((None, B, tn), lambda l, j: (l, 0, j)),
        compiler_params=_params("parallel", "parallel"),
    )(c_all, ada_w, ada_b_cols.reshape(L, 1, cols))


def _ada_grad(c_t, dmod_cols, name, after=()):
    L, B, cols = dmod_cols.shape
    D = c_t.shape[0]
    tn = 512 if cols % 512 == 0 else cols

    def body(c_ref, d_ref, *rest):
        o_ref = rest[-1]
        cv = c_ref[...]
        cond = cv * jax.nn.sigmoid(cv)
        o_ref[...] = _dot3(cond, d_ref[...], NN)

    return pl.pallas_call(
        body, name=name,
        out_shape=jax.ShapeDtypeStruct((L, D, cols), F32),
        grid=(L, cols // tn),
        in_specs=[pl.BlockSpec((D, B), lambda l, j: (0, 0)),
                  pl.BlockSpec((None, B, tn), lambda l, j: (l, 0, j))] + [pl.BlockSpec(memory_space=pl.ANY)] * len(after),
        out_specs=pl.BlockSpec((None, D, tn), lambda l, j: (l, 0, j)),
        compiler_params=_params("parallel", "parallel"),
    )(c_t, dmod_cols, *after)


def _row_spec(ts, width):
    return pl.BlockSpec((ts, width), lambda i: (i, 0))


def _vec_spec(width):
    return pl.BlockSpec((1, width), lambda i: (0, 0))


def _pre_fwd(x, pre_g, scale, shift, name, after=()):
    S, D = x.shape
    ts = min(256, S)

    def body(x_ref, g_ref, sc_ref, sh_ref, *rest):
        h_ref = rest[-1]
        xv = x_ref[...]
        r = lax.rsqrt(jnp.mean(xv * xv, axis=-1, keepdims=True) + NORM_EPS)
        h_ref[...] = ((xv * r * g_ref[...]) * (1.0 + sc_ref[...]) + sh_ref[...]).astype(BF16)

    return pl.pallas_call(
        body, name=name, out_shape=jax.ShapeDtypeStruct((S, D), BF16), grid=(S // ts,),
        in_specs=[_row_spec(ts, D), _vec_spec(D), _vec_spec(D), _vec_spec(D)]
        + [pl.BlockSpec(memory_space=pl.ANY)] * len(after),
        out_specs=_row_spec(ts, D), compiler_params=_params("parallel"),
    )(x, pre_g, scale, shift, *after)


def _pre_bwd(dh, x, dx_res, pre_g, scale, name):
    S, D = x.shape
    ts = min(256, S)

    def body(dh_ref, x_ref, dr_ref, g_ref, sc_ref, dx_ref, dsh_ref, dsc_ref, dg_ref):
        @pl.when(pl.program_id(0) == 0)
        def _():
            dsh_ref[...] = jnp.zeros_like(dsh_ref)
            dsc_ref[...] = jnp.zeros_like(dsc_ref)
            dg_ref[...] = jnp.zeros_like(dg_ref)

        dh = dh_ref[...]
        xv = x_ref[...]
        g = g_ref[...]
        one_sc = 1.0 + sc_ref[...]
        r = lax.rsqrt(jnp.mean(xv * xv, axis=-1, keepdims=True) + NORM_EPS)
        xn = xv * r
        dsh_ref[...] += jnp.sum(dh, axis=0, keepdims=True)
        dsc_ref[...] += jnp.sum(dh * (xn * g), axis=0, keepdims=True)
        dg_ref[...] += jnp.sum(dh * one_sc * xn, axis=0, keepdims=True)
        dxn = dh * one_sc * g
        dx_ref[...] = dr_ref[...] + r * (dxn - xn * jnp.mean(dxn * xn, axis=-1, keepdims=True))

    vec = jax.ShapeDtypeStruct((1, D), F32)
    return pl.pallas_call(
        body, name=name, out_shape=(jax.ShapeDtypeStruct((S, D), F32), vec, vec, vec), grid=(S // ts,),
        in_specs=[_row_spec(ts, D), _row_spec(ts, D), _row_spec(ts, D), _vec_spec(D), _vec_spec(D)],
        out_specs=(_row_spec(ts, D), _vec_spec(D), _vec_spec(D), _vec_spec(D)),
        compiler_params=_params("arbitrary"),
    )(dh, x, dx_res, pre_g, scale)


def _post_fwd(x, y, gate, post_g, name):
    S, D = x.shape
    ts = min(256, S)

    def body(x_ref, y_ref, gt_ref, g_ref, o_ref):
        yv = y_ref[...]
        r = lax.rsqrt(jnp.mean(yv * yv, axis=-1, keepdims=True) + NORM_EPS)
        o_ref[...] = x_ref[...] + gt_ref[...] * (yv * r * g_ref[...])

    return pl.pallas_call(
        body, name=name, out_shape=jax.ShapeDtypeStruct((S, D), F32), grid=(S // ts,),
        in_specs=[_row_spec(ts, D), _row_spec(ts, D), _vec_spec(D), _vec_spec(D)],
        out_specs=_row_spec(ts, D), compiler_params=_params("parallel"),
    )(x, y, gate, post_g)


def _post_bwd(dx, y, gate, post_g, name):
    S, D = y.shape
    ts = min(256, S)

    def body(dx_ref, y_ref, gt_ref, g_ref, dy_ref, dgt_ref, dg_ref):
        @pl.when(pl.program_id(0) == 0)
        def _():
            dgt_ref[...] = jnp.zeros_like(dgt_ref)
            dg_ref[...] = jnp.zeros_like(dg_ref)

        dxv = dx_ref[...]
        yv = y_ref[...]
        g = g_ref[...]
        gt = gt_ref[...]
        r = lax.rsqrt(jnp.mean(yv * yv, axis=-1, keepdims=True) + NORM_EPS)
        yn = yv * r
        dgt_ref[...] += jnp.sum(dxv * (yn * g), axis=0, keepdims=True)
        dg_ref[...] += jnp.sum(dxv * gt * yn, axis=0, keepdims=True)
        dyn = dxv * gt * g
        dy_ref[...] = (r * (dyn - yn * jnp.mean(dyn * yn, axis=-1, keepdims=True))).astype(BF16)

    vec = jax.ShapeDtypeStruct((1, D), F32)
    return pl.pallas_call(
        body, name=name, out_shape=(jax.ShapeDtypeStruct((S, D), BF16), vec, vec), grid=(S // ts,),
        in_specs=[_row_spec(ts, D), _row_spec(ts, D), _vec_spec(D), _vec_spec(D)],
        out_specs=(_row_spec(ts, D), _vec_spec(D), _vec_spec(D)),
        compiler_params=_params("arbitrary"),
    )(dx, y, gate, post_g)


def _loss_grad(xf, target, name):
    S, D = xf.shape
    ts = min(256, S)

    def body(x_ref, t_ref, dx_ref, l_ref):
        @pl.when(pl.program_id(0) == 0)
        def _():
            l_ref[...] = jnp.zeros_like(l_ref)

        e = x_ref[...] - t_ref[...]
        dx_ref[...] = e * (1.0 / D)
        row = jnp.sum(e * e, axis=1, keepdims=True) * (1.0 / D)
        l_ref[...] += 0.5 * jnp.sum(row, axis=0, keepdims=True)

    return pl.pallas_call(
        body, name=name,
        out_shape=(jax.ShapeDtypeStruct((S, D), F32), jax.ShapeDtypeStruct((1, 1), F32)), grid=(S // ts,),
        in_specs=[_row_spec(ts, D), _row_spec(ts, D)],
        out_specs=(_row_spec(ts, D), pl.BlockSpec((1, 1), lambda i: (0, 0))),
        compiler_params=_params("arbitrary"),
    )(xf, target)


def _chunk_mask(transposed=False):
    row = lax.broadcasted_iota(jnp.int32, (SGU_BLOCK, SGU_BLOCK), 0) // CHUNK
    col = lax.broadcasted_iota(jnp.int32, (SGU_BLOCK, SGU_BLOCK), 1) // CHUNK
    return (row <= col) if transposed else (col <= row)


def _sgu_gate_fwd(uvz, norm_g, w_s, b_bc, name):
    S, E3 = uvz.shape
    E = E3 // 3
    T = SGU_BLOCK
    gd = E // SGU_GROUPS

    def body(uvz_ref, ng_ref, ws_ref, bb_ref, y_ref, v_scr):
        gv = _gelu(uvz_ref[:, E:2 * E])
        mu = jnp.mean(gv, axis=-1, keepdims=True)
        xc = gv - mu
        rstd = lax.rsqrt(jnp.mean(xc * xc, axis=-1, keepdims=True) + NORM_EPS)
        v_scr[...] = (xc * rstd * ng_ref[...]).astype(BF16)
        mask = _chunk_mask()
        for g in range(SGU_GROUPS):
            sl = slice(g * gd, (g + 1) * gd)
            wg = jnp.where(mask, ws_ref[g], 0.0).astype(BF16)
            vm = lax.dot_general(wg, v_scr[:, sl], NN, preferred_element_type=F32)
            vm = vm + jnp.tile(bb_ref[g], (1, gd // LANE))
            z = uvz_ref[:, 2 * E + g * gd:2 * E + (g + 1) * gd]
            y_ref[:, sl] = (_gelu(uvz_ref[:, sl]) * vm * (z * jax.nn.sigmoid(z))).astype(BF16)

    return pl.pallas_call(
        body, name=name, out_shape=jax.ShapeDtypeStruct((S, E), BF16), grid=(S // T,),
        in_specs=[_row_spec(T, E3), _vec_spec(E),
                  pl.BlockSpec((SGU_GROUPS, T, T), lambda i: (0, 0, 0)),
                  pl.BlockSpec((SGU_GROUPS, T, LANE), lambda i: (0, 0, 0))],
        out_specs=_row_spec(T, E),
        scratch_shapes=[pltpu.VMEM((T, E), BF16)],
        compiler_params=_params("parallel"),
    )(uvz, norm_g, w_s, b_bc)


def _sgu_gate_bwd(uvz, dyv, norm_g, w_s, w_sT, b_bc, name):
    S, E3 = uvz.shape
    E = E3 // 3
    T = SGU_BLOCK
    gd = E // SGU_GROUPS

    def body(uvz_ref, dyv_ref, ng_ref, ws_ref, wst_ref, bb_ref,
             d_ref, dng_ref, dws_ref, dbs_ref, vhat_scr, dv_scr):
        @pl.when(pl.program_id(0) == 0)
        def _():
            dng_ref[...] = jnp.zeros_like(dng_ref)
            dws_ref[...] = jnp.zeros_like(dws_ref)
            dbs_ref[...] = jnp.zeros_like(dbs_ref)

        gv = _gelu(uvz_ref[:, E:2 * E])
        mu = jnp.mean(gv, axis=-1, keepdims=True)
        xc = gv - mu
        rstd = lax.rsqrt(jnp.mean(xc * xc, axis=-1, keepdims=True) + NORM_EPS)
        vhat_scr[...] = xc * rstd
        mask = _chunk_mask()
        mask_t = _chunk_mask(transposed=True)
        for g in range(SGU_GROUPS):
            sl = slice(g * gd, (g + 1) * gd)
            u_pre = uvz_ref[:, sl]
            z = uvz_ref[:, 2 * E + g * gd:2 * E + (g + 1) * gd]
            dy = dyv_ref[:, sl]
            u = _gelu(u_pre)
            sig = jax.nn.sigmoid(z)
            sz = z * sig
            vg = (vhat_scr[:, sl] * ng_ref[:, sl]).astype(BF16)
            wg = jnp.where(mask, ws_ref[g], 0.0).astype(BF16)
            vm = lax.dot_general(wg, vg, NN, preferred_element_type=F32)
            vm = vm + jnp.tile(bb_ref[g], (1, gd // LANE))
            dy_u = dy * u
            d_ref[:, sl] = (dy * vm * sz * _gelu_grad(u_pre)).astype(BF16)
            d_ref[:, 2 * E + g * gd:2 * E + (g + 1) * gd] = (
                dy_u * vm * (sig * (1.0 + z * (1.0 - sig)))).astype(BF16)
            dvm = dy_u * sz
            dvm_b = dvm.astype(BF16)
            dws_ref[g] += jnp.where(mask, lax.dot_general(dvm_b, vg, NT, preferred_element_type=F32), 0.0)
            dbs_ref[g] += jnp.broadcast_to(jnp.sum(dvm, axis=1, keepdims=True), (T, LANE))
            wgt = jnp.where(mask_t, wst_ref[g], 0.0).astype(BF16)
            dv_scr[:, sl] = lax.dot_general(wgt, dvm_b, NN, preferred_element_type=F32)
        dv = dv_scr[...]
        vhat = vhat_scr[...]
        dng_ref[...] += jnp.sum(dv * vhat, axis=0, keepdims=True)
        dvh = dv * ng_ref[...]
        dgv = rstd * (dvh - jnp.mean(dvh, axis=-1, keepdims=True)
                      - vhat * jnp.mean(dvh * vhat, axis=-1, keepdims=True))
        d_ref[:, E:2 * E] = (dgv * _gelu_grad(uvz_ref[:, E:2 * E])).astype(BF16)

    wspec = pl.BlockSpec((SGU_GROUPS, T, T), lambda i: (0, 0, 0))
    bspec = pl.BlockSpec((SGU_GROUPS, T, LANE), lambda i: (0, 0, 0))
    return pl.pallas_call(
        body, name=name,
        out_shape=(jax.ShapeDtypeStruct((S, E3), BF16), jax.ShapeDtypeStruct((1, E), F32),
                   jax.ShapeDtypeStruct((SGU_GROUPS, T, T), F32),
                   jax.ShapeDtypeStruct((SGU_GROUPS, T, LANE), F32)),
        grid=(S // T,),
        in_specs=[_row_spec(T, E3), _row_spec(T, E), _vec_spec(E), wspec, wspec, bspec],
        out_specs=(_row_spec(T, E3), _vec_spec(E), wspec, bspec),
        scratch_shapes=[pltpu.VMEM((T, E), F32), pltpu.VMEM((T, E), F32)],
        compiler_params=_params("arbitrary"),
    )(uvz, dyv, norm_g, w_s, w_sT, b_bc)


MLA_WIDTH = HEADS * VDIM
P_LATENT = Q_RANK + KV_RANK + ROPE
P_WIDTH = P_LATENT + MLA_WIDTH


def _swap_halves(v):
    lane = lax.broadcasted_iota(jnp.int32, v.shape, 1)
    return jnp.where(lane % ROPE < ROPE // 2, pltpu.roll(v, LANE - ROPE // 2, 1), pltpu.roll(v, ROPE // 2, 1))


def _low_lanes(rows):
    return lax.broadcasted_iota(jnp.int32, (rows, LANE), 1) < ROPE


def _latent_tiles(ref):
    return [ref[:, t * LANE:(t + 1) * LANE] for t in range(P_LATENT // LANE)]


def _split_latents(tiles, low):
    cq = jnp.concatenate(tiles[0:3] + [jnp.where(low, tiles[3], 0.0)], axis=1)
    rolled = [pltpu.roll(t, ROPE, 1) for t in tiles[3:8]]
    ckv = jnp.concatenate([jnp.where(low, rolled[t], rolled[t + 1]) for t in range(4)], axis=1)
    kr = jnp.where(low, rolled[4], 0.0)
    return cq, ckv, kr


def _mla_mid_fwd(p, qg, kvg, name):
    S, PW = p.shape
    ts = min(256, S)

    def body(p_ref, qg_ref, kvg_ref, cqn_ref, ckvn_ref):
        cq, ckv, _ = _split_latents(_latent_tiles(p_ref), _low_lanes(ts))
        r = lax.rsqrt(jnp.sum(cq * cq, axis=-1, keepdims=True) * (1.0 / Q_RANK) + NORM_EPS)
        cqn_ref[...] = (cq * r * qg_ref[...]).astype(BF16)
        r2 = lax.rsqrt(jnp.mean(ckv * ckv, axis=-1, keepdims=True) + NORM_EPS)
        ckvn_ref[...] = (ckv * r2 * kvg_ref[...]).astype(BF16)

    return pl.pallas_call(
        body, name=name,
        out_shape=(jax.ShapeDtypeStruct((S, Q_RANK_PAD), BF16), jax.ShapeDtypeStruct((S, KV_RANK), BF16)),
        grid=(S // ts,),
        in_specs=[_row_spec(ts, P_LATENT), _vec_spec(Q_RANK_PAD), _vec_spec(KV_RANK)],
        out_specs=(_row_spec(ts, Q_RANK_PAD), _row_spec(ts, KV_RANK)),
        compiler_params=_params("parallel"),
    )(p, qg, kvg)


def _mla_pack(q, kv, p, cos_t, sin_t, name):
    S = q.shape[0]
    ts = min(256, S)
    pair_w = 2 * (NOPE + ROPE)
    head_w = NOPE + VDIM

    def body(q_ref, kv_ref, kr_ref, cos_ref, sin_ref, qo_ref, ko_ref, vo_ref, vt_ref):
        cosv = cos_ref[...]
        sinv = sin_ref[...]
        low = _low_lanes(ts)
        kr = jnp.where(low, pltpu.roll(kr_ref[...], ROPE, 1), 0.0)
        kr = (kr * cosv + _swap_halves(kr) * sinv).astype(BF16)
        for pair in range(HEADS // 2):
            t0, t1, t2 = (q_ref[:, pair * pair_w + t * LANE:pair * pair_w + (t + 1) * LANE] for t in range(3))
            nope_b = jnp.where(low, pltpu.roll(t1, ROPE, 1), pltpu.roll(t2, ROPE, 1))
            ropes = jnp.where(low, t1, t2)
            roped = (ropes * cosv + _swap_halves(ropes) * sinv) * Q_FOLD
            qo_ref[2 * pair, :, 0:NOPE] = (t0 * Q_FOLD).astype(BF16)
            qo_ref[2 * pair, :, NOPE:QK_PAD] = jnp.where(low, roped, 0.0).astype(BF16)
            qo_ref[2 * pair + 1, :, 0:NOPE] = (nope_b * Q_FOLD).astype(BF16)
            qo_ref[2 * pair + 1, :, NOPE:QK_PAD] = jnp.where(low, pltpu.roll(roped, ROPE, 1), 0.0).astype(BF16)
        eye = _identity(VDIM)
        for h in range(HEADS):
            ko_ref[h, :, 0:NOPE] = kv_ref[:, h * head_w:h * head_w + NOPE].astype(BF16)
            ko_ref[h, :, NOPE:QK_PAD] = kr
            vh = kv_ref[:, h * head_w + NOPE:(h + 1) * head_w].astype(BF16)
            vo_ref[h] = vh
            vt_ref[h] = lax.dot_general(eye, vh, NT, preferred_element_type=F32).astype(BF16)

    T = min(ATTN_TILE, S)
    per_tile = T // ts
    return pl.pallas_call(
        body, name=name,
        out_shape=(jax.ShapeDtypeStruct((HEADS, S, QK_PAD), BF16), jax.ShapeDtypeStruct((HEADS, S, QK_PAD), BF16),
                   jax.ShapeDtypeStruct((HEADS, S, VDIM), BF16), jax.ShapeDtypeStruct((HEADS, S // T, VDIM, T), BF16)),
        grid=(S // ts,),
        in_specs=[_row_spec(ts, q.shape[1]), _row_spec(ts, kv.shape[1]),
                  pl.BlockSpec((ts, LANE), lambda i: (i, P_LATENT // LANE - 1)),
                  _row_spec(ts, LANE), _row_spec(ts, LANE)],
        out_specs=(pl.BlockSpec((HEADS, ts, QK_PAD), lambda i: (0, i, 0)),
                   pl.BlockSpec((HEADS, ts, QK_PAD), lambda i: (0, i, 0)),
                   pl.BlockSpec((HEADS, ts, VDIM), lambda i: (0, i, 0)),
                   pl.BlockSpec((HEADS, None, VDIM, ts), lambda i: (0, i // per_tile, 0, i % per_tile))),
        compiler_params=_params("parallel"),
    )(q, kv, p, cos_t, sin_t)


def _mla_unpack(dQ, dK, dV, cos_t, sin_t, name):
    S = dQ.shape[1]
    ts = min(256, S)
    pair_w = 2 * (NOPE + ROPE)
    head_w = NOPE + VDIM

    def body(dq_ref, dk_ref, dv_ref, cos_ref, sin_ref, q_ref, kv_ref, kr_ref):
        cosv = cos_ref[...]
        sinv = sin_ref[...]
        low = _low_lanes(ts)
        for pair in range(HEADS // 2):
            blk = dq_ref[2 * pair, :, NOPE:QK_PAD] + pltpu.roll(dq_ref[2 * pair + 1, :, NOPE:QK_PAD], ROPE, 1)
            ropes = blk * cosv - _swap_halves(blk) * sinv
            nope_b = pltpu.roll(dq_ref[2 * pair + 1, :, 0:NOPE], ROPE, 1)
            q_ref[:, pair * pair_w:pair * pair_w + LANE] = dq_ref[2 * pair, :, 0:NOPE].astype(BF16)
            q_ref[:, pair * pair_w + LANE:pair * pair_w + 2 * LANE] = jnp.where(low, ropes, nope_b).astype(BF16)
            q_ref[:, pair * pair_w + 2 * LANE:(pair + 1) * pair_w] = jnp.where(low, nope_b, ropes).astype(BF16)
        dkr = dk_ref[0, :, NOPE:QK_PAD]
        for h in range(1, HEADS):
            dkr = dkr + dk_ref[h, :, NOPE:QK_PAD]
        kr_ref[...] = dkr * cosv - _swap_halves(dkr) * sinv
        for h in range(HEADS):
            kv_ref[:, h * head_w:h * head_w + NOPE] = dk_ref[h, :, 0:NOPE].astype(BF16)
            kv_ref[:, h * head_w + NOPE:(h + 1) * head_w] = dv_ref[h].astype(BF16)

    return pl.pallas_call(
        body, name=name,
        out_shape=(jax.ShapeDtypeStruct((S, HEADS * (NOPE + ROPE)), BF16),
                   jax.ShapeDtypeStruct((S, HEADS * (NOPE + VDIM)), BF16),
                   jax.ShapeDtypeStruct((S, LANE), F32)),
        grid=(S // ts,),
        in_specs=[pl.BlockSpec((HEADS, ts, QK_PAD), lambda i: (0, i, 0)),
                  pl.BlockSpec((HEADS, ts, QK_PAD), lambda i: (0, i, 0)),
                  pl.BlockSpec((HEADS, ts, VDIM), lambda i: (0, i, 0)),
                  _row_spec(ts, LANE), _row_spec(ts, LANE)],
        out_specs=(_row_spec(ts, HEADS * (NOPE + ROPE)), _row_spec(ts, HEADS * (NOPE + VDIM)),
                   _row_spec(ts, LANE)),
        compiler_params=_params("parallel"),
    )(dQ, dK, dV, cos_t, sin_t)


def _mla_gate_fwd(o, p, name):
    S, W = o.shape
    ts = min(256, S)
    wb = P_LATENT

    def body(o_ref, z_ref, y_ref):
        z = z_ref[...]
        y_ref[...] = (o_ref[...] * (z * jax.nn.sigmoid(z))).astype(BF16)

    return pl.pallas_call(
        body, name=name, out_shape=jax.ShapeDtypeStruct((S, W), BF16), grid=(S // ts, W // wb),
        in_specs=[pl.BlockSpec((ts, wb), lambda i, j: (i, j)), pl.BlockSpec((ts, wb), lambda i, j: (i, j + 1))],
        out_specs=pl.BlockSpec((ts, wb), lambda i, j: (i, j)), compiler_params=_params("parallel", "parallel"),
    )(o, p)


def _mla_gate_bwd(dyv, p, name):
    S, W = dyv.shape
    ts = min(256, S)
    wb = P_LATENT

    def body(d_ref, z_ref, do_ref):
        z = z_ref[...]
        do_ref[...] = d_ref[...] * (z * jax.nn.sigmoid(z))

    return pl.pallas_call(
        body, name=name, out_shape=jax.ShapeDtypeStruct((S, W), F32), grid=(S // ts, W // wb),
        in_specs=[pl.BlockSpec((ts, wb), lambda i, j: (i, j)), pl.BlockSpec((ts, wb), lambda i, j: (i, j + 1))],
        out_specs=pl.BlockSpec((ts, wb), lambda i, j: (i, j)), compiler_params=_params("parallel", "parallel"),
    )(dyv, p)


def _mla_mid_bwd(p, dcqn, dckvn, dkr, dyv, o, qg, kvg, name):
    S, PW = p.shape
    W = o.shape[1]
    ts = min(256, S)
    nt = Q_RANK_PAD // LANE

    def rms_bwd(xv, dy, g, count):
        r = lax.rsqrt(jnp.sum(xv * xv, axis=-1, keepdims=True) * (1.0 / count) + NORM_EPS)
        xn = xv * r
        dg = jnp.sum(dy * xn, axis=0, keepdims=True)
        dxn = dy * g
        dx = r * (dxn - xn * (jnp.sum(dxn * xn, axis=-1, keepdims=True) * (1.0 / count)))
        return dx, dg

    def body(p_ref, dcq_ref, dckv_ref, dkr_ref, dyv_ref, o_ref, qg_ref, kvg_ref, dp_ref, dqg_ref, dkvg_ref):
        @pl.when(pl.program_id(0) == 0)
        def _():
            dqg_ref[...] = jnp.zeros_like(dqg_ref)
            dkvg_ref[...] = jnp.zeros_like(dkvg_ref)

        low = _low_lanes(ts)
        cq, ckv, _ = _split_latents(_latent_tiles(p_ref), low)
        dcq, dg = rms_bwd(cq, dcq_ref[...], qg_ref[...], Q_RANK)
        dqg_ref[...] += dg
        dckv, dg = rms_bwd(ckv, dckv_ref[...], kvg_ref[...], KV_RANK)
        dkvg_ref[...] += dg
        moved = [pltpu.roll(dckv[:, t * LANE:(t + 1) * LANE], ROPE, 1) for t in range(nt)]
        moved.append(pltpu.roll(dkr_ref[...], ROPE, 1))
        for t in range(nt - 1):
            dp_ref[:, t * LANE:(t + 1) * LANE] = dcq[:, t * LANE:(t + 1) * LANE].astype(BF16)
        dp_ref[:, (nt - 1) * LANE:nt * LANE] = jnp.where(low, dcq[:, (nt - 1) * LANE:nt * LANE], moved[0]).astype(BF16)
        for t in range(nt):
            dp_ref[:, (nt + t) * LANE:(nt + t + 1) * LANE] = jnp.where(low, moved[t], moved[t + 1]).astype(BF16)
        z = p_ref[:, P_LATENT:PW]
        sig = jax.nn.sigmoid(z)
        dp_ref[:, P_LATENT:PW] = (dyv_ref[...] * o_ref[...] * (sig * (1.0 + z * (1.0 - sig)))).astype(BF16)

    return pl.pallas_call(
        body, name=name,
        out_shape=(jax.ShapeDtypeStruct((S, PW), BF16), jax.ShapeDtypeStruct((1, Q_RANK_PAD), F32),
                   jax.ShapeDtypeStruct((1, KV_RANK), F32)),
        grid=(S // ts,),
        in_specs=[_row_spec(ts, PW), _row_spec(ts, Q_RANK_PAD), _row_spec(ts, KV_RANK), _row_spec(ts, LANE),
                  _row_spec(ts, W), _row_spec(ts, W), _vec_spec(Q_RANK_PAD), _vec_spec(KV_RANK)],
        out_specs=(_row_spec(ts, PW), _vec_spec(Q_RANK_PAD), _vec_spec(KV_RANK)),
        compiler_params=_params("arbitrary"),
    )(p, dcqn, dckvn, dkr, dyv, o, qg, kvg)


def _tile_mask(T):
    row = lax.broadcasted_iota(jnp.int32, (T, T), 0) // CHUNK
    col = lax.broadcasted_iota(jnp.int32, (T, T), 1) // CHUNK
    return col <= row


def _attn_bwd(Q, K, V, o, do, lse, name):
    H, S, _ = Q.shape
    T = min(ATTN_TILE, S)
    nq = S // T

    def body(q_ref, k_ref, v_ref, o_ref, do_ref, lse_ref, dq_ref, dk_ref, dv_ref, dk_scr, dv_scr, s_scr, dp_scr):
        ki = pl.program_id(1)

        @pl.when(ki == 0)
        def _():
            dq_ref[...] = jnp.zeros_like(dq_ref)

        dk_scr[...] = jnp.zeros_like(dk_scr)
        dv_scr[...] = jnp.zeros_like(dv_scr)
        k = k_ref[...]
        v = v_ref[...]

        def scores(i):
            rows = pl.ds(pl.multiple_of(i * T, T), T)
            s = lax.dot_general(q_ref[rows, :], k, NT, preferred_element_type=F32)
            dp = lax.dot_general(do_ref[rows, :].astype(BF16), v, NT, preferred_element_type=F32)
            return s, dp

        def grads(i, s, dp, masked):
            rows = pl.ds(pl.multiple_of(i * T, T), T)
            do_f = do_ref[rows, :]
            delta = jnp.sum(do_f * o_ref[rows, :], axis=1, keepdims=True)
            pr = jnp.exp2(s - lse_ref[rows, 0:1])
            if masked:
                pr = jnp.where(_tile_mask(T), pr, 0.0)
            dv_scr[...] += lax.dot_general(pr.astype(BF16), do_f.astype(BF16), TN, preferred_element_type=F32)
            ds = (pr * (dp - delta)).astype(BF16)
            dk_scr[...] += lax.dot_general(ds, q_ref[rows, :], TN, preferred_element_type=F32)
            dq_ref[rows, :] += lax.dot_general(ds, k, NN, preferred_element_type=F32) * ATTN_SCALE

        s_scr[...], dp_scr[...] = scores(ki)

        @pl.when(ki + 1 == nq)
        def _():
            grads(ki, s_scr[...], dp_scr[...], True)

        @pl.when(ki + 1 < nq)
        def _():
            def step(i, masked):
                nxt_s, nxt_dp = scores(i + 1)
                grads(i, s_scr[...], dp_scr[...], masked)
                s_scr[...] = nxt_s
                dp_scr[...] = nxt_dp

            def loop_step(i, carry):
                step(i, False)
                return carry

            step(ki, True)
            lax.fori_loop(ki + 1, nq - 1, loop_step, 0)
            grads(nq - 1, s_scr[...], dp_scr[...], False)

        dk_ref[...] = dk_scr[...] * (1.0 / LOG2E)
        dv_ref[...] = dv_scr[...]

    return pl.pallas_call(
        body, name=name,
        out_shape=(jax.ShapeDtypeStruct((H, S, QK_PAD), F32), jax.ShapeDtypeStruct((H, S, QK_PAD), F32),
                   jax.ShapeDtypeStruct((H, S, VDIM), F32)),
        grid=(H, nq),
        in_specs=[pl.BlockSpec((None, S, QK_PAD), lambda h, j: (h, 0, 0)),
                  pl.BlockSpec((None, T, QK_PAD), lambda h, j: (h, j, 0)),
                  pl.BlockSpec((None, T, VDIM), lambda h, j: (h, j, 0)),
                  pl.BlockSpec((S, VDIM), lambda h, j: (0, h)),
                  pl.BlockSpec((S, VDIM), lambda h, j: (0, h)),
                  pl.BlockSpec((None, S, LANE), lambda h, j: (h, 0, 0))],
        out_specs=(pl.BlockSpec((None, S, QK_PAD), lambda h, j: (h, 0, 0)),
                   pl.BlockSpec((None, T, QK_PAD), lambda h, j: (h, j, 0)),
                   pl.BlockSpec((None, T, VDIM), lambda h, j: (h, j, 0))),
        scratch_shapes=[pltpu.VMEM((T, QK_PAD), F32), pltpu.VMEM((T, VDIM), F32),
                        pltpu.VMEM((T, T), F32), pltpu.VMEM((T, T), F32)],
        compiler_params=_params("parallel", "arbitrary"),
    )(Q, K, V, o, do, lse)


def _identity(n):
    return (lax.broadcasted_iota(jnp.int32, (n, n), 0) == lax.broadcasted_iota(jnp.int32, (n, n), 1)).astype(BF16)


def _key_le_query(rows, cols, col0):
    key = lax.broadcasted_iota(jnp.int32, (rows, cols), 0) // CHUNK
    query = (lax.broadcasted_iota(jnp.int32, (rows, cols), 1) + col0) // CHUNK
    return key <= query


def _attn_fwd_t(Q, K, VT, name):
    H, nT, _, T = VT.shape
    S = nT * T
    n_part = 2 if T % 256 == 0 else 1
    Tq = T // n_part

    def body(q_ref, k_ref, vt_ref, o_ref, lse_ref, m_scr, l_scr, acc_scr, s_scr):
        qi = pl.program_id(1)
        m_scr[...] = jnp.full_like(m_scr, -jnp.inf)
        l_scr[...] = jnp.zeros_like(l_scr)
        acc_scr[...] = jnp.zeros_like(acc_scr)

        def scores(j):
            kt = k_ref[pl.ds(pl.multiple_of(j * T, T), T), :]
            return lax.dot_general(kt, q_ref[...], NT, preferred_element_type=F32)

        def softmax_step(j, masked):
            vt = vt_ref[j]
            for part in range(n_part):
                sub = slice(part * Tq, (part + 1) * Tq)
                st = s_scr[:, sub]
                if masked:
                    st = jnp.where(_key_le_query(T, Tq, part * Tq), st, -1e30)
                m_prev = m_scr[:, sub]
                m_new = jnp.maximum(m_prev, jnp.max(st, axis=0, keepdims=True))
                pt = jnp.exp2(st - m_new)
                alpha = jnp.exp2(m_prev - m_new)
                l_scr[:, sub] = alpha * l_scr[:, sub] + jnp.sum(pt, axis=0, keepdims=True)
                acc_scr[:, sub] = alpha * acc_scr[:, sub] + lax.dot_general(
                    vt, pt.astype(BF16), NN, preferred_element_type=F32)
                m_scr[:, sub] = m_new

        s_scr[...] = scores(0)

        def step(j, carry):
            nxt = scores(j + 1)
            softmax_step(j, False)
            s_scr[...] = nxt
            return carry

        lax.fori_loop(0, qi, step, 0)
        softmax_step(qi, True)
        l = l_scr[...]
        o_ref[...] = jnp.transpose(acc_scr[...] / l)
        lse_ref[...] = jnp.transpose(jnp.broadcast_to(m_scr[...] + jnp.log2(l), (LANE, T)))

    return pl.pallas_call(
        body, name=name,
        out_shape=(jax.ShapeDtypeStruct((S, H * VDIM), F32), jax.ShapeDtypeStruct((H, S, LANE), F32)),
        grid=(H, nT),
        in_specs=[pl.BlockSpec((None, T, QK_PAD), lambda h, i: (h, i, 0)),
                  pl.BlockSpec((None, S, QK_PAD), lambda h, i: (h, 0, 0)),
                  pl.BlockSpec((None, nT, VDIM, T), lambda h, i: (h, 0, 0, 0))],
        out_specs=(pl.BlockSpec((T, VDIM), lambda h, i: (i, h)),
                   pl.BlockSpec((None, T, LANE), lambda h, i: (h, i, 0))),
        scratch_shapes=[pltpu.VMEM((1, T), F32), pltpu.VMEM((1, T), F32), pltpu.VMEM((VDIM, T), F32),
                        pltpu.VMEM((T, T), F32)],
        compiler_params=_params("parallel", "arbitrary"),
    )(Q, K, VT)


def _adamw(w, g, m, v, name):
    shape = w.shape
    C = shape[-1]
    R = math.prod(shape[:-1])
    flat = [t.reshape(R, C) for t in (w, g, m, v)]
    tr = _row_tile(R, C * 4)

    def body(w_ref, g_ref, m_ref, v_ref, d_ref, nm_ref, nv_ref):
        gv = g_ref[...]
        m_new = ADAM_B1 * m_ref[...] + (1.0 - ADAM_B1) * gv
        v_new = ADAM_B2 * v_ref[...] + (1.0 - ADAM_B2) * jnp.square(gv)
        m_hat = m_new / (1.0 - ADAM_B1 ** ADAM_STEP)
        v_hat = v_new / (1.0 - ADAM_B2 ** ADAM_STEP)
        d_ref[...] = -ADAM_LR * (m_hat / (jnp.sqrt(v_hat) + ADAM_EPS) + ADAM_WD * w_ref[...])
        nm_ref[...] = m_new
        nv_ref[...] = v_new

    spec = pl.BlockSpec((tr, C), lambda i: (i, 0))
    out = jax.ShapeDtypeStruct((R, C), F32)
    d, nm, nv = pl.pallas_call(
        body, name=name, out_shape=(out, out, out), grid=(R // tr,),
        in_specs=[spec] * 4, out_specs=(spec, spec, spec), compiler_params=_params("parallel"),
    )(*flat)
    return d.reshape(shape), nm.reshape(shape), nv.reshape(shape)


def _sum_into_half(r, buf, layer, ci, n_layers, name):
    n, M, N = r.shape
    tr = _row_tile(M, N * 4 * n, 4 << 20)

    def body(c_ref, r_ref, *rest):
        o_ref = rest[-1]
        acc = r_ref[0].astype(F32)
        for s in range(1, n):
            acc = acc + r_ref[s].astype(F32)
        o_ref[...] = acc

    in_specs = [pl.BlockSpec((n, tr, N), lambda i, c: (0, i, 0))]
    operands = [ci.reshape(1), r]
    aliases = {}
    if buf is not None:
        in_specs.append(ANY)
        operands.append(buf)
        aliases = {2: 0}
    return pl.pallas_call(
        body, name=name, out_shape=jax.ShapeDtypeStruct((n_layers, 2, M, N), F32),
        grid_spec=pltpu.PrefetchScalarGridSpec(
            num_scalar_prefetch=1, grid=(M // tr,), in_specs=in_specs,
            out_specs=pl.BlockSpec((None, None, tr, N), lambda i, c: (layer, c[0], i, 0))),
        input_output_aliases=aliases, compiler_params=_params("parallel"),
    )(*operands)


def _sum_slots(r, name):
    n, M, N = r.shape
    tr = _row_tile(M, N * 4 * n, 4 << 20)

    def body(r_ref, o_ref):
        acc = r_ref[0].astype(F32)
        for s in range(1, n):
            acc = acc + r_ref[s].astype(F32)
        o_ref[...] = acc

    return pl.pallas_call(
        body, name=name, out_shape=jax.ShapeDtypeStruct((M, N), F32), grid=(M // tr,),
        in_specs=[pl.BlockSpec((n, tr, N), lambda i: (0, i, 0))],
        out_specs=pl.BlockSpec((tr, N), lambda i: (i, 0)), compiler_params=_params("parallel"),
    )(r)


ANY = pl.BlockSpec(memory_space=pl.ANY)
DMA_CHUNK_BYTES = 1 << 20
DMA_MAX_CHUNKS = 16
PEER_ORDER = (1, 4, 5, 2, 3, 6, 7)


def _position():
    return lax.axis_index("x"), lax.axis_index("y"), lax.axis_index("c")


def _row_chunks(shape, dtype):
    rows, cols = shape
    n = max(1, min(DMA_MAX_CHUNKS, rows * cols * jnp.dtype(dtype).itemsize // DMA_CHUNK_BYTES))
    while n > 1 and (rows % n or (rows // n) % 16):
        n -= 1
    step = rows // n
    return [pl.ds(q * step, step) for q in range(n)]


def _all_gather8(xs, name):
    n = len(xs)

    def body(*refs):
        x_refs, o_refs = refs[:n], refs[n:2 * n]
        send_sems, recv_sems, local_sems = refs[2 * n:]
        x, y, c = _position()
        me, sibling = (x, y, c), (x, y, 1 - c)
        chips = [(1 - x, y), (x, 1 - y), (1 - x, 1 - y)]

        def slot(a, dev, rows):
            return o_refs[a].at[4 * dev[0] + 2 * dev[1] + dev[2], rows]

        def copy(a, k, block, to, rows, from_input=False):
            return pltpu.make_async_remote_copy(
                src_ref=x_refs[a].at[rows] if from_input else slot(a, block, rows), dst_ref=slot(a, block, rows),
                send_sem=send_sems.at[a, k], recv_sem=recv_sems.at[a, k],
                device_id=to, device_id_type=MESH)

        def mine(a, rows):
            return pltpu.make_async_copy(x_refs[a].at[rows], slot(a, me, rows), local_sems.at[a])

        chunks = [_row_chunks(t.shape, t.dtype) for t in xs]
        whole = [pl.ds(0, t.shape[0]) for t in xs]
        for a in range(n):
            for rows in chunks[a]:
                mine(a, rows).start()
        sent = []
        for a in range(n):
            for k, to in enumerate([sibling] + [(*chip, c) for chip in chips]):
                for rows in chunks[a]:
                    copy(a, k, me, to, rows, from_input=True).start()
                sent.append(copy(a, k, me, to, whole[a], from_input=True))
        for a in range(n):
            for j, chip in enumerate(chips):
                copy(a, 1 + j, (*chip, c), me, whole[a]).wait_recv()
                for rows in chunks[a]:
                    copy(a, 4 + j, (*chip, c), sibling, rows).start()
                sent.append(copy(a, 4 + j, (*chip, c), sibling, whole[a]))
        for a in range(n):
            copy(a, 0, sibling, me, whole[a]).wait_recv()
            for j, chip in enumerate(chips):
                copy(a, 4 + j, (*chip, 1 - c), me, whole[a]).wait_recv()
        for cp in sent:
            cp.wait_send()
        for a in range(n):
            mine(a, whole[a]).wait()

    return pl.pallas_call(
        body, name=name,
        out_shape=[jax.ShapeDtypeStruct((8,) + t.shape, t.dtype) for t in xs],
        in_specs=[ANY] * n, out_specs=[ANY] * n,
        scratch_shapes=[pltpu.SemaphoreType.DMA((n, 7)), pltpu.SemaphoreType.DMA((n, 7)),
                        pltpu.SemaphoreType.DMA((n,))],
    )(*xs)


HBM = pl.BlockSpec(memory_space=pltpu.HBM)
SEM = pl.BlockSpec(memory_space=pltpu.SEMAPHORE)
EFFECT = pltpu.SideEffectType.DATAFLOW_SIDE_EFFECTING


def _peer(m, x, y, c):
    return ((1 - x) if m & 4 else x, (1 - y) if m & 2 else y, (1 - c) if m & 1 else c)


def _send_copies(src_refs, land_refs, send_sems, recv_sems, broadcast):
    x, y, c = _position()
    my = 4 * x + 2 * y + c
    out = []
    for a in range(len(src_refs)):
        for m in PEER_ORDER:
            px, py, pc = _peer(m, x, y, c)
            src = src_refs[a] if broadcast else src_refs[a].at[4 * px + 2 * py + pc]
            out.append(pltpu.make_async_remote_copy(
                src_ref=src, dst_ref=land_refs[a].at[my], send_sem=send_sems[a], recv_sem=recv_sems[a],
                device_id=(px, py, pc), device_id_type=MESH))
    return out


def _send_drain(land_refs, send_sems, recv_sems):
    x, y, c = _position()
    for a in range(len(land_refs)):
        seven = land_refs[a].at[pl.ds(0, 7)]
        both = pltpu.make_async_remote_copy(
            src_ref=seven, dst_ref=seven, send_sem=send_sems[a], recv_sem=recv_sems[a],
            device_id=(x, y, c), device_id_type=MESH)
        both.wait_send()
        both.wait_recv()


def _send_start(srcs, lands, after, broadcast, name):
    n = len(srcs)
    extra = [] if after is None else [after]

    def body(*refs):
        src_refs, land_refs = refs[:n], refs[n:2 * n]
        outs = refs[2 * n + len(extra):]
        send_sems, recv_sems = outs[:n], outs[n:2 * n]
        token = refs[-1]
        for cp in _send_copies(src_refs, land_refs, send_sems, recv_sems, broadcast):
            cp.start()
        token[...] = jnp.zeros_like(token)

    hbm = [pltpu.with_memory_space_constraint(t, pltpu.HBM) for t in list(srcs) + list(lands)]
    res = pl.pallas_call(
        body, name=name,
        out_shape=(*[pltpu.SemaphoreType.DMA(())] * (2 * n),
                   *[pltpu.HBM(t.shape, t.dtype) for t in hbm], jax.ShapeDtypeStruct((8, LANE), F32)),
        in_specs=[HBM] * (2 * n) + [ANY] * len(extra),
        out_specs=(*[SEM] * (2 * n), *[HBM] * (2 * n), pl.BlockSpec(memory_space=pltpu.VMEM)),
        input_output_aliases={i: 2 * n + i for i in range(2 * n)},
        compiler_params=pltpu.CompilerParams(has_side_effects=EFFECT),
    )(*hbm, *extra)
    return dict(sems=res[:2 * n], srcs=res[2 * n:3 * n], lands=res[3 * n:4 * n], token=res[-1], broadcast=broadcast)


def _send_wait(started, after, name):
    n = len(started["srcs"])

    def body(*refs):
        land_refs = refs[n:2 * n]
        send_sems, recv_sems = refs[2 * n:3 * n], refs[3 * n:4 * n]
        _send_drain(land_refs, send_sems, recv_sems)

    operands = list(started["srcs"]) + list(started["lands"])
    res = pl.pallas_call(
        body, name=name,
        out_shape=[pltpu.HBM(t.shape, t.dtype) for t in operands],
        in_specs=[HBM] * (2 * n) + [SEM] * (2 * n) + [ANY],
        out_specs=[HBM] * (2 * n),
        input_output_aliases={i: i for i in range(2 * n)},
        compiler_params=pltpu.CompilerParams(has_side_effects=EFFECT),
    )(*operands, *started["sems"], after)
    return res[n:]


def _own_slot(block, dev):
    zone = lax.empty((8,) + block.shape, block.dtype)
    return lax.dynamic_update_slice(zone, block[None], (dev, 0, 0))


def _pair_swap(bufs, name):
    n = len(bufs)
    pieces = [(a, l) for a, t in enumerate(bufs) for l in range(t.shape[0])]

    def body(*refs):
        b_refs = refs[n:2 * n]
        send_sems, recv_sems = refs[2 * n:]
        x, y, c = _position()

        def copy(k, rows):
            a, l = pieces[k]
            half = b_refs[a].at[l, c, rows]
            return pltpu.make_async_remote_copy(
                src_ref=half, dst_ref=half, send_sem=send_sems.at[k], recv_sem=recv_sems.at[k],
                device_id=(x, y, 1 - c), device_id_type=MESH)

        chunks = [_row_chunks(bufs[a].shape[2:], bufs[a].dtype) for a, _ in pieces]
        whole = [pl.ds(0, bufs[a].shape[2]) for a, _ in pieces]
        for k in range(len(pieces)):
            for rows in chunks[k]:
                copy(k, rows).start()
        for k in range(len(pieces)):
            copy(k, whole[k]).wait_recv()
        for k in range(len(pieces)):
            copy(k, whole[k]).wait_send()

    return pl.pallas_call(
        body, name=name,
        out_shape=[jax.ShapeDtypeStruct(t.shape, t.dtype) for t in bufs],
        in_specs=[ANY] * n, out_specs=[ANY] * n,
        input_output_aliases={a: a for a in range(n)},
        scratch_shapes=[pltpu.SemaphoreType.DMA((len(pieces),)), pltpu.SemaphoreType.DMA((len(pieces),))],
    )(*bufs)


def _pack_rows(parts):
    flat = jnp.concatenate([t.reshape(-1).astype(F32) for t in parts])
    pad = (-flat.shape[0]) % (256 * LANE)
    return jnp.pad(flat, (0, pad)).reshape(-1, LANE)


def _my_half(w2d, ci):
    half = w2d.shape[0] // 2
    return lax.dynamic_slice_in_dim(w2d, ci * half, half, axis=0).astype(BF16)


def _col_view(g):
    _, half, Cs = g.shape
    return g.reshape(4, 2 * half, Cs)


def _row_view(g):
    _, half, C = g.shape
    return g.reshape(8 * half, C)


def _rope_tables(S):
    pos = jnp.arange(S, dtype=F32)
    inv_freq = ROPE_THETA ** (-jnp.arange(0, ROPE, 2, dtype=F32) / ROPE)
    ang = pos[:, None] * inv_freq[None, :]
    cos, sin = jnp.cos(ang), jnp.sin(ang)
    cos_t = jnp.concatenate([cos, cos, cos, cos], axis=1)
    sin_t = jnp.concatenate([-sin, sin, -sin, sin], axis=1)
    return cos_t, sin_t


def kernel(x, c, ada_w, ada_b, pre_g, post_g, sgu_w_in, sgu_norm_g, sgu_w_s, sgu_b_s, sgu_w_out, mla_w_in, mla_q_norm_g, mla_kv_norm_g, mla_w_uq, mla_w_ukv, mla_w_out, loss_target, m_ada_w, m_ada_b, m_pre_g, m_post_g, m_sgu_w_in, m_sgu_norm_g, m_sgu_w_s, m_sgu_b_s, m_sgu_w_out, m_mla_w_in, m_mla_q_norm_g, m_mla_kv_norm_g, m_mla_w_uq, m_mla_w_ukv, m_mla_w_out, v_ada_w, v_ada_b, v_pre_g, v_post_g, v_sgu_w_in, v_sgu_norm_g, v_sgu_w_s, v_sgu_b_s, v_sgu_w_out, v_mla_w_in, v_mla_q_norm_g, v_mla_kv_norm_g, v_mla_w_uq, v_mla_w_ukv, v_mla_w_out):
    S, D = x.shape[1], x.shape[2]
    depth = ada_w.shape[0]
    E = sgu_w_out.shape[1] * 4
    xi, yi, ci = _position()
    chip = 2 * xi + yi
    dev = 4 * xi + 2 * yi + ci
    x0 = x.reshape(S, D)
    target = loss_target.reshape(S, D)

    small = _pack_rows([c, mla_q_norm_g, mla_kv_norm_g])
    mixer_w = dict(sin=sgu_w_in, sout=sgu_w_out, min=mla_w_in, uq=mla_w_uq, ukv=mla_w_ukv, mout=mla_w_out)
    small_g, first_g = _all_gather8([small, _my_half(sgu_w_in[0], ci)], "gather_first")
    small_all = small_g.reshape(8, -1)
    gathered_w = {("sin", 0): first_g}
    qn_w, kvn_w = mla_q_norm_g.shape[1], mla_kv_norm_g.shape[1]
    c_all = small_all[:, :D]
    qn_all = small_all[0::2, D:D + 2 * qn_w].reshape(4, 2, qn_w)
    kvn_all = small_all[0::2, D + 2 * qn_w:D + 2 * qn_w + 2 * kvn_w].reshape(4, 2, kvn_w)
    q_gain = jnp.pad(jnp.transpose(qn_all, (1, 0, 2)).reshape(2, 1, Q_RANK), ((0, 0), (0, 0), (0, Q_RANK_PAD - Q_RANK)))
    kv_gain = jnp.transpose(kvn_all, (1, 0, 2)).reshape(2, 1, KV_RANK)

    views = {}

    def weight(t, j):
        if (t, j) not in views:
            g = gathered_w[(t, j)]
            v = _row_view(g) if t in ("sout", "mout") else _col_view(g)
            if t == "uq":
                v = jnp.pad(v, ((0, 0), (0, Q_RANK_PAD - Q_RANK), (0, 0)))
            views[(t, j)] = v
        return views[(t, j)]

    cols = ada_w.shape[2]
    ada_b_cols = lax.dynamic_slice_in_dim(ada_b, chip * cols, cols, axis=1)
    c_pad = jnp.pad(c_all, ((0, 8), (0, 0)))
    mod_cols = _ada_mod(c_pad, ada_w, ada_b_cols, "ada_mod")[:, :8]
    mod_g, = _all_gather8([mod_cols.reshape(depth * 8, cols)], "gather_mod")
    mod_all = jnp.transpose(mod_g[0::2].reshape(4, depth, 8, cols), (1, 2, 0, 3)).reshape(depth, 8, 4 * cols)

    groups = [("sout0", [("sout", 0)]), ("mla0", [(t, 0) for t in ("min", "uq", "ukv", "mout")]),
              ("sgu1", [("sin", 1), ("sout", 1)]), ("mla1", [(t, 1) for t in ("min", "uq", "ukv", "mout")])]
    sends = {}
    behind = mod_g
    for gname, items in groups:
        blocks = [_my_half(mixer_w[t][j], ci) for t, j in items]
        sends[gname] = _send_start(blocks, [_own_slot(b, dev) for b in blocks], behind, True, f"send_{gname}")
        behind = sends[gname]["token"]

    def arrive(gname, after):
        lands = _send_wait(sends[gname], after, f"arrive_{gname}")
        gathered_w.update(zip(dict(groups)[gname], lands))
    mod = lax.dynamic_index_in_dim(mod_all, dev, 1, keepdims=False)
    shift = [mod[i:i + 1, :D] for i in range(depth)]
    scale = [mod[i:i + 1, D:2 * D] for i in range(depth)]
    gate = [mod[i:i + 1, 2 * D:] for i in range(depth)]

    cos_t, sin_t = _rope_tables(S)
    b_bc = jnp.broadcast_to(sgu_b_s[:, :, :, None], sgu_b_s.shape + (LANE,))
    w_sT = jnp.swapaxes(sgu_w_s, 2, 3)

    saved = []
    xs = x0
    for i in range(depth):
        j = i // 2
        tag = f"l{i}"
        h = _pre_fwd(xs, pre_g[i:i + 1], scale[i], shift[i], f"pre_fwd_{tag}", after=(behind,) if i == 0 else ())
        if i % 2 == 0:
            if j > 0:
                arrive(f"sgu{j}", h)
            uvz = _mm(h, weight("sin", j), b_sharded=True, name=f"sgu_in_{tag}")
            y = _sgu_gate_fwd(uvz, sgu_norm_g[j:j + 1], sgu_w_s[j], b_bc[j], f"sgu_gate_fwd_{tag}")
            if j == 0:
                arrive("sout0", y)
            out = _mm(y, weight("sout", j), name=f"sgu_out_{tag}")
            saved.append(dict(x=xs, h=h, uvz=uvz, y=y, out=out))
        else:
            arrive(f"mla{j}", h)
            p = _mm(h, weight("min", j), b_sharded=True, name=f"mla_in_{tag}")
            cqn, ckvn = _mla_mid_fwd(p, q_gain[j], kv_gain[j], f"mla_mid_fwd_{tag}")
            q = _mm(cqn, weight("uq", j), b_sharded=True, name=f"mla_uq_{tag}")
            kv = _mm(ckvn, weight("ukv", j), b_sharded=True, name=f"mla_ukv_{tag}")
            Q, K, V, VT = _mla_pack(q, kv, p, cos_t, sin_t, f"mla_pack_{tag}")
            o, lse = _attn_fwd_t(Q, K, VT, f"attn_fwd_{tag}")
            y = _mla_gate_fwd(o, p, f"mla_gate_fwd_{tag}")
            out = _mm(y, weight("mout", j), name=f"mla_out_{tag}")
            saved.append(dict(x=xs, h=h, p=p, cqn=cqn, ckvn=ckvn, Q=Q, K=K, V=V, o=o, lse=lse, y=y, out=out))
        xs = _post_fwd(xs, out, gate[i], post_g[i:i + 1], f"post_fwd_{tag}")

    dx, loss_part = _loss_grad(xs, target, "loss")
    loss = lax.psum(loss_part[0, 0], ("x", "y", "c"))

    dmod = [None] * depth
    d_pre_g = [None] * depth
    d_post_g = [None] * depth
    d_sgu = [None] * 2
    d_mla = [None] * 2
    kinds = ("sgu_w_in", "sgu_w_out", "mla_w_in", "mla_w_uq", "mla_w_ukv", "mla_w_out")
    halves = dict.fromkeys(kinds)
    in_flight = []

    def send_grads(items, label):
        slices = [dw.reshape(8, -1, dw.shape[-1]) for _, _, dw in items]
        lands = [_own_slot(lax.dynamic_index_in_dim(s, dev, 0, keepdims=False), dev) for s in slices]
        started = _send_start(slices, lands, None, False, f"send_{label}")
        in_flight.append((started, [(kind, layer) for kind, layer, _ in items], label))
        return (started["token"],)

    def collect(count, after):
        for _ in range(count):
            started, keys, label = in_flight.pop(0)
            lands = _send_wait(started, after, f"arrive_{label}")
            for (kind, layer), r in zip(keys, lands):
                halves[kind] = _sum_into_half(r, halves[kind], layer, ci, 2, f"sum_{kind}_{layer}")

    for i in reversed(range(depth)):
        j = i // 2
        tag = f"l{i}"
        sv = saved[i]
        older = len(in_flight)
        dy, dgate, d_post_g[i] = _post_bwd(dx, sv["out"], gate[i], post_g[i:i + 1], f"post_bwd_{tag}")
        if i % 2 == 0:
            dw_out = _mm(sv["y"], dy, ta=True, out_dtype=BF16, name=f"sgu_out_dw_{tag}")
            sent = send_grads([("sgu_w_out", j, dw_out)], f"{tag}_out")
            dyv = _mm(dy, weight("sout", j), tb=True, after=sent, name=f"sgu_out_dx_{tag}")
            duvz, dng, dws, dbs = _sgu_gate_bwd(sv["uvz"], dyv, sgu_norm_g[j:j + 1], sgu_w_s[j], w_sT[j], b_bc[j],
                                                f"sgu_gate_bwd_{tag}")
            sent = ()
            if i > 0:
                dw_in = _mm(sv["h"], duvz, ta=True, out_sharded=True, out_dtype=BF16, name=f"sgu_in_dw_{tag}")
                sent = send_grads([("sgu_w_in", j, dw_in)], f"{tag}_in")
            dh = _mm(duvz, weight("sin", j), tb=True, b_sharded=True, tk=3072, after=sent, name=f"sgu_in_dx_{tag}")
            d_sgu[j] = dict(norm_g=dng, w_s=dws, b_s=dbs[:, :, 0], duvz=duvz)
        else:
            dw_out = _mm(sv["y"], dy, ta=True, out_dtype=BF16, name=f"mla_out_dw_{tag}")
            sent = send_grads([("mla_w_out", j, dw_out)], f"{tag}_out")
            dyv = _mm(dy, weight("mout", j), tb=True, after=sent, name=f"mla_out_dx_{tag}")
            p = sv["p"]
            do = _mla_gate_bwd(dyv, p, f"mla_gate_bwd_{tag}")
            dQ, dK, dV = _attn_bwd(sv["Q"], sv["K"], sv["V"], sv["o"], do, sv["lse"], f"attn_bwd_{tag}")
            dq, dkv, dkr = _mla_unpack(dQ, dK, dV, cos_t, sin_t, f"mla_unpack_{tag}")
            dw_uq = _mm(sv["cqn"], dq, ta=True, out_sharded=True, out_dtype=BF16, name=f"mla_uq_dw_{tag}")
            dcqn = _mm(dq, weight("uq", j), tb=True, b_sharded=True, name=f"mla_uq_dx_{tag}")
            dw_ukv = _mm(sv["ckvn"], dkv, ta=True, out_sharded=True, out_dtype=BF16, name=f"mla_ukv_dw_{tag}")
            dckvn = _mm(dkv, weight("ukv", j), tb=True, b_sharded=True, name=f"mla_ukv_dx_{tag}")
            dp, dqg, dkvg = _mla_mid_bwd(p, dcqn, dckvn, dkr, dyv, sv["o"], q_gain[j], kv_gain[j], f"mla_mid_bwd_{tag}")
            dw_in = _mm(sv["h"], dp, ta=True, out_sharded=True, out_dtype=BF16, name=f"mla_in_dw_{tag}")
            sent = send_grads([("mla_w_in", j, dw_in), ("mla_w_uq", j, dw_uq[:, :Q_RANK]), ("mla_w_ukv", j, dw_ukv)],
                              f"{tag}_in")
            dh = _mm(dp, weight("min", j), tb=True, b_sharded=True, after=sent, name=f"mla_in_dx_{tag}")
            d_mla[j] = dict(qg=dqg[0, :Q_RANK], kvg=dkvg[0])
        dx, dshift, dscale, d_pre_g[i] = _pre_bwd(dh, sv["x"], dx, pre_g[i:i + 1], scale[i], f"pre_bwd_{tag}")
        dmod[i] = jnp.concatenate([dshift, dscale, dgate], axis=1)
        collect(older, dx)
    grad_x = dx.reshape(x.shape)

    parts = [jnp.concatenate(dmod, axis=0), jnp.concatenate(d_pre_g, axis=0), jnp.concatenate(d_post_g, axis=0),
             jnp.stack([d["norm_g"][0] for d in d_sgu]), jnp.stack([d["w_s"] for d in d_sgu]),
             jnp.stack([d["b_s"] for d in d_sgu]), jnp.stack([d["qg"] for d in d_mla]),
             jnp.stack([d["kvg"] for d in d_mla])]
    sizes = [int(np.prod(t.shape)) for t in parts]
    packed = _pack_rows(parts)
    packed_all, dmod_all = _all_gather8([packed, parts[0]], "gather_small_grads")
    total = _sum_slots(packed_all, "sum_small_grads").reshape(-1)
    offs = np.concatenate([[0], np.cumsum(sizes)])
    pieces = [total[int(offs[t]):int(offs[t + 1])].reshape(parts[t].shape) for t in range(len(parts))]
    g_ada_b, g_pre_g, g_post_g, g_norm_g, g_w_s, g_b_s, g_qg_full, g_kvg_full = pieces
    g_qg = lax.dynamic_slice_in_dim(g_qg_full, chip * qn_w, qn_w, axis=1)
    g_kvg = lax.dynamic_slice_in_dim(g_kvg_full, chip * kvn_w, kvn_w, axis=1)
    dmod_cols = jnp.stack([lax.dynamic_slice_in_dim(dmod_all[:, i], chip * cols, cols, axis=1) for i in range(depth)])
    dmod_cols = jnp.pad(dmod_cols, ((0, 0), (0, LANE - 8), (0, 0)))

    dw_in0 = _mm(saved[0]["h"], d_sgu[0]["duvz"], ta=True, out_sharded=True, out_dtype=BF16, after=(packed_all,),
                 name="sgu_in_dw_l0")
    sent = send_grads([("sgu_w_in", 0, dw_in0)], "l0_in")
    g_ada_w = _ada_grad(jnp.pad(c_all.T, ((0, 0), (0, LANE - 8))), dmod_cols, "ada_grad", after=sent)

    wnames = ["ada_w", "ada_b", "pre_g", "post_g", "sgu_w_in", "sgu_norm_g", "sgu_w_s", "sgu_b_s", "sgu_w_out",
              "mla_w_in", "mla_q_norm_g", "mla_kv_norm_g", "mla_w_uq", "mla_w_ukv", "mla_w_out"]
    weights = dict(zip(wnames, [ada_w, ada_b, pre_g, post_g, sgu_w_in, sgu_norm_g, sgu_w_s, sgu_b_s, sgu_w_out,
                                mla_w_in, mla_q_norm_g, mla_kv_norm_g, mla_w_uq, mla_w_ukv, mla_w_out]))
    ms = dict(zip(wnames, [m_ada_w, m_ada_b, m_pre_g, m_post_g, m_sgu_w_in, m_sgu_norm_g, m_sgu_w_s, m_sgu_b_s,
                           m_sgu_w_out, m_mla_w_in, m_mla_q_norm_g, m_mla_kv_norm_g, m_mla_w_uq, m_mla_w_ukv,
                           m_mla_w_out]))
    vs = dict(zip(wnames, [v_ada_w, v_ada_b, v_pre_g, v_post_g, v_sgu_w_in, v_sgu_norm_g, v_sgu_w_s, v_sgu_b_s,
                           v_sgu_w_out, v_mla_w_in, v_mla_q_norm_g, v_mla_kv_norm_g, v_mla_w_uq, v_mla_w_ukv,
                           v_mla_w_out]))
    grads = dict(ada_w=g_ada_w, ada_b=g_ada_b, pre_g=g_pre_g, post_g=g_post_g, sgu_norm_g=g_norm_g, sgu_w_s=g_w_s,
                 sgu_b_s=g_b_s, mla_q_norm_g=g_qg, mla_kv_norm_g=g_kvg)
    stepped = {}

    def step(nm):
        grads[nm] = grads[nm].reshape(weights[nm].shape)
        stepped[nm] = _adamw(weights[nm], grads[nm], ms[nm], vs[nm], f"adamw_{nm}")

    for nm in wnames:
        if nm in grads:
            step(nm)
    early = [kind for kind in kinds if kind != "sgu_w_in"]
    collect(len(in_flight) - 1, stepped["ada_w"][0])
    for kind, g in zip(early, _pair_swap([halves[kind] for kind in early], "swap_grads")):
        grads[kind] = g
        step(kind)
    collect(len(in_flight), stepped[early[-1]][0])
    grads["sgu_w_in"], = _pair_swap([halves["sgu_w_in"]], "swap_grads_last")
    step("sgu_w_in")
    return (loss, grad_x, *[grads[nm] for nm in wnames], *[stepped[nm][0] for nm in wnames],
            *[stepped[nm][1] for nm in wnames], *[stepped[nm][2] for nm in wnames])
```

```python
import math

import jax
import jax.numpy as jnp
import numpy as np
from jax import lax
from jax.experimental import pallas as pl
from jax.experimental.pallas import tpu as pltpu

F32 = jnp.float32
BF16 = jnp.bfloat16
MESH = pl.DeviceIdType.MESH

NORM_EPS = 1e-6
CHUNK = 64
SGU_BLOCK = 128
SGU_GROUPS = 16
HEADS = 16
NOPE = 128
ROPE = 64
VDIM = 128
QK_PAD = 256
Q_RANK = 448
Q_RANK_PAD = 512
KV_RANK = 512
ROPE_THETA = 10000.0
ATTN_SCALE = (NOPE + ROPE) ** -0.5
LOG2E = 1.4426950408889634
Q_FOLD = ATTN_SCALE * LOG2E
ATTN_TILE = 512

ADAM_LR = 0.001
ADAM_B1 = 0.9
ADAM_B2 = 0.999
ADAM_EPS = 1e-08
ADAM_WD = 0.01
ADAM_STEP = 10

LANE = 128
VMEM_LIMIT = 48 * 1024 * 1024

NN = (((1,), (0,)), ((), ()))
NT = (((1,), (1,)), ((), ()))
TN = (((0,), (0,)), ((), ()))


def _params(*sem):
    return pltpu.CompilerParams(dimension_semantics=sem, vmem_limit_bytes=VMEM_LIMIT)


def _row_tile(rows, row_bytes, target_bytes=1 << 20):
    if rows * row_bytes <= target_bytes or rows % 16:
        return rows
    best = 16
    t = 16
    while t <= rows:
        if rows % t == 0 and t * row_bytes <= target_bytes:
            best = t
        t += 16
    return best


def _fit(dim, target):
    if dim <= target:
        return dim
    t = (target // LANE) * LANE
    while t > LANE and dim % t:
        t -= LANE
    return t


def _gelu(x):
    return 0.5 * x * (1.0 + lax.erf(x * 0.7071067811865476))


def _gelu_and_grad(x):
    cdf = 0.5 * (1.0 + lax.erf(x * 0.7071067811865476))
    return x * cdf, cdf + x * jnp.exp(-0.5 * x * x) * 0.3989422804014327


def _mm(a, b, *, ta=False, tb=False, b_sharded=False, out_sharded=False, out_dtype=F32,
        tm=1024, tn=1024, tk=2048, after=(), name):
    if ta:
        K, M = a.shape
    else:
        M, K = a.shape
    if b_sharded:
        shards, rows, Cs = b.shape
        b_shape = (rows, shards * Cs)
    else:
        b_shape = b.shape
    if tb:
        N, K2 = b_shape
    else:
        K2, N = b_shape
    assert K == K2, (a.shape, b.shape, ta, tb)
    n_lim = Cs if (b_sharded and not tb) else (N // 4 if out_sharded else N)
    k_lim = Cs if (b_sharded and tb) else K
    tm, tn, tk = _fit(M, tm), _fit(n_lim, tn), _fit(k_lim, tk)
    assert M % tm == 0 and n_lim % tn == 0 and k_lim % tk == 0, (M, N, K, tm, tn, tk)
    nk = K // tk
    nb_n = n_lim // tn
    nb_k = k_lim // tk
    dims = (((0 if ta else 1,), (1 if tb else 0,)), ((), ()))

    def body(a_ref, b_ref, *rest):
        o_ref, *scratch = rest[len(after):]
        prod = lax.dot_general(a_ref[...].astype(BF16), b_ref[...].astype(BF16), dims,
                               preferred_element_type=F32)
        if nk == 1:
            o_ref[...] = prod.astype(out_dtype)
        else:
            acc_ref, = scratch
            k = pl.program_id(2)

            @pl.when(k == 0)
            def _():
                acc_ref[...] = prod

            @pl.when(k > 0)
            def _():
                acc_ref[...] += prod

            @pl.when(k == nk - 1)
            def _():
                o_ref[...] = acc_ref[...].astype(out_dtype)

    a_spec = (pl.BlockSpec((tk, tm), lambda i, j, k: (k, i)) if ta
              else pl.BlockSpec((tm, tk), lambda i, j, k: (i, k)))
    if b_sharded and tb:
        b_spec = pl.BlockSpec((None, tn, tk), lambda i, j, k: (k // nb_k, j, k % nb_k))
    elif b_sharded:
        b_spec = pl.BlockSpec((None, tk, tn), lambda i, j, k: (j // nb_n, k, j % nb_n))
    elif tb:
        b_spec = pl.BlockSpec((tn, tk), lambda i, j, k: (j, k))
    else:
        b_spec = pl.BlockSpec((tk, tn), lambda i, j, k: (k, j))
    if out_sharded:
        out_shape = jax.ShapeDtypeStruct((4, M, N // 4), out_dtype)
        out_spec = pl.BlockSpec((None, tm, tn), lambda i, j, k: (j // nb_n, i, j % nb_n))
    else:
        out_shape = jax.ShapeDtypeStruct((M, N), out_dtype)
        out_spec = pl.BlockSpec((tm, tn), lambda i, j, k: (i, j))
    return pl.pallas_call(
        body, name=name,
        out_shape=out_shape,
        grid=(M // tm, N // tn, nk),
        in_specs=[a_spec, b_spec] + [pl.BlockSpec(memory_space=pl.ANY)] * len(after),
        out_specs=out_spec,
        scratch_shapes=[] if nk == 1 else [pltpu.VMEM((tm, tn), F32)],
        compiler_params=_params("parallel", "parallel", "arbitrary"),
    )(a, b, *after)


def _split_bf16(v):
    hi = v.astype(BF16)
    lo = (v - hi.astype(F32)).astype(BF16)
    return hi, lo


def _dot3(a, b, dims):
    a_hi, a_lo = _split_bf16(a)
    b_hi, b_lo = _split_bf16(b)
    out = lax.dot_general(a_hi, b_hi, dims, preferred_element_type=F32)
    out += lax.dot_general(a_lo, b_hi, dims, preferred_element_type=F32)
    out += lax.dot_general(a_hi, b_lo, dims, preferred_element_type=F32)
    return out


def _ada_mod(c_all, ada_w, ada_b_cols, name):
    L, D, cols = ada_w.shape
    B = c_all.shape[0]
    tn = 512 if cols % 512 == 0 else cols

    def body(c_ref, w_ref, b_ref, o_ref):
        cv = c_ref[...]
        cond = cv * jax.nn.sigmoid(cv)
        o_ref[...] = _dot3(cond, w_ref[...], NN) + b_ref[...]

    return pl.pallas_call(
        body, name=name,
        out_shape=jax.ShapeDtypeStruct((L, B, cols), F32),
        grid=(L, cols // tn),
        in_specs=[pl.BlockSpec((B, D), lambda l, j: (0, 0)),
                  pl.BlockSpec((None, D, tn), lambda l, j: (l, 0, j)),
                  pl.BlockSpec((None, 1, tn), lambda l, j: (l, 0, j))],
        out_specs=pl.BlockSpec((None, B, tn), lambda l, j: (l, 0, j)),
        compiler_params=_params("parallel", "parallel"),
    )(c_all, ada_w, ada_b_cols.reshape(L, 1, cols))


def _ada_grad(c_t, dmod_cols, name, after=()):
    L, B, cols = dmod_cols.shape
    D = c_t.shape[0]
    tn = 512 if cols % 512 == 0 else cols

    def body(c_ref, d_ref, *rest):
        o_ref = rest[-1]
        cv = c_ref[...]
        cond = cv * jax.nn.sigmoid(cv)
        o_ref[...] = _dot3(cond, d_ref[...], NN)

    return pl.pallas_call(
        body, name=name,
        out_shape=jax.ShapeDtypeStruct((L, D, cols), F32),
        grid=(L, cols // tn),
        in_specs=[pl.BlockSpec((D, B), lambda l, j: (0, 0)),
                  pl.BlockSpec((None, B, tn), lambda l, j: (l, 0, j))] + [pl.BlockSpec(memory_space=pl.ANY)] * len(after),
        out_specs=pl.BlockSpec((None, D, tn), lambda l, j: (l, 0, j)),
        compiler_params=_params("parallel", "parallel"),
    )(c_t, dmod_cols, *after)


def _row_spec(ts, width):
    return pl.BlockSpec((ts, width), lambda i: (i, 0))


def _vec_spec(width):
    return pl.BlockSpec((1, width), lambda i: (0, 0))


def _pre_fwd(x, pre_g, scale, shift, name, after=()):
    S, D = x.shape
    ts = min(256, S)

    def body(x_ref, g_ref, sc_ref, sh_ref, *rest):
        h_ref = rest[-1]
        xv = x_ref[...]
        r = lax.rsqrt(jnp.mean(xv * xv, axis=-1, keepdims=True) + NORM_EPS)
        h_ref[...] = ((xv * r * g_ref[...]) * (1.0 + sc_ref[...]) + sh_ref[...]).astype(BF16)

    return pl.pallas_call(
        body, name=name, out_shape=jax.ShapeDtypeStruct((S, D), BF16), grid=(S // ts,),
        in_specs=[_row_spec(ts, D), _vec_spec(D), _vec_spec(D), _vec_spec(D)]
        + [pl.BlockSpec(memory_space=pl.ANY)] * len(after),
        out_specs=_row_spec(ts, D), compiler_params=_params("parallel"),
    )(x, pre_g, scale, shift, *after)


def _pre_bwd(dh, x, dx_res, pre_g, scale, name):
    S, D = x.shape
    ts = min(256, S)

    def body(dh_ref, x_ref, dr_ref, g_ref, sc_ref, dx_ref, dsh_ref, dsc_ref, dg_ref):
        @pl.when(pl.program_id(0) == 0)
        def _():
            dsh_ref[...] = jnp.zeros_like(dsh_ref)
            dsc_ref[...] = jnp.zeros_like(dsc_ref)
            dg_ref[...] = jnp.zeros_like(dg_ref)

        dh = dh_ref[...]
        xv = x_ref[...]
        g = g_ref[...]
        one_sc = 1.0 + sc_ref[...]
        r = lax.rsqrt(jnp.mean(xv * xv, axis=-1, keepdims=True) + NORM_EPS)
        xn = xv * r
        dsh_ref[...] += jnp.sum(dh, axis=0, keepdims=True)
        dsc_ref[...] += jnp.sum(dh * (xn * g), axis=0, keepdims=True)
        dg_ref[...] += jnp.sum(dh * one_sc * xn, axis=0, keepdims=True)
        dxn = dh * one_sc * g
        dx_ref[...] = dr_ref[...] + r * (dxn - xn * jnp.mean(dxn * xn, axis=-1, keepdims=True))

    vec = jax.ShapeDtypeStruct((1, D), F32)
    return pl.pallas_call(
        body, name=name, out_shape=(jax.ShapeDtypeStruct((S, D), F32), vec, vec, vec), grid=(S // ts,),
        in_specs=[_row_spec(ts, D), _row_spec(ts, D), _row_spec(ts, D), _vec_spec(D), _vec_spec(D)],
        out_specs=(_row_spec(ts, D), _vec_spec(D), _vec_spec(D), _vec_spec(D)),
        compiler_params=_params("arbitrary"),
    )(dh, x, dx_res, pre_g, scale)


def _post_fwd(x, y, gate, post_g, name):
    S, D = x.shape
    ts = min(256, S)

    def body(x_ref, y_ref, gt_ref, g_ref, o_ref):
        yv = y_ref[...]
        r = lax.rsqrt(jnp.mean(yv * yv, axis=-1, keepdims=True) + NORM_EPS)
        o_ref[...] = x_ref[...] + gt_ref[...] * (yv * r * g_ref[...])

    return pl.pallas_call(
        body, name=name, out_shape=jax.ShapeDtypeStruct((S, D), F32), grid=(S // ts,),
        in_specs=[_row_spec(ts, D), _row_spec(ts, D), _vec_spec(D), _vec_spec(D)],
        out_specs=_row_spec(ts, D), compiler_params=_params("parallel"),
    )(x, y, gate, post_g)


def _post_bwd(dx, y, gate, post_g, name):
    S, D = y.shape
    ts = min(256, S)

    def body(dx_ref, y_ref, gt_ref, g_ref, dy_ref, dgt_ref, dg_ref):
        @pl.when(pl.program_id(0) == 0)
        def _():
            dgt_ref[...] = jnp.zeros_like(dgt_ref)
            dg_ref[...] = jnp.zeros_like(dg_ref)

        dxv = dx_ref[...]
        yv = y_ref[...]
        g = g_ref[...]
        gt = gt_ref[...]
        r = lax.rsqrt(jnp.mean(yv * yv, axis=-1, keepdims=True) + NORM_EPS)
        yn = yv * r
        dgt_ref[...] += jnp.sum(dxv * (yn * g), axis=0, keepdims=True)
        dg_ref[...] += jnp.sum(dxv * gt * yn, axis=0, keepdims=True)
        dyn = dxv * gt * g
        dy_ref[...] = (r * (dyn - yn * jnp.mean(dyn * yn, axis=-1, keepdims=True))).astype(BF16)

    vec = jax.ShapeDtypeStruct((1, D), F32)
    return pl.pallas_call(
        body, name=name, out_shape=(jax.ShapeDtypeStruct((S, D), BF16), vec, vec), grid=(S // ts,),
        in_specs=[_row_spec(ts, D), _row_spec(ts, D), _vec_spec(D), _vec_spec(D)],
        out_specs=(_row_spec(ts, D), _vec_spec(D), _vec_spec(D)),
        compiler_params=_params("arbitrary"),
    )(dx, y, gate, post_g)


def _loss_grad(xf, target, name):
    S, D = xf.shape
    ts = min(256, S)

    def body(x_ref, t_ref, dx_ref, l_ref):
        @pl.when(pl.program_id(0) == 0)
        def _():
            l_ref[...] = jnp.zeros_like(l_ref)

        e = x_ref[...] - t_ref[...]
        dx_ref[...] = e * (1.0 / D)
        row = jnp.sum(e * e, axis=1, keepdims=True) * (1.0 / D)
        l_ref[...] += 0.5 * jnp.sum(row, axis=0, keepdims=True)

    return pl.pallas_call(
        body, name=name,
        out_shape=(jax.ShapeDtypeStruct((S, D), F32), jax.ShapeDtypeStruct((1, 1), F32)), grid=(S // ts,),
        in_specs=[_row_spec(ts, D), _row_spec(ts, D)],
        out_specs=(_row_spec(ts, D), pl.BlockSpec((1, 1), lambda i: (0, 0))),
        compiler_params=_params("arbitrary"),
    )(xf, target)


def _chunk_mask(transposed=False):
    row = lax.broadcasted_iota(jnp.int32, (SGU_BLOCK, SGU_BLOCK), 0) // CHUNK
    col = lax.broadcasted_iota(jnp.int32, (SGU_BLOCK, SGU_BLOCK), 1) // CHUNK
    return (row <= col) if transposed else (col <= row)


def _sgu_gate_fwd(uvz, norm_g, w_s, b_bc, name):
    S, E3 = uvz.shape
    E = E3 // 3
    T = SGU_BLOCK
    gd = E // SGU_GROUPS

    def body(uvz_ref, ng_ref, ws_ref, bb_ref, y_ref, v_scr):
        gv = _gelu(uvz_ref[:, E:2 * E])
        mu = jnp.mean(gv, axis=-1, keepdims=True)
        xc = gv - mu
        rstd = lax.rsqrt(jnp.mean(xc * xc, axis=-1, keepdims=True) + NORM_EPS)
        v_scr[...] = (xc * rstd * ng_ref[...]).astype(BF16)
        mask = _chunk_mask()
        for g in range(SGU_GROUPS):
            sl = slice(g * gd, (g + 1) * gd)
            wg = jnp.where(mask, ws_ref[g], 0.0).astype(BF16)
            vm = lax.dot_general(wg, v_scr[:, sl], NN, preferred_element_type=F32)
            vm = vm + jnp.tile(bb_ref[g], (1, gd // LANE))
            z = uvz_ref[:, 2 * E + g * gd:2 * E + (g + 1) * gd]
            y_ref[:, sl] = (_gelu(uvz_ref[:, sl]) * vm * (z * jax.nn.sigmoid(z))).astype(BF16)

    return pl.pallas_call(
        body, name=name, out_shape=jax.ShapeDtypeStruct((S, E), BF16), grid=(S // T,),
        in_specs=[_row_spec(T, E3), _vec_spec(E),
                  pl.BlockSpec((SGU_GROUPS, T, T), lambda i: (0, 0, 0)),
                  pl.BlockSpec((SGU_GROUPS, T, LANE), lambda i: (0, 0, 0))],
        out_specs=_row_spec(T, E),
        scratch_shapes=[pltpu.VMEM((T, E), BF16)],
        compiler_params=_params("parallel"),
    )(uvz, norm_g, w_s, b_bc)


def _sgu_gate_bwd(uvz, dyv, norm_g, w_s, w_sT, b_bc, name):
    S, E3 = uvz.shape
    E = E3 // 3
    T = SGU_BLOCK
    gd = E // SGU_GROUPS

    def body(uvz_ref, dyv_ref, ng_ref, ws_ref, wst_ref, bb_ref,
             d_ref, dng_ref, dws_ref, dbs_ref, vhat_scr, dv_scr, vgrad_scr):
        @pl.when(pl.program_id(0) == 0)
        def _():
            dng_ref[...] = jnp.zeros_like(dng_ref)
            dws_ref[...] = jnp.zeros_like(dws_ref)
            dbs_ref[...] = jnp.zeros_like(dbs_ref)

        gv, vgrad_scr[...] = _gelu_and_grad(uvz_ref[:, E:2 * E])
        mu = jnp.mean(gv, axis=-1, keepdims=True)
        xc = gv - mu
        rstd = lax.rsqrt(jnp.mean(xc * xc, axis=-1, keepdims=True) + NORM_EPS)
        vhat_scr[...] = xc * rstd
        mask = _chunk_mask()
        mask_t = _chunk_mask(transposed=True)
        for g in range(SGU_GROUPS):
            sl = slice(g * gd, (g + 1) * gd)
            u_pre = uvz_ref[:, sl]
            z = uvz_ref[:, 2 * E + g * gd:2 * E + (g + 1) * gd]
            dy = dyv_ref[:, sl]
            u, u_grad = _gelu_and_grad(u_pre)
            sig = jax.nn.sigmoid(z)
            sz = z * sig
            vg = (vhat_scr[:, sl] * ng_ref[:, sl]).astype(BF16)
            wg = jnp.where(mask, ws_ref[g], 0.0).astype(BF16)
            vm = lax.dot_general(wg, vg, NN, preferred_element_type=F32)
            vm = vm + jnp.tile(bb_ref[g], (1, gd // LANE))
            dy_u = dy * u
            d_ref[:, sl] = (dy * vm * sz * u_grad).astype(BF16)
            d_ref[:, 2 * E + g * gd:2 * E + (g + 1) * gd] = (
                dy_u * vm * (sig * (1.0 + z * (1.0 - sig)))).astype(BF16)
            dvm = dy_u * sz
            dvm_b = dvm.astype(BF16)
            dws_ref[g] += jnp.where(mask, lax.dot_general(dvm_b, vg, NT, preferred_element_type=F32), 0.0)
            dbs_ref[g] += jnp.broadcast_to(jnp.sum(dvm, axis=1, keepdims=True), (T, LANE))
            wgt = jnp.where(mask_t, wst_ref[g], 0.0).astype(BF16)
            dv_scr[:, sl] = lax.dot_general(wgt, dvm_b, NN, preferred_element_type=F32)
        dv = dv_scr[...]
        vhat = vhat_scr[...]
        dng_ref[...] += jnp.sum(dv * vhat, axis=0, keepdims=True)
        dvh = dv * ng_ref[...]
        dgv = rstd * (dvh - jnp.mean(dvh, axis=-1, keepdims=True)
                      - vhat * jnp.mean(dvh * vhat, axis=-1, keepdims=True))
        d_ref[:, E:2 * E] = (dgv * vgrad_scr[...]).astype(BF16)

    wspec = pl.BlockSpec((SGU_GROUPS, T, T), lambda i: (0, 0, 0))
    bspec = pl.BlockSpec((SGU_GROUPS, T, LANE), lambda i: (0, 0, 0))
    return pl.pallas_call(
        body, name=name,
        out_shape=(jax.ShapeDtypeStruct((S, E3), BF16), jax.ShapeDtypeStruct((1, E), F32),
                   jax.ShapeDtypeStruct((SGU_GROUPS, T, T), F32),
                   jax.ShapeDtypeStruct((SGU_GROUPS, T, LANE), F32)),
        grid=(S // T,),
        in_specs=[_row_spec(T, E3), _row_spec(T, E), _vec_spec(E), wspec, wspec, bspec],
        out_specs=(_row_spec(T, E3), _vec_spec(E), wspec, bspec),
        scratch_shapes=[pltpu.VMEM((T, E), F32), pltpu.VMEM((T, E), F32), pltpu.VMEM((T, E), F32)],
        compiler_params=_params("arbitrary"),
    )(uvz, dyv, norm_g, w_s, w_sT, b_bc)


MLA_WIDTH = HEADS * VDIM
P_LATENT = Q_RANK + KV_RANK + ROPE
P_WIDTH = P_LATENT + MLA_WIDTH


def _swap_halves(v):
    lane = lax.broadcasted_iota(jnp.int32, v.shape, 1)
    return jnp.where(lane % ROPE < ROPE // 2, pltpu.roll(v, LANE - ROPE // 2, 1), pltpu.roll(v, ROPE // 2, 1))


def _low_lanes(rows):
    return lax.broadcasted_iota(jnp.int32, (rows, LANE), 1) < ROPE


def _latent_tiles(ref):
    return [ref[:, t * LANE:(t + 1) * LANE] for t in range(P_LATENT // LANE)]


def _split_latents(tiles, low):
    cq = jnp.concatenate(tiles[0:3] + [jnp.where(low, tiles[3], 0.0)], axis=1)
    rolled = [pltpu.roll(t, ROPE, 1) for t in tiles[3:8]]
    ckv = jnp.concatenate([jnp.where(low, rolled[t], rolled[t + 1]) for t in range(4)], axis=1)
    kr = jnp.where(low, rolled[4], 0.0)
    return cq, ckv, kr


def _mla_mid_fwd(p, qg, kvg, name):
    S, PW = p.shape
    ts = min(256, S)

    def body(p_ref, qg_ref, kvg_ref, cqn_ref, ckvn_ref):
        cq, ckv, _ = _split_latents(_latent_tiles(p_ref), _low_lanes(ts))
        r = lax.rsqrt(jnp.sum(cq * cq, axis=-1, keepdims=True) * (1.0 / Q_RANK) + NORM_EPS)
        cqn_ref[...] = (cq * r * qg_ref[...]).astype(BF16)
        r2 = lax.rsqrt(jnp.mean(ckv * ckv, axis=-1, keepdims=True) + NORM_EPS)
        ckvn_ref[...] = (ckv * r2 * kvg_ref[...]).astype(BF16)

    return pl.pallas_call(
        body, name=name,
        out_shape=(jax.ShapeDtypeStruct((S, Q_RANK_PAD), BF16), jax.ShapeDtypeStruct((S, KV_RANK), BF16)),
        grid=(S // ts,),
        in_specs=[_row_spec(ts, P_LATENT), _vec_spec(Q_RANK_PAD), _vec_spec(KV_RANK)],
        out_specs=(_row_spec(ts, Q_RANK_PAD), _row_spec(ts, KV_RANK)),
        compiler_params=_params("parallel"),
    )(p, qg, kvg)


def _mla_pack(q, kv, p, cos_t, sin_t, name):
    S = q.shape[0]
    ts = min(256, S)
    pair_w = 2 * (NOPE + ROPE)
    head_w = NOPE + VDIM

    def body(q_ref, kv_ref, kr_ref, cos_ref, sin_ref, qo_ref, ko_ref, vo_ref, vt_ref):
        cosv = cos_ref[...]
        sinv = sin_ref[...]
        low = _low_lanes(ts)
        kr = jnp.where(low, pltpu.roll(kr_ref[...], ROPE, 1), 0.0)
        kr = (kr * cosv + _swap_halves(kr) * sinv).astype(BF16)
        for pair in range(HEADS // 2):
            t0, t1, t2 = (q_ref[:, pair * pair_w + t * LANE:pair * pair_w + (t + 1) * LANE] for t in range(3))
            nope_b = jnp.where(low, pltpu.roll(t1, ROPE, 1), pltpu.roll(t2, ROPE, 1))
            ropes = jnp.where(low, t1, t2)
            roped = (ropes * cosv + _swap_halves(ropes) * sinv) * Q_FOLD
            qo_ref[2 * pair, :, 0:NOPE] = (t0 * Q_FOLD).astype(BF16)
            qo_ref[2 * pair, :, NOPE:QK_PAD] = jnp.where(low, roped, 0.0).astype(BF16)
            qo_ref[2 * pair + 1, :, 0:NOPE] = (nope_b * Q_FOLD).astype(BF16)
            qo_ref[2 * pair + 1, :, NOPE:QK_PAD] = jnp.where(low, pltpu.roll(roped, ROPE, 1), 0.0).astype(BF16)
        eye = _identity(VDIM)
        for h in range(HEADS):
            ko_ref[h, :, 0:NOPE] = kv_ref[:, h * head_w:h * head_w + NOPE].astype(BF16)
            ko_ref[h, :, NOPE:QK_PAD] = kr
            vh = kv_ref[:, h * head_w + NOPE:(h + 1) * head_w].astype(BF16)
            vo_ref[h] = vh
            vt_ref[h] = lax.dot_general(eye, vh, NT, preferred_element_type=F32).astype(BF16)

    T = min(ATTN_TILE, S)
    per_tile = T // ts
    return pl.pallas_call(
        body, name=name,
        out_shape=(jax.ShapeDtypeStruct((HEADS, S, QK_PAD), BF16), jax.ShapeDtypeStruct((HEADS, S, QK_PAD), BF16),
                   jax.ShapeDtypeStruct((HEADS, S, VDIM), BF16), jax.ShapeDtypeStruct((HEADS, S // T, VDIM, T), BF16)),
        grid=(S // ts,),
        in_specs=[_row_spec(ts, q.shape[1]), _row_spec(ts, kv.shape[1]),
                  pl.BlockSpec((ts, LANE), lambda i: (i, P_LATENT // LANE - 1)),
                  _row_spec(ts, LANE), _row_spec(ts, LANE)],
        out_specs=(pl.BlockSpec((HEADS, ts, QK_PAD), lambda i: (0, i, 0)),
                   pl.BlockSpec((HEADS, ts, QK_PAD), lambda i: (0, i, 0)),
                   pl.BlockSpec((HEADS, ts, VDIM), lambda i: (0, i, 0)),
                   pl.BlockSpec((HEADS, None, VDIM, ts), lambda i: (0, i // per_tile, 0, i % per_tile))),
        compiler_params=_params("parallel"),
    )(q, kv, p, cos_t, sin_t)


def _mla_unpack(dQ, dK, dV, cos_t, sin_t, name):
    S = dQ.shape[1]
    ts = min(256, S)
    pair_w = 2 * (NOPE + ROPE)
    head_w = NOPE + VDIM

    def body(dq_ref, dk_ref, dv_ref, cos_ref, sin_ref, q_ref, kv_ref, kr_ref):
        cosv = cos_ref[...]
        sinv = sin_ref[...]
        low = _low_lanes(ts)
        for pair in range(HEADS // 2):
            blk = dq_ref[2 * pair, :, NOPE:QK_PAD] + pltpu.roll(dq_ref[2 * pair + 1, :, NOPE:QK_PAD], ROPE, 1)
            ropes = blk * cosv - _swap_halves(blk) * sinv
            nope_b = pltpu.roll(dq_ref[2 * pair + 1, :, 0:NOPE], ROPE, 1)
            q_ref[:, pair * pair_w:pair * pair_w + LANE] = dq_ref[2 * pair, :, 0:NOPE].astype(BF16)
            q_ref[:, pair * pair_w + LANE:pair * pair_w + 2 * LANE] = jnp.where(low, ropes, nope_b).astype(BF16)
            q_ref[:, pair * pair_w + 2 * LANE:(pair + 1) * pair_w] = jnp.where(low, nope_b, ropes).astype(BF16)
        dkr = dk_ref[0, :, NOPE:QK_PAD]
        for h in range(1, HEADS):
            dkr = dkr + dk_ref[h, :, NOPE:QK_PAD]
        kr_ref[...] = dkr * cosv - _swap_halves(dkr) * sinv
        for h in range(HEADS):
            kv_ref[:, h * head_w:h * head_w + NOPE] = dk_ref[h, :, 0:NOPE].astype(BF16)
            kv_ref[:, h * head_w + NOPE:(h + 1) * head_w] = dv_ref[h].astype(BF16)

    return pl.pallas_call(
        body, name=name,
        out_shape=(jax.ShapeDtypeStruct((S, HEADS * (NOPE + ROPE)), BF16),
                   jax.ShapeDtypeStruct((S, HEADS * (NOPE + VDIM)), BF16),
                   jax.ShapeDtypeStruct((S, LANE), F32)),
        grid=(S // ts,),
        in_specs=[pl.BlockSpec((HEADS, ts, QK_PAD), lambda i: (0, i, 0)),
                  pl.BlockSpec((HEADS, ts, QK_PAD), lambda i: (0, i, 0)),
                  pl.BlockSpec((HEADS, ts, VDIM), lambda i: (0, i, 0)),
                  _row_spec(ts, LANE), _row_spec(ts, LANE)],
        out_specs=(_row_spec(ts, HEADS * (NOPE + ROPE)), _row_spec(ts, HEADS * (NOPE + VDIM)),
                   _row_spec(ts, LANE)),
        compiler_params=_params("parallel"),
    )(dQ, dK, dV, cos_t, sin_t)


def _mla_gate_fwd(o, p, name):
    S, W = o.shape
    ts = min(256, S)
    wb = P_LATENT

    def body(o_ref, z_ref, y_ref):
        z = z_ref[...]
        y_ref[...] = (o_ref[...] * (z * jax.nn.sigmoid(z))).astype(BF16)

    return pl.pallas_call(
        body, name=name, out_shape=jax.ShapeDtypeStruct((S, W), BF16), grid=(S // ts, W // wb),
        in_specs=[pl.BlockSpec((ts, wb), lambda i, j: (i, j)), pl.BlockSpec((ts, wb), lambda i, j: (i, j + 1))],
        out_specs=pl.BlockSpec((ts, wb), lambda i, j: (i, j)), compiler_params=_params("parallel", "parallel"),
    )(o, p)


def _mla_gate_bwd(dyv, p, name):
    S, W = dyv.shape
    ts = min(256, S)
    wb = P_LATENT

    def body(d_ref, z_ref, do_ref):
        z = z_ref[...]
        do_ref[...] = d_ref[...] * (z * jax.nn.sigmoid(z))

    return pl.pallas_call(
        body, name=name, out_shape=jax.ShapeDtypeStruct((S, W), F32), grid=(S // ts, W // wb),
        in_specs=[pl.BlockSpec((ts, wb), lambda i, j: (i, j)), pl.BlockSpec((ts, wb), lambda i, j: (i, j + 1))],
        out_specs=pl.BlockSpec((ts, wb), lambda i, j: (i, j)), compiler_params=_params("parallel", "parallel"),
    )(dyv, p)


def _mla_mid_bwd(p, dcqn, dckvn, dkr, dyv, o, qg, kvg, name):
    S, PW = p.shape
    W = o.shape[1]
    ts = min(256, S)
    nt = Q_RANK_PAD // LANE

    def rms_bwd(xv, dy, g, count):
        r = lax.rsqrt(jnp.sum(xv * xv, axis=-1, keepdims=True) * (1.0 / count) + NORM_EPS)
        xn = xv * r
        dg = jnp.sum(dy * xn, axis=0, keepdims=True)
        dxn = dy * g
        dx = r * (dxn - xn * (jnp.sum(dxn * xn, axis=-1, keepdims=True) * (1.0 / count)))
        return dx, dg

    def body(p_ref, dcq_ref, dckv_ref, dkr_ref, dyv_ref, o_ref, qg_ref, kvg_ref, dp_ref, dqg_ref, dkvg_ref):
        @pl.when(pl.program_id(0) == 0)
        def _():
            dqg_ref[...] = jnp.zeros_like(dqg_ref)
            dkvg_ref[...] = jnp.zeros_like(dkvg_ref)

        low = _low_lanes(ts)
        cq, ckv, _ = _split_latents(_latent_tiles(p_ref), low)
        dcq, dg = rms_bwd(cq, dcq_ref[...], qg_ref[...], Q_RANK)
        dqg_ref[...] += dg
        dckv, dg = rms_bwd(ckv, dckv_ref[...], kvg_ref[...], KV_RANK)
        dkvg_ref[...] += dg
        moved = [pltpu.roll(dckv[:, t * LANE:(t + 1) * LANE], ROPE, 1) for t in range(nt)]
        moved.append(pltpu.roll(dkr_ref[...], ROPE, 1))
        for t in range(nt - 1):
            dp_ref[:, t * LANE:(t + 1) * LANE] = dcq[:, t * LANE:(t + 1) * LANE].astype(BF16)
        dp_ref[:, (nt - 1) * LANE:nt * LANE] = jnp.where(low, dcq[:, (nt - 1) * LANE:nt * LANE], moved[0]).astype(BF16)
        for t in range(nt):
            dp_ref[:, (nt + t) * LANE:(nt + t + 1) * LANE] = jnp.where(low, moved[t], moved[t + 1]).astype(BF16)
        z = p_ref[:, P_LATENT:PW]
        sig = jax.nn.sigmoid(z)
        dp_ref[:, P_LATENT:PW] = (dyv_ref[...] * o_ref[...] * (sig * (1.0 + z * (1.0 - sig)))).astype(BF16)

    return pl.pallas_call(
        body, name=name,
        out_shape=(jax.ShapeDtypeStruct((S, PW), BF16), jax.ShapeDtypeStruct((1, Q_RANK_PAD), F32),
                   jax.ShapeDtypeStruct((1, KV_RANK), F32)),
        grid=(S // ts,),
        in_specs=[_row_spec(ts, PW), _row_spec(ts, Q_RANK_PAD), _row_spec(ts, KV_RANK), _row_spec(ts, LANE),
                  _row_spec(ts, W), _row_spec(ts, W), _vec_spec(Q_RANK_PAD), _vec_spec(KV_RANK)],
        out_specs=(_row_spec(ts, PW), _vec_spec(Q_RANK_PAD), _vec_spec(KV_RANK)),
        compiler_params=_params("arbitrary"),
    )(p, dcqn, dckvn, dkr, dyv, o, qg, kvg)


def _tile_mask(T):
    row = lax.broadcasted_iota(jnp.int32, (T, T), 0) // CHUNK
    col = lax.broadcasted_iota(jnp.int32, (T, T), 1) // CHUNK
    return col <= row


def _attn_bwd(Q, K, V, o, do, lse, name):
    H, S, _ = Q.shape
    T = min(ATTN_TILE, S)
    nq = S // T

    def body(q_ref, k_ref, v_ref, o_ref, do_ref, lse_ref, dq_ref, dk_ref, dv_ref, dk_scr, dv_scr, s_scr, dp_scr):
        ki = pl.program_id(1)

        @pl.when(ki == 0)
        def _():
            dq_ref[...] = jnp.zeros_like(dq_ref)

        dk_scr[...] = jnp.zeros_like(dk_scr)
        dv_scr[...] = jnp.zeros_like(dv_scr)
        k = k_ref[...]
        v = v_ref[...]

        def scores(i):
            rows = pl.ds(pl.multiple_of(i * T, T), T)
            s = lax.dot_general(q_ref[rows, :], k, NT, preferred_element_type=F32)
            dp = lax.dot_general(do_ref[rows, :].astype(BF16), v, NT, preferred_element_type=F32)
            return s, dp

        def grads(i, s, dp, masked):
            rows = pl.ds(pl.multiple_of(i * T, T), T)
            do_f = do_ref[rows, :]
            delta = jnp.sum(do_f * o_ref[rows, :], axis=1, keepdims=True)
            pr = jnp.exp2(s - lse_ref[rows, 0:1])
            if masked:
                pr = jnp.where(_tile_mask(T), pr, 0.0)
            dv_scr[...] += lax.dot_general(pr.astype(BF16), do_f.astype(BF16), TN, preferred_element_type=F32)
            ds = (pr * (dp - delta)).astype(BF16)
            dk_scr[...] += lax.dot_general(ds, q_ref[rows, :], TN, preferred_element_type=F32)
            dq_ref[rows, :] += lax.dot_general(ds, k, NN, preferred_element_type=F32) * ATTN_SCALE

        s_scr[...], dp_scr[...] = scores(ki)

        @pl.when(ki + 1 == nq)
        def _():
            grads(ki, s_scr[...], dp_scr[...], True)

        @pl.when(ki + 1 < nq)
        def _():
            def step(i, masked):
                nxt_s, nxt_dp = scores(i + 1)
                grads(i, s_scr[...], dp_scr[...], masked)
                s_scr[...] = nxt_s
                dp_scr[...] = nxt_dp

            def loop_step(i, carry):
                step(i, False)
                return carry

            step(ki, True)
            lax.fori_loop(ki + 1, nq - 1, loop_step, 0)
            grads(nq - 1, s_scr[...], dp_scr[...], False)

        dk_ref[...] = dk_scr[...] * (1.0 / LOG2E)
        dv_ref[...] = dv_scr[...]

    return pl.pallas_call(
        body, name=name,
        out_shape=(jax.ShapeDtypeStruct((H, S, QK_PAD), F32), jax.ShapeDtypeStruct((H, S, QK_PAD), F32),
                   jax.ShapeDtypeStruct((H, S, VDIM), F32)),
        grid=(H, nq),
        in_specs=[pl.BlockSpec((None, S, QK_PAD), lambda h, j: (h, 0, 0)),
                  pl.BlockSpec((None, T, QK_PAD), lambda h, j: (h, j, 0)),
                  pl.BlockSpec((None, T, VDIM), lambda h, j: (h, j, 0)),
                  pl.BlockSpec((S, VDIM), lambda h, j: (0, h)),
                  pl.BlockSpec((S, VDIM), lambda h, j: (0, h)),
                  pl.BlockSpec((None, S, LANE), lambda h, j: (h, 0, 0))],
        out_specs=(pl.BlockSpec((None, S, QK_PAD), lambda h, j: (h, 0, 0)),
                   pl.BlockSpec((None, T, QK_PAD), lambda h, j: (h, j, 0)),
                   pl.BlockSpec((None, T, VDIM), lambda h, j: (h, j, 0))),
        scratch_shapes=[pltpu.VMEM((T, QK_PAD), F32), pltpu.VMEM((T, VDIM), F32),
                        pltpu.VMEM((T, T), F32), pltpu.VMEM((T, T), F32)],
        compiler_params=_params("parallel", "arbitrary"),
    )(Q, K, V, o, do, lse)


def _identity(n):
    return (lax.broadcasted_iota(jnp.int32, (n, n), 0) == lax.broadcasted_iota(jnp.int32, (n, n), 1)).astype(BF16)


def _key_le_query(rows, cols, col0):
    key = lax.broadcasted_iota(jnp.int32, (rows, cols), 0) // CHUNK
    query = (lax.broadcasted_iota(jnp.int32, (rows, cols), 1) + col0) // CHUNK
    return key <= query


def _attn_fwd_t(Q, K, VT, name):
    H, nT, _, T = VT.shape
    S = nT * T
    n_part = 2 if T % 256 == 0 else 1
    Tq = T // n_part

    def body(q_ref, k_ref, vt_ref, o_ref, lse_ref, m_scr, l_scr, acc_scr, s_scr):
        qi = pl.program_id(1)
        m_scr[...] = jnp.full_like(m_scr, -jnp.inf)
        l_scr[...] = jnp.zeros_like(l_scr)
        acc_scr[...] = jnp.zeros_like(acc_scr)

        def scores(j):
            kt = k_ref[pl.ds(pl.multiple_of(j * T, T), T), :]
            return lax.dot_general(kt, q_ref[...], NT, preferred_element_type=F32)

        def softmax_step(j, masked):
            vt = vt_ref[j]
            for part in range(n_part):
                sub = slice(part * Tq, (part + 1) * Tq)
                st = s_scr[:, sub]
                if masked:
                    st = jnp.where(_key_le_query(T, Tq, part * Tq), st, -1e30)
                m_prev = m_scr[:, sub]
                m_new = jnp.maximum(m_prev, jnp.max(st, axis=0, keepdims=True))
                pt = jnp.exp2(st - m_new)
                alpha = jnp.exp2(m_prev - m_new)
                l_scr[:, sub] = alpha * l_scr[:, sub] + jnp.sum(pt, axis=0, keepdims=True)
                acc_scr[:, sub] = alpha * acc_scr[:, sub] + lax.dot_general(
                    vt, pt.astype(BF16), NN, preferred_element_type=F32)
                m_scr[:, sub] = m_new

        s_scr[...] = scores(0)

        def step(j, carry):
            nxt = scores(j + 1)
            softmax_step(j, False)
            s_scr[...] = nxt
            return carry

        lax.fori_loop(0, qi, step, 0)
        softmax_step(qi, True)
        l = l_scr[...]
        o_ref[...] = jnp.transpose(acc_scr[...] / l)
        lse_ref[...] = jnp.transpose(jnp.broadcast_to(m_scr[...] + jnp.log2(l), (LANE, T)))

    return pl.pallas_call(
        body, name=name,
        out_shape=(jax.ShapeDtypeStruct((S, H * VDIM), F32), jax.ShapeDtypeStruct((H, S, LANE), F32)),
        grid=(H, nT),
        in_specs=[pl.BlockSpec((None, T, QK_PAD), lambda h, i: (h, i, 0)),
                  pl.BlockSpec((None, S, QK_PAD), lambda h, i: (h, 0, 0)),
                  pl.BlockSpec((None, nT, VDIM, T), lambda h, i: (h, 0, 0, 0))],
        out_specs=(pl.BlockSpec((T, VDIM), lambda h, i: (i, h)),
                   pl.BlockSpec((None, T, LANE), lambda h, i: (h, i, 0))),
        scratch_shapes=[pltpu.VMEM((1, T), F32), pltpu.VMEM((1, T), F32), pltpu.VMEM((VDIM, T), F32),
                        pltpu.VMEM((T, T), F32)],
        compiler_params=_params("parallel", "arbitrary"),
    )(Q, K, VT)


def _adamw(w, g, m, v, name):
    shape = w.shape
    C = shape[-1]
    R = math.prod(shape[:-1])
    flat = [t.reshape(R, C) for t in (w, g, m, v)]
    tr = _row_tile(R, C * 4)

    def body(w_ref, g_ref, m_ref, v_ref, d_ref, nm_ref, nv_ref):
        gv = g_ref[...]
        m_new = ADAM_B1 * m_ref[...] + (1.0 - ADAM_B1) * gv
        v_new = ADAM_B2 * v_ref[...] + (1.0 - ADAM_B2) * jnp.square(gv)
        m_hat = m_new / (1.0 - ADAM_B1 ** ADAM_STEP)
        v_hat = v_new / (1.0 - ADAM_B2 ** ADAM_STEP)
        d_ref[...] = -ADAM_LR * (m_hat / (jnp.sqrt(v_hat) + ADAM_EPS) + ADAM_WD * w_ref[...])
        nm_ref[...] = m_new
        nv_ref[...] = v_new

    spec = pl.BlockSpec((tr, C), lambda i: (i, 0))
    out = jax.ShapeDtypeStruct((R, C), F32)
    d, nm, nv = pl.pallas_call(
        body, name=name, out_shape=(out, out, out), grid=(R // tr,),
        in_specs=[spec] * 4, out_specs=(spec, spec, spec), compiler_params=_params("parallel"),
    )(*flat)
    return d.reshape(shape), nm.reshape(shape), nv.reshape(shape)


def _sum_into_half(r, buf, layer, ci, n_layers, name):
    n, M, N = r.shape
    tr = _row_tile(M, N * 4 * n, 4 << 20)

    def body(c_ref, r_ref, *rest):
        o_ref = rest[-1]
        acc = r_ref[0].astype(F32)
        for s in range(1, n):
            acc = acc + r_ref[s].astype(F32)
        o_ref[...] = acc

    in_specs = [pl.BlockSpec((n, tr, N), lambda i, c: (0, i, 0))]
    operands = [ci.reshape(1), r]
    aliases = {}
    if buf is not None:
        in_specs.append(ANY)
        operands.append(buf)
        aliases = {2: 0}
    return pl.pallas_call(
        body, name=name, out_shape=jax.ShapeDtypeStruct((n_layers, 2, M, N), F32),
        grid_spec=pltpu.PrefetchScalarGridSpec(
            num_scalar_prefetch=1, grid=(M // tr,), in_specs=in_specs,
            out_specs=pl.BlockSpec((None, None, tr, N), lambda i, c: (layer, c[0], i, 0))),
        input_output_aliases=aliases, compiler_params=_params("parallel"),
    )(*operands)


def _sum_slots(r, name):
    n, M, N = r.shape
    tr = _row_tile(M, N * 4 * n, 4 << 20)

    def body(r_ref, o_ref):
        acc = r_ref[0].astype(F32)
        for s in range(1, n):
            acc = acc + r_ref[s].astype(F32)
        o_ref[...] = acc

    return pl.pallas_call(
        body, name=name, out_shape=jax.ShapeDtypeStruct((M, N), F32), grid=(M // tr,),
        in_specs=[pl.BlockSpec((n, tr, N), lambda i: (0, i, 0))],
        out_specs=pl.BlockSpec((tr, N), lambda i: (i, 0)), compiler_params=_params("parallel"),
    )(r)


ANY = pl.BlockSpec(memory_space=pl.ANY)
DMA_CHUNK_BYTES = 1 << 20
DMA_MAX_CHUNKS = 16
PEER_ORDER = (1, 4, 5, 2, 3, 6, 7)


def _position():
    return lax.axis_index("x"), lax.axis_index("y"), lax.axis_index("c")


def _row_chunks(shape, dtype):
    rows, cols = shape
    n = max(1, min(DMA_MAX_CHUNKS, rows * cols * jnp.dtype(dtype).itemsize // DMA_CHUNK_BYTES))
    while n > 1 and (rows % n or (rows // n) % 16):
        n -= 1
    step = rows // n
    return [pl.ds(q * step, step) for q in range(n)]


def _all_gather8(xs, name):
    n = len(xs)

    def body(*refs):
        x_refs, o_refs = refs[:n], refs[n:2 * n]
        send_sems, recv_sems, local_sems = refs[2 * n:]
        x, y, c = _position()
        me, sibling = (x, y, c), (x, y, 1 - c)
        chips = [(1 - x, y), (x, 1 - y), (1 - x, 1 - y)]

        def slot(a, dev, rows):
            return o_refs[a].at[4 * dev[0] + 2 * dev[1] + dev[2], rows]

        def copy(a, k, block, to, rows, from_input=False):
            return pltpu.make_async_remote_copy(
                src_ref=x_refs[a].at[rows] if from_input else slot(a, block, rows), dst_ref=slot(a, block, rows),
                send_sem=send_sems.at[a, k], recv_sem=recv_sems.at[a, k],
                device_id=to, device_id_type=MESH)

        def mine(a, rows):
            return pltpu.make_async_copy(x_refs[a].at[rows], slot(a, me, rows), local_sems.at[a])

        chunks = [_row_chunks(t.shape, t.dtype) for t in xs]
        whole = [pl.ds(0, t.shape[0]) for t in xs]
        for a in range(n):
            for rows in chunks[a]:
                mine(a, rows).start()
        sent = []
        for a in range(n):
            for k, to in enumerate([sibling] + [(*chip, c) for chip in chips]):
                for rows in chunks[a]:
                    copy(a, k, me, to, rows, from_input=True).start()
                sent.append(copy(a, k, me, to, whole[a], from_input=True))
        for a in range(n):
            for j, chip in enumerate(chips):
                copy(a, 1 + j, (*chip, c), me, whole[a]).wait_recv()
                for rows in chunks[a]:
                    copy(a, 4 + j, (*chip, c), sibling, rows).start()
                sent.append(copy(a, 4 + j, (*chip, c), sibling, whole[a]))
        for a in range(n):
            copy(a, 0, sibling, me, whole[a]).wait_recv()
            for j, chip in enumerate(chips):
                copy(a, 4 + j, (*chip, 1 - c), me, whole[a]).wait_recv()
        for cp in sent:
            cp.wait_send()
        for a in range(n):
            mine(a, whole[a]).wait()

    return pl.pallas_call(
        body, name=name,
        out_shape=[jax.ShapeDtypeStruct((8,) + t.shape, t.dtype) for t in xs],
        in_specs=[ANY] * n, out_specs=[ANY] * n,
        scratch_shapes=[pltpu.SemaphoreType.DMA((n, 7)), pltpu.SemaphoreType.DMA((n, 7)),
                        pltpu.SemaphoreType.DMA((n,))],
    )(*xs)


HBM = pl.BlockSpec(memory_space=pltpu.HBM)
SEM = pl.BlockSpec(memory_space=pltpu.SEMAPHORE)
EFFECT = pltpu.SideEffectType.DATAFLOW_SIDE_EFFECTING


def _peer(m, x, y, c):
    return ((1 - x) if m & 4 else x, (1 - y) if m & 2 else y, (1 - c) if m & 1 else c)


def _send_copies(src_refs, land_refs, send_sems, recv_sems, broadcast):
    x, y, c = _position()
    my = 4 * x + 2 * y + c
    out = []
    for a in range(len(src_refs)):
        for m in PEER_ORDER:
            px, py, pc = _peer(m, x, y, c)
            src = src_refs[a] if broadcast else src_refs[a].at[4 * px + 2 * py + pc]
            out.append(pltpu.make_async_remote_copy(
                src_ref=src, dst_ref=land_refs[a].at[my], send_sem=send_sems[a], recv_sem=recv_sems[a],
                device_id=(px, py, pc), device_id_type=MESH))
    return out


def _send_drain(land_refs, send_sems, recv_sems):
    x, y, c = _position()
    for a in range(len(land_refs)):
        seven = land_refs[a].at[pl.ds(0, 7)]
        both = pltpu.make_async_remote_copy(
            src_ref=seven, dst_ref=seven, send_sem=send_sems[a], recv_sem=recv_sems[a],
            device_id=(x, y, c), device_id_type=MESH)
        both.wait_send()
        both.wait_recv()


def _send_start(srcs, lands, after, broadcast, name):
    n = len(srcs)
    extra = [] if after is None else [after]

    def body(*refs):
        src_refs, land_refs = refs[:n], refs[n:2 * n]
        outs = refs[2 * n + len(extra):]
        send_sems, recv_sems = outs[:n], outs[n:2 * n]
        token = refs[-1]
        for cp in _send_copies(src_refs, land_refs, send_sems, recv_sems, broadcast):
            cp.start()
        token[...] = jnp.zeros_like(token)

    hbm = [pltpu.with_memory_space_constraint(t, pltpu.HBM) for t in list(srcs) + list(lands)]
    res = pl.pallas_call(
        body, name=name,
        out_shape=(*[pltpu.SemaphoreType.DMA(())] * (2 * n),
                   *[pltpu.HBM(t.shape, t.dtype) for t in hbm], jax.ShapeDtypeStruct((8, LANE), F32)),
        in_specs=[HBM] * (2 * n) + [ANY] * len(extra),
        out_specs=(*[SEM] * (2 * n), *[HBM] * (2 * n), pl.BlockSpec(memory_space=pltpu.VMEM)),
        input_output_aliases={i: 2 * n + i for i in range(2 * n)},
        compiler_params=pltpu.CompilerParams(has_side_effects=EFFECT),
    )(*hbm, *extra)
    return dict(sems=res[:2 * n], srcs=res[2 * n:3 * n], lands=res[3 * n:4 * n], token=res[-1], broadcast=broadcast)


def _send_wait(started, after, name):
    n = len(started["srcs"])

    def body(*refs):
        land_refs = refs[n:2 * n]
        send_sems, recv_sems = refs[2 * n:3 * n], refs[3 * n:4 * n]
        _send_drain(land_refs, send_sems, recv_sems)

    operands = list(started["srcs"]) + list(started["lands"])
    res = pl.pallas_call(
        body, name=name,
        out_shape=[pltpu.HBM(t.shape, t.dtype) for t in operands],
        in_specs=[HBM] * (2 * n) + [SEM] * (2 * n) + [ANY],
        out_specs=[HBM] * (2 * n),
        input_output_aliases={i: i for i in range(2 * n)},
        compiler_params=pltpu.CompilerParams(has_side_effects=EFFECT),
    )(*operands, *started["sems"], after)
    return res[n:]


def _own_slot(block, dev):
    zone = lax.empty((8,) + block.shape, block.dtype)
    return lax.dynamic_update_slice(zone, block[None], (dev, 0, 0))


def _pair_swap(bufs, name):
    n = len(bufs)
    pieces = [(a, l) for a, t in enumerate(bufs) for l in range(t.shape[0])]

    def body(*refs):
        b_refs = refs[n:2 * n]
        send_sems, recv_sems = refs[2 * n:]
        x, y, c = _position()

        def copy(k, rows):
            a, l = pieces[k]
            half = b_refs[a].at[l, c, rows]
            return pltpu.make_async_remote_copy(
                src_ref=half, dst_ref=half, send_sem=send_sems.at[k], recv_sem=recv_sems.at[k],
                device_id=(x, y, 1 - c), device_id_type=MESH)

        chunks = [_row_chunks(bufs[a].shape[2:], bufs[a].dtype) for a, _ in pieces]
        whole = [pl.ds(0, bufs[a].shape[2]) for a, _ in pieces]
        for k in range(len(pieces)):
            for rows in chunks[k]:
                copy(k, rows).start()
        for k in range(len(pieces)):
            copy(k, whole[k]).wait_recv()
        for k in range(len(pieces)):
            copy(k, whole[k]).wait_send()

    return pl.pallas_call(
        body, name=name,
        out_shape=[jax.ShapeDtypeStruct(t.shape, t.dtype) for t in bufs],
        in_specs=[ANY] * n, out_specs=[ANY] * n,
        input_output_aliases={a: a for a in range(n)},
        scratch_shapes=[pltpu.SemaphoreType.DMA((len(pieces),)), pltpu.SemaphoreType.DMA((len(pieces),))],
    )(*bufs)


def _pack_rows(parts):
    flat = jnp.concatenate([t.reshape(-1).astype(F32) for t in parts])
    pad = (-flat.shape[0]) % (256 * LANE)
    return jnp.pad(flat, (0, pad)).reshape(-1, LANE)


def _my_half(w2d, ci):
    half = w2d.shape[0] // 2
    return lax.dynamic_slice_in_dim(w2d, ci * half, half, axis=0).astype(BF16)


def _col_view(g):
    _, half, Cs = g.shape
    return g.reshape(4, 2 * half, Cs)


def _row_view(g):
    _, half, C = g.shape
    return g.reshape(8 * half, C)


def _rope_tables(S):
    pos = jnp.arange(S, dtype=F32)
    inv_freq = ROPE_THETA ** (-jnp.arange(0, ROPE, 2, dtype=F32) / ROPE)
    ang = pos[:, None] * inv_freq[None, :]
    cos, sin = jnp.cos(ang), jnp.sin(ang)
    cos_t = jnp.concatenate([cos, cos, cos, cos], axis=1)
    sin_t = jnp.concatenate([-sin, sin, -sin, sin], axis=1)
    return cos_t, sin_t


def kernel(x, c, ada_w, ada_b, pre_g, post_g, sgu_w_in, sgu_norm_g, sgu_w_s, sgu_b_s, sgu_w_out, mla_w_in, mla_q_norm_g, mla_kv_norm_g, mla_w_uq, mla_w_ukv, mla_w_out, loss_target, m_ada_w, m_ada_b, m_pre_g, m_post_g, m_sgu_w_in, m_sgu_norm_g, m_sgu_w_s, m_sgu_b_s, m_sgu_w_out, m_mla_w_in, m_mla_q_norm_g, m_mla_kv_norm_g, m_mla_w_uq, m_mla_w_ukv, m_mla_w_out, v_ada_w, v_ada_b, v_pre_g, v_post_g, v_sgu_w_in, v_sgu_norm_g, v_sgu_w_s, v_sgu_b_s, v_sgu_w_out, v_mla_w_in, v_mla_q_norm_g, v_mla_kv_norm_g, v_mla_w_uq, v_mla_w_ukv, v_mla_w_out):
    S, D = x.shape[1], x.shape[2]
    depth = ada_w.shape[0]
    E = sgu_w_out.shape[1] * 4
    xi, yi, ci = _position()
    chip = 2 * xi + yi
    dev = 4 * xi + 2 * yi + ci
    x0 = x.reshape(S, D)
    target = loss_target.reshape(S, D)

    small = _pack_rows([c, mla_q_norm_g, mla_kv_norm_g])
    mixer_w = dict(sin=sgu_w_in, sout=sgu_w_out, min=mla_w_in, uq=mla_w_uq, ukv=mla_w_ukv, mout=mla_w_out)
    small_g, first_g = _all_gather8([small, _my_half(sgu_w_in[0], ci)], "gather_first")
    small_all = small_g.reshape(8, -1)
    gathered_w = {("sin", 0): first_g}
    qn_w, kvn_w = mla_q_norm_g.shape[1], mla_kv_norm_g.shape[1]
    c_all = small_all[:, :D]
    qn_all = small_all[0::2, D:D + 2 * qn_w].reshape(4, 2, qn_w)
    kvn_all = small_all[0::2, D + 2 * qn_w:D + 2 * qn_w + 2 * kvn_w].reshape(4, 2, kvn_w)
    q_gain = jnp.pad(jnp.transpose(qn_all, (1, 0, 2)).reshape(2, 1, Q_RANK), ((0, 0), (0, 0), (0, Q_RANK_PAD - Q_RANK)))
    kv_gain = jnp.transpose(kvn_all, (1, 0, 2)).reshape(2, 1, KV_RANK)

    views = {}

    def weight(t, j):
        if (t, j) not in views:
            g = gathered_w[(t, j)]
            v = _row_view(g) if t in ("sout", "mout") else _col_view(g)
            if t == "uq":
                v = jnp.pad(v, ((0, 0), (0, Q_RANK_PAD - Q_RANK), (0, 0)))
            views[(t, j)] = v
        return views[(t, j)]

    cols = ada_w.shape[2]
    ada_b_cols = lax.dynamic_slice_in_dim(ada_b, chip * cols, cols, axis=1)
    c_pad = jnp.pad(c_all, ((0, 8), (0, 0)))
    mod_cols = _ada_mod(c_pad, ada_w, ada_b_cols, "ada_mod")[:, :8]
    mod_g, = _all_gather8([mod_cols.reshape(depth * 8, cols)], "gather_mod")
    mod_all = jnp.transpose(mod_g[0::2].reshape(4, depth, 8, cols), (1, 2, 0, 3)).reshape(depth, 8, 4 * cols)

    groups = [("sout0", [("sout", 0)]), ("mla0", [(t, 0) for t in ("min", "uq", "ukv", "mout")]),
              ("sgu1", [("sin", 1), ("sout", 1)]), ("mla1", [(t, 1) for t in ("min", "uq", "ukv", "mout")])]
    sends = {}
    behind = mod_g
    for gname, items in groups:
        blocks = [_my_half(mixer_w[t][j], ci) for t, j in items]
        sends[gname] = _send_start(blocks, [_own_slot(b, dev) for b in blocks], behind, True, f"send_{gname}")
        behind = sends[gname]["token"]

    def arrive(gname, after):
        lands = _send_wait(sends[gname], after, f"arrive_{gname}")
        gathered_w.update(zip(dict(groups)[gname], lands))
    mod = lax.dynamic_index_in_dim(mod_all, dev, 1, keepdims=False)
    shift = [mod[i:i + 1, :D] for i in range(depth)]
    scale = [mod[i:i + 1, D:2 * D] for i in range(depth)]
    gate = [mod[i:i + 1, 2 * D:] for i in range(depth)]

    cos_t, sin_t = _rope_tables(S)
    b_bc = jnp.broadcast_to(sgu_b_s[:, :, :, None], sgu_b_s.shape + (LANE,))
    w_sT = jnp.swapaxes(sgu_w_s, 2, 3)

    saved = []
    xs = x0
    for i in range(depth):
        j = i // 2
        tag = f"l{i}"
        h = _pre_fwd(xs, pre_g[i:i + 1], scale[i], shift[i], f"pre_fwd_{tag}", after=(behind,) if i == 0 else ())
        if i % 2 == 0:
            if j > 0:
                arrive(f"sgu{j}", h)
            uvz = _mm(h, weight("sin", j), b_sharded=True, name=f"sgu_in_{tag}")
            y = _sgu_gate_fwd(uvz, sgu_norm_g[j:j + 1], sgu_w_s[j], b_bc[j], f"sgu_gate_fwd_{tag}")
            if j == 0:
                arrive("sout0", y)
            out = _mm(y, weight("sout", j), name=f"sgu_out_{tag}")
            saved.append(dict(x=xs, h=h, uvz=uvz, y=y, out=out))
        else:
            arrive(f"mla{j}", h)
            p = _mm(h, weight("min", j), b_sharded=True, name=f"mla_in_{tag}")
            cqn, ckvn = _mla_mid_fwd(p, q_gain[j], kv_gain[j], f"mla_mid_fwd_{tag}")
            q = _mm(cqn, weight("uq", j), b_sharded=True, name=f"mla_uq_{tag}")
            kv = _mm(ckvn, weight("ukv", j), b_sharded=True, name=f"mla_ukv_{tag}")
            Q, K, V, VT = _mla_pack(q, kv, p, cos_t, sin_t, f"mla_pack_{tag}")
            o, lse = _attn_fwd_t(Q, K, VT, f"attn_fwd_{tag}")
            y = _mla_gate_fwd(o, p, f"mla_gate_fwd_{tag}")
            out = _mm(y, weight("mout", j), name=f"mla_out_{tag}")
            saved.append(dict(x=xs, h=h, p=p, cqn=cqn, ckvn=ckvn, Q=Q, K=K, V=V, o=o, lse=lse, y=y, out=out))
        xs = _post_fwd(xs, out, gate[i], post_g[i:i + 1], f"post_fwd_{tag}")

    dx, loss_part = _loss_grad(xs, target, "loss")
    loss = lax.psum(loss_part[0, 0], ("x", "y", "c"))

    dmod = [None] * depth
    d_pre_g = [None] * depth
    d_post_g = [None] * depth
    d_sgu = [None] * 2
    d_mla = [None] * 2
    kinds = ("sgu_w_in", "sgu_w_out", "mla_w_in", "mla_w_uq", "mla_w_ukv", "mla_w_out")
    halves = dict.fromkeys(kinds)
    in_flight = []

    def send_grads(items, label):
        slices = [dw.reshape(8, -1, dw.shape[-1]) for _, _, dw in items]
        lands = [_own_slot(lax.dynamic_index_in_dim(s, dev, 0, keepdims=False), dev) for s in slices]
        started = _send_start(slices, lands, None, False, f"send_{label}")
        in_flight.append((started, [(kind, layer) for kind, layer, _ in items], label))
        return (started["token"],)

    def collect(count, after):
        for _ in range(count):
            started, keys, label = in_flight.pop(0)
            lands = _send_wait(started, after, f"arrive_{label}")
            for (kind, layer), r in zip(keys, lands):
                halves[kind] = _sum_into_half(r, halves[kind], layer, ci, 2, f"sum_{kind}_{layer}")

    for i in reversed(range(depth)):
        j = i // 2
        tag = f"l{i}"
        sv = saved[i]
        older = len(in_flight)
        dy, dgate, d_post_g[i] = _post_bwd(dx, sv["out"], gate[i], post_g[i:i + 1], f"post_bwd_{tag}")
        if i % 2 == 0:
            dw_out = _mm(sv["y"], dy, ta=True, out_dtype=BF16, name=f"sgu_out_dw_{tag}")
            sent = send_grads([("sgu_w_out", j, dw_out)], f"{tag}_out")
            dyv = _mm(dy, weight("sout", j), tb=True, after=sent, name=f"sgu_out_dx_{tag}")
            duvz, dng, dws, dbs = _sgu_gate_bwd(sv["uvz"], dyv, sgu_norm_g[j:j + 1], sgu_w_s[j], w_sT[j], b_bc[j],
                                                f"sgu_gate_bwd_{tag}")
            sent = ()
            if i > 0:
                dw_in = _mm(sv["h"], duvz, ta=True, out_sharded=True, out_dtype=BF16, tk=4096,
                            name=f"sgu_in_dw_{tag}")
                sent = send_grads([("sgu_w_in", j, dw_in)], f"{tag}_in")
            dh = _mm(duvz, weight("sin", j), tb=True, b_sharded=True, tk=3072, after=sent, name=f"sgu_in_dx_{tag}")
            d_sgu[j] = dict(norm_g=dng, w_s=dws, b_s=dbs[:, :, 0], duvz=duvz)
        else:
            dw_out = _mm(sv["y"], dy, ta=True, out_dtype=BF16, name=f"mla_out_dw_{tag}")
            sent = send_grads([("mla_w_out", j, dw_out)], f"{tag}_out")
            dyv = _mm(dy, weight("mout", j), tb=True, after=sent, name=f"mla_out_dx_{tag}")
            p = sv["p"]
            do = _mla_gate_bwd(dyv, p, f"mla_gate_bwd_{tag}")
            dQ, dK, dV = _attn_bwd(sv["Q"], sv["K"], sv["V"], sv["o"], do, sv["lse"], f"attn_bwd_{tag}")
            dq, dkv, dkr = _mla_unpack(dQ, dK, dV, cos_t, sin_t, f"mla_unpack_{tag}")
            dw_uq = _mm(sv["cqn"], dq, ta=True, out_sharded=True, out_dtype=BF16, name=f"mla_uq_dw_{tag}")
            dcqn = _mm(dq, weight("uq", j), tb=True, b_sharded=True, name=f"mla_uq_dx_{tag}")
            dw_ukv = _mm(sv["ckvn"], dkv, ta=True, out_sharded=True, out_dtype=BF16, name=f"mla_ukv_dw_{tag}")
            dckvn = _mm(dkv, weight("ukv", j), tb=True, b_sharded=True, name=f"mla_ukv_dx_{tag}")
            dp, dqg, dkvg = _mla_mid_bwd(p, dcqn, dckvn, dkr, dyv, sv["o"], q_gain[j], kv_gain[j], f"mla_mid_bwd_{tag}")
            dw_in = _mm(sv["h"], dp, ta=True, out_sharded=True, out_dtype=BF16, name=f"mla_in_dw_{tag}")
            sent = send_grads([("mla_w_in", j, dw_in), ("mla_w_uq", j, dw_uq[:, :Q_RANK]), ("mla_w_ukv", j, dw_ukv)],
                              f"{tag}_in")
            dh = _mm(dp, weight("min", j), tb=True, b_sharded=True, after=sent, name=f"mla_in_dx_{tag}")
            d_mla[j] = dict(qg=dqg[0, :Q_RANK], kvg=dkvg[0])
        dx, dshift, dscale, d_pre_g[i] = _pre_bwd(dh, sv["x"], dx, pre_g[i:i + 1], scale[i], f"pre_bwd_{tag}")
        dmod[i] = jnp.concatenate([dshift, dscale, dgate], axis=1)
        collect(older, dx)
    grad_x = dx.reshape(x.shape)

    parts = [jnp.concatenate(dmod, axis=0), jnp.concatenate(d_pre_g, axis=0), jnp.concatenate(d_post_g, axis=0),
             jnp.stack([d["norm_g"][0] for d in d_sgu]), jnp.stack([d["w_s"] for d in d_sgu]),
             jnp.stack([d["b_s"] for d in d_sgu]), jnp.stack([d["qg"] for d in d_mla]),
             jnp.stack([d["kvg"] for d in d_mla])]
    sizes = [int(np.prod(t.shape)) for t in parts]
    packed = _pack_rows(parts)
    packed_all, dmod_all = _all_gather8([packed, parts[0]], "gather_small_grads")
    total = _sum_slots(packed_all, "sum_small_grads").reshape(-1)
    offs = np.concatenate([[0], np.cumsum(sizes)])
    pieces = [total[int(offs[t]):int(offs[t + 1])].reshape(parts[t].shape) for t in range(len(parts))]
    g_ada_b, g_pre_g, g_post_g, g_norm_g, g_w_s, g_b_s, g_qg_full, g_kvg_full = pieces
    g_qg = lax.dynamic_slice_in_dim(g_qg_full, chip * qn_w, qn_w, axis=1)
    g_kvg = lax.dynamic_slice_in_dim(g_kvg_full, chip * kvn_w, kvn_w, axis=1)
    dmod_cols = jnp.stack([lax.dynamic_slice_in_dim(dmod_all[:, i], chip * cols, cols, axis=1) for i in range(depth)])
    dmod_cols = jnp.pad(dmod_cols, ((0, 0), (0, LANE - 8), (0, 0)))

    dw_in0 = _mm(saved[0]["h"], d_sgu[0]["duvz"], ta=True, out_sharded=True, out_dtype=BF16, tk=4096,
                 after=(packed_all,), name="sgu_in_dw_l0")
    sent = send_grads([("sgu_w_in", 0, dw_in0)], "l0_in")
    g_ada_w = _ada_grad(jnp.pad(c_all.T, ((0, 0), (0, LANE - 8))), dmod_cols, "ada_grad", after=sent)

    wnames = ["ada_w", "ada_b", "pre_g", "post_g", "sgu_w_in", "sgu_norm_g", "sgu_w_s", "sgu_b_s", "sgu_w_out",
              "mla_w_in", "mla_q_norm_g", "mla_kv_norm_g", "mla_w_uq", "mla_w_ukv", "mla_w_out"]
    weights = dict(zip(wnames, [ada_w, ada_b, pre_g, post_g, sgu_w_in, sgu_norm_g, sgu_w_s, sgu_b_s, sgu_w_out,
                                mla_w_in, mla_q_norm_g, mla_kv_norm_g, mla_w_uq, mla_w_ukv, mla_w_out]))
    ms = dict(zip(wnames, [m_ada_w, m_ada_b, m_pre_g, m_post_g, m_sgu_w_in, m_sgu_norm_g, m_sgu_w_s, m_sgu_b_s,
                           m_sgu_w_out, m_mla_w_in, m_mla_q_norm_g, m_mla_kv_norm_g, m_mla_w_uq, m_mla_w_ukv,
                           m_mla_w_out]))
    vs = dict(zip(wnames, [v_ada_w, v_ada_b, v_pre_g, v_post_g, v_sgu_w_in, v_sgu_norm_g, v_sgu_w_s, v_sgu_b_s,
                           v_sgu_w_out, v_mla_w_in, v_mla_q_norm_g, v_mla_kv_norm_g, v_mla_w_uq, v_mla_w_ukv,
                           v_mla_w_out]))
    grads = dict(ada_w=g_ada_w, ada_b=g_ada_b, pre_g=g_pre_g, post_g=g_post_g, sgu_norm_g=g_norm_g, sgu_w_s=g_w_s,
                 sgu_b_s=g_b_s, mla_q_norm_g=g_qg, mla_kv_norm_g=g_kvg)
    stepped = {}

    def step(nm):
        grads[nm] = grads[nm].reshape(weights[nm].shape)
        stepped[nm] = _adamw(weights[nm], grads[nm], ms[nm], vs[nm], f"adamw_{nm}")

    for nm in wnames:
        if nm in grads:
            step(nm)
    early = [kind for kind in kinds if kind != "sgu_w_in"]
    collect(len(in_flight) - 1, stepped["ada_w"][0])
    for kind, g in zip(early, _pair_swap([halves[kind] for kind in early], "swap_grads")):
        grads[kind] = g
        step(kind)
    collect(len(in_flight), stepped[early[-1]][0])
    grads["sgu_w_in"], = _pair_swap([halves["sgu_w_in"]], "swap_grads_last")
    step("sgu_w_in")
    return (loss, grad_x, *[grads[nm] for nm in wnames], *[stepped[nm][0] for nm in wnames],
            *[stepped[nm][1] for nm in wnames], *[stepped[nm][2] for nm in wnames])
```

```python
import math

import jax
import jax.numpy as jnp
import numpy as np
from jax import lax
from jax.experimental import pallas as pl
from jax.experimental.pallas import tpu as pltpu

F32 = jnp.float32
BF16 = jnp.bfloat16
MESH = pl.DeviceIdType.MESH

NORM_EPS = 1e-6
CHUNK = 64
SGU_BLOCK = 128
SGU_GROUPS = 16
HEADS = 16
NOPE = 128
ROPE = 64
VDIM = 128
QK_PAD = 256
Q_RANK = 448
Q_RANK_PAD = 512
KV_RANK = 512
ROPE_THETA = 10000.0
ATTN_SCALE = (NOPE + ROPE) ** -0.5
LOG2E = 1.4426950408889634
Q_FOLD = ATTN_SCALE * LOG2E
ATTN_TILE = 512

ADAM_LR = 0.001
ADAM_B1 = 0.9
ADAM_B2 = 0.999
ADAM_EPS = 1e-08
ADAM_WD = 0.01
ADAM_STEP = 10

LANE = 128
VMEM_LIMIT = 48 * 1024 * 1024

NN = (((1,), (0,)), ((), ()))
NT = (((1,), (1,)), ((), ()))
TN = (((0,), (0,)), ((), ()))


def _params(*sem):
    return pltpu.CompilerParams(dimension_semantics=sem, vmem_limit_bytes=VMEM_LIMIT)


def _row_tile(rows, row_bytes, target_bytes=1 << 20):
    if rows * row_bytes <= target_bytes or rows % 16:
        return rows
    best = 16
    t = 16
    while t <= rows:
        if rows % t == 0 and t * row_bytes <= target_bytes:
            best = t
        t += 16
    return best


def _fit(dim, target):
    if dim <= target:
        return dim
    t = (target // LANE) * LANE
    while t > LANE and dim % t:
        t -= LANE
    return t


def _gelu(x):
    return 0.5 * x * (1.0 + lax.erf(x * 0.7071067811865476))


def _gelu_and_grad(x):
    cdf = 0.5 * (1.0 + lax.erf(x * 0.7071067811865476))
    return x * cdf, cdf + x * jnp.exp(-0.5 * x * x) * 0.3989422804014327


def _mm(a, b, *, ta=False, tb=False, b_sharded=False, out_sharded=False, out_dtype=F32,
        tm=1024, tn=1024, tk=2048, after=(), name):
    if ta:
        K, M = a.shape
    else:
        M, K = a.shape
    if b_sharded:
        shards, rows, Cs = b.shape
        b_shape = (rows, shards * Cs)
    else:
        b_shape = b.shape
    if tb:
        N, K2 = b_shape
    else:
        K2, N = b_shape
    assert K == K2, (a.shape, b.shape, ta, tb)
    n_lim = Cs if (b_sharded and not tb) else (N // 4 if out_sharded else N)
    k_lim = Cs if (b_sharded and tb) else K
    tm, tn, tk = _fit(M, tm), _fit(n_lim, tn), _fit(k_lim, tk)
    assert M % tm == 0 and n_lim % tn == 0 and k_lim % tk == 0, (M, N, K, tm, tn, tk)
    nk = K // tk
    nb_n = n_lim // tn
    nb_k = k_lim // tk
    dims = (((0 if ta else 1,), (1 if tb else 0,)), ((), ()))

    def body(a_ref, b_ref, *rest):
        o_ref, *scratch = rest[len(after):]
        prod = lax.dot_general(a_ref[...].astype(BF16), b_ref[...].astype(BF16), dims,
                               preferred_element_type=F32)
        if nk == 1:
            o_ref[...] = prod.astype(out_dtype)
        else:
            acc_ref, = scratch
            k = pl.program_id(2)

            @pl.when(k == 0)
            def _():
                acc_ref[...] = prod

            @pl.when(k > 0)
            def _():
                acc_ref[...] += prod

            @pl.when(k == nk - 1)
            def _():
                o_ref[...] = acc_ref[...].astype(out_dtype)

    a_spec = (pl.BlockSpec((tk, tm), lambda i, j, k: (k, i)) if ta
              else pl.BlockSpec((tm, tk), lambda i, j, k: (i, k)))
    if b_sharded and tb:
        b_spec = pl.BlockSpec((None, tn, tk), lambda i, j, k: (k // nb_k, j, k % nb_k))
    elif b_sharded:
        b_spec = pl.BlockSpec((None, tk, tn), lambda i, j, k: (j // nb_n, k, j % nb_n))
    elif tb:
        b_spec = pl.BlockSpec((tn, tk), lambda i, j, k: (j, k))
    else:
        b_spec = pl.BlockSpec((tk, tn), lambda i, j, k: (k, j))
    if out_sharded:
        out_shape = jax.ShapeDtypeStruct((4, M, N // 4), out_dtype)
        out_spec = pl.BlockSpec((None, tm, tn), lambda i, j, k: (j // nb_n, i, j % nb_n))
    else:
        out_shape = jax.ShapeDtypeStruct((M, N), out_dtype)
        out_spec = pl.BlockSpec((tm, tn), lambda i, j, k: (i, j))
    return pl.pallas_call(
        body, name=name,
        out_shape=out_shape,
        grid=(M // tm, N // tn, nk),
        in_specs=[a_spec, b_spec] + [pl.BlockSpec(memory_space=pl.ANY)] * len(after),
        out_specs=out_spec,
        scratch_shapes=[] if nk == 1 else [pltpu.VMEM((tm, tn), F32)],
        compiler_params=_params("parallel", "parallel", "arbitrary"),
    )(a, b, *after)


def _split_bf16(v):
    hi = v.astype(BF16)
    lo = (v - hi.astype(F32)).astype(BF16)
    return hi, lo


def _dot3(a, b, dims):
    a_hi, a_lo = _split_bf16(a)
    b_hi, b_lo = _split_bf16(b)
    out = lax.dot_general(a_hi, b_hi, dims, preferred_element_type=F32)
    out += lax.dot_general(a_lo, b_hi, dims, preferred_element_type=F32)
    out += lax.dot_general(a_hi, b_lo, dims, preferred_element_type=F32)
    return out


def _ada_mod(c_all, ada_w, ada_b_cols, name):
    L, D, cols = ada_w.shape
    B = c_all.shape[0]
    tn = 512 if cols % 512 == 0 else cols

    def body(c_ref, w_ref, b_ref, o_ref):
        cv = c_ref[...]
        cond = cv * jax.nn.sigmoid(cv)
        o_ref[...] = _dot3(cond, w_ref[...], NN) + b_ref[...]

    return pl.pallas_call(
        body, name=name,
        out_shape=jax.ShapeDtypeStruct((L, B, cols), F32),
        grid=(L, cols // tn),
        in_specs=[pl.BlockSpec((B, D), lambda l, j: (0, 0)),
                  pl.BlockSpec((None, D, tn), lambda l, j: (l, 0, j)),
                  pl.BlockSpec((None, 1, tn), lambda l, j: (l, 0, j))],
        out_specs=pl.BlockSpec((None, B, tn), lambda l, j: (l, 0, j)),
        compiler_params=_params("parallel", "parallel"),
    )(c_all, ada_w, ada_b_cols.reshape(L, 1, cols))


def _ada_grad(c_t, dmod_cols, name, after=()):
    L, B, cols = dmod_cols.shape
    D = c_t.shape[0]
    tn = 512 if cols % 512 == 0 else cols

    def body(c_ref, d_ref, *rest):
        o_ref = rest[-1]
        cv = c_ref[...]
        cond = cv * jax.nn.sigmoid(cv)
        o_ref[...] = _dot3(cond, d_ref[...], NN)

    return pl.pallas_call(
        body, name=name,
        out_shape=jax.ShapeDtypeStruct((L, D, cols), F32),
        grid=(L, cols // tn),
        in_specs=[pl.BlockSpec((D, B), lambda l, j: (0, 0)),
                  pl.BlockSpec((None, B, tn), lambda l, j: (l, 0, j))] + [pl.BlockSpec(memory_space=pl.ANY)] * len(after),
        out_specs=pl.BlockSpec((None, D, tn), lambda l, j: (l, 0, j)),
        compiler_params=_params("parallel", "parallel"),
    )(c_t, dmod_cols, *after)


def _row_spec(ts, width):
    return pl.BlockSpec((ts, width), lambda i: (i, 0))


def _vec_spec(width):
    return pl.BlockSpec((1, width), lambda i: (0, 0))


def _pre_fwd(x, pre_g, scale, shift, name, after=()):
    S, D = x.shape
    ts = min(256, S)

    def body(x_ref, g_ref, sc_ref, sh_ref, *rest):
        h_ref = rest[-1]
        xv = x_ref[...]
        r = lax.rsqrt(jnp.mean(xv * xv, axis=-1, keepdims=True) + NORM_EPS)
        h_ref[...] = ((xv * r * g_ref[...]) * (1.0 + sc_ref[...]) + sh_ref[...]).astype(BF16)

    return pl.pallas_call(
        body, name=name, out_shape=jax.ShapeDtypeStruct((S, D), BF16), grid=(S // ts,),
        in_specs=[_row_spec(ts, D), _vec_spec(D), _vec_spec(D), _vec_spec(D)]
        + [pl.BlockSpec(memory_space=pl.ANY)] * len(after),
        out_specs=_row_spec(ts, D), compiler_params=_params("parallel"),
    )(x, pre_g, scale, shift, *after)


def _pre_bwd(dh, x, dx_res, pre_g, scale, name):
    S, D = x.shape
    ts = min(256, S)

    def body(dh_ref, x_ref, dr_ref, g_ref, sc_ref, dx_ref, dsh_ref, dsc_ref, dg_ref):
        @pl.when(pl.program_id(0) == 0)
        def _():
            dsh_ref[...] = jnp.zeros_like(dsh_ref)
            dsc_ref[...] = jnp.zeros_like(dsc_ref)
            dg_ref[...] = jnp.zeros_like(dg_ref)

        dh = dh_ref[...]
        xv = x_ref[...]
        g = g_ref[...]
        one_sc = 1.0 + sc_ref[...]
        r = lax.rsqrt(jnp.mean(xv * xv, axis=-1, keepdims=True) + NORM_EPS)
        xn = xv * r
        dsh_ref[...] += jnp.sum(dh, axis=0, keepdims=True)
        dsc_ref[...] += jnp.sum(dh * (xn * g), axis=0, keepdims=True)
        dg_ref[...] += jnp.sum(dh * one_sc * xn, axis=0, keepdims=True)
        dxn = dh * one_sc * g
        dx_ref[...] = dr_ref[...] + r * (dxn - xn * jnp.mean(dxn * xn, axis=-1, keepdims=True))

    vec = jax.ShapeDtypeStruct((1, D), F32)
    return pl.pallas_call(
        body, name=name, out_shape=(jax.ShapeDtypeStruct((S, D), F32), vec, vec, vec), grid=(S // ts,),
        in_specs=[_row_spec(ts, D), _row_spec(ts, D), _row_spec(ts, D), _vec_spec(D), _vec_spec(D)],
        out_specs=(_row_spec(ts, D), _vec_spec(D), _vec_spec(D), _vec_spec(D)),
        compiler_params=_params("arbitrary"),
    )(dh, x, dx_res, pre_g, scale)


def _post_fwd(x, y, gate, post_g, name):
    S, D = x.shape
    ts = min(256, S)

    def body(x_ref, y_ref, gt_ref, g_ref, o_ref):
        yv = y_ref[...]
        r = lax.rsqrt(jnp.mean(yv * yv, axis=-1, keepdims=True) + NORM_EPS)
        o_ref[...] = x_ref[...] + gt_ref[...] * (yv * r * g_ref[...])

    return pl.pallas_call(
        body, name=name, out_shape=jax.ShapeDtypeStruct((S, D), F32), grid=(S // ts,),
        in_specs=[_row_spec(ts, D), _row_spec(ts, D), _vec_spec(D), _vec_spec(D)],
        out_specs=_row_spec(ts, D), compiler_params=_params("parallel"),
    )(x, y, gate, post_g)


def _post_bwd(dx, y, gate, post_g, name):
    S, D = y.shape
    ts = min(256, S)

    def body(dx_ref, y_ref, gt_ref, g_ref, dy_ref, dgt_ref, dg_ref):
        @pl.when(pl.program_id(0) == 0)
        def _():
            dgt_ref[...] = jnp.zeros_like(dgt_ref)
            dg_ref[...] = jnp.zeros_like(dg_ref)

        dxv = dx_ref[...]
        yv = y_ref[...]
        g = g_ref[...]
        gt = gt_ref[...]
        r = lax.rsqrt(jnp.mean(yv * yv, axis=-1, keepdims=True) + NORM_EPS)
        yn = yv * r
        dgt_ref[...] += jnp.sum(dxv * (yn * g), axis=0, keepdims=True)
        dg_ref[...] += jnp.sum(dxv * gt * yn, axis=0, keepdims=True)
        dyn = dxv * gt * g
        dy_ref[...] = (r * (dyn - yn * jnp.mean(dyn * yn, axis=-1, keepdims=True))).astype(BF16)

    vec = jax.ShapeDtypeStruct((1, D), F32)
    return pl.pallas_call(
        body, name=name, out_shape=(jax.ShapeDtypeStruct((S, D), BF16), vec, vec), grid=(S // ts,),
        in_specs=[_row_spec(ts, D), _row_spec(ts, D), _vec_spec(D), _vec_spec(D)],
        out_specs=(_row_spec(ts, D), _vec_spec(D), _vec_spec(D)),
        compiler_params=_params("arbitrary"),
    )(dx, y, gate, post_g)


def _loss_grad(xf, target, name):
    S, D = xf.shape
    ts = min(256, S)

    def body(x_ref, t_ref, dx_ref, l_ref):
        @pl.when(pl.program_id(0) == 0)
        def _():
            l_ref[...] = jnp.zeros_like(l_ref)

        e = x_ref[...] - t_ref[...]
        dx_ref[...] = e * (1.0 / D)
        row = jnp.sum(e * e, axis=1, keepdims=True) * (1.0 / D)
        l_ref[...] += 0.5 * jnp.sum(row, axis=0, keepdims=True)

    return pl.pallas_call(
        body, name=name,
        out_shape=(jax.ShapeDtypeStruct((S, D), F32), jax.ShapeDtypeStruct((1, 1), F32)), grid=(S // ts,),
        in_specs=[_row_spec(ts, D), _row_spec(ts, D)],
        out_specs=(_row_spec(ts, D), pl.BlockSpec((1, 1), lambda i: (0, 0))),
        compiler_params=_params("arbitrary"),
    )(xf, target)


def _chunk_mask(transposed=False):
    row = lax.broadcasted_iota(jnp.int32, (SGU_BLOCK, SGU_BLOCK), 0) // CHUNK
    col = lax.broadcasted_iota(jnp.int32, (SGU_BLOCK, SGU_BLOCK), 1) // CHUNK
    return (row <= col) if transposed else (col <= row)


def _sgu_gate_fwd(uvz, norm_g, w_s, b_bc, name):
    S, E3 = uvz.shape
    E = E3 // 3
    T = SGU_BLOCK
    gd = E // SGU_GROUPS

    def body(uvz_ref, ng_ref, ws_ref, bb_ref, y_ref, v_scr):
        gv = _gelu(uvz_ref[:, E:2 * E])
        mu = jnp.mean(gv, axis=-1, keepdims=True)
        xc = gv - mu
        rstd = lax.rsqrt(jnp.mean(xc * xc, axis=-1, keepdims=True) + NORM_EPS)
        v_scr[...] = (xc * rstd * ng_ref[...]).astype(BF16)
        mask = _chunk_mask()
        for g in range(SGU_GROUPS):
            sl = slice(g * gd, (g + 1) * gd)
            wg = jnp.where(mask, ws_ref[g], 0.0).astype(BF16)
            vm = lax.dot_general(wg, v_scr[:, sl], NN, preferred_element_type=F32)
            vm = vm + jnp.tile(bb_ref[g], (1, gd // LANE))
            z = uvz_ref[:, 2 * E + g * gd:2 * E + (g + 1) * gd]
            y_ref[:, sl] = (_gelu(uvz_ref[:, sl]) * vm * (z * jax.nn.sigmoid(z))).astype(BF16)

    return pl.pallas_call(
        body, name=name, out_shape=jax.ShapeDtypeStruct((S, E), BF16), grid=(S // T,),
        in_specs=[_row_spec(T, E3), _vec_spec(E),
                  pl.BlockSpec((SGU_GROUPS, T, T), lambda i: (0, 0, 0)),
                  pl.BlockSpec((SGU_GROUPS, T, LANE), lambda i: (0, 0, 0))],
        out_specs=_row_spec(T, E),
        scratch_shapes=[pltpu.VMEM((T, E), BF16)],
        compiler_params=_params("parallel"),
    )(uvz, norm_g, w_s, b_bc)


def _sgu_gate_bwd(uvz, dyv, norm_g, w_s, w_sT, b_bc, name):
    S, E3 = uvz.shape
    E = E3 // 3
    T = SGU_BLOCK
    gd = E // SGU_GROUPS

    def body(uvz_ref, dyv_ref, ng_ref, ws_ref, wst_ref, bb_ref,
             d_ref, dng_ref, dws_ref, dbs_ref, vhat_scr, dv_scr, vgrad_scr):
        @pl.when(pl.program_id(0) == 0)
        def _():
            dng_ref[...] = jnp.zeros_like(dng_ref)
            dws_ref[...] = jnp.zeros_like(dws_ref)
            dbs_ref[...] = jnp.zeros_like(dbs_ref)

        gv, vgrad_scr[...] = _gelu_and_grad(uvz_ref[:, E:2 * E])
        mu = jnp.mean(gv, axis=-1, keepdims=True)
        xc = gv - mu
        rstd = lax.rsqrt(jnp.mean(xc * xc, axis=-1, keepdims=True) + NORM_EPS)
        vhat_scr[...] = xc * rstd
        mask = _chunk_mask()
        mask_t = _chunk_mask(transposed=True)
        for g in range(SGU_GROUPS):
            sl = slice(g * gd, (g + 1) * gd)
            u_pre = uvz_ref[:, sl]
            z = uvz_ref[:, 2 * E + g * gd:2 * E + (g + 1) * gd]
            dy = dyv_ref[:, sl]
            u, u_grad = _gelu_and_grad(u_pre)
            sig = jax.nn.sigmoid(z)
            sz = z * sig
            vg = (vhat_scr[:, sl] * ng_ref[:, sl]).astype(BF16)
            wg = jnp.where(mask, ws_ref[g], 0.0).astype(BF16)
            vm = lax.dot_general(wg, vg, NN, preferred_element_type=F32)
            vm = vm + jnp.tile(bb_ref[g], (1, gd // LANE))
            dy_u = dy * u
            d_ref[:, sl] = (dy * vm * sz * u_grad).astype(BF16)
            d_ref[:, 2 * E + g * gd:2 * E + (g + 1) * gd] = (
                dy_u * vm * (sig * (1.0 + z * (1.0 - sig)))).astype(BF16)
            dvm = dy_u * sz
            dvm_b = dvm.astype(BF16)
            dws_ref[g] += jnp.where(mask, lax.dot_general(dvm_b, vg, NT, preferred_element_type=F32), 0.0)
            dbs_ref[g] += jnp.broadcast_to(jnp.sum(dvm, axis=1, keepdims=True), (T, LANE))
            wgt = jnp.where(mask_t, wst_ref[g], 0.0).astype(BF16)
            dv_scr[:, sl] = lax.dot_general(wgt, dvm_b, NN, preferred_element_type=F32)
        dv = dv_scr[...]
        vhat = vhat_scr[...]
        dng_ref[...] += jnp.sum(dv * vhat, axis=0, keepdims=True)
        dvh = dv * ng_ref[...]
        dgv = rstd * (dvh - jnp.mean(dvh, axis=-1, keepdims=True)
                      - vhat * jnp.mean(dvh * vhat, axis=-1, keepdims=True))
        d_ref[:, E:2 * E] = (dgv * vgrad_scr[...]).astype(BF16)

    wspec = pl.BlockSpec((SGU_GROUPS, T, T), lambda i: (0, 0, 0))
    bspec = pl.BlockSpec((SGU_GROUPS, T, LANE), lambda i: (0, 0, 0))
    return pl.pallas_call(
        body, name=name,
        out_shape=(jax.ShapeDtypeStruct((S, E3), BF16), jax.ShapeDtypeStruct((1, E), F32),
                   jax.ShapeDtypeStruct((SGU_GROUPS, T, T), F32),
                   jax.ShapeDtypeStruct((SGU_GROUPS, T, LANE), F32)),
        grid=(S // T,),
        in_specs=[_row_spec(T, E3), _row_spec(T, E), _vec_spec(E), wspec, wspec, bspec],
        out_specs=(_row_spec(T, E3), _vec_spec(E), wspec, bspec),
        scratch_shapes=[pltpu.VMEM((T, E), F32), pltpu.VMEM((T, E), F32), pltpu.VMEM((T, E), F32)],
        compiler_params=_params("arbitrary"),
    )(uvz, dyv, norm_g, w_s, w_sT, b_bc)


MLA_WIDTH = HEADS * VDIM
P_LATENT = Q_RANK + KV_RANK + ROPE
P_WIDTH = P_LATENT + MLA_WIDTH


def _swap_halves(v):
    lane = lax.broadcasted_iota(jnp.int32, v.shape, 1)
    return jnp.where(lane % ROPE < ROPE // 2, pltpu.roll(v, LANE - ROPE // 2, 1), pltpu.roll(v, ROPE // 2, 1))


def _low_lanes(rows):
    return lax.broadcasted_iota(jnp.int32, (rows, LANE), 1) < ROPE


def _latent_tiles(ref):
    return [ref[:, t * LANE:(t + 1) * LANE] for t in range(P_LATENT // LANE)]


def _split_latents(tiles, low):
    cq = jnp.concatenate(tiles[0:3] + [jnp.where(low, tiles[3], 0.0)], axis=1)
    rolled = [pltpu.roll(t, ROPE, 1) for t in tiles[3:8]]
    ckv = jnp.concatenate([jnp.where(low, rolled[t], rolled[t + 1]) for t in range(4)], axis=1)
    kr = jnp.where(low, rolled[4], 0.0)
    return cq, ckv, kr


def _mla_mid_fwd(p, qg, kvg, name):
    S, PW = p.shape
    ts = min(256, S)

    def body(p_ref, qg_ref, kvg_ref, cqn_ref, ckvn_ref):
        cq, ckv, _ = _split_latents(_latent_tiles(p_ref), _low_lanes(ts))
        r = lax.rsqrt(jnp.sum(cq * cq, axis=-1, keepdims=True) * (1.0 / Q_RANK) + NORM_EPS)
        cqn_ref[...] = (cq * r * qg_ref[...]).astype(BF16)
        r2 = lax.rsqrt(jnp.mean(ckv * ckv, axis=-1, keepdims=True) + NORM_EPS)
        ckvn_ref[...] = (ckv * r2 * kvg_ref[...]).astype(BF16)

    return pl.pallas_call(
        body, name=name,
        out_shape=(jax.ShapeDtypeStruct((S, Q_RANK_PAD), BF16), jax.ShapeDtypeStruct((S, KV_RANK), BF16)),
        grid=(S // ts,),
        in_specs=[_row_spec(ts, P_LATENT), _vec_spec(Q_RANK_PAD), _vec_spec(KV_RANK)],
        out_specs=(_row_spec(ts, Q_RANK_PAD), _row_spec(ts, KV_RANK)),
        compiler_params=_params("parallel"),
    )(p, qg, kvg)


def _mla_pack(q, kv, p, cos_t, sin_t, name):
    S = q.shape[0]
    ts = min(256, S)
    pair_w = 2 * (NOPE + ROPE)
    head_w = NOPE + VDIM

    def body(q_ref, kv_ref, kr_ref, cos_ref, sin_ref, qo_ref, ko_ref, vo_ref, vt_ref):
        cosv = cos_ref[...]
        sinv = sin_ref[...]
        low = _low_lanes(ts)
        kr = jnp.where(low, pltpu.roll(kr_ref[...], ROPE, 1), 0.0)
        kr = (kr * cosv + _swap_halves(kr) * sinv).astype(BF16)
        for pair in range(HEADS // 2):
            t0, t1, t2 = (q_ref[:, pair * pair_w + t * LANE:pair * pair_w + (t + 1) * LANE] for t in range(3))
            nope_b = jnp.where(low, pltpu.roll(t1, ROPE, 1), pltpu.roll(t2, ROPE, 1))
            ropes = jnp.where(low, t1, t2)
            roped = (ropes * cosv + _swap_halves(ropes) * sinv) * Q_FOLD
            qo_ref[2 * pair, :, 0:NOPE] = (t0 * Q_FOLD).astype(BF16)
            qo_ref[2 * pair, :, NOPE:QK_PAD] = jnp.where(low, roped, 0.0).astype(BF16)
            qo_ref[2 * pair + 1, :, 0:NOPE] = (nope_b * Q_FOLD).astype(BF16)
            qo_ref[2 * pair + 1, :, NOPE:QK_PAD] = jnp.where(low, pltpu.roll(roped, ROPE, 1), 0.0).astype(BF16)
        eye = _identity(VDIM)
        for h in range(HEADS):
            ko_ref[h, :, 0:NOPE] = kv_ref[:, h * head_w:h * head_w + NOPE].astype(BF16)
            ko_ref[h, :, NOPE:QK_PAD] = kr
            vh = kv_ref[:, h * head_w + NOPE:(h + 1) * head_w].astype(BF16)
            vo_ref[h] = vh
            vt_ref[h] = lax.dot_general(eye, vh, NT, preferred_element_type=F32).astype(BF16)

    T = min(ATTN_TILE, S)
    per_tile = T // ts
    return pl.pallas_call(
        body, name=name,
        out_shape=(jax.ShapeDtypeStruct((HEADS, S, QK_PAD), BF16), jax.ShapeDtypeStruct((HEADS, S, QK_PAD), BF16),
                   jax.ShapeDtypeStruct((HEADS, S, VDIM), BF16), jax.ShapeDtypeStruct((HEADS, S // T, VDIM, T), BF16)),
        grid=(S // ts,),
        in_specs=[_row_spec(ts, q.shape[1]), _row_spec(ts, kv.shape[1]),
                  pl.BlockSpec((ts, LANE), lambda i: (i, P_LATENT // LANE - 1)),
                  _row_spec(ts, LANE), _row_spec(ts, LANE)],
        out_specs=(pl.BlockSpec((HEADS, ts, QK_PAD), lambda i: (0, i, 0)),
                   pl.BlockSpec((HEADS, ts, QK_PAD), lambda i: (0, i, 0)),
                   pl.BlockSpec((HEADS, ts, VDIM), lambda i: (0, i, 0)),
                   pl.BlockSpec((HEADS, None, VDIM, ts), lambda i: (0, i // per_tile, 0, i % per_tile))),
        compiler_params=_params("parallel"),
    )(q, kv, p, cos_t, sin_t)


def _mla_unpack(dQ, dK, dV, cos_t, sin_t, name):
    S = dQ.shape[1]
    ts = min(256, S)
    pair_w = 2 * (NOPE + ROPE)
    head_w = NOPE + VDIM

    def body(dq_ref, dk_ref, dv_ref, cos_ref, sin_ref, q_ref, kv_ref, kr_ref):
        cosv = cos_ref[...]
        sinv = sin_ref[...]
        low = _low_lanes(ts)
        for pair in range(HEADS // 2):
            blk = dq_ref[2 * pair, :, NOPE:QK_PAD] + pltpu.roll(dq_ref[2 * pair + 1, :, NOPE:QK_PAD], ROPE, 1)
            ropes = blk * cosv - _swap_halves(blk) * sinv
            nope_b = pltpu.roll(dq_ref[2 * pair + 1, :, 0:NOPE], ROPE, 1)
            q_ref[:, pair * pair_w:pair * pair_w + LANE] = dq_ref[2 * pair, :, 0:NOPE].astype(BF16)
            q_ref[:, pair * pair_w + LANE:pair * pair_w + 2 * LANE] = jnp.where(low, ropes, nope_b).astype(BF16)
            q_ref[:, pair * pair_w + 2 * LANE:(pair + 1) * pair_w] = jnp.where(low, nope_b, ropes).astype(BF16)
        dkr = dk_ref[0, :, NOPE:QK_PAD]
        for h in range(1, HEADS):
            dkr = dkr + dk_ref[h, :, NOPE:QK_PAD]
        kr_ref[...] = dkr * cosv - _swap_halves(dkr) * sinv
        for h in range(HEADS):
            kv_ref[:, h * head_w:h * head_w + NOPE] = dk_ref[h, :, 0:NOPE].astype(BF16)
            kv_ref[:, h * head_w + NOPE:(h + 1) * head_w] = dv_ref[h].astype(BF16)

    return pl.pallas_call(
        body, name=name,
        out_shape=(jax.ShapeDtypeStruct((S, HEADS * (NOPE + ROPE)), BF16),
                   jax.ShapeDtypeStruct((S, HEADS * (NOPE + VDIM)), BF16),
                   jax.ShapeDtypeStruct((S, LANE), F32)),
        grid=(S // ts,),
        in_specs=[pl.BlockSpec((HEADS, ts, QK_PAD), lambda i: (0, i, 0)),
                  pl.BlockSpec((HEADS, ts, QK_PAD), lambda i: (0, i, 0)),
                  pl.BlockSpec((HEADS, ts, VDIM), lambda i: (0, i, 0)),
                  _row_spec(ts, LANE), _row_spec(ts, LANE)],
        out_specs=(_row_spec(ts, HEADS * (NOPE + ROPE)), _row_spec(ts, HEADS * (NOPE + VDIM)),
                   _row_spec(ts, LANE)),
        compiler_params=_params("parallel"),
    )(dQ, dK, dV, cos_t, sin_t)


def _mla_gate_fwd(o, p, name):
    S, W = o.shape
    ts = min(256, S)
    wb = P_LATENT

    def body(o_ref, z_ref, y_ref):
        z = z_ref[...]
        y_ref[...] = (o_ref[...] * (z * jax.nn.sigmoid(z))).astype(BF16)

    return pl.pallas_call(
        body, name=name, out_shape=jax.ShapeDtypeStruct((S, W), BF16), grid=(S // ts, W // wb),
        in_specs=[pl.BlockSpec((ts, wb), lambda i, j: (i, j)), pl.BlockSpec((ts, wb), lambda i, j: (i, j + 1))],
        out_specs=pl.BlockSpec((ts, wb), lambda i, j: (i, j)), compiler_params=_params("parallel", "parallel"),
    )(o, p)


def _mla_gate_bwd(dyv, p, name):
    S, W = dyv.shape
    ts = min(256, S)
    wb = P_LATENT

    def body(d_ref, z_ref, do_ref):
        z = z_ref[...]
        do_ref[...] = d_ref[...] * (z * jax.nn.sigmoid(z))

    return pl.pallas_call(
        body, name=name, out_shape=jax.ShapeDtypeStruct((S, W), F32), grid=(S // ts, W // wb),
        in_specs=[pl.BlockSpec((ts, wb), lambda i, j: (i, j)), pl.BlockSpec((ts, wb), lambda i, j: (i, j + 1))],
        out_specs=pl.BlockSpec((ts, wb), lambda i, j: (i, j)), compiler_params=_params("parallel", "parallel"),
    )(dyv, p)


def _mla_mid_bwd(p, dcqn, dckvn, dkr, dyv, o, qg, kvg, name):
    S, PW = p.shape
    W = o.shape[1]
    ts = min(256, S)
    nt = Q_RANK_PAD // LANE

    def rms_bwd(xv, dy, g, count):
        r = lax.rsqrt(jnp.sum(xv * xv, axis=-1, keepdims=True) * (1.0 / count) + NORM_EPS)
        xn = xv * r
        dg = jnp.sum(dy * xn, axis=0, keepdims=True)
        dxn = dy * g
        dx = r * (dxn - xn * (jnp.sum(dxn * xn, axis=-1, keepdims=True) * (1.0 / count)))
        return dx, dg

    def body(p_ref, dcq_ref, dckv_ref, dkr_ref, dyv_ref, o_ref, qg_ref, kvg_ref, dp_ref, dqg_ref, dkvg_ref):
        @pl.when(pl.program_id(0) == 0)
        def _():
            dqg_ref[...] = jnp.zeros_like(dqg_ref)
            dkvg_ref[...] = jnp.zeros_like(dkvg_ref)

        low = _low_lanes(ts)
        cq, ckv, _ = _split_latents(_latent_tiles(p_ref), low)
        dcq, dg = rms_bwd(cq, dcq_ref[...], qg_ref[...], Q_RANK)
        dqg_ref[...] += dg
        dckv, dg = rms_bwd(ckv, dckv_ref[...], kvg_ref[...], KV_RANK)
        dkvg_ref[...] += dg
        moved = [pltpu.roll(dckv[:, t * LANE:(t + 1) * LANE], ROPE, 1) for t in range(nt)]
        moved.append(pltpu.roll(dkr_ref[...], ROPE, 1))
        for t in range(nt - 1):
            dp_ref[:, t * LANE:(t + 1) * LANE] = dcq[:, t * LANE:(t + 1) * LANE].astype(BF16)
        dp_ref[:, (nt - 1) * LANE:nt * LANE] = jnp.where(low, dcq[:, (nt - 1) * LANE:nt * LANE], moved[0]).astype(BF16)
        for t in range(nt):
            dp_ref[:, (nt + t) * LANE:(nt + t + 1) * LANE] = jnp.where(low, moved[t], moved[t + 1]).astype(BF16)
        z = p_ref[:, P_LATENT:PW]
        sig = jax.nn.sigmoid(z)
        dp_ref[:, P_LATENT:PW] = (dyv_ref[...] * o_ref[...] * (sig * (1.0 + z * (1.0 - sig)))).astype(BF16)

    return pl.pallas_call(
        body, name=name,
        out_shape=(jax.ShapeDtypeStruct((S, PW), BF16), jax.ShapeDtypeStruct((1, Q_RANK_PAD), F32),
                   jax.ShapeDtypeStruct((1, KV_RANK), F32)),
        grid=(S // ts,),
        in_specs=[_row_spec(ts, PW), _row_spec(ts, Q_RANK_PAD), _row_spec(ts, KV_RANK), _row_spec(ts, LANE),
                  _row_spec(ts, W), _row_spec(ts, W), _vec_spec(Q_RANK_PAD), _vec_spec(KV_RANK)],
        out_specs=(_row_spec(ts, PW), _vec_spec(Q_RANK_PAD), _vec_spec(KV_RANK)),
        compiler_params=_params("arbitrary"),
    )(p, dcqn, dckvn, dkr, dyv, o, qg, kvg)


def _tile_mask(T):
    row = lax.broadcasted_iota(jnp.int32, (T, T), 0) // CHUNK
    col = lax.broadcasted_iota(jnp.int32, (T, T), 1) // CHUNK
    return col <= row


def _attn_bwd(Q, K, V, o, do, lse, name):
    H, S, _ = Q.shape
    T = min(ATTN_TILE, S)
    nq = S // T

    def body(q_ref, k_ref, v_ref, o_ref, do_ref, lse_ref, dq_ref, dk_ref, dv_ref, dk_scr, dv_scr, s_scr, dp_scr):
        ki = pl.program_id(1)

        @pl.when(ki == 0)
        def _():
            dq_ref[...] = jnp.zeros_like(dq_ref)

        dk_scr[...] = jnp.zeros_like(dk_scr)
        dv_scr[...] = jnp.zeros_like(dv_scr)
        k = k_ref[...]
        v = v_ref[...]

        def scores(i):
            rows = pl.ds(pl.multiple_of(i * T, T), T)
            s = lax.dot_general(q_ref[rows, :], k, NT, preferred_element_type=F32)
            dp = lax.dot_general(do_ref[rows, :].astype(BF16), v, NT, preferred_element_type=F32)
            return s, dp

        def grads(i, s, dp, masked):
            rows = pl.ds(pl.multiple_of(i * T, T), T)
            do_f = do_ref[rows, :]
            delta = jnp.sum(do_f * o_ref[rows, :], axis=1, keepdims=True)
            pr = jnp.exp2(s - lse_ref[rows, 0:1])
            if masked:
                pr = jnp.where(_tile_mask(T), pr, 0.0)
            dv_scr[...] += lax.dot_general(pr.astype(BF16), do_f.astype(BF16), TN, preferred_element_type=F32)
            ds = (pr * (dp - delta)).astype(BF16)
            dk_scr[...] += lax.dot_general(ds, q_ref[rows, :], TN, preferred_element_type=F32)
            dq_ref[rows, :] += lax.dot_general(ds, k, NN, preferred_element_type=F32) * ATTN_SCALE

        s_scr[...], dp_scr[...] = scores(ki)

        @pl.when(ki + 1 == nq)
        def _():
            grads(ki, s_scr[...], dp_scr[...], True)

        @pl.when(ki + 1 < nq)
        def _():
            def step(i, masked):
                nxt_s, nxt_dp = scores(i + 1)
                grads(i, s_scr[...], dp_scr[...], masked)
                s_scr[...] = nxt_s
                dp_scr[...] = nxt_dp

            def loop_step(i, carry):
                step(i, False)
                return carry

            step(ki, True)
            lax.fori_loop(ki + 1, nq - 1, loop_step, 0)
            grads(nq - 1, s_scr[...], dp_scr[...], False)

        dk_ref[...] = dk_scr[...] * (1.0 / LOG2E)
        dv_ref[...] = dv_scr[...]

    return pl.pallas_call(
        body, name=name,
        out_shape=(jax.ShapeDtypeStruct((H, S, QK_PAD), F32), jax.ShapeDtypeStruct((H, S, QK_PAD), F32),
                   jax.ShapeDtypeStruct((H, S, VDIM), F32)),
        grid=(H, nq),
        in_specs=[pl.BlockSpec((None, S, QK_PAD), lambda h, j: (h, 0, 0)),
                  pl.BlockSpec((None, T, QK_PAD), lambda h, j: (h, j, 0)),
                  pl.BlockSpec((None, T, VDIM), lambda h, j: (h, j, 0)),
                  pl.BlockSpec((S, VDIM), lambda h, j: (0, h)),
                  pl.BlockSpec((S, VDIM), lambda h, j: (0, h)),
                  pl.BlockSpec((None, S, LANE), lambda h, j: (h, 0, 0))],
        out_specs=(pl.BlockSpec((None, S, QK_PAD), lambda h, j: (h, 0, 0)),
                   pl.BlockSpec((None, T, QK_PAD), lambda h, j: (h, j, 0)),
                   pl.BlockSpec((None, T, VDIM), lambda h, j: (h, j, 0))),
        scratch_shapes=[pltpu.VMEM((T, QK_PAD), F32), pltpu.VMEM((T, VDIM), F32),
                        pltpu.VMEM((T, T), F32), pltpu.VMEM((T, T), F32)],
        compiler_params=_params("parallel", "arbitrary"),
    )(Q, K, V, o, do, lse)


def _identity(n):
    return (lax.broadcasted_iota(jnp.int32, (n, n), 0) == lax.broadcasted_iota(jnp.int32, (n, n), 1)).astype(BF16)


def _key_le_query(rows, cols, col0):
    key = lax.broadcasted_iota(jnp.int32, (rows, cols), 0) // CHUNK
    query = (lax.broadcasted_iota(jnp.int32, (rows, cols), 1) + col0) // CHUNK
    return key <= query


def _attn_fwd_t(Q, K, VT, name):
    H, nT, _, T = VT.shape
    S = nT * T
    n_part = 2 if T % 256 == 0 else 1
    Tq = T // n_part

    def body(q_ref, k_ref, vt_ref, o_ref, lse_ref, m_scr, l_scr, acc_scr, s_scr):
        qi = pl.program_id(1)
        m_scr[...] = jnp.full_like(m_scr, -jnp.inf)
        l_scr[...] = jnp.zeros_like(l_scr)
        acc_scr[...] = jnp.zeros_like(acc_scr)

        def scores(j):
            kt = k_ref[pl.ds(pl.multiple_of(j * T, T), T), :]
            return lax.dot_general(kt, q_ref[...], NT, preferred_element_type=F32)

        def softmax_step(j, masked):
            vt = vt_ref[j]
            for part in range(n_part):
                sub = slice(part * Tq, (part + 1) * Tq)
                st = s_scr[:, sub]
                if masked:
                    st = jnp.where(_key_le_query(T, Tq, part * Tq), st, -1e30)
                m_prev = m_scr[:, sub]
                m_new = jnp.maximum(m_prev, jnp.max(st, axis=0, keepdims=True))
                pt = jnp.exp2(st - m_new)
                alpha = jnp.exp2(m_prev - m_new)
                l_scr[:, sub] = alpha * l_scr[:, sub] + jnp.sum(pt, axis=0, keepdims=True)
                acc_scr[:, sub] = alpha * acc_scr[:, sub] + lax.dot_general(
                    vt, pt.astype(BF16), NN, preferred_element_type=F32)
                m_scr[:, sub] = m_new

        s_scr[...] = scores(0)

        def step(j, carry):
            nxt = scores(j + 1)
            softmax_step(j, False)
            s_scr[...] = nxt
            return carry

        lax.fori_loop(0, qi, step, 0)
        softmax_step(qi, True)
        l = l_scr[...]
        o_ref[...] = jnp.transpose(acc_scr[...] / l)
        lse_ref[...] = jnp.transpose(jnp.broadcast_to(m_scr[...] + jnp.log2(l), (LANE, T)))

    return pl.pallas_call(
        body, name=name,
        out_shape=(jax.ShapeDtypeStruct((S, H * VDIM), F32), jax.ShapeDtypeStruct((H, S, LANE), F32)),
        grid=(H, nT),
        in_specs=[pl.BlockSpec((None, T, QK_PAD), lambda h, i: (h, i, 0)),
                  pl.BlockSpec((None, S, QK_PAD), lambda h, i: (h, 0, 0)),
                  pl.BlockSpec((None, nT, VDIM, T), lambda h, i: (h, 0, 0, 0))],
        out_specs=(pl.BlockSpec((T, VDIM), lambda h, i: (i, h)),
                   pl.BlockSpec((None, T, LANE), lambda h, i: (h, i, 0))),
        scratch_shapes=[pltpu.VMEM((1, T), F32), pltpu.VMEM((1, T), F32), pltpu.VMEM((VDIM, T), F32),
                        pltpu.VMEM((T, T), F32)],
        compiler_params=_params("parallel", "arbitrary"),
    )(Q, K, VT)


def _adamw(w, g, m, v, name):
    shape = w.shape
    C = shape[-1]
    R = math.prod(shape[:-1])
    flat = [t.reshape(R, C) for t in (w, g, m, v)]
    tr = _row_tile(R, C * 4)

    def body(w_ref, g_ref, m_ref, v_ref, d_ref, nm_ref, nv_ref):
        gv = g_ref[...]
        m_new = ADAM_B1 * m_ref[...] + (1.0 - ADAM_B1) * gv
        v_new = ADAM_B2 * v_ref[...] + (1.0 - ADAM_B2) * jnp.square(gv)
        m_hat = m_new / (1.0 - ADAM_B1 ** ADAM_STEP)
        v_hat = v_new / (1.0 - ADAM_B2 ** ADAM_STEP)
        d_ref[...] = -ADAM_LR * (m_hat / (jnp.sqrt(v_hat) + ADAM_EPS) + ADAM_WD * w_ref[...])
        nm_ref[...] = m_new
        nv_ref[...] = v_new

    spec = pl.BlockSpec((tr, C), lambda i: (i, 0))
    out = jax.ShapeDtypeStruct((R, C), F32)
    d, nm, nv = pl.pallas_call(
        body, name=name, out_shape=(out, out, out), grid=(R // tr,),
        in_specs=[spec] * 4, out_specs=(spec, spec, spec), compiler_params=_params("parallel"),
    )(*flat)
    return d.reshape(shape), nm.reshape(shape), nv.reshape(shape)


def _sum_into_half(r, buf, layer, ci, n_layers, name):
    n, M, N = r.shape
    tr = _row_tile(M, N * 4 * n, 4 << 20)

    def body(c_ref, r_ref, *rest):
        o_ref = rest[-1]
        acc = r_ref[0].astype(F32)
        for s in range(1, n):
            acc = acc + r_ref[s].astype(F32)
        o_ref[...] = acc

    in_specs = [pl.BlockSpec((n, tr, N), lambda i, c: (0, i, 0))]
    operands = [ci.reshape(1), r]
    aliases = {}
    if buf is not None:
        in_specs.append(ANY)
        operands.append(buf)
        aliases = {2: 0}
    return pl.pallas_call(
        body, name=name, out_shape=jax.ShapeDtypeStruct((n_layers, 2, M, N), F32),
        grid_spec=pltpu.PrefetchScalarGridSpec(
            num_scalar_prefetch=1, grid=(M // tr,), in_specs=in_specs,
            out_specs=pl.BlockSpec((None, None, tr, N), lambda i, c: (layer, c[0], i, 0))),
        input_output_aliases=aliases, compiler_params=_params("parallel"),
    )(*operands)


def _sum_slots(r, name):
    n, M, N = r.shape
    tr = _row_tile(M, N * 4 * n, 4 << 20)

    def body(r_ref, o_ref):
        acc = r_ref[0].astype(F32)
        for s in range(1, n):
            acc = acc + r_ref[s].astype(F32)
        o_ref[...] = acc

    return pl.pallas_call(
        body, name=name, out_shape=jax.ShapeDtypeStruct((M, N), F32), grid=(M // tr,),
        in_specs=[pl.BlockSpec((n, tr, N), lambda i: (0, i, 0))],
        out_specs=pl.BlockSpec((tr, N), lambda i: (i, 0)), compiler_params=_params("parallel"),
    )(r)


ANY = pl.BlockSpec(memory_space=pl.ANY)
DMA_CHUNK_BYTES = 1 << 20
DMA_MAX_CHUNKS = 16
PEER_ORDER = (1, 4, 5, 2, 3, 6, 7)


def _position():
    return lax.axis_index("x"), lax.axis_index("y"), lax.axis_index("c")


def _row_chunks(shape, dtype):
    rows, cols = shape
    n = max(1, min(DMA_MAX_CHUNKS, rows * cols * jnp.dtype(dtype).itemsize // DMA_CHUNK_BYTES))
    while n > 1 and (rows % n or (rows // n) % 16):
        n -= 1
    step = rows // n
    return [pl.ds(q * step, step) for q in range(n)]


def _all_gather8(xs, name):
    n = len(xs)

    def body(*refs):
        x_refs, o_refs = refs[:n], refs[n:2 * n]
        send_sems, recv_sems, local_sems = refs[2 * n:]
        x, y, c = _position()
        me, sibling = (x, y, c), (x, y, 1 - c)
        chips = [(1 - x, y), (x, 1 - y), (1 - x, 1 - y)]

        def slot(a, dev, rows):
            return o_refs[a].at[4 * dev[0] + 2 * dev[1] + dev[2], rows]

        def copy(a, k, block, to, rows, from_input=False):
            return pltpu.make_async_remote_copy(
                src_ref=x_refs[a].at[rows] if from_input else slot(a, block, rows), dst_ref=slot(a, block, rows),
                send_sem=send_sems.at[a, k], recv_sem=recv_sems.at[a, k],
                device_id=to, device_id_type=MESH)

        def mine(a, rows):
            return pltpu.make_async_copy(x_refs[a].at[rows], slot(a, me, rows), local_sems.at[a])

        chunks = [_row_chunks(t.shape, t.dtype) for t in xs]
        whole = [pl.ds(0, t.shape[0]) for t in xs]
        for a in range(n):
            for rows in chunks[a]:
                mine(a, rows).start()
        sent = []
        for a in range(n):
            for k, to in enumerate([sibling] + [(*chip, c) for chip in chips]):
                for rows in chunks[a]:
                    copy(a, k, me, to, rows, from_input=True).start()
                sent.append(copy(a, k, me, to, whole[a], from_input=True))
        for a in range(n):
            for j, chip in enumerate(chips):
                copy(a, 1 + j, (*chip, c), me, whole[a]).wait_recv()
                for rows in chunks[a]:
                    copy(a, 4 + j, (*chip, c), sibling, rows).start()
                sent.append(copy(a, 4 + j, (*chip, c), sibling, whole[a]))
        for a in range(n):
            copy(a, 0, sibling, me, whole[a]).wait_recv()
            for j, chip in enumerate(chips):
                copy(a, 4 + j, (*chip, 1 - c), me, whole[a]).wait_recv()
        for cp in sent:
            cp.wait_send()
        for a in range(n):
            mine(a, whole[a]).wait()

    return pl.pallas_call(
        body, name=name,
        out_shape=[jax.ShapeDtypeStruct((8,) + t.shape, t.dtype) for t in xs],
        in_specs=[ANY] * n, out_specs=[ANY] * n,
        scratch_shapes=[pltpu.SemaphoreType.DMA((n, 7)), pltpu.SemaphoreType.DMA((n, 7)),
                        pltpu.SemaphoreType.DMA((n,))],
    )(*xs)


HBM = pl.BlockSpec(memory_space=pltpu.HBM)
SEM = pl.BlockSpec(memory_space=pltpu.SEMAPHORE)
EFFECT = pltpu.SideEffectType.DATAFLOW_SIDE_EFFECTING


def _peer(m, x, y, c):
    return ((1 - x) if m & 4 else x, (1 - y) if m & 2 else y, (1 - c) if m & 1 else c)


def _send_copies(src_refs, land_refs, send_sems, recv_sems, broadcast):
    x, y, c = _position()
    my = 4 * x + 2 * y + c
    out = []
    for a in range(len(src_refs)):
        for m in PEER_ORDER:
            px, py, pc = _peer(m, x, y, c)
            src = src_refs[a] if broadcast else src_refs[a].at[4 * px + 2 * py + pc]
            out.append(pltpu.make_async_remote_copy(
                src_ref=src, dst_ref=land_refs[a].at[my], send_sem=send_sems[a], recv_sem=recv_sems[a],
                device_id=(px, py, pc), device_id_type=MESH))
    return out


def _send_drain(land_refs, send_sems, recv_sems):
    x, y, c = _position()
    for a in range(len(land_refs)):
        seven = land_refs[a].at[pl.ds(0, 7)]
        both = pltpu.make_async_remote_copy(
            src_ref=seven, dst_ref=seven, send_sem=send_sems[a], recv_sem=recv_sems[a],
            device_id=(x, y, c), device_id_type=MESH)
        both.wait_send()
        both.wait_recv()


def _send_start(srcs, lands, after, broadcast, name):
    n = len(srcs)
    extra = [] if after is None else [after]

    def body(*refs):
        src_refs, land_refs = refs[:n], refs[n:2 * n]
        outs = refs[2 * n + len(extra):]
        send_sems, recv_sems = outs[:n], outs[n:2 * n]
        token = refs[-1]
        for cp in _send_copies(src_refs, land_refs, send_sems, recv_sems, broadcast):
            cp.start()
        token[...] = jnp.zeros_like(token)

    hbm = [pltpu.with_memory_space_constraint(t, pltpu.HBM) for t in list(srcs) + list(lands)]
    res = pl.pallas_call(
        body, name=name,
        out_shape=(*[pltpu.SemaphoreType.DMA(())] * (2 * n),
                   *[pltpu.HBM(t.shape, t.dtype) for t in hbm], jax.ShapeDtypeStruct((8, LANE), F32)),
        in_specs=[HBM] * (2 * n) + [ANY] * len(extra),
        out_specs=(*[SEM] * (2 * n), *[HBM] * (2 * n), pl.BlockSpec(memory_space=pltpu.VMEM)),
        input_output_aliases={i: 2 * n + i for i in range(2 * n)},
        compiler_params=pltpu.CompilerParams(has_side_effects=EFFECT),
    )(*hbm, *extra)
    return dict(sems=res[:2 * n], srcs=res[2 * n:3 * n], lands=res[3 * n:4 * n], token=res[-1], broadcast=broadcast)


def _send_wait(started, after, name):
    n = len(started["srcs"])

    def body(*refs):
        land_refs = refs[n:2 * n]
        send_sems, recv_sems = refs[2 * n:3 * n], refs[3 * n:4 * n]
        _send_drain(land_refs, send_sems, recv_sems)

    operands = list(started["srcs"]) + list(started["lands"])
    res = pl.pallas_call(
        body, name=name,
        out_shape=[pltpu.HBM(t.shape, t.dtype) for t in operands],
        in_specs=[HBM] * (2 * n) + [SEM] * (2 * n) + [ANY],
        out_specs=[HBM] * (2 * n),
        input_output_aliases={i: i for i in range(2 * n)},
        compiler_params=pltpu.CompilerParams(has_side_effects=EFFECT),
    )(*operands, *started["sems"], after)
    return res[n:]


def _own_slot(block, dev):
    zone = lax.empty((8,) + block.shape, block.dtype)
    return lax.dynamic_update_slice(zone, block[None], (dev, 0, 0))


def _pair_swap(bufs, name):
    n = len(bufs)
    pieces = [(a, l) for a, t in enumerate(bufs) for l in range(t.shape[0])]

    def body(*refs):
        b_refs = refs[n:2 * n]
        send_sems, recv_sems = refs[2 * n:]
        x, y, c = _position()

        def copy(k, rows):
            a, l = pieces[k]
            half = b_refs[a].at[l, c, rows]
            return pltpu.make_async_remote_copy(
                src_ref=half, dst_ref=half, send_sem=send_sems.at[k], recv_sem=recv_sems.at[k],
                device_id=(x, y, 1 - c), device_id_type=MESH)

        chunks = [_row_chunks(bufs[a].shape[2:], bufs[a].dtype) for a, _ in pieces]
        whole = [pl.ds(0, bufs[a].shape[2]) for a, _ in pieces]
        for k in range(len(pieces)):
            for rows in chunks[k]:
                copy(k, rows).start()
        for k in range(len(pieces)):
            copy(k, whole[k]).wait_recv()
        for k in range(len(pieces)):
            copy(k, whole[k]).wait_send()

    return pl.pallas_call(
        body, name=name,
        out_shape=[jax.ShapeDtypeStruct(t.shape, t.dtype) for t in bufs],
        in_specs=[ANY] * n, out_specs=[ANY] * n,
        input_output_aliases={a: a for a in range(n)},
        scratch_shapes=[pltpu.SemaphoreType.DMA((len(pieces),)), pltpu.SemaphoreType.DMA((len(pieces),))],
    )(*bufs)


def _pack_rows(parts):
    flat = jnp.concatenate([t.reshape(-1).astype(F32) for t in parts])
    pad = (-flat.shape[0]) % (256 * LANE)
    return jnp.pad(flat, (0, pad)).reshape(-1, LANE)


def _my_half(w2d, ci):
    half = w2d.shape[0] // 2
    return lax.dynamic_slice_in_dim(w2d, ci * half, half, axis=0).astype(BF16)


def _col_view(g):
    _, half, Cs = g.shape
    return g.reshape(4, 2 * half, Cs)


def _row_view(g):
    _, half, C = g.shape
    return g.reshape(8 * half, C)


def _rope_tables(S):
    pos = jnp.arange(S, dtype=F32)
    inv_freq = ROPE_THETA ** (-jnp.arange(0, ROPE, 2, dtype=F32) / ROPE)
    ang = pos[:, None] * inv_freq[None, :]
    cos, sin = jnp.cos(ang), jnp.sin(ang)
    cos_t = jnp.concatenate([cos, cos, cos, cos], axis=1)
    sin_t = jnp.concatenate([-sin, sin, -sin, sin], axis=1)
    return cos_t, sin_t


def kernel(x, c, ada_w, ada_b, pre_g, post_g, sgu_w_in, sgu_norm_g, sgu_w_s, sgu_b_s, sgu_w_out, mla_w_in, mla_q_norm_g, mla_kv_norm_g, mla_w_uq, mla_w_ukv, mla_w_out, loss_target, m_ada_w, m_ada_b, m_pre_g, m_post_g, m_sgu_w_in, m_sgu_norm_g, m_sgu_w_s, m_sgu_b_s, m_sgu_w_out, m_mla_w_in, m_mla_q_norm_g, m_mla_kv_norm_g, m_mla_w_uq, m_mla_w_ukv, m_mla_w_out, v_ada_w, v_ada_b, v_pre_g, v_post_g, v_sgu_w_in, v_sgu_norm_g, v_sgu_w_s, v_sgu_b_s, v_sgu_w_out, v_mla_w_in, v_mla_q_norm_g, v_mla_kv_norm_g, v_mla_w_uq, v_mla_w_ukv, v_mla_w_out):
    S, D = x.shape[1], x.shape[2]
    depth = ada_w.shape[0]
    E = sgu_w_out.shape[1] * 4
    xi, yi, ci = _position()
    chip = 2 * xi + yi
    dev = 4 * xi + 2 * yi + ci
    x0 = x.reshape(S, D)
    target = loss_target.reshape(S, D)

    small = _pack_rows([c, mla_q_norm_g, mla_kv_norm_g])
    mixer_w = dict(sin=sgu_w_in, sout=sgu_w_out, min=mla_w_in, uq=mla_w_uq, ukv=mla_w_ukv, mout=mla_w_out)
    small_g, first_g = _all_gather8([small, _my_half(sgu_w_in[0], ci)], "gather_first")
    small_all = small_g.reshape(8, -1)
    gathered_w = {("sin", 0): first_g}
    qn_w, kvn_w = mla_q_norm_g.shape[1], mla_kv_norm_g.shape[1]
    c_all = small_all[:, :D]
    qn_all = small_all[0::2, D:D + 2 * qn_w].reshape(4, 2, qn_w)
    kvn_all = small_all[0::2, D + 2 * qn_w:D + 2 * qn_w + 2 * kvn_w].reshape(4, 2, kvn_w)
    q_gain = jnp.pad(jnp.transpose(qn_all, (1, 0, 2)).reshape(2, 1, Q_RANK), ((0, 0), (0, 0), (0, Q_RANK_PAD - Q_RANK)))
    kv_gain = jnp.transpose(kvn_all, (1, 0, 2)).reshape(2, 1, KV_RANK)

    views = {}

    def weight(t, j):
        if (t, j) not in views:
            g = gathered_w[(t, j)]
            v = _row_view(g) if t in ("sout", "mout") else _col_view(g)
            if t == "uq":
                v = jnp.pad(v, ((0, 0), (0, Q_RANK_PAD - Q_RANK), (0, 0)))
            views[(t, j)] = v
        return views[(t, j)]

    cols = ada_w.shape[2]
    ada_b_cols = lax.dynamic_slice_in_dim(ada_b, chip * cols, cols, axis=1)
    c_pad = jnp.pad(c_all, ((0, 8), (0, 0)))
    mod_cols = _ada_mod(c_pad, ada_w, ada_b_cols, "ada_mod")[:, :8]
    mod_g, = _all_gather8([mod_cols.reshape(depth * 8, cols)], "gather_mod")
    mod_all = jnp.transpose(mod_g[0::2].reshape(4, depth, 8, cols), (1, 2, 0, 3)).reshape(depth, 8, 4 * cols)

    groups = [("sout0", [("sout", 0)]), ("mla0", [(t, 0) for t in ("min", "uq", "ukv", "mout")]),
              ("sgu1", [("sin", 1), ("sout", 1)]), ("mla1", [(t, 1) for t in ("min", "uq", "ukv", "mout")])]
    sends = {}
    behind = mod_g
    for gname, items in groups:
        blocks = [_my_half(mixer_w[t][j], ci) for t, j in items]
        sends[gname] = _send_start(blocks, [_own_slot(b, dev) for b in blocks], behind, True, f"send_{gname}")
        behind = sends[gname]["token"]

    def arrive(gname, after):
        lands = _send_wait(sends[gname], after, f"arrive_{gname}")
        gathered_w.update(zip(dict(groups)[gname], lands))
    mod = lax.dynamic_index_in_dim(mod_all, dev, 1, keepdims=False)
    shift = [mod[i:i + 1, :D] for i in range(depth)]
    scale = [mod[i:i + 1, D:2 * D] for i in range(depth)]
    gate = [mod[i:i + 1, 2 * D:] for i in range(depth)]

    cos_t, sin_t = _rope_tables(S)
    b_bc = jnp.broadcast_to(sgu_b_s[:, :, :, None], sgu_b_s.shape + (LANE,))
    w_sT = jnp.swapaxes(sgu_w_s, 2, 3)

    saved = []
    xs = x0
    for i in range(depth):
        j = i // 2
        tag = f"l{i}"
        h = _pre_fwd(xs, pre_g[i:i + 1], scale[i], shift[i], f"pre_fwd_{tag}", after=(behind,) if i == 0 else ())
        if i % 2 == 0:
            if j > 0:
                arrive(f"sgu{j}", h)
            uvz = _mm(h, weight("sin", j), b_sharded=True, name=f"sgu_in_{tag}")
            y = _sgu_gate_fwd(uvz, sgu_norm_g[j:j + 1], sgu_w_s[j], b_bc[j], f"sgu_gate_fwd_{tag}")
            if j == 0:
                arrive("sout0", y)
            out = _mm(y, weight("sout", j), name=f"sgu_out_{tag}")
            saved.append(dict(x=xs, h=h, uvz=uvz, y=y, out=out))
        else:
            arrive(f"mla{j}", h)
            p = _mm(h, weight("min", j), b_sharded=True, name=f"mla_in_{tag}")
            cqn, ckvn = _mla_mid_fwd(p, q_gain[j], kv_gain[j], f"mla_mid_fwd_{tag}")
            q = _mm(cqn, weight("uq", j), b_sharded=True, name=f"mla_uq_{tag}")
            kv = _mm(ckvn, weight("ukv", j), b_sharded=True, name=f"mla_ukv_{tag}")
            Q, K, V, VT = _mla_pack(q, kv, p, cos_t, sin_t, f"mla_pack_{tag}")
            o, lse = _attn_fwd_t(Q, K, VT, f"attn_fwd_{tag}")
            y = _mla_gate_fwd(o, p, f"mla_gate_fwd_{tag}")
            out = _mm(y, weight("mout", j), name=f"mla_out_{tag}")
            saved.append(dict(x=xs, h=h, p=p, cqn=cqn, ckvn=ckvn, Q=Q, K=K, V=V, o=o, lse=lse, y=y, out=out))
        xs = _post_fwd(xs, out, gate[i], post_g[i:i + 1], f"post_fwd_{tag}")

    dx, loss_part = _loss_grad(xs, target, "loss")

    dmod = [None] * depth
    d_pre_g = [None] * depth
    d_post_g = [None] * depth
    d_sgu = [None] * 2
    d_mla = [None] * 2
    kinds = ("sgu_w_in", "sgu_w_out", "mla_w_in", "mla_w_uq", "mla_w_ukv", "mla_w_out")
    halves = dict.fromkeys(kinds)
    in_flight = []

    def send_grads(items, label):
        slices = [dw.reshape(8, -1, dw.shape[-1]) for _, _, dw in items]
        lands = [_own_slot(lax.dynamic_index_in_dim(s, dev, 0, keepdims=False), dev) for s in slices]
        started = _send_start(slices, lands, None, False, f"send_{label}")
        in_flight.append((started, [(kind, layer) for kind, layer, _ in items], label))
        return (started["token"],)

    def collect(count, after):
        for _ in range(count):
            started, keys, label = in_flight.pop(0)
            lands = _send_wait(started, after, f"arrive_{label}")
            for (kind, layer), r in zip(keys, lands):
                halves[kind] = _sum_into_half(r, halves[kind], layer, ci, 2, f"sum_{kind}_{layer}")

    for i in reversed(range(depth)):
        j = i // 2
        tag = f"l{i}"
        sv = saved[i]
        older = len(in_flight)
        dy, dgate, d_post_g[i] = _post_bwd(dx, sv["out"], gate[i], post_g[i:i + 1], f"post_bwd_{tag}")
        if i % 2 == 0:
            dw_out = _mm(sv["y"], dy, ta=True, out_dtype=BF16, name=f"sgu_out_dw_{tag}")
            sent = send_grads([("sgu_w_out", j, dw_out)], f"{tag}_out")
            dyv = _mm(dy, weight("sout", j), tb=True, after=sent, name=f"sgu_out_dx_{tag}")
            duvz, dng, dws, dbs = _sgu_gate_bwd(sv["uvz"], dyv, sgu_norm_g[j:j + 1], sgu_w_s[j], w_sT[j], b_bc[j],
                                                f"sgu_gate_bwd_{tag}")
            sent = ()
            if i > 0:
                dw_in = _mm(sv["h"], duvz, ta=True, out_sharded=True, out_dtype=BF16, tk=4096,
                            name=f"sgu_in_dw_{tag}")
                sent = send_grads([("sgu_w_in", j, dw_in)], f"{tag}_in")
            dh = _mm(duvz, weight("sin", j), tb=True, b_sharded=True, tk=3072, after=sent, name=f"sgu_in_dx_{tag}")
            d_sgu[j] = dict(norm_g=dng, w_s=dws, b_s=dbs[:, :, 0], duvz=duvz)
        else:
            dw_out = _mm(sv["y"], dy, ta=True, out_dtype=BF16, name=f"mla_out_dw_{tag}")
            sent = send_grads([("mla_w_out", j, dw_out)], f"{tag}_out")
            dyv = _mm(dy, weight("mout", j), tb=True, after=sent, name=f"mla_out_dx_{tag}")
            p = sv["p"]
            do = _mla_gate_bwd(dyv, p, f"mla_gate_bwd_{tag}")
            dQ, dK, dV = _attn_bwd(sv["Q"], sv["K"], sv["V"], sv["o"], do, sv["lse"], f"attn_bwd_{tag}")
            dq, dkv, dkr = _mla_unpack(dQ, dK, dV, cos_t, sin_t, f"mla_unpack_{tag}")
            dw_uq = _mm(sv["cqn"], dq, ta=True, out_sharded=True, out_dtype=BF16, name=f"mla_uq_dw_{tag}")
            dcqn = _mm(dq, weight("uq", j), tb=True, b_sharded=True, name=f"mla_uq_dx_{tag}")
            dw_ukv = _mm(sv["ckvn"], dkv, ta=True, out_sharded=True, out_dtype=BF16, name=f"mla_ukv_dw_{tag}")
            dckvn = _mm(dkv, weight("ukv", j), tb=True, b_sharded=True, name=f"mla_ukv_dx_{tag}")
            dp, dqg, dkvg = _mla_mid_bwd(p, dcqn, dckvn, dkr, dyv, sv["o"], q_gain[j], kv_gain[j], f"mla_mid_bwd_{tag}")
            dw_in = _mm(sv["h"], dp, ta=True, out_sharded=True, out_dtype=BF16, name=f"mla_in_dw_{tag}")
            sent = send_grads([("mla_w_in", j, dw_in), ("mla_w_uq", j, dw_uq[:, :Q_RANK]), ("mla_w_ukv", j, dw_ukv)],
                              f"{tag}_in")
            dh = _mm(dp, weight("min", j), tb=True, b_sharded=True, after=sent, name=f"mla_in_dx_{tag}")
            d_mla[j] = dict(qg=dqg[0, :Q_RANK], kvg=dkvg[0])
        dx, dshift, dscale, d_pre_g[i] = _pre_bwd(dh, sv["x"], dx, pre_g[i:i + 1], scale[i], f"pre_bwd_{tag}")
        dmod[i] = jnp.concatenate([dshift, dscale, dgate], axis=1)
        collect(older, dx)
    grad_x = dx.reshape(x.shape)

    parts = [jnp.concatenate(dmod, axis=0), jnp.concatenate(d_pre_g, axis=0), jnp.concatenate(d_post_g, axis=0),
             jnp.stack([d["norm_g"][0] for d in d_sgu]), jnp.stack([d["w_s"] for d in d_sgu]),
             jnp.stack([d["b_s"] for d in d_sgu]), jnp.stack([d["qg"] for d in d_mla]),
             jnp.stack([d["kvg"] for d in d_mla]), loss_part]
    sizes = [int(np.prod(t.shape)) for t in parts]
    packed = _pack_rows(parts)
    packed_all, dmod_all = _all_gather8([packed, parts[0]], "gather_small_grads")
    total = _sum_slots(packed_all, "sum_small_grads").reshape(-1)
    offs = np.concatenate([[0], np.cumsum(sizes)])
    pieces = [total[int(offs[t]):int(offs[t + 1])].reshape(parts[t].shape) for t in range(len(parts))]
    g_ada_b, g_pre_g, g_post_g, g_norm_g, g_w_s, g_b_s, g_qg_full, g_kvg_full, loss_sum = pieces
    loss = loss_sum.reshape(())
    g_qg = lax.dynamic_slice_in_dim(g_qg_full, chip * qn_w, qn_w, axis=1)
    g_kvg = lax.dynamic_slice_in_dim(g_kvg_full, chip * kvn_w, kvn_w, axis=1)
    dmod_cols = jnp.stack([lax.dynamic_slice_in_dim(dmod_all[:, i], chip * cols, cols, axis=1) for i in range(depth)])
    dmod_cols = jnp.pad(dmod_cols, ((0, 0), (0, LANE - 8), (0, 0)))

    dw_in0 = _mm(saved[0]["h"], d_sgu[0]["duvz"], ta=True, out_sharded=True, out_dtype=BF16, tk=4096,
                 after=(packed_all,), name="sgu_in_dw_l0")
    sent = send_grads([("sgu_w_in", 0, dw_in0)], "l0_in")
    g_ada_w = _ada_grad(jnp.pad(c_all.T, ((0, 0), (0, LANE - 8))), dmod_cols, "ada_grad", after=sent)

    wnames = ["ada_w", "ada_b", "pre_g", "post_g", "sgu_w_in", "sgu_norm_g", "sgu_w_s", "sgu_b_s", "sgu_w_out",
              "mla_w_in", "mla_q_norm_g", "mla_kv_norm_g", "mla_w_uq", "mla_w_ukv", "mla_w_out"]
    weights = dict(zip(wnames, [ada_w, ada_b, pre_g, post_g, sgu_w_in, sgu_norm_g, sgu_w_s, sgu_b_s, sgu_w_out,
                                mla_w_in, mla_q_norm_g, mla_kv_norm_g, mla_w_uq, mla_w_ukv, mla_w_out]))
    ms = dict(zip(wnames, [m_ada_w, m_ada_b, m_pre_g, m_post_g, m_sgu_w_in, m_sgu_norm_g, m_sgu_w_s, m_sgu_b_s,
                           m_sgu_w_out, m_mla_w_in, m_mla_q_norm_g, m_mla_kv_norm_g, m_mla_w_uq, m_mla_w_ukv,
                           m_mla_w_out]))
    vs = dict(zip(wnames, [v_ada_w, v_ada_b, v_pre_g, v_post_g, v_sgu_w_in, v_sgu_norm_g, v_sgu_w_s, v_sgu_b_s,
                           v_sgu_w_out, v_mla_w_in, v_mla_q_norm_g, v_mla_kv_norm_g, v_mla_w_uq, v_mla_w_ukv,
                           v_mla_w_out]))
    grads = dict(ada_w=g_ada_w, ada_b=g_ada_b, pre_g=g_pre_g, post_g=g_post_g, sgu_norm_g=g_norm_g, sgu_w_s=g_w_s,
                 sgu_b_s=g_b_s, mla_q_norm_g=g_qg, mla_kv_norm_g=g_kvg)
    stepped = {}

    def step(nm):
        grads[nm] = grads[nm].reshape(weights[nm].shape)
        stepped[nm] = _adamw(weights[nm], grads[nm], ms[nm], vs[nm], f"adamw_{nm}")

    for nm in wnames:
        if nm in grads:
            step(nm)
    early = [kind for kind in kinds if kind != "sgu_w_in"]
    collect(len(in_flight) - 1, stepped["ada_w"][0])
    for kind, g in zip(early, _pair_swap([halves[kind] for kind in early], "swap_grads")):
        grads[kind] = g
        step(kind)
    collect(len(in_flight), stepped[early[-1]][0])
    grads["sgu_w_in"], = _pair_swap([halves["sgu_w_in"]], "swap_grads_last")
    step("sgu_w_in")
    return (loss, grad_x, *[grads[nm] for nm in wnames], *[stepped[nm][0] for nm in wnames],
            *[stepped[nm][1] for nm in wnames], *[stepped[nm][2] for nm in wnames])
```

```python
import math

import jax
import jax.numpy as jnp
import numpy as np
from jax import lax
from jax.experimental import pallas as pl
from jax.experimental.pallas import tpu as pltpu

F32 = jnp.float32
BF16 = jnp.bfloat16
MESH = pl.DeviceIdType.MESH

NORM_EPS = 1e-6
CHUNK = 64
SGU_BLOCK = 128
SGU_GROUPS = 16
HEADS = 16
NOPE = 128
ROPE = 64
VDIM = 128
QK_PAD = 256
Q_RANK = 448
Q_RANK_PAD = 512
KV_RANK = 512
ROPE_THETA = 10000.0
ATTN_SCALE = (NOPE + ROPE) ** -0.5
LOG2E = 1.4426950408889634
Q_FOLD = ATTN_SCALE * LOG2E
ATTN_TILE = 512

ADAM_LR = 0.001
ADAM_B1 = 0.9
ADAM_B2 = 0.999
ADAM_EPS = 1e-08
ADAM_WD = 0.01
ADAM_STEP = 10

LANE = 128
VMEM_LIMIT = 48 * 1024 * 1024

NN = (((1,), (0,)), ((), ()))
NT = (((1,), (1,)), ((), ()))
TN = (((0,), (0,)), ((), ()))


def _params(*sem):
    return pltpu.CompilerParams(dimension_semantics=sem, vmem_limit_bytes=VMEM_LIMIT)


def _row_tile(rows, row_bytes, target_bytes=1 << 20):
    if rows * row_bytes <= target_bytes or rows % 16:
        return rows
    best = 16
    t = 16
    while t <= rows:
        if rows % t == 0 and t * row_bytes <= target_bytes:
            best = t
        t += 16
    return best


def _fit(dim, target):
    if dim <= target:
        return dim
    t = (target // LANE) * LANE
    while t > LANE and dim % t:
        t -= LANE
    return t


def _gelu(x):
    return 0.5 * x * (1.0 + lax.erf(x * 0.7071067811865476))


def _gelu_and_grad(x):
    cdf = 0.5 * (1.0 + lax.erf(x * 0.7071067811865476))
    return x * cdf, cdf + x * jnp.exp(-0.5 * x * x) * 0.3989422804014327


def _mm(a, b, *, ta=False, tb=False, b_sharded=False, out_sharded=False, out_dtype=F32,
        tm=1024, tn=1024, tk=2048, after=(), name):
    if ta:
        K, M = a.shape
    else:
        M, K = a.shape
    if b_sharded:
        shards, rows, Cs = b.shape
        b_shape = (rows, shards * Cs)
    else:
        b_shape = b.shape
    if tb:
        N, K2 = b_shape
    else:
        K2, N = b_shape
    assert K == K2, (a.shape, b.shape, ta, tb)
    n_lim = Cs if (b_sharded and not tb) else (N // 4 if out_sharded else N)
    k_lim = Cs if (b_sharded and tb) else K
    tm, tn, tk = _fit(M, tm), _fit(n_lim, tn), _fit(k_lim, tk)
    assert M % tm == 0 and n_lim % tn == 0 and k_lim % tk == 0, (M, N, K, tm, tn, tk)
    nk = K // tk
    nb_n = n_lim // tn
    nb_k = k_lim // tk
    dims = (((0 if ta else 1,), (1 if tb else 0,)), ((), ()))

    def body(a_ref, b_ref, *rest):
        o_ref, *scratch = rest[len(after):]
        prod = lax.dot_general(a_ref[...].astype(BF16), b_ref[...].astype(BF16), dims,
                               preferred_element_type=F32)
        if nk == 1:
            o_ref[...] = prod.astype(out_dtype)
        else:
            acc_ref, = scratch
            k = pl.program_id(2)

            @pl.when(k == 0)
            def _():
                acc_ref[...] = prod

            @pl.when(k > 0)
            def _():
                acc_ref[...] += prod

            @pl.when(k == nk - 1)
            def _():
                o_ref[...] = acc_ref[...].astype(out_dtype)

    a_spec = (pl.BlockSpec((tk, tm), lambda i, j, k: (k, i)) if ta
              else pl.BlockSpec((tm, tk), lambda i, j, k: (i, k)))
    if b_sharded and tb:
        b_spec = pl.BlockSpec((None, tn, tk), lambda i, j, k: (k // nb_k, j, k % nb_k))
    elif b_sharded:
        b_spec = pl.BlockSpec((None, tk, tn), lambda i, j, k: (j // nb_n, k, j % nb_n))
    elif tb:
        b_spec = pl.BlockSpec((tn, tk), lambda i, j, k: (j, k))
    else:
        b_spec = pl.BlockSpec((tk, tn), lambda i, j, k: (k, j))
    if out_sharded:
        out_shape = jax.ShapeDtypeStruct((4, M, N // 4), out_dtype)
        out_spec = pl.BlockSpec((None, tm, tn), lambda i, j, k: (j // nb_n, i, j % nb_n))
    else:
        out_shape = jax.ShapeDtypeStruct((M, N), out_dtype)
        out_spec = pl.BlockSpec((tm, tn), lambda i, j, k: (i, j))
    return pl.pallas_call(
        body, name=name,
        out_shape=out_shape,
        grid=(M // tm, N // tn, nk),
        in_specs=[a_spec, b_spec] + [pl.BlockSpec(memory_space=pl.ANY)] * len(after),
        out_specs=out_spec,
        scratch_shapes=[] if nk == 1 else [pltpu.VMEM((tm, tn), F32)],
        compiler_params=_params("parallel", "parallel", "arbitrary"),
    )(a, b, *after)


def _split_bf16(v):
    hi = v.astype(BF16)
    lo = (v - hi.astype(F32)).astype(BF16)
    return hi, lo


def _dot3(a, b, dims):
    a_hi, a_lo = _split_bf16(a)
    b_hi, b_lo = _split_bf16(b)
    out = lax.dot_general(a_hi, b_hi, dims, preferred_element_type=F32)
    out += lax.dot_general(a_lo, b_hi, dims, preferred_element_type=F32)
    out += lax.dot_general(a_hi, b_lo, dims, preferred_element_type=F32)
    return out


def _ada_mod(c_all, ada_w, ada_b_cols, name):
    L, D, cols = ada_w.shape
    B = c_all.shape[0]
    tn = 512 if cols % 512 == 0 else cols

    def body(c_ref, w_ref, b_ref, o_ref):
        cv = c_ref[...]
        cond = cv * jax.nn.sigmoid(cv)
        o_ref[...] = _dot3(cond, w_ref[...], NN) + b_ref[...]

    return pl.pallas_call(
        body, name=name,
        out_shape=jax.ShapeDtypeStruct((L, B, cols), F32),
        grid=(L, cols // tn),
        in_specs=[pl.BlockSpec((B, D), lambda l, j: (0, 0)),
                  pl.BlockSpec((None, D, tn), lambda l, j: (l, 0, j)),
                  pl.BlockSpec((None, 1, tn), lambda l, j: (l, 0, j))],
        out_specs=pl.BlockSpec((None, B, tn), lambda l, j: (l, 0, j)),
        compiler_params=_params("parallel", "parallel"),
    )(c_all, ada_w, ada_b_cols.reshape(L, 1, cols))


def _ada_grad(c_t, dmod_cols, name, after=()):
    L, B, cols = dmod_cols.shape
    D = c_t.shape[0]
    tn = 512 if cols % 512 == 0 else cols

    def body(c_ref, d_ref, *rest):
        o_ref = rest[-1]
        cv = c_ref[...]
        cond = cv * jax.nn.sigmoid(cv)
        o_ref[...] = _dot3(cond, d_ref[...], NN)

    return pl.pallas_call(
        body, name=name,
        out_shape=jax.ShapeDtypeStruct((L, D, cols), F32),
        grid=(L, cols // tn),
        in_specs=[pl.BlockSpec((D, B), lambda l, j: (0, 0)),
                  pl.BlockSpec((None, B, tn), lambda l, j: (l, 0, j))] + [pl.BlockSpec(memory_space=pl.ANY)] * len(after),
        out_specs=pl.BlockSpec((None, D, tn), lambda l, j: (l, 0, j)),
        compiler_params=_params("parallel", "parallel"),
    )(c_t, dmod_cols, *after)


def _row_spec(ts, width):
    return pl.BlockSpec((ts, width), lambda i: (i, 0))


def _vec_spec(width):
    return pl.BlockSpec((1, width), lambda i: (0, 0))


def _pre_fwd(x, pre_g, scale, shift, name, after=()):
    S, D = x.shape
    ts = min(256, S)

    def body(x_ref, g_ref, sc_ref, sh_ref, *rest):
        h_ref = rest[-1]
        xv = x_ref[...]
        r = lax.rsqrt(jnp.mean(xv * xv, axis=-1, keepdims=True) + NORM_EPS)
        h_ref[...] = ((xv * r * g_ref[...]) * (1.0 + sc_ref[...]) + sh_ref[...]).astype(BF16)

    return pl.pallas_call(
        body, name=name, out_shape=jax.ShapeDtypeStruct((S, D), BF16), grid=(S // ts,),
        in_specs=[_row_spec(ts, D), _vec_spec(D), _vec_spec(D), _vec_spec(D)]
        + [pl.BlockSpec(memory_space=pl.ANY)] * len(after),
        out_specs=_row_spec(ts, D), compiler_params=_params("parallel"),
    )(x, pre_g, scale, shift, *after)


def _pre_bwd(dh, x, dx_res, pre_g, scale, name):
    S, D = x.shape
    ts = min(256, S)

    def body(dh_ref, x_ref, dr_ref, g_ref, sc_ref, dx_ref, dsh_ref, dsc_ref, dg_ref):
        @pl.when(pl.program_id(0) == 0)
        def _():
            dsh_ref[...] = jnp.zeros_like(dsh_ref)
            dsc_ref[...] = jnp.zeros_like(dsc_ref)
            dg_ref[...] = jnp.zeros_like(dg_ref)

        dh = dh_ref[...]
        xv = x_ref[...]
        g = g_ref[...]
        one_sc = 1.0 + sc_ref[...]
        r = lax.rsqrt(jnp.mean(xv * xv, axis=-1, keepdims=True) + NORM_EPS)
        xn = xv * r
        dsh_ref[...] += jnp.sum(dh, axis=0, keepdims=True)
        dsc_ref[...] += jnp.sum(dh * (xn * g), axis=0, keepdims=True)
        dg_ref[...] += jnp.sum(dh * one_sc * xn, axis=0, keepdims=True)
        dxn = dh * one_sc * g
        dx_ref[...] = dr_ref[...] + r * (dxn - xn * jnp.mean(dxn * xn, axis=-1, keepdims=True))

    vec = jax.ShapeDtypeStruct((1, D), F32)
    return pl.pallas_call(
        body, name=name, out_shape=(jax.ShapeDtypeStruct((S, D), F32), vec, vec, vec), grid=(S // ts,),
        in_specs=[_row_spec(ts, D), _row_spec(ts, D), _row_spec(ts, D), _vec_spec(D), _vec_spec(D)],
        out_specs=(_row_spec(ts, D), _vec_spec(D), _vec_spec(D), _vec_spec(D)),
        compiler_params=_params("arbitrary"),
    )(dh, x, dx_res, pre_g, scale)


def _post_fwd(x, y, gate, post_g, name):
    S, D = x.shape
    ts = min(256, S)

    def body(x_ref, y_ref, gt_ref, g_ref, o_ref):
        yv = y_ref[...]
        r = lax.rsqrt(jnp.mean(yv * yv, axis=-1, keepdims=True) + NORM_EPS)
        o_ref[...] = x_ref[...] + gt_ref[...] * (yv * r * g_ref[...])

    return pl.pallas_call(
        body, name=name, out_shape=jax.ShapeDtypeStruct((S, D), F32), grid=(S // ts,),
        in_specs=[_row_spec(ts, D), _row_spec(ts, D), _vec_spec(D), _vec_spec(D)],
        out_specs=_row_spec(ts, D), compiler_params=_params("parallel"),
    )(x, y, gate, post_g)


def _post_bwd(dx, y, gate, post_g, name):
    S, D = y.shape
    ts = min(256, S)

    def body(dx_ref, y_ref, gt_ref, g_ref, dy_ref, dgt_ref, dg_ref):
        @pl.when(pl.program_id(0) == 0)
        def _():
            dgt_ref[...] = jnp.zeros_like(dgt_ref)
            dg_ref[...] = jnp.zeros_like(dg_ref)

        dxv = dx_ref[...]
        yv = y_ref[...]
        g = g_ref[...]
        gt = gt_ref[...]
        r = lax.rsqrt(jnp.mean(yv * yv, axis=-1, keepdims=True) + NORM_EPS)
        yn = yv * r
        dgt_ref[...] += jnp.sum(dxv * (yn * g), axis=0, keepdims=True)
        dg_ref[...] += jnp.sum(dxv * gt * yn, axis=0, keepdims=True)
        dyn = dxv * gt * g
        dy_ref[...] = (r * (dyn - yn * jnp.mean(dyn * yn, axis=-1, keepdims=True))).astype(BF16)

    vec = jax.ShapeDtypeStruct((1, D), F32)
    return pl.pallas_call(
        body, name=name, out_shape=(jax.ShapeDtypeStruct((S, D), BF16), vec, vec), grid=(S // ts,),
        in_specs=[_row_spec(ts, D), _row_spec(ts, D), _vec_spec(D), _vec_spec(D)],
        out_specs=(_row_spec(ts, D), _vec_spec(D), _vec_spec(D)),
        compiler_params=_params("arbitrary"),
    )(dx, y, gate, post_g)


def _loss_grad(xf, target, name):
    S, D = xf.shape
    ts = min(256, S)

    def body(x_ref, t_ref, dx_ref, l_ref):
        @pl.when(pl.program_id(0) == 0)
        def _():
            l_ref[...] = jnp.zeros_like(l_ref)

        e = x_ref[...] - t_ref[...]
        dx_ref[...] = e * (1.0 / D)
        row = jnp.sum(e * e, axis=1, keepdims=True) * (1.0 / D)
        l_ref[...] += 0.5 * jnp.sum(row, axis=0, keepdims=True)

    return pl.pallas_call(
        body, name=name,
        out_shape=(jax.ShapeDtypeStruct((S, D), F32), jax.ShapeDtypeStruct((1, 1), F32)), grid=(S // ts,),
        in_specs=[_row_spec(ts, D), _row_spec(ts, D)],
        out_specs=(_row_spec(ts, D), pl.BlockSpec((1, 1), lambda i: (0, 0))),
        compiler_params=_params("arbitrary"),
    )(xf, target)


def _chunk_mask(transposed=False):
    row = lax.broadcasted_iota(jnp.int32, (SGU_BLOCK, SGU_BLOCK), 0) // CHUNK
    col = lax.broadcasted_iota(jnp.int32, (SGU_BLOCK, SGU_BLOCK), 1) // CHUNK
    return (row <= col) if transposed else (col <= row)


def _sgu_gate_fwd(uvz, norm_g, w_s, b_bc, name):
    S, E3 = uvz.shape
    E = E3 // 3
    T = SGU_BLOCK
    gd = E // SGU_GROUPS

    def body(uvz_ref, ng_ref, ws_ref, bb_ref, y_ref, v_scr):
        gv = _gelu(uvz_ref[:, E:2 * E])
        mu = jnp.mean(gv, axis=-1, keepdims=True)
        xc = gv - mu
        rstd = lax.rsqrt(jnp.mean(xc * xc, axis=-1, keepdims=True) + NORM_EPS)
        v_scr[...] = (xc * rstd * ng_ref[...]).astype(BF16)
        mask = _chunk_mask()
        for g in range(SGU_GROUPS):
            sl = slice(g * gd, (g + 1) * gd)
            wg = jnp.where(mask, ws_ref[g], 0.0).astype(BF16)
            vm = lax.dot_general(wg, v_scr[:, sl], NN, preferred_element_type=F32)
            vm = vm + jnp.tile(bb_ref[g], (1, gd // LANE))
            z = uvz_ref[:, 2 * E + g * gd:2 * E + (g + 1) * gd]
            y_ref[:, sl] = (_gelu(uvz_ref[:, sl]) * vm * (z * jax.nn.sigmoid(z))).astype(BF16)

    return pl.pallas_call(
        body, name=name, out_shape=jax.ShapeDtypeStruct((S, E), BF16), grid=(S // T,),
        in_specs=[_row_spec(T, E3), _vec_spec(E),
                  pl.BlockSpec((SGU_GROUPS, T, T), lambda i: (0, 0, 0)),
                  pl.BlockSpec((SGU_GROUPS, T, LANE), lambda i: (0, 0, 0))],
        out_specs=_row_spec(T, E),
        scratch_shapes=[pltpu.VMEM((T, E), BF16)],
        compiler_params=_params("parallel"),
    )(uvz, norm_g, w_s, b_bc)


def _sgu_gate_bwd(uvz, dyv, norm_g, w_s, w_sT, b_bc, name):
    S, E3 = uvz.shape
    E = E3 // 3
    T = SGU_BLOCK
    gd = E // SGU_GROUPS

    def body(uvz_ref, dyv_ref, ng_ref, ws_ref, wst_ref, bb_ref,
             d_ref, dng_ref, dws_ref, dbs_ref, vhat_scr, dv_scr, vgrad_scr):
        @pl.when(pl.program_id(0) == 0)
        def _():
            dng_ref[...] = jnp.zeros_like(dng_ref)
            dws_ref[...] = jnp.zeros_like(dws_ref)
            dbs_ref[...] = jnp.zeros_like(dbs_ref)

        gv, vgrad_scr[...] = _gelu_and_grad(uvz_ref[:, E:2 * E])
        mu = jnp.mean(gv, axis=-1, keepdims=True)
        xc = gv - mu
        rstd = lax.rsqrt(jnp.mean(xc * xc, axis=-1, keepdims=True) + NORM_EPS)
        vhat_scr[...] = xc * rstd
        mask = _chunk_mask()
        mask_t = _chunk_mask(transposed=True)
        for g in range(SGU_GROUPS):
            sl = slice(g * gd, (g + 1) * gd)
            u_pre = uvz_ref[:, sl]
            z = uvz_ref[:, 2 * E + g * gd:2 * E + (g + 1) * gd]
            dy = dyv_ref[:, sl]
            u, u_grad = _gelu_and_grad(u_pre)
            sig = jax.nn.sigmoid(z)
            sz = z * sig
            vg = (vhat_scr[:, sl] * ng_ref[:, sl]).astype(BF16)
            wg = jnp.where(mask, ws_ref[g], 0.0).astype(BF16)
            vm = lax.dot_general(wg, vg, NN, preferred_element_type=F32)
            vm = vm + jnp.tile(bb_ref[g], (1, gd // LANE))
            dy_u = dy * u
            d_ref[:, sl] = (dy * vm * sz * u_grad).astype(BF16)
            d_ref[:, 2 * E + g * gd:2 * E + (g + 1) * gd] = (
                dy_u * vm * (sig * (1.0 + z * (1.0 - sig)))).astype(BF16)
            dvm = dy_u * sz
            dvm_b = dvm.astype(BF16)
            dws_ref[g] += jnp.where(mask, lax.dot_general(dvm_b, vg, NT, preferred_element_type=F32), 0.0)
            dbs_ref[g] += jnp.broadcast_to(jnp.sum(dvm, axis=1, keepdims=True), (T, LANE))
            wgt = jnp.where(mask_t, wst_ref[g], 0.0).astype(BF16)
            dv_scr[:, sl] = lax.dot_general(wgt, dvm_b, NN, preferred_element_type=F32)
        dv = dv_scr[...]
        vhat = vhat_scr[...]
        dng_ref[...] += jnp.sum(dv * vhat, axis=0, keepdims=True)
        dvh = dv * ng_ref[...]
        dgv = rstd * (dvh - jnp.mean(dvh, axis=-1, keepdims=True)
                      - vhat * jnp.mean(dvh * vhat, axis=-1, keepdims=True))
        d_ref[:, E:2 * E] = (dgv * vgrad_scr[...]).astype(BF16)

    wspec = pl.BlockSpec((SGU_GROUPS, T, T), lambda i: (0, 0, 0))
    bspec = pl.BlockSpec((SGU_GROUPS, T, LANE), lambda i: (0, 0, 0))
    return pl.pallas_call(
        body, name=name,
        out_shape=(jax.ShapeDtypeStruct((S, E3), BF16), jax.ShapeDtypeStruct((1, E), F32),
                   jax.ShapeDtypeStruct((SGU_GROUPS, T, T), F32),
                   jax.ShapeDtypeStruct((SGU_GROUPS, T, LANE), F32)),
        grid=(S // T,),
        in_specs=[_row_spec(T, E3), _row_spec(T, E), _vec_spec(E), wspec, wspec, bspec],
        out_specs=(_row_spec(T, E3), _vec_spec(E), wspec, bspec),
        scratch_shapes=[pltpu.VMEM((T, E), F32), pltpu.VMEM((T, E), F32), pltpu.VMEM((T, E), F32)],
        compiler_params=_params("arbitrary"),
    )(uvz, dyv, norm_g, w_s, w_sT, b_bc)


MLA_WIDTH = HEADS * VDIM
P_LATENT = Q_RANK + KV_RANK + ROPE
P_WIDTH = P_LATENT + MLA_WIDTH


def _swap_halves(v):
    lane = lax.broadcasted_iota(jnp.int32, v.shape, 1)
    return jnp.where(lane % ROPE < ROPE // 2, pltpu.roll(v, LANE - ROPE // 2, 1), pltpu.roll(v, ROPE // 2, 1))


def _low_lanes(rows):
    return lax.broadcasted_iota(jnp.int32, (rows, LANE), 1) < ROPE


def _latent_tiles(ref):
    return [ref[:, t * LANE:(t + 1) * LANE] for t in range(P_LATENT // LANE)]


def _split_latents(tiles, low):
    cq = jnp.concatenate(tiles[0:3] + [jnp.where(low, tiles[3], 0.0)], axis=1)
    rolled = [pltpu.roll(t, ROPE, 1) for t in tiles[3:8]]
    ckv = jnp.concatenate([jnp.where(low, rolled[t], rolled[t + 1]) for t in range(4)], axis=1)
    kr = jnp.where(low, rolled[4], 0.0)
    return cq, ckv, kr


def _mla_mid_fwd(p, qg, kvg, name):
    S, PW = p.shape
    ts = min(256, S)

    def body(p_ref, qg_ref, kvg_ref, cqn_ref, ckvn_ref):
        cq, ckv, _ = _split_latents(_latent_tiles(p_ref), _low_lanes(ts))
        r = lax.rsqrt(jnp.sum(cq * cq, axis=-1, keepdims=True) * (1.0 / Q_RANK) + NORM_EPS)
        cqn_ref[...] = (cq * r * qg_ref[...]).astype(BF16)
        r2 = lax.rsqrt(jnp.mean(ckv * ckv, axis=-1, keepdims=True) + NORM_EPS)
        ckvn_ref[...] = (ckv * r2 * kvg_ref[...]).astype(BF16)

    return pl.pallas_call(
        body, name=name,
        out_shape=(jax.ShapeDtypeStruct((S, Q_RANK_PAD), BF16), jax.ShapeDtypeStruct((S, KV_RANK), BF16)),
        grid=(S // ts,),
        in_specs=[_row_spec(ts, P_LATENT), _vec_spec(Q_RANK_PAD), _vec_spec(KV_RANK)],
        out_specs=(_row_spec(ts, Q_RANK_PAD), _row_spec(ts, KV_RANK)),
        compiler_params=_params("parallel"),
    )(p, qg, kvg)


def _mla_pack(q, kv, p, cos_t, sin_t, name):
    S = q.shape[0]
    ts = min(256, S)
    pair_w = 2 * (NOPE + ROPE)
    head_w = NOPE + VDIM

    def body(q_ref, kv_ref, kr_ref, cos_ref, sin_ref, qo_ref, ko_ref, vo_ref, vt_ref):
        cosv = cos_ref[...]
        sinv = sin_ref[...]
        low = _low_lanes(ts)
        kr = jnp.where(low, pltpu.roll(kr_ref[...], ROPE, 1), 0.0)
        kr = (kr * cosv + _swap_halves(kr) * sinv).astype(BF16)
        for pair in range(HEADS // 2):
            t0, t1, t2 = (q_ref[:, pair * pair_w + t * LANE:pair * pair_w + (t + 1) * LANE] for t in range(3))
            nope_b = jnp.where(low, pltpu.roll(t1, ROPE, 1), pltpu.roll(t2, ROPE, 1))
            ropes = jnp.where(low, t1, t2)
            roped = (ropes * cosv + _swap_halves(ropes) * sinv) * Q_FOLD
            qo_ref[2 * pair, :, 0:NOPE] = (t0 * Q_FOLD).astype(BF16)
            qo_ref[2 * pair, :, NOPE:QK_PAD] = jnp.where(low, roped, 0.0).astype(BF16)
            qo_ref[2 * pair + 1, :, 0:NOPE] = (nope_b * Q_FOLD).astype(BF16)
            qo_ref[2 * pair + 1, :, NOPE:QK_PAD] = jnp.where(low, pltpu.roll(roped, ROPE, 1), 0.0).astype(BF16)
        eye = _identity(VDIM)
        for h in range(HEADS):
            ko_ref[h, :, 0:NOPE] = kv_ref[:, h * head_w:h * head_w + NOPE].astype(BF16)
            ko_ref[h, :, NOPE:QK_PAD] = kr
            vh = kv_ref[:, h * head_w + NOPE:(h + 1) * head_w].astype(BF16)
            vo_ref[h] = vh
            vt_ref[h] = lax.dot_general(eye, vh, NT, preferred_element_type=F32).astype(BF16)

    T = min(ATTN_TILE, S)
    per_tile = T // ts
    return pl.pallas_call(
        body, name=name,
        out_shape=(jax.ShapeDtypeStruct((HEADS, S, QK_PAD), BF16), jax.ShapeDtypeStruct((HEADS, S, QK_PAD), BF16),
                   jax.ShapeDtypeStruct((HEADS, S, VDIM), BF16), jax.ShapeDtypeStruct((HEADS, S // T, VDIM, T), BF16)),
        grid=(S // ts,),
        in_specs=[_row_spec(ts, q.shape[1]), _row_spec(ts, kv.shape[1]),
                  pl.BlockSpec((ts, LANE), lambda i: (i, P_LATENT // LANE - 1)),
                  _row_spec(ts, LANE), _row_spec(ts, LANE)],
        out_specs=(pl.BlockSpec((HEADS, ts, QK_PAD), lambda i: (0, i, 0)),
                   pl.BlockSpec((HEADS, ts, QK_PAD), lambda i: (0, i, 0)),
                   pl.BlockSpec((HEADS, ts, VDIM), lambda i: (0, i, 0)),
                   pl.BlockSpec((HEADS, None, VDIM, ts), lambda i: (0, i // per_tile, 0, i % per_tile))),
        compiler_params=_params("parallel"),
    )(q, kv, p, cos_t, sin_t)


def _mla_unpack(dQ, dK, dV, cos_t, sin_t, name):
    S = dQ.shape[1]
    ts = min(256, S)
    pair_w = 2 * (NOPE + ROPE)
    head_w = NOPE + VDIM

    def body(dq_ref, dk_ref, dv_ref, cos_ref, sin_ref, q_ref, kv_ref, kr_ref):
        cosv = cos_ref[...]
        sinv = sin_ref[...]
        low = _low_lanes(ts)
        for pair in range(HEADS // 2):
            blk = dq_ref[2 * pair, :, NOPE:QK_PAD] + pltpu.roll(dq_ref[2 * pair + 1, :, NOPE:QK_PAD], ROPE, 1)
            ropes = blk * cosv - _swap_halves(blk) * sinv
            nope_b = pltpu.roll(dq_ref[2 * pair + 1, :, 0:NOPE], ROPE, 1)
            q_ref[:, pair * pair_w:pair * pair_w + LANE] = dq_ref[2 * pair, :, 0:NOPE].astype(BF16)
            q_ref[:, pair * pair_w + LANE:pair * pair_w + 2 * LANE] = jnp.where(low, ropes, nope_b).astype(BF16)
            q_ref[:, pair * pair_w + 2 * LANE:(pair + 1) * pair_w] = jnp.where(low, nope_b, ropes).astype(BF16)
        dkr = dk_ref[0, :, NOPE:QK_PAD]
        for h in range(1, HEADS):
            dkr = dkr + dk_ref[h, :, NOPE:QK_PAD]
        kr_ref[...] = dkr * cosv - _swap_halves(dkr) * sinv
        for h in range(HEADS):
            kv_ref[:, h * head_w:h * head_w + NOPE] = dk_ref[h, :, 0:NOPE].astype(BF16)
            kv_ref[:, h * head_w + NOPE:(h + 1) * head_w] = dv_ref[h].astype(BF16)

    return pl.pallas_call(
        body, name=name,
        out_shape=(jax.ShapeDtypeStruct((S, HEADS * (NOPE + ROPE)), BF16),
                   jax.ShapeDtypeStruct((S, HEADS * (NOPE + VDIM)), BF16),
                   jax.ShapeDtypeStruct((S, LANE), F32)),
        grid=(S // ts,),
        in_specs=[pl.BlockSpec((HEADS, ts, QK_PAD), lambda i: (0, i, 0)),
                  pl.BlockSpec((HEADS, ts, QK_PAD), lambda i: (0, i, 0)),
                  pl.BlockSpec((HEADS, ts, VDIM), lambda i: (0, i, 0)),
                  _row_spec(ts, LANE), _row_spec(ts, LANE)],
        out_specs=(_row_spec(ts, HEADS * (NOPE + ROPE)), _row_spec(ts, HEADS * (NOPE + VDIM)),
                   _row_spec(ts, LANE)),
        compiler_params=_params("parallel"),
    )(dQ, dK, dV, cos_t, sin_t)


def _mla_gate_fwd(o, p, name):
    S, W = o.shape
    ts = min(256, S)
    wb = P_LATENT

    def body(o_ref, z_ref, y_ref):
        z = z_ref[...]
        y_ref[...] = (o_ref[...] * (z * jax.nn.sigmoid(z))).astype(BF16)

    return pl.pallas_call(
        body, name=name, out_shape=jax.ShapeDtypeStruct((S, W), BF16), grid=(S // ts, W // wb),
        in_specs=[pl.BlockSpec((ts, wb), lambda i, j: (i, j)), pl.BlockSpec((ts, wb), lambda i, j: (i, j + 1))],
        out_specs=pl.BlockSpec((ts, wb), lambda i, j: (i, j)), compiler_params=_params("parallel", "parallel"),
    )(o, p)


def _mla_gate_bwd(dyv, p, name):
    S, W = dyv.shape
    ts = min(256, S)
    wb = P_LATENT

    def body(d_ref, z_ref, do_ref):
        z = z_ref[...]
        do_ref[...] = d_ref[...] * (z * jax.nn.sigmoid(z))

    return pl.pallas_call(
        body, name=name, out_shape=jax.ShapeDtypeStruct((S, W), F32), grid=(S // ts, W // wb),
        in_specs=[pl.BlockSpec((ts, wb), lambda i, j: (i, j)), pl.BlockSpec((ts, wb), lambda i, j: (i, j + 1))],
        out_specs=pl.BlockSpec((ts, wb), lambda i, j: (i, j)), compiler_params=_params("parallel", "parallel"),
    )(dyv, p)


def _mla_mid_bwd(p, dcqn, dckvn, dkr, dyv, o, qg, kvg, name):
    S, PW = p.shape
    W = o.shape[1]
    ts = min(256, S)
    nt = Q_RANK_PAD // LANE

    def rms_bwd(xv, dy, g, count):
        r = lax.rsqrt(jnp.sum(xv * xv, axis=-1, keepdims=True) * (1.0 / count) + NORM_EPS)
        xn = xv * r
        dg = jnp.sum(dy * xn, axis=0, keepdims=True)
        dxn = dy * g
        dx = r * (dxn - xn * (jnp.sum(dxn * xn, axis=-1, keepdims=True) * (1.0 / count)))
        return dx, dg

    def body(p_ref, dcq_ref, dckv_ref, dkr_ref, dyv_ref, o_ref, qg_ref, kvg_ref, dp_ref, dqg_ref, dkvg_ref):
        @pl.when(pl.program_id(0) == 0)
        def _():
            dqg_ref[...] = jnp.zeros_like(dqg_ref)
            dkvg_ref[...] = jnp.zeros_like(dkvg_ref)

        low = _low_lanes(ts)
        cq, ckv, _ = _split_latents(_latent_tiles(p_ref), low)
        dcq, dg = rms_bwd(cq, dcq_ref[...], qg_ref[...], Q_RANK)
        dqg_ref[...] += dg
        dckv, dg = rms_bwd(ckv, dckv_ref[...], kvg_ref[...], KV_RANK)
        dkvg_ref[...] += dg
        moved = [pltpu.roll(dckv[:, t * LANE:(t + 1) * LANE], ROPE, 1) for t in range(nt)]
        moved.append(pltpu.roll(dkr_ref[...], ROPE, 1))
        for t in range(nt - 1):
            dp_ref[:, t * LANE:(t + 1) * LANE] = dcq[:, t * LANE:(t + 1) * LANE].astype(BF16)
        dp_ref[:, (nt - 1) * LANE:nt * LANE] = jnp.where(low, dcq[:, (nt - 1) * LANE:nt * LANE], moved[0]).astype(BF16)
        for t in range(nt):
            dp_ref[:, (nt + t) * LANE:(nt + t + 1) * LANE] = jnp.where(low, moved[t], moved[t + 1]).astype(BF16)
        z = p_ref[:, P_LATENT:PW]
        sig = jax.nn.sigmoid(z)
        dp_ref[:, P_LATENT:PW] = (dyv_ref[...] * o_ref[...] * (sig * (1.0 + z * (1.0 - sig)))).astype(BF16)

    return pl.pallas_call(
        body, name=name,
        out_shape=(jax.ShapeDtypeStruct((S, PW), BF16), jax.ShapeDtypeStruct((1, Q_RANK_PAD), F32),
                   jax.ShapeDtypeStruct((1, KV_RANK), F32)),
        grid=(S // ts,),
        in_specs=[_row_spec(ts, PW), _row_spec(ts, Q_RANK_PAD), _row_spec(ts, KV_RANK), _row_spec(ts, LANE),
                  _row_spec(ts, W), _row_spec(ts, W), _vec_spec(Q_RANK_PAD), _vec_spec(KV_RANK)],
        out_specs=(_row_spec(ts, PW), _vec_spec(Q_RANK_PAD), _vec_spec(KV_RANK)),
        compiler_params=_params("arbitrary"),
    )(p, dcqn, dckvn, dkr, dyv, o, qg, kvg)


def _tile_mask(T):
    row = lax.broadcasted_iota(jnp.int32, (T, T), 0) // CHUNK
    col = lax.broadcasted_iota(jnp.int32, (T, T), 1) // CHUNK
    return col <= row


def _attn_bwd(Q, K, V, o, do, lse, name):
    H, S, _ = Q.shape
    T = min(ATTN_TILE, S)
    nq = S // T

    def body(q_ref, k_ref, v_ref, o_ref, do_ref, lse_ref, dq_ref, dk_ref, dv_ref, dk_scr, dv_scr, s_scr, dp_scr):
        ki = pl.program_id(1)

        @pl.when(ki == 0)
        def _():
            dq_ref[...] = jnp.zeros_like(dq_ref)

        dk_scr[...] = jnp.zeros_like(dk_scr)
        dv_scr[...] = jnp.zeros_like(dv_scr)
        k = k_ref[...]
        v = v_ref[...]

        def scores(i):
            rows = pl.ds(pl.multiple_of(i * T, T), T)
            s = lax.dot_general(q_ref[rows, :], k, NT, preferred_element_type=F32)
            dp = lax.dot_general(do_ref[rows, :].astype(BF16), v, NT, preferred_element_type=F32)
            return s, dp

        def grads(i, s, dp, masked):
            rows = pl.ds(pl.multiple_of(i * T, T), T)
            do_f = do_ref[rows, :]
            delta = jnp.sum(do_f * o_ref[rows, :], axis=1, keepdims=True)
            pr = jnp.exp2(s - lse_ref[rows, 0:1])
            if masked:
                pr = jnp.where(_tile_mask(T), pr, 0.0)
            dv_scr[...] += lax.dot_general(pr.astype(BF16), do_f.astype(BF16), TN, preferred_element_type=F32)
            ds = (pr * (dp - delta)).astype(BF16)
            dk_scr[...] += lax.dot_general(ds, q_ref[rows, :], TN, preferred_element_type=F32)
            dq_ref[rows, :] += lax.dot_general(ds, k, NN, preferred_element_type=F32) * ATTN_SCALE

        s_scr[...], dp_scr[...] = scores(ki)

        @pl.when(ki + 1 == nq)
        def _():
            grads(ki, s_scr[...], dp_scr[...], True)

        @pl.when(ki + 1 < nq)
        def _():
            def step(i, masked):
                nxt_s, nxt_dp = scores(i + 1)
                grads(i, s_scr[...], dp_scr[...], masked)
                s_scr[...] = nxt_s
                dp_scr[...] = nxt_dp

            def loop_step(i, carry):
                step(i, False)
                return carry

            step(ki, True)
            lax.fori_loop(ki + 1, nq - 1, loop_step, 0)
            grads(nq - 1, s_scr[...], dp_scr[...], False)

        dk_ref[...] = dk_scr[...] * (1.0 / LOG2E)
        dv_ref[...] = dv_scr[...]

    return pl.pallas_call(
        body, name=name,
        out_shape=(jax.ShapeDtypeStruct((H, S, QK_PAD), F32), jax.ShapeDtypeStruct((H, S, QK_PAD), F32),
                   jax.ShapeDtypeStruct((H, S, VDIM), F32)),
        grid=(H, nq),
        in_specs=[pl.BlockSpec((None, S, QK_PAD), lambda h, j: (h, 0, 0)),
                  pl.BlockSpec((None, T, QK_PAD), lambda h, j: (h, j, 0)),
                  pl.BlockSpec((None, T, VDIM), lambda h, j: (h, j, 0)),
                  pl.BlockSpec((S, VDIM), lambda h, j: (0, h)),
                  pl.BlockSpec((S, VDIM), lambda h, j: (0, h)),
                  pl.BlockSpec((None, S, LANE), lambda h, j: (h, 0, 0))],
        out_specs=(pl.BlockSpec((None, S, QK_PAD), lambda h, j: (h, 0, 0)),
                   pl.BlockSpec((None, T, QK_PAD), lambda h, j: (h, j, 0)),
                   pl.BlockSpec((None, T, VDIM), lambda h, j: (h, j, 0))),
        scratch_shapes=[pltpu.VMEM((T, QK_PAD), F32), pltpu.VMEM((T, VDIM), F32),
                        pltpu.VMEM((T, T), F32), pltpu.VMEM((T, T), F32)],
        compiler_params=_params("parallel", "arbitrary"),
    )(Q, K, V, o, do, lse)


def _identity(n):
    return (lax.broadcasted_iota(jnp.int32, (n, n), 0) == lax.broadcasted_iota(jnp.int32, (n, n), 1)).astype(BF16)


def _key_le_query(rows, cols, col0):
    key = lax.broadcasted_iota(jnp.int32, (rows, cols), 0) // CHUNK
    query = (lax.broadcasted_iota(jnp.int32, (rows, cols), 1) + col0) // CHUNK
    return key <= query


def _attn_fwd_t(Q, K, VT, name):
    H, nT, _, T = VT.shape
    S = nT * T
    n_part = 2 if T % 256 == 0 else 1
    Tq = T // n_part

    def body(q_ref, k_ref, vt_ref, o_ref, lse_ref, m_scr, l_scr, acc_scr, s_scr):
        qi = pl.program_id(1)
        m_scr[...] = jnp.full_like(m_scr, -jnp.inf)
        l_scr[...] = jnp.zeros_like(l_scr)
        acc_scr[...] = jnp.zeros_like(acc_scr)

        def scores(j):
            kt = k_ref[pl.ds(pl.multiple_of(j * T, T), T), :]
            return lax.dot_general(kt, q_ref[...], NT, preferred_element_type=F32)

        def softmax_step(j, masked):
            vt = vt_ref[j]
            for part in range(n_part):
                sub = slice(part * Tq, (part + 1) * Tq)
                st = s_scr[:, sub]
                if masked:
                    st = jnp.where(_key_le_query(T, Tq, part * Tq), st, -1e30)
                m_prev = m_scr[:, sub]
                m_new = jnp.maximum(m_prev, jnp.max(st, axis=0, keepdims=True))
                pt = jnp.exp2(st - m_new)
                alpha = jnp.exp2(m_prev - m_new)
                l_scr[:, sub] = alpha * l_scr[:, sub] + jnp.sum(pt, axis=0, keepdims=True)
                acc_scr[:, sub] = alpha * acc_scr[:, sub] + lax.dot_general(
                    vt, pt.astype(BF16), NN, preferred_element_type=F32)
                m_scr[:, sub] = m_new

        s_scr[...] = scores(0)

        def step(j, carry):
            nxt = scores(j + 1)
            softmax_step(j, False)
            s_scr[...] = nxt
            return carry

        lax.fori_loop(0, qi, step, 0)
        softmax_step(qi, True)
        l = l_scr[...]
        o_ref[...] = jnp.transpose(acc_scr[...] / l)
        lse_ref[...] = jnp.transpose(jnp.broadcast_to(m_scr[...] + jnp.log2(l), (LANE, T)))

    return pl.pallas_call(
        body, name=name,
        out_shape=(jax.ShapeDtypeStruct((S, H * VDIM), F32), jax.ShapeDtypeStruct((H, S, LANE), F32)),
        grid=(H, nT),
        in_specs=[pl.BlockSpec((None, T, QK_PAD), lambda h, i: (h, i, 0)),
                  pl.BlockSpec((None, S, QK_PAD), lambda h, i: (h, 0, 0)),
                  pl.BlockSpec((None, nT, VDIM, T), lambda h, i: (h, 0, 0, 0))],
        out_specs=(pl.BlockSpec((T, VDIM), lambda h, i: (i, h)),
                   pl.BlockSpec((None, T, LANE), lambda h, i: (h, i, 0))),
        scratch_shapes=[pltpu.VMEM((1, T), F32), pltpu.VMEM((1, T), F32), pltpu.VMEM((VDIM, T), F32),
                        pltpu.VMEM((T, T), F32)],
        compiler_params=_params("parallel", "arbitrary"),
    )(Q, K, VT)


def _adamw(w, g, m, v, name):
    shape = w.shape
    C = shape[-1]
    R = math.prod(shape[:-1])
    flat = [t.reshape(R, C) for t in (w, g, m, v)]
    tr = _row_tile(R, C * 4)

    def body(w_ref, g_ref, m_ref, v_ref, d_ref, nm_ref, nv_ref):
        gv = g_ref[...]
        m_new = ADAM_B1 * m_ref[...] + (1.0 - ADAM_B1) * gv
        v_new = ADAM_B2 * v_ref[...] + (1.0 - ADAM_B2) * jnp.square(gv)
        m_hat = m_new / (1.0 - ADAM_B1 ** ADAM_STEP)
        v_hat = v_new / (1.0 - ADAM_B2 ** ADAM_STEP)
        d_ref[...] = -ADAM_LR * (m_hat / (jnp.sqrt(v_hat) + ADAM_EPS) + ADAM_WD * w_ref[...])
        nm_ref[...] = m_new
        nv_ref[...] = v_new

    spec = pl.BlockSpec((tr, C), lambda i: (i, 0))
    out = jax.ShapeDtypeStruct((R, C), F32)
    d, nm, nv = pl.pallas_call(
        body, name=name, out_shape=(out, out, out), grid=(R // tr,),
        in_specs=[spec] * 4, out_specs=(spec, spec, spec), compiler_params=_params("parallel"),
    )(*flat)
    return d.reshape(shape), nm.reshape(shape), nv.reshape(shape)


def _sum_into_half(r, buf, layer, ci, n_layers, name):
    n, M, N = r.shape
    tr = _row_tile(M, N * 4 * n, 4 << 20)

    def body(c_ref, r_ref, *rest):
        o_ref = rest[-1]
        acc = r_ref[0].astype(F32)
        for s in range(1, n):
            acc = acc + r_ref[s].astype(F32)
        o_ref[...] = acc

    in_specs = [pl.BlockSpec((n, tr, N), lambda i, c: (0, i, 0))]
    operands = [ci.reshape(1), r]
    aliases = {}
    if buf is not None:
        in_specs.append(ANY)
        operands.append(buf)
        aliases = {2: 0}
    return pl.pallas_call(
        body, name=name, out_shape=jax.ShapeDtypeStruct((n_layers, 2, M, N), F32),
        grid_spec=pltpu.PrefetchScalarGridSpec(
            num_scalar_prefetch=1, grid=(M // tr,), in_specs=in_specs,
            out_specs=pl.BlockSpec((None, None, tr, N), lambda i, c: (layer, c[0], i, 0))),
        input_output_aliases=aliases, compiler_params=_params("parallel"),
    )(*operands)


def _sum_slots(r, name):
    n, M, N = r.shape
    tr = _row_tile(M, N * 4 * n, 4 << 20)

    def body(r_ref, o_ref):
        acc = r_ref[0].astype(F32)
        for s in range(1, n):
            acc = acc + r_ref[s].astype(F32)
        o_ref[...] = acc

    return pl.pallas_call(
        body, name=name, out_shape=jax.ShapeDtypeStruct((M, N), F32), grid=(M // tr,),
        in_specs=[pl.BlockSpec((n, tr, N), lambda i: (0, i, 0))],
        out_specs=pl.BlockSpec((tr, N), lambda i: (i, 0)), compiler_params=_params("parallel"),
    )(r)


ANY = pl.BlockSpec(memory_space=pl.ANY)
DMA_CHUNK_BYTES = 1 << 20
DMA_MAX_CHUNKS = 16
PEER_ORDER = (1, 4, 5, 2, 3, 6, 7)


def _position():
    return lax.axis_index("x"), lax.axis_index("y"), lax.axis_index("c")


def _row_chunks(shape, dtype):
    rows, cols = shape
    n = max(1, min(DMA_MAX_CHUNKS, rows * cols * jnp.dtype(dtype).itemsize // DMA_CHUNK_BYTES))
    while n > 1 and (rows % n or (rows // n) % 16):
        n -= 1
    step = rows // n
    return [pl.ds(q * step, step) for q in range(n)]


def _all_gather8(xs, name):
    n = len(xs)

    def body(*refs):
        x_refs, o_refs = refs[:n], refs[n:2 * n]
        send_sems, recv_sems, local_sems = refs[2 * n:]
        x, y, c = _position()
        me, sibling = (x, y, c), (x, y, 1 - c)
        chips = [(1 - x, y), (x, 1 - y), (1 - x, 1 - y)]

        def slot(a, dev, rows):
            return o_refs[a].at[4 * dev[0] + 2 * dev[1] + dev[2], rows]

        def copy(a, k, block, to, rows, from_input=False):
            return pltpu.make_async_remote_copy(
                src_ref=x_refs[a].at[rows] if from_input else slot(a, block, rows), dst_ref=slot(a, block, rows),
                send_sem=send_sems.at[a, k], recv_sem=recv_sems.at[a, k],
                device_id=to, device_id_type=MESH)

        def mine(a, rows):
            return pltpu.make_async_copy(x_refs[a].at[rows], slot(a, me, rows), local_sems.at[a])

        chunks = [_row_chunks(t.shape, t.dtype) for t in xs]
        whole = [pl.ds(0, t.shape[0]) for t in xs]
        for a in range(n):
            for rows in chunks[a]:
                mine(a, rows).start()
        sent = []
        for a in range(n):
            for k, to in enumerate([sibling] + [(*chip, c) for chip in chips]):
                for rows in chunks[a]:
                    copy(a, k, me, to, rows, from_input=True).start()
                sent.append(copy(a, k, me, to, whole[a], from_input=True))
        for a in range(n):
            for j, chip in enumerate(chips):
                copy(a, 1 + j, (*chip, c), me, whole[a]).wait_recv()
                for rows in chunks[a]:
                    copy(a, 4 + j, (*chip, c), sibling, rows).start()
                sent.append(copy(a, 4 + j, (*chip, c), sibling, whole[a]))
        for a in range(n):
            copy(a, 0, sibling, me, whole[a]).wait_recv()
            for j, chip in enumerate(chips):
                copy(a, 4 + j, (*chip, 1 - c), me, whole[a]).wait_recv()
        for cp in sent:
            cp.wait_send()
        for a in range(n):
            mine(a, whole[a]).wait()

    return pl.pallas_call(
        body, name=name,
        out_shape=[jax.ShapeDtypeStruct((8,) + t.shape, t.dtype) for t in xs],
        in_specs=[ANY] * n, out_specs=[ANY] * n,
        scratch_shapes=[pltpu.SemaphoreType.DMA((n, 7)), pltpu.SemaphoreType.DMA((n, 7)),
                        pltpu.SemaphoreType.DMA((n,))],
    )(*xs)


HBM = pl.BlockSpec(memory_space=pltpu.HBM)
SEM = pl.BlockSpec(memory_space=pltpu.SEMAPHORE)
EFFECT = pltpu.SideEffectType.DATAFLOW_SIDE_EFFECTING


def _peer(m, x, y, c):
    return ((1 - x) if m & 4 else x, (1 - y) if m & 2 else y, (1 - c) if m & 1 else c)


def _send_copies(src_refs, land_refs, send_sems, recv_sems, broadcast):
    x, y, c = _position()
    my = 4 * x + 2 * y + c
    out = []
    for a in range(len(src_refs)):
        for m in PEER_ORDER:
            px, py, pc = _peer(m, x, y, c)
            src = src_refs[a] if broadcast else src_refs[a].at[4 * px + 2 * py + pc]
            out.append(pltpu.make_async_remote_copy(
                src_ref=src, dst_ref=land_refs[a].at[my], send_sem=send_sems[a], recv_sem=recv_sems[a],
                device_id=(px, py, pc), device_id_type=MESH))
    return out


def _send_drain(land_refs, send_sems, recv_sems):
    x, y, c = _position()
    for a in range(len(land_refs)):
        seven = land_refs[a].at[pl.ds(0, 7)]
        both = pltpu.make_async_remote_copy(
            src_ref=seven, dst_ref=seven, send_sem=send_sems[a], recv_sem=recv_sems[a],
            device_id=(x, y, c), device_id_type=MESH)
        both.wait_send()
        both.wait_recv()


def _send_start(srcs, lands, after, broadcast, name):
    n = len(srcs)
    extra = [] if after is None else [after]

    def body(*refs):
        src_refs, land_refs = refs[:n], refs[n:2 * n]
        outs = refs[2 * n + len(extra):]
        send_sems, recv_sems = outs[:n], outs[n:2 * n]
        token = refs[-1]
        for cp in _send_copies(src_refs, land_refs, send_sems, recv_sems, broadcast):
            cp.start()
        token[...] = jnp.zeros_like(token)

    hbm = [pltpu.with_memory_space_constraint(t, pltpu.HBM) for t in list(srcs) + list(lands)]
    res = pl.pallas_call(
        body, name=name,
        out_shape=(*[pltpu.SemaphoreType.DMA(())] * (2 * n),
                   *[pltpu.HBM(t.shape, t.dtype) for t in hbm], jax.ShapeDtypeStruct((8, LANE), F32)),
        in_specs=[HBM] * (2 * n) + [ANY] * len(extra),
        out_specs=(*[SEM] * (2 * n), *[HBM] * (2 * n), pl.BlockSpec(memory_space=pltpu.VMEM)),
        input_output_aliases={i: 2 * n + i for i in range(2 * n)},
        compiler_params=pltpu.CompilerParams(has_side_effects=EFFECT),
    )(*hbm, *extra)
    return dict(sems=res[:2 * n], srcs=res[2 * n:3 * n], lands=res[3 * n:4 * n], token=res[-1], broadcast=broadcast)


def _send_wait(started, after, name):
    n = len(started["srcs"])

    def body(*refs):
        land_refs = refs[n:2 * n]
        send_sems, recv_sems = refs[2 * n:3 * n], refs[3 * n:4 * n]
        _send_drain(land_refs, send_sems, recv_sems)

    operands = list(started["srcs"]) + list(started["lands"])
    res = pl.pallas_call(
        body, name=name,
        out_shape=[pltpu.HBM(t.shape, t.dtype) for t in operands],
        in_specs=[HBM] * (2 * n) + [SEM] * (2 * n) + [ANY],
        out_specs=[HBM] * (2 * n),
        input_output_aliases={i: i for i in range(2 * n)},
        compiler_params=pltpu.CompilerParams(has_side_effects=EFFECT),
    )(*operands, *started["sems"], after)
    return res[n:]


def _own_slot(block, dev):
    zone = lax.empty((8,) + block.shape, block.dtype)
    return lax.dynamic_update_slice(zone, block[None], (dev, 0, 0))


def _pair_swap(bufs, name):
    n = len(bufs)
    pieces = [(a, l) for a, t in enumerate(bufs) for l in range(t.shape[0])]

    def body(*refs):
        b_refs = refs[n:2 * n]
        send_sems, recv_sems = refs[2 * n:]
        x, y, c = _position()

        def copy(k, rows):
            a, l = pieces[k]
            half = b_refs[a].at[l, c, rows]
            return pltpu.make_async_remote_copy(
                src_ref=half, dst_ref=half, send_sem=send_sems.at[k], recv_sem=recv_sems.at[k],
                device_id=(x, y, 1 - c), device_id_type=MESH)

        chunks = [_row_chunks(bufs[a].shape[2:], bufs[a].dtype) for a, _ in pieces]
        whole = [pl.ds(0, bufs[a].shape[2]) for a, _ in pieces]
        for k in range(len(pieces)):
            for rows in chunks[k]:
                copy(k, rows).start()
        for k in range(len(pieces)):
            copy(k, whole[k]).wait_recv()
        for k in range(len(pieces)):
            copy(k, whole[k]).wait_send()

    return pl.pallas_call(
        body, name=name,
        out_shape=[jax.ShapeDtypeStruct(t.shape, t.dtype) for t in bufs],
        in_specs=[ANY] * n, out_specs=[ANY] * n,
        input_output_aliases={a: a for a in range(n)},
        scratch_shapes=[pltpu.SemaphoreType.DMA((len(pieces),)), pltpu.SemaphoreType.DMA((len(pieces),))],
    )(*bufs)


def _pack_rows(parts):
    flat = jnp.concatenate([t.reshape(-1).astype(F32) for t in parts])
    pad = (-flat.shape[0]) % (256 * LANE)
    return jnp.pad(flat, (0, pad)).reshape(-1, LANE)


def _my_half(w2d, ci):
    half = w2d.shape[0] // 2
    return lax.dynamic_slice_in_dim(w2d, ci * half, half, axis=0).astype(BF16)


def _col_view(g):
    _, half, Cs = g.shape
    return g.reshape(4, 2 * half, Cs)


def _row_view(g):
    _, half, C = g.shape
    return g.reshape(8 * half, C)


def _rope_tables(S):
    pos = jnp.arange(S, dtype=F32)
    inv_freq = ROPE_THETA ** (-jnp.arange(0, ROPE, 2, dtype=F32) / ROPE)
    ang = pos[:, None] * inv_freq[None, :]
    cos, sin = jnp.cos(ang), jnp.sin(ang)
    cos_t = jnp.concatenate([cos, cos, cos, cos], axis=1)
    sin_t = jnp.concatenate([-sin, sin, -sin, sin], axis=1)
    return cos_t, sin_t


def kernel(x, c, ada_w, ada_b, pre_g, post_g, sgu_w_in, sgu_norm_g, sgu_w_s, sgu_b_s, sgu_w_out, mla_w_in, mla_q_norm_g, mla_kv_norm_g, mla_w_uq, mla_w_ukv, mla_w_out, loss_target, m_ada_w, m_ada_b, m_pre_g, m_post_g, m_sgu_w_in, m_sgu_norm_g, m_sgu_w_s, m_sgu_b_s, m_sgu_w_out, m_mla_w_in, m_mla_q_norm_g, m_mla_kv_norm_g, m_mla_w_uq, m_mla_w_ukv, m_mla_w_out, v_ada_w, v_ada_b, v_pre_g, v_post_g, v_sgu_w_in, v_sgu_norm_g, v_sgu_w_s, v_sgu_b_s, v_sgu_w_out, v_mla_w_in, v_mla_q_norm_g, v_mla_kv_norm_g, v_mla_w_uq, v_mla_w_ukv, v_mla_w_out):
    S, D = x.shape[1], x.shape[2]
    depth = ada_w.shape[0]
    E = sgu_w_out.shape[1] * 4
    xi, yi, ci = _position()
    chip = 2 * xi + yi
    dev = 4 * xi + 2 * yi + ci
    x0 = x.reshape(S, D)
    target = loss_target.reshape(S, D)

    small = _pack_rows([c, mla_q_norm_g, mla_kv_norm_g])
    mixer_w = dict(sin=sgu_w_in, sout=sgu_w_out, min=mla_w_in, uq=mla_w_uq, ukv=mla_w_ukv, mout=mla_w_out)
    small_g, first_g = _all_gather8([small, _my_half(sgu_w_in[0], ci)], "gather_first")
    small_all = small_g.reshape(8, -1)
    gathered_w = {("sin", 0): first_g}
    qn_w, kvn_w = mla_q_norm_g.shape[1], mla_kv_norm_g.shape[1]
    c_all = small_all[:, :D]
    qn_all = small_all[0::2, D:D + 2 * qn_w].reshape(4, 2, qn_w)
    kvn_all = small_all[0::2, D + 2 * qn_w:D + 2 * qn_w + 2 * kvn_w].reshape(4, 2, kvn_w)
    q_gain = jnp.pad(jnp.transpose(qn_all, (1, 0, 2)).reshape(2, 1, Q_RANK), ((0, 0), (0, 0), (0, Q_RANK_PAD - Q_RANK)))
    kv_gain = jnp.transpose(kvn_all, (1, 0, 2)).reshape(2, 1, KV_RANK)

    views = {}

    def weight(t, j):
        if (t, j) not in views:
            g = gathered_w[(t, j)]
            v = _row_view(g) if t in ("sout", "mout") else _col_view(g)
            if t == "uq":
                v = jnp.pad(v, ((0, 0), (0, Q_RANK_PAD - Q_RANK), (0, 0)))
            views[(t, j)] = v
        return views[(t, j)]

    cols = ada_w.shape[2]
    ada_b_cols = lax.dynamic_slice_in_dim(ada_b, chip * cols, cols, axis=1)
    c_pad = jnp.pad(c_all, ((0, 8), (0, 0)))
    mod_cols = _ada_mod(c_pad, ada_w, ada_b_cols, "ada_mod")[:, :8]
    mod_g, = _all_gather8([mod_cols.reshape(depth * 8, cols)], "gather_mod")
    mod_all = jnp.transpose(mod_g[0::2].reshape(4, depth, 8, cols), (1, 2, 0, 3)).reshape(depth, 8, 4 * cols)

    groups = [("sout0", [("sout", 0)]), ("mla0", [(t, 0) for t in ("min", "uq", "ukv", "mout")]),
              ("sgu1", [("sin", 1), ("sout", 1)]), ("mla1", [(t, 1) for t in ("min", "uq", "ukv", "mout")])]
    sends = {}
    behind = mod_g
    for gname, items in groups:
        blocks = [_my_half(mixer_w[t][j], ci) for t, j in items]
        sends[gname] = _send_start(blocks, [_own_slot(b, dev) for b in blocks], behind, True, f"send_{gname}")
        behind = sends[gname]["token"]

    def arrive(gname, after):
        lands = _send_wait(sends[gname], after, f"arrive_{gname}")
        gathered_w.update(zip(dict(groups)[gname], lands))
    mod = lax.dynamic_index_in_dim(mod_all, dev, 1, keepdims=False)
    shift = [mod[i:i + 1, :D] for i in range(depth)]
    scale = [mod[i:i + 1, D:2 * D] for i in range(depth)]
    gate = [mod[i:i + 1, 2 * D:] for i in range(depth)]

    cos_t, sin_t = _rope_tables(S)
    b_bc = jnp.broadcast_to(sgu_b_s[:, :, :, None], sgu_b_s.shape + (LANE,))
    w_sT = jnp.swapaxes(sgu_w_s, 2, 3)

    saved = []
    xs = x0
    for i in range(depth):
        j = i // 2
        tag = f"l{i}"
        h = _pre_fwd(xs, pre_g[i:i + 1], scale[i], shift[i], f"pre_fwd_{tag}", after=(behind,) if i == 0 else ())
        if i % 2 == 0:
            if j > 0:
                arrive(f"sgu{j}", h)
            uvz = _mm(h, weight("sin", j), b_sharded=True, name=f"sgu_in_{tag}")
            y = _sgu_gate_fwd(uvz, sgu_norm_g[j:j + 1], sgu_w_s[j], b_bc[j], f"sgu_gate_fwd_{tag}")
            if j == 0:
                arrive("sout0", y)
            out = _mm(y, weight("sout", j), name=f"sgu_out_{tag}")
            saved.append(dict(x=xs, h=h, uvz=uvz, y=y, out=out))
        else:
            arrive(f"mla{j}", h)
            p = _mm(h, weight("min", j), b_sharded=True, name=f"mla_in_{tag}")
            cqn, ckvn = _mla_mid_fwd(p, q_gain[j], kv_gain[j], f"mla_mid_fwd_{tag}")
            q = _mm(cqn, weight("uq", j), b_sharded=True, name=f"mla_uq_{tag}")
            kv = _mm(ckvn, weight("ukv", j), b_sharded=True, name=f"mla_ukv_{tag}")
            Q, K, V, VT = _mla_pack(q, kv, p, cos_t, sin_t, f"mla_pack_{tag}")
            o, lse = _attn_fwd_t(Q, K, VT, f"attn_fwd_{tag}")
            y = _mla_gate_fwd(o, p, f"mla_gate_fwd_{tag}")
            out = _mm(y, weight("mout", j), name=f"mla_out_{tag}")
            saved.append(dict(x=xs, h=h, p=p, cqn=cqn, ckvn=ckvn, Q=Q, K=K, V=V, o=o, lse=lse, y=y, out=out))
        xs = _post_fwd(xs, out, gate[i], post_g[i:i + 1], f"post_fwd_{tag}")

    dx, loss_part = _loss_grad(xs, target, "loss")

    dmod = [None] * depth
    d_pre_g = [None] * depth
    d_post_g = [None] * depth
    d_sgu = [None] * 2
    d_mla = [None] * 2
    kinds = ("sgu_w_in", "sgu_w_out", "mla_w_in", "mla_w_uq", "mla_w_ukv", "mla_w_out")
    halves = dict.fromkeys(kinds)
    in_flight = []

    def send_grads(items, label):
        slices = [dw.reshape(8, -1, dw.shape[-1]) for _, _, dw in items]
        lands = [_own_slot(lax.dynamic_index_in_dim(s, dev, 0, keepdims=False), dev) for s in slices]
        started = _send_start(slices, lands, None, False, f"send_{label}")
        in_flight.append((started, [(kind, layer) for kind, layer, _ in items], label))
        return (started["token"],)

    def collect(count, after):
        for _ in range(count):
            started, keys, label = in_flight.pop(0)
            lands = _send_wait(started, after, f"arrive_{label}")
            for (kind, layer), r in zip(keys, lands):
                halves[kind] = _sum_into_half(r, halves[kind], layer, ci, 2, f"sum_{kind}_{layer}")

    for i in reversed(range(depth)):
        j = i // 2
        tag = f"l{i}"
        sv = saved[i]
        older = len(in_flight)
        dy, dgate, d_post_g[i] = _post_bwd(dx, sv["out"], gate[i], post_g[i:i + 1], f"post_bwd_{tag}")
        if i % 2 == 0:
            dw_out = _mm(sv["y"], dy, ta=True, out_dtype=BF16, name=f"sgu_out_dw_{tag}")
            sent = send_grads([("sgu_w_out", j, dw_out)], f"{tag}_out")
            dyv = _mm(dy, weight("sout", j), tb=True, after=sent, name=f"sgu_out_dx_{tag}")
            duvz, dng, dws, dbs = _sgu_gate_bwd(sv["uvz"], dyv, sgu_norm_g[j:j + 1], sgu_w_s[j], w_sT[j], b_bc[j],
                                                f"sgu_gate_bwd_{tag}")
            sent = ()
            if i > 0:
                dw_in = _mm(sv["h"], duvz, ta=True, out_sharded=True, out_dtype=BF16, tk=4096,
                            name=f"sgu_in_dw_{tag}")
                sent = send_grads([("sgu_w_in", j, dw_in)], f"{tag}_in")
            dh = _mm(duvz, weight("sin", j), tb=True, b_sharded=True, tk=3072, after=sent, name=f"sgu_in_dx_{tag}")
            d_sgu[j] = dict(norm_g=dng, w_s=dws, b_s=dbs[:, :, 0], duvz=duvz)
        else:
            dw_out = _mm(sv["y"], dy, ta=True, out_dtype=BF16, name=f"mla_out_dw_{tag}")
            sent = send_grads([("mla_w_out", j, dw_out)], f"{tag}_out")
            dyv = _mm(dy, weight("mout", j), tb=True, after=sent, name=f"mla_out_dx_{tag}")
            p = sv["p"]
            do = _mla_gate_bwd(dyv, p, f"mla_gate_bwd_{tag}")
            dQ, dK, dV = _attn_bwd(sv["Q"], sv["K"], sv["V"], sv["o"], do, sv["lse"], f"attn_bwd_{tag}")
            dq, dkv, dkr = _mla_unpack(dQ, dK, dV, cos_t, sin_t, f"mla_unpack_{tag}")
            dw_uq = _mm(sv["cqn"], dq, ta=True, out_sharded=True, out_dtype=BF16, name=f"mla_uq_dw_{tag}")
            dcqn = _mm(dq, weight("uq", j), tb=True, b_sharded=True, name=f"mla_uq_dx_{tag}")
            dw_ukv = _mm(sv["ckvn"], dkv, ta=True, out_sharded=True, out_dtype=BF16, name=f"mla_ukv_dw_{tag}")
            dckvn = _mm(dkv, weight("ukv", j), tb=True, b_sharded=True, name=f"mla_ukv_dx_{tag}")
            dp, dqg, dkvg = _mla_mid_bwd(p, dcqn, dckvn, dkr, dyv, sv["o"], q_gain[j], kv_gain[j], f"mla_mid_bwd_{tag}")
            dw_in = _mm(sv["h"], dp, ta=True, out_sharded=True, out_dtype=BF16, name=f"mla_in_dw_{tag}")
            sent = send_grads([("mla_w_in", j, dw_in), ("mla_w_uq", j, dw_uq[:, :Q_RANK]), ("mla_w_ukv", j, dw_ukv)],
                              f"{tag}_in")
            dh = _mm(dp, weight("min", j), tb=True, b_sharded=True, after=sent, name=f"mla_in_dx_{tag}")
            d_mla[j] = dict(qg=dqg[0, :Q_RANK], kvg=dkvg[0])
        dx, dshift, dscale, d_pre_g[i] = _pre_bwd(dh, sv["x"], dx, pre_g[i:i + 1], scale[i], f"pre_bwd_{tag}")
        dmod[i] = jnp.concatenate([dshift, dscale, dgate], axis=1)
        collect(older, dx)
    grad_x = dx.reshape(x.shape)

    parts = [jnp.concatenate(dmod, axis=0), jnp.concatenate(d_pre_g, axis=0), jnp.concatenate(d_post_g, axis=0),
             jnp.stack([d["norm_g"][0] for d in d_sgu]), jnp.stack([d["w_s"] for d in d_sgu]),
             jnp.stack([d["b_s"] for d in d_sgu]), jnp.stack([d["qg"] for d in d_mla]),
             jnp.stack([d["kvg"] for d in d_mla]), loss_part]
    sizes = [int(np.prod(t.shape)) for t in parts]
    packed = _pack_rows(parts)
    small = [packed, parts[0]]
    small_sent = _send_start(small, [_own_slot(t, dev) for t in small], None, True, "send_small_grads")
    dw_in0 = _mm(saved[0]["h"], d_sgu[0]["duvz"], ta=True, out_sharded=True, out_dtype=BF16, tk=4096,
                 after=(small_sent["token"],), name="sgu_in_dw_l0")
    sent = send_grads([("sgu_w_in", 0, dw_in0)], "l0_in")
    packed_all, dmod_all = _send_wait(small_sent, sent[0], "arrive_small_grads")
    total = _sum_slots(packed_all, "sum_small_grads").reshape(-1)
    offs = np.concatenate([[0], np.cumsum(sizes)])
    pieces = [total[int(offs[t]):int(offs[t + 1])].reshape(parts[t].shape) for t in range(len(parts))]
    g_ada_b, g_pre_g, g_post_g, g_norm_g, g_w_s, g_b_s, g_qg_full, g_kvg_full, loss_sum = pieces
    loss = loss_sum.reshape(())
    g_qg = lax.dynamic_slice_in_dim(g_qg_full, chip * qn_w, qn_w, axis=1)
    g_kvg = lax.dynamic_slice_in_dim(g_kvg_full, chip * kvn_w, kvn_w, axis=1)
    dmod_cols = jnp.stack([lax.dynamic_slice_in_dim(dmod_all[:, i], chip * cols, cols, axis=1) for i in range(depth)])
    dmod_cols = jnp.pad(dmod_cols, ((0, 0), (0, LANE - 8), (0, 0)))
    g_ada_w = _ada_grad(jnp.pad(c_all.T, ((0, 0), (0, LANE - 8))), dmod_cols, "ada_grad")

    wnames = ["ada_w", "ada_b", "pre_g", "post_g", "sgu_w_in", "sgu_norm_g", "sgu_w_s", "sgu_b_s", "sgu_w_out",
              "mla_w_in", "mla_q_norm_g", "mla_kv_norm_g", "mla_w_uq", "mla_w_ukv", "mla_w_out"]
    weights = dict(zip(wnames, [ada_w, ada_b, pre_g, post_g, sgu_w_in, sgu_norm_g, sgu_w_s, sgu_b_s, sgu_w_out,
                                mla_w_in, mla_q_norm_g, mla_kv_norm_g, mla_w_uq, mla_w_ukv, mla_w_out]))
    ms = dict(zip(wnames, [m_ada_w, m_ada_b, m_pre_g, m_post_g, m_sgu_w_in, m_sgu_norm_g, m_sgu_w_s, m_sgu_b_s,
                           m_sgu_w_out, m_mla_w_in, m_mla_q_norm_g, m_mla_kv_norm_g, m_mla_w_uq, m_mla_w_ukv,
                           m_mla_w_out]))
    vs = dict(zip(wnames, [v_ada_w, v_ada_b, v_pre_g, v_post_g, v_sgu_w_in, v_sgu_norm_g, v_sgu_w_s, v_sgu_b_s,
                           v_sgu_w_out, v_mla_w_in, v_mla_q_norm_g, v_mla_kv_norm_g, v_mla_w_uq, v_mla_w_ukv,
                           v_mla_w_out]))
    grads = dict(ada_w=g_ada_w, ada_b=g_ada_b, pre_g=g_pre_g, post_g=g_post_g, sgu_norm_g=g_norm_g, sgu_w_s=g_w_s,
                 sgu_b_s=g_b_s, mla_q_norm_g=g_qg, mla_kv_norm_g=g_kvg)
    stepped = {}

    def step(nm):
        grads[nm] = grads[nm].reshape(weights[nm].shape)
        stepped[nm] = _adamw(weights[nm], grads[nm], ms[nm], vs[nm], f"adamw_{nm}")

    for nm in wnames:
        if nm in grads:
            step(nm)
    early = [kind for kind in kinds if kind != "sgu_w_in"]
    collect(len(in_flight) - 1, stepped["ada_w"][0])
    for kind, g in zip(early, _pair_swap([halves[kind] for kind in early], "swap_grads")):
        grads[kind] = g
        step(kind)
    collect(len(in_flight), stepped[early[-1]][0])
    grads["sgu_w_in"], = _pair_swap([halves["sgu_w_in"]], "swap_grads_last")
    step("sgu_w_in")
    return (loss, grad_x, *[grads[nm] for nm in wnames], *[stepped[nm][0] for nm in wnames],
            *[stepped[nm][1] for nm in wnames], *[stepped[nm][2] for nm in wnames])
```

```python
import math

import jax
import jax.numpy as jnp
import numpy as np
from jax import lax
from jax.experimental import pallas as pl
from jax.experimental.pallas import tpu as pltpu

F32 = jnp.float32
BF16 = jnp.bfloat16
MESH = pl.DeviceIdType.MESH

NORM_EPS = 1e-6
CHUNK = 64
SGU_BLOCK = 128
SGU_GROUPS = 16
HEADS = 16
NOPE = 128
ROPE = 64
VDIM = 128
QK_PAD = 256
Q_RANK = 448
Q_RANK_PAD = 512
KV_RANK = 512
ROPE_THETA = 10000.0
ATTN_SCALE = (NOPE + ROPE) ** -0.5
LOG2E = 1.4426950408889634
Q_FOLD = ATTN_SCALE * LOG2E
ATTN_TILE = 512

ADAM_LR = 0.001
ADAM_B1 = 0.9
ADAM_B2 = 0.999
ADAM_EPS = 1e-08
ADAM_WD = 0.01
ADAM_STEP = 10

LANE = 128
VMEM_LIMIT = 48 * 1024 * 1024

NN = (((1,), (0,)), ((), ()))
NT = (((1,), (1,)), ((), ()))
TN = (((0,), (0,)), ((), ()))


def _params(*sem):
    return pltpu.CompilerParams(dimension_semantics=sem, vmem_limit_bytes=VMEM_LIMIT)


def _row_tile(rows, row_bytes, target_bytes=1 << 20):
    if rows * row_bytes <= target_bytes or rows % 16:
        return rows
    best = 16
    t = 16
    while t <= rows:
        if rows % t == 0 and t * row_bytes <= target_bytes:
            best = t
        t += 16
    return best


def _fit(dim, target):
    if dim <= target:
        return dim
    t = (target // LANE) * LANE
    while t > LANE and dim % t:
        t -= LANE
    return t


def _gelu(x):
    return 0.5 * x * (1.0 + lax.erf(x * 0.7071067811865476))


def _gelu_and_grad(x):
    cdf = 0.5 * (1.0 + lax.erf(x * 0.7071067811865476))
    return x * cdf, cdf + x * jnp.exp(-0.5 * x * x) * 0.3989422804014327


def _mm(a, b, *, ta=False, tb=False, b_sharded=False, out_sharded=False, out_dtype=F32,
        tm=1024, tn=1024, tk=2048, after=(), name):
    if ta:
        K, M = a.shape
    else:
        M, K = a.shape
    if b_sharded:
        shards, rows, Cs = b.shape
        b_shape = (rows, shards * Cs)
    else:
        b_shape = b.shape
    if tb:
        N, K2 = b_shape
    else:
        K2, N = b_shape
    assert K == K2, (a.shape, b.shape, ta, tb)
    n_lim = Cs if (b_sharded and not tb) else (N // 4 if out_sharded else N)
    k_lim = Cs if (b_sharded and tb) else K
    tm, tn, tk = _fit(M, tm), _fit(n_lim, tn), _fit(k_lim, tk)
    assert M % tm == 0 and n_lim % tn == 0 and k_lim % tk == 0, (M, N, K, tm, tn, tk)
    nk = K // tk
    nb_n = n_lim // tn
    nb_k = k_lim // tk
    dims = (((0 if ta else 1,), (1 if tb else 0,)), ((), ()))

    def body(a_ref, b_ref, *rest):
        o_ref, *scratch = rest[len(after):]
        prod = lax.dot_general(a_ref[...].astype(BF16), b_ref[...].astype(BF16), dims,
                               preferred_element_type=F32)
        if nk == 1:
            o_ref[...] = prod.astype(out_dtype)
        else:
            acc_ref, = scratch
            k = pl.program_id(2)

            @pl.when(k == 0)
            def _():
                acc_ref[...] = prod

            @pl.when(k > 0)
            def _():
                acc_ref[...] += prod

            @pl.when(k == nk - 1)
            def _():
                o_ref[...] = acc_ref[...].astype(out_dtype)

    a_spec = (pl.BlockSpec((tk, tm), lambda i, j, k: (k, i)) if ta
              else pl.BlockSpec((tm, tk), lambda i, j, k: (i, k)))
    if b_sharded and tb:
        b_spec = pl.BlockSpec((None, tn, tk), lambda i, j, k: (k // nb_k, j, k % nb_k))
    elif b_sharded:
        b_spec = pl.BlockSpec((None, tk, tn), lambda i, j, k: (j // nb_n, k, j % nb_n))
    elif tb:
        b_spec = pl.BlockSpec((tn, tk), lambda i, j, k: (j, k))
    else:
        b_spec = pl.BlockSpec((tk, tn), lambda i, j, k: (k, j))
    if out_sharded:
        out_shape = jax.ShapeDtypeStruct((4, M, N // 4), out_dtype)
        out_spec = pl.BlockSpec((None, tm, tn), lambda i, j, k: (j // nb_n, i, j % nb_n))
    else:
        out_shape = jax.ShapeDtypeStruct((M, N), out_dtype)
        out_spec = pl.BlockSpec((tm, tn), lambda i, j, k: (i, j))
    return pl.pallas_call(
        body, name=name,
        out_shape=out_shape,
        grid=(M // tm, N // tn, nk),
        in_specs=[a_spec, b_spec] + [pl.BlockSpec(memory_space=pl.ANY)] * len(after),
        out_specs=out_spec,
        scratch_shapes=[] if nk == 1 else [pltpu.VMEM((tm, tn), F32)],
        compiler_params=_params("parallel", "parallel", "arbitrary"),
    )(a, b, *after)


def _split_bf16(v):
    hi = v.astype(BF16)
    lo = (v - hi.astype(F32)).astype(BF16)
    return hi, lo


def _dot3(a, b, dims):
    a_hi, a_lo = _split_bf16(a)
    b_hi, b_lo = _split_bf16(b)
    out = lax.dot_general(a_hi, b_hi, dims, preferred_element_type=F32)
    out += lax.dot_general(a_lo, b_hi, dims, preferred_element_type=F32)
    out += lax.dot_general(a_hi, b_lo, dims, preferred_element_type=F32)
    return out


def _ada_mod(c_all, ada_w, ada_b_cols, name):
    L, D, cols = ada_w.shape
    B = c_all.shape[0]
    tn = 512 if cols % 512 == 0 else cols

    def body(c_ref, w_ref, b_ref, o_ref):
        cv = c_ref[...]
        cond = cv * jax.nn.sigmoid(cv)
        o_ref[...] = _dot3(cond, w_ref[...], NN) + b_ref[...]

    return pl.pallas_call(
        body, name=name,
        out_shape=jax.ShapeDtypeStruct((L, B, cols), F32),
        grid=(L, cols // tn),
        in_specs=[pl.BlockSpec((B, D), lambda l, j: (0, 0)),
                  pl.BlockSpec((None, D, tn), lambda l, j: (l, 0, j)),
                  pl.BlockSpec((None, 1, tn), lambda l, j: (l, 0, j))],
        out_specs=pl.BlockSpec((None, B, tn), lambda l, j: (l, 0, j)),
        compiler_params=_params("parallel", "parallel"),
    )(c_all, ada_w, ada_b_cols.reshape(L, 1, cols))


def _ada_grad(c_t, dmod_cols, name, after=()):
    L, B, cols = dmod_cols.shape
    D = c_t.shape[0]
    tn = 512 if cols % 512 == 0 else cols

    def body(c_ref, d_ref, *rest):
        o_ref = rest[-1]
        cv = c_ref[...]
        cond = cv * jax.nn.sigmoid(cv)
        o_ref[...] = _dot3(cond, d_ref[...], NN)

    return pl.pallas_call(
        body, name=name,
        out_shape=jax.ShapeDtypeStruct((L, D, cols), F32),
        grid=(L, cols // tn),
        in_specs=[pl.BlockSpec((D, B), lambda l, j: (0, 0)),
                  pl.BlockSpec((None, B, tn), lambda l, j: (l, 0, j))] + [pl.BlockSpec(memory_space=pl.ANY)] * len(after),
        out_specs=pl.BlockSpec((None, D, tn), lambda l, j: (l, 0, j)),
        compiler_params=_params("parallel", "parallel"),
    )(c_t, dmod_cols, *after)


def _row_spec(ts, width):
    return pl.BlockSpec((ts, width), lambda i: (i, 0))


def _vec_spec(width):
    return pl.BlockSpec((1, width), lambda i: (0, 0))


def _pre_fwd(x, pre_g, scale, shift, name, after=()):
    S, D = x.shape
    ts = min(256, S)

    def body(x_ref, g_ref, sc_ref, sh_ref, *rest):
        h_ref = rest[-1]
        xv = x_ref[...]
        r = lax.rsqrt(jnp.mean(xv * xv, axis=-1, keepdims=True) + NORM_EPS)
        h_ref[...] = ((xv * r * g_ref[...]) * (1.0 + sc_ref[...]) + sh_ref[...]).astype(BF16)

    return pl.pallas_call(
        body, name=name, out_shape=jax.ShapeDtypeStruct((S, D), BF16), grid=(S // ts,),
        in_specs=[_row_spec(ts, D), _vec_spec(D), _vec_spec(D), _vec_spec(D)]
        + [pl.BlockSpec(memory_space=pl.ANY)] * len(after),
        out_specs=_row_spec(ts, D), compiler_params=_params("parallel"),
    )(x, pre_g, scale, shift, *after)


def _pre_bwd(dh, x, dx_res, pre_g, scale, name):
    S, D = x.shape
    ts = min(256, S)

    def body(dh_ref, x_ref, dr_ref, g_ref, sc_ref, dx_ref, dsh_ref, dsc_ref, dg_ref):
        @pl.when(pl.program_id(0) == 0)
        def _():
            dsh_ref[...] = jnp.zeros_like(dsh_ref)
            dsc_ref[...] = jnp.zeros_like(dsc_ref)
            dg_ref[...] = jnp.zeros_like(dg_ref)

        dh = dh_ref[...]
        xv = x_ref[...]
        g = g_ref[...]
        one_sc = 1.0 + sc_ref[...]
        r = lax.rsqrt(jnp.mean(xv * xv, axis=-1, keepdims=True) + NORM_EPS)
        xn = xv * r
        dsh_ref[...] += jnp.sum(dh, axis=0, keepdims=True)
        dsc_ref[...] += jnp.sum(dh * (xn * g), axis=0, keepdims=True)
        dg_ref[...] += jnp.sum(dh * one_sc * xn, axis=0, keepdims=True)
        dxn = dh * one_sc * g
        dx_ref[...] = dr_ref[...] + r * (dxn - xn * jnp.mean(dxn * xn, axis=-1, keepdims=True))

    vec = jax.ShapeDtypeStruct((1, D), F32)
    return pl.pallas_call(
        body, name=name, out_shape=(jax.ShapeDtypeStruct((S, D), F32), vec, vec, vec), grid=(S // ts,),
        in_specs=[_row_spec(ts, D), _row_spec(ts, D), _row_spec(ts, D), _vec_spec(D), _vec_spec(D)],
        out_specs=(_row_spec(ts, D), _vec_spec(D), _vec_spec(D), _vec_spec(D)),
        compiler_params=_params("arbitrary"),
    )(dh, x, dx_res, pre_g, scale)


def _post_fwd(x, y, gate, post_g, name):
    S, D = x.shape
    ts = min(256, S)

    def body(x_ref, y_ref, gt_ref, g_ref, o_ref):
        yv = y_ref[...]
        r = lax.rsqrt(jnp.mean(yv * yv, axis=-1, keepdims=True) + NORM_EPS)
        o_ref[...] = x_ref[...] + gt_ref[...] * (yv * r * g_ref[...])

    return pl.pallas_call(
        body, name=name, out_shape=jax.ShapeDtypeStruct((S, D), F32), grid=(S // ts,),
        in_specs=[_row_spec(ts, D), _row_spec(ts, D), _vec_spec(D), _vec_spec(D)],
        out_specs=_row_spec(ts, D), compiler_params=_params("parallel"),
    )(x, y, gate, post_g)


def _post_bwd(dx, y, gate, post_g, name):
    S, D = y.shape
    ts = min(256, S)

    def body(dx_ref, y_ref, gt_ref, g_ref, dy_ref, dgt_ref, dg_ref):
        @pl.when(pl.program_id(0) == 0)
        def _():
            dgt_ref[...] = jnp.zeros_like(dgt_ref)
            dg_ref[...] = jnp.zeros_like(dg_ref)

        dxv = dx_ref[...]
        yv = y_ref[...]
        g = g_ref[...]
        gt = gt_ref[...]
        r = lax.rsqrt(jnp.mean(yv * yv, axis=-1, keepdims=True) + NORM_EPS)
        yn = yv * r
        dgt_ref[...] += jnp.sum(dxv * (yn * g), axis=0, keepdims=True)
        dg_ref[...] += jnp.sum(dxv * gt * yn, axis=0, keepdims=True)
        dyn = dxv * gt * g
        dy_ref[...] = (r * (dyn - yn * jnp.mean(dyn * yn, axis=-1, keepdims=True))).astype(BF16)

    vec = jax.ShapeDtypeStruct((1, D), F32)
    return pl.pallas_call(
        body, name=name, out_shape=(jax.ShapeDtypeStruct((S, D), BF16), vec, vec), grid=(S // ts,),
        in_specs=[_row_spec(ts, D), _row_spec(ts, D), _vec_spec(D), _vec_spec(D)],
        out_specs=(_row_spec(ts, D), _vec_spec(D), _vec_spec(D)),
        compiler_params=_params("arbitrary"),
    )(dx, y, gate, post_g)


def _loss_grad(xf, target, name):
    S, D = xf.shape
    ts = min(256, S)

    def body(x_ref, t_ref, dx_ref, l_ref):
        @pl.when(pl.program_id(0) == 0)
        def _():
            l_ref[...] = jnp.zeros_like(l_ref)

        e = x_ref[...] - t_ref[...]
        dx_ref[...] = e * (1.0 / D)
        row = jnp.sum(e * e, axis=1, keepdims=True) * (1.0 / D)
        l_ref[...] += 0.5 * jnp.sum(row, axis=0, keepdims=True)

    return pl.pallas_call(
        body, name=name,
        out_shape=(jax.ShapeDtypeStruct((S, D), F32), jax.ShapeDtypeStruct((1, 1), F32)), grid=(S // ts,),
        in_specs=[_row_spec(ts, D), _row_spec(ts, D)],
        out_specs=(_row_spec(ts, D), pl.BlockSpec((1, 1), lambda i: (0, 0))),
        compiler_params=_params("arbitrary"),
    )(xf, target)


def _chunk_mask(transposed=False):
    row = lax.broadcasted_iota(jnp.int32, (SGU_BLOCK, SGU_BLOCK), 0) // CHUNK
    col = lax.broadcasted_iota(jnp.int32, (SGU_BLOCK, SGU_BLOCK), 1) // CHUNK
    return (row <= col) if transposed else (col <= row)


def _sgu_gate_fwd(uvz, norm_g, w_s, b_bc, name):
    S, E3 = uvz.shape
    E = E3 // 3
    T = SGU_BLOCK
    gd = E // SGU_GROUPS

    def body(uvz_ref, ng_ref, ws_ref, bb_ref, y_ref, v_scr):
        gv = _gelu(uvz_ref[:, E:2 * E])
        mu = jnp.mean(gv, axis=-1, keepdims=True)
        xc = gv - mu
        rstd = lax.rsqrt(jnp.mean(xc * xc, axis=-1, keepdims=True) + NORM_EPS)
        v_scr[...] = (xc * rstd * ng_ref[...]).astype(BF16)
        mask = _chunk_mask()
        for g in range(SGU_GROUPS):
            sl = slice(g * gd, (g + 1) * gd)
            wg = jnp.where(mask, ws_ref[g], 0.0).astype(BF16)
            vm = lax.dot_general(wg, v_scr[:, sl], NN, preferred_element_type=F32)
            vm = vm + jnp.tile(bb_ref[g], (1, gd // LANE))
            z = uvz_ref[:, 2 * E + g * gd:2 * E + (g + 1) * gd]
            y_ref[:, sl] = (_gelu(uvz_ref[:, sl]) * vm * (z * jax.nn.sigmoid(z))).astype(BF16)

    return pl.pallas_call(
        body, name=name, out_shape=jax.ShapeDtypeStruct((S, E), BF16), grid=(S // T,),
        in_specs=[_row_spec(T, E3), _vec_spec(E),
                  pl.BlockSpec((SGU_GROUPS, T, T), lambda i: (0, 0, 0)),
                  pl.BlockSpec((SGU_GROUPS, T, LANE), lambda i: (0, 0, 0))],
        out_specs=_row_spec(T, E),
        scratch_shapes=[pltpu.VMEM((T, E), BF16)],
        compiler_params=_params("parallel"),
    )(uvz, norm_g, w_s, b_bc)


def _sgu_gate_bwd(uvz, dyv, norm_g, w_s, w_sT, b_bc, name):
    S, E3 = uvz.shape
    E = E3 // 3
    T = SGU_BLOCK
    gd = E // SGU_GROUPS

    def body(uvz_ref, dyv_ref, ng_ref, ws_ref, wst_ref, bb_ref,
             d_ref, dng_ref, dws_ref, dbs_ref, vhat_scr, dv_scr, vgrad_scr):
        @pl.when(pl.program_id(0) == 0)
        def _():
            dng_ref[...] = jnp.zeros_like(dng_ref)
            dws_ref[...] = jnp.zeros_like(dws_ref)
            dbs_ref[...] = jnp.zeros_like(dbs_ref)

        gv, vgrad_scr[...] = _gelu_and_grad(uvz_ref[:, E:2 * E])
        mu = jnp.mean(gv, axis=-1, keepdims=True)
        xc = gv - mu
        rstd = lax.rsqrt(jnp.mean(xc * xc, axis=-1, keepdims=True) + NORM_EPS)
        vhat_scr[...] = xc * rstd
        mask = _chunk_mask()
        mask_t = _chunk_mask(transposed=True)
        for g in range(SGU_GROUPS):
            sl = slice(g * gd, (g + 1) * gd)
            u_pre = uvz_ref[:, sl]
            z = uvz_ref[:, 2 * E + g * gd:2 * E + (g + 1) * gd]
            dy = dyv_ref[:, sl]
            u, u_grad = _gelu_and_grad(u_pre)
            sig = jax.nn.sigmoid(z)
            sz = z * sig
            vg = (vhat_scr[:, sl] * ng_ref[:, sl]).astype(BF16)
            wg = jnp.where(mask, ws_ref[g], 0.0).astype(BF16)
            vm = lax.dot_general(wg, vg, NN, preferred_element_type=F32)
            vm = vm + jnp.tile(bb_ref[g], (1, gd // LANE))
            dy_u = dy * u
            d_ref[:, sl] = (dy * vm * sz * u_grad).astype(BF16)
            d_ref[:, 2 * E + g * gd:2 * E + (g + 1) * gd] = (
                dy_u * vm * (sig * (1.0 + z * (1.0 - sig)))).astype(BF16)
            dvm = dy_u * sz
            dvm_b = dvm.astype(BF16)
            dws_ref[g] += jnp.where(mask, lax.dot_general(dvm_b, vg, NT, preferred_element_type=F32), 0.0)
            dbs_ref[g] += jnp.broadcast_to(jnp.sum(dvm, axis=1, keepdims=True), (T, LANE))
            wgt = jnp.where(mask_t, wst_ref[g], 0.0).astype(BF16)
            dv_scr[:, sl] = lax.dot_general(wgt, dvm_b, NN, preferred_element_type=F32)
        dv = dv_scr[...]
        vhat = vhat_scr[...]
        dng_ref[...] += jnp.sum(dv * vhat, axis=0, keepdims=True)
        dvh = dv * ng_ref[...]
        dgv = rstd * (dvh - jnp.mean(dvh, axis=-1, keepdims=True)
                      - vhat * jnp.mean(dvh * vhat, axis=-1, keepdims=True))
        d_ref[:, E:2 * E] = (dgv * vgrad_scr[...]).astype(BF16)

    wspec = pl.BlockSpec((SGU_GROUPS, T, T), lambda i: (0, 0, 0))
    bspec = pl.BlockSpec((SGU_GROUPS, T, LANE), lambda i: (0, 0, 0))
    return pl.pallas_call(
        body, name=name,
        out_shape=(jax.ShapeDtypeStruct((S, E3), BF16), jax.ShapeDtypeStruct((1, E), F32),
                   jax.ShapeDtypeStruct((SGU_GROUPS, T, T), F32),
                   jax.ShapeDtypeStruct((SGU_GROUPS, T, LANE), F32)),
        grid=(S // T,),
        in_specs=[_row_spec(T, E3), _row_spec(T, E), _vec_spec(E), wspec, wspec, bspec],
        out_specs=(_row_spec(T, E3), _vec_spec(E), wspec, bspec),
        scratch_shapes=[pltpu.VMEM((T, E), F32), pltpu.VMEM((T, E), F32), pltpu.VMEM((T, E), F32)],
        compiler_params=_params("arbitrary"),
    )(uvz, dyv, norm_g, w_s, w_sT, b_bc)


MLA_WIDTH = HEADS * VDIM
P_LATENT = Q_RANK + KV_RANK + ROPE
P_WIDTH = P_LATENT + MLA_WIDTH


def _swap_halves(v):
    lane = lax.broadcasted_iota(jnp.int32, v.shape, 1)
    return jnp.where(lane % ROPE < ROPE // 2, pltpu.roll(v, LANE - ROPE // 2, 1), pltpu.roll(v, ROPE // 2, 1))


def _low_lanes(rows):
    return lax.broadcasted_iota(jnp.int32, (rows, LANE), 1) < ROPE


def _latent_tiles(ref):
    return [ref[:, t * LANE:(t + 1) * LANE] for t in range(P_LATENT // LANE)]


def _split_latents(tiles, low):
    cq = jnp.concatenate(tiles[0:3] + [jnp.where(low, tiles[3], 0.0)], axis=1)
    rolled = [pltpu.roll(t, ROPE, 1) for t in tiles[3:8]]
    ckv = jnp.concatenate([jnp.where(low, rolled[t], rolled[t + 1]) for t in range(4)], axis=1)
    kr = jnp.where(low, rolled[4], 0.0)
    return cq, ckv, kr


def _mla_mid_fwd(p, qg, kvg, name):
    S, PW = p.shape
    ts = min(256, S)

    def body(p_ref, qg_ref, kvg_ref, cqn_ref, ckvn_ref):
        cq, ckv, _ = _split_latents(_latent_tiles(p_ref), _low_lanes(ts))
        r = lax.rsqrt(jnp.sum(cq * cq, axis=-1, keepdims=True) * (1.0 / Q_RANK) + NORM_EPS)
        cqn_ref[...] = (cq * r * qg_ref[...]).astype(BF16)
        r2 = lax.rsqrt(jnp.mean(ckv * ckv, axis=-1, keepdims=True) + NORM_EPS)
        ckvn_ref[...] = (ckv * r2 * kvg_ref[...]).astype(BF16)

    return pl.pallas_call(
        body, name=name,
        out_shape=(jax.ShapeDtypeStruct((S, Q_RANK_PAD), BF16), jax.ShapeDtypeStruct((S, KV_RANK), BF16)),
        grid=(S // ts,),
        in_specs=[_row_spec(ts, P_LATENT), _vec_spec(Q_RANK_PAD), _vec_spec(KV_RANK)],
        out_specs=(_row_spec(ts, Q_RANK_PAD), _row_spec(ts, KV_RANK)),
        compiler_params=_params("parallel"),
    )(p, qg, kvg)


def _mla_pack(q, kv, p, cos_t, sin_t, name):
    S = q.shape[0]
    ts = min(256, S)
    pair_w = 2 * (NOPE + ROPE)
    head_w = NOPE + VDIM

    def body(q_ref, kv_ref, kr_ref, cos_ref, sin_ref, qo_ref, ko_ref, vo_ref, vt_ref):
        cosv = cos_ref[...]
        sinv = sin_ref[...]
        low = _low_lanes(ts)
        kr = jnp.where(low, pltpu.roll(kr_ref[...], ROPE, 1), 0.0)
        kr = (kr * cosv + _swap_halves(kr) * sinv).astype(BF16)
        for pair in range(HEADS // 2):
            t0, t1, t2 = (q_ref[:, pair * pair_w + t * LANE:pair * pair_w + (t + 1) * LANE] for t in range(3))
            nope_b = jnp.where(low, pltpu.roll(t1, ROPE, 1), pltpu.roll(t2, ROPE, 1))
            ropes = jnp.where(low, t1, t2)
            roped = (ropes * cosv + _swap_halves(ropes) * sinv) * Q_FOLD
            qo_ref[2 * pair, :, 0:NOPE] = (t0 * Q_FOLD).astype(BF16)
            qo_ref[2 * pair, :, NOPE:QK_PAD] = jnp.where(low, roped, 0.0).astype(BF16)
            qo_ref[2 * pair + 1, :, 0:NOPE] = (nope_b * Q_FOLD).astype(BF16)
            qo_ref[2 * pair + 1, :, NOPE:QK_PAD] = jnp.where(low, pltpu.roll(roped, ROPE, 1), 0.0).astype(BF16)
        eye = _identity(VDIM)
        for h in range(HEADS):
            ko_ref[h, :, 0:NOPE] = kv_ref[:, h * head_w:h * head_w + NOPE].astype(BF16)
            ko_ref[h, :, NOPE:QK_PAD] = kr
            vh = kv_ref[:, h * head_w + NOPE:(h + 1) * head_w].astype(BF16)
            vo_ref[h] = vh
            vt_ref[h] = lax.dot_general(eye, vh, NT, preferred_element_type=F32).astype(BF16)

    T = min(ATTN_TILE, S)
    per_tile = T // ts
    return pl.pallas_call(
        body, name=name,
        out_shape=(jax.ShapeDtypeStruct((HEADS, S, QK_PAD), BF16), jax.ShapeDtypeStruct((HEADS, S, QK_PAD), BF16),
                   jax.ShapeDtypeStruct((HEADS, S, VDIM), BF16), jax.ShapeDtypeStruct((HEADS, S // T, VDIM, T), BF16)),
        grid=(S // ts,),
        in_specs=[_row_spec(ts, q.shape[1]), _row_spec(ts, kv.shape[1]),
                  pl.BlockSpec((ts, LANE), lambda i: (i, P_LATENT // LANE - 1)),
                  _row_spec(ts, LANE), _row_spec(ts, LANE)],
        out_specs=(pl.BlockSpec((HEADS, ts, QK_PAD), lambda i: (0, i, 0)),
                   pl.BlockSpec((HEADS, ts, QK_PAD), lambda i: (0, i, 0)),
                   pl.BlockSpec((HEADS, ts, VDIM), lambda i: (0, i, 0)),
                   pl.BlockSpec((HEADS, None, VDIM, ts), lambda i: (0, i // per_tile, 0, i % per_tile))),
        compiler_params=_params("parallel"),
    )(q, kv, p, cos_t, sin_t)


def _mla_unpack(dQ, dK, dV, cos_t, sin_t, name):
    S = dQ.shape[1]
    ts = min(256, S)
    pair_w = 2 * (NOPE + ROPE)
    head_w = NOPE + VDIM

    def body(dq_ref, dk_ref, dv_ref, cos_ref, sin_ref, q_ref, kv_ref, kr_ref):
        cosv = cos_ref[...]
        sinv = sin_ref[...]
        low = _low_lanes(ts)
        for pair in range(HEADS // 2):
            blk = dq_ref[2 * pair, :, NOPE:QK_PAD] + pltpu.roll(dq_ref[2 * pair + 1, :, NOPE:QK_PAD], ROPE, 1)
            ropes = blk * cosv - _swap_halves(blk) * sinv
            nope_b = pltpu.roll(dq_ref[2 * pair + 1, :, 0:NOPE], ROPE, 1)
            q_ref[:, pair * pair_w:pair * pair_w + LANE] = dq_ref[2 * pair, :, 0:NOPE].astype(BF16)
            q_ref[:, pair * pair_w + LANE:pair * pair_w + 2 * LANE] = jnp.where(low, ropes, nope_b).astype(BF16)
            q_ref[:, pair * pair_w + 2 * LANE:(pair + 1) * pair_w] = jnp.where(low, nope_b, ropes).astype(BF16)
        dkr = dk_ref[0, :, NOPE:QK_PAD]
        for h in range(1, HEADS):
            dkr = dkr + dk_ref[h, :, NOPE:QK_PAD]
        kr_ref[...] = dkr * cosv - _swap_halves(dkr) * sinv
        for h in range(HEADS):
            kv_ref[:, h * head_w:h * head_w + NOPE] = dk_ref[h, :, 0:NOPE].astype(BF16)
            kv_ref[:, h * head_w + NOPE:(h + 1) * head_w] = dv_ref[h].astype(BF16)

    return pl.pallas_call(
        body, name=name,
        out_shape=(jax.ShapeDtypeStruct((S, HEADS * (NOPE + ROPE)), BF16),
                   jax.ShapeDtypeStruct((S, HEADS * (NOPE + VDIM)), BF16),
                   jax.ShapeDtypeStruct((S, LANE), F32)),
        grid=(S // ts,),
        in_specs=[pl.BlockSpec((HEADS, ts, QK_PAD), lambda i: (0, i, 0)),
                  pl.BlockSpec((HEADS, ts, QK_PAD), lambda i: (0, i, 0)),
                  pl.BlockSpec((HEADS, ts, VDIM), lambda i: (0, i, 0)),
                  _row_spec(ts, LANE), _row_spec(ts, LANE)],
        out_specs=(_row_spec(ts, HEADS * (NOPE + ROPE)), _row_spec(ts, HEADS * (NOPE + VDIM)),
                   _row_spec(ts, LANE)),
        compiler_params=_params("parallel"),
    )(dQ, dK, dV, cos_t, sin_t)


def _mla_gate_fwd(o, p, name):
    S, W = o.shape
    ts = min(256, S)
    wb = P_LATENT

    def body(o_ref, z_ref, y_ref):
        z = z_ref[...]
        y_ref[...] = (o_ref[...] * (z * jax.nn.sigmoid(z))).astype(BF16)

    return pl.pallas_call(
        body, name=name, out_shape=jax.ShapeDtypeStruct((S, W), BF16), grid=(S // ts, W // wb),
        in_specs=[pl.BlockSpec((ts, wb), lambda i, j: (i, j)), pl.BlockSpec((ts, wb), lambda i, j: (i, j + 1))],
        out_specs=pl.BlockSpec((ts, wb), lambda i, j: (i, j)), compiler_params=_params("parallel", "parallel"),
    )(o, p)


def _mla_gate_bwd(dyv, p, name):
    S, W = dyv.shape
    ts = min(256, S)
    wb = P_LATENT

    def body(d_ref, z_ref, do_ref):
        z = z_ref[...]
        do_ref[...] = d_ref[...] * (z * jax.nn.sigmoid(z))

    return pl.pallas_call(
        body, name=name, out_shape=jax.ShapeDtypeStruct((S, W), F32), grid=(S // ts, W // wb),
        in_specs=[pl.BlockSpec((ts, wb), lambda i, j: (i, j)), pl.BlockSpec((ts, wb), lambda i, j: (i, j + 1))],
        out_specs=pl.BlockSpec((ts, wb), lambda i, j: (i, j)), compiler_params=_params("parallel", "parallel"),
    )(dyv, p)


def _mla_mid_bwd(p, dcqn, dckvn, dkr, dyv, o, qg, kvg, name):
    S, PW = p.shape
    W = o.shape[1]
    ts = min(256, S)
    nt = Q_RANK_PAD // LANE

    def rms_bwd(xv, dy, g, count):
        r = lax.rsqrt(jnp.sum(xv * xv, axis=-1, keepdims=True) * (1.0 / count) + NORM_EPS)
        xn = xv * r
        dg = jnp.sum(dy * xn, axis=0, keepdims=True)
        dxn = dy * g
        dx = r * (dxn - xn * (jnp.sum(dxn * xn, axis=-1, keepdims=True) * (1.0 / count)))
        return dx, dg

    def body(p_ref, dcq_ref, dckv_ref, dkr_ref, dyv_ref, o_ref, qg_ref, kvg_ref, dp_ref, dqg_ref, dkvg_ref):
        @pl.when(pl.program_id(0) == 0)
        def _():
            dqg_ref[...] = jnp.zeros_like(dqg_ref)
            dkvg_ref[...] = jnp.zeros_like(dkvg_ref)

        low = _low_lanes(ts)
        cq, ckv, _ = _split_latents(_latent_tiles(p_ref), low)
        dcq, dg = rms_bwd(cq, dcq_ref[...], qg_ref[...], Q_RANK)
        dqg_ref[...] += dg
        dckv, dg = rms_bwd(ckv, dckv_ref[...], kvg_ref[...], KV_RANK)
        dkvg_ref[...] += dg
        moved = [pltpu.roll(dckv[:, t * LANE:(t + 1) * LANE], ROPE, 1) for t in range(nt)]
        moved.append(pltpu.roll(dkr_ref[...], ROPE, 1))
        for t in range(nt - 1):
            dp_ref[:, t * LANE:(t + 1) * LANE] = dcq[:, t * LANE:(t + 1) * LANE].astype(BF16)
        dp_ref[:, (nt - 1) * LANE:nt * LANE] = jnp.where(low, dcq[:, (nt - 1) * LANE:nt * LANE], moved[0]).astype(BF16)
        for t in range(nt):
            dp_ref[:, (nt + t) * LANE:(nt + t + 1) * LANE] = jnp.where(low, moved[t], moved[t + 1]).astype(BF16)
        z = p_ref[:, P_LATENT:PW]
        sig = jax.nn.sigmoid(z)
        dp_ref[:, P_LATENT:PW] = (dyv_ref[...] * o_ref[...] * (sig * (1.0 + z * (1.0 - sig)))).astype(BF16)

    return pl.pallas_call(
        body, name=name,
        out_shape=(jax.ShapeDtypeStruct((S, PW), BF16), jax.ShapeDtypeStruct((1, Q_RANK_PAD), F32),
                   jax.ShapeDtypeStruct((1, KV_RANK), F32)),
        grid=(S // ts,),
        in_specs=[_row_spec(ts, PW), _row_spec(ts, Q_RANK_PAD), _row_spec(ts, KV_RANK), _row_spec(ts, LANE),
                  _row_spec(ts, W), _row_spec(ts, W), _vec_spec(Q_RANK_PAD), _vec_spec(KV_RANK)],
        out_specs=(_row_spec(ts, PW), _vec_spec(Q_RANK_PAD), _vec_spec(KV_RANK)),
        compiler_params=_params("arbitrary"),
    )(p, dcqn, dckvn, dkr, dyv, o, qg, kvg)


def _tile_mask(T):
    row = lax.broadcasted_iota(jnp.int32, (T, T), 0) // CHUNK
    col = lax.broadcasted_iota(jnp.int32, (T, T), 1) // CHUNK
    return col <= row


def _attn_bwd(Q, K, V, o, do, lse, name):
    H, S, _ = Q.shape
    T = min(ATTN_TILE, S)
    nq = S // T

    def body(q_ref, k_ref, v_ref, o_ref, do_ref, lse_ref, dq_ref, dk_ref, dv_ref, dk_scr, dv_scr, s_scr, dp_scr):
        ki = pl.program_id(1)

        @pl.when(ki == 0)
        def _():
            dq_ref[...] = jnp.zeros_like(dq_ref)

        dk_scr[...] = jnp.zeros_like(dk_scr)
        dv_scr[...] = jnp.zeros_like(dv_scr)
        k = k_ref[...]
        v = v_ref[...]

        def scores(i):
            rows = pl.ds(pl.multiple_of(i * T, T), T)
            s = lax.dot_general(q_ref[rows, :], k, NT, preferred_element_type=F32)
            dp = lax.dot_general(do_ref[rows, :].astype(BF16), v, NT, preferred_element_type=F32)
            return s, dp

        def grads(i, s, dp, masked):
            rows = pl.ds(pl.multiple_of(i * T, T), T)
            do_f = do_ref[rows, :]
            delta = jnp.sum(do_f * o_ref[rows, :], axis=1, keepdims=True)
            pr = jnp.exp2(s - lse_ref[rows, 0:1])
            if masked:
                pr = jnp.where(_tile_mask(T), pr, 0.0)
            dv_scr[...] += lax.dot_general(pr.astype(BF16), do_f.astype(BF16), TN, preferred_element_type=F32)
            ds = (pr * (dp - delta)).astype(BF16)
            dk_scr[...] += lax.dot_general(ds, q_ref[rows, :], TN, preferred_element_type=F32)
            dq_ref[rows, :] += lax.dot_general(ds, k, NN, preferred_element_type=F32) * ATTN_SCALE

        s_scr[...], dp_scr[...] = scores(ki)

        @pl.when(ki + 1 == nq)
        def _():
            grads(ki, s_scr[...], dp_scr[...], True)

        @pl.when(ki + 1 < nq)
        def _():
            def step(i, masked):
                nxt_s, nxt_dp = scores(i + 1)
                grads(i, s_scr[...], dp_scr[...], masked)
                s_scr[...] = nxt_s
                dp_scr[...] = nxt_dp

            def loop_step(i, carry):
                step(i, False)
                return carry

            step(ki, True)
            lax.fori_loop(ki + 1, nq - 1, loop_step, 0)
            grads(nq - 1, s_scr[...], dp_scr[...], False)

        dk_ref[...] = dk_scr[...] * (1.0 / LOG2E)
        dv_ref[...] = dv_scr[...]

    return pl.pallas_call(
        body, name=name,
        out_shape=(jax.ShapeDtypeStruct((H, S, QK_PAD), F32), jax.ShapeDtypeStruct((H, S, QK_PAD), F32),
                   jax.ShapeDtypeStruct((H, S, VDIM), F32)),
        grid=(H, nq),
        in_specs=[pl.BlockSpec((None, S, QK_PAD), lambda h, j: (h, 0, 0)),
                  pl.BlockSpec((None, T, QK_PAD), lambda h, j: (h, j, 0)),
                  pl.BlockSpec((None, T, VDIM), lambda h, j: (h, j, 0)),
                  pl.BlockSpec((S, VDIM), lambda h, j: (0, h)),
                  pl.BlockSpec((S, VDIM), lambda h, j: (0, h)),
                  pl.BlockSpec((None, S, LANE), lambda h, j: (h, 0, 0))],
        out_specs=(pl.BlockSpec((None, S, QK_PAD), lambda h, j: (h, 0, 0)),
                   pl.BlockSpec((None, T, QK_PAD), lambda h, j: (h, j, 0)),
                   pl.BlockSpec((None, T, VDIM), lambda h, j: (h, j, 0))),
        scratch_shapes=[pltpu.VMEM((T, QK_PAD), F32), pltpu.VMEM((T, VDIM), F32),
                        pltpu.VMEM((T, T), F32), pltpu.VMEM((T, T), F32)],
        compiler_params=_params("parallel", "arbitrary"),
    )(Q, K, V, o, do, lse)


def _identity(n):
    return (lax.broadcasted_iota(jnp.int32, (n, n), 0) == lax.broadcasted_iota(jnp.int32, (n, n), 1)).astype(BF16)


def _key_le_query(rows, cols, col0):
    key = lax.broadcasted_iota(jnp.int32, (rows, cols), 0) // CHUNK
    query = (lax.broadcasted_iota(jnp.int32, (rows, cols), 1) + col0) // CHUNK
    return key <= query


def _attn_fwd_t(Q, K, VT, name):
    H, nT, _, T = VT.shape
    S = nT * T
    n_part = 2 if T % 256 == 0 else 1
    Tq = T // n_part

    def body(q_ref, k_ref, vt_ref, o_ref, lse_ref, m_scr, l_scr, acc_scr, s_scr):
        qi = pl.program_id(1)
        m_scr[...] = jnp.full_like(m_scr, -jnp.inf)
        l_scr[...] = jnp.zeros_like(l_scr)
        acc_scr[...] = jnp.zeros_like(acc_scr)

        def scores(j):
            kt = k_ref[pl.ds(pl.multiple_of(j * T, T), T), :]
            return lax.dot_general(kt, q_ref[...], NT, preferred_element_type=F32)

        def softmax_step(j, masked):
            vt = vt_ref[j]
            for part in range(n_part):
                sub = slice(part * Tq, (part + 1) * Tq)
                st = s_scr[:, sub]
                if masked:
                    st = jnp.where(_key_le_query(T, Tq, part * Tq), st, -1e30)
                m_prev = m_scr[:, sub]
                m_new = jnp.maximum(m_prev, jnp.max(st, axis=0, keepdims=True))
                pt = jnp.exp2(st - m_new)
                alpha = jnp.exp2(m_prev - m_new)
                l_scr[:, sub] = alpha * l_scr[:, sub] + jnp.sum(pt, axis=0, keepdims=True)
                acc_scr[:, sub] = alpha * acc_scr[:, sub] + lax.dot_general(
                    vt, pt.astype(BF16), NN, preferred_element_type=F32)
                m_scr[:, sub] = m_new

        s_scr[...] = scores(0)

        def step(j, carry):
            nxt = scores(j + 1)
            softmax_step(j, False)
            s_scr[...] = nxt
            return carry

        lax.fori_loop(0, qi, step, 0)
        softmax_step(qi, True)
        l = l_scr[...]
        o_ref[...] = jnp.transpose(acc_scr[...] / l)
        lse_ref[...] = jnp.transpose(jnp.broadcast_to(m_scr[...] + jnp.log2(l), (LANE, T)))

    return pl.pallas_call(
        body, name=name,
        out_shape=(jax.ShapeDtypeStruct((S, H * VDIM), F32), jax.ShapeDtypeStruct((H, S, LANE), F32)),
        grid=(H, nT),
        in_specs=[pl.BlockSpec((None, T, QK_PAD), lambda h, i: (h, i, 0)),
                  pl.BlockSpec((None, S, QK_PAD), lambda h, i: (h, 0, 0)),
                  pl.BlockSpec((None, nT, VDIM, T), lambda h, i: (h, 0, 0, 0))],
        out_specs=(pl.BlockSpec((T, VDIM), lambda h, i: (i, h)),
                   pl.BlockSpec((None, T, LANE), lambda h, i: (h, i, 0))),
        scratch_shapes=[pltpu.VMEM((1, T), F32), pltpu.VMEM((1, T), F32), pltpu.VMEM((VDIM, T), F32),
                        pltpu.VMEM((T, T), F32)],
        compiler_params=_params("parallel", "arbitrary"),
    )(Q, K, VT)


def _adamw(w, g, m, v, name):
    shape = w.shape
    C = shape[-1]
    R = math.prod(shape[:-1])
    flat = [t.reshape(R, C) for t in (w, g, m, v)]
    tr = _row_tile(R, C * 4)

    def body(w_ref, g_ref, m_ref, v_ref, d_ref, nm_ref, nv_ref):
        gv = g_ref[...]
        m_new = ADAM_B1 * m_ref[...] + (1.0 - ADAM_B1) * gv
        v_new = ADAM_B2 * v_ref[...] + (1.0 - ADAM_B2) * jnp.square(gv)
        m_hat = m_new / (1.0 - ADAM_B1 ** ADAM_STEP)
        v_hat = v_new / (1.0 - ADAM_B2 ** ADAM_STEP)
        d_ref[...] = -ADAM_LR * (m_hat / (jnp.sqrt(v_hat) + ADAM_EPS) + ADAM_WD * w_ref[...])
        nm_ref[...] = m_new
        nv_ref[...] = v_new

    spec = pl.BlockSpec((tr, C), lambda i: (i, 0))
    out = jax.ShapeDtypeStruct((R, C), F32)
    d, nm, nv = pl.pallas_call(
        body, name=name, out_shape=(out, out, out), grid=(R // tr,),
        in_specs=[spec] * 4, out_specs=(spec, spec, spec), compiler_params=_params("parallel"),
    )(*flat)
    return d.reshape(shape), nm.reshape(shape), nv.reshape(shape)


def _sum_into_half(r, buf, layer, ci, n_layers, name):
    n, M, N = r.shape
    tr = _row_tile(M, N * 4 * n, 4 << 20)

    def body(c_ref, r_ref, *rest):
        o_ref = rest[-1]
        acc = r_ref[0].astype(F32)
        for s in range(1, n):
            acc = acc + r_ref[s].astype(F32)
        o_ref[...] = acc

    in_specs = [pl.BlockSpec((n, tr, N), lambda i, c: (0, i, 0))]
    operands = [ci.reshape(1), r]
    aliases = {}
    if buf is not None:
        in_specs.append(ANY)
        operands.append(buf)
        aliases = {2: 0}
    return pl.pallas_call(
        body, name=name, out_shape=jax.ShapeDtypeStruct((n_layers, 2, M, N), F32),
        grid_spec=pltpu.PrefetchScalarGridSpec(
            num_scalar_prefetch=1, grid=(M // tr,), in_specs=in_specs,
            out_specs=pl.BlockSpec((None, None, tr, N), lambda i, c: (layer, c[0], i, 0))),
        input_output_aliases=aliases, compiler_params=_params("parallel"),
    )(*operands)


def _sum_slots(r, name):
    n, M, N = r.shape
    tr = _row_tile(M, N * 4 * n, 4 << 20)

    def body(r_ref, o_ref):
        acc = r_ref[0].astype(F32)
        for s in range(1, n):
            acc = acc + r_ref[s].astype(F32)
        o_ref[...] = acc

    return pl.pallas_call(
        body, name=name, out_shape=jax.ShapeDtypeStruct((M, N), F32), grid=(M // tr,),
        in_specs=[pl.BlockSpec((n, tr, N), lambda i: (0, i, 0))],
        out_specs=pl.BlockSpec((tr, N), lambda i: (i, 0)), compiler_params=_params("parallel"),
    )(r)


ANY = pl.BlockSpec(memory_space=pl.ANY)
DMA_CHUNK_BYTES = 1 << 20
DMA_MAX_CHUNKS = 16
PEER_ORDER = (1, 4, 5, 2, 3, 6, 7)


def _position():
    return lax.axis_index("x"), lax.axis_index("y"), lax.axis_index("c")


def _row_chunks(shape, dtype):
    rows, cols = shape
    n = max(1, min(DMA_MAX_CHUNKS, rows * cols * jnp.dtype(dtype).itemsize // DMA_CHUNK_BYTES))
    while n > 1 and (rows % n or (rows // n) % 16):
        n -= 1
    step = rows // n
    return [pl.ds(q * step, step) for q in range(n)]


def _all_gather8(xs, name):
    n = len(xs)

    def body(*refs):
        x_refs, o_refs = refs[:n], refs[n:2 * n]
        send_sems, recv_sems, local_sems = refs[2 * n:]
        x, y, c = _position()
        me, sibling = (x, y, c), (x, y, 1 - c)
        chips = [(1 - x, y), (x, 1 - y), (1 - x, 1 - y)]

        def slot(a, dev, rows):
            return o_refs[a].at[4 * dev[0] + 2 * dev[1] + dev[2], rows]

        def copy(a, k, block, to, rows, from_input=False):
            return pltpu.make_async_remote_copy(
                src_ref=x_refs[a].at[rows] if from_input else slot(a, block, rows), dst_ref=slot(a, block, rows),
                send_sem=send_sems.at[a, k], recv_sem=recv_sems.at[a, k],
                device_id=to, device_id_type=MESH)

        def mine(a, rows):
            return pltpu.make_async_copy(x_refs[a].at[rows], slot(a, me, rows), local_sems.at[a])

        chunks = [_row_chunks(t.shape, t.dtype) for t in xs]
        whole = [pl.ds(0, t.shape[0]) for t in xs]
        for a in range(n):
            for rows in chunks[a]:
                mine(a, rows).start()
        sent = []
        for a in range(n):
            for k, to in enumerate([sibling] + [(*chip, c) for chip in chips]):
                for rows in chunks[a]:
                    copy(a, k, me, to, rows, from_input=True).start()
                sent.append(copy(a, k, me, to, whole[a], from_input=True))
        for a in range(n):
            for j, chip in enumerate(chips):
                copy(a, 1 + j, (*chip, c), me, whole[a]).wait_recv()
                for rows in chunks[a]:
                    copy(a, 4 + j, (*chip, c), sibling, rows).start()
                sent.append(copy(a, 4 + j, (*chip, c), sibling, whole[a]))
        for a in range(n):
            copy(a, 0, sibling, me, whole[a]).wait_recv()
            for j, chip in enumerate(chips):
                copy(a, 4 + j, (*chip, 1 - c), me, whole[a]).wait_recv()
        for cp in sent:
            cp.wait_send()
        for a in range(n):
            mine(a, whole[a]).wait()

    return pl.pallas_call(
        body, name=name,
        out_shape=[jax.ShapeDtypeStruct((8,) + t.shape, t.dtype) for t in xs],
        in_specs=[ANY] * n, out_specs=[ANY] * n,
        scratch_shapes=[pltpu.SemaphoreType.DMA((n, 7)), pltpu.SemaphoreType.DMA((n, 7)),
                        pltpu.SemaphoreType.DMA((n,))],
    )(*xs)


HBM = pl.BlockSpec(memory_space=pltpu.HBM)
SEM = pl.BlockSpec(memory_space=pltpu.SEMAPHORE)
EFFECT = pltpu.SideEffectType.DATAFLOW_SIDE_EFFECTING


def _peer(m, x, y, c):
    return ((1 - x) if m & 4 else x, (1 - y) if m & 2 else y, (1 - c) if m & 1 else c)


def _send_copies(src_refs, land_refs, send_sems, recv_sems, broadcast):
    x, y, c = _position()
    my = 4 * x + 2 * y + c
    out = []
    for a in range(len(src_refs)):
        for m in PEER_ORDER:
            px, py, pc = _peer(m, x, y, c)
            src = src_refs[a] if broadcast else src_refs[a].at[4 * px + 2 * py + pc]
            out.append(pltpu.make_async_remote_copy(
                src_ref=src, dst_ref=land_refs[a].at[my], send_sem=send_sems[a], recv_sem=recv_sems[a],
                device_id=(px, py, pc), device_id_type=MESH))
    return out


def _send_drain(land_refs, send_sems, recv_sems):
    x, y, c = _position()
    for a in range(len(land_refs)):
        seven = land_refs[a].at[pl.ds(0, 7)]
        both = pltpu.make_async_remote_copy(
            src_ref=seven, dst_ref=seven, send_sem=send_sems[a], recv_sem=recv_sems[a],
            device_id=(x, y, c), device_id_type=MESH)
        both.wait_send()
        both.wait_recv()


def _send_start(srcs, lands, after, broadcast, name):
    n = len(srcs)
    extra = [] if after is None else [after]

    def body(*refs):
        src_refs, land_refs = refs[:n], refs[n:2 * n]
        outs = refs[2 * n + len(extra):]
        send_sems, recv_sems = outs[:n], outs[n:2 * n]
        token = refs[-1]
        for cp in _send_copies(src_refs, land_refs, send_sems, recv_sems, broadcast):
            cp.start()
        token[...] = jnp.zeros_like(token)

    hbm = [pltpu.with_memory_space_constraint(t, pltpu.HBM) for t in list(srcs) + list(lands)]
    res = pl.pallas_call(
        body, name=name,
        out_shape=(*[pltpu.SemaphoreType.DMA(())] * (2 * n),
                   *[pltpu.HBM(t.shape, t.dtype) for t in hbm], jax.ShapeDtypeStruct((8, LANE), F32)),
        in_specs=[HBM] * (2 * n) + [ANY] * len(extra),
        out_specs=(*[SEM] * (2 * n), *[HBM] * (2 * n), pl.BlockSpec(memory_space=pltpu.VMEM)),
        input_output_aliases={i: 2 * n + i for i in range(2 * n)},
        compiler_params=pltpu.CompilerParams(has_side_effects=EFFECT),
    )(*hbm, *extra)
    return dict(sems=res[:2 * n], srcs=res[2 * n:3 * n], lands=res[3 * n:4 * n], token=res[-1], broadcast=broadcast)


def _send_wait(started, after, name):
    n = len(started["srcs"])

    def body(*refs):
        land_refs = refs[n:2 * n]
        send_sems, recv_sems = refs[2 * n:3 * n], refs[3 * n:4 * n]
        _send_drain(land_refs, send_sems, recv_sems)

    operands = list(started["srcs"]) + list(started["lands"])
    res = pl.pallas_call(
        body, name=name,
        out_shape=[pltpu.HBM(t.shape, t.dtype) for t in operands],
        in_specs=[HBM] * (2 * n) + [SEM] * (2 * n) + [ANY],
        out_specs=[HBM] * (2 * n),
        input_output_aliases={i: i for i in range(2 * n)},
        compiler_params=pltpu.CompilerParams(has_side_effects=EFFECT),
    )(*operands, *started["sems"], after)
    return res[n:]


def _own_slot(block, dev):
    zone = lax.empty((8,) + block.shape, block.dtype)
    return lax.dynamic_update_slice(zone, block[None], (dev, 0, 0))


def _pair_swap(bufs, name, layers=None):
    n = len(bufs)
    layers = layers or [range(t.shape[0]) for t in bufs]
    pieces = [(a, l) for a in range(len(bufs)) for l in layers[a]]

    def body(*refs):
        b_refs = refs[n:2 * n]
        send_sems, recv_sems = refs[2 * n:]
        x, y, c = _position()

        def copy(k, rows):
            a, l = pieces[k]
            half = b_refs[a].at[l, c, rows]
            return pltpu.make_async_remote_copy(
                src_ref=half, dst_ref=half, send_sem=send_sems.at[k], recv_sem=recv_sems.at[k],
                device_id=(x, y, 1 - c), device_id_type=MESH)

        chunks = [_row_chunks(bufs[a].shape[2:], bufs[a].dtype) for a, _ in pieces]
        whole = [pl.ds(0, bufs[a].shape[2]) for a, _ in pieces]
        for k in range(len(pieces)):
            for rows in chunks[k]:
                copy(k, rows).start()
        for k in range(len(pieces)):
            copy(k, whole[k]).wait_recv()
        for k in range(len(pieces)):
            copy(k, whole[k]).wait_send()

    return pl.pallas_call(
        body, name=name,
        out_shape=[jax.ShapeDtypeStruct(t.shape, t.dtype) for t in bufs],
        in_specs=[ANY] * n, out_specs=[ANY] * n,
        input_output_aliases={a: a for a in range(n)},
        scratch_shapes=[pltpu.SemaphoreType.DMA((len(pieces),)), pltpu.SemaphoreType.DMA((len(pieces),))],
    )(*bufs)


def _pack_rows(parts):
    flat = jnp.concatenate([t.reshape(-1).astype(F32) for t in parts])
    pad = (-flat.shape[0]) % (256 * LANE)
    return jnp.pad(flat, (0, pad)).reshape(-1, LANE)


def _my_half(w2d, ci):
    half = w2d.shape[0] // 2
    return lax.dynamic_slice_in_dim(w2d, ci * half, half, axis=0).astype(BF16)


def _col_view(g):
    _, half, Cs = g.shape
    return g.reshape(4, 2 * half, Cs)


def _row_view(g):
    _, half, C = g.shape
    return g.reshape(8 * half, C)


def _rope_tables(S):
    pos = jnp.arange(S, dtype=F32)
    inv_freq = ROPE_THETA ** (-jnp.arange(0, ROPE, 2, dtype=F32) / ROPE)
    ang = pos[:, None] * inv_freq[None, :]
    cos, sin = jnp.cos(ang), jnp.sin(ang)
    cos_t = jnp.concatenate([cos, cos, cos, cos], axis=1)
    sin_t = jnp.concatenate([-sin, sin, -sin, sin], axis=1)
    return cos_t, sin_t


def kernel(x, c, ada_w, ada_b, pre_g, post_g, sgu_w_in, sgu_norm_g, sgu_w_s, sgu_b_s, sgu_w_out, mla_w_in, mla_q_norm_g, mla_kv_norm_g, mla_w_uq, mla_w_ukv, mla_w_out, loss_target, m_ada_w, m_ada_b, m_pre_g, m_post_g, m_sgu_w_in, m_sgu_norm_g, m_sgu_w_s, m_sgu_b_s, m_sgu_w_out, m_mla_w_in, m_mla_q_norm_g, m_mla_kv_norm_g, m_mla_w_uq, m_mla_w_ukv, m_mla_w_out, v_ada_w, v_ada_b, v_pre_g, v_post_g, v_sgu_w_in, v_sgu_norm_g, v_sgu_w_s, v_sgu_b_s, v_sgu_w_out, v_mla_w_in, v_mla_q_norm_g, v_mla_kv_norm_g, v_mla_w_uq, v_mla_w_ukv, v_mla_w_out):
    S, D = x.shape[1], x.shape[2]
    depth = ada_w.shape[0]
    E = sgu_w_out.shape[1] * 4
    xi, yi, ci = _position()
    chip = 2 * xi + yi
    dev = 4 * xi + 2 * yi + ci
    x0 = x.reshape(S, D)
    target = loss_target.reshape(S, D)

    small = _pack_rows([c, mla_q_norm_g, mla_kv_norm_g])
    mixer_w = dict(sin=sgu_w_in, sout=sgu_w_out, min=mla_w_in, uq=mla_w_uq, ukv=mla_w_ukv, mout=mla_w_out)
    small_g, first_g = _all_gather8([small, _my_half(sgu_w_in[0], ci)], "gather_first")
    small_all = small_g.reshape(8, -1)
    gathered_w = {("sin", 0): first_g}
    qn_w, kvn_w = mla_q_norm_g.shape[1], mla_kv_norm_g.shape[1]
    c_all = small_all[:, :D]
    qn_all = small_all[0::2, D:D + 2 * qn_w].reshape(4, 2, qn_w)
    kvn_all = small_all[0::2, D + 2 * qn_w:D + 2 * qn_w + 2 * kvn_w].reshape(4, 2, kvn_w)
    q_gain = jnp.pad(jnp.transpose(qn_all, (1, 0, 2)).reshape(2, 1, Q_RANK), ((0, 0), (0, 0), (0, Q_RANK_PAD - Q_RANK)))
    kv_gain = jnp.transpose(kvn_all, (1, 0, 2)).reshape(2, 1, KV_RANK)

    views = {}

    def weight(t, j):
        if (t, j) not in views:
            g = gathered_w[(t, j)]
            v = _row_view(g) if t in ("sout", "mout") else _col_view(g)
            if t == "uq":
                v = jnp.pad(v, ((0, 0), (0, Q_RANK_PAD - Q_RANK), (0, 0)))
            views[(t, j)] = v
        return views[(t, j)]

    cols = ada_w.shape[2]
    ada_b_cols = lax.dynamic_slice_in_dim(ada_b, chip * cols, cols, axis=1)
    c_pad = jnp.pad(c_all, ((0, 8), (0, 0)))
    mod_cols = _ada_mod(c_pad, ada_w, ada_b_cols, "ada_mod")[:, :8]
    mod_g, = _all_gather8([mod_cols.reshape(depth * 8, cols)], "gather_mod")
    mod_all = jnp.transpose(mod_g[0::2].reshape(4, depth, 8, cols), (1, 2, 0, 3)).reshape(depth, 8, 4 * cols)

    groups = [("sout0", [("sout", 0)]), ("mla0", [(t, 0) for t in ("min", "uq", "ukv", "mout")]),
              ("sgu1", [("sin", 1), ("sout", 1)]), ("mla1", [(t, 1) for t in ("min", "uq", "ukv", "mout")])]
    sends = {}
    behind = mod_g
    for gname, items in groups:
        blocks = [_my_half(mixer_w[t][j], ci) for t, j in items]
        sends[gname] = _send_start(blocks, [_own_slot(b, dev) for b in blocks], behind, True, f"send_{gname}")
        behind = sends[gname]["token"]

    def arrive(gname, after):
        lands = _send_wait(sends[gname], after, f"arrive_{gname}")
        gathered_w.update(zip(dict(groups)[gname], lands))
    mod = lax.dynamic_index_in_dim(mod_all, dev, 1, keepdims=False)
    shift = [mod[i:i + 1, :D] for i in range(depth)]
    scale = [mod[i:i + 1, D:2 * D] for i in range(depth)]
    gate = [mod[i:i + 1, 2 * D:] for i in range(depth)]

    cos_t, sin_t = _rope_tables(S)
    b_bc = jnp.broadcast_to(sgu_b_s[:, :, :, None], sgu_b_s.shape + (LANE,))
    w_sT = jnp.swapaxes(sgu_w_s, 2, 3)

    saved = []
    xs = x0
    for i in range(depth):
        j = i // 2
        tag = f"l{i}"
        h = _pre_fwd(xs, pre_g[i:i + 1], scale[i], shift[i], f"pre_fwd_{tag}", after=(behind,) if i == 0 else ())
        if i % 2 == 0:
            if j > 0:
                arrive(f"sgu{j}", h)
            uvz = _mm(h, weight("sin", j), b_sharded=True, name=f"sgu_in_{tag}")
            y = _sgu_gate_fwd(uvz, sgu_norm_g[j:j + 1], sgu_w_s[j], b_bc[j], f"sgu_gate_fwd_{tag}")
            if j == 0:
                arrive("sout0", y)
            out = _mm(y, weight("sout", j), name=f"sgu_out_{tag}")
            saved.append(dict(x=xs, h=h, uvz=uvz, y=y, out=out))
        else:
            arrive(f"mla{j}", h)
            p = _mm(h, weight("min", j), b_sharded=True, name=f"mla_in_{tag}")
            cqn, ckvn = _mla_mid_fwd(p, q_gain[j], kv_gain[j], f"mla_mid_fwd_{tag}")
            q = _mm(cqn, weight("uq", j), b_sharded=True, name=f"mla_uq_{tag}")
            kv = _mm(ckvn, weight("ukv", j), b_sharded=True, name=f"mla_ukv_{tag}")
            Q, K, V, VT = _mla_pack(q, kv, p, cos_t, sin_t, f"mla_pack_{tag}")
            o, lse = _attn_fwd_t(Q, K, VT, f"attn_fwd_{tag}")
            y = _mla_gate_fwd(o, p, f"mla_gate_fwd_{tag}")
            out = _mm(y, weight("mout", j), name=f"mla_out_{tag}")
            saved.append(dict(x=xs, h=h, p=p, cqn=cqn, ckvn=ckvn, Q=Q, K=K, V=V, o=o, lse=lse, y=y, out=out))
        xs = _post_fwd(xs, out, gate[i], post_g[i:i + 1], f"post_fwd_{tag}")

    dx, loss_part = _loss_grad(xs, target, "loss")

    dmod = [None] * depth
    d_pre_g = [None] * depth
    d_post_g = [None] * depth
    d_sgu = [None] * 2
    d_mla = [None] * 2
    kinds = ("sgu_w_in", "sgu_w_out", "mla_w_in", "mla_w_uq", "mla_w_ukv", "mla_w_out")
    halves = dict.fromkeys(kinds)
    in_flight = []

    def send_grads(items, label):
        slices = [dw.reshape(8, -1, dw.shape[-1]) for _, _, dw in items]
        lands = [_own_slot(lax.dynamic_index_in_dim(s, dev, 0, keepdims=False), dev) for s in slices]
        started = _send_start(slices, lands, None, False, f"send_{label}")
        in_flight.append((started, [(kind, layer) for kind, layer, _ in items], label))
        return (started["token"],)

    def collect(count, after):
        for _ in range(count):
            started, keys, label = in_flight.pop(0)
            lands = _send_wait(started, after, f"arrive_{label}")
            for (kind, layer), r in zip(keys, lands):
                halves[kind] = _sum_into_half(r, halves[kind], layer, ci, 2, f"sum_{kind}_{layer}")

    for i in reversed(range(depth)):
        j = i // 2
        tag = f"l{i}"
        sv = saved[i]
        older = len(in_flight)
        dy, dgate, d_post_g[i] = _post_bwd(dx, sv["out"], gate[i], post_g[i:i + 1], f"post_bwd_{tag}")
        if i % 2 == 0:
            dw_out = _mm(sv["y"], dy, ta=True, out_dtype=BF16, name=f"sgu_out_dw_{tag}")
            sent = send_grads([("sgu_w_out", j, dw_out)], f"{tag}_out")
            dyv = _mm(dy, weight("sout", j), tb=True, after=sent, name=f"sgu_out_dx_{tag}")
            duvz, dng, dws, dbs = _sgu_gate_bwd(sv["uvz"], dyv, sgu_norm_g[j:j + 1], sgu_w_s[j], w_sT[j], b_bc[j],
                                                f"sgu_gate_bwd_{tag}")
            sent = ()
            if i > 0:
                dw_in = _mm(sv["h"], duvz, ta=True, out_sharded=True, out_dtype=BF16, tk=4096,
                            name=f"sgu_in_dw_{tag}")
                sent = send_grads([("sgu_w_in", j, dw_in)], f"{tag}_in")
            dh = _mm(duvz, weight("sin", j), tb=True, b_sharded=True, tk=3072, after=sent, name=f"sgu_in_dx_{tag}")
            d_sgu[j] = dict(norm_g=dng, w_s=dws, b_s=dbs[:, :, 0], duvz=duvz)
        else:
            dw_out = _mm(sv["y"], dy, ta=True, out_dtype=BF16, name=f"mla_out_dw_{tag}")
            sent = send_grads([("mla_w_out", j, dw_out)], f"{tag}_out")
            dyv = _mm(dy, weight("mout", j), tb=True, after=sent, name=f"mla_out_dx_{tag}")
            p = sv["p"]
            do = _mla_gate_bwd(dyv, p, f"mla_gate_bwd_{tag}")
            dQ, dK, dV = _attn_bwd(sv["Q"], sv["K"], sv["V"], sv["o"], do, sv["lse"], f"attn_bwd_{tag}")
            dq, dkv, dkr = _mla_unpack(dQ, dK, dV, cos_t, sin_t, f"mla_unpack_{tag}")
            dw_uq = _mm(sv["cqn"], dq, ta=True, out_sharded=True, out_dtype=BF16, name=f"mla_uq_dw_{tag}")
            dcqn = _mm(dq, weight("uq", j), tb=True, b_sharded=True, name=f"mla_uq_dx_{tag}")
            dw_ukv = _mm(sv["ckvn"], dkv, ta=True, out_sharded=True, out_dtype=BF16, name=f"mla_ukv_dw_{tag}")
            dckvn = _mm(dkv, weight("ukv", j), tb=True, b_sharded=True, name=f"mla_ukv_dx_{tag}")
            dp, dqg, dkvg = _mla_mid_bwd(p, dcqn, dckvn, dkr, dyv, sv["o"], q_gain[j], kv_gain[j], f"mla_mid_bwd_{tag}")
            dw_in = _mm(sv["h"], dp, ta=True, out_sharded=True, out_dtype=BF16, name=f"mla_in_dw_{tag}")
            sent = send_grads([("mla_w_in", j, dw_in), ("mla_w_uq", j, dw_uq[:, :Q_RANK]), ("mla_w_ukv", j, dw_ukv)],
                              f"{tag}_in")
            dh = _mm(dp, weight("min", j), tb=True, b_sharded=True, after=sent, name=f"mla_in_dx_{tag}")
            d_mla[j] = dict(qg=dqg[0, :Q_RANK], kvg=dkvg[0])
        dx, dshift, dscale, d_pre_g[i] = _pre_bwd(dh, sv["x"], dx, pre_g[i:i + 1], scale[i], f"pre_bwd_{tag}")
        dmod[i] = jnp.concatenate([dshift, dscale, dgate], axis=1)
        collect(older, dx)
    grad_x = dx.reshape(x.shape)

    parts = [jnp.concatenate(dmod, axis=0), jnp.concatenate(d_pre_g, axis=0), jnp.concatenate(d_post_g, axis=0),
             jnp.stack([d["norm_g"][0] for d in d_sgu]), jnp.stack([d["w_s"] for d in d_sgu]),
             jnp.stack([d["b_s"] for d in d_sgu]), jnp.stack([d["qg"] for d in d_mla]),
             jnp.stack([d["kvg"] for d in d_mla]), loss_part]
    sizes = [int(np.prod(t.shape)) for t in parts]
    packed = _pack_rows(parts)
    small = [packed, parts[0]]
    small_sent = _send_start(small, [_own_slot(t, dev) for t in small], None, True, "send_small_grads")
    dw_in0 = _mm(saved[0]["h"], d_sgu[0]["duvz"], ta=True, out_sharded=True, out_dtype=BF16, tk=4096,
                 after=(small_sent["token"],), name="sgu_in_dw_l0")
    sent = send_grads([("sgu_w_in", 0, dw_in0)], "l0_in")
    packed_all, dmod_all = _send_wait(small_sent, sent[0], "arrive_small_grads")
    total = _sum_slots(packed_all, "sum_small_grads").reshape(-1)
    offs = np.concatenate([[0], np.cumsum(sizes)])
    pieces = [total[int(offs[t]):int(offs[t + 1])].reshape(parts[t].shape) for t in range(len(parts))]
    g_ada_b, g_pre_g, g_post_g, g_norm_g, g_w_s, g_b_s, g_qg_full, g_kvg_full, loss_sum = pieces
    loss = loss_sum.reshape(())
    g_qg = lax.dynamic_slice_in_dim(g_qg_full, chip * qn_w, qn_w, axis=1)
    g_kvg = lax.dynamic_slice_in_dim(g_kvg_full, chip * kvn_w, kvn_w, axis=1)
    dmod_cols = jnp.stack([lax.dynamic_slice_in_dim(dmod_all[:, i], chip * cols, cols, axis=1) for i in range(depth)])
    dmod_cols = jnp.pad(dmod_cols, ((0, 0), (0, LANE - 8), (0, 0)))
    g_ada_w = _ada_grad(jnp.pad(c_all.T, ((0, 0), (0, LANE - 8))), dmod_cols, "ada_grad")

    wnames = ["ada_w", "ada_b", "pre_g", "post_g", "sgu_w_in", "sgu_norm_g", "sgu_w_s", "sgu_b_s", "sgu_w_out",
              "mla_w_in", "mla_q_norm_g", "mla_kv_norm_g", "mla_w_uq", "mla_w_ukv", "mla_w_out"]
    weights = dict(zip(wnames, [ada_w, ada_b, pre_g, post_g, sgu_w_in, sgu_norm_g, sgu_w_s, sgu_b_s, sgu_w_out,
                                mla_w_in, mla_q_norm_g, mla_kv_norm_g, mla_w_uq, mla_w_ukv, mla_w_out]))
    ms = dict(zip(wnames, [m_ada_w, m_ada_b, m_pre_g, m_post_g, m_sgu_w_in, m_sgu_norm_g, m_sgu_w_s, m_sgu_b_s,
                           m_sgu_w_out, m_mla_w_in, m_mla_q_norm_g, m_mla_kv_norm_g, m_mla_w_uq, m_mla_w_ukv,
                           m_mla_w_out]))
    vs = dict(zip(wnames, [v_ada_w, v_ada_b, v_pre_g, v_post_g, v_sgu_w_in, v_sgu_norm_g, v_sgu_w_s, v_sgu_b_s,
                           v_sgu_w_out, v_mla_w_in, v_mla_q_norm_g, v_mla_kv_norm_g, v_mla_w_uq, v_mla_w_ukv,
                           v_mla_w_out]))
    grads = dict(ada_w=g_ada_w, ada_b=g_ada_b, pre_g=g_pre_g, post_g=g_post_g, sgu_norm_g=g_norm_g, sgu_w_s=g_w_s,
                 sgu_b_s=g_b_s, mla_q_norm_g=g_qg, mla_kv_norm_g=g_kvg)
    stepped = {}

    def step(nm):
        grads[nm] = grads[nm].reshape(weights[nm].shape)
        stepped[nm] = _adamw(weights[nm], grads[nm], ms[nm], vs[nm], f"adamw_{nm}")

    for nm in wnames:
        if nm in grads:
            step(nm)
    early = [kind for kind in kinds if kind != "sgu_w_in"]
    collect(len(in_flight) - 1, stepped["ada_w"][0])
    swapped = _pair_swap([halves[kind] for kind in early] + [halves["sgu_w_in"]], "swap_grads",
                         layers=[range(2)] * len(early) + [[1]])
    halves["sgu_w_in"] = swapped[-1]
    for kind, g in zip(early, swapped):
        grads[kind] = g
        step(kind)
    collect(len(in_flight), stepped[early[-1]][0])
    grads["sgu_w_in"], = _pair_swap([halves["sgu_w_in"]], "swap_grads_last", layers=[[0]])
    step("sgu_w_in")
    return (loss, grad_x, *[grads[nm] for nm in wnames], *[stepped[nm][0] for nm in wnames],
            *[stepped[nm][1] for nm in wnames], *[stepped[nm][2] for nm in wnames])
```

```python
import math

import jax
import jax.numpy as jnp
import numpy as np
from jax import lax
from jax.experimental import pallas as pl
from jax.experimental.pallas import tpu as pltpu

F32 = jnp.float32
BF16 = jnp.bfloat16
MESH = pl.DeviceIdType.MESH

NORM_EPS = 1e-6
CHUNK = 64
SGU_BLOCK = 128
SGU_GROUPS = 16
HEADS = 16
NOPE = 128
ROPE = 64
VDIM = 128
QK_PAD = 256
Q_RANK = 448
Q_RANK_PAD = 512
KV_RANK = 512
ROPE_THETA = 10000.0
ATTN_SCALE = (NOPE + ROPE) ** -0.5
LOG2E = 1.4426950408889634
Q_FOLD = ATTN_SCALE * LOG2E
ATTN_TILE = 512

ADAM_LR = 0.001
ADAM_B1 = 0.9
ADAM_B2 = 0.999
ADAM_EPS = 1e-08
ADAM_WD = 0.01
ADAM_STEP = 10

LANE = 128
VMEM_LIMIT = 48 * 1024 * 1024

NN = (((1,), (0,)), ((), ()))
NT = (((1,), (1,)), ((), ()))
TN = (((0,), (0,)), ((), ()))


def _params(*sem):
    return pltpu.CompilerParams(dimension_semantics=sem, vmem_limit_bytes=VMEM_LIMIT)


def _row_tile(rows, row_bytes, target_bytes=1 << 20):
    if rows * row_bytes <= target_bytes or rows % 16:
        return rows
    best = 16
    t = 16
    while t <= rows:
        if rows % t == 0 and t * row_bytes <= target_bytes:
            best = t
        t += 16
    return best


def _fit(dim, target):
    if dim <= target:
        return dim
    t = (target // LANE) * LANE
    while t > LANE and dim % t:
        t -= LANE
    return t


def _gelu(x):
    return 0.5 * x * (1.0 + lax.erf(x * 0.7071067811865476))


def _gelu_and_grad(x):
    cdf = 0.5 * (1.0 + lax.erf(x * 0.7071067811865476))
    return x * cdf, cdf + x * jnp.exp(-0.5 * x * x) * 0.3989422804014327


def _mm(a, b, *, ta=False, tb=False, b_sharded=False, out_sharded=False, out_dtype=F32,
        tm=1024, tn=1024, tk=2048, after=(), name):
    if ta:
        K, M = a.shape
        tk = max(tk, 4096)
    else:
        M, K = a.shape
    if b_sharded:
        shards, rows, Cs = b.shape
        b_shape = (rows, shards * Cs)
    else:
        b_shape = b.shape
    if tb:
        N, K2 = b_shape
    else:
        K2, N = b_shape
    assert K == K2, (a.shape, b.shape, ta, tb)
    n_lim = Cs if (b_sharded and not tb) else (N // 4 if out_sharded else N)
    k_lim = Cs if (b_sharded and tb) else K
    tm, tn, tk = _fit(M, tm), _fit(n_lim, tn), _fit(k_lim, tk)
    assert M % tm == 0 and n_lim % tn == 0 and k_lim % tk == 0, (M, N, K, tm, tn, tk)
    nk = K // tk
    nb_n = n_lim // tn
    nb_k = k_lim // tk
    dims = (((0 if ta else 1,), (1 if tb else 0,)), ((), ()))

    def body(a_ref, b_ref, *rest):
        o_ref, *scratch = rest[len(after):]
        prod = lax.dot_general(a_ref[...].astype(BF16), b_ref[...].astype(BF16), dims,
                               preferred_element_type=F32)
        if nk == 1:
            o_ref[...] = prod.astype(out_dtype)
        else:
            acc_ref, = scratch
            k = pl.program_id(2)

            @pl.when(k == 0)
            def _():
                acc_ref[...] = prod

            @pl.when(k > 0)
            def _():
                acc_ref[...] += prod

            @pl.when(k == nk - 1)
            def _():
                o_ref[...] = acc_ref[...].astype(out_dtype)

    a_spec = (pl.BlockSpec((tk, tm), lambda i, j, k: (k, i)) if ta
              else pl.BlockSpec((tm, tk), lambda i, j, k: (i, k)))
    if b_sharded and tb:
        b_spec = pl.BlockSpec((None, tn, tk), lambda i, j, k: (k // nb_k, j, k % nb_k))
    elif b_sharded:
        b_spec = pl.BlockSpec((None, tk, tn), lambda i, j, k: (j // nb_n, k, j % nb_n))
    elif tb:
        b_spec = pl.BlockSpec((tn, tk), lambda i, j, k: (j, k))
    else:
        b_spec = pl.BlockSpec((tk, tn), lambda i, j, k: (k, j))
    if out_sharded:
        out_shape = jax.ShapeDtypeStruct((4, M, N // 4), out_dtype)
        out_spec = pl.BlockSpec((None, tm, tn), lambda i, j, k: (j // nb_n, i, j % nb_n))
    else:
        out_shape = jax.ShapeDtypeStruct((M, N), out_dtype)
        out_spec = pl.BlockSpec((tm, tn), lambda i, j, k: (i, j))
    return pl.pallas_call(
        body, name=name,
        out_shape=out_shape,
        grid=(M // tm, N // tn, nk),
        in_specs=[a_spec, b_spec] + [pl.BlockSpec(memory_space=pl.ANY)] * len(after),
        out_specs=out_spec,
        scratch_shapes=[] if nk == 1 else [pltpu.VMEM((tm, tn), F32)],
        compiler_params=_params("parallel", "parallel", "arbitrary"),
    )(a, b, *after)


def _split_bf16(v):
    hi = v.astype(BF16)
    lo = (v - hi.astype(F32)).astype(BF16)
    return hi, lo


def _dot3(a, b, dims):
    a_hi, a_lo = _split_bf16(a)
    b_hi, b_lo = _split_bf16(b)
    out = lax.dot_general(a_hi, b_hi, dims, preferred_element_type=F32)
    out += lax.dot_general(a_lo, b_hi, dims, preferred_element_type=F32)
    out += lax.dot_general(a_hi, b_lo, dims, preferred_element_type=F32)
    return out


def _ada_mod(c_all, ada_w, ada_b_cols, name):
    L, D, cols = ada_w.shape
    B = c_all.shape[0]
    tn = 512 if cols % 512 == 0 else cols

    def body(c_ref, w_ref, b_ref, o_ref):
        cv = c_ref[...]
        cond = cv * jax.nn.sigmoid(cv)
        o_ref[...] = _dot3(cond, w_ref[...], NN) + b_ref[...]

    return pl.pallas_call(
        body, name=name,
        out_shape=jax.ShapeDtypeStruct((L, B, cols), F32),
        grid=(L, cols // tn),
        in_specs=[pl.BlockSpec((B, D), lambda l, j: (0, 0)),
                  pl.BlockSpec((None, D, tn), lambda l, j: (l, 0, j)),
                  pl.BlockSpec((None, 1, tn), lambda l, j: (l, 0, j))],
        out_specs=pl.BlockSpec((None, B, tn), lambda l, j: (l, 0, j)),
        compiler_params=_params("parallel", "parallel"),
    )(c_all, ada_w, ada_b_cols.reshape(L, 1, cols))


def _ada_grad(c_t, dmod_cols, name, after=()):
    L, B, cols = dmod_cols.shape
    D = c_t.shape[0]
    tn = 512 if cols % 512 == 0 else cols

    def body(c_ref, d_ref, *rest):
        o_ref = rest[-1]
        cv = c_ref[...]
        cond = cv * jax.nn.sigmoid(cv)
        o_ref[...] = _dot3(cond, d_ref[...], NN)

    return pl.pallas_call(
        body, name=name,
        out_shape=jax.ShapeDtypeStruct((L, D, cols), F32),
        grid=(L, cols // tn),
        in_specs=[pl.BlockSpec((D, B), lambda l, j: (0, 0)),
                  pl.BlockSpec((None, B, tn), lambda l, j: (l, 0, j))] + [pl.BlockSpec(memory_space=pl.ANY)] * len(after),
        out_specs=pl.BlockSpec((None, D, tn), lambda l, j: (l, 0, j)),
        compiler_params=_params("parallel", "parallel"),
    )(c_t, dmod_cols, *after)


def _row_spec(ts, width):
    return pl.BlockSpec((ts, width), lambda i: (i, 0))


def _vec_spec(width):
    return pl.BlockSpec((1, width), lambda i: (0, 0))


def _pre_fwd(x, pre_g, scale, shift, name, after=()):
    S, D = x.shape
    ts = min(256, S)

    def body(x_ref, g_ref, sc_ref, sh_ref, *rest):
        h_ref = rest[-1]
        xv = x_ref[...]
        r = lax.rsqrt(jnp.mean(xv * xv, axis=-1, keepdims=True) + NORM_EPS)
        h_ref[...] = ((xv * r * g_ref[...]) * (1.0 + sc_ref[...]) + sh_ref[...]).astype(BF16)

    return pl.pallas_call(
        body, name=name, out_shape=jax.ShapeDtypeStruct((S, D), BF16), grid=(S // ts,),
        in_specs=[_row_spec(ts, D), _vec_spec(D), _vec_spec(D), _vec_spec(D)]
        + [pl.BlockSpec(memory_space=pl.ANY)] * len(after),
        out_specs=_row_spec(ts, D), compiler_params=_params("parallel"),
    )(x, pre_g, scale, shift, *after)


def _pre_bwd(dh, x, dx_res, pre_g, scale, name):
    S, D = x.shape
    ts = min(256, S)

    def body(dh_ref, x_ref, dr_ref, g_ref, sc_ref, dx_ref, dsh_ref, dsc_ref, dg_ref):
        @pl.when(pl.program_id(0) == 0)
        def _():
            dsh_ref[...] = jnp.zeros_like(dsh_ref)
            dsc_ref[...] = jnp.zeros_like(dsc_ref)
            dg_ref[...] = jnp.zeros_like(dg_ref)

        dh = dh_ref[...]
        xv = x_ref[...]
        g = g_ref[...]
        one_sc = 1.0 + sc_ref[...]
        r = lax.rsqrt(jnp.mean(xv * xv, axis=-1, keepdims=True) + NORM_EPS)
        xn = xv * r
        dsh_ref[...] += jnp.sum(dh, axis=0, keepdims=True)
        dsc_ref[...] += jnp.sum(dh * (xn * g), axis=0, keepdims=True)
        dg_ref[...] += jnp.sum(dh * one_sc * xn, axis=0, keepdims=True)
        dxn = dh * one_sc * g
        dx_ref[...] = dr_ref[...] + r * (dxn - xn * jnp.mean(dxn * xn, axis=-1, keepdims=True))

    vec = jax.ShapeDtypeStruct((1, D), F32)
    return pl.pallas_call(
        body, name=name, out_shape=(jax.ShapeDtypeStruct((S, D), F32), vec, vec, vec), grid=(S // ts,),
        in_specs=[_row_spec(ts, D), _row_spec(ts, D), _row_spec(ts, D), _vec_spec(D), _vec_spec(D)],
        out_specs=(_row_spec(ts, D), _vec_spec(D), _vec_spec(D), _vec_spec(D)),
        compiler_params=_params("arbitrary"),
    )(dh, x, dx_res, pre_g, scale)


def _post_fwd(x, y, gate, post_g, name):
    S, D = x.shape
    ts = min(256, S)

    def body(x_ref, y_ref, gt_ref, g_ref, o_ref):
        yv = y_ref[...]
        r = lax.rsqrt(jnp.mean(yv * yv, axis=-1, keepdims=True) + NORM_EPS)
        o_ref[...] = x_ref[...] + gt_ref[...] * (yv * r * g_ref[...])

    return pl.pallas_call(
        body, name=name, out_shape=jax.ShapeDtypeStruct((S, D), F32), grid=(S // ts,),
        in_specs=[_row_spec(ts, D), _row_spec(ts, D), _vec_spec(D), _vec_spec(D)],
        out_specs=_row_spec(ts, D), compiler_params=_params("parallel"),
    )(x, y, gate, post_g)


def _post_bwd(dx, y, gate, post_g, name):
    S, D = y.shape
    ts = min(256, S)

    def body(dx_ref, y_ref, gt_ref, g_ref, dy_ref, dgt_ref, dg_ref):
        @pl.when(pl.program_id(0) == 0)
        def _():
            dgt_ref[...] = jnp.zeros_like(dgt_ref)
            dg_ref[...] = jnp.zeros_like(dg_ref)

        dxv = dx_ref[...]
        yv = y_ref[...]
        g = g_ref[...]
        gt = gt_ref[...]
        r = lax.rsqrt(jnp.mean(yv * yv, axis=-1, keepdims=True) + NORM_EPS)
        yn = yv * r
        dgt_ref[...] += jnp.sum(dxv * (yn * g), axis=0, keepdims=True)
        dg_ref[...] += jnp.sum(dxv * gt * yn, axis=0, keepdims=True)
        dyn = dxv * gt * g
        dy_ref[...] = (r * (dyn - yn * jnp.mean(dyn * yn, axis=-1, keepdims=True))).astype(BF16)

    vec = jax.ShapeDtypeStruct((1, D), F32)
    return pl.pallas_call(
        body, name=name, out_shape=(jax.ShapeDtypeStruct((S, D), BF16), vec, vec), grid=(S // ts,),
        in_specs=[_row_spec(ts, D), _row_spec(ts, D), _vec_spec(D), _vec_spec(D)],
        out_specs=(_row_spec(ts, D), _vec_spec(D), _vec_spec(D)),
        compiler_params=_params("arbitrary"),
    )(dx, y, gate, post_g)


def _loss_grad(xf, target, name):
    S, D = xf.shape
    ts = min(256, S)

    def body(x_ref, t_ref, dx_ref, l_ref):
        @pl.when(pl.program_id(0) == 0)
        def _():
            l_ref[...] = jnp.zeros_like(l_ref)

        e = x_ref[...] - t_ref[...]
        dx_ref[...] = e * (1.0 / D)
        row = jnp.sum(e * e, axis=1, keepdims=True) * (1.0 / D)
        l_ref[...] += 0.5 * jnp.sum(row, axis=0, keepdims=True)

    return pl.pallas_call(
        body, name=name,
        out_shape=(jax.ShapeDtypeStruct((S, D), F32), jax.ShapeDtypeStruct((1, 1), F32)), grid=(S // ts,),
        in_specs=[_row_spec(ts, D), _row_spec(ts, D)],
        out_specs=(_row_spec(ts, D), pl.BlockSpec((1, 1), lambda i: (0, 0))),
        compiler_params=_params("arbitrary"),
    )(xf, target)


def _chunk_mask(transposed=False):
    row = lax.broadcasted_iota(jnp.int32, (SGU_BLOCK, SGU_BLOCK), 0) // CHUNK
    col = lax.broadcasted_iota(jnp.int32, (SGU_BLOCK, SGU_BLOCK), 1) // CHUNK
    return (row <= col) if transposed else (col <= row)


def _sgu_gate_fwd(uvz, norm_g, w_s, b_bc, name):
    S, E3 = uvz.shape
    E = E3 // 3
    T = SGU_BLOCK
    gd = E // SGU_GROUPS

    def body(uvz_ref, ng_ref, ws_ref, bb_ref, y_ref, v_scr):
        gv = _gelu(uvz_ref[:, E:2 * E])
        mu = jnp.mean(gv, axis=-1, keepdims=True)
        xc = gv - mu
        rstd = lax.rsqrt(jnp.mean(xc * xc, axis=-1, keepdims=True) + NORM_EPS)
        v_scr[...] = (xc * rstd * ng_ref[...]).astype(BF16)
        mask = _chunk_mask()
        for g in range(SGU_GROUPS):
            sl = slice(g * gd, (g + 1) * gd)
            wg = jnp.where(mask, ws_ref[g], 0.0).astype(BF16)
            vm = lax.dot_general(wg, v_scr[:, sl], NN, preferred_element_type=F32)
            vm = vm + jnp.tile(bb_ref[g], (1, gd // LANE))
            z = uvz_ref[:, 2 * E + g * gd:2 * E + (g + 1) * gd]
            y_ref[:, sl] = (_gelu(uvz_ref[:, sl]) * vm * (z * jax.nn.sigmoid(z))).astype(BF16)

    return pl.pallas_call(
        body, name=name, out_shape=jax.ShapeDtypeStruct((S, E), BF16), grid=(S // T,),
        in_specs=[_row_spec(T, E3), _vec_spec(E),
                  pl.BlockSpec((SGU_GROUPS, T, T), lambda i: (0, 0, 0)),
                  pl.BlockSpec((SGU_GROUPS, T, LANE), lambda i: (0, 0, 0))],
        out_specs=_row_spec(T, E),
        scratch_shapes=[pltpu.VMEM((T, E), BF16)],
        compiler_params=_params("parallel"),
    )(uvz, norm_g, w_s, b_bc)


def _sgu_gate_bwd(uvz, dyv, norm_g, w_s, w_sT, b_bc, name):
    S, E3 = uvz.shape
    E = E3 // 3
    T = SGU_BLOCK
    gd = E // SGU_GROUPS

    def body(uvz_ref, dyv_ref, ng_ref, ws_ref, wst_ref, bb_ref,
             d_ref, dng_ref, dws_ref, dbs_ref, vhat_scr, dv_scr, vgrad_scr):
        @pl.when(pl.program_id(0) == 0)
        def _():
            dng_ref[...] = jnp.zeros_like(dng_ref)
            dws_ref[...] = jnp.zeros_like(dws_ref)
            dbs_ref[...] = jnp.zeros_like(dbs_ref)

        gv, vgrad_scr[...] = _gelu_and_grad(uvz_ref[:, E:2 * E])
        mu = jnp.mean(gv, axis=-1, keepdims=True)
        xc = gv - mu
        rstd = lax.rsqrt(jnp.mean(xc * xc, axis=-1, keepdims=True) + NORM_EPS)
        vhat_scr[...] = xc * rstd
        mask = _chunk_mask()
        mask_t = _chunk_mask(transposed=True)
        for g in range(SGU_GROUPS):
            sl = slice(g * gd, (g + 1) * gd)
            u_pre = uvz_ref[:, sl]
            z = uvz_ref[:, 2 * E + g * gd:2 * E + (g + 1) * gd]
            dy = dyv_ref[:, sl]
            u, u_grad = _gelu_and_grad(u_pre)
            sig = jax.nn.sigmoid(z)
            sz = z * sig
            vg = (vhat_scr[:, sl] * ng_ref[:, sl]).astype(BF16)
            wg = jnp.where(mask, ws_ref[g], 0.0).astype(BF16)
            vm = lax.dot_general(wg, vg, NN, preferred_element_type=F32)
            vm = vm + jnp.tile(bb_ref[g], (1, gd // LANE))
            dy_u = dy * u
            d_ref[:, sl] = (dy * vm * sz * u_grad).astype(BF16)
            d_ref[:, 2 * E + g * gd:2 * E + (g + 1) * gd] = (
                dy_u * vm * (sig * (1.0 + z * (1.0 - sig)))).astype(BF16)
            dvm = dy_u * sz
            dvm_b = dvm.astype(BF16)
            dws_ref[g] += jnp.where(mask, lax.dot_general(dvm_b, vg, NT, preferred_element_type=F32), 0.0)
            dbs_ref[g] += jnp.broadcast_to(jnp.sum(dvm, axis=1, keepdims=True), (T, LANE))
            wgt = jnp.where(mask_t, wst_ref[g], 0.0).astype(BF16)
            dv_scr[:, sl] = lax.dot_general(wgt, dvm_b, NN, preferred_element_type=F32)
        dv = dv_scr[...]
        vhat = vhat_scr[...]
        dng_ref[...] += jnp.sum(dv * vhat, axis=0, keepdims=True)
        dvh = dv * ng_ref[...]
        dgv = rstd * (dvh - jnp.mean(dvh, axis=-1, keepdims=True)
                      - vhat * jnp.mean(dvh * vhat, axis=-1, keepdims=True))
        d_ref[:, E:2 * E] = (dgv * vgrad_scr[...]).astype(BF16)

    wspec = pl.BlockSpec((SGU_GROUPS, T, T), lambda i: (0, 0, 0))
    bspec = pl.BlockSpec((SGU_GROUPS, T, LANE), lambda i: (0, 0, 0))
    return pl.pallas_call(
        body, name=name,
        out_shape=(jax.ShapeDtypeStruct((S, E3), BF16), jax.ShapeDtypeStruct((1, E), F32),
                   jax.ShapeDtypeStruct((SGU_GROUPS, T, T), F32),
                   jax.ShapeDtypeStruct((SGU_GROUPS, T, LANE), F32)),
        grid=(S // T,),
        in_specs=[_row_spec(T, E3), _row_spec(T, E), _vec_spec(E), wspec, wspec, bspec],
        out_specs=(_row_spec(T, E3), _vec_spec(E), wspec, bspec),
        scratch_shapes=[pltpu.VMEM((T, E), F32), pltpu.VMEM((T, E), F32), pltpu.VMEM((T, E), F32)],
        compiler_params=_params("arbitrary"),
    )(uvz, dyv, norm_g, w_s, w_sT, b_bc)


MLA_WIDTH = HEADS * VDIM
P_LATENT = Q_RANK + KV_RANK + ROPE
P_WIDTH = P_LATENT + MLA_WIDTH


def _swap_halves(v):
    lane = lax.broadcasted_iota(jnp.int32, v.shape, 1)
    return jnp.where(lane % ROPE < ROPE // 2, pltpu.roll(v, LANE - ROPE // 2, 1), pltpu.roll(v, ROPE // 2, 1))


def _low_lanes(rows):
    return lax.broadcasted_iota(jnp.int32, (rows, LANE), 1) < ROPE


def _latent_tiles(ref):
    return [ref[:, t * LANE:(t + 1) * LANE] for t in range(P_LATENT // LANE)]


def _split_latents(tiles, low):
    cq = jnp.concatenate(tiles[0:3] + [jnp.where(low, tiles[3], 0.0)], axis=1)
    rolled = [pltpu.roll(t, ROPE, 1) for t in tiles[3:8]]
    ckv = jnp.concatenate([jnp.where(low, rolled[t], rolled[t + 1]) for t in range(4)], axis=1)
    kr = jnp.where(low, rolled[4], 0.0)
    return cq, ckv, kr


def _mla_mid_fwd(p, qg, kvg, name):
    S, PW = p.shape
    ts = min(256, S)

    def body(p_ref, qg_ref, kvg_ref, cqn_ref, ckvn_ref):
        cq, ckv, _ = _split_latents(_latent_tiles(p_ref), _low_lanes(ts))
        r = lax.rsqrt(jnp.sum(cq * cq, axis=-1, keepdims=True) * (1.0 / Q_RANK) + NORM_EPS)
        cqn_ref[...] = (cq * r * qg_ref[...]).astype(BF16)
        r2 = lax.rsqrt(jnp.mean(ckv * ckv, axis=-1, keepdims=True) + NORM_EPS)
        ckvn_ref[...] = (ckv * r2 * kvg_ref[...]).astype(BF16)

    return pl.pallas_call(
        body, name=name,
        out_shape=(jax.ShapeDtypeStruct((S, Q_RANK_PAD), BF16), jax.ShapeDtypeStruct((S, KV_RANK), BF16)),
        grid=(S // ts,),
        in_specs=[_row_spec(ts, P_LATENT), _vec_spec(Q_RANK_PAD), _vec_spec(KV_RANK)],
        out_specs=(_row_spec(ts, Q_RANK_PAD), _row_spec(ts, KV_RANK)),
        compiler_params=_params("parallel"),
    )(p, qg, kvg)


def _mla_pack(q, kv, p, cos_t, sin_t, name):
    S = q.shape[0]
    ts = min(256, S)
    pair_w = 2 * (NOPE + ROPE)
    head_w = NOPE + VDIM

    def body(q_ref, kv_ref, kr_ref, cos_ref, sin_ref, qo_ref, ko_ref, vo_ref, vt_ref):
        cosv = cos_ref[...]
        sinv = sin_ref[...]
        low = _low_lanes(ts)
        kr = jnp.where(low, pltpu.roll(kr_ref[...], ROPE, 1), 0.0)
        kr = (kr * cosv + _swap_halves(kr) * sinv).astype(BF16)
        for pair in range(HEADS // 2):
            t0, t1, t2 = (q_ref[:, pair * pair_w + t * LANE:pair * pair_w + (t + 1) * LANE] for t in range(3))
            nope_b = jnp.where(low, pltpu.roll(t1, ROPE, 1), pltpu.roll(t2, ROPE, 1))
            ropes = jnp.where(low, t1, t2)
            roped = (ropes * cosv + _swap_halves(ropes) * sinv) * Q_FOLD
            qo_ref[2 * pair, :, 0:NOPE] = (t0 * Q_FOLD).astype(BF16)
            qo_ref[2 * pair, :, NOPE:QK_PAD] = jnp.where(low, roped, 0.0).astype(BF16)
            qo_ref[2 * pair + 1, :, 0:NOPE] = (nope_b * Q_FOLD).astype(BF16)
            qo_ref[2 * pair + 1, :, NOPE:QK_PAD] = jnp.where(low, pltpu.roll(roped, ROPE, 1), 0.0).astype(BF16)
        eye = _identity(VDIM)
        for h in range(HEADS):
            ko_ref[h, :, 0:NOPE] = kv_ref[:, h * head_w:h * head_w + NOPE].astype(BF16)
            ko_ref[h, :, NOPE:QK_PAD] = kr
            vh = kv_ref[:, h * head_w + NOPE:(h + 1) * head_w].astype(BF16)
            vo_ref[h] = vh
            vt_ref[h] = lax.dot_general(eye, vh, NT, preferred_element_type=F32).astype(BF16)

    T = min(ATTN_TILE, S)
    per_tile = T // ts
    return pl.pallas_call(
        body, name=name,
        out_shape=(jax.ShapeDtypeStruct((HEADS, S, QK_PAD), BF16), jax.ShapeDtypeStruct((HEADS, S, QK_PAD), BF16),
                   jax.ShapeDtypeStruct((HEADS, S, VDIM), BF16), jax.ShapeDtypeStruct((HEADS, S // T, VDIM, T), BF16)),
        grid=(S // ts,),
        in_specs=[_row_spec(ts, q.shape[1]), _row_spec(ts, kv.shape[1]),
                  pl.BlockSpec((ts, LANE), lambda i: (i, P_LATENT // LANE - 1)),
                  _row_spec(ts, LANE), _row_spec(ts, LANE)],
        out_specs=(pl.BlockSpec((HEADS, ts, QK_PAD), lambda i: (0, i, 0)),
                   pl.BlockSpec((HEADS, ts, QK_PAD), lambda i: (0, i, 0)),
                   pl.BlockSpec((HEADS, ts, VDIM), lambda i: (0, i, 0)),
                   pl.BlockSpec((HEADS, None, VDIM, ts), lambda i: (0, i // per_tile, 0, i % per_tile))),
        compiler_params=_params("parallel"),
    )(q, kv, p, cos_t, sin_t)


def _mla_unpack(dQ, dK, dV, cos_t, sin_t, name):
    S = dQ.shape[1]
    ts = min(256, S)
    pair_w = 2 * (NOPE + ROPE)
    head_w = NOPE + VDIM

    def body(dq_ref, dk_ref, dv_ref, cos_ref, sin_ref, q_ref, kv_ref, kr_ref):
        cosv = cos_ref[...]
        sinv = sin_ref[...]
        low = _low_lanes(ts)
        for pair in range(HEADS // 2):
            blk = dq_ref[2 * pair, :, NOPE:QK_PAD] + pltpu.roll(dq_ref[2 * pair + 1, :, NOPE:QK_PAD], ROPE, 1)
            ropes = blk * cosv - _swap_halves(blk) * sinv
            nope_b = pltpu.roll(dq_ref[2 * pair + 1, :, 0:NOPE], ROPE, 1)
            q_ref[:, pair * pair_w:pair * pair_w + LANE] = dq_ref[2 * pair, :, 0:NOPE].astype(BF16)
            q_ref[:, pair * pair_w + LANE:pair * pair_w + 2 * LANE] = jnp.where(low, ropes, nope_b).astype(BF16)
            q_ref[:, pair * pair_w + 2 * LANE:(pair + 1) * pair_w] = jnp.where(low, nope_b, ropes).astype(BF16)
        dkr = dk_ref[0, :, NOPE:QK_PAD]
        for h in range(1, HEADS):
            dkr = dkr + dk_ref[h, :, NOPE:QK_PAD]
        kr_ref[...] = dkr * cosv - _swap_halves(dkr) * sinv
        for h in range(HEADS):
            kv_ref[:, h * head_w:h * head_w + NOPE] = dk_ref[h, :, 0:NOPE].astype(BF16)
            kv_ref[:, h * head_w + NOPE:(h + 1) * head_w] = dv_ref[h].astype(BF16)

    return pl.pallas_call(
        body, name=name,
        out_shape=(jax.ShapeDtypeStruct((S, HEADS * (NOPE + ROPE)), BF16),
                   jax.ShapeDtypeStruct((S, HEADS * (NOPE + VDIM)), BF16),
                   jax.ShapeDtypeStruct((S, LANE), F32)),
        grid=(S // ts,),
        in_specs=[pl.BlockSpec((HEADS, ts, QK_PAD), lambda i: (0, i, 0)),
                  pl.BlockSpec((HEADS, ts, QK_PAD), lambda i: (0, i, 0)),
                  pl.BlockSpec((HEADS, ts, VDIM), lambda i: (0, i, 0)),
                  _row_spec(ts, LANE), _row_spec(ts, LANE)],
        out_specs=(_row_spec(ts, HEADS * (NOPE + ROPE)), _row_spec(ts, HEADS * (NOPE + VDIM)),
                   _row_spec(ts, LANE)),
        compiler_params=_params("parallel"),
    )(dQ, dK, dV, cos_t, sin_t)


def _mla_gate_fwd(o, p, name):
    S, W = o.shape
    ts = min(256, S)
    wb = P_LATENT

    def body(o_ref, z_ref, y_ref):
        z = z_ref[...]
        y_ref[...] = (o_ref[...] * (z * jax.nn.sigmoid(z))).astype(BF16)

    return pl.pallas_call(
        body, name=name, out_shape=jax.ShapeDtypeStruct((S, W), BF16), grid=(S // ts, W // wb),
        in_specs=[pl.BlockSpec((ts, wb), lambda i, j: (i, j)), pl.BlockSpec((ts, wb), lambda i, j: (i, j + 1))],
        out_specs=pl.BlockSpec((ts, wb), lambda i, j: (i, j)), compiler_params=_params("parallel", "parallel"),
    )(o, p)


def _mla_gate_bwd(dyv, p, name):
    S, W = dyv.shape
    ts = min(256, S)
    wb = P_LATENT

    def body(d_ref, z_ref, do_ref):
        z = z_ref[...]
        do_ref[...] = d_ref[...] * (z * jax.nn.sigmoid(z))

    return pl.pallas_call(
        body, name=name, out_shape=jax.ShapeDtypeStruct((S, W), F32), grid=(S // ts, W // wb),
        in_specs=[pl.BlockSpec((ts, wb), lambda i, j: (i, j)), pl.BlockSpec((ts, wb), lambda i, j: (i, j + 1))],
        out_specs=pl.BlockSpec((ts, wb), lambda i, j: (i, j)), compiler_params=_params("parallel", "parallel"),
    )(dyv, p)


def _mla_mid_bwd(p, dcqn, dckvn, dkr, dyv, o, qg, kvg, name):
    S, PW = p.shape
    W = o.shape[1]
    ts = min(256, S)
    nt = Q_RANK_PAD // LANE

    def rms_bwd(xv, dy, g, count):
        r = lax.rsqrt(jnp.sum(xv * xv, axis=-1, keepdims=True) * (1.0 / count) + NORM_EPS)
        xn = xv * r
        dg = jnp.sum(dy * xn, axis=0, keepdims=True)
        dxn = dy * g
        dx = r * (dxn - xn * (jnp.sum(dxn * xn, axis=-1, keepdims=True) * (1.0 / count)))
        return dx, dg

    def body(p_ref, dcq_ref, dckv_ref, dkr_ref, dyv_ref, o_ref, qg_ref, kvg_ref, dp_ref, dqg_ref, dkvg_ref):
        @pl.when(pl.program_id(0) == 0)
        def _():
            dqg_ref[...] = jnp.zeros_like(dqg_ref)
            dkvg_ref[...] = jnp.zeros_like(dkvg_ref)

        low = _low_lanes(ts)
        cq, ckv, _ = _split_latents(_latent_tiles(p_ref), low)
        dcq, dg = rms_bwd(cq, dcq_ref[...], qg_ref[...], Q_RANK)
        dqg_ref[...] += dg
        dckv, dg = rms_bwd(ckv, dckv_ref[...], kvg_ref[...], KV_RANK)
        dkvg_ref[...] += dg
        moved = [pltpu.roll(dckv[:, t * LANE:(t + 1) * LANE], ROPE, 1) for t in range(nt)]
        moved.append(pltpu.roll(dkr_ref[...], ROPE, 1))
        for t in range(nt - 1):
            dp_ref[:, t * LANE:(t + 1) * LANE] = dcq[:, t * LANE:(t + 1) * LANE].astype(BF16)
        dp_ref[:, (nt - 1) * LANE:nt * LANE] = jnp.where(low, dcq[:, (nt - 1) * LANE:nt * LANE], moved[0]).astype(BF16)
        for t in range(nt):
            dp_ref[:, (nt + t) * LANE:(nt + t + 1) * LANE] = jnp.where(low, moved[t], moved[t + 1]).astype(BF16)
        z = p_ref[:, P_LATENT:PW]
        sig = jax.nn.sigmoid(z)
        dp_ref[:, P_LATENT:PW] = (dyv_ref[...] * o_ref[...] * (sig * (1.0 + z * (1.0 - sig)))).astype(BF16)

    return pl.pallas_call(
        body, name=name,
        out_shape=(jax.ShapeDtypeStruct((S, PW), BF16), jax.ShapeDtypeStruct((1, Q_RANK_PAD), F32),
                   jax.ShapeDtypeStruct((1, KV_RANK), F32)),
        grid=(S // ts,),
        in_specs=[_row_spec(ts, PW), _row_spec(ts, Q_RANK_PAD), _row_spec(ts, KV_RANK), _row_spec(ts, LANE),
                  _row_spec(ts, W), _row_spec(ts, W), _vec_spec(Q_RANK_PAD), _vec_spec(KV_RANK)],
        out_specs=(_row_spec(ts, PW), _vec_spec(Q_RANK_PAD), _vec_spec(KV_RANK)),
        compiler_params=_params("arbitrary"),
    )(p, dcqn, dckvn, dkr, dyv, o, qg, kvg)


def _tile_mask(T):
    row = lax.broadcasted_iota(jnp.int32, (T, T), 0) // CHUNK
    col = lax.broadcasted_iota(jnp.int32, (T, T), 1) // CHUNK
    return col <= row


def _attn_bwd(Q, K, V, o, do, lse, name):
    H, S, _ = Q.shape
    T = min(ATTN_TILE, S)
    nq = S // T

    def body(q_ref, k_ref, v_ref, o_ref, do_ref, lse_ref, dq_ref, dk_ref, dv_ref, dk_scr, dv_scr, s_scr, dp_scr):
        ki = pl.program_id(1)

        @pl.when(ki == 0)
        def _():
            dq_ref[...] = jnp.zeros_like(dq_ref)

        dk_scr[...] = jnp.zeros_like(dk_scr)
        dv_scr[...] = jnp.zeros_like(dv_scr)
        k = k_ref[...]
        v = v_ref[...]

        def scores(i):
            rows = pl.ds(pl.multiple_of(i * T, T), T)
            s = lax.dot_general(q_ref[rows, :], k, NT, preferred_element_type=F32)
            dp = lax.dot_general(do_ref[rows, :].astype(BF16), v, NT, preferred_element_type=F32)
            return s, dp

        def grads(i, s, dp, masked):
            rows = pl.ds(pl.multiple_of(i * T, T), T)
            do_f = do_ref[rows, :]
            delta = jnp.sum(do_f * o_ref[rows, :], axis=1, keepdims=True)
            pr = jnp.exp2(s - lse_ref[rows, 0:1])
            if masked:
                pr = jnp.where(_tile_mask(T), pr, 0.0)
            dv_scr[...] += lax.dot_general(pr.astype(BF16), do_f.astype(BF16), TN, preferred_element_type=F32)
            ds = (pr * (dp - delta)).astype(BF16)
            dk_scr[...] += lax.dot_general(ds, q_ref[rows, :], TN, preferred_element_type=F32)
            dq_ref[rows, :] += lax.dot_general(ds, k, NN, preferred_element_type=F32) * ATTN_SCALE

        s_scr[...], dp_scr[...] = scores(ki)

        @pl.when(ki + 1 == nq)
        def _():
            grads(ki, s_scr[...], dp_scr[...], True)

        @pl.when(ki + 1 < nq)
        def _():
            def step(i, masked):
                nxt_s, nxt_dp = scores(i + 1)
                grads(i, s_scr[...], dp_scr[...], masked)
                s_scr[...] = nxt_s
                dp_scr[...] = nxt_dp

            def loop_step(i, carry):
                step(i, False)
                return carry

            step(ki, True)
            lax.fori_loop(ki + 1, nq - 1, loop_step, 0)
            grads(nq - 1, s_scr[...], dp_scr[...], False)

        dk_ref[...] = dk_scr[...] * (1.0 / LOG2E)
        dv_ref[...] = dv_scr[...]

    return pl.pallas_call(
        body, name=name,
        out_shape=(jax.ShapeDtypeStruct((H, S, QK_PAD), F32), jax.ShapeDtypeStruct((H, S, QK_PAD), F32),
                   jax.ShapeDtypeStruct((H, S, VDIM), F32)),
        grid=(H, nq),
        in_specs=[pl.BlockSpec((None, S, QK_PAD), lambda h, j: (h, 0, 0)),
                  pl.BlockSpec((None, T, QK_PAD), lambda h, j: (h, j, 0)),
                  pl.BlockSpec((None, T, VDIM), lambda h, j: (h, j, 0)),
                  pl.BlockSpec((S, VDIM), lambda h, j: (0, h)),
                  pl.BlockSpec((S, VDIM), lambda h, j: (0, h)),
                  pl.BlockSpec((None, S, LANE), lambda h, j: (h, 0, 0))],
        out_specs=(pl.BlockSpec((None, S, QK_PAD), lambda h, j: (h, 0, 0)),
                   pl.BlockSpec((None, T, QK_PAD), lambda h, j: (h, j, 0)),
                   pl.BlockSpec((None, T, VDIM), lambda h, j: (h, j, 0))),
        scratch_shapes=[pltpu.VMEM((T, QK_PAD), F32), pltpu.VMEM((T, VDIM), F32),
                        pltpu.VMEM((T, T), F32), pltpu.VMEM((T, T), F32)],
        compiler_params=_params("parallel", "arbitrary"),
    )(Q, K, V, o, do, lse)


def _identity(n):
    return (lax.broadcasted_iota(jnp.int32, (n, n), 0) == lax.broadcasted_iota(jnp.int32, (n, n), 1)).astype(BF16)


def _key_le_query(rows, cols, col0):
    key = lax.broadcasted_iota(jnp.int32, (rows, cols), 0) // CHUNK
    query = (lax.broadcasted_iota(jnp.int32, (rows, cols), 1) + col0) // CHUNK
    return key <= query


def _attn_fwd_t(Q, K, VT, name):
    H, nT, _, T = VT.shape
    S = nT * T
    n_part = 2 if T % 256 == 0 else 1
    Tq = T // n_part

    def body(q_ref, k_ref, vt_ref, o_ref, lse_ref, m_scr, l_scr, acc_scr, s_scr):
        qi = pl.program_id(1)
        m_scr[...] = jnp.full_like(m_scr, -jnp.inf)
        l_scr[...] = jnp.zeros_like(l_scr)
        acc_scr[...] = jnp.zeros_like(acc_scr)

        def scores(j):
            kt = k_ref[pl.ds(pl.multiple_of(j * T, T), T), :]
            return lax.dot_general(kt, q_ref[...], NT, preferred_element_type=F32)

        def softmax_step(j, masked):
            vt = vt_ref[j]
            for part in range(n_part):
                sub = slice(part * Tq, (part + 1) * Tq)
                st = s_scr[:, sub]
                if masked:
                    st = jnp.where(_key_le_query(T, Tq, part * Tq), st, -1e30)
                m_prev = m_scr[:, sub]
                m_new = jnp.maximum(m_prev, jnp.max(st, axis=0, keepdims=True))
                pt = jnp.exp2(st - m_new)
                alpha = jnp.exp2(m_prev - m_new)
                l_scr[:, sub] = alpha * l_scr[:, sub] + jnp.sum(pt, axis=0, keepdims=True)
                acc_scr[:, sub] = alpha * acc_scr[:, sub] + lax.dot_general(
                    vt, pt.astype(BF16), NN, preferred_element_type=F32)
                m_scr[:, sub] = m_new

        s_scr[...] = scores(0)

        def step(j, carry):
            nxt = scores(j + 1)
            softmax_step(j, False)
            s_scr[...] = nxt
            return carry

        lax.fori_loop(0, qi, step, 0)
        softmax_step(qi, True)
        l = l_scr[...]
        o_ref[...] = jnp.transpose(acc_scr[...] / l)
        lse_ref[...] = jnp.transpose(jnp.broadcast_to(m_scr[...] + jnp.log2(l), (LANE, T)))

    return pl.pallas_call(
        body, name=name,
        out_shape=(jax.ShapeDtypeStruct((S, H * VDIM), F32), jax.ShapeDtypeStruct((H, S, LANE), F32)),
        grid=(H, nT),
        in_specs=[pl.BlockSpec((None, T, QK_PAD), lambda h, i: (h, i, 0)),
                  pl.BlockSpec((None, S, QK_PAD), lambda h, i: (h, 0, 0)),
                  pl.BlockSpec((None, nT, VDIM, T), lambda h, i: (h, 0, 0, 0))],
        out_specs=(pl.BlockSpec((T, VDIM), lambda h, i: (i, h)),
                   pl.BlockSpec((None, T, LANE), lambda h, i: (h, i, 0))),
        scratch_shapes=[pltpu.VMEM((1, T), F32), pltpu.VMEM((1, T), F32), pltpu.VMEM((VDIM, T), F32),
                        pltpu.VMEM((T, T), F32)],
        compiler_params=_params("parallel", "arbitrary"),
    )(Q, K, VT)


def _adamw(w, g, m, v, name):
    shape = w.shape
    C = shape[-1]
    R = math.prod(shape[:-1])
    flat = [t.reshape(R, C) for t in (w, g, m, v)]
    tr = _row_tile(R, C * 4)

    def body(w_ref, g_ref, m_ref, v_ref, d_ref, nm_ref, nv_ref):
        gv = g_ref[...]
        m_new = ADAM_B1 * m_ref[...] + (1.0 - ADAM_B1) * gv
        v_new = ADAM_B2 * v_ref[...] + (1.0 - ADAM_B2) * jnp.square(gv)
        m_hat = m_new / (1.0 - ADAM_B1 ** ADAM_STEP)
        v_hat = v_new / (1.0 - ADAM_B2 ** ADAM_STEP)
        d_ref[...] = -ADAM_LR * (m_hat / (jnp.sqrt(v_hat) + ADAM_EPS) + ADAM_WD * w_ref[...])
        nm_ref[...] = m_new
        nv_ref[...] = v_new

    spec = pl.BlockSpec((tr, C), lambda i: (i, 0))
    out = jax.ShapeDtypeStruct((R, C), F32)
    d, nm, nv = pl.pallas_call(
        body, name=name, out_shape=(out, out, out), grid=(R // tr,),
        in_specs=[spec] * 4, out_specs=(spec, spec, spec), compiler_params=_params("parallel"),
    )(*flat)
    return d.reshape(shape), nm.reshape(shape), nv.reshape(shape)


def _sum_into_half(r, buf, layer, ci, n_layers, name):
    n, M, N = r.shape
    tr = _row_tile(M, N * 4 * n, 4 << 20)

    def body(c_ref, r_ref, *rest):
        o_ref = rest[-1]
        acc = r_ref[0].astype(F32)
        for s in range(1, n):
            acc = acc + r_ref[s].astype(F32)
        o_ref[...] = acc

    in_specs = [pl.BlockSpec((n, tr, N), lambda i, c: (0, i, 0))]
    operands = [ci.reshape(1), r]
    aliases = {}
    if buf is not None:
        in_specs.append(ANY)
        operands.append(buf)
        aliases = {2: 0}
    return pl.pallas_call(
        body, name=name, out_shape=jax.ShapeDtypeStruct((n_layers, 2, M, N), F32),
        grid_spec=pltpu.PrefetchScalarGridSpec(
            num_scalar_prefetch=1, grid=(M // tr,), in_specs=in_specs,
            out_specs=pl.BlockSpec((None, None, tr, N), lambda i, c: (layer, c[0], i, 0))),
        input_output_aliases=aliases, compiler_params=_params("parallel"),
    )(*operands)


def _sum_slots(r, name):
    n, M, N = r.shape
    tr = _row_tile(M, N * 4 * n, 4 << 20)

    def body(r_ref, o_ref):
        acc = r_ref[0].astype(F32)
        for s in range(1, n):
            acc = acc + r_ref[s].astype(F32)
        o_ref[...] = acc

    return pl.pallas_call(
        body, name=name, out_shape=jax.ShapeDtypeStruct((M, N), F32), grid=(M // tr,),
        in_specs=[pl.BlockSpec((n, tr, N), lambda i: (0, i, 0))],
        out_specs=pl.BlockSpec((tr, N), lambda i: (i, 0)), compiler_params=_params("parallel"),
    )(r)


ANY = pl.BlockSpec(memory_space=pl.ANY)
DMA_CHUNK_BYTES = 1 << 20
DMA_MAX_CHUNKS = 16
PEER_ORDER = (1, 4, 5, 2, 3, 6, 7)


def _position():
    return lax.axis_index("x"), lax.axis_index("y"), lax.axis_index("c")


def _row_chunks(shape, dtype):
    rows, cols = shape
    n = max(1, min(DMA_MAX_CHUNKS, rows * cols * jnp.dtype(dtype).itemsize // DMA_CHUNK_BYTES))
    while n > 1 and (rows % n or (rows // n) % 16):
        n -= 1
    step = rows // n
    return [pl.ds(q * step, step) for q in range(n)]


def _all_gather8(xs, name):
    n = len(xs)

    def body(*refs):
        x_refs, o_refs = refs[:n], refs[n:2 * n]
        send_sems, recv_sems, local_sems = refs[2 * n:]
        x, y, c = _position()
        me, sibling = (x, y, c), (x, y, 1 - c)
        chips = [(1 - x, y), (x, 1 - y), (1 - x, 1 - y)]

        def slot(a, dev, rows):
            return o_refs[a].at[4 * dev[0] + 2 * dev[1] + dev[2], rows]

        def copy(a, k, block, to, rows, from_input=False):
            return pltpu.make_async_remote_copy(
                src_ref=x_refs[a].at[rows] if from_input else slot(a, block, rows), dst_ref=slot(a, block, rows),
                send_sem=send_sems.at[a, k], recv_sem=recv_sems.at[a, k],
                device_id=to, device_id_type=MESH)

        def mine(a, rows):
            return pltpu.make_async_copy(x_refs[a].at[rows], slot(a, me, rows), local_sems.at[a])

        chunks = [_row_chunks(t.shape, t.dtype) for t in xs]
        whole = [pl.ds(0, t.shape[0]) for t in xs]
        for a in range(n):
            for rows in chunks[a]:
                mine(a, rows).start()
        sent = []
        for a in range(n):
            for k, to in enumerate([sibling] + [(*chip, c) for chip in chips]):
                for rows in chunks[a]:
                    copy(a, k, me, to, rows, from_input=True).start()
                sent.append(copy(a, k, me, to, whole[a], from_input=True))
        for a in range(n):
            for j, chip in enumerate(chips):
                copy(a, 1 + j, (*chip, c), me, whole[a]).wait_recv()
                for rows in chunks[a]:
                    copy(a, 4 + j, (*chip, c), sibling, rows).start()
                sent.append(copy(a, 4 + j, (*chip, c), sibling, whole[a]))
        for a in range(n):
            copy(a, 0, sibling, me, whole[a]).wait_recv()
            for j, chip in enumerate(chips):
                copy(a, 4 + j, (*chip, 1 - c), me, whole[a]).wait_recv()
        for cp in sent:
            cp.wait_send()
        for a in range(n):
            mine(a, whole[a]).wait()

    return pl.pallas_call(
        body, name=name,
        out_shape=[jax.ShapeDtypeStruct((8,) + t.shape, t.dtype) for t in xs],
        in_specs=[ANY] * n, out_specs=[ANY] * n,
        scratch_shapes=[pltpu.SemaphoreType.DMA((n, 7)), pltpu.SemaphoreType.DMA((n, 7)),
                        pltpu.SemaphoreType.DMA((n,))],
    )(*xs)


HBM = pl.BlockSpec(memory_space=pltpu.HBM)
SEM = pl.BlockSpec(memory_space=pltpu.SEMAPHORE)
EFFECT = pltpu.SideEffectType.DATAFLOW_SIDE_EFFECTING


def _peer(m, x, y, c):
    return ((1 - x) if m & 4 else x, (1 - y) if m & 2 else y, (1 - c) if m & 1 else c)


def _send_copies(src_refs, land_refs, send_sems, recv_sems, broadcast):
    x, y, c = _position()
    my = 4 * x + 2 * y + c
    out = []
    for a in range(len(src_refs)):
        for m in PEER_ORDER:
            px, py, pc = _peer(m, x, y, c)
            src = src_refs[a] if broadcast else src_refs[a].at[4 * px + 2 * py + pc]
            out.append(pltpu.make_async_remote_copy(
                src_ref=src, dst_ref=land_refs[a].at[my], send_sem=send_sems[a], recv_sem=recv_sems[a],
                device_id=(px, py, pc), device_id_type=MESH))
    return out


def _send_drain(land_refs, send_sems, recv_sems):
    x, y, c = _position()
    for a in range(len(land_refs)):
        seven = land_refs[a].at[pl.ds(0, 7)]
        both = pltpu.make_async_remote_copy(
            src_ref=seven, dst_ref=seven, send_sem=send_sems[a], recv_sem=recv_sems[a],
            device_id=(x, y, c), device_id_type=MESH)
        both.wait_send()
        both.wait_recv()


def _send_start(srcs, lands, after, broadcast, name):
    n = len(srcs)
    extra = [] if after is None else [after]

    def body(*refs):
        src_refs, land_refs = refs[:n], refs[n:2 * n]
        outs = refs[2 * n + len(extra):]
        send_sems, recv_sems = outs[:n], outs[n:2 * n]
        token = refs[-1]
        for cp in _send_copies(src_refs, land_refs, send_sems, recv_sems, broadcast):
            cp.start()
        token[...] = jnp.zeros_like(token)

    hbm = [pltpu.with_memory_space_constraint(t, pltpu.HBM) for t in list(srcs) + list(lands)]
    res = pl.pallas_call(
        body, name=name,
        out_shape=(*[pltpu.SemaphoreType.DMA(())] * (2 * n),
                   *[pltpu.HBM(t.shape, t.dtype) for t in hbm], jax.ShapeDtypeStruct((8, LANE), F32)),
        in_specs=[HBM] * (2 * n) + [ANY] * len(extra),
        out_specs=(*[SEM] * (2 * n), *[HBM] * (2 * n), pl.BlockSpec(memory_space=pltpu.VMEM)),
        input_output_aliases={i: 2 * n + i for i in range(2 * n)},
        compiler_params=pltpu.CompilerParams(has_side_effects=EFFECT),
    )(*hbm, *extra)
    return dict(sems=res[:2 * n], srcs=res[2 * n:3 * n], lands=res[3 * n:4 * n], token=res[-1], broadcast=broadcast)


def _send_wait(started, after, name):
    n = len(started["srcs"])

    def body(*refs):
        land_refs = refs[n:2 * n]
        send_sems, recv_sems = refs[2 * n:3 * n], refs[3 * n:4 * n]
        _send_drain(land_refs, send_sems, recv_sems)

    operands = list(started["srcs"]) + list(started["lands"])
    res = pl.pallas_call(
        body, name=name,
        out_shape=[pltpu.HBM(t.shape, t.dtype) for t in operands],
        in_specs=[HBM] * (2 * n) + [SEM] * (2 * n) + [ANY],
        out_specs=[HBM] * (2 * n),
        input_output_aliases={i: i for i in range(2 * n)},
        compiler_params=pltpu.CompilerParams(has_side_effects=EFFECT),
    )(*operands, *started["sems"], after)
    return res[n:]


def _own_slot(block, dev):
    zone = lax.empty((8,) + block.shape, block.dtype)
    return lax.dynamic_update_slice(zone, block[None], (dev, 0, 0))


def _pair_swap(bufs, name, layers=None):
    n = len(bufs)
    layers = layers or [range(t.shape[0]) for t in bufs]
    pieces = [(a, l) for a in range(len(bufs)) for l in layers[a]]

    def body(*refs):
        b_refs = refs[n:2 * n]
        send_sems, recv_sems = refs[2 * n:]
        x, y, c = _position()

        def copy(k, rows):
            a, l = pieces[k]
            half = b_refs[a].at[l, c, rows]
            return pltpu.make_async_remote_copy(
                src_ref=half, dst_ref=half, send_sem=send_sems.at[k], recv_sem=recv_sems.at[k],
                device_id=(x, y, 1 - c), device_id_type=MESH)

        chunks = [_row_chunks(bufs[a].shape[2:], bufs[a].dtype) for a, _ in pieces]
        whole = [pl.ds(0, bufs[a].shape[2]) for a, _ in pieces]
        for k in range(len(pieces)):
            for rows in chunks[k]:
                copy(k, rows).start()
        for k in range(len(pieces)):
            copy(k, whole[k]).wait_recv()
        for k in range(len(pieces)):
            copy(k, whole[k]).wait_send()

    return pl.pallas_call(
        body, name=name,
        out_shape=[jax.ShapeDtypeStruct(t.shape, t.dtype) for t in bufs],
        in_specs=[ANY] * n, out_specs=[ANY] * n,
        input_output_aliases={a: a for a in range(n)},
        scratch_shapes=[pltpu.SemaphoreType.DMA((len(pieces),)), pltpu.SemaphoreType.DMA((len(pieces),))],
    )(*bufs)


def _pack_rows(parts):
    flat = jnp.concatenate([t.reshape(-1).astype(F32) for t in parts])
    pad = (-flat.shape[0]) % (256 * LANE)
    return jnp.pad(flat, (0, pad)).reshape(-1, LANE)


def _my_half(w2d, ci):
    half = w2d.shape[0] // 2
    return lax.dynamic_slice_in_dim(w2d, ci * half, half, axis=0).astype(BF16)


def _col_view(g):
    _, half, Cs = g.shape
    return g.reshape(4, 2 * half, Cs)


def _row_view(g):
    _, half, C = g.shape
    return g.reshape(8 * half, C)


def _rope_tables(S):
    pos = jnp.arange(S, dtype=F32)
    inv_freq = ROPE_THETA ** (-jnp.arange(0, ROPE, 2, dtype=F32) / ROPE)
    ang = pos[:, None] * inv_freq[None, :]
    cos, sin = jnp.cos(ang), jnp.sin(ang)
    cos_t = jnp.concatenate([cos, cos, cos, cos], axis=1)
    sin_t = jnp.concatenate([-sin, sin, -sin, sin], axis=1)
    return cos_t, sin_t


def kernel(x, c, ada_w, ada_b, pre_g, post_g, sgu_w_in, sgu_norm_g, sgu_w_s, sgu_b_s, sgu_w_out, mla_w_in, mla_q_norm_g, mla_kv_norm_g, mla_w_uq, mla_w_ukv, mla_w_out, loss_target, m_ada_w, m_ada_b, m_pre_g, m_post_g, m_sgu_w_in, m_sgu_norm_g, m_sgu_w_s, m_sgu_b_s, m_sgu_w_out, m_mla_w_in, m_mla_q_norm_g, m_mla_kv_norm_g, m_mla_w_uq, m_mla_w_ukv, m_mla_w_out, v_ada_w, v_ada_b, v_pre_g, v_post_g, v_sgu_w_in, v_sgu_norm_g, v_sgu_w_s, v_sgu_b_s, v_sgu_w_out, v_mla_w_in, v_mla_q_norm_g, v_mla_kv_norm_g, v_mla_w_uq, v_mla_w_ukv, v_mla_w_out):
    S, D = x.shape[1], x.shape[2]
    depth = ada_w.shape[0]
    E = sgu_w_out.shape[1] * 4
    xi, yi, ci = _position()
    chip = 2 * xi + yi
    dev = 4 * xi + 2 * yi + ci
    x0 = x.reshape(S, D)
    target = loss_target.reshape(S, D)

    small = _pack_rows([c, mla_q_norm_g, mla_kv_norm_g])
    mixer_w = dict(sin=sgu_w_in, sout=sgu_w_out, min=mla_w_in, uq=mla_w_uq, ukv=mla_w_ukv, mout=mla_w_out)
    small_g, first_g = _all_gather8([small, _my_half(sgu_w_in[0], ci)], "gather_first")
    small_all = small_g.reshape(8, -1)
    gathered_w = {("sin", 0): first_g}
    qn_w, kvn_w = mla_q_norm_g.shape[1], mla_kv_norm_g.shape[1]
    c_all = small_all[:, :D]
    qn_all = small_all[0::2, D:D + 2 * qn_w].reshape(4, 2, qn_w)
    kvn_all = small_all[0::2, D + 2 * qn_w:D + 2 * qn_w + 2 * kvn_w].reshape(4, 2, kvn_w)
    q_gain = jnp.pad(jnp.transpose(qn_all, (1, 0, 2)).reshape(2, 1, Q_RANK), ((0, 0), (0, 0), (0, Q_RANK_PAD - Q_RANK)))
    kv_gain = jnp.transpose(kvn_all, (1, 0, 2)).reshape(2, 1, KV_RANK)

    views = {}

    def weight(t, j):
        if (t, j) not in views:
            g = gathered_w[(t, j)]
            v = _row_view(g) if t in ("sout", "mout") else _col_view(g)
            if t == "uq":
                v = jnp.pad(v, ((0, 0), (0, Q_RANK_PAD - Q_RANK), (0, 0)))
            views[(t, j)] = v
        return views[(t, j)]

    cols = ada_w.shape[2]
    ada_b_cols = lax.dynamic_slice_in_dim(ada_b, chip * cols, cols, axis=1)
    c_pad = jnp.pad(c_all, ((0, 8), (0, 0)))
    mod_cols = _ada_mod(c_pad, ada_w, ada_b_cols, "ada_mod")[:, :8]
    mod_g, = _all_gather8([mod_cols.reshape(depth * 8, cols)], "gather_mod")
    mod_all = jnp.transpose(mod_g[0::2].reshape(4, depth, 8, cols), (1, 2, 0, 3)).reshape(depth, 8, 4 * cols)

    groups = [("sout0", [("sout", 0)]), ("mla0", [(t, 0) for t in ("min", "uq", "ukv", "mout")]),
              ("sgu1", [("sin", 1), ("sout", 1)]), ("mla1", [(t, 1) for t in ("min", "uq", "ukv", "mout")])]
    sends = {}
    behind = mod_g
    for gname, items in groups:
        blocks = [_my_half(mixer_w[t][j], ci) for t, j in items]
        sends[gname] = _send_start(blocks, [_own_slot(b, dev) for b in blocks], behind, True, f"send_{gname}")
        behind = sends[gname]["token"]

    def arrive(gname, after):
        lands = _send_wait(sends[gname], after, f"arrive_{gname}")
        gathered_w.update(zip(dict(groups)[gname], lands))
    mod = lax.dynamic_index_in_dim(mod_all, dev, 1, keepdims=False)
    shift = [mod[i:i + 1, :D] for i in range(depth)]
    scale = [mod[i:i + 1, D:2 * D] for i in range(depth)]
    gate = [mod[i:i + 1, 2 * D:] for i in range(depth)]

    cos_t, sin_t = _rope_tables(S)
    b_bc = jnp.broadcast_to(sgu_b_s[:, :, :, None], sgu_b_s.shape + (LANE,))
    w_sT = jnp.swapaxes(sgu_w_s, 2, 3)

    saved = []
    xs = x0
    for i in range(depth):
        j = i // 2
        tag = f"l{i}"
        h = _pre_fwd(xs, pre_g[i:i + 1], scale[i], shift[i], f"pre_fwd_{tag}", after=(behind,) if i == 0 else ())
        if i % 2 == 0:
            if j > 0:
                arrive(f"sgu{j}", h)
            uvz = _mm(h, weight("sin", j), b_sharded=True, name=f"sgu_in_{tag}")
            y = _sgu_gate_fwd(uvz, sgu_norm_g[j:j + 1], sgu_w_s[j], b_bc[j], f"sgu_gate_fwd_{tag}")
            if j == 0:
                arrive("sout0", y)
            out = _mm(y, weight("sout", j), name=f"sgu_out_{tag}")
            saved.append(dict(x=xs, h=h, uvz=uvz, y=y, out=out))
        else:
            arrive(f"mla{j}", h)
            p = _mm(h, weight("min", j), b_sharded=True, name=f"mla_in_{tag}")
            cqn, ckvn = _mla_mid_fwd(p, q_gain[j], kv_gain[j], f"mla_mid_fwd_{tag}")
            q = _mm(cqn, weight("uq", j), b_sharded=True, name=f"mla_uq_{tag}")
            kv = _mm(ckvn, weight("ukv", j), b_sharded=True, name=f"mla_ukv_{tag}")
            Q, K, V, VT = _mla_pack(q, kv, p, cos_t, sin_t, f"mla_pack_{tag}")
            o, lse = _attn_fwd_t(Q, K, VT, f"attn_fwd_{tag}")
            y = _mla_gate_fwd(o, p, f"mla_gate_fwd_{tag}")
            out = _mm(y, weight("mout", j), name=f"mla_out_{tag}")
            saved.append(dict(x=xs, h=h, p=p, cqn=cqn, ckvn=ckvn, Q=Q, K=K, V=V, o=o, lse=lse, y=y, out=out))
        xs = _post_fwd(xs, out, gate[i], post_g[i:i + 1], f"post_fwd_{tag}")

    dx, loss_part = _loss_grad(xs, target, "loss")

    dmod = [None] * depth
    d_pre_g = [None] * depth
    d_post_g = [None] * depth
    d_sgu = [None] * 2
    d_mla = [None] * 2
    kinds = ("sgu_w_in", "sgu_w_out", "mla_w_in", "mla_w_uq", "mla_w_ukv", "mla_w_out")
    halves = dict.fromkeys(kinds)
    in_flight = []

    def send_grads(items, label):
        slices = [dw.reshape(8, -1, dw.shape[-1]) for _, _, dw in items]
        lands = [_own_slot(lax.dynamic_index_in_dim(s, dev, 0, keepdims=False), dev) for s in slices]
        started = _send_start(slices, lands, None, False, f"send_{label}")
        in_flight.append((started, [(kind, layer) for kind, layer, _ in items], label))
        return (started["token"],)

    def collect(count, after):
        for _ in range(count):
            started, keys, label = in_flight.pop(0)
            lands = _send_wait(started, after, f"arrive_{label}")
            for (kind, layer), r in zip(keys, lands):
                halves[kind] = _sum_into_half(r, halves[kind], layer, ci, 2, f"sum_{kind}_{layer}")

    for i in reversed(range(depth)):
        j = i // 2
        tag = f"l{i}"
        sv = saved[i]
        older = len(in_flight)
        dy, dgate, d_post_g[i] = _post_bwd(dx, sv["out"], gate[i], post_g[i:i + 1], f"post_bwd_{tag}")
        if i % 2 == 0:
            dw_out = _mm(sv["y"], dy, ta=True, out_dtype=BF16, name=f"sgu_out_dw_{tag}")
            sent = send_grads([("sgu_w_out", j, dw_out)], f"{tag}_out")
            dyv = _mm(dy, weight("sout", j), tb=True, after=sent, name=f"sgu_out_dx_{tag}")
            duvz, dng, dws, dbs = _sgu_gate_bwd(sv["uvz"], dyv, sgu_norm_g[j:j + 1], sgu_w_s[j], w_sT[j], b_bc[j],
                                                f"sgu_gate_bwd_{tag}")
            sent = ()
            if i > 0:
                dw_in = _mm(sv["h"], duvz, ta=True, out_sharded=True, out_dtype=BF16, tk=4096,
                            name=f"sgu_in_dw_{tag}")
                sent = send_grads([("sgu_w_in", j, dw_in)], f"{tag}_in")
            dh = _mm(duvz, weight("sin", j), tb=True, b_sharded=True, tk=3072, after=sent, name=f"sgu_in_dx_{tag}")
            d_sgu[j] = dict(norm_g=dng, w_s=dws, b_s=dbs[:, :, 0], duvz=duvz)
        else:
            dw_out = _mm(sv["y"], dy, ta=True, out_dtype=BF16, name=f"mla_out_dw_{tag}")
            sent = send_grads([("mla_w_out", j, dw_out)], f"{tag}_out")
            dyv = _mm(dy, weight("mout", j), tb=True, after=sent, name=f"mla_out_dx_{tag}")
            p = sv["p"]
            do = _mla_gate_bwd(dyv, p, f"mla_gate_bwd_{tag}")
            dQ, dK, dV = _attn_bwd(sv["Q"], sv["K"], sv["V"], sv["o"], do, sv["lse"], f"attn_bwd_{tag}")
            dq, dkv, dkr = _mla_unpack(dQ, dK, dV, cos_t, sin_t, f"mla_unpack_{tag}")
            dw_uq = _mm(sv["cqn"], dq, ta=True, out_sharded=True, out_dtype=BF16, name=f"mla_uq_dw_{tag}")
            dcqn = _mm(dq, weight("uq", j), tb=True, b_sharded=True, name=f"mla_uq_dx_{tag}")
            dw_ukv = _mm(sv["ckvn"], dkv, ta=True, out_sharded=True, out_dtype=BF16, name=f"mla_ukv_dw_{tag}")
            dckvn = _mm(dkv, weight("ukv", j), tb=True, b_sharded=True, name=f"mla_ukv_dx_{tag}")
            dp, dqg, dkvg = _mla_mid_bwd(p, dcqn, dckvn, dkr, dyv, sv["o"], q_gain[j], kv_gain[j], f"mla_mid_bwd_{tag}")
            dw_in = _mm(sv["h"], dp, ta=True, out_sharded=True, out_dtype=BF16, name=f"mla_in_dw_{tag}")
            sent = send_grads([("mla_w_in", j, dw_in), ("mla_w_uq", j, dw_uq[:, :Q_RANK]), ("mla_w_ukv", j, dw_ukv)],
                              f"{tag}_in")
            dh = _mm(dp, weight("min", j), tb=True, b_sharded=True, after=sent, name=f"mla_in_dx_{tag}")
            d_mla[j] = dict(qg=dqg[0, :Q_RANK], kvg=dkvg[0])
        dx, dshift, dscale, d_pre_g[i] = _pre_bwd(dh, sv["x"], dx, pre_g[i:i + 1], scale[i], f"pre_bwd_{tag}")
        dmod[i] = jnp.concatenate([dshift, dscale, dgate], axis=1)
        collect(older, dx)
    grad_x = dx.reshape(x.shape)

    parts = [jnp.concatenate(dmod, axis=0), jnp.concatenate(d_pre_g, axis=0), jnp.concatenate(d_post_g, axis=0),
             jnp.stack([d["norm_g"][0] for d in d_sgu]), jnp.stack([d["w_s"] for d in d_sgu]),
             jnp.stack([d["b_s"] for d in d_sgu]), jnp.stack([d["qg"] for d in d_mla]),
             jnp.stack([d["kvg"] for d in d_mla]), loss_part]
    sizes = [int(np.prod(t.shape)) for t in parts]
    packed = _pack_rows(parts)
    small = [packed, parts[0]]
    small_sent = _send_start(small, [_own_slot(t, dev) for t in small], None, True, "send_small_grads")
    dw_in0 = _mm(saved[0]["h"], d_sgu[0]["duvz"], ta=True, out_sharded=True, out_dtype=BF16, tk=4096,
                 after=(small_sent["token"],), name="sgu_in_dw_l0")
    sent = send_grads([("sgu_w_in", 0, dw_in0)], "l0_in")
    packed_all, dmod_all = _send_wait(small_sent, sent[0], "arrive_small_grads")
    total = _sum_slots(packed_all, "sum_small_grads").reshape(-1)
    offs = np.concatenate([[0], np.cumsum(sizes)])
    pieces = [total[int(offs[t]):int(offs[t + 1])].reshape(parts[t].shape) for t in range(len(parts))]
    g_ada_b, g_pre_g, g_post_g, g_norm_g, g_w_s, g_b_s, g_qg_full, g_kvg_full, loss_sum = pieces
    loss = loss_sum.reshape(())
    g_qg = lax.dynamic_slice_in_dim(g_qg_full, chip * qn_w, qn_w, axis=1)
    g_kvg = lax.dynamic_slice_in_dim(g_kvg_full, chip * kvn_w, kvn_w, axis=1)
    dmod_cols = jnp.stack([lax.dynamic_slice_in_dim(dmod_all[:, i], chip * cols, cols, axis=1) for i in range(depth)])
    dmod_cols = jnp.pad(dmod_cols, ((0, 0), (0, LANE - 8), (0, 0)))
    g_ada_w = _ada_grad(jnp.pad(c_all.T, ((0, 0), (0, LANE - 8))), dmod_cols, "ada_grad")

    wnames = ["ada_w", "ada_b", "pre_g", "post_g", "sgu_w_in", "sgu_norm_g", "sgu_w_s", "sgu_b_s", "sgu_w_out",
              "mla_w_in", "mla_q_norm_g", "mla_kv_norm_g", "mla_w_uq", "mla_w_ukv", "mla_w_out"]
    weights = dict(zip(wnames, [ada_w, ada_b, pre_g, post_g, sgu_w_in, sgu_norm_g, sgu_w_s, sgu_b_s, sgu_w_out,
                                mla_w_in, mla_q_norm_g, mla_kv_norm_g, mla_w_uq, mla_w_ukv, mla_w_out]))
    ms = dict(zip(wnames, [m_ada_w, m_ada_b, m_pre_g, m_post_g, m_sgu_w_in, m_sgu_norm_g, m_sgu_w_s, m_sgu_b_s,
                           m_sgu_w_out, m_mla_w_in, m_mla_q_norm_g, m_mla_kv_norm_g, m_mla_w_uq, m_mla_w_ukv,
                           m_mla_w_out]))
    vs = dict(zip(wnames, [v_ada_w, v_ada_b, v_pre_g, v_post_g, v_sgu_w_in, v_sgu_norm_g, v_sgu_w_s, v_sgu_b_s,
                           v_sgu_w_out, v_mla_w_in, v_mla_q_norm_g, v_mla_kv_norm_g, v_mla_w_uq, v_mla_w_ukv,
                           v_mla_w_out]))
    grads = dict(ada_w=g_ada_w, ada_b=g_ada_b, pre_g=g_pre_g, post_g=g_post_g, sgu_norm_g=g_norm_g, sgu_w_s=g_w_s,
                 sgu_b_s=g_b_s, mla_q_norm_g=g_qg, mla_kv_norm_g=g_kvg)
    stepped = {}

    def step(nm):
        grads[nm] = grads[nm].reshape(weights[nm].shape)
        stepped[nm] = _adamw(weights[nm], grads[nm], ms[nm], vs[nm], f"adamw_{nm}")

    for nm in wnames:
        if nm in grads:
            step(nm)
    early = [kind for kind in kinds if kind != "sgu_w_in"]
    collect(len(in_flight) - 1, stepped["ada_w"][0])
    swapped = _pair_swap([halves[kind] for kind in early] + [halves["sgu_w_in"]], "swap_grads",
                         layers=[range(2)] * len(early) + [[1]])
    halves["sgu_w_in"] = swapped[-1]
    for kind, g in zip(early, swapped):
        grads[kind] = g
        step(kind)
    collect(len(in_flight), stepped[early[-1]][0])
    grads["sgu_w_in"], = _pair_swap([halves["sgu_w_in"]], "swap_grads_last", layers=[[0]])
    step("sgu_w_in")
    return (loss, grad_x, *[grads[nm] for nm in wnames], *[stepped[nm][0] for nm in wnames],
            *[stepped[nm][1] for nm in wnames], *[stepped[nm][2] for nm in wnames])
```
